```python
import jax, jax.numpy as jnp
from jax import lax
import numpy as np

D_MODEL = 1024
BATCH = 8
SEQ = 8192
DEPTH = 1

D_MIX = D_MODEL
W_CONV = D_MIX // 2
W_LRU = D_MIX - W_CONV
N_CONV_GROUPS = 8
N_LRU_HEADS = 8
LRU_HEAD_DIM = W_LRU // N_LRU_HEADS
CONV_WIDTH = 31
LRU_CONV_WIDTH = 4
LRU_C = 8.0
D_FF = 2816
FFN_RES_SCALE = 0.5
RMS_EPS = 1e-6
LN_EPS = 1e-5

kernel_name = "macaron_conformer_conv_rglru_hybrid"


def rmsnorm(x, g):
    xf = x.astype(jnp.float32)
    y = xf * lax.rsqrt(jnp.mean(xf * xf, axis=-1, keepdims=True) + RMS_EPS)
    return (y * g.astype(jnp.float32)).astype(x.dtype)


def layernorm(x, g, b):
    xf = x.astype(jnp.float32)
    mu = jnp.mean(xf, axis=-1, keepdims=True)
    xc = xf - mu
    var = jnp.mean(xc * xc, axis=-1, keepdims=True)
    y = xc * lax.rsqrt(var + LN_EPS)
    return (y * g.astype(jnp.float32) + b.astype(jnp.float32)).astype(x.dtype)


def swiglu_ffn(x, w_gate, w_up, w_down):
    return (jax.nn.silu(x @ w_gate) * (x @ w_up)) @ w_down


def causal_depthwise_conv(x, w, b):
    k = w.shape[0]
    c = x.shape[-1]
    out = lax.conv_general_dilated(
        x, w[:, None, :].astype(x.dtype), window_strides=(1,), padding=[(k - 1, 0)],
        dimension_numbers=("NWC", "WIO", "NWC"), feature_group_count=c)
    return out + b.astype(x.dtype)


def block_diag_linear(x, w, b):
    bsz, s, _ = x.shape
    xh = x.reshape(bsz, s, N_LRU_HEADS, LRU_HEAD_DIM)
    y = jnp.einsum("bshi,hij->bshj", xh, w.astype(x.dtype))
    return y.reshape(bsz, s, W_LRU) + b.astype(x.dtype)


def rg_lru(x, w_a, b_a, w_x, b_x, lam):
    xf = x.astype(jnp.float32)
    r = jax.nn.sigmoid(block_diag_linear(x, w_a, b_a).astype(jnp.float32))
    i = jax.nn.sigmoid(block_diag_linear(x, w_x, b_x).astype(jnp.float32))
    log_a = -LRU_C * r * jax.nn.softplus(-lam.astype(jnp.float32))
    a = jnp.exp(log_a)
    mult = jnp.sqrt(-jnp.expm1(2.0 * log_a))
    bterm = mult * (i * xf)

    def combine(lhs, rhs):
        a1, b1 = lhs
        a2, b2 = rhs
        return a1 * a2, a2 * b1 + b2

    _, h = lax.associative_scan(combine, (a, bterm), axis=1)
    return h.astype(x.dtype)


def hybrid_mixer(h, w_in, conv_dw, conv_dw_bias, conv_ln_g, conv_ln_b,
                 lru_conv_w, lru_conv_b, lru_w_a, lru_b_a, lru_w_x, lru_b_x, lru_lambda, w_out):
    z = h @ w_in
    c_val, c_gate, r_x, r_gate = jnp.split(
        z, [W_CONV, 2 * W_CONV, 2 * W_CONV + W_LRU], axis=-1)
    u = c_val * jax.nn.sigmoid(c_gate)
    u = causal_depthwise_conv(u, conv_dw, conv_dw_bias)
    u = jax.nn.silu(layernorm(u, conv_ln_g, conv_ln_b))
    xr = causal_depthwise_conv(r_x, lru_conv_w, lru_conv_b)
    yr = rg_lru(xr, lru_w_a, lru_b_a, lru_w_x, lru_b_x, lru_lambda)
    yr = yr * jax.nn.gelu(r_gate, approximate=True)
    return jnp.concatenate([u, yr], axis=-1) @ w_out


def _fwd_setup_inputs(seed: int = 0) -> dict:
    key = jax.random.key(seed)
    ks = jax.random.split(key, 32)
    f32 = jnp.float32
    nrm = lambda k, shape, scale: jax.random.normal(k, shape, f32) * scale
    gain = lambda k, n: 1.0 + 0.02 * jax.random.normal(k, (n,), f32)
    d_in = 2 * W_CONV + 2 * W_LRU
    u = jax.random.uniform(ks[20], (W_LRU,), f32, 0.9, 0.999)
    a0 = u ** (1.0 / LRU_C)
    lru_lambda = jnp.log(a0) - jnp.log1p(-a0)
    return {
        "x": jax.random.normal(ks[0], (BATCH, SEQ, D_MODEL), f32),
        "ffn1_norm": gain(ks[1], D_MODEL),
        "ffn1_w_gate": nrm(ks[2], (D_MODEL, D_FF), D_MODEL ** -0.5),
        "ffn1_w_up": nrm(ks[3], (D_MODEL, D_FF), D_MODEL ** -0.5),
        "ffn1_w_down": nrm(ks[4], (D_FF, D_MODEL), D_FF ** -0.5),
        "mix_norm": gain(ks[5], D_MODEL),
        "w_in": nrm(ks[6], (D_MODEL, d_in), D_MODEL ** -0.5),
        "conv_dw": nrm(ks[7], (CONV_WIDTH, W_CONV), CONV_WIDTH ** -0.5),
        "conv_dw_bias": nrm(ks[8], (W_CONV,), 0.02),
        "conv_ln_g": gain(ks[9], W_CONV),
        "conv_ln_b": nrm(ks[10], (W_CONV,), 0.02),
        "lru_conv_w": nrm(ks[11], (LRU_CONV_WIDTH, W_LRU), LRU_CONV_WIDTH ** -0.5),
        "lru_conv_b": nrm(ks[12], (W_LRU,), 0.02),
        "lru_w_a": nrm(ks[13], (N_LRU_HEADS, LRU_HEAD_DIM, LRU_HEAD_DIM), LRU_HEAD_DIM ** -0.5),
        "lru_b_a": nrm(ks[14], (W_LRU,), 0.02),
        "lru_w_x": nrm(ks[15], (N_LRU_HEADS, LRU_HEAD_DIM, LRU_HEAD_DIM), LRU_HEAD_DIM ** -0.5),
        "lru_b_x": nrm(ks[16], (W_LRU,), 0.02),
        "lru_lambda": lru_lambda,
        "w_out": nrm(ks[17], (D_MIX, D_MODEL), D_MIX ** -0.5),
        "ffn2_norm": gain(ks[18], D_MODEL),
        "ffn2_w_gate": nrm(ks[19], (D_MODEL, D_FF), D_MODEL ** -0.5),
        "ffn2_w_up": nrm(ks[21], (D_MODEL, D_FF), D_MODEL ** -0.5),
        "ffn2_w_down": nrm(ks[22], (D_FF, D_MODEL), D_FF ** -0.5),
        "final_norm": gain(ks[23], D_MODEL),
    }


def _fwd_reference(x, ffn1_norm, ffn1_w_gate, ffn1_w_up, ffn1_w_down, mix_norm, w_in,
              conv_dw, conv_dw_bias, conv_ln_g, conv_ln_b, lru_conv_w, lru_conv_b,
              lru_w_a, lru_b_a, lru_w_x, lru_b_x, lru_lambda, w_out,
              ffn2_norm, ffn2_w_gate, ffn2_w_up, ffn2_w_down, final_norm):
    for _ in range(DEPTH):
        x = x + FFN_RES_SCALE * swiglu_ffn(rmsnorm(x, ffn1_norm), ffn1_w_gate, ffn1_w_up, ffn1_w_down)
        x = x + hybrid_mixer(rmsnorm(x, mix_norm), w_in, conv_dw, conv_dw_bias, conv_ln_g, conv_ln_b,
                             lru_conv_w, lru_conv_b, lru_w_a, lru_b_a, lru_w_x, lru_b_x,
                             lru_lambda, w_out)
        x = x + FFN_RES_SCALE * swiglu_ffn(rmsnorm(x, ffn2_norm), ffn2_w_gate, ffn2_w_up, ffn2_w_down)
    return rmsnorm(x, final_norm)


import jax as _jax
import jax.numpy as _jnp

TWIN_FORMAT = 'train_step'
FWD_PARAMS = ['x', 'ffn1_norm', 'ffn1_w_gate', 'ffn1_w_up', 'ffn1_w_down', 'mix_norm', 'w_in', 'conv_dw', 'conv_dw_bias', 'conv_ln_g', 'conv_ln_b', 'lru_conv_w', 'lru_conv_b', 'lru_w_a', 'lru_b_a', 'lru_w_x', 'lru_b_x', 'lru_lambda', 'w_out', 'ffn2_norm', 'ffn2_w_gate', 'ffn2_w_up', 'ffn2_w_down', 'final_norm']
TWIN_WEIGHTS = ['ffn1_norm', 'ffn1_w_gate', 'ffn1_w_up', 'ffn1_w_down', 'mix_norm', 'w_in', 'conv_dw', 'conv_dw_bias', 'conv_ln_g', 'conv_ln_b', 'lru_conv_w', 'lru_conv_b', 'lru_w_a', 'lru_b_a', 'lru_w_x', 'lru_b_x', 'lru_lambda', 'w_out', 'ffn2_norm', 'ffn2_w_gate', 'ffn2_w_up', 'ffn2_w_down', 'final_norm']
TWIN_DIFF_INPUT = 'x'
TWIN_INPUTS = ['x', 'ffn1_norm', 'ffn1_w_gate', 'ffn1_w_up', 'ffn1_w_down', 'mix_norm', 'w_in', 'conv_dw', 'conv_dw_bias', 'conv_ln_g', 'conv_ln_b', 'lru_conv_w', 'lru_conv_b', 'lru_w_a', 'lru_b_a', 'lru_w_x', 'lru_b_x', 'lru_lambda', 'w_out', 'ffn2_norm', 'ffn2_w_gate', 'ffn2_w_up', 'ffn2_w_down', 'final_norm', 'loss_target', 'm_ffn1_norm', 'm_ffn1_w_gate', 'm_ffn1_w_up', 'm_ffn1_w_down', 'm_mix_norm', 'm_w_in', 'm_conv_dw', 'm_conv_dw_bias', 'm_conv_ln_g', 'm_conv_ln_b', 'm_lru_conv_w', 'm_lru_conv_b', 'm_lru_w_a', 'm_lru_b_a', 'm_lru_w_x', 'm_lru_b_x', 'm_lru_lambda', 'm_w_out', 'm_ffn2_norm', 'm_ffn2_w_gate', 'm_ffn2_w_up', 'm_ffn2_w_down', 'm_final_norm', 'v_ffn1_norm', 'v_ffn1_w_gate', 'v_ffn1_w_up', 'v_ffn1_w_down', 'v_mix_norm', 'v_w_in', 'v_conv_dw', 'v_conv_dw_bias', 'v_conv_ln_g', 'v_conv_ln_b', 'v_lru_conv_w', 'v_lru_conv_b', 'v_lru_w_a', 'v_lru_b_a', 'v_lru_w_x', 'v_lru_b_x', 'v_lru_lambda', 'v_w_out', 'v_ffn2_norm', 'v_ffn2_w_gate', 'v_ffn2_w_up', 'v_ffn2_w_down', 'v_final_norm']
TWIN_OUTPUTS = ['loss', 'grad_x', 'grad_ffn1_norm', 'grad_ffn1_w_gate', 'grad_ffn1_w_up', 'grad_ffn1_w_down', 'grad_mix_norm', 'grad_w_in', 'grad_conv_dw', 'grad_conv_dw_bias', 'grad_conv_ln_g', 'grad_conv_ln_b', 'grad_lru_conv_w', 'grad_lru_conv_b', 'grad_lru_w_a', 'grad_lru_b_a', 'grad_lru_w_x', 'grad_lru_b_x', 'grad_lru_lambda', 'grad_w_out', 'grad_ffn2_norm', 'grad_ffn2_w_gate', 'grad_ffn2_w_up', 'grad_ffn2_w_down', 'grad_final_norm', 'delta_ffn1_norm', 'delta_ffn1_w_gate', 'delta_ffn1_w_up', 'delta_ffn1_w_down', 'delta_mix_norm', 'delta_w_in', 'delta_conv_dw', 'delta_conv_dw_bias', 'delta_conv_ln_g', 'delta_conv_ln_b', 'delta_lru_conv_w', 'delta_lru_conv_b', 'delta_lru_w_a', 'delta_lru_b_a', 'delta_lru_w_x', 'delta_lru_b_x', 'delta_lru_lambda', 'delta_w_out', 'delta_ffn2_norm', 'delta_ffn2_w_gate', 'delta_ffn2_w_up', 'delta_ffn2_w_down', 'delta_final_norm', 'new_m_ffn1_norm', 'new_m_ffn1_w_gate', 'new_m_ffn1_w_up', 'new_m_ffn1_w_down', 'new_m_mix_norm', 'new_m_w_in', 'new_m_conv_dw', 'new_m_conv_dw_bias', 'new_m_conv_ln_g', 'new_m_conv_ln_b', 'new_m_lru_conv_w', 'new_m_lru_conv_b', 'new_m_lru_w_a', 'new_m_lru_b_a', 'new_m_lru_w_x', 'new_m_lru_b_x', 'new_m_lru_lambda', 'new_m_w_out', 'new_m_ffn2_norm', 'new_m_ffn2_w_gate', 'new_m_ffn2_w_up', 'new_m_ffn2_w_down', 'new_m_final_norm', 'new_v_ffn1_norm', 'new_v_ffn1_w_gate', 'new_v_ffn1_w_up', 'new_v_ffn1_w_down', 'new_v_mix_norm', 'new_v_w_in', 'new_v_conv_dw', 'new_v_conv_dw_bias', 'new_v_conv_ln_g', 'new_v_conv_ln_b', 'new_v_lru_conv_w', 'new_v_lru_conv_b', 'new_v_lru_w_a', 'new_v_lru_b_a', 'new_v_lru_w_x', 'new_v_lru_b_x', 'new_v_lru_lambda', 'new_v_w_out', 'new_v_ffn2_norm', 'new_v_ffn2_w_gate', 'new_v_ffn2_w_up', 'new_v_ffn2_w_down', 'new_v_final_norm']
TWIN_LEAF_KINDS = {'loss': 'loss', 'grad_x': 'grad_x', 'grad_ffn1_norm': 'grad_w', 'grad_ffn1_w_gate': 'grad_w', 'grad_ffn1_w_up': 'grad_w', 'grad_ffn1_w_down': 'grad_w', 'grad_mix_norm': 'grad_w', 'grad_w_in': 'grad_w', 'grad_conv_dw': 'grad_w', 'grad_conv_dw_bias': 'grad_w', 'grad_conv_ln_g': 'grad_w', 'grad_conv_ln_b': 'grad_w', 'grad_lru_conv_w': 'grad_w', 'grad_lru_conv_b': 'grad_w', 'grad_lru_w_a': 'grad_w', 'grad_lru_b_a': 'grad_w', 'grad_lru_w_x': 'grad_w', 'grad_lru_b_x': 'grad_w', 'grad_lru_lambda': 'grad_w', 'grad_w_out': 'grad_w', 'grad_ffn2_norm': 'grad_w', 'grad_ffn2_w_gate': 'grad_w', 'grad_ffn2_w_up': 'grad_w', 'grad_ffn2_w_down': 'grad_w', 'grad_final_norm': 'grad_w', 'delta_ffn1_norm': 'delta_w', 'delta_ffn1_w_gate': 'delta_w', 'delta_ffn1_w_up': 'delta_w', 'delta_ffn1_w_down': 'delta_w', 'delta_mix_norm': 'delta_w', 'delta_w_in': 'delta_w', 'delta_conv_dw': 'delta_w', 'delta_conv_dw_bias': 'delta_w', 'delta_conv_ln_g': 'delta_w', 'delta_conv_ln_b': 'delta_w', 'delta_lru_conv_w': 'delta_w', 'delta_lru_conv_b': 'delta_w', 'delta_lru_w_a': 'delta_w', 'delta_lru_b_a': 'delta_w', 'delta_lru_w_x': 'delta_w', 'delta_lru_b_x': 'delta_w', 'delta_lru_lambda': 'delta_w', 'delta_w_out': 'delta_w', 'delta_ffn2_norm': 'delta_w', 'delta_ffn2_w_gate': 'delta_w', 'delta_ffn2_w_up': 'delta_w', 'delta_ffn2_w_down': 'delta_w', 'delta_final_norm': 'delta_w', 'new_m_ffn1_norm': 'new_m', 'new_m_ffn1_w_gate': 'new_m', 'new_m_ffn1_w_up': 'new_m', 'new_m_ffn1_w_down': 'new_m', 'new_m_mix_norm': 'new_m', 'new_m_w_in': 'new_m', 'new_m_conv_dw': 'new_m', 'new_m_conv_dw_bias': 'new_m', 'new_m_conv_ln_g': 'new_m', 'new_m_conv_ln_b': 'new_m', 'new_m_lru_conv_w': 'new_m', 'new_m_lru_conv_b': 'new_m', 'new_m_lru_w_a': 'new_m', 'new_m_lru_b_a': 'new_m', 'new_m_lru_w_x': 'new_m', 'new_m_lru_b_x': 'new_m', 'new_m_lru_lambda': 'new_m', 'new_m_w_out': 'new_m', 'new_m_ffn2_norm': 'new_m', 'new_m_ffn2_w_gate': 'new_m', 'new_m_ffn2_w_up': 'new_m', 'new_m_ffn2_w_down': 'new_m', 'new_m_final_norm': 'new_m', 'new_v_ffn1_norm': 'new_v', 'new_v_ffn1_w_gate': 'new_v', 'new_v_ffn1_w_up': 'new_v', 'new_v_ffn1_w_down': 'new_v', 'new_v_mix_norm': 'new_v', 'new_v_w_in': 'new_v', 'new_v_conv_dw': 'new_v', 'new_v_conv_dw_bias': 'new_v', 'new_v_conv_ln_g': 'new_v', 'new_v_conv_ln_b': 'new_v', 'new_v_lru_conv_w': 'new_v', 'new_v_lru_conv_b': 'new_v', 'new_v_lru_w_a': 'new_v', 'new_v_lru_b_a': 'new_v', 'new_v_lru_w_x': 'new_v', 'new_v_lru_b_x': 'new_v', 'new_v_lru_lambda': 'new_v', 'new_v_w_out': 'new_v', 'new_v_ffn2_norm': 'new_v', 'new_v_ffn2_w_gate': 'new_v', 'new_v_ffn2_w_up': 'new_v', 'new_v_ffn2_w_down': 'new_v', 'new_v_final_norm': 'new_v'}


def _forward(args):
    return _fwd_reference(*[args[k] for k in FWD_PARAMS])


def _output_shape():
    out = _jax.eval_shape(lambda: _forward(_fwd_setup_inputs(0)))
    return out.shape, out.dtype

N_MICROBATCH = 1
ADAM_LR = 0.001
ADAM_B1 = 0.9
ADAM_B2 = 0.999
ADAM_EPS = 1e-08
ADAM_WD = 0.01
ADAM_STEP = 10
PER_EXAMPLE_BATCH_AXIS = {'x': 0, 'loss_target': 0}
SHARED_INPUTS = []
_WEIGHT_DTYPES = {'ffn1_norm': _jnp.float32, 'ffn1_w_gate': _jnp.float32, 'ffn1_w_up': _jnp.float32, 'ffn1_w_down': _jnp.float32, 'mix_norm': _jnp.float32, 'w_in': _jnp.float32, 'conv_dw': _jnp.float32, 'conv_dw_bias': _jnp.float32, 'conv_ln_g': _jnp.float32, 'conv_ln_b': _jnp.float32, 'lru_conv_w': _jnp.float32, 'lru_conv_b': _jnp.float32, 'lru_w_a': _jnp.float32, 'lru_b_a': _jnp.float32, 'lru_w_x': _jnp.float32, 'lru_b_x': _jnp.float32, 'lru_lambda': _jnp.float32, 'w_out': _jnp.float32, 'ffn2_norm': _jnp.float32, 'ffn2_w_gate': _jnp.float32, 'ffn2_w_up': _jnp.float32, 'ffn2_w_down': _jnp.float32, 'final_norm': _jnp.float32}
MOMENT_SCALE = {'ffn1_norm': 1.247071e-01, 'ffn1_w_gate': 4.813794e-02, 'ffn1_w_up': 4.649311e-02, 'ffn1_w_down': 7.693321e-02, 'mix_norm': 1.613035e-01, 'w_in': 1.013854e-01, 'conv_dw': 1.391150e-01, 'conv_dw_bias': 2.979136e-01, 'conv_ln_g': 1.973687e-01, 'conv_ln_b': 2.001870e-01, 'lru_conv_w': 1.217728e-01, 'lru_conv_b': 1.528440e+00, 'lru_w_a': 4.320167e-02, 'lru_b_a': 2.354335e-02, 'lru_w_x': 7.677005e-02, 'lru_b_x': 3.705931e-02, 'lru_lambda': 4.241841e-02, 'w_out': 1.211505e-01, 'ffn2_norm': 9.576240e-02, 'ffn2_w_gate': 3.889965e-02, 'ffn2_w_up': 3.778531e-02, 'ffn2_w_down': 6.255123e-02, 'final_norm': 6.386987e+01}


def _to_microbatches(a, axis):
    t = _jnp.moveaxis(a, axis, 0)
    t = t.reshape((N_MICROBATCH, t.shape[0] // N_MICROBATCH) + t.shape[1:])
    return _jnp.moveaxis(t, 1, axis + 1)


def setup_inputs(seed: int = 0) -> dict:
    inp = _fwd_setup_inputs(seed)
    key = _jax.random.fold_in(_jax.random.key(seed), 7919)
    shape, _ = _output_shape()
    out = dict(inp)
    out["loss_target"] = _jax.random.normal(_jax.random.fold_in(key, 0), shape, _jnp.float32)
    for i, name in enumerate(TWIN_WEIGHTS):
        w = inp[name].astype(_jnp.float32)
        if MOMENT_SCALE is None:
            s = _jnp.sqrt(_jnp.mean(_jnp.square(w)) + 1e-30)
        else:
            s = MOMENT_SCALE[name]
        km, kv = _jax.random.split(_jax.random.fold_in(key, i + 1))
        out[name] = w
        out["m_" + name] = s * _jax.random.normal(km, w.shape, _jnp.float32)
        out["v_" + name] = (s * s) * _jax.random.uniform(kv, w.shape, _jnp.float32, 0.5, 1.5)
    if N_MICROBATCH > 1:
        for name, axis in PER_EXAMPLE_BATCH_AXIS.items():
            out[name] = _to_microbatches(out[name], axis)
    return {'x': out['x'], 'ffn1_norm': out['ffn1_norm'], 'ffn1_w_gate': out['ffn1_w_gate'], 'ffn1_w_up': out['ffn1_w_up'], 'ffn1_w_down': out['ffn1_w_down'], 'mix_norm': out['mix_norm'], 'w_in': out['w_in'], 'conv_dw': out['conv_dw'], 'conv_dw_bias': out['conv_dw_bias'], 'conv_ln_g': out['conv_ln_g'], 'conv_ln_b': out['conv_ln_b'], 'lru_conv_w': out['lru_conv_w'], 'lru_conv_b': out['lru_conv_b'], 'lru_w_a': out['lru_w_a'], 'lru_b_a': out['lru_b_a'], 'lru_w_x': out['lru_w_x'], 'lru_b_x': out['lru_b_x'], 'lru_lambda': out['lru_lambda'], 'w_out': out['w_out'], 'ffn2_norm': out['ffn2_norm'], 'ffn2_w_gate': out['ffn2_w_gate'], 'ffn2_w_up': out['ffn2_w_up'], 'ffn2_w_down': out['ffn2_w_down'], 'final_norm': out['final_norm'], 'loss_target': out['loss_target'], 'm_ffn1_norm': out['m_ffn1_norm'], 'm_ffn1_w_gate': out['m_ffn1_w_gate'], 'm_ffn1_w_up': out['m_ffn1_w_up'], 'm_ffn1_w_down': out['m_ffn1_w_down'], 'm_mix_norm': out['m_mix_norm'], 'm_w_in': out['m_w_in'], 'm_conv_dw': out['m_conv_dw'], 'm_conv_dw_bias': out['m_conv_dw_bias'], 'm_conv_ln_g': out['m_conv_ln_g'], 'm_conv_ln_b': out['m_conv_ln_b'], 'm_lru_conv_w': out['m_lru_conv_w'], 'm_lru_conv_b': out['m_lru_conv_b'], 'm_lru_w_a': out['m_lru_w_a'], 'm_lru_b_a': out['m_lru_b_a'], 'm_lru_w_x': out['m_lru_w_x'], 'm_lru_b_x': out['m_lru_b_x'], 'm_lru_lambda': out['m_lru_lambda'], 'm_w_out': out['m_w_out'], 'm_ffn2_norm': out['m_ffn2_norm'], 'm_ffn2_w_gate': out['m_ffn2_w_gate'], 'm_ffn2_w_up': out['m_ffn2_w_up'], 'm_ffn2_w_down': out['m_ffn2_w_down'], 'm_final_norm': out['m_final_norm'], 'v_ffn1_norm': out['v_ffn1_norm'], 'v_ffn1_w_gate': out['v_ffn1_w_gate'], 'v_ffn1_w_up': out['v_ffn1_w_up'], 'v_ffn1_w_down': out['v_ffn1_w_down'], 'v_mix_norm': out['v_mix_norm'], 'v_w_in': out['v_w_in'], 'v_conv_dw': out['v_conv_dw'], 'v_conv_dw_bias': out['v_conv_dw_bias'], 'v_conv_ln_g': out['v_conv_ln_g'], 'v_conv_ln_b': out['v_conv_ln_b'], 'v_lru_conv_w': out['v_lru_conv_w'], 'v_lru_conv_b': out['v_lru_conv_b'], 'v_lru_w_a': out['v_lru_w_a'], 'v_lru_b_a': out['v_lru_b_a'], 'v_lru_w_x': out['v_lru_w_x'], 'v_lru_b_x': out['v_lru_b_x'], 'v_lru_lambda': out['v_lru_lambda'], 'v_w_out': out['v_w_out'], 'v_ffn2_norm': out['v_ffn2_norm'], 'v_ffn2_w_gate': out['v_ffn2_w_gate'], 'v_ffn2_w_up': out['v_ffn2_w_up'], 'v_ffn2_w_down': out['v_ffn2_w_down'], 'v_final_norm': out['v_final_norm']}


def _loss(weights, diff, rest, loss_target):
    with _jax.named_scope("forward"):
        args = {**rest, TWIN_DIFF_INPUT: diff, **{k: w.astype(_WEIGHT_DTYPES[k]) for k, w in weights.items()}}
        y = _forward(args)
    with _jax.named_scope("loss_head"):
        err = _jnp.square(y.astype(_jnp.float32) - loss_target)
        return 0.5 * _jnp.sum(_jnp.mean(err, axis=-1)) if err.ndim else 0.5 * err


def _adamw(w, g, m, v):
    m = ADAM_B1 * m + (1.0 - ADAM_B1) * g
    v = ADAM_B2 * v + (1.0 - ADAM_B2) * _jnp.square(g)
    m_hat = m / (1.0 - ADAM_B1 ** ADAM_STEP)
    v_hat = v / (1.0 - ADAM_B2 ** ADAM_STEP)
    delta = -ADAM_LR * (m_hat / (_jnp.sqrt(v_hat) + ADAM_EPS) + ADAM_WD * w)
    return delta, m, v


def reference(x, ffn1_norm, ffn1_w_gate, ffn1_w_up, ffn1_w_down, mix_norm, w_in, conv_dw, conv_dw_bias, conv_ln_g, conv_ln_b, lru_conv_w, lru_conv_b, lru_w_a, lru_b_a, lru_w_x, lru_b_x, lru_lambda, w_out, ffn2_norm, ffn2_w_gate, ffn2_w_up, ffn2_w_down, final_norm, loss_target, m_ffn1_norm, m_ffn1_w_gate, m_ffn1_w_up, m_ffn1_w_down, m_mix_norm, m_w_in, m_conv_dw, m_conv_dw_bias, m_conv_ln_g, m_conv_ln_b, m_lru_conv_w, m_lru_conv_b, m_lru_w_a, m_lru_b_a, m_lru_w_x, m_lru_b_x, m_lru_lambda, m_w_out, m_ffn2_norm, m_ffn2_w_gate, m_ffn2_w_up, m_ffn2_w_down, m_final_norm, v_ffn1_norm, v_ffn1_w_gate, v_ffn1_w_up, v_ffn1_w_down, v_mix_norm, v_w_in, v_conv_dw, v_conv_dw_bias, v_conv_ln_g, v_conv_ln_b, v_lru_conv_w, v_lru_conv_b, v_lru_w_a, v_lru_b_a, v_lru_w_x, v_lru_b_x, v_lru_lambda, v_w_out, v_ffn2_norm, v_ffn2_w_gate, v_ffn2_w_up, v_ffn2_w_down, v_final_norm):
    given = dict(x=x, ffn1_norm=ffn1_norm, ffn1_w_gate=ffn1_w_gate, ffn1_w_up=ffn1_w_up, ffn1_w_down=ffn1_w_down, mix_norm=mix_norm, w_in=w_in, conv_dw=conv_dw, conv_dw_bias=conv_dw_bias, conv_ln_g=conv_ln_g, conv_ln_b=conv_ln_b, lru_conv_w=lru_conv_w, lru_conv_b=lru_conv_b, lru_w_a=lru_w_a, lru_b_a=lru_b_a, lru_w_x=lru_w_x, lru_b_x=lru_b_x, lru_lambda=lru_lambda, w_out=w_out, ffn2_norm=ffn2_norm, ffn2_w_gate=ffn2_w_gate, ffn2_w_up=ffn2_w_up, ffn2_w_down=ffn2_w_down, final_norm=final_norm, loss_target=loss_target, m_ffn1_norm=m_ffn1_norm, m_ffn1_w_gate=m_ffn1_w_gate, m_ffn1_w_up=m_ffn1_w_up, m_ffn1_w_down=m_ffn1_w_down, m_mix_norm=m_mix_norm, m_w_in=m_w_in, m_conv_dw=m_conv_dw, m_conv_dw_bias=m_conv_dw_bias, m_conv_ln_g=m_conv_ln_g, m_conv_ln_b=m_conv_ln_b, m_lru_conv_w=m_lru_conv_w, m_lru_conv_b=m_lru_conv_b, m_lru_w_a=m_lru_w_a, m_lru_b_a=m_lru_b_a, m_lru_w_x=m_lru_w_x, m_lru_b_x=m_lru_b_x, m_lru_lambda=m_lru_lambda, m_w_out=m_w_out, m_ffn2_norm=m_ffn2_norm, m_ffn2_w_gate=m_ffn2_w_gate, m_ffn2_w_up=m_ffn2_w_up, m_ffn2_w_down=m_ffn2_w_down, m_final_norm=m_final_norm, v_ffn1_norm=v_ffn1_norm, v_ffn1_w_gate=v_ffn1_w_gate, v_ffn1_w_up=v_ffn1_w_up, v_ffn1_w_down=v_ffn1_w_down, v_mix_norm=v_mix_norm, v_w_in=v_w_in, v_conv_dw=v_conv_dw, v_conv_dw_bias=v_conv_dw_bias, v_conv_ln_g=v_conv_ln_g, v_conv_ln_b=v_conv_ln_b, v_lru_conv_w=v_lru_conv_w, v_lru_conv_b=v_lru_conv_b, v_lru_w_a=v_lru_w_a, v_lru_b_a=v_lru_b_a, v_lru_w_x=v_lru_w_x, v_lru_b_x=v_lru_b_x, v_lru_lambda=v_lru_lambda, v_w_out=v_w_out, v_ffn2_norm=v_ffn2_norm, v_ffn2_w_gate=v_ffn2_w_gate, v_ffn2_w_up=v_ffn2_w_up, v_ffn2_w_down=v_ffn2_w_down, v_final_norm=v_final_norm)
    weights = {n: given[n] for n in TWIN_WEIGHTS}
    shared = {n: given[n] for n in SHARED_INPUTS}
    per_example = {n: given[n] for n in ['x']}
    grad_fn = _jax.value_and_grad(_loss, argnums=(0, 1))

    def one_microbatch(ex, loss_target):
        ex = dict(ex)
        diff = ex.pop(TWIN_DIFF_INPUT)
        return grad_fn(weights, diff, {**shared, **ex}, loss_target)

    if N_MICROBATCH == 1:
        loss, (grad_w, grad_x) = one_microbatch(per_example, given["loss_target"])
    else:
        def body(carry, xs):
            loss_sum, grad_sum = carry
            l_k, (gw_k, gx_k) = one_microbatch(xs[0], xs[1])
            with _jax.named_scope("update"):
                return (loss_sum + l_k, _jax.tree.map(_jnp.add, grad_sum, gw_k)), gx_k

        init = (_jnp.zeros((), _jnp.float32), _jax.tree.map(_jnp.zeros_like, weights))
        (loss, grad_w), grad_x = _jax.lax.scan(body, init, (per_example, given["loss_target"]))
    with _jax.named_scope("update"):
        delta_w, new_m, new_v = {}, {}, {}
        for n in TWIN_WEIGHTS:
            delta_w[n], new_m[n], new_v[n] = _adamw(weights[n], grad_w[n], given["m_" + n], given["v_" + n])
    return (loss, grad_x, *[grad_w[n] for n in TWIN_WEIGHTS], *[delta_w[n] for n in TWIN_WEIGHTS],
            *[new_m[n] for n in TWIN_WEIGHTS], *[new_v[n] for n in TWIN_WEIGHTS])
```

```python
import functools
import math

import jax
import jax.numpy as jnp
from jax import lax
from jax.experimental import pallas as pl
from jax.experimental.pallas import tpu as pltpu

F32 = jnp.float32
BF16 = jnp.bfloat16
MESH = pl.DeviceIdType.MESH

N_DEV = 8
N_CHIP = 4
RMS_EPS = 1e-6
LN_EPS = 1e-5
LRU_C = 8.0
CONV_K = 31
LRU_K = 4
CONV_HALO = 32
LRU_HALO = 8
FFN_RES = 0.5
ADAM_LR, ADAM_B1, ADAM_B2, ADAM_EPS, ADAM_WD, ADAM_STEP = 0.001, 0.9, 0.999, 1e-08, 0.01, 10
GELU_K = math.sqrt(2.0 / math.pi)
GELU_C = 0.044715

MIB = 1024 * 1024
NT = (((1,), (1,)), ((), ()))
NN = (((1,), (0,)), ((), ()))
TN = (((0,), (0,)), ((), ()))

V_CB, V_LNG, V_LNB, V_LCB, V_BA, V_BX, V_LAM = range(7)
G_CW = 0
G_CB, G_LNG, G_LNB = 31, 32, 33
G_LW = 34
G_LCB, G_BA, G_BX, G_LAM = 38, 39, 40, 41
G_ROWS = 48


def _dot(a, b, dims):
    return lax.dot_general(a, b, dims, preferred_element_type=F32)


def _params(vmem_mib, n_axes=0, **kw):
    sem = ("arbitrary",) * n_axes if n_axes else None
    return pltpu.CompilerParams(dimension_semantics=sem, vmem_limit_bytes=vmem_mib * MIB, **kw)


def _sigmoid(x):
    return 1.0 / (1.0 + jnp.exp(-x))


def _gelu_parts(x):
    x2 = x * x
    th = jnp.tanh(GELU_K * x * (1.0 + GELU_C * x2))
    gl = 0.5 * x * (1.0 + th)
    dgl = 0.5 * (1.0 + th) + 0.5 * x * (1.0 - th * th) * GELU_K * (1.0 + 3.0 * GELU_C * x2)
    return gl, dgl


def _neg_expm1(y):
    series = -y * (1.0 + y * (1.0 / 2) * (1.0 + y * (1.0 / 3) * (1.0 + y * (1.0 / 4) * (1.0 + y * (1.0 / 5) * (1.0 + y * (1.0 / 6))))))
    return jnp.where(y > -0.25, series, 1.0 - jnp.exp(y))


def _softplus_neg(lam):
    t = -lam
    e = jnp.exp(-jnp.abs(t))
    s = 1.0 + e
    log1p_e = jnp.log(s) - ((s - 1.0) - e) / s
    return jnp.maximum(t, 0.0) + log1p_e


def _rms_stats(xv):
    rstd = lax.rsqrt(jnp.mean(xv * xv, axis=-1, keepdims=True) + RMS_EPS)
    return xv * rstd, rstd


def _rms_bwd(xhat, rstd, g, dh):
    dxhat = dh * g
    dx = rstd * (dxhat - xhat * jnp.mean(dxhat * xhat, axis=-1, keepdims=True))
    return dx, jnp.sum(dh * xhat, axis=0, keepdims=True)


def _ffn_fwd(x, g, w3, widx, tm, tf, name):
    T, D = x.shape
    F = w3.shape[1]

    def body(x_ref, g_ref, wg_ref, wu_ref, wd_ref, xo_ref, h_ref, gate_ref, up_ref, hs_ref, acc_ref):
        f = pl.program_id(1)

        @pl.when(f == 0)
        def _():
            xhat, _ = _rms_stats(x_ref[...])
            hb = (xhat * g_ref[...]).astype(BF16)
            hs_ref[...] = hb
            h_ref[...] = hb
            acc_ref[...] = jnp.zeros_like(acc_ref)

        hb = hs_ref[...]
        gate = _dot(hb, wg_ref[...], NT)
        up = _dot(hb, wu_ref[...], NT)
        gate_ref[...] = gate.astype(BF16)
        up_ref[...] = up.astype(BF16)
        act = (gate * _sigmoid(gate) * up).astype(BF16)
        acc_ref[...] += _dot(act, wd_ref[...], NN)

        @pl.when(f == pl.num_programs(1) - 1)
        def _():
            xo_ref[...] = x_ref[...] + FFN_RES * acc_ref[...]

    wspec = lambda k: pl.BlockSpec((None, tf, D), lambda i, f: (widx + k, f, 0))
    return pl.pallas_call(
        body, name=name,
        grid=(T // tm, F // tf),
        in_specs=[pl.BlockSpec((tm, D), lambda i, f: (i, 0)), pl.BlockSpec((1, D), lambda i, f: (0, 0)),
                  wspec(0), wspec(1), wspec(2)],
        out_specs=[pl.BlockSpec((tm, D), lambda i, f: (i, 0)), pl.BlockSpec((tm, D), lambda i, f: (i, 0)),
                   pl.BlockSpec((tm, tf), lambda i, f: (i, f)), pl.BlockSpec((tm, tf), lambda i, f: (i, f))],
        out_shape=[jax.ShapeDtypeStruct((T, D), F32), jax.ShapeDtypeStruct((T, D), BF16),
                   jax.ShapeDtypeStruct((T, F), BF16), jax.ShapeDtypeStruct((T, F), BF16)],
        scratch_shapes=[pltpu.VMEM((tm, D), BF16), pltpu.VMEM((tm, D), F32)],
        compiler_params=_params(48, 2),
    )(x, g, w3, w3, w3)


def _ffn_dgrad(dout, x, g, gate, up, w3, widx, tm, tf, name):
    T, D = x.shape
    F = w3.shape[1]

    def body(do_ref, x_ref, g_ref, gate_ref, up_ref, wg_ref, wu_ref, wd_ref,
             dx_ref, dgate_ref, dup_ref, dg_ref, dob_ref, dh_ref):
        i = pl.program_id(0)
        f = pl.program_id(1)

        @pl.when(f == 0)
        def _():
            dob_ref[...] = (FFN_RES * do_ref[...]).astype(BF16)
            dh_ref[...] = jnp.zeros_like(dh_ref)

        @pl.when((f == 0) & (i == 0))
        def _():
            dg_ref[...] = jnp.zeros_like(dg_ref)

        dact = _dot(dob_ref[...], wd_ref[...], NT)
        gv = gate_ref[...].astype(F32)
        uv = up_ref[...].astype(F32)
        sig = _sigmoid(gv)
        dup = (dact * (gv * sig)).astype(BF16)
        dgate = (dact * uv * (sig * (1.0 + gv * (1.0 - sig)))).astype(BF16)
        dgate_ref[...] = dgate
        dup_ref[...] = dup
        dh_ref[...] += _dot(dgate, wg_ref[...], NN) + _dot(dup, wu_ref[...], NN)

        @pl.when(f == pl.num_programs(1) - 1)
        def _():
            xhat, rstd = _rms_stats(x_ref[...])
            dx, dg = _rms_bwd(xhat, rstd, g_ref[...], dh_ref[...])
            dx_ref[...] = do_ref[...] + dx
            dg_ref[...] += dg

    wspec = lambda k: pl.BlockSpec((None, tf, D), lambda i, f: (widx + k, f, 0))
    row = pl.BlockSpec((tm, D), lambda i, f: (i, 0))
    hid = pl.BlockSpec((tm, tf), lambda i, f: (i, f))
    return pl.pallas_call(
        body, name=name,
        grid=(T // tm, F // tf),
        in_specs=[row, row, pl.BlockSpec((1, D), lambda i, f: (0, 0)), hid, hid, wspec(0), wspec(1), wspec(2)],
        out_specs=[row, hid, hid, pl.BlockSpec((1, D), lambda i, f: (0, 0)), row],
        out_shape=[jax.ShapeDtypeStruct((T, D), F32), jax.ShapeDtypeStruct((T, F), BF16),
                   jax.ShapeDtypeStruct((T, F), BF16), jax.ShapeDtypeStruct((1, D), F32),
                   jax.ShapeDtypeStruct((T, D), BF16)],
        scratch_shapes=[pltpu.VMEM((tm, D), F32)],
        compiler_params=_params(48, 2),
    )(dout, x, g, gate, up, w3, w3, w3)


def _ffn_wgrad(dgate, dup, gate, up, h, dob, tm, tf, name):
    T, F = gate.shape
    D = h.shape[1]

    def body(dgate_ref, dup_ref, gate_ref, up_ref, h_ref, dob_ref, dw_ref, ag_ref, au_ref, ad_ref):
        i = pl.program_id(1)

        @pl.when(i == 0)
        def _():
            ag_ref[...] = jnp.zeros_like(ag_ref)
            au_ref[...] = jnp.zeros_like(au_ref)
            ad_ref[...] = jnp.zeros_like(ad_ref)

        gv = gate_ref[...].astype(F32)
        act = (gv * _sigmoid(gv) * up_ref[...].astype(F32)).astype(BF16)
        hb = h_ref[...]
        ag_ref[...] += _dot(dgate_ref[...], hb, TN)
        au_ref[...] += _dot(dup_ref[...], hb, TN)
        ad_ref[...] += _dot(act, dob_ref[...], TN)

        @pl.when(i == pl.num_programs(1) - 1)
        def _():
            dw_ref[0] = ag_ref[...].astype(BF16)
            dw_ref[1] = au_ref[...].astype(BF16)
            dw_ref[2] = ad_ref[...].astype(BF16)

    hid = pl.BlockSpec((tm, tf), lambda f, i: (i, f))
    row = pl.BlockSpec((tm, D), lambda f, i: (i, 0))
    return pl.pallas_call(
        body, name=name,
        grid=(F // tf, T // tm),
        in_specs=[hid, hid, hid, hid, row, row],
        out_specs=pl.BlockSpec((3, tf, D), lambda f, i: (0, f, 0)),
        out_shape=jax.ShapeDtypeStruct((3, F, D), BF16),
        scratch_shapes=[pltpu.VMEM((tf, D), F32)] * 3,
        compiler_params=_params(56, 2),
    )(dgate, dup, gate, up, h, dob)


def _mix_in(x1, g, w_in_t, tm, name):
    T, D = x1.shape
    Z = w_in_t.shape[0]

    def body(x_ref, g_ref, w_ref, z_ref, h_ref):
        xhat, _ = _rms_stats(x_ref[...])
        hb = (xhat * g_ref[...]).astype(BF16)
        h_ref[...] = hb
        z_ref[...] = _dot(hb, w_ref[...], NT)

    return pl.pallas_call(
        body, name=name,
        grid=(T // tm,),
        in_specs=[pl.BlockSpec((tm, D), lambda i: (i, 0)), pl.BlockSpec((1, D), lambda i: (0, 0)),
                  pl.BlockSpec((Z, D), lambda i: (0, 0))],
        out_specs=[pl.BlockSpec((tm, Z), lambda i: (i, 0)), pl.BlockSpec((tm, D), lambda i: (i, 0))],
        out_shape=[jax.ShapeDtypeStruct((T, Z), F32), jax.ShapeDtypeStruct((T, D), BF16)],
        compiler_params=_params(40, 1),
    )(x1, g, w_in_t)


def _lru_gates(xr, bda_ref, bdx_ref, vec_ref):
    xrb = xr.astype(BF16)
    r = _sigmoid(_dot(xrb, bda_ref[...], NN) + vec_ref[V_BA:V_BA + 1, :])
    ig = _sigmoid(_dot(xrb, bdx_ref[...], NN) + vec_ref[V_BX:V_BX + 1, :])
    sp = _softplus_neg(vec_ref[V_LAM:V_LAM + 1, :])
    log_a = (-LRU_C * sp) * r
    a = jnp.exp(log_a)
    mult = jnp.sqrt(_neg_expm1(2.0 * log_a))
    return xrb, r, ig, sp, a, mult


def _layernorm_stats(u1):
    xc = u1 - jnp.mean(u1, axis=-1, keepdims=True)
    rs = lax.rsqrt(jnp.mean(xc * xc, axis=-1, keepdims=True) + LN_EPS)
    return xc * rs, rs


def _mix_core_fwd(z, x1, w_out, bda, bdx, cw, lw, vec, tm, name):
    T, D = x1.shape
    W = cw.shape[1]
    assert tm >= CONV_HALO and z.shape[1] == 4 * W

    def body(z_ref, x1_ref, wo_ref, bda_ref, bdx_ref, cw_ref, lw_ref, vec_ref,
             x2_ref, mix_ref, u1_ref, xr_ref, hst_ref, ubuf, rbuf, hc):
        @pl.when(pl.program_id(0) == 0)
        def _():
            ubuf[0:CONV_HALO, :] = jnp.zeros((CONV_HALO, W), F32)
            rbuf[0:LRU_HALO, :] = jnp.zeros((LRU_HALO, W), F32)
            hc[...] = jnp.zeros_like(hc)

        ubuf[CONV_HALO:CONV_HALO + tm, :] = z_ref[:, 0:W] * _sigmoid(z_ref[:, W:2 * W])
        u1 = jnp.zeros((tm, W), F32) + vec_ref[V_CB:V_CB + 1, :]
        for k in range(CONV_K):
            off = CONV_HALO - (CONV_K - 1) + k
            u1 = u1 + cw_ref[k:k + 1, :] * ubuf[off:off + tm, :]
        ubuf[0:CONV_HALO, :] = ubuf[tm:tm + CONV_HALO, :]
        u1_ref[...] = u1
        xh, _ = _layernorm_stats(u1)
        u2 = xh * vec_ref[V_LNG:V_LNG + 1, :] + vec_ref[V_LNB:V_LNB + 1, :]
        ub = (u2 * _sigmoid(u2)).astype(BF16)
        mix_ref[:, 0:W] = ub

        rbuf[LRU_HALO:LRU_HALO + tm, :] = z_ref[:, 2 * W:3 * W]
        xr = jnp.zeros((tm, W), F32) + vec_ref[V_LCB:V_LCB + 1, :]
        for k in range(LRU_K):
            off = LRU_HALO - (LRU_K - 1) + k
            xr = xr + lw_ref[k:k + 1, :] * rbuf[off:off + tm, :]
        rbuf[0:LRU_HALO, :] = rbuf[tm:tm + LRU_HALO, :]
        xr_ref[...] = xr
        _, _, ig, _, a, mult = _lru_gates(xr, bda_ref, bdx_ref, vec_ref)
        av, bv = a, mult * (ig * xr)
        row = lax.broadcasted_iota(jnp.int32, (tm, W), 0)
        s = 1
        while s < tm:
            keep = row >= s
            bv = jnp.where(keep, av * pltpu.roll(bv, s, 0) + bv, bv)
            av = jnp.where(keep, av * pltpu.roll(av, s, 0), av)
            s *= 2
        h = av * hc[0:1, :] + bv
        hc[0:1, :] = h[tm - 1:tm, :]
        hst_ref[...] = h
        gl, _ = _gelu_parts(z_ref[:, 3 * W:4 * W])
        yb = (h * gl).astype(BF16)
        mix_ref[:, W:2 * W] = yb

        x2_ref[...] = x1_ref[...] + _dot(ub, wo_ref[0:W, :], NN) + _dot(yb, wo_ref[W:2 * W, :], NN)

    full = lambda a: pl.BlockSpec(a.shape, lambda i: (0,) * a.ndim)
    tile = lambda n: pl.BlockSpec((tm, n), lambda i: (i, 0))
    return pl.pallas_call(
        body, name=name,
        grid=(T // tm,),
        in_specs=[tile(4 * W), tile(D), full(w_out), full(bda), full(bdx), full(cw), full(lw), full(vec)],
        out_specs=[tile(D), tile(2 * W), tile(W), tile(W), tile(W)],
        out_shape=[jax.ShapeDtypeStruct((T, D), F32), jax.ShapeDtypeStruct((T, 2 * W), BF16),
                   jax.ShapeDtypeStruct((T, W), F32), jax.ShapeDtypeStruct((T, W), F32),
                   jax.ShapeDtypeStruct((T, W), F32)],
        scratch_shapes=[pltpu.VMEM((tm + CONV_HALO, W), F32), pltpu.VMEM((tm + LRU_HALO, W), F32),
                        pltpu.VMEM((8, W), F32)],
        compiler_params=_params(48, 1),
    )(z, x1, w_out, bda, bdx, cw, lw, vec)


def _mix_core_bwd(dx2, z, u1, xr, hst, w_out, bda, bdx, cw, lw, vec, tm, name):
    T, D = dx2.shape
    W = cw.shape[1]
    nt = T // tm
    assert tm >= CONV_HALO and tm % CONV_HALO == 0

    def body(dx_ref, z_ref, zh_ref, u1_ref, xr_ref, h_ref, hh_ref, wo_ref, bda_ref, bdx_ref, cw_ref, lw_ref, vec_ref,
             dz_ref, sg_ref, dbda_ref, dbdx_ref, u0buf, du1buf, rxbuf, dxrbuf, gc, spacc):
        i = pl.program_id(0)
        first = i == nt - 1
        row = lax.broadcasted_iota(jnp.int32, (tm, W), 0)

        @pl.when(i == 0)
        def _():
            sg_ref[...] = jnp.zeros_like(sg_ref)
            dbda_ref[...] = jnp.zeros_like(dbda_ref)
            dbdx_ref[...] = jnp.zeros_like(dbdx_ref)
            du1buf[tm:tm + CONV_HALO, :] = jnp.zeros((CONV_HALO, W), F32)
            dxrbuf[tm:tm + LRU_HALO, :] = jnp.zeros((LRU_HALO, W), F32)
            gc[...] = jnp.zeros_like(gc)
            spacc[...] = jnp.zeros_like(spacc)

        def accum(r, val):
            sg_ref[r:r + 1, :] += jnp.sum(val, axis=0, keepdims=True)

        dmix = _dot(dx_ref[...].astype(BF16), wo_ref[...], NT)
        d_u = dmix[:, 0:W]
        d_yr = dmix[:, W:2 * W]

        xh, rs = _layernorm_stats(u1_ref[...])
        ln_g = vec_ref[V_LNG:V_LNG + 1, :]
        u2 = xh * ln_g + vec_ref[V_LNB:V_LNB + 1, :]
        s2 = _sigmoid(u2)
        d_u2 = d_u * (s2 * (1.0 + u2 * (1.0 - s2)))
        accum(G_LNG, d_u2 * xh)
        accum(G_LNB, d_u2)
        d_xh = d_u2 * ln_g
        d_u1 = rs * (d_xh - jnp.mean(d_xh, axis=-1, keepdims=True)
                     - xh * jnp.mean(d_xh * xh, axis=-1, keepdims=True))
        accum(G_CB, d_u1)
        halo_on = jnp.where(first, 0.0, 1.0)
        u0buf[0:CONV_HALO, :] = halo_on * (zh_ref[:, 0:W] * _sigmoid(zh_ref[:, W:2 * W]))
        cv = z_ref[:, 0:W]
        sgc = _sigmoid(z_ref[:, W:2 * W])
        u0buf[CONV_HALO:CONV_HALO + tm, :] = cv * sgc
        du1buf[0:tm, :] = d_u1
        d_u0 = jnp.zeros((tm, W), F32)
        for k in range(CONV_K):
            off = CONV_HALO - (CONV_K - 1) + k
            accum(G_CW + k, d_u1 * u0buf[off:off + tm, :])
            d_u0 = d_u0 + cw_ref[k:k + 1, :] * du1buf[CONV_K - 1 - k:CONV_K - 1 - k + tm, :]
        du1buf[tm:tm + CONV_HALO, :] = du1buf[0:CONV_HALO, :]
        dz_ref[:, 0:W] = (d_u0 * sgc).astype(BF16)
        dz_ref[:, W:2 * W] = (d_u0 * cv * (sgc * (1.0 - sgc))).astype(BF16)

        xrv = xr_ref[...]
        xrb, r, ig, sp, a, mult = _lru_gates(xrv, bda_ref, bdx_ref, vec_ref)
        h = h_ref[...]
        gl, dgl = _gelu_parts(z_ref[:, 3 * W:4 * W])
        dz_ref[:, 3 * W:4 * W] = (d_yr * h * dgl).astype(BF16)
        bv = d_yr * gl + jnp.where(row == tm - 1, gc[0:1, :], 0.0)
        av = pltpu.roll(a, tm - 1, 0)
        s = 1
        while s < tm:
            keep = row < tm - s
            bv = jnp.where(keep, av * pltpu.roll(bv, tm - s, 0) + bv, bv)
            av = jnp.where(keep, av * pltpu.roll(av, tm - s, 0), av)
            s *= 2
        g = bv
        gc[0:1, :] = a[0:1, :] * g[0:1, :]
        hprev = jnp.where(row == 0, halo_on * hh_ref[LRU_HALO - 1:LRU_HALO, :], pltpu.roll(h, 1, 0))
        d_log_a = (g * hprev) * a - (g * ig * xrv) * (a * a) / mult
        d_ig = g * mult * xrv
        d_xr = g * mult * ig
        spacc[0:1, :] += jnp.sum(d_log_a * r, axis=0, keepdims=True)
        d_pa32 = (d_log_a * (-LRU_C * sp)) * (r * (1.0 - r))
        d_px32 = d_ig * (ig * (1.0 - ig))
        accum(G_BA, d_pa32)
        accum(G_BX, d_px32)
        d_pa = d_pa32.astype(BF16)
        d_px = d_px32.astype(BF16)
        d_xr = d_xr + _dot(d_pa, bda_ref[...], NT) + _dot(d_px, bdx_ref[...], NT)
        dbda_ref[...] += _dot(xrb, d_pa, TN)
        dbdx_ref[...] += _dot(xrb, d_px, TN)
        accum(G_LCB, d_xr)
        rxbuf[0:LRU_HALO, :] = halo_on * zh_ref[CONV_HALO - LRU_HALO:CONV_HALO, 2 * W:3 * W]
        rxbuf[LRU_HALO:LRU_HALO + tm, :] = z_ref[:, 2 * W:3 * W]
        dxrbuf[0:tm, :] = d_xr
        d_rx = jnp.zeros((tm, W), F32)
        for k in range(LRU_K):
            off = LRU_HALO - (LRU_K - 1) + k
            accum(G_LW + k, d_xr * rxbuf[off:off + tm, :])
            d_rx = d_rx + lw_ref[k:k + 1, :] * dxrbuf[LRU_K - 1 - k:LRU_K - 1 - k + tm, :]
        dxrbuf[tm:tm + LRU_HALO, :] = dxrbuf[0:LRU_HALO, :]
        dz_ref[:, 2 * W:3 * W] = d_rx.astype(BF16)

        @pl.when(first)
        def _():
            lam = vec_ref[V_LAM:V_LAM + 1, :]
            sg_ref[G_LAM:G_LAM + 1, :] = LRU_C * _sigmoid(-lam) * spacc[0:1, :]

    full = lambda a: pl.BlockSpec(a.shape, lambda i: (0,) * a.ndim)
    tile = lambda n: pl.BlockSpec((tm, n), lambda i: (nt - 1 - i, 0))
    halo = lambda rows, n: pl.BlockSpec(
        (rows, n), lambda i: (jnp.maximum((nt - 1 - i) * (tm // rows) - 1, 0), 0))
    return pl.pallas_call(
        body, name=name,
        grid=(nt,),
        in_specs=[tile(D), tile(4 * W), halo(CONV_HALO, 4 * W), tile(W), tile(W), tile(W), halo(LRU_HALO, W),
                  full(w_out), full(bda), full(bdx), full(cw), full(lw), full(vec)],
        out_specs=[tile(4 * W), pl.BlockSpec((G_ROWS, W), lambda i: (0, 0)),
                   pl.BlockSpec((W, W), lambda i: (0, 0)), pl.BlockSpec((W, W), lambda i: (0, 0))],
        out_shape=[jax.ShapeDtypeStruct((T, 4 * W), BF16), jax.ShapeDtypeStruct((G_ROWS, W), F32),
                   jax.ShapeDtypeStruct((W, W), F32), jax.ShapeDtypeStruct((W, W), F32)],
        scratch_shapes=[pltpu.VMEM((tm + CONV_HALO, W), F32), pltpu.VMEM((tm + CONV_HALO, W), F32),
                        pltpu.VMEM((tm + LRU_HALO, W), F32), pltpu.VMEM((tm + LRU_HALO, W), F32),
                        pltpu.VMEM((8, W), F32), pltpu.VMEM((8, W), F32)],
        compiler_params=_params(56, 1),
    )(dx2, z, z, u1, xr, hst, hst, w_out, bda, bdx, cw, lw, vec)


def _mix_in_bwd(dz, x1, dx2, mix, g, w_in_t, tm, name):
    T, D = x1.shape
    Z = w_in_t.shape[0]
    M = mix.shape[1]

    def body(dz_ref, x_ref, dx2_ref, mix_ref, g_ref, w_ref, dx1_ref, dg_ref, dwi_ref, dwo_ref, ai_ref, ao_ref):
        i = pl.program_id(0)

        @pl.when(i == 0)
        def _():
            dg_ref[...] = jnp.zeros_like(dg_ref)
            ai_ref[...] = jnp.zeros_like(ai_ref)
            ao_ref[...] = jnp.zeros_like(ao_ref)

        xhat, rstd = _rms_stats(x_ref[...])
        gv = g_ref[...]
        hb = (xhat * gv).astype(BF16)
        dzb = dz_ref[...]
        dx, dg = _rms_bwd(xhat, rstd, gv, _dot(dzb, w_ref[...], NN))
        dx2 = dx2_ref[...]
        dx1_ref[...] = dx2 + dx
        dg_ref[...] += dg
        ai_ref[...] += _dot(dzb, hb, TN)
        ao_ref[...] += _dot(mix_ref[...], dx2.astype(BF16), TN)

        @pl.when(i == pl.num_programs(0) - 1)
        def _():
            dwi_ref[...] = ai_ref[...].astype(BF16)
            dwo_ref[...] = ao_ref[...].astype(BF16)

    tile = lambda n: pl.BlockSpec((tm, n), lambda i: (i, 0))
    const = lambda r, c: pl.BlockSpec((r, c), lambda i: (0, 0))
    return pl.pallas_call(
        body, name=name,
        grid=(T // tm,),
        in_specs=[tile(Z), tile(D), tile(D), tile(M), const(1, D), const(Z, D)],
        out_specs=[tile(D), const(1, D), const(Z, D), const(M, D)],
        out_shape=[jax.ShapeDtypeStruct((T, D), F32), jax.ShapeDtypeStruct((1, D), F32),
                   jax.ShapeDtypeStruct((Z, D), BF16), jax.ShapeDtypeStruct((M, D), BF16)],
        scratch_shapes=[pltpu.VMEM((Z, D), F32), pltpu.VMEM((M, D), F32)],
        compiler_params=_params(56, 1),
    )(dz, x1, dx2, mix, g, w_in_t)


def _final_loss(x3, g, target, tm, name):
    T, D = x3.shape

    def body(x_ref, g_ref, t_ref, dx_ref, dg_ref, loss_ref):
        @pl.when(pl.program_id(0) == 0)
        def _():
            dg_ref[...] = jnp.zeros_like(dg_ref)
            loss_ref[...] = jnp.zeros_like(loss_ref)

        xhat, rstd = _rms_stats(x_ref[...])
        gv = g_ref[...]
        err = xhat * gv - t_ref[...]
        loss_ref[...] += (0.5 / D) * jnp.sum(err * err)
        dx, dg = _rms_bwd(xhat, rstd, gv, err * (1.0 / D))
        dx_ref[...] = dx
        dg_ref[...] += dg

    tile = pl.BlockSpec((tm, D), lambda i: (i, 0))
    return pl.pallas_call(
        body, name=name,
        grid=(T // tm,),
        in_specs=[tile, pl.BlockSpec((1, D), lambda i: (0, 0)), tile],
        out_specs=[tile, pl.BlockSpec((1, D), lambda i: (0, 0)), pl.BlockSpec((1, 128), lambda i: (0, 0))],
        out_shape=[jax.ShapeDtypeStruct((T, D), F32), jax.ShapeDtypeStruct((1, D), F32),
                   jax.ShapeDtypeStruct((1, 128), F32)],
        compiler_params=_params(32, 1),
    )(x3, g, target)


HBM_SPEC = pl.BlockSpec(memory_space=pltpu.HBM)


def _place():
    x, y, c = lax.axis_index("x"), lax.axis_index("y"), lax.axis_index("c")
    return x, y, c


def _all_gather(shards, block_views, out_shapes, name):
    na = len(shards)

    def body(*refs):
        srcs, outs = refs[:na], refs[na:2 * na]
        send_sems, recv_sems, local_sems = refs[2 * na:]
        x, y, c = _place()
        sibling = (x, y, 1 - c)
        chips = [(1 - x, y), (x, 1 - y), (1 - x, 1 - y)]

        def blk(px, py, pc):
            return 4 * px + 2 * py + pc

        def copy(a, k, block, to, from_src=False):
            dst = block_views[a](outs[a], blk(*block))
            return pltpu.make_async_remote_copy(
                src_ref=srcs[a] if from_src else dst, dst_ref=dst,
                send_sem=send_sems.at[7 * a + k], recv_sem=recv_sems.at[7 * a + k],
                device_id=to, device_id_type=MESH)

        me = (x, y, c)
        local = [pltpu.make_async_copy(srcs[a], block_views[a](outs[a], blk(*me)), local_sems.at[a])
                 for a in range(na)]
        for cp in local:
            cp.start()
        first = []
        for a in range(na):
            first.append(copy(a, 0, me, sibling, from_src=True))
            first += [copy(a, 1 + j, me, (*chip, c), from_src=True) for j, chip in enumerate(chips)]
        for cp in first:
            cp.start()
        passed = []
        for j, chip in enumerate(chips):
            for a in range(na):
                copy(a, 1 + j, (*chip, c), me).wait_recv()
                fwd = copy(a, 4 + j, (*chip, c), sibling)
                fwd.start()
                passed.append(fwd)
        for a in range(na):
            copy(a, 0, sibling, me).wait_recv()
            for j, chip in enumerate(chips):
                copy(a, 4 + j, (*chip, 1 - c), me).wait_recv()
        for cp in first + passed:
            cp.wait_send()
        for cp in local:
            cp.wait()

    return pl.pallas_call(
        body, name=name,
        in_specs=[HBM_SPEC] * na, out_specs=[HBM_SPEC] * na, out_shape=out_shapes,
        scratch_shapes=[pltpu.SemaphoreType.DMA((7 * na,)), pltpu.SemaphoreType.DMA((7 * na,)),
                        pltpu.SemaphoreType.DMA((na,))],
            )(*shards)


def _sibling_exchange(grads, core_views, name):
    na = len(grads)
    out_shapes = [jax.ShapeDtypeStruct(g.shape[:2] + g.shape[3:], g.dtype) for g in grads]

    def body(*refs):
        srcs, outs = refs[:na], refs[na:2 * na]
        send_sems, recv_sems = refs[2 * na:]
        x, y, c = _place()
        copies = [pltpu.make_async_remote_copy(
            src_ref=core_views[a](srcs[a], 1 - c), dst_ref=outs[a],
            send_sem=send_sems.at[a], recv_sem=recv_sems.at[a],
            device_id=(x, y, 1 - c), device_id_type=MESH) for a in range(na)]
        for cp in copies:
            cp.start()
        for cp in copies:
            cp.wait()

    return pl.pallas_call(
        body, name=name,
        in_specs=[HBM_SPEC] * na, out_specs=[HBM_SPEC] * na, out_shape=out_shapes,
        scratch_shapes=[pltpu.SemaphoreType.DMA((na,)), pltpu.SemaphoreType.DMA((na,))],
            )(*grads)


def _chip_exchange(sums, chip_views, name):
    na = len(sums)
    out_shapes = [jax.ShapeDtypeStruct(s.shape, s.dtype) for s in sums]

    def body(*refs):
        srcs, outs = refs[:na], refs[na:2 * na]
        send_sems, recv_sems, local_sems = refs[2 * na:]
        x, y, c = _place()
        mine = 2 * x + y
        chips = [(1 - x, y), (x, 1 - y), (1 - x, 1 - y)]
        local = [pltpu.make_async_copy(chip_views[a](srcs[a], mine), chip_views[a](outs[a], mine), local_sems.at[a])
                 for a in range(na)]
        for cp in local:
            cp.start()
        sends, recvs = [], []
        for a in range(na):
            for j, (px, py) in enumerate(chips):
                theirs = 2 * px + py
                sends.append(pltpu.make_async_remote_copy(
                    src_ref=chip_views[a](srcs[a], theirs), dst_ref=chip_views[a](outs[a], mine),
                    send_sem=send_sems.at[3 * a + j], recv_sem=recv_sems.at[3 * a + j],
                    device_id=(px, py, c), device_id_type=MESH))
                recvs.append(pltpu.make_async_remote_copy(
                    src_ref=chip_views[a](srcs[a], mine), dst_ref=chip_views[a](outs[a], theirs),
                    send_sem=send_sems.at[3 * a + j], recv_sem=recv_sems.at[3 * a + j],
                    device_id=(px, py, c), device_id_type=MESH))
        for cp in sends:
            cp.start()
        for cp in recvs:
            cp.wait_recv()
        for cp in sends:
            cp.wait_send()
        for cp in local:
            cp.wait()

    return pl.pallas_call(
        body, name=name,
        in_specs=[HBM_SPEC] * na, out_specs=[HBM_SPEC] * na, out_shape=out_shapes,
        scratch_shapes=[pltpu.SemaphoreType.DMA((3 * na,)), pltpu.SemaphoreType.DMA((3 * na,)),
                        pltpu.SemaphoreType.DMA((na,))],
            )(*sums)


def _pair_add(full, recv, rows, name):
    K = full.shape[0]
    D = full.shape[-1]

    def body(c_ref, a_ref, b_ref, o_ref):
        o_ref[...] = (a_ref[...].astype(F32) + b_ref[...].astype(F32)).astype(BF16)

    c = lax.axis_index("c").astype(jnp.int32).reshape((1,))
    return pl.pallas_call(
        body, name=name,
        grid_spec=pltpu.PrefetchScalarGridSpec(
            num_scalar_prefetch=1, grid=(K, N_CHIP),
            in_specs=[pl.BlockSpec((None, None, None, rows, D), lambda k, q, c_ref: (k, q, c_ref[0], 0, 0)),
                      pl.BlockSpec((None, None, rows, D), lambda k, q, c_ref: (k, q, 0, 0))],
            out_specs=pl.BlockSpec((None, None, rows, D), lambda k, q, c_ref: (k, q, 0, 0))),
        out_shape=jax.ShapeDtypeStruct(recv.shape, BF16),
        compiler_params=_params(16, 2),
    )(c, full, recv)


def _chip_add(parts, rows, name):
    K = parts.shape[0]
    D = parts.shape[-1]

    def body(p_ref, o_ref):
        acc = p_ref[0].astype(F32)
        for q in range(1, N_CHIP):
            acc = acc + p_ref[q].astype(F32)
        o_ref[...] = acc

    return pl.pallas_call(
        body, name=name,
        grid=(K,),
        in_specs=[pl.BlockSpec((None, N_CHIP, rows, D), lambda k: (k, 0, 0, 0))],
        out_specs=pl.BlockSpec((None, rows, D), lambda k: (k, 0, 0)),
        out_shape=jax.ShapeDtypeStruct((K, rows, D), F32),
        compiler_params=_params(24, 1),
    )(parts)


def _all_reduce_small(s, name):
    R, C = s.shape

    def body(s_ref, o_ref, buf, send_sems, recv_sems):
        x, y, c = _place()
        me = 4 * x + 2 * y + c
        buf[me] = s_ref[...]
        copies = []
        for k in range(1, N_DEV):
            dx, dy, dc = (k >> 2) & 1, (k >> 1) & 1, k & 1
            peer = (x ^ dx, y ^ dy, c ^ dc)
            copies.append(pltpu.make_async_remote_copy(
                src_ref=s_ref, dst_ref=buf.at[me],
                send_sem=send_sems.at[k - 1], recv_sem=recv_sems.at[k - 1],
                device_id=peer, device_id_type=MESH))
        for cp in copies:
            cp.start()
        for k in range(1, N_DEV):
            dx, dy, dc = (k >> 2) & 1, (k >> 1) & 1, k & 1
            theirs = 4 * (x ^ dx) + 2 * (y ^ dy) + (c ^ dc)
            pltpu.make_async_remote_copy(
                src_ref=s_ref, dst_ref=buf.at[theirs],
                send_sem=send_sems.at[k - 1], recv_sem=recv_sems.at[k - 1],
                device_id=(x ^ dx, y ^ dy, c ^ dc), device_id_type=MESH).wait_recv()
        for cp in copies:
            cp.wait_send()
        acc = buf[0]
        for j in range(1, N_DEV):
            acc = acc + buf[j]
        o_ref[...] = acc

    vmem = pl.BlockSpec(memory_space=pltpu.VMEM)
    return pl.pallas_call(
        body, name=name,
        in_specs=[vmem], out_specs=vmem, out_shape=jax.ShapeDtypeStruct((R, C), F32),
        scratch_shapes=[pltpu.VMEM((N_DEV, R, C), F32), pltpu.SemaphoreType.DMA((N_DEV - 1,)),
                        pltpu.SemaphoreType.DMA((N_DEV - 1,))],
        compiler_params=pltpu.CompilerParams(vmem_limit_bytes=24 * MIB),
    )(s)


def _adamw(w, g, m, v, name):
    def body(w_ref, g_ref, m_ref, v_ref, d_ref, mo_ref, vo_ref):
        gv = g_ref[...]
        m2 = ADAM_B1 * m_ref[...] + (1.0 - ADAM_B1) * gv
        v2 = ADAM_B2 * v_ref[...] + (1.0 - ADAM_B2) * (gv * gv)
        m_hat = m2 / (1.0 - ADAM_B1 ** ADAM_STEP)
        v_hat = v2 / (1.0 - ADAM_B2 ** ADAM_STEP)
        d_ref[...] = -ADAM_LR * (m_hat / (jnp.sqrt(v_hat) + ADAM_EPS) + ADAM_WD * w_ref[...])
        mo_ref[...] = m2
        vo_ref[...] = v2

    vmem = pl.BlockSpec(memory_space=pltpu.VMEM)
    shape = jax.ShapeDtypeStruct(w.shape, F32)
    return pl.pallas_call(
        body, name=name,
        in_specs=[vmem] * 4, out_specs=[vmem] * 3, out_shape=[shape] * 3,
        compiler_params=pltpu.CompilerParams(vmem_limit_bytes=32 * MIB),
    )(w, g, m, v)


def _block_diag(w):
    h, d, _ = w.shape
    out = jnp.zeros((h * d, h * d), w.dtype)
    for k in range(h):
        out = lax.dynamic_update_slice(out, w[k], (k * d, k * d))
    return out


def _diag_blocks(m, h):
    d = m.shape[0] // h
    return jnp.stack([m[k * d:(k + 1) * d, k * d:(k + 1) * d] for k in range(h)])


def kernel(x, ffn1_norm, ffn1_w_gate, ffn1_w_up, ffn1_w_down, mix_norm, w_in, conv_dw, conv_dw_bias, conv_ln_g, conv_ln_b, lru_conv_w, lru_conv_b, lru_w_a, lru_b_a, lru_w_x, lru_b_x, lru_lambda, w_out, ffn2_norm, ffn2_w_gate, ffn2_w_up, ffn2_w_down, final_norm, loss_target, m_ffn1_norm, m_ffn1_w_gate, m_ffn1_w_up, m_ffn1_w_down, m_mix_norm, m_w_in, m_conv_dw, m_conv_dw_bias, m_conv_ln_g, m_conv_ln_b, m_lru_conv_w, m_lru_conv_b, m_lru_w_a, m_lru_b_a, m_lru_w_x, m_lru_b_x, m_lru_lambda, m_w_out, m_ffn2_norm, m_ffn2_w_gate, m_ffn2_w_up, m_ffn2_w_down, m_final_norm, v_ffn1_norm, v_ffn1_w_gate, v_ffn1_w_up, v_ffn1_w_down, v_mix_norm, v_w_in, v_conv_dw, v_conv_dw_bias, v_conv_ln_g, v_conv_ln_b, v_lru_conv_w, v_lru_conv_b, v_lru_w_a, v_lru_b_a, v_lru_w_x, v_lru_b_x, v_lru_lambda, v_w_out, v_ffn2_norm, v_ffn2_w_gate, v_ffn2_w_up, v_ffn2_w_down, v_final_norm):
    T, D = x.shape[1], x.shape[2]
    F = ffn1_w_down.shape[0] * N_DEV
    rf = ffn1_w_down.shape[0]
    ri = w_in.shape[1]
    ro = w_out.shape[0]
    W = conv_dw_bias.shape[0]
    wc = conv_dw.shape[1]
    H = lru_w_a.shape[0]
    xs = x.reshape(T, D)
    tgt = loss_target.reshape(T, D)
    tm_ffn = min(1024, T)
    tm_w = min(512, T)
    tm_mix = min(256, T)
    tf = 256
    tf_w = F // 2
    row = lambda v: v.reshape(1, -1)

    p6 = jnp.stack([ffn1_w_gate.T, ffn1_w_up.T, ffn1_w_down, ffn2_w_gate.T, ffn2_w_up.T, ffn2_w_down]).astype(BF16)
    p_in = w_in.T.astype(BF16)
    p_out = w_out.astype(BF16)
    p_cw = jnp.concatenate([conv_dw, lru_conv_w, jnp.zeros((40 - CONV_K - LRU_K, wc), F32)], axis=0)
    g6, g_in, g_out, g_cw = _all_gather(
        [p6, p_in, p_out, p_cw],
        [lambda r, j: r.at[:, j], lambda r, j: r.at[j], lambda r, j: r.at[j], lambda r, j: r.at[j]],
        [jax.ShapeDtypeStruct((6, N_DEV, rf, D), BF16), jax.ShapeDtypeStruct((N_DEV, ri, D), BF16),
         jax.ShapeDtypeStruct((N_DEV, ro, D), BF16), jax.ShapeDtypeStruct((N_DEV, 40, wc), F32)],
        "ag_weights")
    w6 = g6.reshape(6, F, D)
    w_in_t = g_in.reshape(N_DEV * ri, D)
    w_out_f = g_out.reshape(N_DEV * ro, D)
    cw_all = jnp.transpose(g_cw, (1, 0, 2)).reshape(40, N_DEV * wc)
    cw = cw_all[0:CONV_K]
    lw = cw_all[CONV_K:CONV_K + LRU_K]
    bda = _block_diag(lru_w_a).astype(BF16)
    bdx = _block_diag(lru_w_x).astype(BF16)
    vec = jnp.concatenate([jnp.stack([conv_dw_bias, conv_ln_g, conv_ln_b, lru_conv_b, lru_b_a, lru_b_x, lru_lambda]),
                           jnp.zeros((9, W), F32)], axis=0)

    x1, h1, gate1, up1 = _ffn_fwd(xs, row(ffn1_norm), w6, 0, tm_ffn, tf, "ffn1_fwd")
    z, _ = _mix_in(x1, row(mix_norm), w_in_t, tm_w, "mix_in")
    x2, mix, u1, xr, hst = _mix_core_fwd(z, x1, w_out_f, bda, bdx, cw, lw, vec, tm_mix, "mix_core_fwd")
    x3, h3, gate2, up2 = _ffn_fwd(x2, row(ffn2_norm), w6, 3, tm_ffn, tf, "ffn2_fwd")
    dx3, d_final_norm, loss_part = _final_loss(x3, row(final_norm), tgt, tm_w, "final_loss")

    dx2, dgate2, dup2, d_ffn2_norm, dob2 = _ffn_dgrad(dx3, x2, row(ffn2_norm), gate2, up2, w6, 3, tm_ffn, tf, "ffn2_dgrad")
    dw_f2 = _ffn_wgrad(dgate2, dup2, gate2, up2, h3, dob2, tm_w, tf_w, "ffn2_wgrad")
    dz, sg, dbda, dbdx = _mix_core_bwd(dx2, z, u1, xr, hst, w_out_f, bda, bdx, cw, lw, vec, tm_mix, "mix_core_bwd")
    dx1, d_mix_norm, dw_in_t, dw_out = _mix_in_bwd(dz, x1, dx2, mix, row(mix_norm), w_in_t, tm_mix, "mix_in_bwd")
    dx0, dgate1, dup1, d_ffn1_norm, dob1 = _ffn_dgrad(dx1, xs, row(ffn1_norm), gate1, up1, w6, 0, tm_ffn, tf, "ffn1_dgrad")
    dw_f1 = _ffn_wgrad(dgate1, dup1, gate1, up1, h1, dob1, tm_w, tf_w, "ffn1_wgrad")

    full = [dw_f1.reshape(3, N_CHIP, 2, rf, D), dw_f2.reshape(3, N_CHIP, 2, rf, D),
            dw_in_t.reshape(1, N_CHIP, 2, ri, D), dw_out.reshape(1, N_CHIP, 2, ro, D)]
    rows = [rf, rf, ri, ro]
    names = ["f1", "f2", "in", "out"]
    core_view = lambda r, cc: r.at[:, :, cc]
    got = _sibling_exchange(full, [core_view] * 4, "rs_sibling")
    sums = [_pair_add(a, b, n, "rs_pair_add_" + s) for a, b, n, s in zip(full, got, rows, names)]
    parts = _chip_exchange(sums, [lambda r, q: r.at[:, q]] * 4, "rs_chips")
    gf1, gf2, gin, gout = [_chip_add(p, n, "rs_chip_add_" + s) for p, n, s in zip(parts, rows, names)]

    d_lru_w_a = _diag_blocks(dbda, H)
    d_lru_w_x = _diag_blocks(dbdx, H)
    small = [d_ffn1_norm, d_mix_norm, d_ffn2_norm, d_final_norm,
             sg[G_CB], sg[G_LNG], sg[G_LNB], sg[G_LCB], sg[G_BA], sg[G_BX], sg[G_LAM],
             d_lru_w_a, d_lru_w_x,
             sg[G_CW:G_CW + CONV_K], sg[G_LW:G_LW + LRU_K],
             loss_part]
    sizes = [a.size for a in small]
    flat = jnp.concatenate([a.reshape(-1) for a in small])
    n_rows = -(-flat.size // (8 * D)) * 8
    packed = jnp.concatenate([flat, jnp.zeros((n_rows * D - flat.size,), F32)]).reshape(n_rows, D)
    red = _all_reduce_small(packed, "ar_small").reshape(-1)
    offs = [0]
    for n in sizes:
        offs.append(offs[-1] + n)
    piece = lambda k, shape: red[offs[k]:offs[k + 1]].reshape(shape)
    loss = red[offs[15]]

    me = 4 * lax.axis_index("x") + 2 * lax.axis_index("y") + lax.axis_index("c")
    chan = lambda full_g: lax.dynamic_slice_in_dim(full_g, me * wc, wc, axis=1)
    grads = {
        "ffn1_norm": piece(0, (D,)), "mix_norm": piece(1, (D,)), "ffn2_norm": piece(2, (D,)), "final_norm": piece(3, (D,)),
        "conv_dw_bias": piece(4, (W,)), "conv_ln_g": piece(5, (W,)), "conv_ln_b": piece(6, (W,)),
        "lru_conv_b": piece(7, (W,)), "lru_b_a": piece(8, (W,)), "lru_b_x": piece(9, (W,)), "lru_lambda": piece(10, (W,)),
        "lru_w_a": piece(11, lru_w_a.shape), "lru_w_x": piece(12, lru_w_x.shape),
        "conv_dw": chan(piece(13, (CONV_K, W))), "lru_conv_w": chan(piece(14, (LRU_K, W))),
        "ffn1_w_gate": gf1[0].T, "ffn1_w_up": gf1[1].T, "ffn1_w_down": gf1[2],
        "ffn2_w_gate": gf2[0].T, "ffn2_w_up": gf2[1].T, "ffn2_w_down": gf2[2],
        "w_in": gin[0].T, "w_out": gout[0],
    }

    weights = dict(ffn1_norm=ffn1_norm, ffn1_w_gate=ffn1_w_gate, ffn1_w_up=ffn1_w_up, ffn1_w_down=ffn1_w_down, mix_norm=mix_norm, w_in=w_in, conv_dw=conv_dw, conv_dw_bias=conv_dw_bias, conv_ln_g=conv_ln_g, conv_ln_b=conv_ln_b, lru_conv_w=lru_conv_w, lru_conv_b=lru_conv_b, lru_w_a=lru_w_a, lru_b_a=lru_b_a, lru_w_x=lru_w_x, lru_b_x=lru_b_x, lru_lambda=lru_lambda, w_out=w_out, ffn2_norm=ffn2_norm, ffn2_w_gate=ffn2_w_gate, ffn2_w_up=ffn2_w_up, ffn2_w_down=ffn2_w_down, final_norm=final_norm)
    moment1 = dict(ffn1_norm=m_ffn1_norm, ffn1_w_gate=m_ffn1_w_gate, ffn1_w_up=m_ffn1_w_up, ffn1_w_down=m_ffn1_w_down, mix_norm=m_mix_norm, w_in=m_w_in, conv_dw=m_conv_dw, conv_dw_bias=m_conv_dw_bias, conv_ln_g=m_conv_ln_g, conv_ln_b=m_conv_ln_b, lru_conv_w=m_lru_conv_w, lru_conv_b=m_lru_conv_b, lru_w_a=m_lru_w_a, lru_b_a=m_lru_b_a, lru_w_x=m_lru_w_x, lru_b_x=m_lru_b_x, lru_lambda=m_lru_lambda, w_out=m_w_out, ffn2_norm=m_ffn2_norm, ffn2_w_gate=m_ffn2_w_gate, ffn2_w_up=m_ffn2_w_up, ffn2_w_down=m_ffn2_w_down, final_norm=m_final_norm)
    moment2 = dict(ffn1_norm=v_ffn1_norm, ffn1_w_gate=v_ffn1_w_gate, ffn1_w_up=v_ffn1_w_up, ffn1_w_down=v_ffn1_w_down, mix_norm=v_mix_norm, w_in=v_w_in, conv_dw=v_conv_dw, conv_dw_bias=v_conv_dw_bias, conv_ln_g=v_conv_ln_g, conv_ln_b=v_conv_ln_b, lru_conv_w=v_lru_conv_w, lru_conv_b=v_lru_conv_b, lru_w_a=v_lru_w_a, lru_b_a=v_lru_b_a, lru_w_x=v_lru_w_x, lru_b_x=v_lru_b_x, lru_lambda=v_lru_lambda, w_out=v_w_out, ffn2_norm=v_ffn2_norm, ffn2_w_gate=v_ffn2_w_gate, ffn2_w_up=v_ffn2_w_up, ffn2_w_down=v_ffn2_w_down, final_norm=v_final_norm)
    order = list(weights)
    big = ["ffn1_w_gate", "ffn1_w_up", "ffn1_w_down", "w_in", "w_out", "ffn2_w_gate", "ffn2_w_up", "ffn2_w_down"]
    delta, new_m, new_v = {}, {}, {}
    for n in big:
        delta[n], new_m[n], new_v[n] = _adamw(weights[n], grads[n], moment1[n], moment2[n], "adamw_" + n)
    rest = [n for n in order if n not in big]
    rest_sizes = [weights[n].size for n in rest]
    n_flat = sum(rest_sizes)
    pad_rows = -(-n_flat // (8 * 128)) * 8

    def pack(d):
        f = jnp.concatenate([d[n].reshape(-1) for n in rest])
        return jnp.concatenate([f, jnp.zeros((pad_rows * 128 - n_flat,), F32)]).reshape(pad_rows, 128)

    v_pack = pack(moment2)
    d_s, m_s, v_s = _adamw(pack(weights), pack(grads), pack(moment1), v_pack, "adamw_small")
    off = 0
    for n, size in zip(rest, rest_sizes):
        shape = weights[n].shape
        delta[n] = d_s.reshape(-1)[off:off + size].reshape(shape)
        new_m[n] = m_s.reshape(-1)[off:off + size].reshape(shape)
        new_v[n] = v_s.reshape(-1)[off:off + size].reshape(shape)
        off += size

    return (loss, dx0.reshape(x.shape), *[grads[n] for n in order], *[delta[n] for n in order],
            *[new_m[n] for n in order], *[new_v[n] for n in order])
```

```python
import functools
import math

import jax
import jax.numpy as jnp
from jax import lax
from jax.experimental import pallas as pl
from jax.experimental.pallas import tpu as pltpu

F32 = jnp.float32
BF16 = jnp.bfloat16
MESH = pl.DeviceIdType.MESH

N_DEV = 8
N_CHIP = 4
SUBLANES = 8
RMS_EPS = 1e-6
LN_EPS = 1e-5
LRU_C = 8.0
CONV_K = 31
LRU_K = 4
CONV_HALO = 32
LRU_HALO = 8
FFN_RES = 0.5
ADAM_LR, ADAM_B1, ADAM_B2, ADAM_EPS, ADAM_WD, ADAM_STEP = 0.001, 0.9, 0.999, 1e-08, 0.01, 10
GELU_K = math.sqrt(2.0 / math.pi)
GELU_C = 0.044715

MIB = 1024 * 1024
NT = (((1,), (1,)), ((), ()))
NN = (((1,), (0,)), ((), ()))
TN = (((0,), (0,)), ((), ()))

V_CB, V_LNG, V_LNB, V_LCB, V_BA, V_BX, V_LAM = range(7)
G_CW = 0
G_CB, G_LNG, G_LNB = 31, 32, 33
G_LW = 34
G_LCB, G_BA, G_BX, G_LAM = 38, 39, 40, 41
G_ROWS = 48

HBM_SPEC = pl.BlockSpec(memory_space=pltpu.HBM)
VMEM_SPEC = pl.BlockSpec(memory_space=pltpu.VMEM)


def _dot(a, b, dims):
    return lax.dot_general(a, b, dims, preferred_element_type=F32)


def _sigmoid(x):
    return 1.0 / (1.0 + jnp.exp(-x))


def _gelu_parts(x):
    x2 = x * x
    th = jnp.tanh(GELU_K * x * (1.0 + GELU_C * x2))
    gl = 0.5 * x * (1.0 + th)
    dgl = 0.5 * (1.0 + th) + 0.5 * x * (1.0 - th * th) * GELU_K * (1.0 + 3.0 * GELU_C * x2)
    return gl, dgl


def _neg_expm1(y):
    series = -y * (1.0 + y * (1.0 / 2) * (1.0 + y * (1.0 / 3) * (1.0 + y * (1.0 / 4) * (1.0 + y * (1.0 / 5) * (1.0 + y * (1.0 / 6))))))
    return jnp.where(y > -0.25, series, 1.0 - jnp.exp(y))


def _softplus_neg(lam):
    t = -lam
    e = jnp.exp(-jnp.abs(t))
    s = 1.0 + e
    log1p_e = jnp.log(s) - ((s - 1.0) - e) / s
    return jnp.maximum(t, 0.0) + log1p_e


def _rms_stats(xv):
    rstd = lax.rsqrt(jnp.mean(xv * xv, axis=-1, keepdims=True) + RMS_EPS)
    return xv * rstd, rstd


def _rms_bwd(xhat, rstd, g, dh):
    dxhat = dh * g
    dx = rstd * (dxhat - xhat * jnp.mean(dxhat * xhat, axis=-1, keepdims=True))
    return dx, jnp.sum(dh * xhat, axis=0, keepdims=True)


def _row_windows(buf_ref, n_rows, offsets):
    for b in range(SUBLANES):
        offs = [o for o in offsets if o % SUBLANES == b]
        if not offs:
            continue
        lo, hi = min(offs), max(offs)
        shifted = buf_ref[lo:hi + n_rows, :]
        for o in offs:
            yield o, shifted[o - lo:o - lo + n_rows, :]


class _Comm:
    def __init__(self, arrays, in_specs, out_shapes, out_specs, scratch, start, wait, aliases=None):
        self.arrays, self.in_specs = list(arrays), list(in_specs)
        self.out_shapes, self.out_specs = list(out_shapes), list(out_specs)
        self.scratch, self.start, self.wait = list(scratch), start, wait
        self.aliases = dict(aliases or {})


def _call(body, *, name, grid, in_specs, out_specs, out_shape, scratch_shapes, vmem_mib, args, rider=None,
          num_scalar_prefetch=0):
    params = pltpu.CompilerParams(dimension_semantics=("arbitrary",) * len(grid), vmem_limit_bytes=vmem_mib * MIB)
    if rider is None:
        return pl.pallas_call(
            body, name=name,
            grid_spec=pltpu.PrefetchScalarGridSpec(
                num_scalar_prefetch=num_scalar_prefetch, grid=grid, in_specs=in_specs, out_specs=out_specs,
                scratch_shapes=scratch_shapes),
            out_shape=out_shape, compiler_params=params)(*args)
    assert num_scalar_prefetch == 0
    n_in, n_out, n_scr = len(in_specs), len(out_specs), len(scratch_shapes)
    r_in, r_out = len(rider.arrays), len(rider.out_shapes)
    n_axes = len(grid)

    def carried(*refs):
        pos = [0]

        def take(n):
            pos[0] += n
            return refs[pos[0] - n:pos[0]]

        ins, r_ins, outs, r_outs, scr, r_scr = take(n_in), take(r_in), take(n_out), take(r_out), take(n_scr), take(len(rider.scratch))
        first = pl.program_id(0) == 0
        last = pl.program_id(0) == grid[0] - 1
        for ax in range(1, n_axes):
            first = first & (pl.program_id(ax) == 0)
            last = last & (pl.program_id(ax) == grid[ax] - 1)

        @pl.when(first)
        def _():
            rider.start(r_ins, r_outs, r_scr)

        body(*ins, *outs, *scr)

        @pl.when(last)
        def _():
            rider.wait(r_ins, r_outs, r_scr)

    res = pl.pallas_call(
        carried, name=name,
        grid=grid,
        in_specs=list(in_specs) + rider.in_specs,
        out_specs=list(out_specs) + rider.out_specs,
        out_shape=list(out_shape) + rider.out_shapes,
        scratch_shapes=list(scratch_shapes) + rider.scratch,
        input_output_aliases={n_in + i: n_out + o for i, o in rider.aliases.items()},
        compiler_params=params)(*args, *rider.arrays)
    return res[:n_out], res[n_out:]


def _run_comm(comm, name):
    n_in, n_out = len(comm.arrays), len(comm.out_shapes)

    def body(*refs):
        ins, outs, scr = refs[:n_in], refs[n_in:n_in + n_out], refs[n_in + n_out:]
        comm.start(ins, outs, scr)
        comm.wait(ins, outs, scr)

    return pl.pallas_call(
        body, name=name,
        in_specs=comm.in_specs, out_specs=comm.out_specs, out_shape=comm.out_shapes,
        scratch_shapes=comm.scratch, input_output_aliases=comm.aliases,
        compiler_params=pltpu.CompilerParams(vmem_limit_bytes=24 * MIB))(*comm.arrays)


def _place():
    return lax.axis_index("x"), lax.axis_index("y"), lax.axis_index("c")


def _peer(k):
    x, y, c = _place()
    px, py, pc = x ^ ((k >> 2) & 1), y ^ ((k >> 1) & 1), c ^ (k & 1)
    return (px, py, pc), 4 * px + 2 * py + pc


SIBLING = 1
SAME_CORE = (2, 4, 6)
EVERYONE = tuple(range(1, N_DEV))


def _gather_comm(shards, views, out_shapes, relations):
    na = len(shards)

    def copies(ins, outs, scr):
        send_sems, recv_sems, _ = scr
        _, me = _peer(0)
        out = []
        for a in range(na):
            for k in relations[a]:
                peer, theirs = _peer(k)
                send = functools.partial(
                    pltpu.make_async_remote_copy,
                    src_ref=ins[a], dst_ref=views[a](outs[a], me),
                    send_sem=send_sems.at[7 * a + k - 1], recv_sem=recv_sems.at[7 * a + k - 1],
                    device_id=peer, device_id_type=MESH)
                recv = functools.partial(
                    pltpu.make_async_remote_copy,
                    src_ref=ins[a], dst_ref=views[a](outs[a], theirs),
                    send_sem=send_sems.at[7 * a + k - 1], recv_sem=recv_sems.at[7 * a + k - 1],
                    device_id=peer, device_id_type=MESH)
                out.append((send, recv))
        return out

    def local(ins, outs, scr):
        _, me = _peer(0)
        return [pltpu.make_async_copy(ins[a], views[a](outs[a], me), scr[2].at[a]) for a in range(na)]

    def start(ins, outs, scr):
        for cp in local(ins, outs, scr):
            cp.start()
        for send, _ in copies(ins, outs, scr):
            send().start()

    def wait(ins, outs, scr):
        for _, recv in copies(ins, outs, scr):
            recv().wait_recv()
        for send, _ in copies(ins, outs, scr):
            send().wait_send()
        for cp in local(ins, outs, scr):
            cp.wait()

    return _Comm(shards, [HBM_SPEC] * na, out_shapes, [HBM_SPEC] * na,
                 [pltpu.SemaphoreType.DMA((7 * na,)), pltpu.SemaphoreType.DMA((7 * na,)),
                  pltpu.SemaphoreType.DMA((na,))], start, wait)


def _forward_comm(gathered, views):
    na = len(gathered)
    shapes = [jax.ShapeDtypeStruct(g.shape, g.dtype) for g in gathered]

    def copies(outs, scr):
        send_sems, recv_sems = scr
        sibling, _ = _peer(SIBLING)
        out = []
        for a in range(na):
            for n, k in enumerate(SAME_CORE):
                _, mine = _peer(k)
                _, theirs = _peer(k ^ SIBLING)
                send = functools.partial(
                    pltpu.make_async_remote_copy,
                    src_ref=views[a](outs[a], mine), dst_ref=views[a](outs[a], mine),
                    send_sem=send_sems.at[3 * a + n], recv_sem=recv_sems.at[3 * a + n],
                    device_id=sibling, device_id_type=MESH)
                recv = functools.partial(
                    pltpu.make_async_remote_copy,
                    src_ref=views[a](outs[a], mine), dst_ref=views[a](outs[a], theirs),
                    send_sem=send_sems.at[3 * a + n], recv_sem=recv_sems.at[3 * a + n],
                    device_id=sibling, device_id_type=MESH)
                out.append((send, recv))
        return out

    def start(ins, outs, scr):
        for send, _ in copies(outs, scr):
            send().start()

    def wait(ins, outs, scr):
        for _, recv in copies(outs, scr):
            recv().wait_recv()
        for send, _ in copies(outs, scr):
            send().wait_send()

    return _Comm(gathered, [HBM_SPEC] * na, shapes, [HBM_SPEC] * na,
                 [pltpu.SemaphoreType.DMA((3 * na,)), pltpu.SemaphoreType.DMA((3 * na,))], start, wait,
                 aliases={a: a for a in range(na)})


def _all_gather(shards, views, out_shapes, name):
    na = len(shards)
    level1 = _gather_comm(shards, views, out_shapes, [(SIBLING,) + SAME_CORE] * na)

    def body(*refs):
        ins, outs = refs[:na], refs[na:2 * na]
        send_sems, recv_sems, local_sems, fwd_send, fwd_recv = refs[2 * na:]
        sibling, _ = _peer(SIBLING)
        scr = (send_sems, recv_sems, local_sems)
        level1.start(ins, outs, scr)
        passed, landing = [], []
        for a in range(na):
            for n, k in enumerate(SAME_CORE):
                peer, mine = _peer(k)
                _, theirs = _peer(k ^ SIBLING)
                pltpu.make_async_remote_copy(
                    src_ref=ins[a], dst_ref=views[a](outs[a], mine),
                    send_sem=send_sems.at[7 * a + k - 1], recv_sem=recv_sems.at[7 * a + k - 1],
                    device_id=peer, device_id_type=MESH).wait_recv()
                fwd = pltpu.make_async_remote_copy(
                    src_ref=views[a](outs[a], mine), dst_ref=views[a](outs[a], mine),
                    send_sem=fwd_send.at[3 * a + n], recv_sem=fwd_recv.at[3 * a + n],
                    device_id=sibling, device_id_type=MESH)
                fwd.start()
                passed.append(fwd)
                landing.append(pltpu.make_async_remote_copy(
                    src_ref=views[a](outs[a], mine), dst_ref=views[a](outs[a], theirs),
                    send_sem=fwd_send.at[3 * a + n], recv_sem=fwd_recv.at[3 * a + n],
                    device_id=sibling, device_id_type=MESH))
        for a in range(na):
            _, theirs = _peer(SIBLING)
            pltpu.make_async_remote_copy(
                src_ref=ins[a], dst_ref=views[a](outs[a], theirs),
                send_sem=send_sems.at[7 * a + SIBLING - 1], recv_sem=recv_sems.at[7 * a + SIBLING - 1],
                device_id=sibling, device_id_type=MESH).wait_recv()
        for cp in landing:
            cp.wait_recv()
        for cp in passed:
            cp.wait_send()
        _, me = _peer(0)
        for a in range(na):
            for k in (SIBLING,) + SAME_CORE:
                peer, _ = _peer(k)
                pltpu.make_async_remote_copy(
                    src_ref=ins[a], dst_ref=views[a](outs[a], me),
                    send_sem=send_sems.at[7 * a + k - 1], recv_sem=recv_sems.at[7 * a + k - 1],
                    device_id=peer, device_id_type=MESH).wait_send()
            pltpu.make_async_copy(ins[a], views[a](outs[a], me), local_sems.at[a]).wait()

    return pl.pallas_call(
        body, name=name,
        in_specs=[HBM_SPEC] * na, out_specs=[HBM_SPEC] * na, out_shape=out_shapes,
        scratch_shapes=level1.scratch + [pltpu.SemaphoreType.DMA((3 * na,)), pltpu.SemaphoreType.DMA((3 * na,))],
    )(*shards)


def _sibling_comm(grads):
    na = len(grads)
    shapes = [jax.ShapeDtypeStruct(g.shape[:2] + g.shape[3:], g.dtype) for g in grads]

    def copies(ins, outs, scr):
        x, y, c = _place()
        return [pltpu.make_async_remote_copy(
            src_ref=ins[a].at[:, :, 1 - c], dst_ref=outs[a],
            send_sem=scr[0].at[a], recv_sem=scr[1].at[a],
            device_id=(x, y, 1 - c), device_id_type=MESH) for a in range(na)]

    def start(ins, outs, scr):
        for cp in copies(ins, outs, scr):
            cp.start()

    def wait(ins, outs, scr):
        for cp in copies(ins, outs, scr):
            cp.wait()

    return _Comm(grads, [HBM_SPEC] * na, shapes, [HBM_SPEC] * na,
                 [pltpu.SemaphoreType.DMA((na,)), pltpu.SemaphoreType.DMA((na,))], start, wait)


def _chips_comm(sums):
    na = len(sums)
    shapes = [jax.ShapeDtypeStruct(s.shape, s.dtype) for s in sums]

    def copies(ins, outs, scr):
        x, y, c = _place()
        mine = 2 * x + y
        out = []
        for a in range(na):
            for n, k in enumerate(SAME_CORE):
                (px, py, pc), _ = _peer(k)
                theirs = 2 * px + py
                send = functools.partial(
                    pltpu.make_async_remote_copy,
                    src_ref=ins[a].at[:, theirs], dst_ref=outs[a].at[:, mine],
                    send_sem=scr[0].at[3 * a + n], recv_sem=scr[1].at[3 * a + n],
                    device_id=(px, py, pc), device_id_type=MESH)
                recv = functools.partial(
                    pltpu.make_async_remote_copy,
                    src_ref=ins[a].at[:, mine], dst_ref=outs[a].at[:, theirs],
                    send_sem=scr[0].at[3 * a + n], recv_sem=scr[1].at[3 * a + n],
                    device_id=(px, py, pc), device_id_type=MESH)
                out.append((send, recv))
        return out

    def local(ins, outs, scr):
        x, y, _ = _place()
        mine = 2 * x + y
        return [pltpu.make_async_copy(ins[a].at[:, mine], outs[a].at[:, mine], scr[2].at[a]) for a in range(na)]

    def start(ins, outs, scr):
        for cp in local(ins, outs, scr):
            cp.start()
        for send, _ in copies(ins, outs, scr):
            send().start()

    def wait(ins, outs, scr):
        for _, recv in copies(ins, outs, scr):
            recv().wait_recv()
        for send, _ in copies(ins, outs, scr):
            send().wait_send()
        for cp in local(ins, outs, scr):
            cp.wait()

    return _Comm(sums, [HBM_SPEC] * na, shapes, [HBM_SPEC] * na,
                 [pltpu.SemaphoreType.DMA((3 * na,)), pltpu.SemaphoreType.DMA((3 * na,)),
                  pltpu.SemaphoreType.DMA((na,))], start, wait)


def _small_sum_comm(s):
    R, C = s.shape

    def copies(ins, scr):
        buf, send_sems, recv_sems = scr
        _, me = _peer(0)
        out = []
        for k in EVERYONE:
            peer, theirs = _peer(k)
            send = functools.partial(
                    pltpu.make_async_remote_copy,
                src_ref=ins[0], dst_ref=buf.at[me], send_sem=send_sems.at[k - 1], recv_sem=recv_sems.at[k - 1],
                device_id=peer, device_id_type=MESH)
            recv = functools.partial(
                    pltpu.make_async_remote_copy,
                src_ref=ins[0], dst_ref=buf.at[theirs], send_sem=send_sems.at[k - 1], recv_sem=recv_sems.at[k - 1],
                device_id=peer, device_id_type=MESH)
            out.append((send, recv))
        return out

    def start(ins, outs, scr):
        _, me = _peer(0)
        scr[0][me] = ins[0][...]
        for send, _ in copies(ins, scr):
            send().start()

    def wait(ins, outs, scr):
        for _, recv in copies(ins, scr):
            recv().wait_recv()
        for send, _ in copies(ins, scr):
            send().wait_send()
        acc = scr[0][0]
        for j in range(1, N_DEV):
            acc = acc + scr[0][j]
        outs[0][...] = acc

    return _Comm([s], [VMEM_SPEC], [jax.ShapeDtypeStruct((R, C), F32)], [VMEM_SPEC],
                 [pltpu.VMEM((N_DEV, R, C), F32), pltpu.SemaphoreType.DMA((N_DEV - 1,)),
                  pltpu.SemaphoreType.DMA((N_DEV - 1,))], start, wait)


def _load_weights(w_hbm, w_vmem, sem):
    @pl.when(pl.program_id(0) == 0)
    def _():
        copies = [pltpu.make_async_copy(w_hbm.at[k], w_vmem.at[k], sem.at[k]) for k in range(3)]
        for cp in copies:
            cp.start()
        for cp in copies:
            cp.wait()


def _ffn_fwd(x, g, w3, tm, cf, name, rider=None):
    T, D = x.shape
    F = w3.shape[1]

    def body(x_ref, g_ref, w_hbm, xo_ref, h_ref, gate_ref, up_ref, wv, act_ref, sem):
        _load_weights(w_hbm, wv, sem)
        xhat, _ = _rms_stats(x_ref[...])
        hb = (xhat * g_ref[...]).astype(BF16)
        h_ref[...] = hb
        for lo in range(0, F, cf):
            gate = _dot(hb, wv[0, lo:lo + cf, :], NT)
            up = _dot(hb, wv[1, lo:lo + cf, :], NT)
            gate_ref[:, lo:lo + cf] = gate.astype(BF16)
            up_ref[:, lo:lo + cf] = up.astype(BF16)
            act_ref[:, lo:lo + cf] = (gate * _sigmoid(gate) * up).astype(BF16)
        xo_ref[...] = x_ref[...] + FFN_RES * _dot(act_ref[...], wv[2], NN)

    row = pl.BlockSpec((tm, D), lambda i: (i, 0))
    hid = pl.BlockSpec((tm, F), lambda i: (i, 0))
    return _call(
        body, name=name, grid=(T // tm,),
        in_specs=[row, pl.BlockSpec((1, D), lambda i: (0, 0)), HBM_SPEC],
        out_specs=[row, row, hid, hid],
        out_shape=[jax.ShapeDtypeStruct((T, D), F32), jax.ShapeDtypeStruct((T, D), BF16),
                   jax.ShapeDtypeStruct((T, F), BF16), jax.ShapeDtypeStruct((T, F), BF16)],
        scratch_shapes=[pltpu.VMEM((3, F, D), BF16), pltpu.VMEM((tm, F), BF16), pltpu.SemaphoreType.DMA((3,))],
        vmem_mib=48, args=(x, g, w3), rider=rider)


def _ffn_dgrad(dout, x, g, gate, up, w3, tm, cf, name, rider=None):
    T, D = x.shape
    F = w3.shape[1]

    def body(do_ref, x_ref, g_ref, gate_ref, up_ref, w_hbm, dx_ref, dgate_ref, dup_ref, dg_ref, wv, sem):
        _load_weights(w_hbm, wv, sem)

        @pl.when(pl.program_id(0) == 0)
        def _():
            dg_ref[...] = jnp.zeros_like(dg_ref)

        dob = (FFN_RES * do_ref[...]).astype(BF16)
        for lo in range(0, F, cf):
            dact = _dot(dob, wv[2, lo:lo + cf, :], NT)
            gv = gate_ref[:, lo:lo + cf].astype(F32)
            uv = up_ref[:, lo:lo + cf].astype(F32)
            sig = _sigmoid(gv)
            dup_ref[:, lo:lo + cf] = (dact * (gv * sig)).astype(BF16)
            dgate_ref[:, lo:lo + cf] = (dact * uv * (sig * (1.0 + gv * (1.0 - sig)))).astype(BF16)
        dh = _dot(dgate_ref[...], wv[0], NN) + _dot(dup_ref[...], wv[1], NN)
        xhat, rstd = _rms_stats(x_ref[...])
        dx, dg = _rms_bwd(xhat, rstd, g_ref[...], dh)
        dx_ref[...] = do_ref[...] + dx
        dg_ref[...] += dg

    row = pl.BlockSpec((tm, D), lambda i: (i, 0))
    hid = pl.BlockSpec((tm, F), lambda i: (i, 0))
    vec = pl.BlockSpec((1, D), lambda i: (0, 0))
    return _call(
        body, name=name, grid=(T // tm,),
        in_specs=[row, row, vec, hid, hid, HBM_SPEC],
        out_specs=[row, hid, hid, vec],
        out_shape=[jax.ShapeDtypeStruct((T, D), F32), jax.ShapeDtypeStruct((T, F), BF16),
                   jax.ShapeDtypeStruct((T, F), BF16), jax.ShapeDtypeStruct((1, D), F32)],
        scratch_shapes=[pltpu.VMEM((3, F, D), BF16), pltpu.SemaphoreType.DMA((3,))],
        vmem_mib=52, args=(dout, x, g, gate, up, w3), rider=rider)


def _wgrad(lhs, rhs, tm, tf, name, swiglu=False, rider=None):
    T, F = lhs[0].shape
    D = rhs.shape[1]
    K = 1 if swiglu else len(lhs)
    nl = len(lhs)

    def body(*refs):
        lhs_refs, rhs_ref, dw_ref, accs = refs[:nl], refs[nl], refs[nl + 1], refs[nl + 2:]
        i = pl.program_id(1)

        @pl.when(i == 0)
        def _():
            for acc in accs:
                acc[...] = jnp.zeros_like(acc)

        rv = rhs_ref[...]
        if swiglu:
            gv = lhs_refs[0][...].astype(F32)
            left = [(gv * _sigmoid(gv) * lhs_refs[1][...].astype(F32)).astype(BF16)]
        else:
            left = [r[...] for r in lhs_refs]
        for acc, lv in zip(accs, left):
            acc[...] += _dot(lv, rv, TN)

        @pl.when(i == pl.num_programs(1) - 1)
        def _():
            for k, acc in enumerate(accs):
                dw_ref[k] = acc[...].astype(BF16)

    hid = pl.BlockSpec((tm, tf), lambda f, i: (i, f))
    return _call(
        body, name=name, grid=(F // tf, T // tm),
        in_specs=[hid] * nl + [pl.BlockSpec((tm, D), lambda f, i: (i, 0))],
        out_specs=[pl.BlockSpec((K, tf, D), lambda f, i: (0, f, 0))],
        out_shape=[jax.ShapeDtypeStruct((K, F, D), BF16)],
        scratch_shapes=[pltpu.VMEM((tf, D), F32)] * K,
        vmem_mib=56, args=(*lhs, rhs), rider=rider)


def _mix_in(x1, g, w_in_t, tm, name):
    T, D = x1.shape
    Z = w_in_t.shape[0]

    def body(x_ref, g_ref, w_ref, z_ref):
        xhat, _ = _rms_stats(x_ref[...])
        z_ref[...] = _dot((xhat * g_ref[...]).astype(BF16), w_ref[...], NT)

    return _call(
        body, name=name, grid=(T // tm,),
        in_specs=[pl.BlockSpec((tm, D), lambda i: (i, 0)), pl.BlockSpec((1, D), lambda i: (0, 0)),
                  pl.BlockSpec((Z, D), lambda i: (0, 0))],
        out_specs=[pl.BlockSpec((tm, Z), lambda i: (i, 0))],
        out_shape=[jax.ShapeDtypeStruct((T, Z), F32)],
        scratch_shapes=[], vmem_mib=40, args=(x1, g, w_in_t))[0]


def _lru_gates(xr, bda_ref, bdx_ref, vec_ref):
    xrb = xr.astype(BF16)
    r = _sigmoid(_dot(xrb, bda_ref[...], NN) + vec_ref[V_BA:V_BA + 1, :])
    ig = _sigmoid(_dot(xrb, bdx_ref[...], NN) + vec_ref[V_BX:V_BX + 1, :])
    sp = _softplus_neg(vec_ref[V_LAM:V_LAM + 1, :])
    log_a = (-LRU_C * sp) * r
    a = jnp.exp(log_a)
    mult = jnp.sqrt(_neg_expm1(2.0 * log_a))
    return xrb, r, ig, sp, a, mult


def _layernorm_stats(u1):
    xc = u1 - jnp.mean(u1, axis=-1, keepdims=True)
    rs = lax.rsqrt(jnp.mean(xc * xc, axis=-1, keepdims=True) + LN_EPS)
    return xc * rs, rs


def _mix_core_fwd(z, x1, w_out, bda, bdx, cw, lw, vec, tm, name, rider=None):
    T, D = x1.shape
    W = cw.shape[1]
    assert tm >= CONV_HALO and z.shape[1] == 4 * W

    def body(z_ref, x1_ref, wo_ref, bda_ref, bdx_ref, cw_ref, lw_ref, vec_ref,
             x2_ref, mix_ref, u1_ref, xr_ref, hst_ref, ubuf, rbuf, hc):
        @pl.when(pl.program_id(0) == 0)
        def _():
            ubuf[0:CONV_HALO, :] = jnp.zeros((CONV_HALO, W), F32)
            rbuf[0:LRU_HALO, :] = jnp.zeros((LRU_HALO, W), F32)
            hc[...] = jnp.zeros_like(hc)

        ubuf[CONV_HALO:CONV_HALO + tm, :] = z_ref[:, 0:W] * _sigmoid(z_ref[:, W:2 * W])
        u1 = jnp.zeros((tm, W), F32) + vec_ref[V_CB:V_CB + 1, :]
        base = CONV_HALO - (CONV_K - 1)
        for off, win in _row_windows(ubuf, tm, range(base, base + CONV_K)):
            u1 = u1 + cw_ref[off - base:off - base + 1, :] * win
        ubuf[0:CONV_HALO, :] = ubuf[tm:tm + CONV_HALO, :]
        u1_ref[...] = u1
        xh, _ = _layernorm_stats(u1)
        u2 = xh * vec_ref[V_LNG:V_LNG + 1, :] + vec_ref[V_LNB:V_LNB + 1, :]
        ub = (u2 * _sigmoid(u2)).astype(BF16)
        mix_ref[:, 0:W] = ub

        rbuf[LRU_HALO:LRU_HALO + tm, :] = z_ref[:, 2 * W:3 * W]
        xr = jnp.zeros((tm, W), F32) + vec_ref[V_LCB:V_LCB + 1, :]
        for k in range(LRU_K):
            off = LRU_HALO - (LRU_K - 1) + k
            xr = xr + lw_ref[k:k + 1, :] * rbuf[off:off + tm, :]
        rbuf[0:LRU_HALO, :] = rbuf[tm:tm + LRU_HALO, :]
        xr_ref[...] = xr
        _, _, ig, _, a, mult = _lru_gates(xr, bda_ref, bdx_ref, vec_ref)
        av, bv = a, mult * (ig * xr)
        row = lax.broadcasted_iota(jnp.int32, (tm, W), 0)
        s = 1
        while s < tm:
            keep = row >= s
            bv = jnp.where(keep, av * pltpu.roll(bv, s, 0) + bv, bv)
            av = jnp.where(keep, av * pltpu.roll(av, s, 0), av)
            s *= 2
        h = av * hc[0:1, :] + bv
        hc[0:1, :] = h[tm - 1:tm, :]
        hst_ref[...] = h
        gl, _ = _gelu_parts(z_ref[:, 3 * W:4 * W])
        yb = (h * gl).astype(BF16)
        mix_ref[:, W:2 * W] = yb

        x2_ref[...] = x1_ref[...] + _dot(ub, wo_ref[0:W, :], NN) + _dot(yb, wo_ref[W:2 * W, :], NN)

    full = lambda a: pl.BlockSpec(a.shape, lambda i: (0,) * a.ndim)
    tile = lambda n: pl.BlockSpec((tm, n), lambda i: (i, 0))
    return _call(
        body, name=name, grid=(T // tm,),
        in_specs=[tile(4 * W), tile(D), full(w_out), full(bda), full(bdx), full(cw), full(lw), full(vec)],
        out_specs=[tile(D), tile(2 * W), tile(W), tile(W), tile(W)],
        out_shape=[jax.ShapeDtypeStruct((T, D), F32), jax.ShapeDtypeStruct((T, 2 * W), BF16),
                   jax.ShapeDtypeStruct((T, W), F32), jax.ShapeDtypeStruct((T, W), F32),
                   jax.ShapeDtypeStruct((T, W), F32)],
        scratch_shapes=[pltpu.VMEM((tm + CONV_HALO, W), F32), pltpu.VMEM((tm + LRU_HALO, W), F32),
                        pltpu.VMEM((8, W), F32)],
        vmem_mib=48, args=(z, x1, w_out, bda, bdx, cw, lw, vec), rider=rider)


def _mix_core_bwd(dx2, z, u1, xr, hst, w_out, bda, bdx, cw, lw, vec, tm, name, rider=None):
    T, D = dx2.shape
    W = cw.shape[1]
    nt = T // tm
    assert tm >= CONV_HALO and tm % CONV_HALO == 0

    def body(dx_ref, z_ref, zh_ref, u1_ref, xr_ref, h_ref, hh_ref, wo_ref, bda_ref, bdx_ref, cw_ref, lw_ref, vec_ref,
             dz_ref, sg_ref, dbda_ref, dbdx_ref, u0buf, du1buf, rxbuf, dxrbuf, gc, spacc):
        i = pl.program_id(0)
        first = i == nt - 1
        row = lax.broadcasted_iota(jnp.int32, (tm, W), 0)

        @pl.when(i == 0)
        def _():
            sg_ref[...] = jnp.zeros_like(sg_ref)
            dbda_ref[...] = jnp.zeros_like(dbda_ref)
            dbdx_ref[...] = jnp.zeros_like(dbdx_ref)
            du1buf[tm:tm + CONV_HALO, :] = jnp.zeros((CONV_HALO, W), F32)
            dxrbuf[tm:tm + LRU_HALO, :] = jnp.zeros((LRU_HALO, W), F32)
            gc[...] = jnp.zeros_like(gc)
            spacc[...] = jnp.zeros_like(spacc)

        def accum(r, val):
            sg_ref[r:r + 1, :] += jnp.sum(val, axis=0, keepdims=True)

        dmix = _dot(dx_ref[...].astype(BF16), wo_ref[...], NT)
        d_u = dmix[:, 0:W]
        d_yr = dmix[:, W:2 * W]

        xh, rs = _layernorm_stats(u1_ref[...])
        ln_g = vec_ref[V_LNG:V_LNG + 1, :]
        u2 = xh * ln_g + vec_ref[V_LNB:V_LNB + 1, :]
        s2 = _sigmoid(u2)
        d_u2 = d_u * (s2 * (1.0 + u2 * (1.0 - s2)))
        accum(G_LNG, d_u2 * xh)
        accum(G_LNB, d_u2)
        d_xh = d_u2 * ln_g
        d_u1 = rs * (d_xh - jnp.mean(d_xh, axis=-1, keepdims=True)
                     - xh * jnp.mean(d_xh * xh, axis=-1, keepdims=True))
        accum(G_CB, d_u1)
        halo_on = jnp.where(first, 0.0, 1.0)
        u0buf[0:CONV_HALO, :] = halo_on * (zh_ref[:, 0:W] * _sigmoid(zh_ref[:, W:2 * W]))
        cv = z_ref[:, 0:W]
        sgc = _sigmoid(z_ref[:, W:2 * W])
        u0buf[CONV_HALO:CONV_HALO + tm, :] = cv * sgc
        du1buf[0:tm, :] = d_u1
        base = CONV_HALO - (CONV_K - 1)
        for off, win in _row_windows(u0buf, tm, range(base, base + CONV_K)):
            accum(G_CW + off - base, d_u1 * win)
        d_u0 = jnp.zeros((tm, W), F32)
        for off, win in _row_windows(du1buf, tm, range(0, CONV_K)):
            d_u0 = d_u0 + cw_ref[CONV_K - 1 - off:CONV_K - off, :] * win
        du1buf[tm:tm + CONV_HALO, :] = du1buf[0:CONV_HALO, :]
        dz_ref[:, 0:W] = (d_u0 * sgc).astype(BF16)
        dz_ref[:, W:2 * W] = (d_u0 * cv * (sgc * (1.0 - sgc))).astype(BF16)

        xrv = xr_ref[...]
        xrb, r, ig, sp, a, mult = _lru_gates(xrv, bda_ref, bdx_ref, vec_ref)
        h = h_ref[...]
        gl, dgl = _gelu_parts(z_ref[:, 3 * W:4 * W])
        dz_ref[:, 3 * W:4 * W] = (d_yr * h * dgl).astype(BF16)
        bv = d_yr * gl + jnp.where(row == tm - 1, gc[0:1, :], 0.0)
        av = pltpu.roll(a, tm - 1, 0)
        s = 1
        while s < tm:
            keep = row < tm - s
            bv = jnp.where(keep, av * pltpu.roll(bv, tm - s, 0) + bv, bv)
            av = jnp.where(keep, av * pltpu.roll(av, tm - s, 0), av)
            s *= 2
        g = bv
        gc[0:1, :] = a[0:1, :] * g[0:1, :]
        hprev = jnp.where(row == 0, halo_on * hh_ref[LRU_HALO - 1:LRU_HALO, :], pltpu.roll(h, 1, 0))
        d_log_a = (g * hprev) * a - (g * ig * xrv) * (a * a) / mult
        d_ig = g * mult * xrv
        d_xr = g * mult * ig
        spacc[0:1, :] += jnp.sum(d_log_a * r, axis=0, keepdims=True)
        d_pa32 = (d_log_a * (-LRU_C * sp)) * (r * (1.0 - r))
        d_px32 = d_ig * (ig * (1.0 - ig))
        accum(G_BA, d_pa32)
        accum(G_BX, d_px32)
        d_pa = d_pa32.astype(BF16)
        d_px = d_px32.astype(BF16)
        d_xr = d_xr + _dot(d_pa, bda_ref[...], NT) + _dot(d_px, bdx_ref[...], NT)
        dbda_ref[...] += _dot(xrb, d_pa, TN)
        dbdx_ref[...] += _dot(xrb, d_px, TN)
        accum(G_LCB, d_xr)
        rxbuf[0:LRU_HALO, :] = halo_on * zh_ref[CONV_HALO - LRU_HALO:CONV_HALO, 2 * W:3 * W]
        rxbuf[LRU_HALO:LRU_HALO + tm, :] = z_ref[:, 2 * W:3 * W]
        dxrbuf[0:tm, :] = d_xr
        d_rx = jnp.zeros((tm, W), F32)
        for k in range(LRU_K):
            off = LRU_HALO - (LRU_K - 1) + k
            accum(G_LW + k, d_xr * rxbuf[off:off + tm, :])
            d_rx = d_rx + lw_ref[k:k + 1, :] * dxrbuf[LRU_K - 1 - k:LRU_K - 1 - k + tm, :]
        dxrbuf[tm:tm + LRU_HALO, :] = dxrbuf[0:LRU_HALO, :]
        dz_ref[:, 2 * W:3 * W] = d_rx.astype(BF16)

        @pl.when(first)
        def _():
            lam = vec_ref[V_LAM:V_LAM + 1, :]
            sg_ref[G_LAM:G_LAM + 1, :] = LRU_C * _sigmoid(-lam) * spacc[0:1, :]

    full = lambda a: pl.BlockSpec(a.shape, lambda i: (0,) * a.ndim)
    tile = lambda n: pl.BlockSpec((tm, n), lambda i: (nt - 1 - i, 0))
    halo = lambda rows, n: pl.BlockSpec(
        (rows, n), lambda i: (jnp.maximum((nt - 1 - i) * (tm // rows) - 1, 0), 0))
    return _call(
        body, name=name, grid=(nt,),
        in_specs=[tile(D), tile(4 * W), halo(CONV_HALO, 4 * W), tile(W), tile(W), tile(W), halo(LRU_HALO, W),
                  full(w_out), full(bda), full(bdx), full(cw), full(lw), full(vec)],
        out_specs=[tile(4 * W), pl.BlockSpec((G_ROWS, W), lambda i: (0, 0)),
                   pl.BlockSpec((W, W), lambda i: (0, 0)), pl.BlockSpec((W, W), lambda i: (0, 0))],
        out_shape=[jax.ShapeDtypeStruct((T, 4 * W), BF16), jax.ShapeDtypeStruct((G_ROWS, W), F32),
                   jax.ShapeDtypeStruct((W, W), F32), jax.ShapeDtypeStruct((W, W), F32)],
        scratch_shapes=[pltpu.VMEM((tm + CONV_HALO, W), F32), pltpu.VMEM((tm + CONV_HALO, W), F32),
                        pltpu.VMEM((tm + LRU_HALO, W), F32), pltpu.VMEM((tm + LRU_HALO, W), F32),
                        pltpu.VMEM((8, W), F32), pltpu.VMEM((8, W), F32)],
        vmem_mib=56, args=(dx2, z, z, u1, xr, hst, hst, w_out, bda, bdx, cw, lw, vec), rider=rider)


def _mix_in_bwd(dz, x1, dx2, mix, g, w_in_t, tm, name, rider=None):
    T, D = x1.shape
    Z = w_in_t.shape[0]
    M = mix.shape[1]

    def body(dz_ref, x_ref, dx2_ref, mix_ref, g_ref, w_ref, dx1_ref, dg_ref, dwi_ref, dwo_ref, dob_ref, ai_ref, ao_ref):
        i = pl.program_id(0)

        @pl.when(i == 0)
        def _():
            dg_ref[...] = jnp.zeros_like(dg_ref)
            ai_ref[...] = jnp.zeros_like(ai_ref)
            ao_ref[...] = jnp.zeros_like(ao_ref)

        xhat, rstd = _rms_stats(x_ref[...])
        gv = g_ref[...]
        hb = (xhat * gv).astype(BF16)
        dzb = dz_ref[...]
        dx, dg = _rms_bwd(xhat, rstd, gv, _dot(dzb, w_ref[...], NN))
        dx2 = dx2_ref[...]
        dx1 = dx2 + dx
        dx1_ref[...] = dx1
        dob_ref[...] = (FFN_RES * dx1).astype(BF16)
        dg_ref[...] += dg
        ai_ref[...] += _dot(dzb, hb, TN)
        ao_ref[...] += _dot(mix_ref[...], dx2.astype(BF16), TN)

        @pl.when(i == pl.num_programs(0) - 1)
        def _():
            dwi_ref[...] = ai_ref[...].astype(BF16)
            dwo_ref[...] = ao_ref[...].astype(BF16)

    tile = lambda n: pl.BlockSpec((tm, n), lambda i: (i, 0))
    const = lambda r, c: pl.BlockSpec((r, c), lambda i: (0, 0))
    return _call(
        body, name=name, grid=(T // tm,),
        in_specs=[tile(Z), tile(D), tile(D), tile(M), const(1, D), const(Z, D)],
        out_specs=[tile(D), const(1, D), const(Z, D), const(M, D), tile(D)],
        out_shape=[jax.ShapeDtypeStruct((T, D), F32), jax.ShapeDtypeStruct((1, D), F32),
                   jax.ShapeDtypeStruct((Z, D), BF16), jax.ShapeDtypeStruct((M, D), BF16),
                   jax.ShapeDtypeStruct((T, D), BF16)],
        scratch_shapes=[pltpu.VMEM((Z, D), F32), pltpu.VMEM((M, D), F32)],
        vmem_mib=56, args=(dz, x1, dx2, mix, g, w_in_t), rider=rider)


def _final_loss(x3, g, target, tm, name):
    T, D = x3.shape

    def body(x_ref, g_ref, t_ref, dx_ref, dg_ref, loss_ref, dob_ref):
        @pl.when(pl.program_id(0) == 0)
        def _():
            dg_ref[...] = jnp.zeros_like(dg_ref)
            loss_ref[...] = jnp.zeros_like(loss_ref)

        xhat, rstd = _rms_stats(x_ref[...])
        gv = g_ref[...]
        err = xhat * gv - t_ref[...]
        loss_ref[...] += (0.5 / D) * jnp.sum(err * err)
        dx, dg = _rms_bwd(xhat, rstd, gv, err * (1.0 / D))
        dx_ref[...] = dx
        dob_ref[...] = (FFN_RES * dx).astype(BF16)
        dg_ref[...] += dg

    tile = pl.BlockSpec((tm, D), lambda i: (i, 0))
    return _call(
        body, name=name, grid=(T // tm,),
        in_specs=[tile, pl.BlockSpec((1, D), lambda i: (0, 0)), tile],
        out_specs=[tile, pl.BlockSpec((1, D), lambda i: (0, 0)), pl.BlockSpec((1, 128), lambda i: (0, 0)), tile],
        out_shape=[jax.ShapeDtypeStruct((T, D), F32), jax.ShapeDtypeStruct((1, D), F32),
                   jax.ShapeDtypeStruct((1, 128), F32), jax.ShapeDtypeStruct((T, D), BF16)],
        scratch_shapes=[], vmem_mib=32, args=(x3, g, target))


def _pair_add(full, recv, name):
    K, _, _, rows, D = full.shape

    def body(c_ref, a_ref, b_ref, o_ref):
        o_ref[...] = (a_ref[...].astype(F32) + b_ref[...].astype(F32)).astype(BF16)

    c = lax.axis_index("c").astype(jnp.int32).reshape((1,))
    return _call(
        body, name=name, grid=(K, N_CHIP), num_scalar_prefetch=1,
        in_specs=[pl.BlockSpec((None, None, None, rows, D), lambda k, q, c_ref: (k, q, c_ref[0], 0, 0)),
                  pl.BlockSpec((None, None, rows, D), lambda k, q, c_ref: (k, q, 0, 0))],
        out_specs=pl.BlockSpec((None, None, rows, D), lambda k, q, c_ref: (k, q, 0, 0)),
        out_shape=jax.ShapeDtypeStruct(recv.shape, BF16),
        scratch_shapes=[], vmem_mib=16, args=(c, full, recv))


def _chip_add(parts, name):
    K, _, rows, D = parts.shape

    def body(p_ref, o_ref):
        acc = p_ref[0].astype(F32)
        for q in range(1, N_CHIP):
            acc = acc + p_ref[q].astype(F32)
        o_ref[...] = acc

    return _call(
        body, name=name, grid=(K,),
        in_specs=[pl.BlockSpec((None, N_CHIP, rows, D), lambda k: (k, 0, 0, 0))],
        out_specs=pl.BlockSpec((None, rows, D), lambda k: (k, 0, 0)),
        out_shape=jax.ShapeDtypeStruct((K, rows, D), F32),
        scratch_shapes=[], vmem_mib=24, args=(parts,))


def _adamw(w, g, m, v, name):
    def body(w_ref, g_ref, m_ref, v_ref, d_ref, mo_ref, vo_ref):
        gv = g_ref[...]
        m2 = ADAM_B1 * m_ref[...] + (1.0 - ADAM_B1) * gv
        v2 = ADAM_B2 * v_ref[...] + (1.0 - ADAM_B2) * (gv * gv)
        m_hat = m2 / (1.0 - ADAM_B1 ** ADAM_STEP)
        v_hat = v2 / (1.0 - ADAM_B2 ** ADAM_STEP)
        d_ref[...] = -ADAM_LR * (m_hat / (jnp.sqrt(v_hat) + ADAM_EPS) + ADAM_WD * w_ref[...])
        mo_ref[...] = m2
        vo_ref[...] = v2

    shape = jax.ShapeDtypeStruct(w.shape, F32)
    return pl.pallas_call(
        body, name=name,
        in_specs=[VMEM_SPEC] * 4, out_specs=[VMEM_SPEC] * 3, out_shape=[shape] * 3,
        compiler_params=pltpu.CompilerParams(vmem_limit_bytes=32 * MIB),
    )(w, g, m, v)


def _block_diag(w):
    h, d, _ = w.shape
    out = jnp.zeros((h * d, h * d), w.dtype)
    for k in range(h):
        out = lax.dynamic_update_slice(out, w[k], (k * d, k * d))
    return out


def _diag_blocks(m, h):
    d = m.shape[0] // h
    return jnp.stack([m[k * d:(k + 1) * d, k * d:(k + 1) * d] for k in range(h)])


def _reduce_level1(full, tag):
    got = _run_comm(_sibling_comm(full), "rs_sibling_" + tag)
    return [_pair_add(a, b, "rs_pair_add_%s%d" % (tag, n)) for n, (a, b) in enumerate(zip(full, got))]


def kernel(x, ffn1_norm, ffn1_w_gate, ffn1_w_up, ffn1_w_down, mix_norm, w_in, conv_dw, conv_dw_bias, conv_ln_g, conv_ln_b, lru_conv_w, lru_conv_b, lru_w_a, lru_b_a, lru_w_x, lru_b_x, lru_lambda, w_out, ffn2_norm, ffn2_w_gate, ffn2_w_up, ffn2_w_down, final_norm, loss_target, m_ffn1_norm, m_ffn1_w_gate, m_ffn1_w_up, m_ffn1_w_down, m_mix_norm, m_w_in, m_conv_dw, m_conv_dw_bias, m_conv_ln_g, m_conv_ln_b, m_lru_conv_w, m_lru_conv_b, m_lru_w_a, m_lru_b_a, m_lru_w_x, m_lru_b_x, m_lru_lambda, m_w_out, m_ffn2_norm, m_ffn2_w_gate, m_ffn2_w_up, m_ffn2_w_down, m_final_norm, v_ffn1_norm, v_ffn1_w_gate, v_ffn1_w_up, v_ffn1_w_down, v_mix_norm, v_w_in, v_conv_dw, v_conv_dw_bias, v_conv_ln_g, v_conv_ln_b, v_lru_conv_w, v_lru_conv_b, v_lru_w_a, v_lru_b_a, v_lru_w_x, v_lru_b_x, v_lru_lambda, v_w_out, v_ffn2_norm, v_ffn2_w_gate, v_ffn2_w_up, v_ffn2_w_down, v_final_norm):
    T, D = x.shape[1], x.shape[2]
    F = ffn1_w_down.shape[0] * N_DEV
    rf = ffn1_w_down.shape[0]
    ri = w_in.shape[1]
    ro = w_out.shape[0]
    W = conv_dw_bias.shape[0]
    wc = conv_dw.shape[1]
    H = lru_w_a.shape[0]
    xs = x.reshape(T, D)
    tgt = loss_target.reshape(T, D)
    tm_ffn = min(256, T)
    cf = 256
    tm_w = min(512, T)
    tm_mix = min(256, T)
    tf_w = F // 2
    row = lambda v: v.reshape(1, -1)
    by_owner = lambda a, rows: a.reshape(a.shape[0], N_CHIP, 2, rows, D)

    p3a = jnp.stack([ffn1_w_gate.T, ffn1_w_up.T, ffn1_w_down]).astype(BF16)
    p3b = jnp.stack([ffn2_w_gate.T, ffn2_w_up.T, ffn2_w_down]).astype(BF16)
    p_in = w_in.T.astype(BF16)
    p_out = w_out.astype(BF16)
    p_cw = jnp.concatenate([conv_dw, lru_conv_w, jnp.zeros((40 - CONV_K - LRU_K, wc), F32)], axis=0)
    stacked = lambda r, j: r.at[:, j]
    plain = lambda r, j: r.at[j]
    g3_shape = jax.ShapeDtypeStruct((3, N_DEV, rf, D), BF16)
    (g3a,) = _all_gather([p3a], [stacked], [g3_shape], "ag_ffn1")
    w3a = g3a.reshape(3, F, D)
    bda = _block_diag(lru_w_a).astype(BF16)
    bdx = _block_diag(lru_w_x).astype(BF16)
    vec = jnp.concatenate([jnp.stack([conv_dw_bias, conv_ln_g, conv_ln_b, lru_conv_b, lru_b_a, lru_b_x, lru_lambda]),
                           jnp.zeros((9, W), F32)], axis=0)

    gather_rest = _gather_comm(
        [p3b, p_in, p_out, p_cw], [stacked, plain, plain, plain],
        [g3_shape, jax.ShapeDtypeStruct((N_DEV, ri, D), BF16), jax.ShapeDtypeStruct((N_DEV, ro, D), BF16),
         jax.ShapeDtypeStruct((N_DEV, 40, wc), F32)],
        [(SIBLING,) + SAME_CORE, EVERYONE, EVERYONE, EVERYONE])
    (x1, h1, gate1, up1), (g3b_half, g_in, g_out, g_cw) = _ffn_fwd(
        xs, row(ffn1_norm), w3a, tm_ffn, cf, "ffn1_fwd", rider=gather_rest)
    w_in_t = g_in.reshape(N_DEV * ri, D)
    w_out_f = g_out.reshape(N_DEV * ro, D)
    cw_all = jnp.transpose(g_cw, (1, 0, 2)).reshape(40, N_DEV * wc)
    cw = cw_all[0:CONV_K]
    lw = cw_all[CONV_K:CONV_K + LRU_K]
    z = _mix_in(x1, row(mix_norm), w_in_t, tm_w, "mix_in")
    (x2, mix, u1, xr, hst), (g3b,) = _mix_core_fwd(
        z, x1, w_out_f, bda, bdx, cw, lw, vec, tm_mix, "mix_core_fwd", rider=_forward_comm([g3b_half], [stacked]))
    w3b = g3b.reshape(3, F, D)
    x3, h3, gate2, up2 = _ffn_fwd(x2, row(ffn2_norm), w3b, tm_ffn, cf, "ffn2_fwd")
    dx3, d_final_norm, loss_part, dob2 = _final_loss(x3, row(final_norm), tgt, tm_w, "final_loss")

    dx2, dgate2, dup2, d_ffn2_norm = _ffn_dgrad(dx3, x2, row(ffn2_norm), gate2, up2, w3b, tm_ffn, cf, "ffn2_dgrad")
    (dw_gu2,) = _wgrad([dgate2, dup2], h3, tm_w, tf_w, "ffn2_wgrad_gu")
    (dw_d2,) = _wgrad([gate2, up2], dob2, tm_w, tf_w, "ffn2_wgrad_d", swiglu=True)
    sums_f2 = _reduce_level1([by_owner(dw_gu2, rf), by_owner(dw_d2, rf)], "f2")
    (dz, sg, dbda, dbdx), parts_f2 = _mix_core_bwd(
        dx2, z, u1, xr, hst, w_out_f, bda, bdx, cw, lw, vec, tm_mix, "mix_core_bwd", rider=_chips_comm(sums_f2))
    dx1, d_mix_norm, dw_in_t, dw_out, dob1 = _mix_in_bwd(dz, x1, dx2, mix, row(mix_norm), w_in_t, tm_mix, "mix_in_bwd")
    (dw_d1,) = _wgrad([gate1, up1], dob1, tm_w, tf_w, "ffn1_wgrad_d", swiglu=True)
    sums_mix = _reduce_level1([by_owner(dw_in_t[None], ri), by_owner(dw_out[None], ro), by_owner(dw_d1, rf)], "mix")
    (dx0, dgate1, dup1, d_ffn1_norm), parts_mix = _ffn_dgrad(
        dx1, xs, row(ffn1_norm), gate1, up1, w3a, tm_ffn, cf, "ffn1_dgrad", rider=_chips_comm(sums_mix))

    d_lru_w_a = _diag_blocks(dbda, H)
    d_lru_w_x = _diag_blocks(dbdx, H)
    small = [d_ffn1_norm, d_mix_norm, d_ffn2_norm, d_final_norm,
             sg[G_CB], sg[G_LNG], sg[G_LNB], sg[G_LCB], sg[G_BA], sg[G_BX], sg[G_LAM],
             d_lru_w_a, d_lru_w_x,
             sg[G_CW:G_CW + CONV_K], sg[G_LW:G_LW + LRU_K],
             loss_part]
    sizes = [a.size for a in small]
    flat = jnp.concatenate([a.reshape(-1) for a in small])
    n_rows = -(-flat.size // (8 * D)) * 8
    packed = jnp.concatenate([flat, jnp.zeros((n_rows * D - flat.size,), F32)]).reshape(n_rows, D)
    (dw_g1,), (red,) = _wgrad([dgate1], h1, tm_w, tf_w, "ffn1_wgrad_g", rider=_small_sum_comm(packed))
    sums_g1 = _reduce_level1([by_owner(dw_g1, rf)], "g1")
    (dw_u1,), parts_g1 = _wgrad([dup1], h1, tm_w, tf_w, "ffn1_wgrad_u", rider=_chips_comm(sums_g1))
    sums_u1 = _reduce_level1([by_owner(dw_u1, rf)], "u1")
    parts_u1 = _run_comm(_chips_comm(sums_u1), "rs_chips_u1")

    g_gu2, g_d2 = [_chip_add(p, "rs_chip_add_f2%d" % n) for n, p in enumerate(parts_f2)]
    g_in_t, g_out_w, g_d1 = [_chip_add(p, "rs_chip_add_mix%d" % n) for n, p in enumerate(parts_mix)]
    g_g1 = _chip_add(parts_g1[0], "rs_chip_add_g1")
    g_u1 = _chip_add(parts_u1[0], "rs_chip_add_u1")

    red = red.reshape(-1)
    offs = [0]
    for n in sizes:
        offs.append(offs[-1] + n)
    piece = lambda k, shape: red[offs[k]:offs[k + 1]].reshape(shape)
    loss = red[offs[15]]

    me = 4 * lax.axis_index("x") + 2 * lax.axis_index("y") + lax.axis_index("c")
    chan = lambda full_g: lax.dynamic_slice_in_dim(full_g, me * wc, wc, axis=1)
    grads = {
        "ffn1_norm": piece(0, (D,)), "mix_norm": piece(1, (D,)), "ffn2_norm": piece(2, (D,)), "final_norm": piece(3, (D,)),
        "conv_dw_bias": piece(4, (W,)), "conv_ln_g": piece(5, (W,)), "conv_ln_b": piece(6, (W,)),
        "lru_conv_b": piece(7, (W,)), "lru_b_a": piece(8, (W,)), "lru_b_x": piece(9, (W,)), "lru_lambda": piece(10, (W,)),
        "lru_w_a": piece(11, lru_w_a.shape), "lru_w_x": piece(12, lru_w_x.shape),
        "conv_dw": chan(piece(13, (CONV_K, W))), "lru_conv_w": chan(piece(14, (LRU_K, W))),
        "ffn1_w_gate": g_g1[0].T, "ffn1_w_up": g_u1[0].T, "ffn1_w_down": g_d1[0],
        "ffn2_w_gate": g_gu2[0].T, "ffn2_w_up": g_gu2[1].T, "ffn2_w_down": g_d2[0],
        "w_in": g_in_t[0].T, "w_out": g_out_w[0],
    }

    weights = dict(ffn1_norm=ffn1_norm, ffn1_w_gate=ffn1_w_gate, ffn1_w_up=ffn1_w_up, ffn1_w_down=ffn1_w_down, mix_norm=mix_norm, w_in=w_in, conv_dw=conv_dw, conv_dw_bias=conv_dw_bias, conv_ln_g=conv_ln_g, conv_ln_b=conv_ln_b, lru_conv_w=lru_conv_w, lru_conv_b=lru_conv_b, lru_w_a=lru_w_a, lru_b_a=lru_b_a, lru_w_x=lru_w_x, lru_b_x=lru_b_x, lru_lambda=lru_lambda, w_out=w_out, ffn2_norm=ffn2_norm, ffn2_w_gate=ffn2_w_gate, ffn2_w_up=ffn2_w_up, ffn2_w_down=ffn2_w_down, final_norm=final_norm)
    moment1 = dict(ffn1_norm=m_ffn1_norm, ffn1_w_gate=m_ffn1_w_gate, ffn1_w_up=m_ffn1_w_up, ffn1_w_down=m_ffn1_w_down, mix_norm=m_mix_norm, w_in=m_w_in, conv_dw=m_conv_dw, conv_dw_bias=m_conv_dw_bias, conv_ln_g=m_conv_ln_g, conv_ln_b=m_conv_ln_b, lru_conv_w=m_lru_conv_w, lru_conv_b=m_lru_conv_b, lru_w_a=m_lru_w_a, lru_b_a=m_lru_b_a, lru_w_x=m_lru_w_x, lru_b_x=m_lru_b_x, lru_lambda=m_lru_lambda, w_out=m_w_out, ffn2_norm=m_ffn2_norm, ffn2_w_gate=m_ffn2_w_gate, ffn2_w_up=m_ffn2_w_up, ffn2_w_down=m_ffn2_w_down, final_norm=m_final_norm)
    moment2 = dict(ffn1_norm=v_ffn1_norm, ffn1_w_gate=v_ffn1_w_gate, ffn1_w_up=v_ffn1_w_up, ffn1_w_down=v_ffn1_w_down, mix_norm=v_mix_norm, w_in=v_w_in, conv_dw=v_conv_dw, conv_dw_bias=v_conv_dw_bias, conv_ln_g=v_conv_ln_g, conv_ln_b=v_conv_ln_b, lru_conv_w=v_lru_conv_w, lru_conv_b=v_lru_conv_b, lru_w_a=v_lru_w_a, lru_b_a=v_lru_b_a, lru_w_x=v_lru_w_x, lru_b_x=v_lru_b_x, lru_lambda=v_lru_lambda, w_out=v_w_out, ffn2_norm=v_ffn2_norm, ffn2_w_gate=v_ffn2_w_gate, ffn2_w_up=v_ffn2_w_up, ffn2_w_down=v_ffn2_w_down, final_norm=v_final_norm)
    order = list(weights)
    big = ["ffn1_w_gate", "ffn1_w_up", "ffn1_w_down", "w_in", "w_out", "ffn2_w_gate", "ffn2_w_up", "ffn2_w_down"]
    delta, new_m, new_v = {}, {}, {}
    for n in big:
        delta[n], new_m[n], new_v[n] = _adamw(weights[n], grads[n], moment1[n], moment2[n], "adamw_" + n)
    rest = [n for n in order if n not in big]
    rest_sizes = [weights[n].size for n in rest]
    n_flat = sum(rest_sizes)
    pad_rows = -(-n_flat // (8 * 128)) * 8

    def pack(d):
        f = jnp.concatenate([d[n].reshape(-1) for n in rest])
        return jnp.concatenate([f, jnp.zeros((pad_rows * 128 - n_flat,), F32)]).reshape(pad_rows, 128)

    d_s, m_s, v_s = _adamw(pack(weights), pack(grads), pack(moment1), pack(moment2), "adamw_small")
    off = 0
    for n, size in zip(rest, rest_sizes):
        shape = weights[n].shape
        delta[n] = d_s.reshape(-1)[off:off + size].reshape(shape)
        new_m[n] = m_s.reshape(-1)[off:off + size].reshape(shape)
        new_v[n] = v_s.reshape(-1)[off:off + size].reshape(shape)
        off += size

    return (loss, dx0.reshape(x.shape), *[grads[n] for n in order], *[delta[n] for n in order],
            *[new_m[n] for n in order], *[new_v[n] for n in order])
```

```python
import functools
import math

import jax
import jax.numpy as jnp
from jax import lax
from jax.experimental import pallas as pl
from jax.experimental.pallas import tpu as pltpu

F32 = jnp.float32
BF16 = jnp.bfloat16
MESH = pl.DeviceIdType.MESH

N_DEV = 8
N_CHIP = 4
SUBLANES = 8
RMS_EPS = 1e-6
LN_EPS = 1e-5
LRU_C = 8.0
CONV_K = 31
LRU_K = 4
CONV_HALO = 32
LRU_HALO = 8
FFN_RES = 0.5
ADAM_LR, ADAM_B1, ADAM_B2, ADAM_EPS, ADAM_WD, ADAM_STEP = 0.001, 0.9, 0.999, 1e-08, 0.01, 10
GELU_K = math.sqrt(2.0 / math.pi)
GELU_C = 0.044715

MIB = 1024 * 1024
NT = (((1,), (1,)), ((), ()))
NN = (((1,), (0,)), ((), ()))
TN = (((0,), (0,)), ((), ()))

V_CB, V_LNG, V_LNB, V_LCB, V_BA, V_BX, V_LAM = range(7)
G_CW = 0
G_CB, G_LNG, G_LNB = 31, 32, 33
G_LW = 34
G_LCB, G_BA, G_BX, G_LAM = 38, 39, 40, 41
G_ROWS = 48

HBM_SPEC = pl.BlockSpec(memory_space=pltpu.HBM)
VMEM_SPEC = pl.BlockSpec(memory_space=pltpu.VMEM)


def _dot(a, b, dims):
    return lax.dot_general(a, b, dims, preferred_element_type=F32)


def _sigmoid(x):
    return 1.0 / (1.0 + jnp.exp(-x))


def _gelu_parts(x):
    x2 = x * x
    th = jnp.tanh(GELU_K * x * (1.0 + GELU_C * x2))
    gl = 0.5 * x * (1.0 + th)
    dgl = 0.5 * (1.0 + th) + 0.5 * x * (1.0 - th * th) * GELU_K * (1.0 + 3.0 * GELU_C * x2)
    return gl, dgl


def _neg_expm1(y):
    series = -y * (1.0 + y * (1.0 / 2) * (1.0 + y * (1.0 / 3) * (1.0 + y * (1.0 / 4) * (1.0 + y * (1.0 / 5) * (1.0 + y * (1.0 / 6))))))
    return jnp.where(y > -0.25, series, 1.0 - jnp.exp(y))


def _softplus_neg(lam):
    t = -lam
    e = jnp.exp(-jnp.abs(t))
    s = 1.0 + e
    log1p_e = jnp.log(s) - ((s - 1.0) - e) / s
    return jnp.maximum(t, 0.0) + log1p_e


def _rms_stats(xv):
    rstd = lax.rsqrt(jnp.mean(xv * xv, axis=-1, keepdims=True) + RMS_EPS)
    return xv * rstd, rstd


def _rms_bwd(xhat, rstd, g, dh):
    dxhat = dh * g
    dx = rstd * (dxhat - xhat * jnp.mean(dxhat * xhat, axis=-1, keepdims=True))
    return dx, jnp.sum(dh * xhat, axis=0, keepdims=True)


def _row_windows(buf_ref, n_rows, offsets):
    total = buf_ref.shape[0]
    full = buf_ref[...]
    for b in range(SUBLANES):
        offs = [o for o in offsets if o % SUBLANES == b]
        if not offs:
            continue
        assert max(offs) + n_rows <= total
        moved = full if b == 0 else pltpu.roll(full, total - b, 0)
        for o in offs:
            yield o, moved[o - b:o - b + n_rows, :]


def _scan_rows(av, bv, edge, out_ref, reverse=False):
    tm, W = av.shape
    sub = lax.broadcasted_iota(jnp.int32, (tm, W), 0) % SUBLANES
    s = 1
    while s < SUBLANES:
        keep = (sub < SUBLANES - s) if reverse else (sub >= s)
        shift = tm - s if reverse else s
        bv = jnp.where(keep, av * pltpu.roll(bv, shift, 0) + bv, bv)
        av = jnp.where(keep, av * pltpu.roll(av, shift, 0), av)
        s *= 2
    starts = range(0, tm, SUBLANES)
    for r0 in (reversed(starts) if reverse else starts):
        group = av[r0:r0 + SUBLANES, :] * edge + bv[r0:r0 + SUBLANES, :]
        out_ref[r0:r0 + SUBLANES, :] = group
        edge = group[0:1, :] if reverse else group[SUBLANES - 1:SUBLANES, :]
    return edge


class _Comm:
    def __init__(self, arrays, in_specs, out_shapes, out_specs, scratch, start, wait, aliases=None):
        self.arrays, self.in_specs = list(arrays), list(in_specs)
        self.out_shapes, self.out_specs = list(out_shapes), list(out_specs)
        self.scratch, self.start, self.wait = list(scratch), start, wait
        self.aliases = dict(aliases or {})


def _call(body, *, name, grid, in_specs, out_specs, out_shape, scratch_shapes, vmem_mib, args, rider=None,
          num_scalar_prefetch=0):
    params = pltpu.CompilerParams(dimension_semantics=("arbitrary",) * len(grid), vmem_limit_bytes=vmem_mib * MIB)
    if rider is None:
        return pl.pallas_call(
            body, name=name,
            grid_spec=pltpu.PrefetchScalarGridSpec(
                num_scalar_prefetch=num_scalar_prefetch, grid=grid, in_specs=in_specs, out_specs=out_specs,
                scratch_shapes=scratch_shapes),
            out_shape=out_shape, compiler_params=params)(*args)
    assert num_scalar_prefetch == 0
    n_in, n_out, n_scr = len(in_specs), len(out_specs), len(scratch_shapes)
    r_in, r_out = len(rider.arrays), len(rider.out_shapes)
    n_axes = len(grid)

    def carried(*refs):
        pos = [0]

        def take(n):
            pos[0] += n
            return refs[pos[0] - n:pos[0]]

        ins, r_ins, outs, r_outs, scr, r_scr = take(n_in), take(r_in), take(n_out), take(r_out), take(n_scr), take(len(rider.scratch))
        first = pl.program_id(0) == 0
        last = pl.program_id(0) == grid[0] - 1
        for ax in range(1, n_axes):
            first = first & (pl.program_id(ax) == 0)
            last = last & (pl.program_id(ax) == grid[ax] - 1)

        @pl.when(first)
        def _():
            rider.start(r_ins, r_outs, r_scr)

        body(*ins, *outs, *scr)

        @pl.when(last)
        def _():
            rider.wait(r_ins, r_outs, r_scr)

    res = pl.pallas_call(
        carried, name=name,
        grid=grid,
        in_specs=list(in_specs) + rider.in_specs,
        out_specs=list(out_specs) + rider.out_specs,
        out_shape=list(out_shape) + rider.out_shapes,
        scratch_shapes=list(scratch_shapes) + rider.scratch,
        input_output_aliases={n_in + i: n_out + o for i, o in rider.aliases.items()},
        compiler_params=params)(*args, *rider.arrays)
    return res[:n_out], res[n_out:]


def _run_comm(comm, name):
    n_in, n_out = len(comm.arrays), len(comm.out_shapes)

    def body(*refs):
        ins, outs, scr = refs[:n_in], refs[n_in:n_in + n_out], refs[n_in + n_out:]
        comm.start(ins, outs, scr)
        comm.wait(ins, outs, scr)

    return pl.pallas_call(
        body, name=name,
        in_specs=comm.in_specs, out_specs=comm.out_specs, out_shape=comm.out_shapes,
        scratch_shapes=comm.scratch, input_output_aliases=comm.aliases,
        compiler_params=pltpu.CompilerParams(vmem_limit_bytes=24 * MIB))(*comm.arrays)


def _both(a, b):
    ni, no, ns = len(a.arrays), len(a.out_shapes), len(a.scratch)

    def start(ins, outs, scr):
        a.start(ins[:ni], outs[:no], scr[:ns])
        b.start(ins[ni:], outs[no:], scr[ns:])

    def wait(ins, outs, scr):
        a.wait(ins[:ni], outs[:no], scr[:ns])
        b.wait(ins[ni:], outs[no:], scr[ns:])

    aliases = dict(a.aliases)
    aliases.update({ni + i: no + o for i, o in b.aliases.items()})
    return _Comm(a.arrays + b.arrays, a.in_specs + b.in_specs, a.out_shapes + b.out_shapes,
                 a.out_specs + b.out_specs, a.scratch + b.scratch, start, wait, aliases)


def _place():
    return lax.axis_index("x"), lax.axis_index("y"), lax.axis_index("c")


def _peer(k):
    x, y, c = _place()
    px, py, pc = x ^ ((k >> 2) & 1), y ^ ((k >> 1) & 1), c ^ (k & 1)
    return (px, py, pc), 4 * px + 2 * py + pc


SIBLING = 1
SAME_CORE = (2, 4, 6)
EVERYONE = tuple(range(1, N_DEV))


def _gather_comm(shards, views, out_shapes, relations):
    na = len(shards)

    def copies(ins, outs, scr):
        send_sems, recv_sems, _ = scr
        _, me = _peer(0)
        out = []
        for a in range(na):
            for k in relations[a]:
                peer, theirs = _peer(k)
                send = functools.partial(
                    pltpu.make_async_remote_copy,
                    src_ref=ins[a], dst_ref=views[a](outs[a], me),
                    send_sem=send_sems.at[7 * a + k - 1], recv_sem=recv_sems.at[7 * a + k - 1],
                    device_id=peer, device_id_type=MESH)
                recv = functools.partial(
                    pltpu.make_async_remote_copy,
                    src_ref=ins[a], dst_ref=views[a](outs[a], theirs),
                    send_sem=send_sems.at[7 * a + k - 1], recv_sem=recv_sems.at[7 * a + k - 1],
                    device_id=peer, device_id_type=MESH)
                out.append((send, recv))
        return out

    def local(ins, outs, scr):
        _, me = _peer(0)
        return [pltpu.make_async_copy(ins[a], views[a](outs[a], me), scr[2].at[a]) for a in range(na)]

    def start(ins, outs, scr):
        for cp in local(ins, outs, scr):
            cp.start()
        for send, _ in copies(ins, outs, scr):
            send().start()

    def wait(ins, outs, scr):
        for _, recv in copies(ins, outs, scr):
            recv().wait_recv()
        for send, _ in copies(ins, outs, scr):
            send().wait_send()
        for cp in local(ins, outs, scr):
            cp.wait()

    return _Comm(shards, [HBM_SPEC] * na, out_shapes, [HBM_SPEC] * na,
                 [pltpu.SemaphoreType.DMA((7 * na,)), pltpu.SemaphoreType.DMA((7 * na,)),
                  pltpu.SemaphoreType.DMA((na,))], start, wait)


def _forward_comm(gathered, views):
    na = len(gathered)
    shapes = [jax.ShapeDtypeStruct(g.shape, g.dtype) for g in gathered]

    def copies(outs, scr):
        send_sems, recv_sems = scr
        sibling, _ = _peer(SIBLING)
        out = []
        for a in range(na):
            for n, k in enumerate(SAME_CORE):
                _, mine = _peer(k)
                _, theirs = _peer(k ^ SIBLING)
                send = functools.partial(
                    pltpu.make_async_remote_copy,
                    src_ref=views[a](outs[a], mine), dst_ref=views[a](outs[a], mine),
                    send_sem=send_sems.at[3 * a + n], recv_sem=recv_sems.at[3 * a + n],
                    device_id=sibling, device_id_type=MESH)
                recv = functools.partial(
                    pltpu.make_async_remote_copy,
                    src_ref=views[a](outs[a], mine), dst_ref=views[a](outs[a], theirs),
                    send_sem=send_sems.at[3 * a + n], recv_sem=recv_sems.at[3 * a + n],
                    device_id=sibling, device_id_type=MESH)
                out.append((send, recv))
        return out

    def start(ins, outs, scr):
        for send, _ in copies(outs, scr):
            send().start()

    def wait(ins, outs, scr):
        for _, recv in copies(outs, scr):
            recv().wait_recv()
        for send, _ in copies(outs, scr):
            send().wait_send()

    return _Comm(gathered, [HBM_SPEC] * na, shapes, [HBM_SPEC] * na,
                 [pltpu.SemaphoreType.DMA((3 * na,)), pltpu.SemaphoreType.DMA((3 * na,))], start, wait,
                 aliases={a: a for a in range(na)})


def _all_gather(shards, views, out_shapes, name):
    na = len(shards)
    level1 = _gather_comm(shards, views, out_shapes, [(SIBLING,) + SAME_CORE] * na)

    def body(*refs):
        ins, outs = refs[:na], refs[na:2 * na]
        send_sems, recv_sems, local_sems, fwd_send, fwd_recv = refs[2 * na:]
        sibling, _ = _peer(SIBLING)
        scr = (send_sems, recv_sems, local_sems)
        level1.start(ins, outs, scr)
        passed, landing = [], []
        for a in range(na):
            for n, k in enumerate(SAME_CORE):
                peer, mine = _peer(k)
                _, theirs = _peer(k ^ SIBLING)
                pltpu.make_async_remote_copy(
                    src_ref=ins[a], dst_ref=views[a](outs[a], mine),
                    send_sem=send_sems.at[7 * a + k - 1], recv_sem=recv_sems.at[7 * a + k - 1],
                    device_id=peer, device_id_type=MESH).wait_recv()
                fwd = pltpu.make_async_remote_copy(
                    src_ref=views[a](outs[a], mine), dst_ref=views[a](outs[a], mine),
                    send_sem=fwd_send.at[3 * a + n], recv_sem=fwd_recv.at[3 * a + n],
                    device_id=sibling, device_id_type=MESH)
                fwd.start()
                passed.append(fwd)
                landing.append(pltpu.make_async_remote_copy(
                    src_ref=views[a](outs[a], mine), dst_ref=views[a](outs[a], theirs),
                    send_sem=fwd_send.at[3 * a + n], recv_sem=fwd_recv.at[3 * a + n],
                    device_id=sibling, device_id_type=MESH))
        for a in range(na):
            _, theirs = _peer(SIBLING)
            pltpu.make_async_remote_copy(
                src_ref=ins[a], dst_ref=views[a](outs[a], theirs),
                send_sem=send_sems.at[7 * a + SIBLING - 1], recv_sem=recv_sems.at[7 * a + SIBLING - 1],
                device_id=sibling, device_id_type=MESH).wait_recv()
        for cp in landing:
            cp.wait_recv()
        for cp in passed:
            cp.wait_send()
        _, me = _peer(0)
        for a in range(na):
            for k in (SIBLING,) + SAME_CORE:
                peer, _ = _peer(k)
                pltpu.make_async_remote_copy(
                    src_ref=ins[a], dst_ref=views[a](outs[a], me),
                    send_sem=send_sems.at[7 * a + k - 1], recv_sem=recv_sems.at[7 * a + k - 1],
                    device_id=peer, device_id_type=MESH).wait_send()
            pltpu.make_async_copy(ins[a], views[a](outs[a], me), local_sems.at[a]).wait()

    return pl.pallas_call(
        body, name=name,
        in_specs=[HBM_SPEC] * na, out_specs=[HBM_SPEC] * na, out_shape=out_shapes,
        scratch_shapes=level1.scratch + [pltpu.SemaphoreType.DMA((3 * na,)), pltpu.SemaphoreType.DMA((3 * na,))],
    )(*shards)


def _sibling_comm(grads):
    na = len(grads)
    shapes = [jax.ShapeDtypeStruct(g.shape[:2] + g.shape[3:], g.dtype) for g in grads]

    def copies(ins, outs, scr):
        x, y, c = _place()
        return [pltpu.make_async_remote_copy(
            src_ref=ins[a].at[:, :, 1 - c], dst_ref=outs[a],
            send_sem=scr[0].at[a], recv_sem=scr[1].at[a],
            device_id=(x, y, 1 - c), device_id_type=MESH) for a in range(na)]

    def start(ins, outs, scr):
        for cp in copies(ins, outs, scr):
            cp.start()

    def wait(ins, outs, scr):
        for cp in copies(ins, outs, scr):
            cp.wait()

    return _Comm(grads, [HBM_SPEC] * na, shapes, [HBM_SPEC] * na,
                 [pltpu.SemaphoreType.DMA((na,)), pltpu.SemaphoreType.DMA((na,))], start, wait)


def _chips_comm(sums):
    na = len(sums)
    shapes = [jax.ShapeDtypeStruct(s.shape, s.dtype) for s in sums]

    def copies(ins, outs, scr):
        x, y, c = _place()
        mine = 2 * x + y
        out = []
        for a in range(na):
            for n, k in enumerate(SAME_CORE):
                (px, py, pc), _ = _peer(k)
                theirs = 2 * px + py
                send = functools.partial(
                    pltpu.make_async_remote_copy,
                    src_ref=ins[a].at[:, theirs], dst_ref=outs[a].at[:, mine],
                    send_sem=scr[0].at[3 * a + n], recv_sem=scr[1].at[3 * a + n],
                    device_id=(px, py, pc), device_id_type=MESH)
                recv = functools.partial(
                    pltpu.make_async_remote_copy,
                    src_ref=ins[a].at[:, mine], dst_ref=outs[a].at[:, theirs],
                    send_sem=scr[0].at[3 * a + n], recv_sem=scr[1].at[3 * a + n],
                    device_id=(px, py, pc), device_id_type=MESH)
                out.append((send, recv))
        return out

    def local(ins, outs, scr):
        x, y, _ = _place()
        mine = 2 * x + y
        return [pltpu.make_async_copy(ins[a].at[:, mine], outs[a].at[:, mine], scr[2].at[a]) for a in range(na)]

    def start(ins, outs, scr):
        for cp in local(ins, outs, scr):
            cp.start()
        for send, _ in copies(ins, outs, scr):
            send().start()

    def wait(ins, outs, scr):
        for _, recv in copies(ins, outs, scr):
            recv().wait_recv()
        for send, _ in copies(ins, outs, scr):
            send().wait_send()
        for cp in local(ins, outs, scr):
            cp.wait()

    return _Comm(sums, [HBM_SPEC] * na, shapes, [HBM_SPEC] * na,
                 [pltpu.SemaphoreType.DMA((3 * na,)), pltpu.SemaphoreType.DMA((3 * na,)),
                  pltpu.SemaphoreType.DMA((na,))], start, wait)


def _small_sum_comm(s):
    R, C = s.shape

    def copies(ins, scr):
        buf, send_sems, recv_sems = scr
        _, me = _peer(0)
        out = []
        for k in EVERYONE:
            peer, theirs = _peer(k)
            send = functools.partial(
                    pltpu.make_async_remote_copy,
                src_ref=ins[0], dst_ref=buf.at[me], send_sem=send_sems.at[k - 1], recv_sem=recv_sems.at[k - 1],
                device_id=peer, device_id_type=MESH)
            recv = functools.partial(
                    pltpu.make_async_remote_copy,
                src_ref=ins[0], dst_ref=buf.at[theirs], send_sem=send_sems.at[k - 1], recv_sem=recv_sems.at[k - 1],
                device_id=peer, device_id_type=MESH)
            out.append((send, recv))
        return out

    def start(ins, outs, scr):
        _, me = _peer(0)
        scr[0][me] = ins[0][...]
        for send, _ in copies(ins, scr):
            send().start()

    def wait(ins, outs, scr):
        for _, recv in copies(ins, scr):
            recv().wait_recv()
        for send, _ in copies(ins, scr):
            send().wait_send()
        acc = scr[0][0]
        for j in range(1, N_DEV):
            acc = acc + scr[0][j]
        outs[0][...] = acc

    return _Comm([s], [VMEM_SPEC], [jax.ShapeDtypeStruct((R, C), F32)], [VMEM_SPEC],
                 [pltpu.VMEM((N_DEV, R, C), F32), pltpu.SemaphoreType.DMA((N_DEV - 1,)),
                  pltpu.SemaphoreType.DMA((N_DEV - 1,))], start, wait)


def _load_weights(w_hbm, w_vmem, sem):
    @pl.when(pl.program_id(0) == 0)
    def _():
        copies = [pltpu.make_async_copy(w_hbm.at[k], w_vmem.at[k], sem.at[k]) for k in range(3)]
        for cp in copies:
            cp.start()
        for cp in copies:
            cp.wait()


def _ffn_fwd(x, g, w3, tm, cf, name, rider=None):
    T, D = x.shape
    F = w3.shape[1]

    def body(x_ref, g_ref, w_hbm, xo_ref, h_ref, gate_ref, up_ref, wv, act_ref, sem):
        _load_weights(w_hbm, wv, sem)
        xhat, _ = _rms_stats(x_ref[...])
        hb = (xhat * g_ref[...]).astype(BF16)
        h_ref[...] = hb
        for lo in range(0, F, cf):
            gate = _dot(hb, wv[0, lo:lo + cf, :], NT)
            up = _dot(hb, wv[1, lo:lo + cf, :], NT)
            gate_ref[:, lo:lo + cf] = gate.astype(BF16)
            up_ref[:, lo:lo + cf] = up.astype(BF16)
            act_ref[:, lo:lo + cf] = (gate * _sigmoid(gate) * up).astype(BF16)
        xo_ref[...] = x_ref[...] + FFN_RES * _dot(act_ref[...], wv[2], NN)

    row = pl.BlockSpec((tm, D), lambda i: (i, 0))
    hid = pl.BlockSpec((tm, F), lambda i: (i, 0))
    return _call(
        body, name=name, grid=(T // tm,),
        in_specs=[row, pl.BlockSpec((1, D), lambda i: (0, 0)), HBM_SPEC],
        out_specs=[row, row, hid, hid],
        out_shape=[jax.ShapeDtypeStruct((T, D), F32), jax.ShapeDtypeStruct((T, D), BF16),
                   jax.ShapeDtypeStruct((T, F), BF16), jax.ShapeDtypeStruct((T, F), BF16)],
        scratch_shapes=[pltpu.VMEM((3, F, D), BF16), pltpu.VMEM((tm, F), BF16), pltpu.SemaphoreType.DMA((3,))],
        vmem_mib=48, args=(x, g, w3), rider=rider)


def _ffn_dgrad(dout, x, g, gate, up, w3, tm, cf, name, rider=None):
    T, D = x.shape
    F = w3.shape[1]

    def body(do_ref, x_ref, g_ref, gate_ref, up_ref, w_hbm, dx_ref, dgate_ref, dup_ref, dg_ref, wv, sem):
        _load_weights(w_hbm, wv, sem)

        @pl.when(pl.program_id(0) == 0)
        def _():
            dg_ref[...] = jnp.zeros_like(dg_ref)

        dob = (FFN_RES * do_ref[...]).astype(BF16)
        for lo in range(0, F, cf):
            dact = _dot(dob, wv[2, lo:lo + cf, :], NT)
            gv = gate_ref[:, lo:lo + cf].astype(F32)
            uv = up_ref[:, lo:lo + cf].astype(F32)
            sig = _sigmoid(gv)
            dup_ref[:, lo:lo + cf] = (dact * (gv * sig)).astype(BF16)
            dgate_ref[:, lo:lo + cf] = (dact * uv * (sig * (1.0 + gv * (1.0 - sig)))).astype(BF16)
        dh = _dot(dgate_ref[...], wv[0], NN) + _dot(dup_ref[...], wv[1], NN)
        xhat, rstd = _rms_stats(x_ref[...])
        dx, dg = _rms_bwd(xhat, rstd, g_ref[...], dh)
        dx_ref[...] = do_ref[...] + dx
        dg_ref[...] += dg

    row = pl.BlockSpec((tm, D), lambda i: (i, 0))
    hid = pl.BlockSpec((tm, F), lambda i: (i, 0))
    vec = pl.BlockSpec((1, D), lambda i: (0, 0))
    return _call(
        body, name=name, grid=(T // tm,),
        in_specs=[row, row, vec, hid, hid, HBM_SPEC],
        out_specs=[row, hid, hid, vec],
        out_shape=[jax.ShapeDtypeStruct((T, D), F32), jax.ShapeDtypeStruct((T, F), BF16),
                   jax.ShapeDtypeStruct((T, F), BF16), jax.ShapeDtypeStruct((1, D), F32)],
        scratch_shapes=[pltpu.VMEM((3, F, D), BF16), pltpu.SemaphoreType.DMA((3,))],
        vmem_mib=52, args=(dout, x, g, gate, up, w3), rider=rider)


def _wgrad(lhs, rhs, tm, tf, name, swiglu=False, rider=None):
    T, F = lhs[0].shape
    D = rhs.shape[1]
    K = 1 if swiglu else len(lhs)
    nl = len(lhs)

    rows = min(256, tm)

    def body(*refs):
        lhs_refs, rhs_ref, dw_ref = refs[:nl], refs[nl], refs[nl + 1]
        accs, act_ref = refs[nl + 2:nl + 2 + K], refs[nl + 2 + K:]
        i = pl.program_id(1)

        @pl.when(i == 0)
        def _():
            for acc in accs:
                acc[...] = jnp.zeros_like(acc)

        rv = rhs_ref[...]
        if swiglu:
            for r0 in range(0, tm, rows):
                gv = lhs_refs[0][r0:r0 + rows, :].astype(F32)
                act_ref[0][r0:r0 + rows, :] = (gv * _sigmoid(gv) * lhs_refs[1][r0:r0 + rows, :].astype(F32)).astype(BF16)
            left = [act_ref[0][...]]
        else:
            left = [r[...] for r in lhs_refs]
        for acc, lv in zip(accs, left):
            acc[...] += _dot(lv, rv, TN)

        @pl.when(i == pl.num_programs(1) - 1)
        def _():
            for k, acc in enumerate(accs):
                dw_ref[k] = acc[...].astype(BF16)

    hid = pl.BlockSpec((tm, tf), lambda f, i: (i, f))
    return _call(
        body, name=name, grid=(F // tf, T // tm),
        in_specs=[hid] * nl + [pl.BlockSpec((tm, D), lambda f, i: (i, 0))],
        out_specs=[pl.BlockSpec((K, tf, D), lambda f, i: (0, f, 0))],
        out_shape=[jax.ShapeDtypeStruct((K, F, D), BF16)],
        scratch_shapes=[pltpu.VMEM((tf, D), F32)] * K + ([pltpu.VMEM((tm, tf), BF16)] if swiglu else []),
        vmem_mib=56, args=(*lhs, rhs), rider=rider)


def _lru_gates(xr, bda_ref, bdx_ref, vec_ref):
    xrb = xr.astype(BF16)
    r = _sigmoid(_dot(xrb, bda_ref[...], NN) + vec_ref[V_BA:V_BA + 1, :])
    ig = _sigmoid(_dot(xrb, bdx_ref[...], NN) + vec_ref[V_BX:V_BX + 1, :])
    sp = _softplus_neg(vec_ref[V_LAM:V_LAM + 1, :])
    log_a = (-LRU_C * sp) * r
    a = jnp.exp(log_a)
    mult = jnp.sqrt(_neg_expm1(2.0 * log_a))
    return xrb, r, ig, sp, a, mult


def _layernorm_stats(u1):
    xc = u1 - jnp.mean(u1, axis=-1, keepdims=True)
    rs = lax.rsqrt(jnp.mean(xc * xc, axis=-1, keepdims=True) + LN_EPS)
    return xc * rs, rs


def _mix_core_fwd(x1, g, w_in_t, w_out, bda, bdx, cw, lw, vec, tm, name, rider=None):
    T, D = x1.shape
    W = cw.shape[1]
    assert tm >= CONV_HALO and w_in_t.shape[0] == 4 * W

    def body(x1_ref, g_ref, wi_ref, wo_ref, bda_ref, bdx_ref, cw_ref, lw_ref, vec_ref,
             x2_ref, z_ref, mix_ref, u1_ref, xr_ref, hst_ref, ubuf, rbuf, hc):
        @pl.when(pl.program_id(0) == 0)
        def _():
            ubuf[0:CONV_HALO, :] = jnp.zeros((CONV_HALO, W), F32)
            rbuf[0:LRU_HALO, :] = jnp.zeros((LRU_HALO, W), F32)
            hc[...] = jnp.zeros_like(hc)

        xhat, _ = _rms_stats(x1_ref[...])
        z_ref[...] = _dot((xhat * g_ref[...]).astype(BF16), wi_ref[...], NT)

        ubuf[CONV_HALO:CONV_HALO + tm, :] = z_ref[:, 0:W] * _sigmoid(z_ref[:, W:2 * W])
        u1 = jnp.zeros((tm, W), F32) + vec_ref[V_CB:V_CB + 1, :]
        base = CONV_HALO - (CONV_K - 1)
        for off, win in _row_windows(ubuf, tm, range(base, base + CONV_K)):
            u1 = u1 + cw_ref[off - base:off - base + 1, :] * win
        ubuf[0:CONV_HALO, :] = ubuf[tm:tm + CONV_HALO, :]
        u1_ref[...] = u1
        xh, _ = _layernorm_stats(u1)
        u2 = xh * vec_ref[V_LNG:V_LNG + 1, :] + vec_ref[V_LNB:V_LNB + 1, :]
        ub = (u2 * _sigmoid(u2)).astype(BF16)
        mix_ref[:, 0:W] = ub

        rbuf[LRU_HALO:LRU_HALO + tm, :] = z_ref[:, 2 * W:3 * W]
        xr = jnp.zeros((tm, W), F32) + vec_ref[V_LCB:V_LCB + 1, :]
        for k in range(LRU_K):
            off = LRU_HALO - (LRU_K - 1) + k
            xr = xr + lw_ref[k:k + 1, :] * rbuf[off:off + tm, :]
        rbuf[0:LRU_HALO, :] = rbuf[tm:tm + LRU_HALO, :]
        xr_ref[...] = xr
        _, _, ig, _, a, mult = _lru_gates(xr, bda_ref, bdx_ref, vec_ref)
        hc[0:1, :] = _scan_rows(a, mult * (ig * xr), hc[0:1, :], hst_ref)
        gl, _ = _gelu_parts(z_ref[:, 3 * W:4 * W])
        yb = (hst_ref[...] * gl).astype(BF16)
        mix_ref[:, W:2 * W] = yb

        x2_ref[...] = x1_ref[...] + _dot(ub, wo_ref[0:W, :], NN) + _dot(yb, wo_ref[W:2 * W, :], NN)

    full = lambda a: pl.BlockSpec(a.shape, lambda i: (0,) * a.ndim)
    tile = lambda n: pl.BlockSpec((tm, n), lambda i: (i, 0))
    return _call(
        body, name=name, grid=(T // tm,),
        in_specs=[tile(D), full(g), full(w_in_t), full(w_out), full(bda), full(bdx), full(cw), full(lw), full(vec)],
        out_specs=[tile(D), tile(4 * W), tile(2 * W), tile(W), tile(W), tile(W)],
        out_shape=[jax.ShapeDtypeStruct((T, D), F32), jax.ShapeDtypeStruct((T, 4 * W), F32),
                   jax.ShapeDtypeStruct((T, 2 * W), BF16), jax.ShapeDtypeStruct((T, W), F32),
                   jax.ShapeDtypeStruct((T, W), F32), jax.ShapeDtypeStruct((T, W), F32)],
        scratch_shapes=[pltpu.VMEM((tm + CONV_HALO, W), F32), pltpu.VMEM((tm + LRU_HALO, W), F32),
                        pltpu.VMEM((8, W), F32)],
        vmem_mib=56, args=(x1, g, w_in_t, w_out, bda, bdx, cw, lw, vec), rider=rider)


def _mix_core_bwd(dx2, z, u1, xr, hst, w_out, bda, bdx, cw, lw, vec, tm, name, rider=None):
    T, D = dx2.shape
    W = cw.shape[1]
    nt = T // tm
    assert tm >= CONV_HALO and tm % CONV_HALO == 0

    def body(dx_ref, z_ref, zh_ref, u1_ref, xr_ref, h_ref, hh_ref, wo_ref, bda_ref, bdx_ref, cw_ref, lw_ref, vec_ref,
             dz_ref, sg_ref, dbda_ref, dbdx_ref, u0buf, du1buf, rxbuf, dxrbuf, gbuf, gc, spacc):
        i = pl.program_id(0)
        first = i == nt - 1
        row = lax.broadcasted_iota(jnp.int32, (tm, W), 0)

        @pl.when(i == 0)
        def _():
            sg_ref[...] = jnp.zeros_like(sg_ref)
            dbda_ref[...] = jnp.zeros_like(dbda_ref)
            dbdx_ref[...] = jnp.zeros_like(dbdx_ref)
            du1buf[tm:tm + CONV_HALO, :] = jnp.zeros((CONV_HALO, W), F32)
            dxrbuf[tm:tm + LRU_HALO, :] = jnp.zeros((LRU_HALO, W), F32)
            gc[...] = jnp.zeros_like(gc)
            spacc[...] = jnp.zeros_like(spacc)

        def accum(r, val):
            sg_ref[r:r + 1, :] += jnp.sum(val, axis=0, keepdims=True)

        dmix = _dot(dx_ref[...].astype(BF16), wo_ref[...], NT)
        d_u = dmix[:, 0:W]
        d_yr = dmix[:, W:2 * W]

        xh, rs = _layernorm_stats(u1_ref[...])
        ln_g = vec_ref[V_LNG:V_LNG + 1, :]
        u2 = xh * ln_g + vec_ref[V_LNB:V_LNB + 1, :]
        s2 = _sigmoid(u2)
        d_u2 = d_u * (s2 * (1.0 + u2 * (1.0 - s2)))
        accum(G_LNG, d_u2 * xh)
        accum(G_LNB, d_u2)
        d_xh = d_u2 * ln_g
        d_u1 = rs * (d_xh - jnp.mean(d_xh, axis=-1, keepdims=True)
                     - xh * jnp.mean(d_xh * xh, axis=-1, keepdims=True))
        accum(G_CB, d_u1)
        halo_on = jnp.where(first, 0.0, 1.0)
        u0buf[0:CONV_HALO, :] = halo_on * (zh_ref[:, 0:W] * _sigmoid(zh_ref[:, W:2 * W]))
        cv = z_ref[:, 0:W]
        sgc = _sigmoid(z_ref[:, W:2 * W])
        u0buf[CONV_HALO:CONV_HALO + tm, :] = cv * sgc
        du1buf[0:tm, :] = d_u1
        base = CONV_HALO - (CONV_K - 1)
        for off, win in _row_windows(u0buf, tm, range(base, base + CONV_K)):
            accum(G_CW + off - base, d_u1 * win)
        d_u0 = jnp.zeros((tm, W), F32)
        for off, win in _row_windows(du1buf, tm, range(0, CONV_K)):
            d_u0 = d_u0 + cw_ref[CONV_K - 1 - off:CONV_K - off, :] * win
        du1buf[tm:tm + CONV_HALO, :] = du1buf[0:CONV_HALO, :]
        dz_ref[:, 0:W] = (d_u0 * sgc).astype(BF16)
        dz_ref[:, W:2 * W] = (d_u0 * cv * (sgc * (1.0 - sgc))).astype(BF16)

        xrv = xr_ref[...]
        xrb, r, ig, sp, a, mult = _lru_gates(xrv, bda_ref, bdx_ref, vec_ref)
        h = h_ref[...]
        gl, dgl = _gelu_parts(z_ref[:, 3 * W:4 * W])
        dz_ref[:, 3 * W:4 * W] = (d_yr * h * dgl).astype(BF16)
        a_next = jnp.where(row == tm - 1, 1.0, pltpu.roll(a, tm - 1, 0))
        g_first = _scan_rows(a_next, d_yr * gl, gc[0:1, :], gbuf, reverse=True)
        g = gbuf[...]
        gc[0:1, :] = a[0:1, :] * g_first
        hprev = jnp.where(row == 0, halo_on * hh_ref[LRU_HALO - 1:LRU_HALO, :], pltpu.roll(h, 1, 0))
        d_log_a = (g * hprev) * a - (g * ig * xrv) * (a * a) / mult
        d_ig = g * mult * xrv
        d_xr = g * mult * ig
        spacc[0:1, :] += jnp.sum(d_log_a * r, axis=0, keepdims=True)
        d_pa32 = (d_log_a * (-LRU_C * sp)) * (r * (1.0 - r))
        d_px32 = d_ig * (ig * (1.0 - ig))
        accum(G_BA, d_pa32)
        accum(G_BX, d_px32)
        d_pa = d_pa32.astype(BF16)
        d_px = d_px32.astype(BF16)
        d_xr = d_xr + _dot(d_pa, bda_ref[...], NT) + _dot(d_px, bdx_ref[...], NT)
        dbda_ref[...] += _dot(xrb, d_pa, TN)
        dbdx_ref[...] += _dot(xrb, d_px, TN)
        accum(G_LCB, d_xr)
        rxbuf[0:LRU_HALO, :] = halo_on * zh_ref[CONV_HALO - LRU_HALO:CONV_HALO, 2 * W:3 * W]
        rxbuf[LRU_HALO:LRU_HALO + tm, :] = z_ref[:, 2 * W:3 * W]
        dxrbuf[0:tm, :] = d_xr
        d_rx = jnp.zeros((tm, W), F32)
        for k in range(LRU_K):
            off = LRU_HALO - (LRU_K - 1) + k
            accum(G_LW + k, d_xr * rxbuf[off:off + tm, :])
            d_rx = d_rx + lw_ref[k:k + 1, :] * dxrbuf[LRU_K - 1 - k:LRU_K - 1 - k + tm, :]
        dxrbuf[tm:tm + LRU_HALO, :] = dxrbuf[0:LRU_HALO, :]
        dz_ref[:, 2 * W:3 * W] = d_rx.astype(BF16)

        @pl.when(first)
        def _():
            lam = vec_ref[V_LAM:V_LAM + 1, :]
            sg_ref[G_LAM:G_LAM + 1, :] = LRU_C * _sigmoid(-lam) * spacc[0:1, :]

    full = lambda a: pl.BlockSpec(a.shape, lambda i: (0,) * a.ndim)
    tile = lambda n: pl.BlockSpec((tm, n), lambda i: (nt - 1 - i, 0))
    halo = lambda rows, n: pl.BlockSpec(
        (rows, n), lambda i: (jnp.maximum((nt - 1 - i) * (tm // rows) - 1, 0), 0))
    return _call(
        body, name=name, grid=(nt,),
        in_specs=[tile(D), tile(4 * W), halo(CONV_HALO, 4 * W), tile(W), tile(W), tile(W), halo(LRU_HALO, W),
                  full(w_out), full(bda), full(bdx), full(cw), full(lw), full(vec)],
        out_specs=[tile(4 * W), pl.BlockSpec((G_ROWS, W), lambda i: (0, 0)),
                   pl.BlockSpec((W, W), lambda i: (0, 0)), pl.BlockSpec((W, W), lambda i: (0, 0))],
        out_shape=[jax.ShapeDtypeStruct((T, 4 * W), BF16), jax.ShapeDtypeStruct((G_ROWS, W), F32),
                   jax.ShapeDtypeStruct((W, W), F32), jax.ShapeDtypeStruct((W, W), F32)],
        scratch_shapes=[pltpu.VMEM((tm + CONV_HALO, W), F32), pltpu.VMEM((tm + CONV_HALO, W), F32),
                        pltpu.VMEM((tm + LRU_HALO, W), F32), pltpu.VMEM((tm + LRU_HALO, W), F32),
                        pltpu.VMEM((tm, W), F32), pltpu.VMEM((8, W), F32), pltpu.VMEM((8, W), F32)],
        vmem_mib=56, args=(dx2, z, z, u1, xr, hst, hst, w_out, bda, bdx, cw, lw, vec), rider=rider)


def _mix_in_bwd(dz, x1, dx2, mix, g, w_in_t, tm, name, rider=None):
    T, D = x1.shape
    Z = w_in_t.shape[0]
    M = mix.shape[1]

    def body(dz_ref, x_ref, dx2_ref, mix_ref, g_ref, w_ref, dx1_ref, dg_ref, dwi_ref, dwo_ref, dob_ref, ai_ref, ao_ref):
        i = pl.program_id(0)

        @pl.when(i == 0)
        def _():
            dg_ref[...] = jnp.zeros_like(dg_ref)
            ai_ref[...] = jnp.zeros_like(ai_ref)
            ao_ref[...] = jnp.zeros_like(ao_ref)

        xhat, rstd = _rms_stats(x_ref[...])
        gv = g_ref[...]
        hb = (xhat * gv).astype(BF16)
        dzb = dz_ref[...]
        dx, dg = _rms_bwd(xhat, rstd, gv, _dot(dzb, w_ref[...], NN))
        dx2 = dx2_ref[...]
        dx1 = dx2 + dx
        dx1_ref[...] = dx1
        dob_ref[...] = (FFN_RES * dx1).astype(BF16)
        dg_ref[...] += dg
        ai_ref[...] += _dot(dzb, hb, TN)
        ao_ref[...] += _dot(mix_ref[...], dx2.astype(BF16), TN)

        @pl.when(i == pl.num_programs(0) - 1)
        def _():
            dwi_ref[...] = ai_ref[...].astype(BF16)
            dwo_ref[...] = ao_ref[...].astype(BF16)

    tile = lambda n: pl.BlockSpec((tm, n), lambda i: (i, 0))
    const = lambda r, c: pl.BlockSpec((r, c), lambda i: (0, 0))
    return _call(
        body, name=name, grid=(T // tm,),
        in_specs=[tile(Z), tile(D), tile(D), tile(M), const(1, D), const(Z, D)],
        out_specs=[tile(D), const(1, D), const(Z, D), const(M, D), tile(D)],
        out_shape=[jax.ShapeDtypeStruct((T, D), F32), jax.ShapeDtypeStruct((1, D), F32),
                   jax.ShapeDtypeStruct((Z, D), BF16), jax.ShapeDtypeStruct((M, D), BF16),
                   jax.ShapeDtypeStruct((T, D), BF16)],
        scratch_shapes=[pltpu.VMEM((Z, D), F32), pltpu.VMEM((M, D), F32)],
        vmem_mib=56, args=(dz, x1, dx2, mix, g, w_in_t), rider=rider)


def _final_loss(x3, g, target, tm, name):
    T, D = x3.shape

    def body(x_ref, g_ref, t_ref, dx_ref, dg_ref, loss_ref, dob_ref):
        @pl.when(pl.program_id(0) == 0)
        def _():
            dg_ref[...] = jnp.zeros_like(dg_ref)
            loss_ref[...] = jnp.zeros_like(loss_ref)

        xhat, rstd = _rms_stats(x_ref[...])
        gv = g_ref[...]
        err = xhat * gv - t_ref[...]
        loss_ref[...] += (0.5 / D) * jnp.sum(err * err)
        dx, dg = _rms_bwd(xhat, rstd, gv, err * (1.0 / D))
        dx_ref[...] = dx
        dob_ref[...] = (FFN_RES * dx).astype(BF16)
        dg_ref[...] += dg

    tile = pl.BlockSpec((tm, D), lambda i: (i, 0))
    return _call(
        body, name=name, grid=(T // tm,),
        in_specs=[tile, pl.BlockSpec((1, D), lambda i: (0, 0)), tile],
        out_specs=[tile, pl.BlockSpec((1, D), lambda i: (0, 0)), pl.BlockSpec((1, 128), lambda i: (0, 0)), tile],
        out_shape=[jax.ShapeDtypeStruct((T, D), F32), jax.ShapeDtypeStruct((1, D), F32),
                   jax.ShapeDtypeStruct((1, 128), F32), jax.ShapeDtypeStruct((T, D), BF16)],
        scratch_shapes=[], vmem_mib=32, args=(x3, g, target))


def _pair_add(full, recv, name):
    K, _, _, rows, D = full.shape

    def body(c_ref, a_ref, b_ref, o_ref):
        o_ref[...] = (a_ref[...].astype(F32) + b_ref[...].astype(F32)).astype(BF16)

    c = lax.axis_index("c").astype(jnp.int32).reshape((1,))
    return _call(
        body, name=name, grid=(K, N_CHIP), num_scalar_prefetch=1,
        in_specs=[pl.BlockSpec((None, None, None, rows, D), lambda k, q, c_ref: (k, q, c_ref[0], 0, 0)),
                  pl.BlockSpec((None, None, rows, D), lambda k, q, c_ref: (k, q, 0, 0))],
        out_specs=pl.BlockSpec((None, None, rows, D), lambda k, q, c_ref: (k, q, 0, 0)),
        out_shape=jax.ShapeDtypeStruct(recv.shape, BF16),
        scratch_shapes=[], vmem_mib=16, args=(c, full, recv))


def _adamw_update(wv, gv, mv, vv):
    m2 = ADAM_B1 * mv + (1.0 - ADAM_B1) * gv
    v2 = ADAM_B2 * vv + (1.0 - ADAM_B2) * (gv * gv)
    m_hat = m2 / (1.0 - ADAM_B1 ** ADAM_STEP)
    v_hat = v2 / (1.0 - ADAM_B2 ** ADAM_STEP)
    return -ADAM_LR * (m_hat / (jnp.sqrt(v_hat) + ADAM_EPS) + ADAM_WD * wv), m2, v2


def _finish(parts, k, w, m, v, transpose, name):
    _, _, rows, D = parts.shape

    def body(p_ref, w_ref, m_ref, v_ref, g_ref, d_ref, mo_ref, vo_ref):
        acc = p_ref[0].astype(F32)
        for q in range(1, N_CHIP):
            acc = acc + p_ref[q].astype(F32)
        gv = acc.T if transpose else acc
        g_ref[...] = gv
        d_ref[...], mo_ref[...], vo_ref[...] = _adamw_update(w_ref[...], gv, m_ref[...], v_ref[...])

    whole = pl.BlockSpec(w.shape, lambda i: (0, 0))
    return _call(
        body, name=name, grid=(1,),
        in_specs=[pl.BlockSpec((None, N_CHIP, rows, D), lambda i: (k, 0, 0, 0)), whole, whole, whole],
        out_specs=[whole] * 4, out_shape=[jax.ShapeDtypeStruct(w.shape, F32)] * 4,
        scratch_shapes=[], vmem_mib=40, args=(parts, w, m, v))


def _adamw(w, g, m, v, name):
    def body(w_ref, g_ref, m_ref, v_ref, d_ref, mo_ref, vo_ref):
        d_ref[...], mo_ref[...], vo_ref[...] = _adamw_update(w_ref[...], g_ref[...], m_ref[...], v_ref[...])

    shape = jax.ShapeDtypeStruct(w.shape, F32)
    return pl.pallas_call(
        body, name=name,
        in_specs=[VMEM_SPEC] * 4, out_specs=[VMEM_SPEC] * 3, out_shape=[shape] * 3,
        compiler_params=pltpu.CompilerParams(vmem_limit_bytes=32 * MIB),
    )(w, g, m, v)


def _block_diag(w):
    h, d, _ = w.shape
    out = jnp.zeros((h * d, h * d), w.dtype)
    for k in range(h):
        out = lax.dynamic_update_slice(out, w[k], (k * d, k * d))
    return out


def _diag_blocks(m, h):
    d = m.shape[0] // h
    return jnp.stack([m[k * d:(k + 1) * d, k * d:(k + 1) * d] for k in range(h)])


def _reduce_level1(full, tag):
    got = _run_comm(_sibling_comm(full), "rs_sibling_" + tag)
    return [_pair_add(a, b, "rs_pair_add_%s%d" % (tag, n)) for n, (a, b) in enumerate(zip(full, got))]


def kernel(x, ffn1_norm, ffn1_w_gate, ffn1_w_up, ffn1_w_down, mix_norm, w_in, conv_dw, conv_dw_bias, conv_ln_g, conv_ln_b, lru_conv_w, lru_conv_b, lru_w_a, lru_b_a, lru_w_x, lru_b_x, lru_lambda, w_out, ffn2_norm, ffn2_w_gate, ffn2_w_up, ffn2_w_down, final_norm, loss_target, m_ffn1_norm, m_ffn1_w_gate, m_ffn1_w_up, m_ffn1_w_down, m_mix_norm, m_w_in, m_conv_dw, m_conv_dw_bias, m_conv_ln_g, m_conv_ln_b, m_lru_conv_w, m_lru_conv_b, m_lru_w_a, m_lru_b_a, m_lru_w_x, m_lru_b_x, m_lru_lambda, m_w_out, m_ffn2_norm, m_ffn2_w_gate, m_ffn2_w_up, m_ffn2_w_down, m_final_norm, v_ffn1_norm, v_ffn1_w_gate, v_ffn1_w_up, v_ffn1_w_down, v_mix_norm, v_w_in, v_conv_dw, v_conv_dw_bias, v_conv_ln_g, v_conv_ln_b, v_lru_conv_w, v_lru_conv_b, v_lru_w_a, v_lru_b_a, v_lru_w_x, v_lru_b_x, v_lru_lambda, v_w_out, v_ffn2_norm, v_ffn2_w_gate, v_ffn2_w_up, v_ffn2_w_down, v_final_norm):
    T, D = x.shape[1], x.shape[2]
    F = ffn1_w_down.shape[0] * N_DEV
    rf = ffn1_w_down.shape[0]
    ri = w_in.shape[1]
    ro = w_out.shape[0]
    W = conv_dw_bias.shape[0]
    wc = conv_dw.shape[1]
    H = lru_w_a.shape[0]
    xs = x.reshape(T, D)
    tgt = loss_target.reshape(T, D)
    tm_ffn = min(256, T)
    cf = 256
    tm_w = min(1024, T)
    tm_mix = min(256, T)
    tf_w = F // 2
    row = lambda v: v.reshape(1, -1)
    by_owner = lambda a, rows: a.reshape(a.shape[0], N_CHIP, 2, rows, D)

    p3a = jnp.stack([ffn1_w_gate.T, ffn1_w_up.T, ffn1_w_down]).astype(BF16)
    p3b = jnp.stack([ffn2_w_gate.T, ffn2_w_up.T, ffn2_w_down]).astype(BF16)
    p_in = w_in.T.astype(BF16)
    p_out = w_out.astype(BF16)
    p_cw = jnp.concatenate([conv_dw, lru_conv_w, jnp.zeros((40 - CONV_K - LRU_K, wc), F32)], axis=0)
    stacked = lambda r, j: r.at[:, j]
    plain = lambda r, j: r.at[j]
    g3_shape = jax.ShapeDtypeStruct((3, N_DEV, rf, D), BF16)
    (g3a,) = _all_gather([p3a], [stacked], [g3_shape], "ag_ffn1")
    w3a = g3a.reshape(3, F, D)
    bda = _block_diag(lru_w_a).astype(BF16)
    bdx = _block_diag(lru_w_x).astype(BF16)
    vec = jnp.concatenate([jnp.stack([conv_dw_bias, conv_ln_g, conv_ln_b, lru_conv_b, lru_b_a, lru_b_x, lru_lambda]),
                           jnp.zeros((9, W), F32)], axis=0)

    gather_rest = _gather_comm(
        [p3b, p_in, p_out, p_cw], [stacked, plain, plain, plain],
        [g3_shape, jax.ShapeDtypeStruct((N_DEV, ri, D), BF16), jax.ShapeDtypeStruct((N_DEV, ro, D), BF16),
         jax.ShapeDtypeStruct((N_DEV, 40, wc), F32)],
        [(SIBLING,) + SAME_CORE, EVERYONE, EVERYONE, EVERYONE])
    (x1, h1, gate1, up1), (g3b_half, g_in, g_out, g_cw) = _ffn_fwd(
        xs, row(ffn1_norm), w3a, tm_ffn, cf, "ffn1_fwd", rider=gather_rest)
    w_in_t = g_in.reshape(N_DEV * ri, D)
    w_out_f = g_out.reshape(N_DEV * ro, D)
    cw_all = jnp.transpose(g_cw, (1, 0, 2)).reshape(40, N_DEV * wc)
    cw = cw_all[0:CONV_K]
    lw = cw_all[CONV_K:CONV_K + LRU_K]
    (x2, z, mix, u1, xr, hst), (g3b,) = _mix_core_fwd(
        x1, row(mix_norm), w_in_t, w_out_f, bda, bdx, cw, lw, vec, tm_mix, "mix_core_fwd",
        rider=_forward_comm([g3b_half], [stacked]))
    w3b = g3b.reshape(3, F, D)
    x3, h3, gate2, up2 = _ffn_fwd(x2, row(ffn2_norm), w3b, tm_ffn, cf, "ffn2_fwd")
    dx3, d_final_norm, loss_part, dob2 = _final_loss(x3, row(final_norm), tgt, min(512, T), "final_loss")

    dx2, dgate2, dup2, d_ffn2_norm = _ffn_dgrad(dx3, x2, row(ffn2_norm), gate2, up2, w3b, tm_ffn, cf, "ffn2_dgrad")
    (dw_gu2,) = _wgrad([dgate2, dup2], h3, tm_w, tf_w, "ffn2_wgrad_gu")
    (dw_d2,) = _wgrad([gate2, up2], dob2, tm_w, tf_w, "ffn2_wgrad_d", swiglu=True)
    sums_f2 = _reduce_level1([by_owner(dw_gu2, rf), by_owner(dw_d2, rf)], "f2")
    (dz, sg, dbda, dbdx), parts_f2 = _mix_core_bwd(
        dx2, z, u1, xr, hst, w_out_f, bda, bdx, cw, lw, vec, tm_mix, "mix_core_bwd", rider=_chips_comm(sums_f2))
    dx1, d_mix_norm, dw_in_t, dw_out, dob1 = _mix_in_bwd(dz, x1, dx2, mix, row(mix_norm), w_in_t, tm_mix, "mix_in_bwd")
    (dw_d1,) = _wgrad([gate1, up1], dob1, tm_w, tf_w, "ffn1_wgrad_d", swiglu=True)
    sums_mix = _reduce_level1([by_owner(dw_in_t[None], ri), by_owner(dw_out[None], ro), by_owner(dw_d1, rf)], "mix")
    dx0, dgate1, dup1, d_ffn1_norm = _ffn_dgrad(dx1, xs, row(ffn1_norm), gate1, up1, w3a, tm_ffn, cf, "ffn1_dgrad")

    d_lru_w_a = _diag_blocks(dbda, H)
    d_lru_w_x = _diag_blocks(dbdx, H)
    small = [d_ffn1_norm, d_mix_norm, d_ffn2_norm, d_final_norm,
             sg[G_CB], sg[G_LNG], sg[G_LNB], sg[G_LCB], sg[G_BA], sg[G_BX], sg[G_LAM],
             d_lru_w_a, d_lru_w_x,
             sg[G_CW:G_CW + CONV_K], sg[G_LW:G_LW + LRU_K],
             loss_part]
    sizes = [a.size for a in small]
    flat = jnp.concatenate([a.reshape(-1) for a in small])
    n_rows = -(-flat.size // (8 * D)) * 8
    packed = jnp.concatenate([flat, jnp.zeros((n_rows * D - flat.size,), F32)]).reshape(n_rows, D)
    (dw_g1,), (red, *parts_mix) = _wgrad(
        [dgate1], h1, tm_w, tf_w, "ffn1_wgrad_g", rider=_both(_small_sum_comm(packed), _chips_comm(sums_mix)))
    sums_g1 = _reduce_level1([by_owner(dw_g1, rf)], "g1")
    (dw_u1,), parts_g1 = _wgrad([dup1], h1, tm_w, tf_w, "ffn1_wgrad_u", rider=_chips_comm(sums_g1))
    sums_u1 = _reduce_level1([by_owner(dw_u1, rf)], "u1")
    parts_u1 = _run_comm(_chips_comm(sums_u1), "rs_chips_u1")

    red = red.reshape(-1)
    offs = [0]
    for n in sizes:
        offs.append(offs[-1] + n)
    piece = lambda k, shape: red[offs[k]:offs[k + 1]].reshape(shape)
    loss = red[offs[15]]

    me = 4 * lax.axis_index("x") + 2 * lax.axis_index("y") + lax.axis_index("c")
    chan = lambda full_g: lax.dynamic_slice_in_dim(full_g, me * wc, wc, axis=1)
    grads = {
        "ffn1_norm": piece(0, (D,)), "mix_norm": piece(1, (D,)), "ffn2_norm": piece(2, (D,)), "final_norm": piece(3, (D,)),
        "conv_dw_bias": piece(4, (W,)), "conv_ln_g": piece(5, (W,)), "conv_ln_b": piece(6, (W,)),
        "lru_conv_b": piece(7, (W,)), "lru_b_a": piece(8, (W,)), "lru_b_x": piece(9, (W,)), "lru_lambda": piece(10, (W,)),
        "lru_w_a": piece(11, lru_w_a.shape), "lru_w_x": piece(12, lru_w_x.shape),
        "conv_dw": chan(piece(13, (CONV_K, W))), "lru_conv_w": chan(piece(14, (LRU_K, W))),
    }

    weights = dict(ffn1_norm=ffn1_norm, ffn1_w_gate=ffn1_w_gate, ffn1_w_up=ffn1_w_up, ffn1_w_down=ffn1_w_down, mix_norm=mix_norm, w_in=w_in, conv_dw=conv_dw, conv_dw_bias=conv_dw_bias, conv_ln_g=conv_ln_g, conv_ln_b=conv_ln_b, lru_conv_w=lru_conv_w, lru_conv_b=lru_conv_b, lru_w_a=lru_w_a, lru_b_a=lru_b_a, lru_w_x=lru_w_x, lru_b_x=lru_b_x, lru_lambda=lru_lambda, w_out=w_out, ffn2_norm=ffn2_norm, ffn2_w_gate=ffn2_w_gate, ffn2_w_up=ffn2_w_up, ffn2_w_down=ffn2_w_down, final_norm=final_norm)
    moment1 = dict(ffn1_norm=m_ffn1_norm, ffn1_w_gate=m_ffn1_w_gate, ffn1_w_up=m_ffn1_w_up, ffn1_w_down=m_ffn1_w_down, mix_norm=m_mix_norm, w_in=m_w_in, conv_dw=m_conv_dw, conv_dw_bias=m_conv_dw_bias, conv_ln_g=m_conv_ln_g, conv_ln_b=m_conv_ln_b, lru_conv_w=m_lru_conv_w, lru_conv_b=m_lru_conv_b, lru_w_a=m_lru_w_a, lru_b_a=m_lru_b_a, lru_w_x=m_lru_w_x, lru_b_x=m_lru_b_x, lru_lambda=m_lru_lambda, w_out=m_w_out, ffn2_norm=m_ffn2_norm, ffn2_w_gate=m_ffn2_w_gate, ffn2_w_up=m_ffn2_w_up, ffn2_w_down=m_ffn2_w_down, final_norm=m_final_norm)
    moment2 = dict(ffn1_norm=v_ffn1_norm, ffn1_w_gate=v_ffn1_w_gate, ffn1_w_up=v_ffn1_w_up, ffn1_w_down=v_ffn1_w_down, mix_norm=v_mix_norm, w_in=v_w_in, conv_dw=v_conv_dw, conv_dw_bias=v_conv_dw_bias, conv_ln_g=v_conv_ln_g, conv_ln_b=v_conv_ln_b, lru_conv_w=v_lru_conv_w, lru_conv_b=v_lru_conv_b, lru_w_a=v_lru_w_a, lru_b_a=v_lru_b_a, lru_w_x=v_lru_w_x, lru_b_x=v_lru_b_x, lru_lambda=v_lru_lambda, w_out=v_w_out, ffn2_norm=v_ffn2_norm, ffn2_w_gate=v_ffn2_w_gate, ffn2_w_up=v_ffn2_w_up, ffn2_w_down=v_ffn2_w_down, final_norm=v_final_norm)
    order = list(weights)
    big = {"ffn1_w_gate": (parts_g1[0], 0, True), "ffn1_w_up": (parts_u1[0], 0, True),
           "ffn1_w_down": (parts_mix[2], 0, False), "w_in": (parts_mix[0], 0, True), "w_out": (parts_mix[1], 0, False),
           "ffn2_w_gate": (parts_f2[0], 0, True), "ffn2_w_up": (parts_f2[0], 1, True),
           "ffn2_w_down": (parts_f2[1], 0, False)}
    delta, new_m, new_v = {}, {}, {}
    for n, (parts, k, transposed) in big.items():
        grads[n], delta[n], new_m[n], new_v[n] = _finish(
            parts, k, weights[n], moment1[n], moment2[n], transposed, "finish_" + n)
    rest = [n for n in order if n not in big]
    rest_sizes = [weights[n].size for n in rest]
    n_flat = sum(rest_sizes)
    pad_rows = -(-n_flat // (8 * 128)) * 8

    def pack(d):
        f = jnp.concatenate([d[n].reshape(-1) for n in rest])
        return jnp.concatenate([f, jnp.zeros((pad_rows * 128 - n_flat,), F32)]).reshape(pad_rows, 128)

    d_s, m_s, v_s = _adamw(pack(weights), pack(grads), pack(moment1), pack(moment2), "adamw_small")
    off = 0
    for n, size in zip(rest, rest_sizes):
        shape = weights[n].shape
        delta[n] = d_s.reshape(-1)[off:off + size].reshape(shape)
        new_m[n] = m_s.reshape(-1)[off:off + size].reshape(shape)
        new_v[n] = v_s.reshape(-1)[off:off + size].reshape(shape)
        off += size

    return (loss, dx0.reshape(x.shape), *[grads[n] for n in order], *[delta[n] for n in order],
            *[new_m[n] for n in order], *[new_v[n] for n in order])
```

```python
import functools
import math

import jax
import jax.numpy as jnp
from jax import lax
from jax.experimental import pallas as pl
from jax.experimental.pallas import tpu as pltpu

F32 = jnp.float32
BF16 = jnp.bfloat16
MESH = pl.DeviceIdType.MESH

N_DEV = 8
N_CHIP = 4
SUBLANES = 8
RMS_EPS = 1e-6
LN_EPS = 1e-5
LRU_C = 8.0
CONV_K = 31
LRU_K = 4
CONV_HALO = 32
LRU_HALO = 8
FFN_RES = 0.5
ADAM_LR, ADAM_B1, ADAM_B2, ADAM_EPS, ADAM_WD, ADAM_STEP = 0.001, 0.9, 0.999, 1e-08, 0.01, 10
GELU_K = math.sqrt(2.0 / math.pi)
GELU_C = 0.044715

MIB = 1024 * 1024
NT = (((1,), (1,)), ((), ()))
NN = (((1,), (0,)), ((), ()))
TN = (((0,), (0,)), ((), ()))

V_CB, V_LNG, V_LNB, V_LCB, V_BA, V_BX, V_LAM = range(0, 7 * SUBLANES, SUBLANES)
G_CW = 0
G_CB, G_LNG, G_LNB = 31, 32, 33
G_LW = 34
G_LCB, G_BA, G_BX, G_LAM = 38, 39, 40, 41
G_ROWS = 48

HBM_SPEC = pl.BlockSpec(memory_space=pltpu.HBM)
VMEM_SPEC = pl.BlockSpec(memory_space=pltpu.VMEM)


def _dot(a, b, dims):
    return lax.dot_general(a, b, dims, preferred_element_type=F32)


def _sigmoid(x):
    return 1.0 / (1.0 + jnp.exp(-x))


def _gelu_parts(x):
    x2 = x * x
    th = jnp.tanh(GELU_K * x * (1.0 + GELU_C * x2))
    gl = 0.5 * x * (1.0 + th)
    dgl = 0.5 * (1.0 + th) + 0.5 * x * (1.0 - th * th) * GELU_K * (1.0 + 3.0 * GELU_C * x2)
    return gl, dgl


def _neg_expm1(y):
    series = -y * (1.0 + y * (1.0 / 2) * (1.0 + y * (1.0 / 3) * (1.0 + y * (1.0 / 4) * (1.0 + y * (1.0 / 5) * (1.0 + y * (1.0 / 6))))))
    return jnp.where(y > -0.25, series, 1.0 - jnp.exp(y))


def _softplus_neg(lam):
    t = -lam
    e = jnp.exp(-jnp.abs(t))
    s = 1.0 + e
    log1p_e = jnp.log(s) - ((s - 1.0) - e) / s
    return jnp.maximum(t, 0.0) + log1p_e


def _rms_stats(xv):
    rstd = lax.rsqrt(jnp.mean(xv * xv, axis=-1, keepdims=True) + RMS_EPS)
    return xv * rstd, rstd


def _rms_bwd(xhat, rstd, g, dh):
    dxhat = dh * g
    dx = rstd * (dxhat - xhat * jnp.mean(dxhat * xhat, axis=-1, keepdims=True))
    return dx, jnp.sum(dh * xhat, axis=0, keepdims=True)


def _row_windows(buf_ref, n_rows, offsets):
    total = buf_ref.shape[0]
    full = buf_ref[...]
    for b in range(SUBLANES):
        offs = [o for o in offsets if o % SUBLANES == b]
        if not offs:
            continue
        assert max(offs) + n_rows <= total
        moved = full if b == 0 else pltpu.roll(full, total - b, 0)
        for o in offs:
            yield o, moved[o - b:o - b + n_rows, :]


def _scan_rows(av, bv, edge, out_ref, reverse=False):
    tm, W = av.shape
    sub = lax.broadcasted_iota(jnp.int32, (tm, W), 0) % SUBLANES
    s = 1
    while s < SUBLANES:
        keep = (sub < SUBLANES - s) if reverse else (sub >= s)
        shift = tm - s if reverse else s
        bv = jnp.where(keep, av * pltpu.roll(bv, shift, 0) + bv, bv)
        av = jnp.where(keep, av * pltpu.roll(av, shift, 0), av)
        s *= 2
    starts = range(0, tm, SUBLANES)
    for r0 in (reversed(starts) if reverse else starts):
        group = av[r0:r0 + SUBLANES, :] * edge + bv[r0:r0 + SUBLANES, :]
        out_ref[r0:r0 + SUBLANES, :] = group
        edge = group[0:1, :] if reverse else group[SUBLANES - 1:SUBLANES, :]
    return edge


class _Comm:
    def __init__(self, arrays, in_specs, out_shapes, out_specs, scratch, start, wait, aliases=None):
        self.arrays, self.in_specs = list(arrays), list(in_specs)
        self.out_shapes, self.out_specs = list(out_shapes), list(out_specs)
        self.scratch, self.start, self.wait = list(scratch), start, wait
        self.aliases = dict(aliases or {})


def _call(body, *, name, grid, in_specs, out_specs, out_shape, scratch_shapes, vmem_mib, args, rider=None,
          num_scalar_prefetch=0):
    params = pltpu.CompilerParams(dimension_semantics=("arbitrary",) * len(grid), vmem_limit_bytes=vmem_mib * MIB)
    if rider is None:
        return pl.pallas_call(
            body, name=name,
            grid_spec=pltpu.PrefetchScalarGridSpec(
                num_scalar_prefetch=num_scalar_prefetch, grid=grid, in_specs=in_specs, out_specs=out_specs,
                scratch_shapes=scratch_shapes),
            out_shape=out_shape, compiler_params=params)(*args)
    assert num_scalar_prefetch == 0
    n_in, n_out, n_scr = len(in_specs), len(out_specs), len(scratch_shapes)
    r_in, r_out = len(rider.arrays), len(rider.out_shapes)
    n_axes = len(grid)

    def carried(*refs):
        pos = [0]

        def take(n):
            pos[0] += n
            return refs[pos[0] - n:pos[0]]

        ins, r_ins, outs, r_outs, scr, r_scr = take(n_in), take(r_in), take(n_out), take(r_out), take(n_scr), take(len(rider.scratch))
        first = pl.program_id(0) == 0
        last = pl.program_id(0) == grid[0] - 1
        for ax in range(1, n_axes):
            first = first & (pl.program_id(ax) == 0)
            last = last & (pl.program_id(ax) == grid[ax] - 1)

        @pl.when(first)
        def _():
            rider.start(r_ins, r_outs, r_scr)

        body(*ins, *outs, *scr)

        @pl.when(last)
        def _():
            rider.wait(r_ins, r_outs, r_scr)

    res = pl.pallas_call(
        carried, name=name,
        grid=grid,
        in_specs=list(in_specs) + rider.in_specs,
        out_specs=list(out_specs) + rider.out_specs,
        out_shape=list(out_shape) + rider.out_shapes,
        scratch_shapes=list(scratch_shapes) + rider.scratch,
        input_output_aliases={n_in + i: n_out + o for i, o in rider.aliases.items()},
        compiler_params=params)(*args, *rider.arrays)
    return res[:n_out], res[n_out:]


def _run_comm(comm, name):
    n_in, n_out = len(comm.arrays), len(comm.out_shapes)

    def body(*refs):
        ins, outs, scr = refs[:n_in], refs[n_in:n_in + n_out], refs[n_in + n_out:]
        comm.start(ins, outs, scr)
        comm.wait(ins, outs, scr)

    return pl.pallas_call(
        body, name=name,
        in_specs=comm.in_specs, out_specs=comm.out_specs, out_shape=comm.out_shapes,
        scratch_shapes=comm.scratch, input_output_aliases=comm.aliases,
        compiler_params=pltpu.CompilerParams(vmem_limit_bytes=24 * MIB))(*comm.arrays)


def _both(a, b):
    ni, no, ns = len(a.arrays), len(a.out_shapes), len(a.scratch)

    def start(ins, outs, scr):
        a.start(ins[:ni], outs[:no], scr[:ns])
        b.start(ins[ni:], outs[no:], scr[ns:])

    def wait(ins, outs, scr):
        a.wait(ins[:ni], outs[:no], scr[:ns])
        b.wait(ins[ni:], outs[no:], scr[ns:])

    aliases = dict(a.aliases)
    aliases.update({ni + i: no + o for i, o in b.aliases.items()})
    return _Comm(a.arrays + b.arrays, a.in_specs + b.in_specs, a.out_shapes + b.out_shapes,
                 a.out_specs + b.out_specs, a.scratch + b.scratch, start, wait, aliases)


def _place():
    return lax.axis_index("x"), lax.axis_index("y"), lax.axis_index("c")


def _peer(k):
    x, y, c = _place()
    px, py, pc = x ^ ((k >> 2) & 1), y ^ ((k >> 1) & 1), c ^ (k & 1)
    return (px, py, pc), 4 * px + 2 * py + pc


SIBLING = 1
SAME_CORE = (2, 4, 6)
EVERYONE = tuple(range(1, N_DEV))


def _gather_comm(shards, views, out_shapes, relations):
    na = len(shards)

    def copies(ins, outs, scr):
        send_sems, recv_sems, _ = scr
        _, me = _peer(0)
        out = []
        for a in range(na):
            for k in relations[a]:
                peer, theirs = _peer(k)
                send = functools.partial(
                    pltpu.make_async_remote_copy,
                    src_ref=ins[a], dst_ref=views[a](outs[a], me),
                    send_sem=send_sems.at[7 * a + k - 1], recv_sem=recv_sems.at[7 * a + k - 1],
                    device_id=peer, device_id_type=MESH)
                recv = functools.partial(
                    pltpu.make_async_remote_copy,
                    src_ref=ins[a], dst_ref=views[a](outs[a], theirs),
                    send_sem=send_sems.at[7 * a + k - 1], recv_sem=recv_sems.at[7 * a + k - 1],
                    device_id=peer, device_id_type=MESH)
                out.append((send, recv))
        return out

    def local(ins, outs, scr):
        _, me = _peer(0)
        return [pltpu.make_async_copy(ins[a], views[a](outs[a], me), scr[2].at[a]) for a in range(na)]

    def start(ins, outs, scr):
        for cp in local(ins, outs, scr):
            cp.start()
        for send, _ in copies(ins, outs, scr):
            send().start()

    def wait(ins, outs, scr):
        for _, recv in copies(ins, outs, scr):
            recv().wait_recv()
        for send, _ in copies(ins, outs, scr):
            send().wait_send()
        for cp in local(ins, outs, scr):
            cp.wait()

    return _Comm(shards, [HBM_SPEC] * na, out_shapes, [HBM_SPEC] * na,
                 [pltpu.SemaphoreType.DMA((7 * na,)), pltpu.SemaphoreType.DMA((7 * na,)),
                  pltpu.SemaphoreType.DMA((na,))], start, wait)


def _forward_comm(gathered, views):
    na = len(gathered)
    shapes = [jax.ShapeDtypeStruct(g.shape, g.dtype) for g in gathered]

    def copies(outs, scr):
        send_sems, recv_sems = scr
        sibling, _ = _peer(SIBLING)
        out = []
        for a in range(na):
            for n, k in enumerate(SAME_CORE):
                _, mine = _peer(k)
                _, theirs = _peer(k ^ SIBLING)
                send = functools.partial(
                    pltpu.make_async_remote_copy,
                    src_ref=views[a](outs[a], mine), dst_ref=views[a](outs[a], mine),
                    send_sem=send_sems.at[3 * a + n], recv_sem=recv_sems.at[3 * a + n],
                    device_id=sibling, device_id_type=MESH)
                recv = functools.partial(
                    pltpu.make_async_remote_copy,
                    src_ref=views[a](outs[a], mine), dst_ref=views[a](outs[a], theirs),
                    send_sem=send_sems.at[3 * a + n], recv_sem=recv_sems.at[3 * a + n],
                    device_id=sibling, device_id_type=MESH)
                out.append((send, recv))
        return out

    def start(ins, outs, scr):
        for send, _ in copies(outs, scr):
            send().start()

    def wait(ins, outs, scr):
        for _, recv in copies(outs, scr):
            recv().wait_recv()
        for send, _ in copies(outs, scr):
            send().wait_send()

    return _Comm(gathered, [HBM_SPEC] * na, shapes, [HBM_SPEC] * na,
                 [pltpu.SemaphoreType.DMA((3 * na,)), pltpu.SemaphoreType.DMA((3 * na,))], start, wait,
                 aliases={a: a for a in range(na)})


def _all_gather(shards, views, out_shapes, name):
    na = len(shards)
    level1 = _gather_comm(shards, views, out_shapes, [(SIBLING,) + SAME_CORE] * na)

    def body(*refs):
        ins, outs = refs[:na], refs[na:2 * na]
        send_sems, recv_sems, local_sems, fwd_send, fwd_recv = refs[2 * na:]
        sibling, _ = _peer(SIBLING)
        scr = (send_sems, recv_sems, local_sems)
        level1.start(ins, outs, scr)
        passed, landing = [], []
        for a in range(na):
            for n, k in enumerate(SAME_CORE):
                peer, mine = _peer(k)
                _, theirs = _peer(k ^ SIBLING)
                pltpu.make_async_remote_copy(
                    src_ref=ins[a], dst_ref=views[a](outs[a], mine),
                    send_sem=send_sems.at[7 * a + k - 1], recv_sem=recv_sems.at[7 * a + k - 1],
                    device_id=peer, device_id_type=MESH).wait_recv()
                fwd = pltpu.make_async_remote_copy(
                    src_ref=views[a](outs[a], mine), dst_ref=views[a](outs[a], mine),
                    send_sem=fwd_send.at[3 * a + n], recv_sem=fwd_recv.at[3 * a + n],
                    device_id=sibling, device_id_type=MESH)
                fwd.start()
                passed.append(fwd)
                landing.append(pltpu.make_async_remote_copy(
                    src_ref=views[a](outs[a], mine), dst_ref=views[a](outs[a], theirs),
                    send_sem=fwd_send.at[3 * a + n], recv_sem=fwd_recv.at[3 * a + n],
                    device_id=sibling, device_id_type=MESH))
        for a in range(na):
            _, theirs = _peer(SIBLING)
            pltpu.make_async_remote_copy(
                src_ref=ins[a], dst_ref=views[a](outs[a], theirs),
                send_sem=send_sems.at[7 * a + SIBLING - 1], recv_sem=recv_sems.at[7 * a + SIBLING - 1],
                device_id=sibling, device_id_type=MESH).wait_recv()
        for cp in landing:
            cp.wait_recv()
        for cp in passed:
            cp.wait_send()
        _, me = _peer(0)
        for a in range(na):
            for k in (SIBLING,) + SAME_CORE:
                peer, _ = _peer(k)
                pltpu.make_async_remote_copy(
                    src_ref=ins[a], dst_ref=views[a](outs[a], me),
                    send_sem=send_sems.at[7 * a + k - 1], recv_sem=recv_sems.at[7 * a + k - 1],
                    device_id=peer, device_id_type=MESH).wait_send()
            pltpu.make_async_copy(ins[a], views[a](outs[a], me), local_sems.at[a]).wait()

    return pl.pallas_call(
        body, name=name,
        in_specs=[HBM_SPEC] * na, out_specs=[HBM_SPEC] * na, out_shape=out_shapes,
        scratch_shapes=level1.scratch + [pltpu.SemaphoreType.DMA((3 * na,)), pltpu.SemaphoreType.DMA((3 * na,))],
    )(*shards)


def _sibling_comm(grads):
    na = len(grads)
    shapes = [jax.ShapeDtypeStruct(g.shape[:2] + g.shape[3:], g.dtype) for g in grads]

    def copies(ins, outs, scr):
        x, y, c = _place()
        return [pltpu.make_async_remote_copy(
            src_ref=ins[a].at[:, :, 1 - c], dst_ref=outs[a],
            send_sem=scr[0].at[a], recv_sem=scr[1].at[a],
            device_id=(x, y, 1 - c), device_id_type=MESH) for a in range(na)]

    def start(ins, outs, scr):
        for cp in copies(ins, outs, scr):
            cp.start()

    def wait(ins, outs, scr):
        for cp in copies(ins, outs, scr):
            cp.wait()

    return _Comm(grads, [HBM_SPEC] * na, shapes, [HBM_SPEC] * na,
                 [pltpu.SemaphoreType.DMA((na,)), pltpu.SemaphoreType.DMA((na,))], start, wait)


def _chips_comm(sums):
    na = len(sums)
    shapes = [jax.ShapeDtypeStruct(s.shape, s.dtype) for s in sums]

    def copies(ins, outs, scr):
        x, y, c = _place()
        mine = 2 * x + y
        out = []
        for a in range(na):
            for n, k in enumerate(SAME_CORE):
                (px, py, pc), _ = _peer(k)
                theirs = 2 * px + py
                send = functools.partial(
                    pltpu.make_async_remote_copy,
                    src_ref=ins[a].at[:, theirs], dst_ref=outs[a].at[:, mine],
                    send_sem=scr[0].at[3 * a + n], recv_sem=scr[1].at[3 * a + n],
                    device_id=(px, py, pc), device_id_type=MESH)
                recv = functools.partial(
                    pltpu.make_async_remote_copy,
                    src_ref=ins[a].at[:, mine], dst_ref=outs[a].at[:, theirs],
                    send_sem=scr[0].at[3 * a + n], recv_sem=scr[1].at[3 * a + n],
                    device_id=(px, py, pc), device_id_type=MESH)
                out.append((send, recv))
        return out

    def local(ins, outs, scr):
        x, y, _ = _place()
        mine = 2 * x + y
        return [pltpu.make_async_copy(ins[a].at[:, mine], outs[a].at[:, mine], scr[2].at[a]) for a in range(na)]

    def start(ins, outs, scr):
        for cp in local(ins, outs, scr):
            cp.start()
        for send, _ in copies(ins, outs, scr):
            send().start()

    def wait(ins, outs, scr):
        for _, recv in copies(ins, outs, scr):
            recv().wait_recv()
        for send, _ in copies(ins, outs, scr):
            send().wait_send()
        for cp in local(ins, outs, scr):
            cp.wait()

    return _Comm(sums, [HBM_SPEC] * na, shapes, [HBM_SPEC] * na,
                 [pltpu.SemaphoreType.DMA((3 * na,)), pltpu.SemaphoreType.DMA((3 * na,)),
                  pltpu.SemaphoreType.DMA((na,))], start, wait)


def _small_sum_comm(s):
    R, C = s.shape

    def copies(ins, scr):
        buf, send_sems, recv_sems = scr
        _, me = _peer(0)
        out = []
        for k in EVERYONE:
            peer, theirs = _peer(k)
            send = functools.partial(
                    pltpu.make_async_remote_copy,
                src_ref=ins[0], dst_ref=buf.at[me], send_sem=send_sems.at[k - 1], recv_sem=recv_sems.at[k - 1],
                device_id=peer, device_id_type=MESH)
            recv = functools.partial(
                    pltpu.make_async_remote_copy,
                src_ref=ins[0], dst_ref=buf.at[theirs], send_sem=send_sems.at[k - 1], recv_sem=recv_sems.at[k - 1],
                device_id=peer, device_id_type=MESH)
            out.append((send, recv))
        return out

    def start(ins, outs, scr):
        _, me = _peer(0)
        scr[0][me] = ins[0][...]
        for send, _ in copies(ins, scr):
            send().start()

    def wait(ins, outs, scr):
        for _, recv in copies(ins, scr):
            recv().wait_recv()
        for send, _ in copies(ins, scr):
            send().wait_send()
        acc = scr[0][0]
        for j in range(1, N_DEV):
            acc = acc + scr[0][j]
        outs[0][...] = acc

    return _Comm([s], [VMEM_SPEC], [jax.ShapeDtypeStruct((R, C), F32)], [VMEM_SPEC],
                 [pltpu.VMEM((N_DEV, R, C), F32), pltpu.SemaphoreType.DMA((N_DEV - 1,)),
                  pltpu.SemaphoreType.DMA((N_DEV - 1,))], start, wait)


def _load_weights(w_hbm, w_vmem, sem):
    @pl.when(pl.program_id(0) == 0)
    def _():
        copies = [pltpu.make_async_copy(w_hbm.at[k], w_vmem.at[k], sem.at[k]) for k in range(3)]
        for cp in copies:
            cp.start()
        for cp in copies:
            cp.wait()


def _ffn_fwd(x, g, w3, tm, cf, name, rider=None):
    T, D = x.shape
    F = w3.shape[1]

    def body(x_ref, g_ref, w_hbm, xo_ref, h_ref, dau_ref, dag_ref, act_ref, wv, sem):
        _load_weights(w_hbm, wv, sem)
        xhat, _ = _rms_stats(x_ref[...])
        hb = (xhat * g_ref[...]).astype(BF16)
        h_ref[...] = hb
        for lo in range(0, F, cf):
            gate = _dot(hb, wv[0, lo:lo + cf, :], NT)
            up = _dot(hb, wv[1, lo:lo + cf, :], NT)
            sig = _sigmoid(gate)
            silu = gate * sig
            dau_ref[:, lo:lo + cf] = silu.astype(BF16)
            dag_ref[:, lo:lo + cf] = (up * (sig * (1.0 + gate * (1.0 - sig)))).astype(BF16)
            act_ref[:, lo:lo + cf] = (silu * up).astype(BF16)
        xo_ref[...] = x_ref[...] + FFN_RES * _dot(act_ref[...], wv[2], NN)

    row = pl.BlockSpec((tm, D), lambda i: (i, 0))
    hid = pl.BlockSpec((tm, F), lambda i: (i, 0))
    return _call(
        body, name=name, grid=(T // tm,),
        in_specs=[row, pl.BlockSpec((1, D), lambda i: (0, 0)), HBM_SPEC],
        out_specs=[row, row, hid, hid, hid],
        out_shape=[jax.ShapeDtypeStruct((T, D), F32), jax.ShapeDtypeStruct((T, D), BF16)]
        + [jax.ShapeDtypeStruct((T, F), BF16)] * 3,
        scratch_shapes=[pltpu.VMEM((3, F, D), BF16), pltpu.SemaphoreType.DMA((3,))],
        vmem_mib=48, args=(x, g, w3), rider=rider)


def _ffn_dgrad(dout, x, g, dau, dag, w3, tm, cf, name, rider=None):
    T, D = x.shape
    F = w3.shape[1]

    def body(do_ref, x_ref, g_ref, dau_ref, dag_ref, w_hbm, dx_ref, dgate_ref, dup_ref, dg_ref, wv, sem):
        _load_weights(w_hbm, wv, sem)

        @pl.when(pl.program_id(0) == 0)
        def _():
            dg_ref[...] = jnp.zeros_like(dg_ref)

        dob = (FFN_RES * do_ref[...]).astype(BF16)
        for lo in range(0, F, cf):
            dact = _dot(dob, wv[2, lo:lo + cf, :], NT)
            dup_ref[:, lo:lo + cf] = (dact * dau_ref[:, lo:lo + cf].astype(F32)).astype(BF16)
            dgate_ref[:, lo:lo + cf] = (dact * dag_ref[:, lo:lo + cf].astype(F32)).astype(BF16)
        dh = _dot(dgate_ref[...], wv[0], NN) + _dot(dup_ref[...], wv[1], NN)
        xhat, rstd = _rms_stats(x_ref[...])
        dx, dg = _rms_bwd(xhat, rstd, g_ref[...], dh)
        dx_ref[...] = do_ref[...] + dx
        dg_ref[...] += dg

    row = pl.BlockSpec((tm, D), lambda i: (i, 0))
    hid = pl.BlockSpec((tm, F), lambda i: (i, 0))
    vec = pl.BlockSpec((1, D), lambda i: (0, 0))
    return _call(
        body, name=name, grid=(T // tm,),
        in_specs=[row, row, vec, hid, hid, HBM_SPEC],
        out_specs=[row, hid, hid, vec],
        out_shape=[jax.ShapeDtypeStruct((T, D), F32), jax.ShapeDtypeStruct((T, F), BF16),
                   jax.ShapeDtypeStruct((T, F), BF16), jax.ShapeDtypeStruct((1, D), F32)],
        scratch_shapes=[pltpu.VMEM((3, F, D), BF16), pltpu.SemaphoreType.DMA((3,))],
        vmem_mib=52, args=(dout, x, g, dau, dag, w3), rider=rider)


def _wgrad(lhs, rhs, tm, tf, name, rider=None):
    T, F = lhs[0].shape
    D = rhs.shape[1]
    K = len(lhs)

    def body(*refs):
        lhs_refs, rhs_ref, dw_ref, accs = refs[:K], refs[K], refs[K + 1], refs[K + 2:]
        i = pl.program_id(1)

        @pl.when(i == 0)
        def _():
            for acc in accs:
                acc[...] = jnp.zeros_like(acc)

        rv = rhs_ref[...]
        for acc, lhs_ref in zip(accs, lhs_refs):
            acc[...] += _dot(lhs_ref[...], rv, TN)

        @pl.when(i == pl.num_programs(1) - 1)
        def _():
            for k, acc in enumerate(accs):
                dw_ref[k] = acc[...].astype(BF16)

    hid = pl.BlockSpec((tm, tf), lambda f, i: (i, f))
    return _call(
        body, name=name, grid=(F // tf, T // tm),
        in_specs=[hid] * K + [pl.BlockSpec((tm, D), lambda f, i: (i, 0))],
        out_specs=[pl.BlockSpec((K, tf, D), lambda f, i: (0, f, 0))],
        out_shape=[jax.ShapeDtypeStruct((K, F, D), BF16)],
        scratch_shapes=[pltpu.VMEM((tf, D), F32)] * K,
        vmem_mib=56, args=(*lhs, rhs), rider=rider)


def _lru_gates(xr, bda_ref, bdx_ref, vec_ref):
    xrb = xr.astype(BF16)
    r = _sigmoid(_dot(xrb, bda_ref[...], NN) + vec_ref[V_BA:V_BA + 1, :])
    ig = _sigmoid(_dot(xrb, bdx_ref[...], NN) + vec_ref[V_BX:V_BX + 1, :])
    sp = _softplus_neg(vec_ref[V_LAM:V_LAM + 1, :])
    log_a = (-LRU_C * sp) * r
    a = jnp.exp(log_a)
    mult = jnp.sqrt(_neg_expm1(2.0 * log_a))
    return xrb, r, ig, sp, a, mult


def _layernorm_stats(u1):
    xc = u1 - jnp.mean(u1, axis=-1, keepdims=True)
    rs = lax.rsqrt(jnp.mean(xc * xc, axis=-1, keepdims=True) + LN_EPS)
    return xc * rs, rs


def _mix_core_fwd(x1, g, w_in_t, w_out, bda, bdx, cw, lw, vec, tm, name, rider=None):
    T, D = x1.shape
    W = cw.shape[1]
    assert tm >= CONV_HALO and w_in_t.shape[0] == 4 * W

    def body(x1_ref, g_ref, wi_ref, wo_ref, bda_ref, bdx_ref, cw_ref, lw_ref, vec_ref,
             x2_ref, z_ref, mix_ref, u1_ref, xr_ref, hst_ref, ubuf, rbuf, hc):
        @pl.when(pl.program_id(0) == 0)
        def _():
            ubuf[0:CONV_HALO, :] = jnp.zeros((CONV_HALO, W), F32)
            rbuf[0:LRU_HALO, :] = jnp.zeros((LRU_HALO, W), F32)
            hc[...] = jnp.zeros_like(hc)

        xhat, _ = _rms_stats(x1_ref[...])
        z_ref[...] = _dot((xhat * g_ref[...]).astype(BF16), wi_ref[...], NT)

        ubuf[CONV_HALO:CONV_HALO + tm, :] = z_ref[:, 0:W] * _sigmoid(z_ref[:, W:2 * W])
        u1 = jnp.zeros((tm, W), F32) + vec_ref[V_CB:V_CB + 1, :]
        base = CONV_HALO - (CONV_K - 1)
        for off, win in _row_windows(ubuf, tm, range(base, base + CONV_K)):
            u1 = u1 + cw_ref[off - base:off - base + 1, :] * win
        ubuf[0:CONV_HALO, :] = ubuf[tm:tm + CONV_HALO, :]
        u1_ref[...] = u1
        xh, _ = _layernorm_stats(u1)
        u2 = xh * vec_ref[V_LNG:V_LNG + 1, :] + vec_ref[V_LNB:V_LNB + 1, :]
        ub = (u2 * _sigmoid(u2)).astype(BF16)
        mix_ref[:, 0:W] = ub

        rbuf[LRU_HALO:LRU_HALO + tm, :] = z_ref[:, 2 * W:3 * W]
        xr = jnp.zeros((tm, W), F32) + vec_ref[V_LCB:V_LCB + 1, :]
        for k in range(LRU_K):
            off = LRU_HALO - (LRU_K - 1) + k
            xr = xr + lw_ref[k:k + 1, :] * rbuf[off:off + tm, :]
        rbuf[0:LRU_HALO, :] = rbuf[tm:tm + LRU_HALO, :]
        xr_ref[...] = xr
        _, _, ig, _, a, mult = _lru_gates(xr, bda_ref, bdx_ref, vec_ref)
        hc[0:1, :] = _scan_rows(a, mult * (ig * xr), hc[0:1, :], hst_ref)
        gl, _ = _gelu_parts(z_ref[:, 3 * W:4 * W])
        yb = (hst_ref[...] * gl).astype(BF16)
        mix_ref[:, W:2 * W] = yb

        x2_ref[...] = x1_ref[...] + _dot(ub, wo_ref[0:W, :], NN) + _dot(yb, wo_ref[W:2 * W, :], NN)

    full = lambda a: pl.BlockSpec(a.shape, lambda i: (0,) * a.ndim)
    tile = lambda n: pl.BlockSpec((tm, n), lambda i: (i, 0))
    return _call(
        body, name=name, grid=(T // tm,),
        in_specs=[tile(D), full(g), full(w_in_t), full(w_out), full(bda), full(bdx), full(cw), full(lw), full(vec)],
        out_specs=[tile(D), tile(4 * W), tile(2 * W), tile(W), tile(W), tile(W)],
        out_shape=[jax.ShapeDtypeStruct((T, D), F32), jax.ShapeDtypeStruct((T, 4 * W), F32),
                   jax.ShapeDtypeStruct((T, 2 * W), BF16), jax.ShapeDtypeStruct((T, W), F32),
                   jax.ShapeDtypeStruct((T, W), F32), jax.ShapeDtypeStruct((T, W), F32)],
        scratch_shapes=[pltpu.VMEM((tm + CONV_HALO, W), F32), pltpu.VMEM((tm + LRU_HALO, W), F32),
                        pltpu.VMEM((8, W), F32)],
        vmem_mib=56, args=(x1, g, w_in_t, w_out, bda, bdx, cw, lw, vec), rider=rider)


def _mix_core_bwd(dx2, z, u1, xr, hst, w_out, bda, bdx, cw, lw, vec, tm, name, rider=None):
    T, D = dx2.shape
    W = cw.shape[1]
    nt = T // tm
    assert tm >= CONV_HALO and tm % CONV_HALO == 0

    def body(dx_ref, z_ref, zh_ref, u1_ref, xr_ref, h_ref, hh_ref, wo_ref, bda_ref, bdx_ref, cw_ref, lw_ref, vec_ref,
             dz_ref, sg_ref, dbda_ref, dbdx_ref, u0buf, du1buf, rxbuf, dxrbuf, gbuf, gc, spacc):
        i = pl.program_id(0)
        first = i == nt - 1
        row = lax.broadcasted_iota(jnp.int32, (tm, W), 0)

        @pl.when(i == 0)
        def _():
            sg_ref[...] = jnp.zeros_like(sg_ref)
            dbda_ref[...] = jnp.zeros_like(dbda_ref)
            dbdx_ref[...] = jnp.zeros_like(dbdx_ref)
            du1buf[tm:tm + CONV_HALO, :] = jnp.zeros((CONV_HALO, W), F32)
            dxrbuf[tm:tm + LRU_HALO, :] = jnp.zeros((LRU_HALO, W), F32)
            gc[...] = jnp.zeros_like(gc)
            spacc[...] = jnp.zeros_like(spacc)

        def accum(r, val):
            sg_ref[r:r + 1, :] += jnp.sum(val, axis=0, keepdims=True)

        dmix = _dot(dx_ref[...].astype(BF16), wo_ref[...], NT)
        d_u = dmix[:, 0:W]
        d_yr = dmix[:, W:2 * W]

        xh, rs = _layernorm_stats(u1_ref[...])
        ln_g = vec_ref[V_LNG:V_LNG + 1, :]
        u2 = xh * ln_g + vec_ref[V_LNB:V_LNB + 1, :]
        s2 = _sigmoid(u2)
        d_u2 = d_u * (s2 * (1.0 + u2 * (1.0 - s2)))
        accum(G_LNG, d_u2 * xh)
        accum(G_LNB, d_u2)
        d_xh = d_u2 * ln_g
        d_u1 = rs * (d_xh - jnp.mean(d_xh, axis=-1, keepdims=True)
                     - xh * jnp.mean(d_xh * xh, axis=-1, keepdims=True))
        accum(G_CB, d_u1)
        halo_on = jnp.where(first, 0.0, 1.0)
        u0buf[0:CONV_HALO, :] = halo_on * (zh_ref[:, 0:W] * _sigmoid(zh_ref[:, W:2 * W]))
        cv = z_ref[:, 0:W]
        sgc = _sigmoid(z_ref[:, W:2 * W])
        u0buf[CONV_HALO:CONV_HALO + tm, :] = cv * sgc
        du1buf[0:tm, :] = d_u1
        base = CONV_HALO - (CONV_K - 1)
        for off, win in _row_windows(u0buf, tm, range(base, base + CONV_K)):
            accum(G_CW + off - base, d_u1 * win)
        d_u0 = jnp.zeros((tm, W), F32)
        for off, win in _row_windows(du1buf, tm, range(0, CONV_K)):
            d_u0 = d_u0 + cw_ref[CONV_K - 1 - off:CONV_K - off, :] * win
        du1buf[tm:tm + CONV_HALO, :] = du1buf[0:CONV_HALO, :]
        dz_ref[:, 0:W] = (d_u0 * sgc).astype(BF16)
        dz_ref[:, W:2 * W] = (d_u0 * cv * (sgc * (1.0 - sgc))).astype(BF16)

        xrv = xr_ref[...]
        xrb, r, ig, sp, a, mult = _lru_gates(xrv, bda_ref, bdx_ref, vec_ref)
        h = h_ref[...]
        gl, dgl = _gelu_parts(z_ref[:, 3 * W:4 * W])
        dz_ref[:, 3 * W:4 * W] = (d_yr * h * dgl).astype(BF16)
        a_next = jnp.where(row == tm - 1, 1.0, pltpu.roll(a, tm - 1, 0))
        g_first = _scan_rows(a_next, d_yr * gl, gc[0:1, :], gbuf, reverse=True)
        g = gbuf[...]
        gc[0:1, :] = a[0:1, :] * g_first
        hprev = jnp.where(row == 0, halo_on * hh_ref[LRU_HALO - 1:LRU_HALO, :], pltpu.roll(h, 1, 0))
        d_log_a = (g * hprev) * a - (g * ig * xrv) * (a * a) / mult
        d_ig = g * mult * xrv
        d_xr = g * mult * ig
        spacc[0:1, :] += jnp.sum(d_log_a * r, axis=0, keepdims=True)
        d_pa32 = (d_log_a * (-LRU_C * sp)) * (r * (1.0 - r))
        d_px32 = d_ig * (ig * (1.0 - ig))
        accum(G_BA, d_pa32)
        accum(G_BX, d_px32)
        d_pa = d_pa32.astype(BF16)
        d_px = d_px32.astype(BF16)
        d_xr = d_xr + _dot(d_pa, bda_ref[...], NT) + _dot(d_px, bdx_ref[...], NT)
        dbda_ref[...] += _dot(xrb, d_pa, TN)
        dbdx_ref[...] += _dot(xrb, d_px, TN)
        accum(G_LCB, d_xr)
        rxbuf[0:LRU_HALO, :] = halo_on * zh_ref[CONV_HALO - LRU_HALO:CONV_HALO, 2 * W:3 * W]
        rxbuf[LRU_HALO:LRU_HALO + tm, :] = z_ref[:, 2 * W:3 * W]
        dxrbuf[0:tm, :] = d_xr
        d_rx = jnp.zeros((tm, W), F32)
        for k in range(LRU_K):
            off = LRU_HALO - (LRU_K - 1) + k
            accum(G_LW + k, d_xr * rxbuf[off:off + tm, :])
            d_rx = d_rx + lw_ref[k:k + 1, :] * dxrbuf[LRU_K - 1 - k:LRU_K - 1 - k + tm, :]
        dxrbuf[tm:tm + LRU_HALO, :] = dxrbuf[0:LRU_HALO, :]
        dz_ref[:, 2 * W:3 * W] = d_rx.astype(BF16)

        @pl.when(first)
        def _():
            lam = vec_ref[V_LAM:V_LAM + 1, :]
            sg_ref[G_LAM:G_LAM + 1, :] = LRU_C * _sigmoid(-lam) * spacc[0:1, :]

    full = lambda a: pl.BlockSpec(a.shape, lambda i: (0,) * a.ndim)
    tile = lambda n: pl.BlockSpec((tm, n), lambda i: (nt - 1 - i, 0))
    halo = lambda rows, n: pl.BlockSpec(
        (rows, n), lambda i: (jnp.maximum((nt - 1 - i) * (tm // rows) - 1, 0), 0))
    return _call(
        body, name=name, grid=(nt,),
        in_specs=[tile(D), tile(4 * W), halo(CONV_HALO, 4 * W), tile(W), tile(W), tile(W), halo(LRU_HALO, W),
                  full(w_out), full(bda), full(bdx), full(cw), full(lw), full(vec)],
        out_specs=[tile(4 * W), pl.BlockSpec((G_ROWS, W), lambda i: (0, 0)),
                   pl.BlockSpec((W, W), lambda i: (0, 0)), pl.BlockSpec((W, W), lambda i: (0, 0))],
        out_shape=[jax.ShapeDtypeStruct((T, 4 * W), BF16), jax.ShapeDtypeStruct((G_ROWS, W), F32),
                   jax.ShapeDtypeStruct((W, W), F32), jax.ShapeDtypeStruct((W, W), F32)],
        scratch_shapes=[pltpu.VMEM((tm + CONV_HALO, W), F32), pltpu.VMEM((tm + CONV_HALO, W), F32),
                        pltpu.VMEM((tm + LRU_HALO, W), F32), pltpu.VMEM((tm + LRU_HALO, W), F32),
                        pltpu.VMEM((tm, W), F32), pltpu.VMEM((8, W), F32), pltpu.VMEM((8, W), F32)],
        vmem_mib=56, args=(dx2, z, z, u1, xr, hst, hst, w_out, bda, bdx, cw, lw, vec), rider=rider)


def _mix_in_bwd(dz, x1, dx2, mix, g, w_in_t, tm, name, rider=None):
    T, D = x1.shape
    Z = w_in_t.shape[0]
    M = mix.shape[1]

    def body(dz_ref, x_ref, dx2_ref, mix_ref, g_ref, w_ref, dx1_ref, dg_ref, dwi_ref, dwo_ref, dob_ref, ai_ref, ao_ref):
        i = pl.program_id(0)

        @pl.when(i == 0)
        def _():
            dg_ref[...] = jnp.zeros_like(dg_ref)
            ai_ref[...] = jnp.zeros_like(ai_ref)
            ao_ref[...] = jnp.zeros_like(ao_ref)

        xhat, rstd = _rms_stats(x_ref[...])
        gv = g_ref[...]
        hb = (xhat * gv).astype(BF16)
        dzb = dz_ref[...]
        dx, dg = _rms_bwd(xhat, rstd, gv, _dot(dzb, w_ref[...], NN))
        dx2 = dx2_ref[...]
        dx1 = dx2 + dx
        dx1_ref[...] = dx1
        dob_ref[...] = (FFN_RES * dx1).astype(BF16)
        dg_ref[...] += dg
        ai_ref[...] += _dot(dzb, hb, TN)
        ao_ref[...] += _dot(mix_ref[...], dx2.astype(BF16), TN)

        @pl.when(i == pl.num_programs(0) - 1)
        def _():
            dwi_ref[...] = ai_ref[...].astype(BF16)
            dwo_ref[...] = ao_ref[...].astype(BF16)

    tile = lambda n: pl.BlockSpec((tm, n), lambda i: (i, 0))
    const = lambda r, c: pl.BlockSpec((r, c), lambda i: (0, 0))
    return _call(
        body, name=name, grid=(T // tm,),
        in_specs=[tile(Z), tile(D), tile(D), tile(M), const(1, D), const(Z, D)],
        out_specs=[tile(D), const(1, D), const(Z, D), const(M, D), tile(D)],
        out_shape=[jax.ShapeDtypeStruct((T, D), F32), jax.ShapeDtypeStruct((1, D), F32),
                   jax.ShapeDtypeStruct((Z, D), BF16), jax.ShapeDtypeStruct((M, D), BF16),
                   jax.ShapeDtypeStruct((T, D), BF16)],
        scratch_shapes=[pltpu.VMEM((Z, D), F32), pltpu.VMEM((M, D), F32)],
        vmem_mib=56, args=(dz, x1, dx2, mix, g, w_in_t), rider=rider)


def _final_loss(x3, g, target, tm, name):
    T, D = x3.shape

    def body(x_ref, g_ref, t_ref, dx_ref, dg_ref, loss_ref, dob_ref):
        @pl.when(pl.program_id(0) == 0)
        def _():
            dg_ref[...] = jnp.zeros_like(dg_ref)
            loss_ref[...] = jnp.zeros_like(loss_ref)

        xhat, rstd = _rms_stats(x_ref[...])
        gv = g_ref[...]
        err = xhat * gv - t_ref[...]
        loss_ref[...] += (0.5 / D) * jnp.sum(err * err)
        dx, dg = _rms_bwd(xhat, rstd, gv, err * (1.0 / D))
        dx_ref[...] = dx
        dob_ref[...] = (FFN_RES * dx).astype(BF16)
        dg_ref[...] += dg

    tile = pl.BlockSpec((tm, D), lambda i: (i, 0))
    return _call(
        body, name=name, grid=(T // tm,),
        in_specs=[tile, pl.BlockSpec((1, D), lambda i: (0, 0)), tile],
        out_specs=[tile, pl.BlockSpec((1, D), lambda i: (0, 0)), pl.BlockSpec((1, 128), lambda i: (0, 0)), tile],
        out_shape=[jax.ShapeDtypeStruct((T, D), F32), jax.ShapeDtypeStruct((1, D), F32),
                   jax.ShapeDtypeStruct((1, 128), F32), jax.ShapeDtypeStruct((T, D), BF16)],
        scratch_shapes=[], vmem_mib=32, args=(x3, g, target))


def _pair_add(full, recv, name):
    K, _, _, rows, D = full.shape

    def body(c_ref, a_ref, b_ref, o_ref):
        o_ref[...] = (a_ref[...].astype(F32) + b_ref[...].astype(F32)).astype(BF16)

    c = lax.axis_index("c").astype(jnp.int32).reshape((1,))
    return _call(
        body, name=name, grid=(K, N_CHIP), num_scalar_prefetch=1,
        in_specs=[pl.BlockSpec((None, None, None, rows, D), lambda k, q, c_ref: (k, q, c_ref[0], 0, 0)),
                  pl.BlockSpec((None, None, rows, D), lambda k, q, c_ref: (k, q, 0, 0))],
        out_specs=pl.BlockSpec((None, None, rows, D), lambda k, q, c_ref: (k, q, 0, 0)),
        out_shape=jax.ShapeDtypeStruct(recv.shape, BF16),
        scratch_shapes=[], vmem_mib=16, args=(c, full, recv))


def _adamw_update(wv, gv, mv, vv):
    m2 = ADAM_B1 * mv + (1.0 - ADAM_B1) * gv
    v2 = ADAM_B2 * vv + (1.0 - ADAM_B2) * (gv * gv)
    m_hat = m2 / (1.0 - ADAM_B1 ** ADAM_STEP)
    v_hat = v2 / (1.0 - ADAM_B2 ** ADAM_STEP)
    return -ADAM_LR * (m_hat / (jnp.sqrt(v_hat) + ADAM_EPS) + ADAM_WD * wv), m2, v2


def _finish(parts, k, w, m, v, transpose, name):
    _, _, rows, D = parts.shape

    def body(p_ref, w_ref, m_ref, v_ref, g_ref, d_ref, mo_ref, vo_ref):
        acc = p_ref[0].astype(F32)
        for q in range(1, N_CHIP):
            acc = acc + p_ref[q].astype(F32)
        gv = acc.T if transpose else acc
        g_ref[...] = gv
        d_ref[...], mo_ref[...], vo_ref[...] = _adamw_update(w_ref[...], gv, m_ref[...], v_ref[...])

    whole = pl.BlockSpec(w.shape, lambda i: (0, 0))
    return _call(
        body, name=name, grid=(1,),
        in_specs=[pl.BlockSpec((None, N_CHIP, rows, D), lambda i: (k, 0, 0, 0)), whole, whole, whole],
        out_specs=[whole] * 4, out_shape=[jax.ShapeDtypeStruct(w.shape, F32)] * 4,
        scratch_shapes=[], vmem_mib=40, args=(parts, w, m, v))


def _adamw(w, g, m, v, name):
    def body(w_ref, g_ref, m_ref, v_ref, d_ref, mo_ref, vo_ref):
        d_ref[...], mo_ref[...], vo_ref[...] = _adamw_update(w_ref[...], g_ref[...], m_ref[...], v_ref[...])

    shape = jax.ShapeDtypeStruct(w.shape, F32)
    return pl.pallas_call(
        body, name=name,
        in_specs=[VMEM_SPEC] * 4, out_specs=[VMEM_SPEC] * 3, out_shape=[shape] * 3,
        compiler_params=pltpu.CompilerParams(vmem_limit_bytes=32 * MIB),
    )(w, g, m, v)


def _block_diag(w):
    h, d, _ = w.shape
    out = jnp.zeros((h * d, h * d), w.dtype)
    for k in range(h):
        out = lax.dynamic_update_slice(out, w[k], (k * d, k * d))
    return out


def _diag_blocks(m, h):
    d = m.shape[0] // h
    return jnp.stack([m[k * d:(k + 1) * d, k * d:(k + 1) * d] for k in range(h)])


def _reduce_level1(full, tag):
    got = _run_comm(_sibling_comm(full), "rs_sibling_" + tag)
    return [_pair_add(a, b, "rs_pair_add_%s%d" % (tag, n)) for n, (a, b) in enumerate(zip(full, got))]


def kernel(x, ffn1_norm, ffn1_w_gate, ffn1_w_up, ffn1_w_down, mix_norm, w_in, conv_dw, conv_dw_bias, conv_ln_g, conv_ln_b, lru_conv_w, lru_conv_b, lru_w_a, lru_b_a, lru_w_x, lru_b_x, lru_lambda, w_out, ffn2_norm, ffn2_w_gate, ffn2_w_up, ffn2_w_down, final_norm, loss_target, m_ffn1_norm, m_ffn1_w_gate, m_ffn1_w_up, m_ffn1_w_down, m_mix_norm, m_w_in, m_conv_dw, m_conv_dw_bias, m_conv_ln_g, m_conv_ln_b, m_lru_conv_w, m_lru_conv_b, m_lru_w_a, m_lru_b_a, m_lru_w_x, m_lru_b_x, m_lru_lambda, m_w_out, m_ffn2_norm, m_ffn2_w_gate, m_ffn2_w_up, m_ffn2_w_down, m_final_norm, v_ffn1_norm, v_ffn1_w_gate, v_ffn1_w_up, v_ffn1_w_down, v_mix_norm, v_w_in, v_conv_dw, v_conv_dw_bias, v_conv_ln_g, v_conv_ln_b, v_lru_conv_w, v_lru_conv_b, v_lru_w_a, v_lru_b_a, v_lru_w_x, v_lru_b_x, v_lru_lambda, v_w_out, v_ffn2_norm, v_ffn2_w_gate, v_ffn2_w_up, v_ffn2_w_down, v_final_norm):
    T, D = x.shape[1], x.shape[2]
    F = ffn1_w_down.shape[0] * N_DEV
    rf = ffn1_w_down.shape[0]
    ri = w_in.shape[1]
    ro = w_out.shape[0]
    W = conv_dw_bias.shape[0]
    wc = conv_dw.shape[1]
    H = lru_w_a.shape[0]
    xs = x.reshape(T, D)
    tgt = loss_target.reshape(T, D)
    tm_ffn = min(256, T)
    cf = 256
    tm_w = min(1024, T)
    tm_mix = min(256, T)
    tf_w = F // 2
    row = lambda v: v.reshape(1, -1)
    by_owner = lambda a, rows: a.reshape(a.shape[0], N_CHIP, 2, rows, D)

    p3a = jnp.stack([ffn1_w_gate.T, ffn1_w_up.T, ffn1_w_down]).astype(BF16)
    p3b = jnp.stack([ffn2_w_gate.T, ffn2_w_up.T, ffn2_w_down]).astype(BF16)
    p_in = w_in.T.astype(BF16)
    p_out = w_out.astype(BF16)
    tile_rows = lambda a: jnp.pad(a, ((0, -a.shape[0] % SUBLANES), (0, 0)))
    p_cw = jnp.concatenate([tile_rows(conv_dw), tile_rows(lru_conv_w)], axis=0)
    lw_row = p_cw.shape[0] - SUBLANES
    stacked = lambda r, j: r.at[:, j]
    plain = lambda r, j: r.at[j]
    g3_shape = jax.ShapeDtypeStruct((3, N_DEV, rf, D), BF16)
    (g3a,) = _all_gather([p3a], [stacked], [g3_shape], "ag_ffn1")
    w3a = g3a.reshape(3, F, D)
    bda = _block_diag(lru_w_a).astype(BF16)
    bdx = _block_diag(lru_w_x).astype(BF16)
    vec = jnp.concatenate([tile_rows(v[None]) for v in
                           (conv_dw_bias, conv_ln_g, conv_ln_b, lru_conv_b, lru_b_a, lru_b_x, lru_lambda)], axis=0)

    gather_rest = _gather_comm(
        [p3b, p_in, p_out, p_cw], [stacked, plain, plain, plain],
        [g3_shape, jax.ShapeDtypeStruct((N_DEV, ri, D), BF16), jax.ShapeDtypeStruct((N_DEV, ro, D), BF16),
         jax.ShapeDtypeStruct((N_DEV,) + p_cw.shape, F32)],
        [(SIBLING,) + SAME_CORE, EVERYONE, EVERYONE, EVERYONE])
    (x1, h1, dau1, dag1, act1), (g3b_half, g_in, g_out, g_cw) = _ffn_fwd(
        xs, row(ffn1_norm), w3a, tm_ffn, cf, "ffn1_fwd", rider=gather_rest)
    w_in_t = g_in.reshape(N_DEV * ri, D)
    w_out_f = g_out.reshape(N_DEV * ro, D)
    cw_all = jnp.transpose(g_cw, (1, 0, 2)).reshape(p_cw.shape[0], N_DEV * wc)
    cw = cw_all[0:CONV_K]
    lw = cw_all[lw_row:lw_row + LRU_K]
    (x2, z, mix, u1, xr, hst), (g3b,) = _mix_core_fwd(
        x1, row(mix_norm), w_in_t, w_out_f, bda, bdx, cw, lw, vec, tm_mix, "mix_core_fwd",
        rider=_forward_comm([g3b_half], [stacked]))
    w3b = g3b.reshape(3, F, D)
    x3, h3, dau2, dag2, act2 = _ffn_fwd(x2, row(ffn2_norm), w3b, tm_ffn, cf, "ffn2_fwd")
    dx3, d_final_norm, loss_part, dob2 = _final_loss(x3, row(final_norm), tgt, min(512, T), "final_loss")

    dx2, dgate2, dup2, d_ffn2_norm = _ffn_dgrad(dx3, x2, row(ffn2_norm), dau2, dag2, w3b, tm_ffn, cf, "ffn2_dgrad")
    (dw_gu2,) = _wgrad([dgate2, dup2], h3, tm_w, tf_w, "ffn2_wgrad_gu")
    (dw_d2,) = _wgrad([act2], dob2, tm_w, tf_w, "ffn2_wgrad_d")
    sums_f2 = _reduce_level1([by_owner(dw_gu2, rf), by_owner(dw_d2, rf)], "f2")
    (dz, sg, dbda, dbdx), parts_f2 = _mix_core_bwd(
        dx2, z, u1, xr, hst, w_out_f, bda, bdx, cw, lw, vec, tm_mix, "mix_core_bwd", rider=_chips_comm(sums_f2))
    dx1, d_mix_norm, dw_in_t, dw_out, dob1 = _mix_in_bwd(dz, x1, dx2, mix, row(mix_norm), w_in_t, tm_mix, "mix_in_bwd")
    sums_io = _reduce_level1([by_owner(dw_in_t[None], ri), by_owner(dw_out[None], ro)], "io")
    (dw_d1,), parts_io = _wgrad([act1], dob1, tm_w, tf_w, "ffn1_wgrad_d", rider=_chips_comm(sums_io))
    sums_d1 = _reduce_level1([by_owner(dw_d1, rf)], "d1")
    dx0, dgate1, dup1, d_ffn1_norm = _ffn_dgrad(dx1, xs, row(ffn1_norm), dau1, dag1, w3a, tm_ffn, cf, "ffn1_dgrad")

    d_lru_w_a = _diag_blocks(dbda, H)
    d_lru_w_x = _diag_blocks(dbdx, H)
    small = [d_ffn1_norm, d_mix_norm, d_ffn2_norm, d_final_norm,
             sg[G_CB], sg[G_LNG], sg[G_LNB], sg[G_LCB], sg[G_BA], sg[G_BX], sg[G_LAM],
             d_lru_w_a, d_lru_w_x,
             sg[G_CW:G_CW + CONV_K], sg[G_LW:G_LW + LRU_K],
             loss_part]
    sizes = [a.size for a in small]
    flat = jnp.concatenate([a.reshape(-1) for a in small])
    n_rows = -(-flat.size // (8 * D)) * 8
    packed = jnp.concatenate([flat, jnp.zeros((n_rows * D - flat.size,), F32)]).reshape(n_rows, D)
    (dw_g1,), (red, *parts_d1) = _wgrad(
        [dgate1], h1, tm_w, tf_w, "ffn1_wgrad_g", rider=_both(_small_sum_comm(packed), _chips_comm(sums_d1)))
    sums_g1 = _reduce_level1([by_owner(dw_g1, rf)], "g1")
    (dw_u1,), parts_g1 = _wgrad([dup1], h1, tm_w, tf_w, "ffn1_wgrad_u", rider=_chips_comm(sums_g1))
    sums_u1 = _reduce_level1([by_owner(dw_u1, rf)], "u1")
    parts_u1 = _run_comm(_chips_comm(sums_u1), "rs_chips_u1")

    red = red.reshape(-1)
    offs = [0]
    for n in sizes:
        offs.append(offs[-1] + n)
    piece = lambda k, shape: red[offs[k]:offs[k + 1]].reshape(shape)
    loss = red[offs[15]]

    me = 4 * lax.axis_index("x") + 2 * lax.axis_index("y") + lax.axis_index("c")
    chan = lambda full_g: lax.dynamic_slice_in_dim(full_g, me * wc, wc, axis=1)
    grads = {
        "ffn1_norm": piece(0, (D,)), "mix_norm": piece(1, (D,)), "ffn2_norm": piece(2, (D,)), "final_norm": piece(3, (D,)),
        "conv_dw_bias": piece(4, (W,)), "conv_ln_g": piece(5, (W,)), "conv_ln_b": piece(6, (W,)),
        "lru_conv_b": piece(7, (W,)), "lru_b_a": piece(8, (W,)), "lru_b_x": piece(9, (W,)), "lru_lambda": piece(10, (W,)),
        "lru_w_a": piece(11, lru_w_a.shape), "lru_w_x": piece(12, lru_w_x.shape),
        "conv_dw": chan(piece(13, (CONV_K, W))), "lru_conv_w": chan(piece(14, (LRU_K, W))),
    }

    weights = dict(ffn1_norm=ffn1_norm, ffn1_w_gate=ffn1_w_gate, ffn1_w_up=ffn1_w_up, ffn1_w_down=ffn1_w_down, mix_norm=mix_norm, w_in=w_in, conv_dw=conv_dw, conv_dw_bias=conv_dw_bias, conv_ln_g=conv_ln_g, conv_ln_b=conv_ln_b, lru_conv_w=lru_conv_w, lru_conv_b=lru_conv_b, lru_w_a=lru_w_a, lru_b_a=lru_b_a, lru_w_x=lru_w_x, lru_b_x=lru_b_x, lru_lambda=lru_lambda, w_out=w_out, ffn2_norm=ffn2_norm, ffn2_w_gate=ffn2_w_gate, ffn2_w_up=ffn2_w_up, ffn2_w_down=ffn2_w_down, final_norm=final_norm)
    moment1 = dict(ffn1_norm=m_ffn1_norm, ffn1_w_gate=m_ffn1_w_gate, ffn1_w_up=m_ffn1_w_up, ffn1_w_down=m_ffn1_w_down, mix_norm=m_mix_norm, w_in=m_w_in, conv_dw=m_conv_dw, conv_dw_bias=m_conv_dw_bias, conv_ln_g=m_conv_ln_g, conv_ln_b=m_conv_ln_b, lru_conv_w=m_lru_conv_w, lru_conv_b=m_lru_conv_b, lru_w_a=m_lru_w_a, lru_b_a=m_lru_b_a, lru_w_x=m_lru_w_x, lru_b_x=m_lru_b_x, lru_lambda=m_lru_lambda, w_out=m_w_out, ffn2_norm=m_ffn2_norm, ffn2_w_gate=m_ffn2_w_gate, ffn2_w_up=m_ffn2_w_up, ffn2_w_down=m_ffn2_w_down, final_norm=m_final_norm)
    moment2 = dict(ffn1_norm=v_ffn1_norm, ffn1_w_gate=v_ffn1_w_gate, ffn1_w_up=v_ffn1_w_up, ffn1_w_down=v_ffn1_w_down, mix_norm=v_mix_norm, w_in=v_w_in, conv_dw=v_conv_dw, conv_dw_bias=v_conv_dw_bias, conv_ln_g=v_conv_ln_g, conv_ln_b=v_conv_ln_b, lru_conv_w=v_lru_conv_w, lru_conv_b=v_lru_conv_b, lru_w_a=v_lru_w_a, lru_b_a=v_lru_b_a, lru_w_x=v_lru_w_x, lru_b_x=v_lru_b_x, lru_lambda=v_lru_lambda, w_out=v_w_out, ffn2_norm=v_ffn2_norm, ffn2_w_gate=v_ffn2_w_gate, ffn2_w_up=v_ffn2_w_up, ffn2_w_down=v_ffn2_w_down, final_norm=v_final_norm)
    order = list(weights)
    big = {"ffn1_w_gate": (parts_g1[0], 0, True), "ffn1_w_up": (parts_u1[0], 0, True),
           "ffn1_w_down": (parts_d1[0], 0, False), "w_in": (parts_io[0], 0, True), "w_out": (parts_io[1], 0, False),
           "ffn2_w_gate": (parts_f2[0], 0, True), "ffn2_w_up": (parts_f2[0], 1, True),
           "ffn2_w_down": (parts_f2[1], 0, False)}
    delta, new_m, new_v = {}, {}, {}
    for n, (parts, k, transposed) in big.items():
        grads[n], delta[n], new_m[n], new_v[n] = _finish(
            parts, k, weights[n], moment1[n], moment2[n], transposed, "finish_" + n)
    rest = [n for n in order if n not in big]
    rest_sizes = [weights[n].size for n in rest]
    n_flat = sum(rest_sizes)
    pad_rows = -(-n_flat // (8 * 128)) * 8

    def pack(d):
        f = jnp.concatenate([d[n].reshape(-1) for n in rest])
        return jnp.concatenate([f, jnp.zeros((pad_rows * 128 - n_flat,), F32)]).reshape(pad_rows, 128)

    d_s, m_s, v_s = _adamw(pack(weights), pack(grads), pack(moment1), pack(moment2), "adamw_small")
    off = 0
    for n, size in zip(rest, rest_sizes):
        shape = weights[n].shape
        delta[n] = d_s.reshape(-1)[off:off + size].reshape(shape)
        new_m[n] = m_s.reshape(-1)[off:off + size].reshape(shape)
        new_v[n] = v_s.reshape(-1)[off:off + size].reshape(shape)
        off += size

    return (loss, dx0.reshape(x.shape), *[grads[n] for n in order], *[delta[n] for n in order],
            *[new_m[n] for n in order], *[new_v[n] for n in order])
```

```python
import functools
import math

import jax
import jax.numpy as jnp
from jax import lax
from jax.experimental import pallas as pl
from jax.experimental.pallas import tpu as pltpu

F32 = jnp.float32
BF16 = jnp.bfloat16
MESH = pl.DeviceIdType.MESH

N_DEV = 8
N_CHIP = 4
SUBLANES = 8
RMS_EPS = 1e-6
LN_EPS = 1e-5
LRU_C = 8.0
CONV_K = 31
LRU_K = 4
CONV_HALO = 32
LRU_HALO = 8
FFN_RES = 0.5
ADAM_LR, ADAM_B1, ADAM_B2, ADAM_EPS, ADAM_WD, ADAM_STEP = 0.001, 0.9, 0.999, 1e-08, 0.01, 10
GELU_K = math.sqrt(2.0 / math.pi)
GELU_C = 0.044715

MIB = 1024 * 1024
NT = (((1,), (1,)), ((), ()))
NN = (((1,), (0,)), ((), ()))
TN = (((0,), (0,)), ((), ()))

V_CB, V_LNG, V_LNB, V_LCB, V_BA, V_BX, V_LAM = range(0, 7 * SUBLANES, SUBLANES)
G_CW = 0
G_CB, G_LNG, G_LNB = 31, 32, 33
G_LW = 34
G_LCB, G_BA, G_BX, G_LAM = 38, 39, 40, 41
G_ROWS = 48

HBM_SPEC = pl.BlockSpec(memory_space=pltpu.HBM)
VMEM_SPEC = pl.BlockSpec(memory_space=pltpu.VMEM)


def _dot(a, b, dims):
    return lax.dot_general(a, b, dims, preferred_element_type=F32)


def _sigmoid(x):
    return 1.0 / (1.0 + jnp.exp(-x))


def _gelu_parts(x):
    x2 = x * x
    th = jnp.tanh(GELU_K * x * (1.0 + GELU_C * x2))
    gl = 0.5 * x * (1.0 + th)
    dgl = 0.5 * (1.0 + th) + 0.5 * x * (1.0 - th * th) * GELU_K * (1.0 + 3.0 * GELU_C * x2)
    return gl, dgl


def _neg_expm1(y):
    series = -y * (1.0 + y * (1.0 / 2) * (1.0 + y * (1.0 / 3) * (1.0 + y * (1.0 / 4) * (1.0 + y * (1.0 / 5) * (1.0 + y * (1.0 / 6))))))
    return jnp.where(y > -0.25, series, 1.0 - jnp.exp(y))


def _softplus_neg(lam):
    t = -lam
    e = jnp.exp(-jnp.abs(t))
    s = 1.0 + e
    log1p_e = jnp.log(s) - ((s - 1.0) - e) / s
    return jnp.maximum(t, 0.0) + log1p_e


def _rms_stats(xv):
    rstd = lax.rsqrt(jnp.mean(xv * xv, axis=-1, keepdims=True) + RMS_EPS)
    return xv * rstd, rstd


def _rms_bwd(xhat, rstd, g, dh):
    dxhat = dh * g
    dx = rstd * (dxhat - xhat * jnp.mean(dxhat * xhat, axis=-1, keepdims=True))
    return dx, jnp.sum(dh * xhat, axis=0, keepdims=True)


def _row_windows(buf_ref, n_rows, offsets):
    total = buf_ref.shape[0]
    full = buf_ref[...]
    for b in range(SUBLANES):
        offs = [o for o in offsets if o % SUBLANES == b]
        if not offs:
            continue
        assert max(offs) + n_rows <= total
        moved = full if b == 0 else pltpu.roll(full, total - b, 0)
        for o in offs:
            yield o, moved[o - b:o - b + n_rows, :]


def _scan_rows(av, bv, edge, out_ref, reverse=False):
    tm, W = av.shape
    sub = lax.broadcasted_iota(jnp.int32, (tm, W), 0) % SUBLANES
    s = 1
    while s < SUBLANES:
        keep = (sub < SUBLANES - s) if reverse else (sub >= s)
        shift = tm - s if reverse else s
        bv = jnp.where(keep, av * pltpu.roll(bv, shift, 0) + bv, bv)
        av = jnp.where(keep, av * pltpu.roll(av, shift, 0), av)
        s *= 2
    starts = range(0, tm, SUBLANES)
    for r0 in (reversed(starts) if reverse else starts):
        group = av[r0:r0 + SUBLANES, :] * edge + bv[r0:r0 + SUBLANES, :]
        out_ref[r0:r0 + SUBLANES, :] = group
        edge = group[0:1, :] if reverse else group[SUBLANES - 1:SUBLANES, :]
    return edge


class _Comm:
    def __init__(self, arrays, in_specs, out_shapes, out_specs, scratch, start, wait, aliases=None):
        self.arrays, self.in_specs = list(arrays), list(in_specs)
        self.out_shapes, self.out_specs = list(out_shapes), list(out_specs)
        self.scratch, self.start, self.wait = list(scratch), start, wait
        self.aliases = dict(aliases or {})


def _in_hbm(a):
    return pltpu.with_memory_space_constraint(a, pltpu.HBM)


def _operands(comm):
    return [a if spec is VMEM_SPEC else _in_hbm(a) for a, spec in zip(comm.arrays, comm.in_specs)]


def _call(body, *, name, grid, in_specs, out_specs, out_shape, scratch_shapes, vmem_mib, args, rider=None,
          num_scalar_prefetch=0):
    params = pltpu.CompilerParams(dimension_semantics=("arbitrary",) * len(grid), vmem_limit_bytes=vmem_mib * MIB)
    args = [a if k < num_scalar_prefetch else _in_hbm(a) for k, a in enumerate(args)]
    if rider is None:
        return pl.pallas_call(
            body, name=name,
            grid_spec=pltpu.PrefetchScalarGridSpec(
                num_scalar_prefetch=num_scalar_prefetch, grid=grid, in_specs=in_specs, out_specs=out_specs,
                scratch_shapes=scratch_shapes),
            out_shape=out_shape, compiler_params=params)(*args)
    assert num_scalar_prefetch == 0
    n_in, n_out, n_scr = len(in_specs), len(out_specs), len(scratch_shapes)
    r_in, r_out = len(rider.arrays), len(rider.out_shapes)
    n_axes = len(grid)

    def carried(*refs):
        pos = [0]

        def take(n):
            pos[0] += n
            return refs[pos[0] - n:pos[0]]

        ins, r_ins, outs, r_outs, scr, r_scr = take(n_in), take(r_in), take(n_out), take(r_out), take(n_scr), take(len(rider.scratch))
        first = pl.program_id(0) == 0
        last = pl.program_id(0) == grid[0] - 1
        for ax in range(1, n_axes):
            first = first & (pl.program_id(ax) == 0)
            last = last & (pl.program_id(ax) == grid[ax] - 1)

        @pl.when(first)
        def _():
            rider.start(r_ins, r_outs, r_scr)

        body(*ins, *outs, *scr)

        @pl.when(last)
        def _():
            rider.wait(r_ins, r_outs, r_scr)

    res = pl.pallas_call(
        carried, name=name,
        grid=grid,
        in_specs=list(in_specs) + rider.in_specs,
        out_specs=list(out_specs) + rider.out_specs,
        out_shape=list(out_shape) + rider.out_shapes,
        scratch_shapes=list(scratch_shapes) + rider.scratch,
        input_output_aliases={n_in + i: n_out + o for i, o in rider.aliases.items()},
        compiler_params=params)(*args, *_operands(rider))
    return res[:n_out], res[n_out:]


def _run_comm(comm, name):
    n_in, n_out = len(comm.arrays), len(comm.out_shapes)

    def body(*refs):
        ins, outs, scr = refs[:n_in], refs[n_in:n_in + n_out], refs[n_in + n_out:]
        comm.start(ins, outs, scr)
        comm.wait(ins, outs, scr)

    return pl.pallas_call(
        body, name=name,
        in_specs=comm.in_specs, out_specs=comm.out_specs, out_shape=comm.out_shapes,
        scratch_shapes=comm.scratch, input_output_aliases=comm.aliases,
        compiler_params=pltpu.CompilerParams(vmem_limit_bytes=24 * MIB))(*_operands(comm))


def _both(a, b):
    ni, no, ns = len(a.arrays), len(a.out_shapes), len(a.scratch)

    def start(ins, outs, scr):
        a.start(ins[:ni], outs[:no], scr[:ns])
        b.start(ins[ni:], outs[no:], scr[ns:])

    def wait(ins, outs, scr):
        a.wait(ins[:ni], outs[:no], scr[:ns])
        b.wait(ins[ni:], outs[no:], scr[ns:])

    aliases = dict(a.aliases)
    aliases.update({ni + i: no + o for i, o in b.aliases.items()})
    return _Comm(a.arrays + b.arrays, a.in_specs + b.in_specs, a.out_shapes + b.out_shapes,
                 a.out_specs + b.out_specs, a.scratch + b.scratch, start, wait, aliases)


def _place():
    return lax.axis_index("x"), lax.axis_index("y"), lax.axis_index("c")


def _peer(k):
    x, y, c = _place()
    px, py, pc = x ^ ((k >> 2) & 1), y ^ ((k >> 1) & 1), c ^ (k & 1)
    return (px, py, pc), 4 * px + 2 * py + pc


SIBLING = 1
SAME_CORE = (2, 4, 6)
EVERYONE = tuple(range(1, N_DEV))


def _gather_comm(shards, views, out_shapes, relations):
    na = len(shards)

    def copies(ins, outs, scr):
        send_sems, recv_sems, _ = scr
        _, me = _peer(0)
        out = []
        for a in range(na):
            for k in relations[a]:
                peer, theirs = _peer(k)
                send = functools.partial(
                    pltpu.make_async_remote_copy,
                    src_ref=ins[a], dst_ref=views[a](outs[a], me),
                    send_sem=send_sems.at[7 * a + k - 1], recv_sem=recv_sems.at[7 * a + k - 1],
                    device_id=peer, device_id_type=MESH)
                recv = functools.partial(
                    pltpu.make_async_remote_copy,
                    src_ref=ins[a], dst_ref=views[a](outs[a], theirs),
                    send_sem=send_sems.at[7 * a + k - 1], recv_sem=recv_sems.at[7 * a + k - 1],
                    device_id=peer, device_id_type=MESH)
                out.append((send, recv))
        return out

    def local(ins, outs, scr):
        _, me = _peer(0)
        return [pltpu.make_async_copy(ins[a], views[a](outs[a], me), scr[2].at[a]) for a in range(na)]

    def start(ins, outs, scr):
        for cp in local(ins, outs, scr):
            cp.start()
        for send, _ in copies(ins, outs, scr):
            send().start()

    def wait(ins, outs, scr):
        for _, recv in copies(ins, outs, scr):
            recv().wait_recv()
        for send, _ in copies(ins, outs, scr):
            send().wait_send()
        for cp in local(ins, outs, scr):
            cp.wait()

    return _Comm(shards, [HBM_SPEC] * na, out_shapes, [HBM_SPEC] * na,
                 [pltpu.SemaphoreType.DMA((7 * na,)), pltpu.SemaphoreType.DMA((7 * na,)),
                  pltpu.SemaphoreType.DMA((na,))], start, wait)


def _forward_comm(gathered, views):
    na = len(gathered)
    shapes = [jax.ShapeDtypeStruct(g.shape, g.dtype) for g in gathered]

    def copies(outs, scr):
        send_sems, recv_sems = scr
        sibling, _ = _peer(SIBLING)
        out = []
        for a in range(na):
            for n, k in enumerate(SAME_CORE):
                _, mine = _peer(k)
                _, theirs = _peer(k ^ SIBLING)
                send = functools.partial(
                    pltpu.make_async_remote_copy,
                    src_ref=views[a](outs[a], mine), dst_ref=views[a](outs[a], mine),
                    send_sem=send_sems.at[3 * a + n], recv_sem=recv_sems.at[3 * a + n],
                    device_id=sibling, device_id_type=MESH)
                recv = functools.partial(
                    pltpu.make_async_remote_copy,
                    src_ref=views[a](outs[a], mine), dst_ref=views[a](outs[a], theirs),
                    send_sem=send_sems.at[3 * a + n], recv_sem=recv_sems.at[3 * a + n],
                    device_id=sibling, device_id_type=MESH)
                out.append((send, recv))
        return out

    def start(ins, outs, scr):
        for send, _ in copies(outs, scr):
            send().start()

    def wait(ins, outs, scr):
        for _, recv in copies(outs, scr):
            recv().wait_recv()
        for send, _ in copies(outs, scr):
            send().wait_send()

    return _Comm(gathered, [HBM_SPEC] * na, shapes, [HBM_SPEC] * na,
                 [pltpu.SemaphoreType.DMA((3 * na,)), pltpu.SemaphoreType.DMA((3 * na,))], start, wait,
                 aliases={a: a for a in range(na)})


def _all_gather(shards, views, out_shapes, name):
    na = len(shards)
    level1 = _gather_comm(shards, views, out_shapes, [(SIBLING,) + SAME_CORE] * na)

    def body(*refs):
        ins, outs = refs[:na], refs[na:2 * na]
        send_sems, recv_sems, local_sems, fwd_send, fwd_recv = refs[2 * na:]
        sibling, _ = _peer(SIBLING)
        scr = (send_sems, recv_sems, local_sems)
        level1.start(ins, outs, scr)
        passed, landing = [], []
        for a in range(na):
            for n, k in enumerate(SAME_CORE):
                peer, mine = _peer(k)
                _, theirs = _peer(k ^ SIBLING)
                pltpu.make_async_remote_copy(
                    src_ref=ins[a], dst_ref=views[a](outs[a], mine),
                    send_sem=send_sems.at[7 * a + k - 1], recv_sem=recv_sems.at[7 * a + k - 1],
                    device_id=peer, device_id_type=MESH).wait_recv()
                fwd = pltpu.make_async_remote_copy(
                    src_ref=views[a](outs[a], mine), dst_ref=views[a](outs[a], mine),
                    send_sem=fwd_send.at[3 * a + n], recv_sem=fwd_recv.at[3 * a + n],
                    device_id=sibling, device_id_type=MESH)
                fwd.start()
                passed.append(fwd)
                landing.append(pltpu.make_async_remote_copy(
                    src_ref=views[a](outs[a], mine), dst_ref=views[a](outs[a], theirs),
                    send_sem=fwd_send.at[3 * a + n], recv_sem=fwd_recv.at[3 * a + n],
                    device_id=sibling, device_id_type=MESH))
        for a in range(na):
            _, theirs = _peer(SIBLING)
            pltpu.make_async_remote_copy(
                src_ref=ins[a], dst_ref=views[a](outs[a], theirs),
                send_sem=send_sems.at[7 * a + SIBLING - 1], recv_sem=recv_sems.at[7 * a + SIBLING - 1],
                device_id=sibling, device_id_type=MESH).wait_recv()
        for cp in landing:
            cp.wait_recv()
        for cp in passed:
            cp.wait_send()
        _, me = _peer(0)
        for a in range(na):
            for k in (SIBLING,) + SAME_CORE:
                peer, _ = _peer(k)
                pltpu.make_async_remote_copy(
                    src_ref=ins[a], dst_ref=views[a](outs[a], me),
                    send_sem=send_sems.at[7 * a + k - 1], recv_sem=recv_sems.at[7 * a + k - 1],
                    device_id=peer, device_id_type=MESH).wait_send()
            pltpu.make_async_copy(ins[a], views[a](outs[a], me), local_sems.at[a]).wait()

    return pl.pallas_call(
        body, name=name,
        in_specs=[HBM_SPEC] * na, out_specs=[HBM_SPEC] * na, out_shape=out_shapes,
        scratch_shapes=level1.scratch + [pltpu.SemaphoreType.DMA((3 * na,)), pltpu.SemaphoreType.DMA((3 * na,))],
    )(*[_in_hbm(s) for s in shards])


def _sibling_comm(grads):
    na = len(grads)
    shapes = [jax.ShapeDtypeStruct(g.shape[:2] + g.shape[3:], g.dtype) for g in grads]

    def copies(ins, outs, scr):
        x, y, c = _place()
        return [pltpu.make_async_remote_copy(
            src_ref=ins[a].at[:, :, 1 - c], dst_ref=outs[a],
            send_sem=scr[0].at[a], recv_sem=scr[1].at[a],
            device_id=(x, y, 1 - c), device_id_type=MESH) for a in range(na)]

    def start(ins, outs, scr):
        for cp in copies(ins, outs, scr):
            cp.start()

    def wait(ins, outs, scr):
        for cp in copies(ins, outs, scr):
            cp.wait()

    return _Comm(grads, [HBM_SPEC] * na, shapes, [HBM_SPEC] * na,
                 [pltpu.SemaphoreType.DMA((na,)), pltpu.SemaphoreType.DMA((na,))], start, wait)


def _chips_comm(sums):
    na = len(sums)
    shapes = [jax.ShapeDtypeStruct(s.shape, s.dtype) for s in sums]

    def copies(ins, outs, scr):
        x, y, c = _place()
        mine = 2 * x + y
        out = []
        for a in range(na):
            for n, k in enumerate(SAME_CORE):
                (px, py, pc), _ = _peer(k)
                theirs = 2 * px + py
                send = functools.partial(
                    pltpu.make_async_remote_copy,
                    src_ref=ins[a].at[:, theirs], dst_ref=outs[a].at[:, mine],
                    send_sem=scr[0].at[3 * a + n], recv_sem=scr[1].at[3 * a + n],
                    device_id=(px, py, pc), device_id_type=MESH)
                recv = functools.partial(
                    pltpu.make_async_remote_copy,
                    src_ref=ins[a].at[:, mine], dst_ref=outs[a].at[:, theirs],
                    send_sem=scr[0].at[3 * a + n], recv_sem=scr[1].at[3 * a + n],
                    device_id=(px, py, pc), device_id_type=MESH)
                out.append((send, recv))
        return out

    def local(ins, outs, scr):
        x, y, _ = _place()
        mine = 2 * x + y
        return [pltpu.make_async_copy(ins[a].at[:, mine], outs[a].at[:, mine], scr[2].at[a]) for a in range(na)]

    def start(ins, outs, scr):
        for cp in local(ins, outs, scr):
            cp.start()
        for send, _ in copies(ins, outs, scr):
            send().start()

    def wait(ins, outs, scr):
        for _, recv in copies(ins, outs, scr):
            recv().wait_recv()
        for send, _ in copies(ins, outs, scr):
            send().wait_send()
        for cp in local(ins, outs, scr):
            cp.wait()

    return _Comm(sums, [HBM_SPEC] * na, shapes, [HBM_SPEC] * na,
                 [pltpu.SemaphoreType.DMA((3 * na,)), pltpu.SemaphoreType.DMA((3 * na,)),
                  pltpu.SemaphoreType.DMA((na,))], start, wait)


def _small_sum_comm(s):
    R, C = s.shape

    def copies(ins, scr):
        buf, send_sems, recv_sems = scr
        _, me = _peer(0)
        out = []
        for k in EVERYONE:
            peer, theirs = _peer(k)
            send = functools.partial(
                    pltpu.make_async_remote_copy,
                src_ref=ins[0], dst_ref=buf.at[me], send_sem=send_sems.at[k - 1], recv_sem=recv_sems.at[k - 1],
                device_id=peer, device_id_type=MESH)
            recv = functools.partial(
                    pltpu.make_async_remote_copy,
                src_ref=ins[0], dst_ref=buf.at[theirs], send_sem=send_sems.at[k - 1], recv_sem=recv_sems.at[k - 1],
                device_id=peer, device_id_type=MESH)
            out.append((send, recv))
        return out

    def start(ins, outs, scr):
        _, me = _peer(0)
        scr[0][me] = ins[0][...]
        for send, _ in copies(ins, scr):
            send().start()

    def wait(ins, outs, scr):
        for _, recv in copies(ins, scr):
            recv().wait_recv()
        for send, _ in copies(ins, scr):
            send().wait_send()
        acc = scr[0][0]
        for j in range(1, N_DEV):
            acc = acc + scr[0][j]
        outs[0][...] = acc

    return _Comm([s], [VMEM_SPEC], [jax.ShapeDtypeStruct((R, C), F32)], [VMEM_SPEC],
                 [pltpu.VMEM((N_DEV, R, C), F32), pltpu.SemaphoreType.DMA((N_DEV - 1,)),
                  pltpu.SemaphoreType.DMA((N_DEV - 1,))], start, wait)


def _prep_weights(ffn1, ffn2, w_in, w_out, name):
    rf, D = ffn1[2].shape
    ri, ro = w_in.shape[1], w_out.shape[0]

    def body(g1, u1, d1, g2, u2, d2, wi, wo, p1_ref, p2_ref, pi_ref, po_ref):
        for p_ref, (gate, up, down) in ((p1_ref, (g1, u1, d1)), (p2_ref, (g2, u2, d2))):
            p_ref[0] = gate[...].T.astype(BF16)
            p_ref[1] = up[...].T.astype(BF16)
            p_ref[2] = down[...].astype(BF16)
        pi_ref[...] = wi[...].T.astype(BF16)
        po_ref[...] = wo[...].astype(BF16)

    args = (*ffn1, *ffn2, w_in, w_out)
    whole = lambda shape: pl.BlockSpec(shape, lambda i: (0,) * len(shape))
    out_shapes = [(3, rf, D), (3, rf, D), (ri, D), (ro, D)]
    return _call(
        body, name=name, grid=(1,),
        in_specs=[whole(a.shape) for a in args], out_specs=[whole(s) for s in out_shapes],
        out_shape=[jax.ShapeDtypeStruct(s, BF16) for s in out_shapes],
        scratch_shapes=[], vmem_mib=48, args=args)


def _load_weights(w_hbm, w_vmem, sem):
    @pl.when(pl.program_id(0) == 0)
    def _():
        copies = [pltpu.make_async_copy(w_hbm.at[k], w_vmem.at[k], sem.at[k]) for k in range(3)]
        for cp in copies:
            cp.start()
        for cp in copies:
            cp.wait()


def _ffn_fwd(x, g, w3, tm, cf, name, rider=None):
    T, D = x.shape
    F = w3.shape[1]

    def body(x_ref, g_ref, w_hbm, xo_ref, h_ref, dau_ref, dag_ref, act_ref, wv, sem):
        _load_weights(w_hbm, wv, sem)
        xhat, _ = _rms_stats(x_ref[...])
        hb = (xhat * g_ref[...]).astype(BF16)
        h_ref[...] = hb
        for lo in range(0, F, cf):
            gate = _dot(hb, wv[0, lo:lo + cf, :], NT)
            up = _dot(hb, wv[1, lo:lo + cf, :], NT)
            sig = _sigmoid(gate)
            silu = gate * sig
            dau_ref[:, lo:lo + cf] = silu.astype(BF16)
            dag_ref[:, lo:lo + cf] = (up * (sig * (1.0 + gate * (1.0 - sig)))).astype(BF16)
            act_ref[:, lo:lo + cf] = (silu * up).astype(BF16)
        xo_ref[...] = x_ref[...] + FFN_RES * _dot(act_ref[...], wv[2], NN)

    row = pl.BlockSpec((tm, D), lambda i: (i, 0))
    hid = pl.BlockSpec((tm, F), lambda i: (i, 0))
    return _call(
        body, name=name, grid=(T // tm,),
        in_specs=[row, pl.BlockSpec((1, D), lambda i: (0, 0)), HBM_SPEC],
        out_specs=[row, row, hid, hid, hid],
        out_shape=[jax.ShapeDtypeStruct((T, D), F32), jax.ShapeDtypeStruct((T, D), BF16)]
        + [jax.ShapeDtypeStruct((T, F), BF16)] * 3,
        scratch_shapes=[pltpu.VMEM((3, F, D), BF16), pltpu.SemaphoreType.DMA((3,))],
        vmem_mib=48, args=(x, g, w3), rider=rider)


def _ffn_dgrad(dout, x, g, dau, dag, w3, tm, cf, name, rider=None):
    T, D = x.shape
    F = w3.shape[1]

    def body(do_ref, x_ref, g_ref, dau_ref, dag_ref, w_hbm, dx_ref, dgate_ref, dup_ref, dg_ref, wv, sem):
        _load_weights(w_hbm, wv, sem)

        @pl.when(pl.program_id(0) == 0)
        def _():
            dg_ref[...] = jnp.zeros_like(dg_ref)

        dob = (FFN_RES * do_ref[...]).astype(BF16)
        for lo in range(0, F, cf):
            dact = _dot(dob, wv[2, lo:lo + cf, :], NT)
            dup_ref[:, lo:lo + cf] = (dact * dau_ref[:, lo:lo + cf].astype(F32)).astype(BF16)
            dgate_ref[:, lo:lo + cf] = (dact * dag_ref[:, lo:lo + cf].astype(F32)).astype(BF16)
        dh = _dot(dgate_ref[...], wv[0], NN) + _dot(dup_ref[...], wv[1], NN)
        xhat, rstd = _rms_stats(x_ref[...])
        dx, dg = _rms_bwd(xhat, rstd, g_ref[...], dh)
        dx_ref[...] = do_ref[...] + dx
        dg_ref[...] += dg

    row = pl.BlockSpec((tm, D), lambda i: (i, 0))
    hid = pl.BlockSpec((tm, F), lambda i: (i, 0))
    vec = pl.BlockSpec((1, D), lambda i: (0, 0))
    return _call(
        body, name=name, grid=(T // tm,),
        in_specs=[row, row, vec, hid, hid, HBM_SPEC],
        out_specs=[row, hid, hid, vec],
        out_shape=[jax.ShapeDtypeStruct((T, D), F32), jax.ShapeDtypeStruct((T, F), BF16),
                   jax.ShapeDtypeStruct((T, F), BF16), jax.ShapeDtypeStruct((1, D), F32)],
        scratch_shapes=[pltpu.VMEM((3, F, D), BF16), pltpu.SemaphoreType.DMA((3,))],
        vmem_mib=52, args=(dout, x, g, dau, dag, w3), rider=rider)


def _wgrad(lhs, rhs, tm, tf, name, rider=None):
    T, F = lhs[0].shape
    D = rhs.shape[1]
    K = len(lhs)

    def body(*refs):
        lhs_refs, rhs_ref, dw_ref, accs = refs[:K], refs[K], refs[K + 1], refs[K + 2:]
        i = pl.program_id(1)

        @pl.when(i == 0)
        def _():
            for acc in accs:
                acc[...] = jnp.zeros_like(acc)

        rv = rhs_ref[...]
        for acc, lhs_ref in zip(accs, lhs_refs):
            acc[...] += _dot(lhs_ref[...], rv, TN)

        @pl.when(i == pl.num_programs(1) - 1)
        def _():
            for k, acc in enumerate(accs):
                dw_ref[k] = acc[...].astype(BF16)

    hid = pl.BlockSpec((tm, tf), lambda f, i: (i, f))
    return _call(
        body, name=name, grid=(F // tf, T // tm),
        in_specs=[hid] * K + [pl.BlockSpec((tm, D), lambda f, i: (i, 0))],
        out_specs=[pl.BlockSpec((K, tf, D), lambda f, i: (0, f, 0))],
        out_shape=[jax.ShapeDtypeStruct((K, F, D), BF16)],
        scratch_shapes=[pltpu.VMEM((tf, D), F32)] * K,
        vmem_mib=56, args=(*lhs, rhs), rider=rider)


def _lru_gates(xr, bda_ref, bdx_ref, vec_ref):
    xrb = xr.astype(BF16)
    r = _sigmoid(_dot(xrb, bda_ref[...], NN) + vec_ref[V_BA:V_BA + 1, :])
    ig = _sigmoid(_dot(xrb, bdx_ref[...], NN) + vec_ref[V_BX:V_BX + 1, :])
    sp = _softplus_neg(vec_ref[V_LAM:V_LAM + 1, :])
    log_a = (-LRU_C * sp) * r
    a = jnp.exp(log_a)
    mult = jnp.sqrt(_neg_expm1(2.0 * log_a))
    return xrb, r, ig, sp, a, mult


def _layernorm_stats(u1):
    xc = u1 - jnp.mean(u1, axis=-1, keepdims=True)
    rs = lax.rsqrt(jnp.mean(xc * xc, axis=-1, keepdims=True) + LN_EPS)
    return xc * rs, rs


def _mix_core_fwd(x1, g, w_in_t, w_out, bda, bdx, cw, lw, vec, tm, name, rider=None):
    T, D = x1.shape
    W = cw.shape[1]
    assert tm >= CONV_HALO and w_in_t.shape[0] == 4 * W

    def body(x1_ref, g_ref, wi_ref, wo_ref, bda_ref, bdx_ref, cw_ref, lw_ref, vec_ref,
             x2_ref, z_ref, mix_ref, u1_ref, xr_ref, hst_ref, ubuf, rbuf, hc):
        @pl.when(pl.program_id(0) == 0)
        def _():
            ubuf[0:CONV_HALO, :] = jnp.zeros((CONV_HALO, W), F32)
            rbuf[0:LRU_HALO, :] = jnp.zeros((LRU_HALO, W), F32)
            hc[...] = jnp.zeros_like(hc)

        xhat, _ = _rms_stats(x1_ref[...])
        z_ref[...] = _dot((xhat * g_ref[...]).astype(BF16), wi_ref[...], NT)

        ubuf[CONV_HALO:CONV_HALO + tm, :] = z_ref[:, 0:W] * _sigmoid(z_ref[:, W:2 * W])
        u1 = jnp.zeros((tm, W), F32) + vec_ref[V_CB:V_CB + 1, :]
        base = CONV_HALO - (CONV_K - 1)
        for off, win in _row_windows(ubuf, tm, range(base, base + CONV_K)):
            u1 = u1 + cw_ref[off - base:off - base + 1, :] * win
        ubuf[0:CONV_HALO, :] = ubuf[tm:tm + CONV_HALO, :]
        u1_ref[...] = u1
        xh, _ = _layernorm_stats(u1)
        u2 = xh * vec_ref[V_LNG:V_LNG + 1, :] + vec_ref[V_LNB:V_LNB + 1, :]
        ub = (u2 * _sigmoid(u2)).astype(BF16)
        mix_ref[:, 0:W] = ub

        rbuf[LRU_HALO:LRU_HALO + tm, :] = z_ref[:, 2 * W:3 * W]
        xr = jnp.zeros((tm, W), F32) + vec_ref[V_LCB:V_LCB + 1, :]
        for k in range(LRU_K):
            off = LRU_HALO - (LRU_K - 1) + k
            xr = xr + lw_ref[k:k + 1, :] * rbuf[off:off + tm, :]
        rbuf[0:LRU_HALO, :] = rbuf[tm:tm + LRU_HALO, :]
        xr_ref[...] = xr
        _, _, ig, _, a, mult = _lru_gates(xr, bda_ref, bdx_ref, vec_ref)
        hc[0:1, :] = _scan_rows(a, mult * (ig * xr), hc[0:1, :], hst_ref)
        gl, _ = _gelu_parts(z_ref[:, 3 * W:4 * W])
        yb = (hst_ref[...] * gl).astype(BF16)
        mix_ref[:, W:2 * W] = yb

        x2_ref[...] = x1_ref[...] + _dot(ub, wo_ref[0:W, :], NN) + _dot(yb, wo_ref[W:2 * W, :], NN)

    full = lambda a: pl.BlockSpec(a.shape, lambda i: (0,) * a.ndim)
    tile = lambda n: pl.BlockSpec((tm, n), lambda i: (i, 0))
    return _call(
        body, name=name, grid=(T // tm,),
        in_specs=[tile(D), full(g), full(w_in_t), full(w_out), full(bda), full(bdx), full(cw), full(lw), full(vec)],
        out_specs=[tile(D), tile(4 * W), tile(2 * W), tile(W), tile(W), tile(W)],
        out_shape=[jax.ShapeDtypeStruct((T, D), F32), jax.ShapeDtypeStruct((T, 4 * W), F32),
                   jax.ShapeDtypeStruct((T, 2 * W), BF16), jax.ShapeDtypeStruct((T, W), F32),
                   jax.ShapeDtypeStruct((T, W), F32), jax.ShapeDtypeStruct((T, W), F32)],
        scratch_shapes=[pltpu.VMEM((tm + CONV_HALO, W), F32), pltpu.VMEM((tm + LRU_HALO, W), F32),
                        pltpu.VMEM((8, W), F32)],
        vmem_mib=56, args=(x1, g, w_in_t, w_out, bda, bdx, cw, lw, vec), rider=rider)


def _mix_core_bwd(dx2, z, u1, xr, hst, w_out, bda, bdx, cw, lw, vec, tm, name, rider=None):
    T, D = dx2.shape
    W = cw.shape[1]
    nt = T // tm
    assert tm >= CONV_HALO and tm % CONV_HALO == 0

    def body(dx_ref, z_ref, zh_ref, u1_ref, xr_ref, h_ref, hh_ref, wo_ref, bda_ref, bdx_ref, cw_ref, lw_ref, vec_ref,
             dz_ref, sg_ref, dbda_ref, dbdx_ref, u0buf, du1buf, rxbuf, dxrbuf, gbuf, gc, spacc):
        i = pl.program_id(0)
        first = i == nt - 1
        row = lax.broadcasted_iota(jnp.int32, (tm, W), 0)

        @pl.when(i == 0)
        def _():
            sg_ref[...] = jnp.zeros_like(sg_ref)
            dbda_ref[...] = jnp.zeros_like(dbda_ref)
            dbdx_ref[...] = jnp.zeros_like(dbdx_ref)
            du1buf[tm:tm + CONV_HALO, :] = jnp.zeros((CONV_HALO, W), F32)
            dxrbuf[tm:tm + LRU_HALO, :] = jnp.zeros((LRU_HALO, W), F32)
            gc[...] = jnp.zeros_like(gc)
            spacc[...] = jnp.zeros_like(spacc)

        def accum(r, val):
            sg_ref[r:r + 1, :] += jnp.sum(val, axis=0, keepdims=True)

        dmix = _dot(dx_ref[...].astype(BF16), wo_ref[...], NT)
        d_u = dmix[:, 0:W]
        d_yr = dmix[:, W:2 * W]

        xh, rs = _layernorm_stats(u1_ref[...])
        ln_g = vec_ref[V_LNG:V_LNG + 1, :]
        u2 = xh * ln_g + vec_ref[V_LNB:V_LNB + 1, :]
        s2 = _sigmoid(u2)
        d_u2 = d_u * (s2 * (1.0 + u2 * (1.0 - s2)))
        accum(G_LNG, d_u2 * xh)
        accum(G_LNB, d_u2)
        d_xh = d_u2 * ln_g
        d_u1 = rs * (d_xh - jnp.mean(d_xh, axis=-1, keepdims=True)
                     - xh * jnp.mean(d_xh * xh, axis=-1, keepdims=True))
        accum(G_CB, d_u1)
        halo_on = jnp.where(first, 0.0, 1.0)
        u0buf[0:CONV_HALO, :] = halo_on * (zh_ref[:, 0:W] * _sigmoid(zh_ref[:, W:2 * W]))
        cv = z_ref[:, 0:W]
        sgc = _sigmoid(z_ref[:, W:2 * W])
        u0buf[CONV_HALO:CONV_HALO + tm, :] = cv * sgc
        du1buf[0:tm, :] = d_u1
        base = CONV_HALO - (CONV_K - 1)
        for off, win in _row_windows(u0buf, tm, range(base, base + CONV_K)):
            accum(G_CW + off - base, d_u1 * win)
        d_u0 = jnp.zeros((tm, W), F32)
        for off, win in _row_windows(du1buf, tm, range(0, CONV_K)):
            d_u0 = d_u0 + cw_ref[CONV_K - 1 - off:CONV_K - off, :] * win
        du1buf[tm:tm + CONV_HALO, :] = du1buf[0:CONV_HALO, :]
        dz_ref[:, 0:W] = (d_u0 * sgc).astype(BF16)
        dz_ref[:, W:2 * W] = (d_u0 * cv * (sgc * (1.0 - sgc))).astype(BF16)

        xrv = xr_ref[...]
        xrb, r, ig, sp, a, mult = _lru_gates(xrv, bda_ref, bdx_ref, vec_ref)
        h = h_ref[...]
        gl, dgl = _gelu_parts(z_ref[:, 3 * W:4 * W])
        dz_ref[:, 3 * W:4 * W] = (d_yr * h * dgl).astype(BF16)
        a_next = jnp.where(row == tm - 1, 1.0, pltpu.roll(a, tm - 1, 0))
        g_first = _scan_rows(a_next, d_yr * gl, gc[0:1, :], gbuf, reverse=True)
        g = gbuf[...]
        gc[0:1, :] = a[0:1, :] * g_first
        hprev = jnp.where(row == 0, halo_on * hh_ref[LRU_HALO - 1:LRU_HALO, :], pltpu.roll(h, 1, 0))
        d_log_a = (g * hprev) * a - (g * ig * xrv) * (a * a) / mult
        d_ig = g * mult * xrv
        d_xr = g * mult * ig
        spacc[0:1, :] += jnp.sum(d_log_a * r, axis=0, keepdims=True)
        d_pa32 = (d_log_a * (-LRU_C * sp)) * (r * (1.0 - r))
        d_px32 = d_ig * (ig * (1.0 - ig))
        accum(G_BA, d_pa32)
        accum(G_BX, d_px32)
        d_pa = d_pa32.astype(BF16)
        d_px = d_px32.astype(BF16)
        d_xr = d_xr + _dot(d_pa, bda_ref[...], NT) + _dot(d_px, bdx_ref[...], NT)
        dbda_ref[...] += _dot(xrb, d_pa, TN)
        dbdx_ref[...] += _dot(xrb, d_px, TN)
        accum(G_LCB, d_xr)
        rxbuf[0:LRU_HALO, :] = halo_on * zh_ref[CONV_HALO - LRU_HALO:CONV_HALO, 2 * W:3 * W]
        rxbuf[LRU_HALO:LRU_HALO + tm, :] = z_ref[:, 2 * W:3 * W]
        dxrbuf[0:tm, :] = d_xr
        d_rx = jnp.zeros((tm, W), F32)
        for k in range(LRU_K):
            off = LRU_HALO - (LRU_K - 1) + k
            accum(G_LW + k, d_xr * rxbuf[off:off + tm, :])
            d_rx = d_rx + lw_ref[k:k + 1, :] * dxrbuf[LRU_K - 1 - k:LRU_K - 1 - k + tm, :]
        dxrbuf[tm:tm + LRU_HALO, :] = dxrbuf[0:LRU_HALO, :]
        dz_ref[:, 2 * W:3 * W] = d_rx.astype(BF16)

        @pl.when(first)
        def _():
            lam = vec_ref[V_LAM:V_LAM + 1, :]
            sg_ref[G_LAM:G_LAM + 1, :] = LRU_C * _sigmoid(-lam) * spacc[0:1, :]

    full = lambda a: pl.BlockSpec(a.shape, lambda i: (0,) * a.ndim)
    tile = lambda n: pl.BlockSpec((tm, n), lambda i: (nt - 1 - i, 0))
    halo = lambda rows, n: pl.BlockSpec(
        (rows, n), lambda i: (jnp.maximum((nt - 1 - i) * (tm // rows) - 1, 0), 0))
    return _call(
        body, name=name, grid=(nt,),
        in_specs=[tile(D), tile(4 * W), halo(CONV_HALO, 4 * W), tile(W), tile(W), tile(W), halo(LRU_HALO, W),
                  full(w_out), full(bda), full(bdx), full(cw), full(lw), full(vec)],
        out_specs=[tile(4 * W), pl.BlockSpec((G_ROWS, W), lambda i: (0, 0)),
                   pl.BlockSpec((W, W), lambda i: (0, 0)), pl.BlockSpec((W, W), lambda i: (0, 0))],
        out_shape=[jax.ShapeDtypeStruct((T, 4 * W), BF16), jax.ShapeDtypeStruct((G_ROWS, W), F32),
                   jax.ShapeDtypeStruct((W, W), F32), jax.ShapeDtypeStruct((W, W), F32)],
        scratch_shapes=[pltpu.VMEM((tm + CONV_HALO, W), F32), pltpu.VMEM((tm + CONV_HALO, W), F32),
                        pltpu.VMEM((tm + LRU_HALO, W), F32), pltpu.VMEM((tm + LRU_HALO, W), F32),
                        pltpu.VMEM((tm, W), F32), pltpu.VMEM((8, W), F32), pltpu.VMEM((8, W), F32)],
        vmem_mib=56, args=(dx2, z, z, u1, xr, hst, hst, w_out, bda, bdx, cw, lw, vec), rider=rider)


def _mix_in_bwd(dz, x1, dx2, mix, g, w_in_t, tm, name, rider=None):
    T, D = x1.shape
    Z = w_in_t.shape[0]
    M = mix.shape[1]

    def body(dz_ref, x_ref, dx2_ref, mix_ref, g_ref, w_ref, dx1_ref, dg_ref, dwi_ref, dwo_ref, dob_ref, ai_ref, ao_ref):
        i = pl.program_id(0)

        @pl.when(i == 0)
        def _():
            dg_ref[...] = jnp.zeros_like(dg_ref)
            ai_ref[...] = jnp.zeros_like(ai_ref)
            ao_ref[...] = jnp.zeros_like(ao_ref)

        xhat, rstd = _rms_stats(x_ref[...])
        gv = g_ref[...]
        hb = (xhat * gv).astype(BF16)
        dzb = dz_ref[...]
        dx, dg = _rms_bwd(xhat, rstd, gv, _dot(dzb, w_ref[...], NN))
        dx2 = dx2_ref[...]
        dx1 = dx2 + dx
        dx1_ref[...] = dx1
        dob_ref[...] = (FFN_RES * dx1).astype(BF16)
        dg_ref[...] += dg
        ai_ref[...] += _dot(dzb, hb, TN)
        ao_ref[...] += _dot(mix_ref[...], dx2.astype(BF16), TN)

        @pl.when(i == pl.num_programs(0) - 1)
        def _():
            dwi_ref[...] = ai_ref[...].astype(BF16)
            dwo_ref[...] = ao_ref[...].astype(BF16)

    tile = lambda n: pl.BlockSpec((tm, n), lambda i: (i, 0))
    const = lambda r, c: pl.BlockSpec((r, c), lambda i: (0, 0))
    return _call(
        body, name=name, grid=(T // tm,),
        in_specs=[tile(Z), tile(D), tile(D), tile(M), const(1, D), const(Z, D)],
        out_specs=[tile(D), const(1, D), const(Z, D), const(M, D), tile(D)],
        out_shape=[jax.ShapeDtypeStruct((T, D), F32), jax.ShapeDtypeStruct((1, D), F32),
                   jax.ShapeDtypeStruct((Z, D), BF16), jax.ShapeDtypeStruct((M, D), BF16),
                   jax.ShapeDtypeStruct((T, D), BF16)],
        scratch_shapes=[pltpu.VMEM((Z, D), F32), pltpu.VMEM((M, D), F32)],
        vmem_mib=56, args=(dz, x1, dx2, mix, g, w_in_t), rider=rider)


def _final_loss(x3, g, target, tm, name):
    T, D = x3.shape

    def body(x_ref, g_ref, t_ref, dx_ref, dg_ref, loss_ref, dob_ref):
        @pl.when(pl.program_id(0) == 0)
        def _():
            dg_ref[...] = jnp.zeros_like(dg_ref)
            loss_ref[...] = jnp.zeros_like(loss_ref)

        xhat, rstd = _rms_stats(x_ref[...])
        gv = g_ref[...]
        err = xhat * gv - t_ref[...]
        loss_ref[...] += (0.5 / D) * jnp.sum(err * err)
        dx, dg = _rms_bwd(xhat, rstd, gv, err * (1.0 / D))
        dx_ref[...] = dx
        dob_ref[...] = (FFN_RES * dx).astype(BF16)
        dg_ref[...] += dg

    tile = pl.BlockSpec((tm, D), lambda i: (i, 0))
    return _call(
        body, name=name, grid=(T // tm,),
        in_specs=[tile, pl.BlockSpec((1, D), lambda i: (0, 0)), tile],
        out_specs=[tile, pl.BlockSpec((1, D), lambda i: (0, 0)), pl.BlockSpec((1, 128), lambda i: (0, 0)), tile],
        out_shape=[jax.ShapeDtypeStruct((T, D), F32), jax.ShapeDtypeStruct((1, D), F32),
                   jax.ShapeDtypeStruct((1, 128), F32), jax.ShapeDtypeStruct((T, D), BF16)],
        scratch_shapes=[], vmem_mib=32, args=(x3, g, target))


def _pair_add(full, recv, name):
    K, _, _, rows, D = full.shape

    def body(c_ref, a_ref, b_ref, o_ref):
        o_ref[...] = (a_ref[...].astype(F32) + b_ref[...].astype(F32)).astype(BF16)

    c = lax.axis_index("c").astype(jnp.int32).reshape((1,))
    return _call(
        body, name=name, grid=(K, N_CHIP), num_scalar_prefetch=1,
        in_specs=[pl.BlockSpec((None, None, None, rows, D), lambda k, q, c_ref: (k, q, c_ref[0], 0, 0)),
                  pl.BlockSpec((None, None, rows, D), lambda k, q, c_ref: (k, q, 0, 0))],
        out_specs=pl.BlockSpec((None, None, rows, D), lambda k, q, c_ref: (k, q, 0, 0)),
        out_shape=jax.ShapeDtypeStruct(recv.shape, BF16),
        scratch_shapes=[], vmem_mib=16, args=(c, full, recv))


def _adamw_update(wv, gv, mv, vv):
    m2 = ADAM_B1 * mv + (1.0 - ADAM_B1) * gv
    v2 = ADAM_B2 * vv + (1.0 - ADAM_B2) * (gv * gv)
    m_hat = m2 / (1.0 - ADAM_B1 ** ADAM_STEP)
    v_hat = v2 / (1.0 - ADAM_B2 ** ADAM_STEP)
    return -ADAM_LR * (m_hat / (jnp.sqrt(v_hat) + ADAM_EPS) + ADAM_WD * wv), m2, v2


def _finish(parts, k, w, m, v, transpose, name):
    _, _, rows, D = parts.shape

    def body(p_ref, w_ref, m_ref, v_ref, g_ref, d_ref, mo_ref, vo_ref):
        acc = p_ref[0].astype(F32)
        for q in range(1, N_CHIP):
            acc = acc + p_ref[q].astype(F32)
        gv = acc.T if transpose else acc
        g_ref[...] = gv
        d_ref[...], mo_ref[...], vo_ref[...] = _adamw_update(w_ref[...], gv, m_ref[...], v_ref[...])

    whole = pl.BlockSpec(w.shape, lambda i: (0, 0))
    return _call(
        body, name=name, grid=(1,),
        in_specs=[pl.BlockSpec((None, N_CHIP, rows, D), lambda i: (k, 0, 0, 0)), whole, whole, whole],
        out_specs=[whole] * 4, out_shape=[jax.ShapeDtypeStruct(w.shape, F32)] * 4,
        scratch_shapes=[], vmem_mib=40, args=(parts, w, m, v))


def _adamw(w, g, m, v, name):
    def body(w_ref, g_ref, m_ref, v_ref, d_ref, mo_ref, vo_ref):
        d_ref[...], mo_ref[...], vo_ref[...] = _adamw_update(w_ref[...], g_ref[...], m_ref[...], v_ref[...])

    shape = jax.ShapeDtypeStruct(w.shape, F32)
    return pl.pallas_call(
        body, name=name,
        in_specs=[VMEM_SPEC] * 4, out_specs=[VMEM_SPEC] * 3, out_shape=[shape] * 3,
        compiler_params=pltpu.CompilerParams(vmem_limit_bytes=32 * MIB),
    )(w, g, m, v)


def _block_diag(w):
    h, d, _ = w.shape
    out = jnp.zeros((h * d, h * d), w.dtype)
    for k in range(h):
        out = lax.dynamic_update_slice(out, w[k], (k * d, k * d))
    return out


def _diag_blocks(m, h):
    d = m.shape[0] // h
    return jnp.stack([m[k * d:(k + 1) * d, k * d:(k + 1) * d] for k in range(h)])


def _reduce_level1(full, tag):
    got = _run_comm(_sibling_comm(full), "rs_sibling_" + tag)
    return [_pair_add(a, b, "rs_pair_add_%s%d" % (tag, n)) for n, (a, b) in enumerate(zip(full, got))]


def kernel(x, ffn1_norm, ffn1_w_gate, ffn1_w_up, ffn1_w_down, mix_norm, w_in, conv_dw, conv_dw_bias, conv_ln_g, conv_ln_b, lru_conv_w, lru_conv_b, lru_w_a, lru_b_a, lru_w_x, lru_b_x, lru_lambda, w_out, ffn2_norm, ffn2_w_gate, ffn2_w_up, ffn2_w_down, final_norm, loss_target, m_ffn1_norm, m_ffn1_w_gate, m_ffn1_w_up, m_ffn1_w_down, m_mix_norm, m_w_in, m_conv_dw, m_conv_dw_bias, m_conv_ln_g, m_conv_ln_b, m_lru_conv_w, m_lru_conv_b, m_lru_w_a, m_lru_b_a, m_lru_w_x, m_lru_b_x, m_lru_lambda, m_w_out, m_ffn2_norm, m_ffn2_w_gate, m_ffn2_w_up, m_ffn2_w_down, m_final_norm, v_ffn1_norm, v_ffn1_w_gate, v_ffn1_w_up, v_ffn1_w_down, v_mix_norm, v_w_in, v_conv_dw, v_conv_dw_bias, v_conv_ln_g, v_conv_ln_b, v_lru_conv_w, v_lru_conv_b, v_lru_w_a, v_lru_b_a, v_lru_w_x, v_lru_b_x, v_lru_lambda, v_w_out, v_ffn2_norm, v_ffn2_w_gate, v_ffn2_w_up, v_ffn2_w_down, v_final_norm):
    T, D = x.shape[1], x.shape[2]
    F = ffn1_w_down.shape[0] * N_DEV
    rf = ffn1_w_down.shape[0]
    ri = w_in.shape[1]
    ro = w_out.shape[0]
    W = conv_dw_bias.shape[0]
    wc = conv_dw.shape[1]
    H = lru_w_a.shape[0]
    xs = x.reshape(T, D)
    tgt = loss_target.reshape(T, D)
    tm_ffn = min(256, T)
    cf = 256
    tm_w = min(1024, T)
    tm_mix = min(256, T)
    tf_w = F // 2
    row = lambda v: v.reshape(1, -1)
    by_owner = lambda a, rows: a.reshape(a.shape[0], N_CHIP, 2, rows, D)

    p3a, p3b, p_in, p_out = _prep_weights(
        (ffn1_w_gate, ffn1_w_up, ffn1_w_down), (ffn2_w_gate, ffn2_w_up, ffn2_w_down), w_in, w_out, "prep_weights")
    tile_rows = lambda a: jnp.pad(a, ((0, -a.shape[0] % SUBLANES), (0, 0)))
    p_cw = jnp.concatenate([tile_rows(conv_dw), tile_rows(lru_conv_w)], axis=0)
    lw_row = p_cw.shape[0] - SUBLANES
    stacked = lambda r, j: r.at[:, j]
    plain = lambda r, j: r.at[j]
    g3_shape = jax.ShapeDtypeStruct((3, N_DEV, rf, D), BF16)
    (g3a,) = _all_gather([p3a], [stacked], [g3_shape], "ag_ffn1")
    w3a = g3a.reshape(3, F, D)
    bda = _block_diag(lru_w_a).astype(BF16)
    bdx = _block_diag(lru_w_x).astype(BF16)
    vec = jnp.concatenate([tile_rows(v[None]) for v in
                           (conv_dw_bias, conv_ln_g, conv_ln_b, lru_conv_b, lru_b_a, lru_b_x, lru_lambda)], axis=0)

    gather_rest = _gather_comm(
        [p3b, p_in, p_out, p_cw], [stacked, plain, plain, plain],
        [g3_shape, jax.ShapeDtypeStruct((N_DEV, ri, D), BF16), jax.ShapeDtypeStruct((N_DEV, ro, D), BF16),
         jax.ShapeDtypeStruct((N_DEV,) + p_cw.shape, F32)],
        [(SIBLING,) + SAME_CORE, EVERYONE, EVERYONE, EVERYONE])
    (x1, h1, dau1, dag1, act1), (g3b_half, g_in, g_out, g_cw) = _ffn_fwd(
        xs, row(ffn1_norm), w3a, tm_ffn, cf, "ffn1_fwd", rider=gather_rest)
    w_in_t = g_in.reshape(N_DEV * ri, D)
    w_out_f = g_out.reshape(N_DEV * ro, D)
    cw_all = jnp.transpose(g_cw, (1, 0, 2)).reshape(p_cw.shape[0], N_DEV * wc)
    cw = cw_all[0:CONV_K]
    lw = cw_all[lw_row:lw_row + LRU_K]
    (x2, z, mix, u1, xr, hst), (g3b,) = _mix_core_fwd(
        x1, row(mix_norm), w_in_t, w_out_f, bda, bdx, cw, lw, vec, tm_mix, "mix_core_fwd",
        rider=_forward_comm([g3b_half], [stacked]))
    w3b = g3b.reshape(3, F, D)
    x3, h3, dau2, dag2, act2 = _ffn_fwd(x2, row(ffn2_norm), w3b, tm_ffn, cf, "ffn2_fwd")
    dx3, d_final_norm, loss_part, dob2 = _final_loss(x3, row(final_norm), tgt, min(512, T), "final_loss")

    dx2, dgate2, dup2, d_ffn2_norm = _ffn_dgrad(dx3, x2, row(ffn2_norm), dau2, dag2, w3b, tm_ffn, cf, "ffn2_dgrad")
    (dw_gu2,) = _wgrad([dgate2, dup2], h3, tm_w, tf_w, "ffn2_wgrad_gu")
    (dw_d2,) = _wgrad([act2], dob2, tm_w, tf_w, "ffn2_wgrad_d")
    sums_f2 = _reduce_level1([by_owner(dw_gu2, rf), by_owner(dw_d2, rf)], "f2")
    (dz, sg, dbda, dbdx), parts_f2 = _mix_core_bwd(
        dx2, z, u1, xr, hst, w_out_f, bda, bdx, cw, lw, vec, tm_mix, "mix_core_bwd", rider=_chips_comm(sums_f2))
    dx1, d_mix_norm, dw_in_t, dw_out, dob1 = _mix_in_bwd(dz, x1, dx2, mix, row(mix_norm), w_in_t, tm_mix, "mix_in_bwd")
    sums_io = _reduce_level1([by_owner(dw_in_t[None], ri), by_owner(dw_out[None], ro)], "io")
    (dw_d1,), parts_io = _wgrad([act1], dob1, tm_w, tf_w, "ffn1_wgrad_d", rider=_chips_comm(sums_io))
    sums_d1 = _reduce_level1([by_owner(dw_d1, rf)], "d1")
    dx0, dgate1, dup1, d_ffn1_norm = _ffn_dgrad(dx1, xs, row(ffn1_norm), dau1, dag1, w3a, tm_ffn, cf, "ffn1_dgrad")

    d_lru_w_a = _diag_blocks(dbda, H)
    d_lru_w_x = _diag_blocks(dbdx, H)
    small = [d_ffn1_norm, d_mix_norm, d_ffn2_norm, d_final_norm,
             sg[G_CB], sg[G_LNG], sg[G_LNB], sg[G_LCB], sg[G_BA], sg[G_BX], sg[G_LAM],
             d_lru_w_a, d_lru_w_x,
             sg[G_CW:G_CW + CONV_K], sg[G_LW:G_LW + LRU_K],
             loss_part]
    sizes = [a.size for a in small]
    flat = jnp.concatenate([a.reshape(-1) for a in small])
    n_rows = -(-flat.size // (8 * D)) * 8
    packed = jnp.concatenate([flat, jnp.zeros((n_rows * D - flat.size,), F32)]).reshape(n_rows, D)
    (dw_g1,), (red, *parts_d1) = _wgrad(
        [dgate1], h1, tm_w, tf_w, "ffn1_wgrad_g", rider=_both(_small_sum_comm(packed), _chips_comm(sums_d1)))
    sums_g1 = _reduce_level1([by_owner(dw_g1, rf)], "g1")
    (dw_u1,), parts_g1 = _wgrad([dup1], h1, tm_w, tf_w, "ffn1_wgrad_u", rider=_chips_comm(sums_g1))
    sums_u1 = _reduce_level1([by_owner(dw_u1, rf)], "u1")
    parts_u1 = _run_comm(_chips_comm(sums_u1), "rs_chips_u1")

    red = red.reshape(-1)
    offs = [0]
    for n in sizes:
        offs.append(offs[-1] + n)
    piece = lambda k, shape: red[offs[k]:offs[k + 1]].reshape(shape)
    loss = red[offs[15]]

    me = 4 * lax.axis_index("x") + 2 * lax.axis_index("y") + lax.axis_index("c")
    chan = lambda full_g: lax.dynamic_slice_in_dim(full_g, me * wc, wc, axis=1)
    grads = {
        "ffn1_norm": piece(0, (D,)), "mix_norm": piece(1, (D,)), "ffn2_norm": piece(2, (D,)), "final_norm": piece(3, (D,)),
        "conv_dw_bias": piece(4, (W,)), "conv_ln_g": piece(5, (W,)), "conv_ln_b": piece(6, (W,)),
        "lru_conv_b": piece(7, (W,)), "lru_b_a": piece(8, (W,)), "lru_b_x": piece(9, (W,)), "lru_lambda": piece(10, (W,)),
        "lru_w_a": piece(11, lru_w_a.shape), "lru_w_x": piece(12, lru_w_x.shape),
        "conv_dw": chan(piece(13, (CONV_K, W))), "lru_conv_w": chan(piece(14, (LRU_K, W))),
    }

    weights = dict(ffn1_norm=ffn1_norm, ffn1_w_gate=ffn1_w_gate, ffn1_w_up=ffn1_w_up, ffn1_w_down=ffn1_w_down, mix_norm=mix_norm, w_in=w_in, conv_dw=conv_dw, conv_dw_bias=conv_dw_bias, conv_ln_g=conv_ln_g, conv_ln_b=conv_ln_b, lru_conv_w=lru_conv_w, lru_conv_b=lru_conv_b, lru_w_a=lru_w_a, lru_b_a=lru_b_a, lru_w_x=lru_w_x, lru_b_x=lru_b_x, lru_lambda=lru_lambda, w_out=w_out, ffn2_norm=ffn2_norm, ffn2_w_gate=ffn2_w_gate, ffn2_w_up=ffn2_w_up, ffn2_w_down=ffn2_w_down, final_norm=final_norm)
    moment1 = dict(ffn1_norm=m_ffn1_norm, ffn1_w_gate=m_ffn1_w_gate, ffn1_w_up=m_ffn1_w_up, ffn1_w_down=m_ffn1_w_down, mix_norm=m_mix_norm, w_in=m_w_in, conv_dw=m_conv_dw, conv_dw_bias=m_conv_dw_bias, conv_ln_g=m_conv_ln_g, conv_ln_b=m_conv_ln_b, lru_conv_w=m_lru_conv_w, lru_conv_b=m_lru_conv_b, lru_w_a=m_lru_w_a, lru_b_a=m_lru_b_a, lru_w_x=m_lru_w_x, lru_b_x=m_lru_b_x, lru_lambda=m_lru_lambda, w_out=m_w_out, ffn2_norm=m_ffn2_norm, ffn2_w_gate=m_ffn2_w_gate, ffn2_w_up=m_ffn2_w_up, ffn2_w_down=m_ffn2_w_down, final_norm=m_final_norm)
    moment2 = dict(ffn1_norm=v_ffn1_norm, ffn1_w_gate=v_ffn1_w_gate, ffn1_w_up=v_ffn1_w_up, ffn1_w_down=v_ffn1_w_down, mix_norm=v_mix_norm, w_in=v_w_in, conv_dw=v_conv_dw, conv_dw_bias=v_conv_dw_bias, conv_ln_g=v_conv_ln_g, conv_ln_b=v_conv_ln_b, lru_conv_w=v_lru_conv_w, lru_conv_b=v_lru_conv_b, lru_w_a=v_lru_w_a, lru_b_a=v_lru_b_a, lru_w_x=v_lru_w_x, lru_b_x=v_lru_b_x, lru_lambda=v_lru_lambda, w_out=v_w_out, ffn2_norm=v_ffn2_norm, ffn2_w_gate=v_ffn2_w_gate, ffn2_w_up=v_ffn2_w_up, ffn2_w_down=v_ffn2_w_down, final_norm=v_final_norm)
    order = list(weights)
    big = {"ffn1_w_gate": (parts_g1[0], 0, True), "ffn1_w_up": (parts_u1[0], 0, True),
           "ffn1_w_down": (parts_d1[0], 0, False), "w_in": (parts_io[0], 0, True), "w_out": (parts_io[1], 0, False),
           "ffn2_w_gate": (parts_f2[0], 0, True), "ffn2_w_up": (parts_f2[0], 1, True),
           "ffn2_w_down": (parts_f2[1], 0, False)}
    delta, new_m, new_v = {}, {}, {}
    for n, (parts, k, transposed) in big.items():
        grads[n], delta[n], new_m[n], new_v[n] = _finish(
            parts, k, weights[n], moment1[n], moment2[n], transposed, "finish_" + n)
    rest = [n for n in order if n not in big]
    rest_sizes = [weights[n].size for n in rest]
    n_flat = sum(rest_sizes)
    pad_rows = -(-n_flat // (8 * 128)) * 8

    def pack(d):
        f = jnp.concatenate([d[n].reshape(-1) for n in rest])
        return jnp.concatenate([f, jnp.zeros((pad_rows * 128 - n_flat,), F32)]).reshape(pad_rows, 128)

    d_s, m_s, v_s = _adamw(pack(weights), pack(grads), pack(moment1), pack(moment2), "adamw_small")
    off = 0
    for n, size in zip(rest, rest_sizes):
        shape = weights[n].shape
        delta[n] = d_s.reshape(-1)[off:off + size].reshape(shape)
        new_m[n] = m_s.reshape(-1)[off:off + size].reshape(shape)
        new_v[n] = v_s.reshape(-1)[off:off + size].reshape(shape)
        off += size

    return (loss, dx0.reshape(x.shape), *[grads[n] for n in order], *[delta[n] for n in order],
            *[new_m[n] for n in order], *[new_v[n] for n in order])
```

```python
import functools
import math

import jax
import jax.numpy as jnp
from jax import lax
from jax.experimental import pallas as pl
from jax.experimental.pallas import tpu as pltpu

F32 = jnp.float32
BF16 = jnp.bfloat16
MESH = pl.DeviceIdType.MESH

N_DEV = 8
N_CHIP = 4
SUBLANES = 8
RMS_EPS = 1e-6
LN_EPS = 1e-5
LRU_C = 8.0
CONV_K = 31
LRU_K = 4
CONV_HALO = 32
LRU_HALO = 8
FFN_RES = 0.5
ADAM_LR, ADAM_B1, ADAM_B2, ADAM_EPS, ADAM_WD, ADAM_STEP = 0.001, 0.9, 0.999, 1e-08, 0.01, 10
GELU_K = math.sqrt(2.0 / math.pi)
GELU_C = 0.044715

MIB = 1024 * 1024
NT = (((1,), (1,)), ((), ()))
NN = (((1,), (0,)), ((), ()))
TN = (((0,), (0,)), ((), ()))

V_CB, V_LNG, V_LNB, V_LCB, V_BA, V_BX, V_LAM = range(0, 7 * SUBLANES, SUBLANES)
G_CW = 0
G_CB, G_LNG, G_LNB = 31, 32, 33
G_LW = 34
G_LCB, G_BA, G_BX, G_LAM = 38, 39, 40, 41
G_ROWS = 48

HBM_SPEC = pl.BlockSpec(memory_space=pltpu.HBM)
VMEM_SPEC = pl.BlockSpec(memory_space=pltpu.VMEM)


def _dot(a, b, dims):
    return lax.dot_general(a, b, dims, preferred_element_type=F32)


def _sigmoid(x):
    return 1.0 / (1.0 + jnp.exp(-x))


def _gelu_parts(x):
    x2 = x * x
    th = jnp.tanh(GELU_K * x * (1.0 + GELU_C * x2))
    gl = 0.5 * x * (1.0 + th)
    dgl = 0.5 * (1.0 + th) + 0.5 * x * (1.0 - th * th) * GELU_K * (1.0 + 3.0 * GELU_C * x2)
    return gl, dgl


def _neg_expm1(y):
    series = -y * (1.0 + y * (1.0 / 2) * (1.0 + y * (1.0 / 3) * (1.0 + y * (1.0 / 4) * (1.0 + y * (1.0 / 5) * (1.0 + y * (1.0 / 6))))))
    return jnp.where(y > -0.25, series, 1.0 - jnp.exp(y))


def _softplus_neg(lam):
    t = -lam
    e = jnp.exp(-jnp.abs(t))
    s = 1.0 + e
    log1p_e = jnp.log(s) - ((s - 1.0) - e) / s
    return jnp.maximum(t, 0.0) + log1p_e


def _rms_stats(xv):
    rstd = lax.rsqrt(jnp.mean(xv * xv, axis=-1, keepdims=True) + RMS_EPS)
    return xv * rstd, rstd


def _rms_bwd(xhat, rstd, g, dh):
    dxhat = dh * g
    dx = rstd * (dxhat - xhat * jnp.mean(dxhat * xhat, axis=-1, keepdims=True))
    return dx, jnp.sum(dh * xhat, axis=0, keepdims=True)


def _row_windows(buf_ref, n_rows, offsets):
    total = buf_ref.shape[0]
    full = buf_ref[...]
    for b in range(SUBLANES):
        offs = [o for o in offsets if o % SUBLANES == b]
        if not offs:
            continue
        assert max(offs) + n_rows <= total
        moved = full if b == 0 else pltpu.roll(full, total - b, 0)
        for o in offs:
            yield o, moved[o - b:o - b + n_rows, :]


def _scan_rows(av, bv, edge, out_ref, reverse=False):
    tm, W = av.shape
    sub = lax.broadcasted_iota(jnp.int32, (tm, W), 0) % SUBLANES
    s = 1
    while s < SUBLANES:
        keep = (sub < SUBLANES - s) if reverse else (sub >= s)
        shift = tm - s if reverse else s
        bv = jnp.where(keep, av * pltpu.roll(bv, shift, 0) + bv, bv)
        av = jnp.where(keep, av * pltpu.roll(av, shift, 0), av)
        s *= 2
    starts = range(0, tm, SUBLANES)
    for r0 in (reversed(starts) if reverse else starts):
        group = av[r0:r0 + SUBLANES, :] * edge + bv[r0:r0 + SUBLANES, :]
        out_ref[r0:r0 + SUBLANES, :] = group
        edge = group[0:1, :] if reverse else group[SUBLANES - 1:SUBLANES, :]
    return edge


class _Comm:
    def __init__(self, arrays, in_specs, out_shapes, out_specs, scratch, start, wait, aliases=None):
        self.arrays, self.in_specs = list(arrays), list(in_specs)
        self.out_shapes, self.out_specs = list(out_shapes), list(out_specs)
        self.scratch, self.start, self.wait = list(scratch), start, wait
        self.aliases = dict(aliases or {})


def _in_hbm(a):
    return pltpu.with_memory_space_constraint(a, pltpu.HBM)


def _operands(comm):
    return [a if spec is VMEM_SPEC else _in_hbm(a) for a, spec in zip(comm.arrays, comm.in_specs)]


def _call(body, *, name, grid, in_specs, out_specs, out_shape, scratch_shapes, vmem_mib, args, rider=None,
          num_scalar_prefetch=0):
    params = pltpu.CompilerParams(dimension_semantics=("arbitrary",) * len(grid), vmem_limit_bytes=vmem_mib * MIB)
    args = [a if k < num_scalar_prefetch else _in_hbm(a) for k, a in enumerate(args)]
    if rider is None:
        return pl.pallas_call(
            body, name=name,
            grid_spec=pltpu.PrefetchScalarGridSpec(
                num_scalar_prefetch=num_scalar_prefetch, grid=grid, in_specs=in_specs, out_specs=out_specs,
                scratch_shapes=scratch_shapes),
            out_shape=out_shape, compiler_params=params)(*args)
    assert num_scalar_prefetch == 0
    n_in, n_out, n_scr = len(in_specs), len(out_specs), len(scratch_shapes)
    r_in, r_out = len(rider.arrays), len(rider.out_shapes)
    n_axes = len(grid)

    def carried(*refs):
        pos = [0]

        def take(n):
            pos[0] += n
            return refs[pos[0] - n:pos[0]]

        ins, r_ins, outs, r_outs, scr, r_scr = take(n_in), take(r_in), take(n_out), take(r_out), take(n_scr), take(len(rider.scratch))
        first = pl.program_id(0) == 0
        last = pl.program_id(0) == grid[0] - 1
        for ax in range(1, n_axes):
            first = first & (pl.program_id(ax) == 0)
            last = last & (pl.program_id(ax) == grid[ax] - 1)

        @pl.when(first)
        def _():
            rider.start(r_ins, r_outs, r_scr)

        body(*ins, *outs, *scr)

        @pl.when(last)
        def _():
            rider.wait(r_ins, r_outs, r_scr)

    res = pl.pallas_call(
        carried, name=name,
        grid=grid,
        in_specs=list(in_specs) + rider.in_specs,
        out_specs=list(out_specs) + rider.out_specs,
        out_shape=list(out_shape) + rider.out_shapes,
        scratch_shapes=list(scratch_shapes) + rider.scratch,
        input_output_aliases={n_in + i: n_out + o for i, o in rider.aliases.items()},
        compiler_params=params)(*args, *_operands(rider))
    return res[:n_out], res[n_out:]


def _run_comm(comm, name):
    n_in, n_out = len(comm.arrays), len(comm.out_shapes)

    def body(*refs):
        ins, outs, scr = refs[:n_in], refs[n_in:n_in + n_out], refs[n_in + n_out:]
        comm.start(ins, outs, scr)
        comm.wait(ins, outs, scr)

    return pl.pallas_call(
        body, name=name,
        in_specs=comm.in_specs, out_specs=comm.out_specs, out_shape=comm.out_shapes,
        scratch_shapes=comm.scratch, input_output_aliases=comm.aliases,
        compiler_params=pltpu.CompilerParams(vmem_limit_bytes=24 * MIB))(*_operands(comm))


def _both(a, b):
    ni, no, ns = len(a.arrays), len(a.out_shapes), len(a.scratch)

    def start(ins, outs, scr):
        a.start(ins[:ni], outs[:no], scr[:ns])
        b.start(ins[ni:], outs[no:], scr[ns:])

    def wait(ins, outs, scr):
        a.wait(ins[:ni], outs[:no], scr[:ns])
        b.wait(ins[ni:], outs[no:], scr[ns:])

    aliases = dict(a.aliases)
    aliases.update({ni + i: no + o for i, o in b.aliases.items()})
    return _Comm(a.arrays + b.arrays, a.in_specs + b.in_specs, a.out_shapes + b.out_shapes,
                 a.out_specs + b.out_specs, a.scratch + b.scratch, start, wait, aliases)


def _place():
    return lax.axis_index("x"), lax.axis_index("y"), lax.axis_index("c")


def _peer(k):
    x, y, c = _place()
    px, py, pc = x ^ ((k >> 2) & 1), y ^ ((k >> 1) & 1), c ^ (k & 1)
    return (px, py, pc), 4 * px + 2 * py + pc


SIBLING = 1
SAME_CORE = (2, 4, 6)
EVERYONE = tuple(range(1, N_DEV))


def _gather_comm(shards, views, out_shapes, relations):
    na = len(shards)

    def copies(ins, outs, scr):
        send_sems, recv_sems, _ = scr
        _, me = _peer(0)
        out = []
        for a in range(na):
            for k in relations[a]:
                peer, theirs = _peer(k)
                send = functools.partial(
                    pltpu.make_async_remote_copy,
                    src_ref=ins[a], dst_ref=views[a](outs[a], me),
                    send_sem=send_sems.at[7 * a + k - 1], recv_sem=recv_sems.at[7 * a + k - 1],
                    device_id=peer, device_id_type=MESH)
                recv = functools.partial(
                    pltpu.make_async_remote_copy,
                    src_ref=ins[a], dst_ref=views[a](outs[a], theirs),
                    send_sem=send_sems.at[7 * a + k - 1], recv_sem=recv_sems.at[7 * a + k - 1],
                    device_id=peer, device_id_type=MESH)
                out.append((send, recv))
        return out

    def local(ins, outs, scr):
        _, me = _peer(0)
        return [pltpu.make_async_copy(ins[a], views[a](outs[a], me), scr[2].at[a]) for a in range(na)]

    def start(ins, outs, scr):
        for cp in local(ins, outs, scr):
            cp.start()
        for send, _ in copies(ins, outs, scr):
            send().start()

    def wait(ins, outs, scr):
        for _, recv in copies(ins, outs, scr):
            recv().wait_recv()
        for send, _ in copies(ins, outs, scr):
            send().wait_send()
        for cp in local(ins, outs, scr):
            cp.wait()

    return _Comm(shards, [HBM_SPEC] * na, out_shapes, [HBM_SPEC] * na,
                 [pltpu.SemaphoreType.DMA((7 * na,)), pltpu.SemaphoreType.DMA((7 * na,)),
                  pltpu.SemaphoreType.DMA((na,))], start, wait)


def _forward_comm(gathered, views):
    na = len(gathered)
    shapes = [jax.ShapeDtypeStruct(g.shape, g.dtype) for g in gathered]

    def copies(outs, scr):
        send_sems, recv_sems = scr
        sibling, _ = _peer(SIBLING)
        out = []
        for a in range(na):
            for n, k in enumerate(SAME_CORE):
                _, mine = _peer(k)
                _, theirs = _peer(k ^ SIBLING)
                send = functools.partial(
                    pltpu.make_async_remote_copy,
                    src_ref=views[a](outs[a], mine), dst_ref=views[a](outs[a], mine),
                    send_sem=send_sems.at[3 * a + n], recv_sem=recv_sems.at[3 * a + n],
                    device_id=sibling, device_id_type=MESH)
                recv = functools.partial(
                    pltpu.make_async_remote_copy,
                    src_ref=views[a](outs[a], mine), dst_ref=views[a](outs[a], theirs),
                    send_sem=send_sems.at[3 * a + n], recv_sem=recv_sems.at[3 * a + n],
                    device_id=sibling, device_id_type=MESH)
                out.append((send, recv))
        return out

    def start(ins, outs, scr):
        for send, _ in copies(outs, scr):
            send().start()

    def wait(ins, outs, scr):
        for _, recv in copies(outs, scr):
            recv().wait_recv()
        for send, _ in copies(outs, scr):
            send().wait_send()

    return _Comm(gathered, [HBM_SPEC] * na, shapes, [HBM_SPEC] * na,
                 [pltpu.SemaphoreType.DMA((3 * na,)), pltpu.SemaphoreType.DMA((3 * na,))], start, wait,
                 aliases={a: a for a in range(na)})


def _all_gather(shards, views, out_shapes, name):
    na = len(shards)
    level1 = _gather_comm(shards, views, out_shapes, [(SIBLING,) + SAME_CORE] * na)

    def body(*refs):
        ins, outs = refs[:na], refs[na:2 * na]
        send_sems, recv_sems, local_sems, fwd_send, fwd_recv = refs[2 * na:]
        sibling, _ = _peer(SIBLING)
        scr = (send_sems, recv_sems, local_sems)
        level1.start(ins, outs, scr)
        passed, landing = [], []
        for a in range(na):
            for n, k in enumerate(SAME_CORE):
                peer, mine = _peer(k)
                _, theirs = _peer(k ^ SIBLING)
                pltpu.make_async_remote_copy(
                    src_ref=ins[a], dst_ref=views[a](outs[a], mine),
                    send_sem=send_sems.at[7 * a + k - 1], recv_sem=recv_sems.at[7 * a + k - 1],
                    device_id=peer, device_id_type=MESH).wait_recv()
                fwd = pltpu.make_async_remote_copy(
                    src_ref=views[a](outs[a], mine), dst_ref=views[a](outs[a], mine),
                    send_sem=fwd_send.at[3 * a + n], recv_sem=fwd_recv.at[3 * a + n],
                    device_id=sibling, device_id_type=MESH)
                fwd.start()
                passed.append(fwd)
                landing.append(pltpu.make_async_remote_copy(
                    src_ref=views[a](outs[a], mine), dst_ref=views[a](outs[a], theirs),
                    send_sem=fwd_send.at[3 * a + n], recv_sem=fwd_recv.at[3 * a + n],
                    device_id=sibling, device_id_type=MESH))
        for a in range(na):
            _, theirs = _peer(SIBLING)
            pltpu.make_async_remote_copy(
                src_ref=ins[a], dst_ref=views[a](outs[a], theirs),
                send_sem=send_sems.at[7 * a + SIBLING - 1], recv_sem=recv_sems.at[7 * a + SIBLING - 1],
                device_id=sibling, device_id_type=MESH).wait_recv()
        for cp in landing:
            cp.wait_recv()
        for cp in passed:
            cp.wait_send()
        _, me = _peer(0)
        for a in range(na):
            for k in (SIBLING,) + SAME_CORE:
                peer, _ = _peer(k)
                pltpu.make_async_remote_copy(
                    src_ref=ins[a], dst_ref=views[a](outs[a], me),
                    send_sem=send_sems.at[7 * a + k - 1], recv_sem=recv_sems.at[7 * a + k - 1],
                    device_id=peer, device_id_type=MESH).wait_send()
            pltpu.make_async_copy(ins[a], views[a](outs[a], me), local_sems.at[a]).wait()

    return pl.pallas_call(
        body, name=name,
        in_specs=[HBM_SPEC] * na, out_specs=[HBM_SPEC] * na, out_shape=out_shapes,
        scratch_shapes=level1.scratch + [pltpu.SemaphoreType.DMA((3 * na,)), pltpu.SemaphoreType.DMA((3 * na,))],
    )(*[_in_hbm(s) for s in shards])


def _sibling_comm(grads):
    na = len(grads)
    shapes = [jax.ShapeDtypeStruct(g.shape[:2] + g.shape[3:], g.dtype) for g in grads]

    def copies(ins, outs, scr):
        x, y, c = _place()
        return [pltpu.make_async_remote_copy(
            src_ref=ins[a].at[:, :, 1 - c], dst_ref=outs[a],
            send_sem=scr[0].at[a], recv_sem=scr[1].at[a],
            device_id=(x, y, 1 - c), device_id_type=MESH) for a in range(na)]

    def start(ins, outs, scr):
        for cp in copies(ins, outs, scr):
            cp.start()

    def wait(ins, outs, scr):
        for cp in copies(ins, outs, scr):
            cp.wait()

    return _Comm(grads, [HBM_SPEC] * na, shapes, [HBM_SPEC] * na,
                 [pltpu.SemaphoreType.DMA((na,)), pltpu.SemaphoreType.DMA((na,))], start, wait)


def _chips_comm(sums):
    na = len(sums)
    shapes = [jax.ShapeDtypeStruct(s.shape, s.dtype) for s in sums]

    def copies(ins, outs, scr):
        x, y, c = _place()
        mine = 2 * x + y
        out = []
        for a in range(na):
            for n, k in enumerate(SAME_CORE):
                (px, py, pc), _ = _peer(k)
                theirs = 2 * px + py
                send = functools.partial(
                    pltpu.make_async_remote_copy,
                    src_ref=ins[a].at[:, theirs], dst_ref=outs[a].at[:, mine],
                    send_sem=scr[0].at[3 * a + n], recv_sem=scr[1].at[3 * a + n],
                    device_id=(px, py, pc), device_id_type=MESH)
                recv = functools.partial(
                    pltpu.make_async_remote_copy,
                    src_ref=ins[a].at[:, mine], dst_ref=outs[a].at[:, theirs],
                    send_sem=scr[0].at[3 * a + n], recv_sem=scr[1].at[3 * a + n],
                    device_id=(px, py, pc), device_id_type=MESH)
                out.append((send, recv))
        return out

    def local(ins, outs, scr):
        x, y, _ = _place()
        mine = 2 * x + y
        return [pltpu.make_async_copy(ins[a].at[:, mine], outs[a].at[:, mine], scr[2].at[a]) for a in range(na)]

    def start(ins, outs, scr):
        for cp in local(ins, outs, scr):
            cp.start()
        for send, _ in copies(ins, outs, scr):
            send().start()

    def wait(ins, outs, scr):
        for _, recv in copies(ins, outs, scr):
            recv().wait_recv()
        for send, _ in copies(ins, outs, scr):
            send().wait_send()
        for cp in local(ins, outs, scr):
            cp.wait()

    return _Comm(sums, [HBM_SPEC] * na, shapes, [HBM_SPEC] * na,
                 [pltpu.SemaphoreType.DMA((3 * na,)), pltpu.SemaphoreType.DMA((3 * na,)),
                  pltpu.SemaphoreType.DMA((na,))], start, wait)


def _small_sum_comm(s):
    R, C = s.shape

    def copies(ins, scr):
        buf, send_sems, recv_sems = scr
        _, me = _peer(0)
        out = []
        for k in EVERYONE:
            peer, theirs = _peer(k)
            send = functools.partial(
                    pltpu.make_async_remote_copy,
                src_ref=ins[0], dst_ref=buf.at[me], send_sem=send_sems.at[k - 1], recv_sem=recv_sems.at[k - 1],
                device_id=peer, device_id_type=MESH)
            recv = functools.partial(
                    pltpu.make_async_remote_copy,
                src_ref=ins[0], dst_ref=buf.at[theirs], send_sem=send_sems.at[k - 1], recv_sem=recv_sems.at[k - 1],
                device_id=peer, device_id_type=MESH)
            out.append((send, recv))
        return out

    def start(ins, outs, scr):
        _, me = _peer(0)
        scr[0][me] = ins[0][...]
        for send, _ in copies(ins, scr):
            send().start()

    def wait(ins, outs, scr):
        for _, recv in copies(ins, scr):
            recv().wait_recv()
        for send, _ in copies(ins, scr):
            send().wait_send()
        acc = scr[0][0]
        for j in range(1, N_DEV):
            acc = acc + scr[0][j]
        outs[0][...] = acc

    return _Comm([s], [VMEM_SPEC], [jax.ShapeDtypeStruct((R, C), F32)], [VMEM_SPEC],
                 [pltpu.VMEM((N_DEV, R, C), F32), pltpu.SemaphoreType.DMA((N_DEV - 1,)),
                  pltpu.SemaphoreType.DMA((N_DEV - 1,))], start, wait)


def _prep_weights(ffn1, ffn2, w_in, w_out, name):
    rf, D = ffn1[2].shape
    ri, ro = w_in.shape[1], w_out.shape[0]

    def body(g1, u1, d1, g2, u2, d2, wi, wo, p1_ref, p2_ref, pi_ref, po_ref):
        for p_ref, (gate, up, down) in ((p1_ref, (g1, u1, d1)), (p2_ref, (g2, u2, d2))):
            p_ref[0] = gate[...].T.astype(BF16)
            p_ref[1] = up[...].T.astype(BF16)
            p_ref[2] = down[...].astype(BF16)
        pi_ref[...] = wi[...].T.astype(BF16)
        po_ref[...] = wo[...].astype(BF16)

    args = (*ffn1, *ffn2, w_in, w_out)
    whole = lambda shape: pl.BlockSpec(shape, lambda i: (0,) * len(shape))
    out_shapes = [(3, rf, D), (3, rf, D), (ri, D), (ro, D)]
    return _call(
        body, name=name, grid=(1,),
        in_specs=[whole(a.shape) for a in args], out_specs=[whole(s) for s in out_shapes],
        out_shape=[jax.ShapeDtypeStruct(s, BF16) for s in out_shapes],
        scratch_shapes=[], vmem_mib=48, args=args)


def _load_weights(w_hbm, w_vmem, sem):
    @pl.when(pl.program_id(0) == 0)
    def _():
        copies = [pltpu.make_async_copy(w_hbm.at[k], w_vmem.at[k], sem.at[k]) for k in range(3)]
        for cp in copies:
            cp.start()
        for cp in copies:
            cp.wait()


def _ffn_fwd(x, g, w3, tm, cf, name, rider=None, head=None):
    T, D = x.shape
    F = w3.shape[1]
    n_head = 0 if head is None else 2

    def body(x_ref, g_ref, w_hbm, *refs):
        head_refs, refs = refs[:n_head], refs[n_head:]
        if head is None:
            (xo_ref, h_ref, dau_ref, dag_ref, act_ref, wv, sem) = refs
        else:
            (dx_ref, dob_ref, dgf_ref, loss_ref, h_ref, dau_ref, dag_ref, act_ref, wv, sem) = refs
        _load_weights(w_hbm, wv, sem)
        xhat, _ = _rms_stats(x_ref[...])
        hb = (xhat * g_ref[...]).astype(BF16)
        h_ref[...] = hb
        for lo in range(0, F, cf):
            gate = _dot(hb, wv[0, lo:lo + cf, :], NT)
            up = _dot(hb, wv[1, lo:lo + cf, :], NT)
            sig = _sigmoid(gate)
            silu = gate * sig
            dau_ref[:, lo:lo + cf] = silu.astype(BF16)
            dag_ref[:, lo:lo + cf] = (up * (sig * (1.0 + gate * (1.0 - sig)))).astype(BF16)
            act_ref[:, lo:lo + cf] = (silu * up).astype(BF16)
        x_out = x_ref[...] + FFN_RES * _dot(act_ref[...], wv[2], NN)
        if head is None:
            xo_ref[...] = x_out
            return

        @pl.when(pl.program_id(0) == 0)
        def _():
            dgf_ref[...] = jnp.zeros_like(dgf_ref)
            loss_ref[...] = jnp.zeros_like(loss_ref)

        gf_ref, tgt_ref = head_refs
        yhat, rstd = _rms_stats(x_out)
        gf = gf_ref[...]
        err = yhat * gf - tgt_ref[...]
        loss_ref[...] += (0.5 / D) * jnp.sum(err * err)
        dx, dgf = _rms_bwd(yhat, rstd, gf, err * (1.0 / D))
        dx_ref[...] = dx
        dob_ref[...] = (FFN_RES * dx).astype(BF16)
        dgf_ref[...] += dgf

    row = pl.BlockSpec((tm, D), lambda i: (i, 0))
    hid = pl.BlockSpec((tm, F), lambda i: (i, 0))
    vec = pl.BlockSpec((1, D), lambda i: (0, 0))
    row_f32, row_bf16 = jax.ShapeDtypeStruct((T, D), F32), jax.ShapeDtypeStruct((T, D), BF16)
    if head is None:
        first_specs, first_shapes = [row], [row_f32]
    else:
        first_specs = [row, row, vec, pl.BlockSpec((1, 128), lambda i: (0, 0))]
        first_shapes = [row_f32, row_bf16, jax.ShapeDtypeStruct((1, D), F32), jax.ShapeDtypeStruct((1, 128), F32)]
    return _call(
        body, name=name, grid=(T // tm,),
        in_specs=[row, vec, HBM_SPEC] + ([] if head is None else [vec, row]),
        out_specs=first_specs + [row, hid, hid, hid],
        out_shape=first_shapes + [row_bf16] + [jax.ShapeDtypeStruct((T, F), BF16)] * 3,
        scratch_shapes=[pltpu.VMEM((3, F, D), BF16), pltpu.SemaphoreType.DMA((3,))],
        vmem_mib=48, args=(x, g, w3) + (() if head is None else tuple(head)), rider=rider)


def _ffn_dgrad(dout, x, g, dau, dag, w3, tm, cf, name, rider=None):
    T, D = x.shape
    F = w3.shape[1]

    def body(do_ref, x_ref, g_ref, dau_ref, dag_ref, w_hbm, dx_ref, dgate_ref, dup_ref, dg_ref, wv, sem):
        _load_weights(w_hbm, wv, sem)

        @pl.when(pl.program_id(0) == 0)
        def _():
            dg_ref[...] = jnp.zeros_like(dg_ref)

        dob = (FFN_RES * do_ref[...]).astype(BF16)
        for lo in range(0, F, cf):
            dact = _dot(dob, wv[2, lo:lo + cf, :], NT)
            dup_ref[:, lo:lo + cf] = (dact * dau_ref[:, lo:lo + cf].astype(F32)).astype(BF16)
            dgate_ref[:, lo:lo + cf] = (dact * dag_ref[:, lo:lo + cf].astype(F32)).astype(BF16)
        dh = _dot(dgate_ref[...], wv[0], NN) + _dot(dup_ref[...], wv[1], NN)
        xhat, rstd = _rms_stats(x_ref[...])
        dx, dg = _rms_bwd(xhat, rstd, g_ref[...], dh)
        dx_ref[...] = do_ref[...] + dx
        dg_ref[...] += dg

    row = pl.BlockSpec((tm, D), lambda i: (i, 0))
    hid = pl.BlockSpec((tm, F), lambda i: (i, 0))
    vec = pl.BlockSpec((1, D), lambda i: (0, 0))
    return _call(
        body, name=name, grid=(T // tm,),
        in_specs=[row, row, vec, hid, hid, HBM_SPEC],
        out_specs=[row, hid, hid, vec],
        out_shape=[jax.ShapeDtypeStruct((T, D), F32), jax.ShapeDtypeStruct((T, F), BF16),
                   jax.ShapeDtypeStruct((T, F), BF16), jax.ShapeDtypeStruct((1, D), F32)],
        scratch_shapes=[pltpu.VMEM((3, F, D), BF16), pltpu.SemaphoreType.DMA((3,))],
        vmem_mib=52, args=(dout, x, g, dau, dag, w3), rider=rider)


def _wgrad(lhs, rhs, tm, tf, name, rider=None):
    T, F = lhs[0].shape
    D = rhs.shape[1]
    K = len(lhs)

    def body(*refs):
        lhs_refs, rhs_ref, dw_ref, accs = refs[:K], refs[K], refs[K + 1], refs[K + 2:]
        i = pl.program_id(1)

        @pl.when(i == 0)
        def _():
            for acc in accs:
                acc[...] = jnp.zeros_like(acc)

        rv = rhs_ref[...]
        for acc, lhs_ref in zip(accs, lhs_refs):
            acc[...] += _dot(lhs_ref[...], rv, TN)

        @pl.when(i == pl.num_programs(1) - 1)
        def _():
            for k, acc in enumerate(accs):
                dw_ref[k] = acc[...].astype(BF16)

    hid = pl.BlockSpec((tm, tf), lambda f, i: (i, f))
    return _call(
        body, name=name, grid=(F // tf, T // tm),
        in_specs=[hid] * K + [pl.BlockSpec((tm, D), lambda f, i: (i, 0))],
        out_specs=[pl.BlockSpec((K, tf, D), lambda f, i: (0, f, 0))],
        out_shape=[jax.ShapeDtypeStruct((K, F, D), BF16)],
        scratch_shapes=[pltpu.VMEM((tf, D), F32)] * K,
        vmem_mib=56, args=(*lhs, rhs), rider=rider)


def _lru_gates(xr, bda_ref, bdx_ref, vec_ref):
    xrb = xr.astype(BF16)
    r = _sigmoid(_dot(xrb, bda_ref[...], NN) + vec_ref[V_BA:V_BA + 1, :])
    ig = _sigmoid(_dot(xrb, bdx_ref[...], NN) + vec_ref[V_BX:V_BX + 1, :])
    sp = _softplus_neg(vec_ref[V_LAM:V_LAM + 1, :])
    log_a = (-LRU_C * sp) * r
    a = jnp.exp(log_a)
    mult = jnp.sqrt(_neg_expm1(2.0 * log_a))
    return xrb, r, ig, sp, a, mult


def _layernorm_stats(u1):
    xc = u1 - jnp.mean(u1, axis=-1, keepdims=True)
    rs = lax.rsqrt(jnp.mean(xc * xc, axis=-1, keepdims=True) + LN_EPS)
    return xc * rs, rs


def _mix_core_fwd(x1, g, w_in_t, w_out, bda, bdx, cw, lw, vec, tm, name, rider=None):
    T, D = x1.shape
    W = cw.shape[1]
    assert tm >= CONV_HALO and w_in_t.shape[0] == 4 * W

    def body(x1_ref, g_ref, wi_ref, wo_ref, bda_ref, bdx_ref, cw_ref, lw_ref, vec_ref,
             x2_ref, z_ref, mix_ref, u1_ref, xr_ref, hst_ref, ubuf, rbuf, hc):
        @pl.when(pl.program_id(0) == 0)
        def _():
            ubuf[0:CONV_HALO, :] = jnp.zeros((CONV_HALO, W), F32)
            rbuf[0:LRU_HALO, :] = jnp.zeros((LRU_HALO, W), F32)
            hc[...] = jnp.zeros_like(hc)

        xhat, _ = _rms_stats(x1_ref[...])
        z_ref[...] = _dot((xhat * g_ref[...]).astype(BF16), wi_ref[...], NT)

        ubuf[CONV_HALO:CONV_HALO + tm, :] = z_ref[:, 0:W] * _sigmoid(z_ref[:, W:2 * W])
        u1 = jnp.zeros((tm, W), F32) + vec_ref[V_CB:V_CB + 1, :]
        base = CONV_HALO - (CONV_K - 1)
        for off, win in _row_windows(ubuf, tm, range(base, base + CONV_K)):
            u1 = u1 + cw_ref[off - base:off - base + 1, :] * win
        ubuf[0:CONV_HALO, :] = ubuf[tm:tm + CONV_HALO, :]
        u1_ref[...] = u1
        xh, _ = _layernorm_stats(u1)
        u2 = xh * vec_ref[V_LNG:V_LNG + 1, :] + vec_ref[V_LNB:V_LNB + 1, :]
        ub = (u2 * _sigmoid(u2)).astype(BF16)
        mix_ref[:, 0:W] = ub

        rbuf[LRU_HALO:LRU_HALO + tm, :] = z_ref[:, 2 * W:3 * W]
        xr = jnp.zeros((tm, W), F32) + vec_ref[V_LCB:V_LCB + 1, :]
        for k in range(LRU_K):
            off = LRU_HALO - (LRU_K - 1) + k
            xr = xr + lw_ref[k:k + 1, :] * rbuf[off:off + tm, :]
        rbuf[0:LRU_HALO, :] = rbuf[tm:tm + LRU_HALO, :]
        xr_ref[...] = xr
        _, _, ig, _, a, mult = _lru_gates(xr, bda_ref, bdx_ref, vec_ref)
        hc[0:1, :] = _scan_rows(a, mult * (ig * xr), hc[0:1, :], hst_ref)
        gl, _ = _gelu_parts(z_ref[:, 3 * W:4 * W])
        yb = (hst_ref[...] * gl).astype(BF16)
        mix_ref[:, W:2 * W] = yb

        x2_ref[...] = x1_ref[...] + _dot(ub, wo_ref[0:W, :], NN) + _dot(yb, wo_ref[W:2 * W, :], NN)

    full = lambda a: pl.BlockSpec(a.shape, lambda i: (0,) * a.ndim)
    tile = lambda n: pl.BlockSpec((tm, n), lambda i: (i, 0))
    return _call(
        body, name=name, grid=(T // tm,),
        in_specs=[tile(D), full(g), full(w_in_t), full(w_out), full(bda), full(bdx), full(cw), full(lw), full(vec)],
        out_specs=[tile(D), tile(4 * W), tile(2 * W), tile(W), tile(W), tile(W)],
        out_shape=[jax.ShapeDtypeStruct((T, D), F32), jax.ShapeDtypeStruct((T, 4 * W), F32),
                   jax.ShapeDtypeStruct((T, 2 * W), BF16), jax.ShapeDtypeStruct((T, W), F32),
                   jax.ShapeDtypeStruct((T, W), F32), jax.ShapeDtypeStruct((T, W), F32)],
        scratch_shapes=[pltpu.VMEM((tm + CONV_HALO, W), F32), pltpu.VMEM((tm + LRU_HALO, W), F32),
                        pltpu.VMEM((8, W), F32)],
        vmem_mib=56, args=(x1, g, w_in_t, w_out, bda, bdx, cw, lw, vec), rider=rider)


def _mix_bwd(dx2, z, u1, xr, hst, x1, mix, g, w_in_t, w_out, bda, bdx, cw, lw, vec, tm, name, rider=None):
    T, D = dx2.shape
    W = cw.shape[1]
    nt = T // tm
    assert tm >= CONV_HALO and tm % CONV_HALO == 0

    def body(dx_ref, z_ref, zh_ref, u1_ref, xr_ref, h_ref, hh_ref, wo_ref, bda_ref, bdx_ref, cw_ref, lw_ref, vec_ref,
             x1_ref, mix_ref, g_ref, wi_ref,
             dx1_ref, sg_ref, dbda_ref, dbdx_ref, dob_ref, dg_ref, dwi_ref, dwo_ref,
             u0buf, du1buf, rxbuf, dxrbuf, gbuf, gc, spacc, dz_ref, ai_ref, ao_ref):
        i = pl.program_id(0)
        first = i == nt - 1
        row = lax.broadcasted_iota(jnp.int32, (tm, W), 0)

        @pl.when(i == 0)
        def _():
            sg_ref[...] = jnp.zeros_like(sg_ref)
            dbda_ref[...] = jnp.zeros_like(dbda_ref)
            dbdx_ref[...] = jnp.zeros_like(dbdx_ref)
            du1buf[tm:tm + CONV_HALO, :] = jnp.zeros((CONV_HALO, W), F32)
            dxrbuf[tm:tm + LRU_HALO, :] = jnp.zeros((LRU_HALO, W), F32)
            gc[...] = jnp.zeros_like(gc)
            spacc[...] = jnp.zeros_like(spacc)
            dg_ref[...] = jnp.zeros_like(dg_ref)
            ai_ref[...] = jnp.zeros_like(ai_ref)
            ao_ref[...] = jnp.zeros_like(ao_ref)

        def accum(r, val):
            sg_ref[r:r + 1, :] += jnp.sum(val, axis=0, keepdims=True)

        x1hat, x1rstd = _rms_stats(x1_ref[...])
        gain = g_ref[...]
        hb = (x1hat * gain).astype(BF16)

        def in_proj_bwd(lo, hi):
            dzb = dz_ref[:, lo:hi]
            ai_ref[lo:hi, :] += _dot(dzb, hb, TN)
            return _dot(dzb, wi_ref[lo:hi, :], NN)

        dxb = dx_ref[...].astype(BF16)
        ao_ref[...] += _dot(mix_ref[...], dxb, TN)
        dmix = _dot(dxb, wo_ref[...], NT)
        d_u = dmix[:, 0:W]
        d_yr = dmix[:, W:2 * W]

        xh, rs = _layernorm_stats(u1_ref[...])
        ln_g = vec_ref[V_LNG:V_LNG + 1, :]
        u2 = xh * ln_g + vec_ref[V_LNB:V_LNB + 1, :]
        s2 = _sigmoid(u2)
        d_u2 = d_u * (s2 * (1.0 + u2 * (1.0 - s2)))
        accum(G_LNG, d_u2 * xh)
        accum(G_LNB, d_u2)
        d_xh = d_u2 * ln_g
        d_u1 = rs * (d_xh - jnp.mean(d_xh, axis=-1, keepdims=True)
                     - xh * jnp.mean(d_xh * xh, axis=-1, keepdims=True))
        accum(G_CB, d_u1)
        halo_on = jnp.where(first, 0.0, 1.0)
        u0buf[0:CONV_HALO, :] = halo_on * (zh_ref[:, 0:W] * _sigmoid(zh_ref[:, W:2 * W]))
        cv = z_ref[:, 0:W]
        sgc = _sigmoid(z_ref[:, W:2 * W])
        u0buf[CONV_HALO:CONV_HALO + tm, :] = cv * sgc
        du1buf[0:tm, :] = d_u1
        base = CONV_HALO - (CONV_K - 1)
        for off, win in _row_windows(u0buf, tm, range(base, base + CONV_K)):
            accum(G_CW + off - base, d_u1 * win)
        d_u0 = jnp.zeros((tm, W), F32)
        for off, win in _row_windows(du1buf, tm, range(0, CONV_K)):
            d_u0 = d_u0 + cw_ref[CONV_K - 1 - off:CONV_K - off, :] * win
        du1buf[tm:tm + CONV_HALO, :] = du1buf[0:CONV_HALO, :]
        dz_ref[:, 0:W] = (d_u0 * sgc).astype(BF16)
        dz_ref[:, W:2 * W] = (d_u0 * cv * (sgc * (1.0 - sgc))).astype(BF16)
        dh = in_proj_bwd(0, 2 * W)

        xrv = xr_ref[...]
        xrb, r, ig, sp, a, mult = _lru_gates(xrv, bda_ref, bdx_ref, vec_ref)
        h = h_ref[...]
        gl, dgl = _gelu_parts(z_ref[:, 3 * W:4 * W])
        dz_ref[:, 3 * W:4 * W] = (d_yr * h * dgl).astype(BF16)
        dh = dh + in_proj_bwd(3 * W, 4 * W)
        a_next = jnp.where(row == tm - 1, 1.0, pltpu.roll(a, tm - 1, 0))
        g_first = _scan_rows(a_next, d_yr * gl, gc[0:1, :], gbuf, reverse=True)
        g = gbuf[...]
        gc[0:1, :] = a[0:1, :] * g_first
        hprev = jnp.where(row == 0, halo_on * hh_ref[LRU_HALO - 1:LRU_HALO, :], pltpu.roll(h, 1, 0))
        d_log_a = (g * hprev) * a - (g * ig * xrv) * (a * a) / mult
        d_ig = g * mult * xrv
        d_xr = g * mult * ig
        spacc[0:1, :] += jnp.sum(d_log_a * r, axis=0, keepdims=True)
        d_pa32 = (d_log_a * (-LRU_C * sp)) * (r * (1.0 - r))
        d_px32 = d_ig * (ig * (1.0 - ig))
        accum(G_BA, d_pa32)
        accum(G_BX, d_px32)
        d_pa = d_pa32.astype(BF16)
        d_px = d_px32.astype(BF16)
        d_xr = d_xr + _dot(d_pa, bda_ref[...], NT) + _dot(d_px, bdx_ref[...], NT)
        dbda_ref[...] += _dot(xrb, d_pa, TN)
        dbdx_ref[...] += _dot(xrb, d_px, TN)
        accum(G_LCB, d_xr)
        rxbuf[0:LRU_HALO, :] = halo_on * zh_ref[CONV_HALO - LRU_HALO:CONV_HALO, 2 * W:3 * W]
        rxbuf[LRU_HALO:LRU_HALO + tm, :] = z_ref[:, 2 * W:3 * W]
        dxrbuf[0:tm, :] = d_xr
        d_rx = jnp.zeros((tm, W), F32)
        for k in range(LRU_K):
            off = LRU_HALO - (LRU_K - 1) + k
            accum(G_LW + k, d_xr * rxbuf[off:off + tm, :])
            d_rx = d_rx + lw_ref[k:k + 1, :] * dxrbuf[LRU_K - 1 - k:LRU_K - 1 - k + tm, :]
        dxrbuf[tm:tm + LRU_HALO, :] = dxrbuf[0:LRU_HALO, :]
        dz_ref[:, 2 * W:3 * W] = d_rx.astype(BF16)
        dh = dh + in_proj_bwd(2 * W, 3 * W)

        dx, dg = _rms_bwd(x1hat, x1rstd, gain, dh)
        dx1 = dx_ref[...] + dx
        dx1_ref[...] = dx1
        dob_ref[...] = (FFN_RES * dx1).astype(BF16)
        dg_ref[...] += dg

        @pl.when(first)
        def _():
            lam = vec_ref[V_LAM:V_LAM + 1, :]
            sg_ref[G_LAM:G_LAM + 1, :] = LRU_C * _sigmoid(-lam) * spacc[0:1, :]
            dwi_ref[...] = ai_ref[...].astype(BF16)
            dwo_ref[...] = ao_ref[...].astype(BF16)

    full = lambda a: pl.BlockSpec(a.shape, lambda i: (0,) * a.ndim)
    tile = lambda n: pl.BlockSpec((tm, n), lambda i: (nt - 1 - i, 0))
    halo = lambda rows, n: pl.BlockSpec(
        (rows, n), lambda i: (jnp.maximum((nt - 1 - i) * (tm // rows) - 1, 0), 0))
    const = lambda r, c: pl.BlockSpec((r, c), lambda i: (0, 0))
    return _call(
        body, name=name, grid=(nt,),
        in_specs=[tile(D), tile(4 * W), halo(CONV_HALO, 4 * W), tile(W), tile(W), tile(W), halo(LRU_HALO, W),
                  full(w_out), full(bda), full(bdx), full(cw), full(lw), full(vec),
                  tile(D), tile(2 * W), full(g), full(w_in_t)],
        out_specs=[tile(D), const(G_ROWS, W), const(W, W), const(W, W),
                   tile(D), const(1, D), const(4 * W, D), const(2 * W, D)],
        out_shape=[jax.ShapeDtypeStruct((T, D), F32), jax.ShapeDtypeStruct((G_ROWS, W), F32),
                   jax.ShapeDtypeStruct((W, W), F32), jax.ShapeDtypeStruct((W, W), F32),
                   jax.ShapeDtypeStruct((T, D), BF16), jax.ShapeDtypeStruct((1, D), F32),
                   jax.ShapeDtypeStruct((4 * W, D), BF16), jax.ShapeDtypeStruct((2 * W, D), BF16)],
        scratch_shapes=[pltpu.VMEM((tm + CONV_HALO, W), F32), pltpu.VMEM((tm + CONV_HALO, W), F32),
                        pltpu.VMEM((tm + LRU_HALO, W), F32), pltpu.VMEM((tm + LRU_HALO, W), F32),
                        pltpu.VMEM((tm, W), F32), pltpu.VMEM((8, W), F32), pltpu.VMEM((8, W), F32),
                        pltpu.VMEM((tm, 4 * W), BF16), pltpu.VMEM((4 * W, D), F32), pltpu.VMEM((2 * W, D), F32)],
        vmem_mib=60, args=(dx2, z, z, u1, xr, hst, hst, w_out, bda, bdx, cw, lw, vec, x1, mix, g, w_in_t),
        rider=rider)


def _pair_add(full, recv, name):
    K, _, _, rows, D = full.shape

    def body(c_ref, a_ref, b_ref, o_ref):
        o_ref[...] = (a_ref[...].astype(F32) + b_ref[...].astype(F32)).astype(BF16)

    c = lax.axis_index("c").astype(jnp.int32).reshape((1,))
    return _call(
        body, name=name, grid=(K, N_CHIP), num_scalar_prefetch=1,
        in_specs=[pl.BlockSpec((None, None, None, rows, D), lambda k, q, c_ref: (k, q, c_ref[0], 0, 0)),
                  pl.BlockSpec((None, None, rows, D), lambda k, q, c_ref: (k, q, 0, 0))],
        out_specs=pl.BlockSpec((None, None, rows, D), lambda k, q, c_ref: (k, q, 0, 0)),
        out_shape=jax.ShapeDtypeStruct(recv.shape, BF16),
        scratch_shapes=[], vmem_mib=16, args=(c, full, recv))


def _adamw_update(wv, gv, mv, vv):
    m2 = ADAM_B1 * mv + (1.0 - ADAM_B1) * gv
    v2 = ADAM_B2 * vv + (1.0 - ADAM_B2) * (gv * gv)
    m_hat = m2 / (1.0 - ADAM_B1 ** ADAM_STEP)
    v_hat = v2 / (1.0 - ADAM_B2 ** ADAM_STEP)
    return -ADAM_LR * (m_hat / (jnp.sqrt(v_hat) + ADAM_EPS) + ADAM_WD * wv), m2, v2


def _finish(parts, k, w, m, v, transpose, name):
    _, _, rows, D = parts.shape

    def body(p_ref, w_ref, m_ref, v_ref, g_ref, d_ref, mo_ref, vo_ref):
        acc = p_ref[0].astype(F32)
        for q in range(1, N_CHIP):
            acc = acc + p_ref[q].astype(F32)
        gv = acc.T if transpose else acc
        g_ref[...] = gv
        d_ref[...], mo_ref[...], vo_ref[...] = _adamw_update(w_ref[...], gv, m_ref[...], v_ref[...])

    whole = pl.BlockSpec(w.shape, lambda i: (0, 0))
    return _call(
        body, name=name, grid=(1,),
        in_specs=[pl.BlockSpec((None, N_CHIP, rows, D), lambda i: (k, 0, 0, 0)), whole, whole, whole],
        out_specs=[whole] * 4, out_shape=[jax.ShapeDtypeStruct(w.shape, F32)] * 4,
        scratch_shapes=[], vmem_mib=40, args=(parts, w, m, v))


def _adamw(w, g, m, v, name):
    def body(w_ref, g_ref, m_ref, v_ref, d_ref, mo_ref, vo_ref):
        d_ref[...], mo_ref[...], vo_ref[...] = _adamw_update(w_ref[...], g_ref[...], m_ref[...], v_ref[...])

    shape = jax.ShapeDtypeStruct(w.shape, F32)
    return pl.pallas_call(
        body, name=name,
        in_specs=[VMEM_SPEC] * 4, out_specs=[VMEM_SPEC] * 3, out_shape=[shape] * 3,
        compiler_params=pltpu.CompilerParams(vmem_limit_bytes=32 * MIB),
    )(w, g, m, v)


def _block_diag(w):
    h, d, _ = w.shape
    out = jnp.zeros((h * d, h * d), w.dtype)
    for k in range(h):
        out = lax.dynamic_update_slice(out, w[k], (k * d, k * d))
    return out


def _diag_blocks(m, h):
    d = m.shape[0] // h
    return jnp.stack([m[k * d:(k + 1) * d, k * d:(k + 1) * d] for k in range(h)])


def _reduce_level1(full, tag):
    got = _run_comm(_sibling_comm(full), "rs_sibling_" + tag)
    return [_pair_add(a, b, "rs_pair_add_%s%d" % (tag, n)) for n, (a, b) in enumerate(zip(full, got))]


def kernel(x, ffn1_norm, ffn1_w_gate, ffn1_w_up, ffn1_w_down, mix_norm, w_in, conv_dw, conv_dw_bias, conv_ln_g, conv_ln_b, lru_conv_w, lru_conv_b, lru_w_a, lru_b_a, lru_w_x, lru_b_x, lru_lambda, w_out, ffn2_norm, ffn2_w_gate, ffn2_w_up, ffn2_w_down, final_norm, loss_target, m_ffn1_norm, m_ffn1_w_gate, m_ffn1_w_up, m_ffn1_w_down, m_mix_norm, m_w_in, m_conv_dw, m_conv_dw_bias, m_conv_ln_g, m_conv_ln_b, m_lru_conv_w, m_lru_conv_b, m_lru_w_a, m_lru_b_a, m_lru_w_x, m_lru_b_x, m_lru_lambda, m_w_out, m_ffn2_norm, m_ffn2_w_gate, m_ffn2_w_up, m_ffn2_w_down, m_final_norm, v_ffn1_norm, v_ffn1_w_gate, v_ffn1_w_up, v_ffn1_w_down, v_mix_norm, v_w_in, v_conv_dw, v_conv_dw_bias, v_conv_ln_g, v_conv_ln_b, v_lru_conv_w, v_lru_conv_b, v_lru_w_a, v_lru_b_a, v_lru_w_x, v_lru_b_x, v_lru_lambda, v_w_out, v_ffn2_norm, v_ffn2_w_gate, v_ffn2_w_up, v_ffn2_w_down, v_final_norm):
    T, D = x.shape[1], x.shape[2]
    F = ffn1_w_down.shape[0] * N_DEV
    rf = ffn1_w_down.shape[0]
    ri = w_in.shape[1]
    ro = w_out.shape[0]
    W = conv_dw_bias.shape[0]
    wc = conv_dw.shape[1]
    H = lru_w_a.shape[0]
    xs = x.reshape(T, D)
    tgt = loss_target.reshape(T, D)
    tm_ffn = min(256, T)
    cf = 256
    tm_w = min(1024, T)
    tm_mix = min(256, T)
    tf_w = F // 2
    row = lambda v: v.reshape(1, -1)
    by_owner = lambda a, rows: a.reshape(a.shape[0], N_CHIP, 2, rows, D)

    p3a, p3b, p_in, p_out = _prep_weights(
        (ffn1_w_gate, ffn1_w_up, ffn1_w_down), (ffn2_w_gate, ffn2_w_up, ffn2_w_down), w_in, w_out, "prep_weights")
    tile_rows = lambda a: jnp.pad(a, ((0, -a.shape[0] % SUBLANES), (0, 0)))
    p_cw = jnp.concatenate([tile_rows(conv_dw), tile_rows(lru_conv_w)], axis=0)
    lw_row = p_cw.shape[0] - SUBLANES
    stacked = lambda r, j: r.at[:, j]
    plain = lambda r, j: r.at[j]
    g3_shape = jax.ShapeDtypeStruct((3, N_DEV, rf, D), BF16)
    (g3a,) = _all_gather([p3a], [stacked], [g3_shape], "ag_ffn1")
    w3a = g3a.reshape(3, F, D)
    bda = _block_diag(lru_w_a).astype(BF16)
    bdx = _block_diag(lru_w_x).astype(BF16)
    vec = jnp.concatenate([tile_rows(v[None]) for v in
                           (conv_dw_bias, conv_ln_g, conv_ln_b, lru_conv_b, lru_b_a, lru_b_x, lru_lambda)], axis=0)

    gather_rest = _gather_comm(
        [p3b, p_in, p_out, p_cw], [stacked, plain, plain, plain],
        [g3_shape, jax.ShapeDtypeStruct((N_DEV, ri, D), BF16), jax.ShapeDtypeStruct((N_DEV, ro, D), BF16),
         jax.ShapeDtypeStruct((N_DEV,) + p_cw.shape, F32)],
        [(SIBLING,) + SAME_CORE, EVERYONE, EVERYONE, EVERYONE])
    (x1, h1, dau1, dag1, act1), (g3b_half, g_in, g_out, g_cw) = _ffn_fwd(
        xs, row(ffn1_norm), w3a, tm_ffn, cf, "ffn1_fwd", rider=gather_rest)
    w_in_t = g_in.reshape(N_DEV * ri, D)
    w_out_f = g_out.reshape(N_DEV * ro, D)
    cw_all = jnp.transpose(g_cw, (1, 0, 2)).reshape(p_cw.shape[0], N_DEV * wc)
    cw = cw_all[0:CONV_K]
    lw = cw_all[lw_row:lw_row + LRU_K]
    (x2, z, mix, u1, xr, hst), (g3b,) = _mix_core_fwd(
        x1, row(mix_norm), w_in_t, w_out_f, bda, bdx, cw, lw, vec, tm_mix, "mix_core_fwd",
        rider=_forward_comm([g3b_half], [stacked]))
    w3b = g3b.reshape(3, F, D)
    dx3, dob2, d_final_norm, loss_part, h3, dau2, dag2, act2 = _ffn_fwd(
        x2, row(ffn2_norm), w3b, tm_ffn, cf, "ffn2_fwd_loss", head=(row(final_norm), tgt))

    dx2, dgate2, dup2, d_ffn2_norm = _ffn_dgrad(dx3, x2, row(ffn2_norm), dau2, dag2, w3b, tm_ffn, cf, "ffn2_dgrad")
    (dw_gu2,) = _wgrad([dgate2, dup2], h3, tm_w, tf_w, "ffn2_wgrad_gu")
    (dw_d2,) = _wgrad([act2], dob2, tm_w, tf_w, "ffn2_wgrad_d")
    sums_f2 = _reduce_level1([by_owner(dw_gu2, rf), by_owner(dw_d2, rf)], "f2")
    (dx1, sg, dbda, dbdx, dob1, d_mix_norm, dw_in_t, dw_out), parts_f2 = _mix_bwd(
        dx2, z, u1, xr, hst, x1, mix, row(mix_norm), w_in_t, w_out_f, bda, bdx, cw, lw, vec, tm_mix, "mix_bwd",
        rider=_chips_comm(sums_f2))
    sums_io = _reduce_level1([by_owner(dw_in_t[None], ri), by_owner(dw_out[None], ro)], "io")
    (dw_d1,), parts_io = _wgrad([act1], dob1, tm_w, tf_w, "ffn1_wgrad_d", rider=_chips_comm(sums_io))
    sums_d1 = _reduce_level1([by_owner(dw_d1, rf)], "d1")
    dx0, dgate1, dup1, d_ffn1_norm = _ffn_dgrad(dx1, xs, row(ffn1_norm), dau1, dag1, w3a, tm_ffn, cf, "ffn1_dgrad")

    d_lru_w_a = _diag_blocks(dbda, H)
    d_lru_w_x = _diag_blocks(dbdx, H)
    small = [d_ffn1_norm, d_mix_norm, d_ffn2_norm, d_final_norm,
             sg[G_CB], sg[G_LNG], sg[G_LNB], sg[G_LCB], sg[G_BA], sg[G_BX], sg[G_LAM],
             d_lru_w_a, d_lru_w_x,
             sg[G_CW:G_CW + CONV_K], sg[G_LW:G_LW + LRU_K],
             loss_part]
    sizes = [a.size for a in small]
    flat = jnp.concatenate([a.reshape(-1) for a in small])
    n_rows = -(-flat.size // (8 * D)) * 8
    packed = jnp.concatenate([flat, jnp.zeros((n_rows * D - flat.size,), F32)]).reshape(n_rows, D)
    (dw_g1,), (red, *parts_d1) = _wgrad(
        [dgate1], h1, tm_w, tf_w, "ffn1_wgrad_g", rider=_both(_small_sum_comm(packed), _chips_comm(sums_d1)))
    sums_g1 = _reduce_level1([by_owner(dw_g1, rf)], "g1")
    (dw_u1,), parts_g1 = _wgrad([dup1], h1, tm_w, tf_w, "ffn1_wgrad_u", rider=_chips_comm(sums_g1))
    sums_u1 = _reduce_level1([by_owner(dw_u1, rf)], "u1")
    parts_u1 = _run_comm(_chips_comm(sums_u1), "rs_chips_u1")

    red = red.reshape(-1)
    offs = [0]
    for n in sizes:
        offs.append(offs[-1] + n)
    piece = lambda k, shape: red[offs[k]:offs[k + 1]].reshape(shape)
    loss = red[offs[15]]

    me = 4 * lax.axis_index("x") + 2 * lax.axis_index("y") + lax.axis_index("c")
    chan = lambda full_g: lax.dynamic_slice_in_dim(full_g, me * wc, wc, axis=1)
    grads = {
        "ffn1_norm": piece(0, (D,)), "mix_norm": piece(1, (D,)), "ffn2_norm": piece(2, (D,)), "final_norm": piece(3, (D,)),
        "conv_dw_bias": piece(4, (W,)), "conv_ln_g": piece(5, (W,)), "conv_ln_b": piece(6, (W,)),
        "lru_conv_b": piece(7, (W,)), "lru_b_a": piece(8, (W,)), "lru_b_x": piece(9, (W,)), "lru_lambda": piece(10, (W,)),
        "lru_w_a": piece(11, lru_w_a.shape), "lru_w_x": piece(12, lru_w_x.shape),
        "conv_dw": chan(piece(13, (CONV_K, W))), "lru_conv_w": chan(piece(14, (LRU_K, W))),
    }

    weights = dict(ffn1_norm=ffn1_norm, ffn1_w_gate=ffn1_w_gate, ffn1_w_up=ffn1_w_up, ffn1_w_down=ffn1_w_down, mix_norm=mix_norm, w_in=w_in, conv_dw=conv_dw, conv_dw_bias=conv_dw_bias, conv_ln_g=conv_ln_g, conv_ln_b=conv_ln_b, lru_conv_w=lru_conv_w, lru_conv_b=lru_conv_b, lru_w_a=lru_w_a, lru_b_a=lru_b_a, lru_w_x=lru_w_x, lru_b_x=lru_b_x, lru_lambda=lru_lambda, w_out=w_out, ffn2_norm=ffn2_norm, ffn2_w_gate=ffn2_w_gate, ffn2_w_up=ffn2_w_up, ffn2_w_down=ffn2_w_down, final_norm=final_norm)
    moment1 = dict(ffn1_norm=m_ffn1_norm, ffn1_w_gate=m_ffn1_w_gate, ffn1_w_up=m_ffn1_w_up, ffn1_w_down=m_ffn1_w_down, mix_norm=m_mix_norm, w_in=m_w_in, conv_dw=m_conv_dw, conv_dw_bias=m_conv_dw_bias, conv_ln_g=m_conv_ln_g, conv_ln_b=m_conv_ln_b, lru_conv_w=m_lru_conv_w, lru_conv_b=m_lru_conv_b, lru_w_a=m_lru_w_a, lru_b_a=m_lru_b_a, lru_w_x=m_lru_w_x, lru_b_x=m_lru_b_x, lru_lambda=m_lru_lambda, w_out=m_w_out, ffn2_norm=m_ffn2_norm, ffn2_w_gate=m_ffn2_w_gate, ffn2_w_up=m_ffn2_w_up, ffn2_w_down=m_ffn2_w_down, final_norm=m_final_norm)
    moment2 = dict(ffn1_norm=v_ffn1_norm, ffn1_w_gate=v_ffn1_w_gate, ffn1_w_up=v_ffn1_w_up, ffn1_w_down=v_ffn1_w_down, mix_norm=v_mix_norm, w_in=v_w_in, conv_dw=v_conv_dw, conv_dw_bias=v_conv_dw_bias, conv_ln_g=v_conv_ln_g, conv_ln_b=v_conv_ln_b, lru_conv_w=v_lru_conv_w, lru_conv_b=v_lru_conv_b, lru_w_a=v_lru_w_a, lru_b_a=v_lru_b_a, lru_w_x=v_lru_w_x, lru_b_x=v_lru_b_x, lru_lambda=v_lru_lambda, w_out=v_w_out, ffn2_norm=v_ffn2_norm, ffn2_w_gate=v_ffn2_w_gate, ffn2_w_up=v_ffn2_w_up, ffn2_w_down=v_ffn2_w_down, final_norm=v_final_norm)
    order = list(weights)
    big = {"ffn1_w_gate": (parts_g1[0], 0, True), "ffn1_w_up": (parts_u1[0], 0, True),
           "ffn1_w_down": (parts_d1[0], 0, False), "w_in": (parts_io[0], 0, True), "w_out": (parts_io[1], 0, False),
           "ffn2_w_gate": (parts_f2[0], 0, True), "ffn2_w_up": (parts_f2[0], 1, True),
           "ffn2_w_down": (parts_f2[1], 0, False)}
    delta, new_m, new_v = {}, {}, {}
    for n, (parts, k, transposed) in big.items():
        grads[n], delta[n], new_m[n], new_v[n] = _finish(
            parts, k, weights[n], moment1[n], moment2[n], transposed, "finish_" + n)
    rest = [n for n in order if n not in big]
    rest_sizes = [weights[n].size for n in rest]
    n_flat = sum(rest_sizes)
    pad_rows = -(-n_flat // (8 * 128)) * 8

    def pack(d):
        f = jnp.concatenate([d[n].reshape(-1) for n in rest])
        return jnp.concatenate([f, jnp.zeros((pad_rows * 128 - n_flat,), F32)]).reshape(pad_rows, 128)

    d_s, m_s, v_s = _adamw(pack(weights), pack(grads), pack(moment1), pack(moment2), "adamw_small")
    off = 0
    for n, size in zip(rest, rest_sizes):
        shape = weights[n].shape
        delta[n] = d_s.reshape(-1)[off:off + size].reshape(shape)
        new_m[n] = m_s.reshape(-1)[off:off + size].reshape(shape)
        new_v[n] = v_s.reshape(-1)[off:off + size].reshape(shape)
        off += size

    return (loss, dx0.reshape(x.shape), *[grads[n] for n in order], *[delta[n] for n in order],
            *[new_m[n] for n in order], *[new_v[n] for n in order])
```

```python
import functools
import math

import jax
import jax.numpy as jnp
from jax import lax
from jax.experimental import pallas as pl
from jax.experimental.pallas import tpu as pltpu

F32 = jnp.float32
BF16 = jnp.bfloat16
MESH = pl.DeviceIdType.MESH

N_DEV = 8
N_CHIP = 4
SUBLANES = 8
RMS_EPS = 1e-6
LN_EPS = 1e-5
LRU_C = 8.0
CONV_K = 31
LRU_K = 4
CONV_HALO = 32
LRU_HALO = 8
FFN_RES = 0.5
ADAM_LR, ADAM_B1, ADAM_B2, ADAM_EPS, ADAM_WD, ADAM_STEP = 0.001, 0.9, 0.999, 1e-08, 0.01, 10
GELU_K = math.sqrt(2.0 / math.pi)
GELU_C = 0.044715

MIB = 1024 * 1024
NT = (((1,), (1,)), ((), ()))
NN = (((1,), (0,)), ((), ()))
TN = (((0,), (0,)), ((), ()))

V_CB, V_LNG, V_LNB, V_LCB, V_BA, V_BX, V_LAM = range(0, 7 * SUBLANES, SUBLANES)
G_CW = 0
G_CB, G_LNG, G_LNB = 31, 32, 33
G_LW = 34
G_LCB, G_BA, G_BX, G_LAM = 38, 39, 40, 41
G_ROWS = 48

HBM_SPEC = pl.BlockSpec(memory_space=pltpu.HBM)
VMEM_SPEC = pl.BlockSpec(memory_space=pltpu.VMEM)


def _dot(a, b, dims):
    return lax.dot_general(a, b, dims, preferred_element_type=F32)


def _sigmoid(x):
    return 1.0 / (1.0 + jnp.exp(-x))


def _gelu_parts(x):
    x2 = x * x
    th = jnp.tanh(GELU_K * x * (1.0 + GELU_C * x2))
    gl = 0.5 * x * (1.0 + th)
    dgl = 0.5 * (1.0 + th) + 0.5 * x * (1.0 - th * th) * GELU_K * (1.0 + 3.0 * GELU_C * x2)
    return gl, dgl


def _neg_expm1(y):
    series = -y * (1.0 + y * (1.0 / 2) * (1.0 + y * (1.0 / 3) * (1.0 + y * (1.0 / 4) * (1.0 + y * (1.0 / 5) * (1.0 + y * (1.0 / 6))))))
    return jnp.where(y > -0.25, series, 1.0 - jnp.exp(y))


def _softplus_neg(lam):
    t = -lam
    e = jnp.exp(-jnp.abs(t))
    s = 1.0 + e
    log1p_e = jnp.log(s) - ((s - 1.0) - e) / s
    return jnp.maximum(t, 0.0) + log1p_e


def _rms_stats(xv):
    rstd = lax.rsqrt(jnp.mean(xv * xv, axis=-1, keepdims=True) + RMS_EPS)
    return xv * rstd, rstd


def _rms_bwd(xhat, rstd, g, dh):
    dxhat = dh * g
    dx = rstd * (dxhat - xhat * jnp.mean(dxhat * xhat, axis=-1, keepdims=True))
    return dx, jnp.sum(dh * xhat, axis=0, keepdims=True)


def _row_windows(buf_ref, n_rows, offsets):
    total = buf_ref.shape[0]
    full = buf_ref[...]
    for b in range(SUBLANES):
        offs = [o for o in offsets if o % SUBLANES == b]
        if not offs:
            continue
        assert max(offs) + n_rows <= total
        moved = full if b == 0 else pltpu.roll(full, total - b, 0)
        for o in offs:
            yield o, moved[o - b:o - b + n_rows, :]


def _scan_rows(av, bv, edge, out_ref, reverse=False):
    tm, W = av.shape
    sub = lax.broadcasted_iota(jnp.int32, (tm, W), 0) % SUBLANES
    s = 1
    while s < SUBLANES:
        keep = (sub < SUBLANES - s) if reverse else (sub >= s)
        shift = tm - s if reverse else s
        bv = jnp.where(keep, av * pltpu.roll(bv, shift, 0) + bv, bv)
        av = jnp.where(keep, av * pltpu.roll(av, shift, 0), av)
        s *= 2
    starts = range(0, tm, SUBLANES)
    for r0 in (reversed(starts) if reverse else starts):
        group = av[r0:r0 + SUBLANES, :] * edge + bv[r0:r0 + SUBLANES, :]
        out_ref[r0:r0 + SUBLANES, :] = group
        edge = group[0:1, :] if reverse else group[SUBLANES - 1:SUBLANES, :]
    return edge


class _Comm:
    def __init__(self, arrays, in_specs, out_shapes, out_specs, scratch, start, wait, aliases=None):
        self.arrays, self.in_specs = list(arrays), list(in_specs)
        self.out_shapes, self.out_specs = list(out_shapes), list(out_specs)
        self.scratch, self.start, self.wait = list(scratch), start, wait
        self.aliases = dict(aliases or {})


def _in_hbm(a):
    return pltpu.with_memory_space_constraint(a, pltpu.HBM)


def _operands(comm):
    return [a if spec is VMEM_SPEC else _in_hbm(a) for a, spec in zip(comm.arrays, comm.in_specs)]


def _call(body, *, name, grid, in_specs, out_specs, out_shape, scratch_shapes, vmem_mib, args, rider=None,
          num_scalar_prefetch=0):
    params = pltpu.CompilerParams(dimension_semantics=("arbitrary",) * len(grid), vmem_limit_bytes=vmem_mib * MIB)
    args = [a if k < num_scalar_prefetch else _in_hbm(a) for k, a in enumerate(args)]
    if rider is None:
        return pl.pallas_call(
            body, name=name,
            grid_spec=pltpu.PrefetchScalarGridSpec(
                num_scalar_prefetch=num_scalar_prefetch, grid=grid, in_specs=in_specs, out_specs=out_specs,
                scratch_shapes=scratch_shapes),
            out_shape=out_shape, compiler_params=params)(*args)
    assert num_scalar_prefetch == 0
    n_in, n_out, n_scr = len(in_specs), len(out_specs), len(scratch_shapes)
    r_in, r_out = len(rider.arrays), len(rider.out_shapes)
    n_axes = len(grid)

    def carried(*refs):
        pos = [0]

        def take(n):
            pos[0] += n
            return refs[pos[0] - n:pos[0]]

        ins, r_ins, outs, r_outs, scr, r_scr = take(n_in), take(r_in), take(n_out), take(r_out), take(n_scr), take(len(rider.scratch))
        first = pl.program_id(0) == 0
        last = pl.program_id(0) == grid[0] - 1
        for ax in range(1, n_axes):
            first = first & (pl.program_id(ax) == 0)
            last = last & (pl.program_id(ax) == grid[ax] - 1)

        @pl.when(first)
        def _():
            rider.start(r_ins, r_outs, r_scr)

        body(*ins, *outs, *scr)

        @pl.when(last)
        def _():
            rider.wait(r_ins, r_outs, r_scr)

    res = pl.pallas_call(
        carried, name=name,
        grid=grid,
        in_specs=list(in_specs) + rider.in_specs,
        out_specs=list(out_specs) + rider.out_specs,
        out_shape=list(out_shape) + rider.out_shapes,
        scratch_shapes=list(scratch_shapes) + rider.scratch,
        input_output_aliases={n_in + i: n_out + o for i, o in rider.aliases.items()},
        compiler_params=params)(*args, *_operands(rider))
    return res[:n_out], res[n_out:]


def _run_comm(comm, name):
    n_in, n_out = len(comm.arrays), len(comm.out_shapes)

    def body(*refs):
        ins, outs, scr = refs[:n_in], refs[n_in:n_in + n_out], refs[n_in + n_out:]
        comm.start(ins, outs, scr)
        comm.wait(ins, outs, scr)

    return pl.pallas_call(
        body, name=name,
        in_specs=comm.in_specs, out_specs=comm.out_specs, out_shape=comm.out_shapes,
        scratch_shapes=comm.scratch, input_output_aliases=comm.aliases,
        compiler_params=pltpu.CompilerParams(vmem_limit_bytes=24 * MIB))(*_operands(comm))


def _both(a, b):
    ni, no, ns = len(a.arrays), len(a.out_shapes), len(a.scratch)

    def start(ins, outs, scr):
        a.start(ins[:ni], outs[:no], scr[:ns])
        b.start(ins[ni:], outs[no:], scr[ns:])

    def wait(ins, outs, scr):
        a.wait(ins[:ni], outs[:no], scr[:ns])
        b.wait(ins[ni:], outs[no:], scr[ns:])

    aliases = dict(a.aliases)
    aliases.update({ni + i: no + o for i, o in b.aliases.items()})
    return _Comm(a.arrays + b.arrays, a.in_specs + b.in_specs, a.out_shapes + b.out_shapes,
                 a.out_specs + b.out_specs, a.scratch + b.scratch, start, wait, aliases)


def _place():
    return lax.axis_index("x"), lax.axis_index("y"), lax.axis_index("c")


def _peer(k):
    x, y, c = _place()
    px, py, pc = x ^ ((k >> 2) & 1), y ^ ((k >> 1) & 1), c ^ (k & 1)
    return (px, py, pc), 4 * px + 2 * py + pc


SIBLING = 1
SAME_CORE = (2, 4, 6)
EVERYONE = tuple(range(1, N_DEV))


def _gather_comm(shards, views, out_shapes, relations):
    na = len(shards)

    def copies(ins, outs, scr):
        send_sems, recv_sems, _ = scr
        _, me = _peer(0)
        out = []
        for a in range(na):
            for k in relations[a]:
                peer, theirs = _peer(k)
                send = functools.partial(
                    pltpu.make_async_remote_copy,
                    src_ref=ins[a], dst_ref=views[a](outs[a], me),
                    send_sem=send_sems.at[7 * a + k - 1], recv_sem=recv_sems.at[7 * a + k - 1],
                    device_id=peer, device_id_type=MESH)
                recv = functools.partial(
                    pltpu.make_async_remote_copy,
                    src_ref=ins[a], dst_ref=views[a](outs[a], theirs),
                    send_sem=send_sems.at[7 * a + k - 1], recv_sem=recv_sems.at[7 * a + k - 1],
                    device_id=peer, device_id_type=MESH)
                out.append((send, recv))
        return out

    def local(ins, outs, scr):
        _, me = _peer(0)
        return [pltpu.make_async_copy(ins[a], views[a](outs[a], me), scr[2].at[a]) for a in range(na)]

    def start(ins, outs, scr):
        for cp in local(ins, outs, scr):
            cp.start()
        for send, _ in copies(ins, outs, scr):
            send().start()

    def wait(ins, outs, scr):
        for _, recv in copies(ins, outs, scr):
            recv().wait_recv()
        for send, _ in copies(ins, outs, scr):
            send().wait_send()
        for cp in local(ins, outs, scr):
            cp.wait()

    return _Comm(shards, [HBM_SPEC] * na, out_shapes, [HBM_SPEC] * na,
                 [pltpu.SemaphoreType.DMA((7 * na,)), pltpu.SemaphoreType.DMA((7 * na,)),
                  pltpu.SemaphoreType.DMA((na,))], start, wait)


def _forward_comm(gathered, views):
    na = len(gathered)
    shapes = [jax.ShapeDtypeStruct(g.shape, g.dtype) for g in gathered]

    def copies(outs, scr):
        send_sems, recv_sems = scr
        sibling, _ = _peer(SIBLING)
        out = []
        for a in range(na):
            for n, k in enumerate(SAME_CORE):
                _, mine = _peer(k)
                _, theirs = _peer(k ^ SIBLING)
                send = functools.partial(
                    pltpu.make_async_remote_copy,
                    src_ref=views[a](outs[a], mine), dst_ref=views[a](outs[a], mine),
                    send_sem=send_sems.at[3 * a + n], recv_sem=recv_sems.at[3 * a + n],
                    device_id=sibling, device_id_type=MESH)
                recv = functools.partial(
                    pltpu.make_async_remote_copy,
                    src_ref=views[a](outs[a], mine), dst_ref=views[a](outs[a], theirs),
                    send_sem=send_sems.at[3 * a + n], recv_sem=recv_sems.at[3 * a + n],
                    device_id=sibling, device_id_type=MESH)
                out.append((send, recv))
        return out

    def start(ins, outs, scr):
        for send, _ in copies(outs, scr):
            send().start()

    def wait(ins, outs, scr):
        for _, recv in copies(outs, scr):
            recv().wait_recv()
        for send, _ in copies(outs, scr):
            send().wait_send()

    return _Comm(gathered, [HBM_SPEC] * na, shapes, [HBM_SPEC] * na,
                 [pltpu.SemaphoreType.DMA((3 * na,)), pltpu.SemaphoreType.DMA((3 * na,))], start, wait,
                 aliases={a: a for a in range(na)})


def _all_gather(shards, views, out_shapes, name):
    na = len(shards)
    level1 = _gather_comm(shards, views, out_shapes, [(SIBLING,) + SAME_CORE] * na)

    def body(*refs):
        ins, outs = refs[:na], refs[na:2 * na]
        send_sems, recv_sems, local_sems, fwd_send, fwd_recv = refs[2 * na:]
        sibling, _ = _peer(SIBLING)
        scr = (send_sems, recv_sems, local_sems)
        level1.start(ins, outs, scr)
        passed, landing = [], []
        for a in range(na):
            for n, k in enumerate(SAME_CORE):
                peer, mine = _peer(k)
                _, theirs = _peer(k ^ SIBLING)
                pltpu.make_async_remote_copy(
                    src_ref=ins[a], dst_ref=views[a](outs[a], mine),
                    send_sem=send_sems.at[7 * a + k - 1], recv_sem=recv_sems.at[7 * a + k - 1],
                    device_id=peer, device_id_type=MESH).wait_recv()
                fwd = pltpu.make_async_remote_copy(
                    src_ref=views[a](outs[a], mine), dst_ref=views[a](outs[a], mine),
                    send_sem=fwd_send.at[3 * a + n], recv_sem=fwd_recv.at[3 * a + n],
                    device_id=sibling, device_id_type=MESH)
                fwd.start()
                passed.append(fwd)
                landing.append(pltpu.make_async_remote_copy(
                    src_ref=views[a](outs[a], mine), dst_ref=views[a](outs[a], theirs),
                    send_sem=fwd_send.at[3 * a + n], recv_sem=fwd_recv.at[3 * a + n],
                    device_id=sibling, device_id_type=MESH))
        for a in range(na):
            _, theirs = _peer(SIBLING)
            pltpu.make_async_remote_copy(
                src_ref=ins[a], dst_ref=views[a](outs[a], theirs),
                send_sem=send_sems.at[7 * a + SIBLING - 1], recv_sem=recv_sems.at[7 * a + SIBLING - 1],
                device_id=sibling, device_id_type=MESH).wait_recv()
        for cp in landing:
            cp.wait_recv()
        for cp in passed:
            cp.wait_send()
        _, me = _peer(0)
        for a in range(na):
            for k in (SIBLING,) + SAME_CORE:
                peer, _ = _peer(k)
                pltpu.make_async_remote_copy(
                    src_ref=ins[a], dst_ref=views[a](outs[a], me),
                    send_sem=send_sems.at[7 * a + k - 1], recv_sem=recv_sems.at[7 * a + k - 1],
                    device_id=peer, device_id_type=MESH).wait_send()
            pltpu.make_async_copy(ins[a], views[a](outs[a], me), local_sems.at[a]).wait()

    return pl.pallas_call(
        body, name=name,
        in_specs=[HBM_SPEC] * na, out_specs=[HBM_SPEC] * na, out_shape=out_shapes,
        scratch_shapes=level1.scratch + [pltpu.SemaphoreType.DMA((3 * na,)), pltpu.SemaphoreType.DMA((3 * na,))],
    )(*[_in_hbm(s) for s in shards])


def _sibling_comm(grads):
    na = len(grads)
    shapes = [jax.ShapeDtypeStruct(g.shape[:2] + g.shape[3:], g.dtype) for g in grads]

    def copies(ins, outs, scr):
        x, y, c = _place()
        return [pltpu.make_async_remote_copy(
            src_ref=ins[a].at[:, :, 1 - c], dst_ref=outs[a],
            send_sem=scr[0].at[a], recv_sem=scr[1].at[a],
            device_id=(x, y, 1 - c), device_id_type=MESH) for a in range(na)]

    def start(ins, outs, scr):
        for cp in copies(ins, outs, scr):
            cp.start()

    def wait(ins, outs, scr):
        for cp in copies(ins, outs, scr):
            cp.wait()

    return _Comm(grads, [HBM_SPEC] * na, shapes, [HBM_SPEC] * na,
                 [pltpu.SemaphoreType.DMA((na,)), pltpu.SemaphoreType.DMA((na,))], start, wait)


def _chips_comm(sums):
    na = len(sums)
    shapes = [jax.ShapeDtypeStruct(s.shape, s.dtype) for s in sums]

    def copies(ins, outs, scr):
        x, y, c = _place()
        mine = 2 * x + y
        out = []
        for a in range(na):
            for n, k in enumerate(SAME_CORE):
                (px, py, pc), _ = _peer(k)
                theirs = 2 * px + py
                send = functools.partial(
                    pltpu.make_async_remote_copy,
                    src_ref=ins[a].at[:, theirs], dst_ref=outs[a].at[:, mine],
                    send_sem=scr[0].at[3 * a + n], recv_sem=scr[1].at[3 * a + n],
                    device_id=(px, py, pc), device_id_type=MESH)
                recv = functools.partial(
                    pltpu.make_async_remote_copy,
                    src_ref=ins[a].at[:, mine], dst_ref=outs[a].at[:, theirs],
                    send_sem=scr[0].at[3 * a + n], recv_sem=scr[1].at[3 * a + n],
                    device_id=(px, py, pc), device_id_type=MESH)
                out.append((send, recv))
        return out

    def local(ins, outs, scr):
        x, y, _ = _place()
        mine = 2 * x + y
        return [pltpu.make_async_copy(ins[a].at[:, mine], outs[a].at[:, mine], scr[2].at[a]) for a in range(na)]

    def start(ins, outs, scr):
        for cp in local(ins, outs, scr):
            cp.start()
        for send, _ in copies(ins, outs, scr):
            send().start()

    def wait(ins, outs, scr):
        for _, recv in copies(ins, outs, scr):
            recv().wait_recv()
        for send, _ in copies(ins, outs, scr):
            send().wait_send()
        for cp in local(ins, outs, scr):
            cp.wait()

    return _Comm(sums, [HBM_SPEC] * na, shapes, [HBM_SPEC] * na,
                 [pltpu.SemaphoreType.DMA((3 * na,)), pltpu.SemaphoreType.DMA((3 * na,)),
                  pltpu.SemaphoreType.DMA((na,))], start, wait)


def _small_sum_comm(s):
    R, C = s.shape

    def copies(ins, scr):
        buf, send_sems, recv_sems = scr
        _, me = _peer(0)
        out = []
        for k in EVERYONE:
            peer, theirs = _peer(k)
            send = functools.partial(
                    pltpu.make_async_remote_copy,
                src_ref=ins[0], dst_ref=buf.at[me], send_sem=send_sems.at[k - 1], recv_sem=recv_sems.at[k - 1],
                device_id=peer, device_id_type=MESH)
            recv = functools.partial(
                    pltpu.make_async_remote_copy,
                src_ref=ins[0], dst_ref=buf.at[theirs], send_sem=send_sems.at[k - 1], recv_sem=recv_sems.at[k - 1],
                device_id=peer, device_id_type=MESH)
            out.append((send, recv))
        return out

    def start(ins, outs, scr):
        _, me = _peer(0)
        scr[0][me] = ins[0][...]
        for send, _ in copies(ins, scr):
            send().start()

    def wait(ins, outs, scr):
        for _, recv in copies(ins, scr):
            recv().wait_recv()
        for send, _ in copies(ins, scr):
            send().wait_send()
        acc = scr[0][0]
        for j in range(1, N_DEV):
            acc = acc + scr[0][j]
        outs[0][...] = acc

    return _Comm([s], [VMEM_SPEC], [jax.ShapeDtypeStruct((R, C), F32)], [VMEM_SPEC],
                 [pltpu.VMEM((N_DEV, R, C), F32), pltpu.SemaphoreType.DMA((N_DEV - 1,)),
                  pltpu.SemaphoreType.DMA((N_DEV - 1,))], start, wait)


def _prep_weights(ffn1, ffn2, w_in, w_out, name):
    rf, D = ffn1[2].shape
    ri, ro = w_in.shape[1], w_out.shape[0]

    def body(g1, u1, d1, g2, u2, d2, wi, wo, p1_ref, p2_ref, pi_ref, po_ref):
        for p_ref, shards in ((p1_ref, (g1, u1, d1)), (p2_ref, (g2, u2, d2))):
            for k, shard in enumerate(shards):
                p_ref[k] = shard[...].astype(BF16)
        pi_ref[...] = wi[...].T.astype(BF16)
        po_ref[...] = wo[...].astype(BF16)

    args = (*ffn1, *ffn2, w_in, w_out)
    whole = lambda shape: pl.BlockSpec(shape, lambda i: (0,) * len(shape))
    out_shapes = [(3, rf, D), (3, rf, D), (ri, D), (ro, D)]
    return _call(
        body, name=name, grid=(1,),
        in_specs=[whole(a.shape) for a in args], out_specs=[whole(s) for s in out_shapes],
        out_shape=[jax.ShapeDtypeStruct(s, BF16) for s in out_shapes],
        scratch_shapes=[], vmem_mib=48, args=args)


def _load_weights(w_hbm, w_vmem, sem):
    @pl.when(pl.program_id(0) == 0)
    def _():
        copies = [pltpu.make_async_copy(w_hbm.at[k], w_vmem.at[k], sem.at[k]) for k in range(3)]
        for cp in copies:
            cp.start()
        for cp in copies:
            cp.wait()


def _ffn_fwd(x, g, w3, tm, cf, name, rider=None, head=None):
    T, D = x.shape
    F = w3.shape[1]
    n_head = 0 if head is None else 2

    def body(x_ref, g_ref, w_hbm, *refs):
        head_refs, refs = refs[:n_head], refs[n_head:]
        if head is None:
            (xo_ref, h_ref, dau_ref, dag_ref, act_ref, wv, sem) = refs
        else:
            (dx_ref, dob_ref, dgf_ref, loss_ref, h_ref, dau_ref, dag_ref, act_ref, wv, sem) = refs
        _load_weights(w_hbm, wv, sem)
        xhat, _ = _rms_stats(x_ref[...])
        hb = (xhat * g_ref[...]).astype(BF16)
        h_ref[...] = hb
        for lo in range(0, F, cf):
            gate = _dot(hb, wv[0, lo:lo + cf, :], NT)
            up = _dot(hb, wv[1, lo:lo + cf, :], NT)
            sig = _sigmoid(gate)
            silu = gate * sig
            dau_ref[:, lo:lo + cf] = silu.astype(BF16)
            dag_ref[:, lo:lo + cf] = (up * (sig * (1.0 + gate * (1.0 - sig)))).astype(BF16)
            act_ref[:, lo:lo + cf] = (silu * up).astype(BF16)
        x_out = x_ref[...] + FFN_RES * _dot(act_ref[...], wv[2], NN)
        if head is None:
            xo_ref[...] = x_out
            return

        @pl.when(pl.program_id(0) == 0)
        def _():
            dgf_ref[...] = jnp.zeros_like(dgf_ref)
            loss_ref[...] = jnp.zeros_like(loss_ref)

        gf_ref, tgt_ref = head_refs
        yhat, rstd = _rms_stats(x_out)
        gf = gf_ref[...]
        err = yhat * gf - tgt_ref[...]
        loss_ref[...] += (0.5 / D) * jnp.sum(err * err)
        dx, dgf = _rms_bwd(yhat, rstd, gf, err * (1.0 / D))
        dx_ref[...] = dx
        dob_ref[...] = (FFN_RES * dx).astype(BF16)
        dgf_ref[...] += dgf

    row = pl.BlockSpec((tm, D), lambda i: (i, 0))
    hid = pl.BlockSpec((tm, F), lambda i: (i, 0))
    vec = pl.BlockSpec((1, D), lambda i: (0, 0))
    row_f32, row_bf16 = jax.ShapeDtypeStruct((T, D), F32), jax.ShapeDtypeStruct((T, D), BF16)
    if head is None:
        first_specs, first_shapes = [row], [row_f32]
    else:
        first_specs = [row, row, vec, pl.BlockSpec((1, 128), lambda i: (0, 0))]
        first_shapes = [row_f32, row_bf16, jax.ShapeDtypeStruct((1, D), F32), jax.ShapeDtypeStruct((1, 128), F32)]
    return _call(
        body, name=name, grid=(T // tm,),
        in_specs=[row, vec, HBM_SPEC] + ([] if head is None else [vec, row]),
        out_specs=first_specs + [row, hid, hid, hid],
        out_shape=first_shapes + [row_bf16] + [jax.ShapeDtypeStruct((T, F), BF16)] * 3,
        scratch_shapes=[pltpu.VMEM((3, F, D), BF16), pltpu.SemaphoreType.DMA((3,))],
        vmem_mib=48, args=(x, g, w3) + (() if head is None else tuple(head)), rider=rider)


def _ffn_dgrad(dout, x, g, dau, dag, w3, tm, cf, name, rider=None):
    T, D = x.shape
    F = w3.shape[1]

    def body(do_ref, x_ref, g_ref, dau_ref, dag_ref, w_hbm, dx_ref, dgate_ref, dup_ref, dg_ref, wv, sem):
        _load_weights(w_hbm, wv, sem)

        @pl.when(pl.program_id(0) == 0)
        def _():
            dg_ref[...] = jnp.zeros_like(dg_ref)

        dob = (FFN_RES * do_ref[...]).astype(BF16)
        for lo in range(0, F, cf):
            dact = _dot(dob, wv[2, lo:lo + cf, :], NT)
            dup_ref[:, lo:lo + cf] = (dact * dau_ref[:, lo:lo + cf].astype(F32)).astype(BF16)
            dgate_ref[:, lo:lo + cf] = (dact * dag_ref[:, lo:lo + cf].astype(F32)).astype(BF16)
        dh = _dot(dgate_ref[...], wv[0], NN) + _dot(dup_ref[...], wv[1], NN)
        xhat, rstd = _rms_stats(x_ref[...])
        dx, dg = _rms_bwd(xhat, rstd, g_ref[...], dh)
        dx_ref[...] = do_ref[...] + dx
        dg_ref[...] += dg

    row = pl.BlockSpec((tm, D), lambda i: (i, 0))
    hid = pl.BlockSpec((tm, F), lambda i: (i, 0))
    vec = pl.BlockSpec((1, D), lambda i: (0, 0))
    return _call(
        body, name=name, grid=(T // tm,),
        in_specs=[row, row, vec, hid, hid, HBM_SPEC],
        out_specs=[row, hid, hid, vec],
        out_shape=[jax.ShapeDtypeStruct((T, D), F32), jax.ShapeDtypeStruct((T, F), BF16),
                   jax.ShapeDtypeStruct((T, F), BF16), jax.ShapeDtypeStruct((1, D), F32)],
        scratch_shapes=[pltpu.VMEM((3, F, D), BF16), pltpu.SemaphoreType.DMA((3,))],
        vmem_mib=52, args=(dout, x, g, dau, dag, w3), rider=rider)


def _wgrad(lhs, rhs, tm, tf, name, rider=None):
    T, F = lhs[0].shape
    D = rhs.shape[1]
    K = len(lhs)

    def body(*refs):
        lhs_refs, rhs_ref, dw_ref, accs = refs[:K], refs[K], refs[K + 1], refs[K + 2:]
        i = pl.program_id(1)

        @pl.when(i == 0)
        def _():
            for acc in accs:
                acc[...] = jnp.zeros_like(acc)

        rv = rhs_ref[...]
        for acc, lhs_ref in zip(accs, lhs_refs):
            acc[...] += _dot(lhs_ref[...], rv, TN)

        @pl.when(i == pl.num_programs(1) - 1)
        def _():
            for k, acc in enumerate(accs):
                dw_ref[k] = acc[...].astype(BF16)

    hid = pl.BlockSpec((tm, tf), lambda f, i: (i, f))
    return _call(
        body, name=name, grid=(F // tf, T // tm),
        in_specs=[hid] * K + [pl.BlockSpec((tm, D), lambda f, i: (i, 0))],
        out_specs=[pl.BlockSpec((K, tf, D), lambda f, i: (0, f, 0))],
        out_shape=[jax.ShapeDtypeStruct((K, F, D), BF16)],
        scratch_shapes=[pltpu.VMEM((tf, D), F32)] * K,
        vmem_mib=56, args=(*lhs, rhs), rider=rider)


def _lru_gates(xr, bda_ref, bdx_ref, vec_ref):
    xrb = xr.astype(BF16)
    r = _sigmoid(_dot(xrb, bda_ref[...], NN) + vec_ref[V_BA:V_BA + 1, :])
    ig = _sigmoid(_dot(xrb, bdx_ref[...], NN) + vec_ref[V_BX:V_BX + 1, :])
    sp = _softplus_neg(vec_ref[V_LAM:V_LAM + 1, :])
    log_a = (-LRU_C * sp) * r
    a = jnp.exp(log_a)
    mult = jnp.sqrt(_neg_expm1(2.0 * log_a))
    return xrb, r, ig, sp, a, mult


def _layernorm_stats(u1):
    xc = u1 - jnp.mean(u1, axis=-1, keepdims=True)
    rs = lax.rsqrt(jnp.mean(xc * xc, axis=-1, keepdims=True) + LN_EPS)
    return xc * rs, rs


def _mix_core_fwd(x1, g, w_in_t, w_out, bda, bdx, cw, lw, vec, tm, name, rider=None):
    T, D = x1.shape
    W = cw.shape[1]
    assert tm >= CONV_HALO and w_in_t.shape[0] == 4 * W

    def body(x1_ref, g_ref, wi_ref, wo_ref, bda_ref, bdx_ref, cw_ref, lw_ref, vec_ref,
             x2_ref, z_ref, mix_ref, u1_ref, xr_ref, hst_ref, ubuf, rbuf, hc):
        @pl.when(pl.program_id(0) == 0)
        def _():
            ubuf[0:CONV_HALO, :] = jnp.zeros((CONV_HALO, W), F32)
            rbuf[0:LRU_HALO, :] = jnp.zeros((LRU_HALO, W), F32)
            hc[...] = jnp.zeros_like(hc)

        xhat, _ = _rms_stats(x1_ref[...])
        z_ref[...] = _dot((xhat * g_ref[...]).astype(BF16), wi_ref[...], NT)

        ubuf[CONV_HALO:CONV_HALO + tm, :] = z_ref[:, 0:W] * _sigmoid(z_ref[:, W:2 * W])
        u1 = jnp.zeros((tm, W), F32) + vec_ref[V_CB:V_CB + 1, :]
        base = CONV_HALO - (CONV_K - 1)
        for off, win in _row_windows(ubuf, tm, range(base, base + CONV_K)):
            u1 = u1 + cw_ref[off - base:off - base + 1, :] * win
        ubuf[0:CONV_HALO, :] = ubuf[tm:tm + CONV_HALO, :]
        u1_ref[...] = u1
        xh, _ = _layernorm_stats(u1)
        u2 = xh * vec_ref[V_LNG:V_LNG + 1, :] + vec_ref[V_LNB:V_LNB + 1, :]
        ub = (u2 * _sigmoid(u2)).astype(BF16)
        mix_ref[:, 0:W] = ub

        rbuf[LRU_HALO:LRU_HALO + tm, :] = z_ref[:, 2 * W:3 * W]
        xr = jnp.zeros((tm, W), F32) + vec_ref[V_LCB:V_LCB + 1, :]
        for k in range(LRU_K):
            off = LRU_HALO - (LRU_K - 1) + k
            xr = xr + lw_ref[k:k + 1, :] * rbuf[off:off + tm, :]
        rbuf[0:LRU_HALO, :] = rbuf[tm:tm + LRU_HALO, :]
        xr_ref[...] = xr
        _, _, ig, _, a, mult = _lru_gates(xr, bda_ref, bdx_ref, vec_ref)
        hc[0:1, :] = _scan_rows(a, mult * (ig * xr), hc[0:1, :], hst_ref)
        gl, _ = _gelu_parts(z_ref[:, 3 * W:4 * W])
        yb = (hst_ref[...] * gl).astype(BF16)
        mix_ref[:, W:2 * W] = yb

        x2_ref[...] = x1_ref[...] + _dot(ub, wo_ref[0:W, :], NN) + _dot(yb, wo_ref[W:2 * W, :], NN)

    full = lambda a: pl.BlockSpec(a.shape, lambda i: (0,) * a.ndim)
    tile = lambda n: pl.BlockSpec((tm, n), lambda i: (i, 0))
    return _call(
        body, name=name, grid=(T // tm,),
        in_specs=[tile(D), full(g), full(w_in_t), full(w_out), full(bda), full(bdx), full(cw), full(lw), full(vec)],
        out_specs=[tile(D), tile(4 * W), tile(2 * W), tile(W), tile(W), tile(W)],
        out_shape=[jax.ShapeDtypeStruct((T, D), F32), jax.ShapeDtypeStruct((T, 4 * W), F32),
                   jax.ShapeDtypeStruct((T, 2 * W), BF16), jax.ShapeDtypeStruct((T, W), F32),
                   jax.ShapeDtypeStruct((T, W), F32), jax.ShapeDtypeStruct((T, W), F32)],
        scratch_shapes=[pltpu.VMEM((tm + CONV_HALO, W), F32), pltpu.VMEM((tm + LRU_HALO, W), F32),
                        pltpu.VMEM((8, W), F32)],
        vmem_mib=56, args=(x1, g, w_in_t, w_out, bda, bdx, cw, lw, vec), rider=rider)


def _mix_bwd(dx2, z, u1, xr, hst, x1, mix, g, w_in_t, w_out, bda, bdx, cw, lw, vec, tm, name, rider=None):
    T, D = dx2.shape
    W = cw.shape[1]
    nt = T // tm
    assert tm >= CONV_HALO and tm % CONV_HALO == 0

    def body(dx_ref, z_ref, zh_ref, u1_ref, xr_ref, h_ref, hh_ref, wo_ref, bda_ref, bdx_ref, cw_ref, lw_ref, vec_ref,
             x1_ref, mix_ref, g_ref, wi_ref,
             dx1_ref, sg_ref, dbda_ref, dbdx_ref, dob_ref, dg_ref, dwi_ref, dwo_ref,
             u0buf, du1buf, rxbuf, dxrbuf, gbuf, gc, spacc, dz_ref, ai_ref, ao_ref):
        i = pl.program_id(0)
        first = i == nt - 1
        row = lax.broadcasted_iota(jnp.int32, (tm, W), 0)

        @pl.when(i == 0)
        def _():
            sg_ref[...] = jnp.zeros_like(sg_ref)
            dbda_ref[...] = jnp.zeros_like(dbda_ref)
            dbdx_ref[...] = jnp.zeros_like(dbdx_ref)
            du1buf[tm:tm + CONV_HALO, :] = jnp.zeros((CONV_HALO, W), F32)
            dxrbuf[tm:tm + LRU_HALO, :] = jnp.zeros((LRU_HALO, W), F32)
            gc[...] = jnp.zeros_like(gc)
            spacc[...] = jnp.zeros_like(spacc)
            dg_ref[...] = jnp.zeros_like(dg_ref)
            ai_ref[...] = jnp.zeros_like(ai_ref)
            ao_ref[...] = jnp.zeros_like(ao_ref)

        def accum(r, val):
            sg_ref[r:r + 1, :] += jnp.sum(val, axis=0, keepdims=True)

        x1hat, x1rstd = _rms_stats(x1_ref[...])
        gain = g_ref[...]
        hb = (x1hat * gain).astype(BF16)

        def in_proj_bwd(lo, hi):
            dzb = dz_ref[:, lo:hi]
            ai_ref[lo:hi, :] += _dot(dzb, hb, TN)
            return _dot(dzb, wi_ref[lo:hi, :], NN)

        dxb = dx_ref[...].astype(BF16)
        ao_ref[...] += _dot(mix_ref[...], dxb, TN)
        dmix = _dot(dxb, wo_ref[...], NT)
        d_u = dmix[:, 0:W]
        d_yr = dmix[:, W:2 * W]

        xh, rs = _layernorm_stats(u1_ref[...])
        ln_g = vec_ref[V_LNG:V_LNG + 1, :]
        u2 = xh * ln_g + vec_ref[V_LNB:V_LNB + 1, :]
        s2 = _sigmoid(u2)
        d_u2 = d_u * (s2 * (1.0 + u2 * (1.0 - s2)))
        accum(G_LNG, d_u2 * xh)
        accum(G_LNB, d_u2)
        d_xh = d_u2 * ln_g
        d_u1 = rs * (d_xh - jnp.mean(d_xh, axis=-1, keepdims=True)
                     - xh * jnp.mean(d_xh * xh, axis=-1, keepdims=True))
        accum(G_CB, d_u1)
        halo_on = jnp.where(first, 0.0, 1.0)
        u0buf[0:CONV_HALO, :] = halo_on * (zh_ref[:, 0:W] * _sigmoid(zh_ref[:, W:2 * W]))
        cv = z_ref[:, 0:W]
        sgc = _sigmoid(z_ref[:, W:2 * W])
        u0buf[CONV_HALO:CONV_HALO + tm, :] = cv * sgc
        du1buf[0:tm, :] = d_u1
        base = CONV_HALO - (CONV_K - 1)
        for off, win in _row_windows(u0buf, tm, range(base, base + CONV_K)):
            accum(G_CW + off - base, d_u1 * win)
        d_u0 = jnp.zeros((tm, W), F32)
        for off, win in _row_windows(du1buf, tm, range(0, CONV_K)):
            d_u0 = d_u0 + cw_ref[CONV_K - 1 - off:CONV_K - off, :] * win
        du1buf[tm:tm + CONV_HALO, :] = du1buf[0:CONV_HALO, :]
        dz_ref[:, 0:W] = (d_u0 * sgc).astype(BF16)
        dz_ref[:, W:2 * W] = (d_u0 * cv * (sgc * (1.0 - sgc))).astype(BF16)
        dh = in_proj_bwd(0, 2 * W)

        xrv = xr_ref[...]
        xrb, r, ig, sp, a, mult = _lru_gates(xrv, bda_ref, bdx_ref, vec_ref)
        h = h_ref[...]
        gl, dgl = _gelu_parts(z_ref[:, 3 * W:4 * W])
        dz_ref[:, 3 * W:4 * W] = (d_yr * h * dgl).astype(BF16)
        dh = dh + in_proj_bwd(3 * W, 4 * W)
        a_next = jnp.where(row == tm - 1, 1.0, pltpu.roll(a, tm - 1, 0))
        g_first = _scan_rows(a_next, d_yr * gl, gc[0:1, :], gbuf, reverse=True)
        g = gbuf[...]
        gc[0:1, :] = a[0:1, :] * g_first
        hprev = jnp.where(row == 0, halo_on * hh_ref[LRU_HALO - 1:LRU_HALO, :], pltpu.roll(h, 1, 0))
        d_log_a = (g * hprev) * a - (g * ig * xrv) * (a * a) / mult
        d_ig = g * mult * xrv
        d_xr = g * mult * ig
        spacc[0:1, :] += jnp.sum(d_log_a * r, axis=0, keepdims=True)
        d_pa32 = (d_log_a * (-LRU_C * sp)) * (r * (1.0 - r))
        d_px32 = d_ig * (ig * (1.0 - ig))
        accum(G_BA, d_pa32)
        accum(G_BX, d_px32)
        d_pa = d_pa32.astype(BF16)
        d_px = d_px32.astype(BF16)
        d_xr = d_xr + _dot(d_pa, bda_ref[...], NT) + _dot(d_px, bdx_ref[...], NT)
        dbda_ref[...] += _dot(xrb, d_pa, TN)
        dbdx_ref[...] += _dot(xrb, d_px, TN)
        accum(G_LCB, d_xr)
        rxbuf[0:LRU_HALO, :] = halo_on * zh_ref[CONV_HALO - LRU_HALO:CONV_HALO, 2 * W:3 * W]
        rxbuf[LRU_HALO:LRU_HALO + tm, :] = z_ref[:, 2 * W:3 * W]
        dxrbuf[0:tm, :] = d_xr
        d_rx = jnp.zeros((tm, W), F32)
        for k in range(LRU_K):
            off = LRU_HALO - (LRU_K - 1) + k
            accum(G_LW + k, d_xr * rxbuf[off:off + tm, :])
            d_rx = d_rx + lw_ref[k:k + 1, :] * dxrbuf[LRU_K - 1 - k:LRU_K - 1 - k + tm, :]
        dxrbuf[tm:tm + LRU_HALO, :] = dxrbuf[0:LRU_HALO, :]
        dz_ref[:, 2 * W:3 * W] = d_rx.astype(BF16)
        dh = dh + in_proj_bwd(2 * W, 3 * W)

        dx, dg = _rms_bwd(x1hat, x1rstd, gain, dh)
        dx1 = dx_ref[...] + dx
        dx1_ref[...] = dx1
        dob_ref[...] = (FFN_RES * dx1).astype(BF16)
        dg_ref[...] += dg

        @pl.when(first)
        def _():
            lam = vec_ref[V_LAM:V_LAM + 1, :]
            sg_ref[G_LAM:G_LAM + 1, :] = LRU_C * _sigmoid(-lam) * spacc[0:1, :]
            dwi_ref[...] = ai_ref[...].astype(BF16)
            dwo_ref[...] = ao_ref[...].astype(BF16)

    full = lambda a: pl.BlockSpec(a.shape, lambda i: (0,) * a.ndim)
    tile = lambda n: pl.BlockSpec((tm, n), lambda i: (nt - 1 - i, 0))
    halo = lambda rows, n: pl.BlockSpec(
        (rows, n), lambda i: (jnp.maximum((nt - 1 - i) * (tm // rows) - 1, 0), 0))
    const = lambda r, c: pl.BlockSpec((r, c), lambda i: (0, 0))
    return _call(
        body, name=name, grid=(nt,),
        in_specs=[tile(D), tile(4 * W), halo(CONV_HALO, 4 * W), tile(W), tile(W), tile(W), halo(LRU_HALO, W),
                  full(w_out), full(bda), full(bdx), full(cw), full(lw), full(vec),
                  tile(D), tile(2 * W), full(g), full(w_in_t)],
        out_specs=[tile(D), const(G_ROWS, W), const(W, W), const(W, W),
                   tile(D), const(1, D), const(4 * W, D), const(2 * W, D)],
        out_shape=[jax.ShapeDtypeStruct((T, D), F32), jax.ShapeDtypeStruct((G_ROWS, W), F32),
                   jax.ShapeDtypeStruct((W, W), F32), jax.ShapeDtypeStruct((W, W), F32),
                   jax.ShapeDtypeStruct((T, D), BF16), jax.ShapeDtypeStruct((1, D), F32),
                   jax.ShapeDtypeStruct((4 * W, D), BF16), jax.ShapeDtypeStruct((2 * W, D), BF16)],
        scratch_shapes=[pltpu.VMEM((tm + CONV_HALO, W), F32), pltpu.VMEM((tm + CONV_HALO, W), F32),
                        pltpu.VMEM((tm + LRU_HALO, W), F32), pltpu.VMEM((tm + LRU_HALO, W), F32),
                        pltpu.VMEM((tm, W), F32), pltpu.VMEM((8, W), F32), pltpu.VMEM((8, W), F32),
                        pltpu.VMEM((tm, 4 * W), BF16), pltpu.VMEM((4 * W, D), F32), pltpu.VMEM((2 * W, D), F32)],
        vmem_mib=60, args=(dx2, z, z, u1, xr, hst, hst, w_out, bda, bdx, cw, lw, vec, x1, mix, g, w_in_t),
        rider=rider)


def _pair_add(full, recv, name):
    K, _, _, rows, D = full.shape

    def body(c_ref, a_ref, b_ref, o_ref):
        o_ref[...] = (a_ref[...].astype(F32) + b_ref[...].astype(F32)).astype(BF16)

    c = lax.axis_index("c").astype(jnp.int32).reshape((1,))
    return _call(
        body, name=name, grid=(K, N_CHIP), num_scalar_prefetch=1,
        in_specs=[pl.BlockSpec((None, None, None, rows, D), lambda k, q, c_ref: (k, q, c_ref[0], 0, 0)),
                  pl.BlockSpec((None, None, rows, D), lambda k, q, c_ref: (k, q, 0, 0))],
        out_specs=pl.BlockSpec((None, None, rows, D), lambda k, q, c_ref: (k, q, 0, 0)),
        out_shape=jax.ShapeDtypeStruct(recv.shape, BF16),
        scratch_shapes=[], vmem_mib=16, args=(c, full, recv))


def _adamw_update(wv, gv, mv, vv):
    m2 = ADAM_B1 * mv + (1.0 - ADAM_B1) * gv
    v2 = ADAM_B2 * vv + (1.0 - ADAM_B2) * (gv * gv)
    m_hat = m2 / (1.0 - ADAM_B1 ** ADAM_STEP)
    v_hat = v2 / (1.0 - ADAM_B2 ** ADAM_STEP)
    return -ADAM_LR * (m_hat / (jnp.sqrt(v_hat) + ADAM_EPS) + ADAM_WD * wv), m2, v2


def _finish(parts, k, w, m, v, transpose, name):
    _, _, rows, D = parts.shape

    def body(p_ref, w_ref, m_ref, v_ref, g_ref, d_ref, mo_ref, vo_ref):
        acc = p_ref[0].astype(F32)
        for q in range(1, N_CHIP):
            acc = acc + p_ref[q].astype(F32)
        gv = acc.T if transpose else acc
        g_ref[...] = gv
        d_ref[...], mo_ref[...], vo_ref[...] = _adamw_update(w_ref[...], gv, m_ref[...], v_ref[...])

    whole = pl.BlockSpec(w.shape, lambda i: (0, 0))
    return _call(
        body, name=name, grid=(1,),
        in_specs=[pl.BlockSpec((None, N_CHIP, rows, D), lambda i: (k, 0, 0, 0)), whole, whole, whole],
        out_specs=[whole] * 4, out_shape=[pltpu.HBM(w.shape, F32)] * 4,
        scratch_shapes=[], vmem_mib=40, args=(parts, w, m, v))


def _adamw(w, g, m, v, name):
    def body(w_ref, g_ref, m_ref, v_ref, d_ref, mo_ref, vo_ref):
        d_ref[...], mo_ref[...], vo_ref[...] = _adamw_update(w_ref[...], g_ref[...], m_ref[...], v_ref[...])

    shape = jax.ShapeDtypeStruct(w.shape, F32)
    return pl.pallas_call(
        body, name=name,
        in_specs=[VMEM_SPEC] * 4, out_specs=[VMEM_SPEC] * 3, out_shape=[shape] * 3,
        compiler_params=pltpu.CompilerParams(vmem_limit_bytes=32 * MIB),
    )(w, g, m, v)


def _block_diag(w):
    h, d, _ = w.shape
    out = jnp.zeros((h * d, h * d), w.dtype)
    for k in range(h):
        out = lax.dynamic_update_slice(out, w[k], (k * d, k * d))
    return out


def _diag_blocks(m, h):
    d = m.shape[0] // h
    return jnp.stack([m[k * d:(k + 1) * d, k * d:(k + 1) * d] for k in range(h)])


def _reduce_level1(full, tag):
    got = _run_comm(_sibling_comm(full), "rs_sibling_" + tag)
    return [_pair_add(a, b, "rs_pair_add_%s%d" % (tag, n)) for n, (a, b) in enumerate(zip(full, got))]


def kernel(x, ffn1_norm, ffn1_w_gate, ffn1_w_up, ffn1_w_down, mix_norm, w_in, conv_dw, conv_dw_bias, conv_ln_g, conv_ln_b, lru_conv_w, lru_conv_b, lru_w_a, lru_b_a, lru_w_x, lru_b_x, lru_lambda, w_out, ffn2_norm, ffn2_w_gate, ffn2_w_up, ffn2_w_down, final_norm, loss_target, m_ffn1_norm, m_ffn1_w_gate, m_ffn1_w_up, m_ffn1_w_down, m_mix_norm, m_w_in, m_conv_dw, m_conv_dw_bias, m_conv_ln_g, m_conv_ln_b, m_lru_conv_w, m_lru_conv_b, m_lru_w_a, m_lru_b_a, m_lru_w_x, m_lru_b_x, m_lru_lambda, m_w_out, m_ffn2_norm, m_ffn2_w_gate, m_ffn2_w_up, m_ffn2_w_down, m_final_norm, v_ffn1_norm, v_ffn1_w_gate, v_ffn1_w_up, v_ffn1_w_down, v_mix_norm, v_w_in, v_conv_dw, v_conv_dw_bias, v_conv_ln_g, v_conv_ln_b, v_lru_conv_w, v_lru_conv_b, v_lru_w_a, v_lru_b_a, v_lru_w_x, v_lru_b_x, v_lru_lambda, v_w_out, v_ffn2_norm, v_ffn2_w_gate, v_ffn2_w_up, v_ffn2_w_down, v_final_norm):
    T, D = x.shape[1], x.shape[2]
    F = ffn1_w_down.shape[0] * N_DEV
    rf = ffn1_w_down.shape[0]
    ri = w_in.shape[1]
    ro = w_out.shape[0]
    W = conv_dw_bias.shape[0]
    wc = conv_dw.shape[1]
    H = lru_w_a.shape[0]
    xs = x.reshape(T, D)
    tgt = loss_target.reshape(T, D)
    tm_ffn = min(256, T)
    cf = 256
    tm_w = min(1024, T)
    tm_mix = min(256, T)
    tf_w = F // 2
    row = lambda v: v.reshape(1, -1)
    by_owner = lambda a, rows: a.reshape(a.shape[0], N_CHIP, 2, rows, D)

    p3a, p3b, p_in, p_out = _prep_weights(
        (ffn1_w_gate.T, ffn1_w_up.T, ffn1_w_down), (ffn2_w_gate.T, ffn2_w_up.T, ffn2_w_down), w_in, w_out,
        "prep_weights")
    tile_rows = lambda a: jnp.pad(a, ((0, -a.shape[0] % SUBLANES), (0, 0)))
    p_cw = jnp.concatenate([tile_rows(conv_dw), tile_rows(lru_conv_w)], axis=0)
    lw_row = p_cw.shape[0] - SUBLANES
    stacked = lambda r, j: r.at[:, j]
    plain = lambda r, j: r.at[j]
    g3_shape = jax.ShapeDtypeStruct((3, N_DEV, rf, D), BF16)
    (g3a,) = _all_gather([p3a], [stacked], [g3_shape], "ag_ffn1")
    w3a = g3a.reshape(3, F, D)
    bda = _block_diag(lru_w_a).astype(BF16)
    bdx = _block_diag(lru_w_x).astype(BF16)
    vec = jnp.concatenate([tile_rows(v[None]) for v in
                           (conv_dw_bias, conv_ln_g, conv_ln_b, lru_conv_b, lru_b_a, lru_b_x, lru_lambda)], axis=0)

    gather_rest = _gather_comm(
        [p3b, p_in, p_out, p_cw], [stacked, plain, plain, plain],
        [g3_shape, jax.ShapeDtypeStruct((N_DEV, ri, D), BF16), jax.ShapeDtypeStruct((N_DEV, ro, D), BF16),
         jax.ShapeDtypeStruct((N_DEV,) + p_cw.shape, F32)],
        [(SIBLING,) + SAME_CORE, EVERYONE, EVERYONE, EVERYONE])
    (x1, h1, dau1, dag1, act1), (g3b_half, g_in, g_out, g_cw) = _ffn_fwd(
        xs, row(ffn1_norm), w3a, tm_ffn, cf, "ffn1_fwd", rider=gather_rest)
    w_in_t = g_in.reshape(N_DEV * ri, D)
    w_out_f = g_out.reshape(N_DEV * ro, D)
    cw_all = jnp.transpose(g_cw, (1, 0, 2)).reshape(p_cw.shape[0], N_DEV * wc)
    cw = cw_all[0:CONV_K]
    lw = cw_all[lw_row:lw_row + LRU_K]
    (x2, z, mix, u1, xr, hst), (g3b,) = _mix_core_fwd(
        x1, row(mix_norm), w_in_t, w_out_f, bda, bdx, cw, lw, vec, tm_mix, "mix_core_fwd",
        rider=_forward_comm([g3b_half], [stacked]))
    w3b = g3b.reshape(3, F, D)
    dx3, dob2, d_final_norm, loss_part, h3, dau2, dag2, act2 = _ffn_fwd(
        x2, row(ffn2_norm), w3b, tm_ffn, cf, "ffn2_fwd_loss", head=(row(final_norm), tgt))

    dx2, dgate2, dup2, d_ffn2_norm = _ffn_dgrad(dx3, x2, row(ffn2_norm), dau2, dag2, w3b, tm_ffn, cf, "ffn2_dgrad")
    (dw_gu2,) = _wgrad([dgate2, dup2], h3, tm_w, tf_w, "ffn2_wgrad_gu")
    (dw_d2,) = _wgrad([act2], dob2, tm_w, tf_w, "ffn2_wgrad_d")
    sums_f2 = _reduce_level1([by_owner(dw_gu2, rf), by_owner(dw_d2, rf)], "f2")
    (dx1, sg, dbda, dbdx, dob1, d_mix_norm, dw_in_t, dw_out), parts_f2 = _mix_bwd(
        dx2, z, u1, xr, hst, x1, mix, row(mix_norm), w_in_t, w_out_f, bda, bdx, cw, lw, vec, tm_mix, "mix_bwd",
        rider=_chips_comm(sums_f2))
    sums_io = _reduce_level1([by_owner(dw_in_t[None], ri), by_owner(dw_out[None], ro)], "io")
    (dw_d1,), parts_io = _wgrad([act1], dob1, tm_w, tf_w, "ffn1_wgrad_d", rider=_chips_comm(sums_io))
    sums_d1 = _reduce_level1([by_owner(dw_d1, rf)], "d1")
    dx0, dgate1, dup1, d_ffn1_norm = _ffn_dgrad(dx1, xs, row(ffn1_norm), dau1, dag1, w3a, tm_ffn, cf, "ffn1_dgrad")

    d_lru_w_a = _diag_blocks(dbda, H)
    d_lru_w_x = _diag_blocks(dbdx, H)
    small = [d_ffn1_norm, d_mix_norm, d_ffn2_norm, d_final_norm,
             sg[G_CB], sg[G_LNG], sg[G_LNB], sg[G_LCB], sg[G_BA], sg[G_BX], sg[G_LAM],
             d_lru_w_a, d_lru_w_x,
             sg[G_CW:G_CW + CONV_K], sg[G_LW:G_LW + LRU_K],
             loss_part]
    sizes = [a.size for a in small]
    flat = jnp.concatenate([a.reshape(-1) for a in small])
    n_rows = -(-flat.size // (8 * D)) * 8
    packed = jnp.concatenate([flat, jnp.zeros((n_rows * D - flat.size,), F32)]).reshape(n_rows, D)
    (dw_g1,), (red, *parts_d1) = _wgrad(
        [dgate1], h1, tm_w, tf_w, "ffn1_wgrad_g", rider=_both(_small_sum_comm(packed), _chips_comm(sums_d1)))
    sums_g1 = _reduce_level1([by_owner(dw_g1, rf)], "g1")
    (dw_u1,), parts_g1 = _wgrad([dup1], h1, tm_w, tf_w, "ffn1_wgrad_u", rider=_chips_comm(sums_g1))
    sums_u1 = _reduce_level1([by_owner(dw_u1, rf)], "u1")
    parts_u1 = _run_comm(_chips_comm(sums_u1), "rs_chips_u1")

    red = red.reshape(-1)
    offs = [0]
    for n in sizes:
        offs.append(offs[-1] + n)
    piece = lambda k, shape: red[offs[k]:offs[k + 1]].reshape(shape)
    loss = red[offs[15]]

    me = 4 * lax.axis_index("x") + 2 * lax.axis_index("y") + lax.axis_index("c")
    chan = lambda full_g: lax.dynamic_slice_in_dim(full_g, me * wc, wc, axis=1)
    grads = {
        "ffn1_norm": piece(0, (D,)), "mix_norm": piece(1, (D,)), "ffn2_norm": piece(2, (D,)), "final_norm": piece(3, (D,)),
        "conv_dw_bias": piece(4, (W,)), "conv_ln_g": piece(5, (W,)), "conv_ln_b": piece(6, (W,)),
        "lru_conv_b": piece(7, (W,)), "lru_b_a": piece(8, (W,)), "lru_b_x": piece(9, (W,)), "lru_lambda": piece(10, (W,)),
        "lru_w_a": piece(11, lru_w_a.shape), "lru_w_x": piece(12, lru_w_x.shape),
        "conv_dw": chan(piece(13, (CONV_K, W))), "lru_conv_w": chan(piece(14, (LRU_K, W))),
    }

    weights = dict(ffn1_norm=ffn1_norm, ffn1_w_gate=ffn1_w_gate, ffn1_w_up=ffn1_w_up, ffn1_w_down=ffn1_w_down, mix_norm=mix_norm, w_in=w_in, conv_dw=conv_dw, conv_dw_bias=conv_dw_bias, conv_ln_g=conv_ln_g, conv_ln_b=conv_ln_b, lru_conv_w=lru_conv_w, lru_conv_b=lru_conv_b, lru_w_a=lru_w_a, lru_b_a=lru_b_a, lru_w_x=lru_w_x, lru_b_x=lru_b_x, lru_lambda=lru_lambda, w_out=w_out, ffn2_norm=ffn2_norm, ffn2_w_gate=ffn2_w_gate, ffn2_w_up=ffn2_w_up, ffn2_w_down=ffn2_w_down, final_norm=final_norm)
    moment1 = dict(ffn1_norm=m_ffn1_norm, ffn1_w_gate=m_ffn1_w_gate, ffn1_w_up=m_ffn1_w_up, ffn1_w_down=m_ffn1_w_down, mix_norm=m_mix_norm, w_in=m_w_in, conv_dw=m_conv_dw, conv_dw_bias=m_conv_dw_bias, conv_ln_g=m_conv_ln_g, conv_ln_b=m_conv_ln_b, lru_conv_w=m_lru_conv_w, lru_conv_b=m_lru_conv_b, lru_w_a=m_lru_w_a, lru_b_a=m_lru_b_a, lru_w_x=m_lru_w_x, lru_b_x=m_lru_b_x, lru_lambda=m_lru_lambda, w_out=m_w_out, ffn2_norm=m_ffn2_norm, ffn2_w_gate=m_ffn2_w_gate, ffn2_w_up=m_ffn2_w_up, ffn2_w_down=m_ffn2_w_down, final_norm=m_final_norm)
    moment2 = dict(ffn1_norm=v_ffn1_norm, ffn1_w_gate=v_ffn1_w_gate, ffn1_w_up=v_ffn1_w_up, ffn1_w_down=v_ffn1_w_down, mix_norm=v_mix_norm, w_in=v_w_in, conv_dw=v_conv_dw, conv_dw_bias=v_conv_dw_bias, conv_ln_g=v_conv_ln_g, conv_ln_b=v_conv_ln_b, lru_conv_w=v_lru_conv_w, lru_conv_b=v_lru_conv_b, lru_w_a=v_lru_w_a, lru_b_a=v_lru_b_a, lru_w_x=v_lru_w_x, lru_b_x=v_lru_b_x, lru_lambda=v_lru_lambda, w_out=v_w_out, ffn2_norm=v_ffn2_norm, ffn2_w_gate=v_ffn2_w_gate, ffn2_w_up=v_ffn2_w_up, ffn2_w_down=v_ffn2_w_down, final_norm=v_final_norm)
    order = list(weights)
    big = {"ffn1_w_gate": (parts_g1[0], 0, True), "ffn1_w_up": (parts_u1[0], 0, True),
           "ffn1_w_down": (parts_d1[0], 0, False), "w_in": (parts_io[0], 0, True), "w_out": (parts_io[1], 0, False),
           "ffn2_w_gate": (parts_f2[0], 0, True), "ffn2_w_up": (parts_f2[0], 1, True),
           "ffn2_w_down": (parts_f2[1], 0, False)}
    delta, new_m, new_v = {}, {}, {}
    for n, (parts, k, d_major) in big.items():
        operands = weights[n], moment1[n], moment2[n]
        if d_major and n != "w_in":
            results = _finish(parts, k, *[a.T for a in operands], False, "finish_" + n)
            grads[n], delta[n], new_m[n], new_v[n] = [r.T for r in results]
        else:
            grads[n], delta[n], new_m[n], new_v[n] = _finish(parts, k, *operands, d_major, "finish_" + n)
    rest = [n for n in order if n not in big]
    rest_sizes = [weights[n].size for n in rest]
    n_flat = sum(rest_sizes)
    pad_rows = -(-n_flat // (8 * 128)) * 8

    def pack(d):
        f = jnp.concatenate([d[n].reshape(-1) for n in rest])
        return jnp.concatenate([f, jnp.zeros((pad_rows * 128 - n_flat,), F32)]).reshape(pad_rows, 128)

    d_s, m_s, v_s = _adamw(pack(weights), pack(grads), pack(moment1), pack(moment2), "adamw_small")
    off = 0
    for n, size in zip(rest, rest_sizes):
        shape = weights[n].shape
        delta[n] = d_s.reshape(-1)[off:off + size].reshape(shape)
        new_m[n] = m_s.reshape(-1)[off:off + size].reshape(shape)
        new_v[n] = v_s.reshape(-1)[off:off + size].reshape(shape)
        off += size

    return (loss, dx0.reshape(x.shape), *[grads[n] for n in order], *[delta[n] for n in order],
            *[new_m[n] for n in order], *[new_v[n] for n in order])
```

```python
import functools
import math

import jax
import jax.numpy as jnp
from jax import lax
from jax.experimental import pallas as pl
from jax.experimental.pallas import tpu as pltpu

F32 = jnp.float32
BF16 = jnp.bfloat16
MESH = pl.DeviceIdType.MESH

N_DEV = 8
N_CHIP = 4
SUBLANES = 8
RMS_EPS = 1e-6
LN_EPS = 1e-5
LRU_C = 8.0
CONV_K = 31
LRU_K = 4
CONV_HALO = 32
LRU_HALO = 8
FFN_RES = 0.5
ADAM_LR, ADAM_B1, ADAM_B2, ADAM_EPS, ADAM_WD, ADAM_STEP = 0.001, 0.9, 0.999, 1e-08, 0.01, 10
GELU_K = math.sqrt(2.0 / math.pi)
GELU_C = 0.044715

MIB = 1024 * 1024
NT = (((1,), (1,)), ((), ()))
NN = (((1,), (0,)), ((), ()))
TN = (((0,), (0,)), ((), ()))

V_CB, V_LNG, V_LNB, V_LCB, V_BA, V_BX, V_LAM = range(0, 7 * SUBLANES, SUBLANES)
G_CW = 0
G_CB, G_LNG, G_LNB = 31, 32, 33
G_LW = 34
G_LCB, G_BA, G_BX, G_LAM = 38, 39, 40, 41
G_ROWS = 48

HBM_SPEC = pl.BlockSpec(memory_space=pltpu.HBM)
VMEM_SPEC = pl.BlockSpec(memory_space=pltpu.VMEM)


def _dot(a, b, dims):
    return lax.dot_general(a, b, dims, preferred_element_type=F32)


def _sigmoid(x):
    return 1.0 / (1.0 + jnp.exp(-x))


def _gelu_parts(x):
    x2 = x * x
    th = jnp.tanh(GELU_K * x * (1.0 + GELU_C * x2))
    gl = 0.5 * x * (1.0 + th)
    dgl = 0.5 * (1.0 + th) + 0.5 * x * (1.0 - th * th) * GELU_K * (1.0 + 3.0 * GELU_C * x2)
    return gl, dgl


def _neg_expm1(y):
    series = -y * (1.0 + y * (1.0 / 2) * (1.0 + y * (1.0 / 3) * (1.0 + y * (1.0 / 4) * (1.0 + y * (1.0 / 5) * (1.0 + y * (1.0 / 6))))))
    return jnp.where(y > -0.25, series, 1.0 - jnp.exp(y))


def _softplus_neg(lam):
    t = -lam
    e = jnp.exp(-jnp.abs(t))
    s = 1.0 + e
    log1p_e = jnp.log(s) - ((s - 1.0) - e) / s
    return jnp.maximum(t, 0.0) + log1p_e


def _rms_stats(xv):
    rstd = lax.rsqrt(jnp.mean(xv * xv, axis=-1, keepdims=True) + RMS_EPS)
    return xv * rstd, rstd


def _rms_bwd(xhat, rstd, g, dh):
    dxhat = dh * g
    dx = rstd * (dxhat - xhat * jnp.mean(dxhat * xhat, axis=-1, keepdims=True))
    return dx, jnp.sum(dh * xhat, axis=0, keepdims=True)


def _row_windows(buf_ref, n_rows, offsets):
    total = buf_ref.shape[0]
    full = buf_ref[...]
    for b in range(SUBLANES):
        offs = [o for o in offsets if o % SUBLANES == b]
        if not offs:
            continue
        assert max(offs) + n_rows <= total
        moved = full if b == 0 else pltpu.roll(full, total - b, 0)
        for o in offs:
            yield o, moved[o - b:o - b + n_rows, :]


def _scan_rows(av, bv, edge, out_ref, reverse=False):
    tm, W = av.shape
    sub = lax.broadcasted_iota(jnp.int32, (tm, W), 0) % SUBLANES
    s = 1
    while s < SUBLANES:
        keep = (sub < SUBLANES - s) if reverse else (sub >= s)
        shift = tm - s if reverse else s
        bv = jnp.where(keep, av * pltpu.roll(bv, shift, 0) + bv, bv)
        av = jnp.where(keep, av * pltpu.roll(av, shift, 0), av)
        s *= 2
    starts = range(0, tm, SUBLANES)
    for r0 in (reversed(starts) if reverse else starts):
        group = av[r0:r0 + SUBLANES, :] * edge + bv[r0:r0 + SUBLANES, :]
        out_ref[r0:r0 + SUBLANES, :] = group
        edge = group[0:1, :] if reverse else group[SUBLANES - 1:SUBLANES, :]
    return edge


class _Comm:
    def __init__(self, arrays, in_specs, out_shapes, out_specs, scratch, start, wait, aliases=None):
        self.arrays, self.in_specs = list(arrays), list(in_specs)
        self.out_shapes, self.out_specs = list(out_shapes), list(out_specs)
        self.scratch, self.start, self.wait = list(scratch), start, wait
        self.aliases = dict(aliases or {})


def _in_hbm(a):
    return pltpu.with_memory_space_constraint(a, pltpu.HBM)


def _operands(comm):
    return [a if spec is VMEM_SPEC else _in_hbm(a) for a, spec in zip(comm.arrays, comm.in_specs)]


def _call(body, *, name, grid, in_specs, out_specs, out_shape, scratch_shapes, vmem_mib, args, rider=None,
          num_scalar_prefetch=0):
    params = pltpu.CompilerParams(dimension_semantics=("arbitrary",) * len(grid), vmem_limit_bytes=vmem_mib * MIB)
    args = [a if k < num_scalar_prefetch else _in_hbm(a) for k, a in enumerate(args)]
    if rider is None:
        return pl.pallas_call(
            body, name=name,
            grid_spec=pltpu.PrefetchScalarGridSpec(
                num_scalar_prefetch=num_scalar_prefetch, grid=grid, in_specs=in_specs, out_specs=out_specs,
                scratch_shapes=scratch_shapes),
            out_shape=out_shape, compiler_params=params)(*args)
    assert num_scalar_prefetch == 0
    n_in, n_out, n_scr = len(in_specs), len(out_specs), len(scratch_shapes)
    r_in, r_out = len(rider.arrays), len(rider.out_shapes)
    n_axes = len(grid)

    def carried(*refs):
        pos = [0]

        def take(n):
            pos[0] += n
            return refs[pos[0] - n:pos[0]]

        ins, r_ins, outs, r_outs, scr, r_scr = take(n_in), take(r_in), take(n_out), take(r_out), take(n_scr), take(len(rider.scratch))
        first = pl.program_id(0) == 0
        last = pl.program_id(0) == grid[0] - 1
        for ax in range(1, n_axes):
            first = first & (pl.program_id(ax) == 0)
            last = last & (pl.program_id(ax) == grid[ax] - 1)

        @pl.when(first)
        def _():
            rider.start(r_ins, r_outs, r_scr)

        body(*ins, *outs, *scr)

        @pl.when(last)
        def _():
            rider.wait(r_ins, r_outs, r_scr)

    res = pl.pallas_call(
        carried, name=name,
        grid=grid,
        in_specs=list(in_specs) + rider.in_specs,
        out_specs=list(out_specs) + rider.out_specs,
        out_shape=list(out_shape) + rider.out_shapes,
        scratch_shapes=list(scratch_shapes) + rider.scratch,
        input_output_aliases={n_in + i: n_out + o for i, o in rider.aliases.items()},
        compiler_params=params)(*args, *_operands(rider))
    return res[:n_out], res[n_out:]


def _run_comm(comm, name):
    n_in, n_out = len(comm.arrays), len(comm.out_shapes)

    def body(*refs):
        ins, outs, scr = refs[:n_in], refs[n_in:n_in + n_out], refs[n_in + n_out:]
        comm.start(ins, outs, scr)
        comm.wait(ins, outs, scr)

    return pl.pallas_call(
        body, name=name,
        in_specs=comm.in_specs, out_specs=comm.out_specs, out_shape=comm.out_shapes,
        scratch_shapes=comm.scratch, input_output_aliases=comm.aliases,
        compiler_params=pltpu.CompilerParams(vmem_limit_bytes=24 * MIB))(*_operands(comm))


def _both(a, b):
    ni, no, ns = len(a.arrays), len(a.out_shapes), len(a.scratch)

    def start(ins, outs, scr):
        a.start(ins[:ni], outs[:no], scr[:ns])
        b.start(ins[ni:], outs[no:], scr[ns:])

    def wait(ins, outs, scr):
        a.wait(ins[:ni], outs[:no], scr[:ns])
        b.wait(ins[ni:], outs[no:], scr[ns:])

    aliases = dict(a.aliases)
    aliases.update({ni + i: no + o for i, o in b.aliases.items()})
    return _Comm(a.arrays + b.arrays, a.in_specs + b.in_specs, a.out_shapes + b.out_shapes,
                 a.out_specs + b.out_specs, a.scratch + b.scratch, start, wait, aliases)


def _place():
    return lax.axis_index("x"), lax.axis_index("y"), lax.axis_index("c")


def _peer(k):
    x, y, c = _place()
    px, py, pc = x ^ ((k >> 2) & 1), y ^ ((k >> 1) & 1), c ^ (k & 1)
    return (px, py, pc), 4 * px + 2 * py + pc


SIBLING = 1
SAME_CORE = (2, 4, 6)
EVERYONE = tuple(range(1, N_DEV))


def _gather_comm(shards, views, out_shapes, relations):
    na = len(shards)

    def copies(ins, outs, scr):
        send_sems, recv_sems, _ = scr
        _, me = _peer(0)
        out = []
        for a in range(na):
            for k in relations[a]:
                peer, theirs = _peer(k)
                send = functools.partial(
                    pltpu.make_async_remote_copy,
                    src_ref=ins[a], dst_ref=views[a](outs[a], me),
                    send_sem=send_sems.at[7 * a + k - 1], recv_sem=recv_sems.at[7 * a + k - 1],
                    device_id=peer, device_id_type=MESH)
                recv = functools.partial(
                    pltpu.make_async_remote_copy,
                    src_ref=ins[a], dst_ref=views[a](outs[a], theirs),
                    send_sem=send_sems.at[7 * a + k - 1], recv_sem=recv_sems.at[7 * a + k - 1],
                    device_id=peer, device_id_type=MESH)
                out.append((send, recv))
        return out

    def local(ins, outs, scr):
        _, me = _peer(0)
        return [pltpu.make_async_copy(ins[a], views[a](outs[a], me), scr[2].at[a]) for a in range(na)]

    def start(ins, outs, scr):
        for cp in local(ins, outs, scr):
            cp.start()
        for send, _ in copies(ins, outs, scr):
            send().start()

    def wait(ins, outs, scr):
        for _, recv in copies(ins, outs, scr):
            recv().wait_recv()
        for send, _ in copies(ins, outs, scr):
            send().wait_send()
        for cp in local(ins, outs, scr):
            cp.wait()

    return _Comm(shards, [HBM_SPEC] * na, out_shapes, [HBM_SPEC] * na,
                 [pltpu.SemaphoreType.DMA((7 * na,)), pltpu.SemaphoreType.DMA((7 * na,)),
                  pltpu.SemaphoreType.DMA((na,))], start, wait)


def _forward_comm(gathered, views):
    na = len(gathered)
    shapes = [jax.ShapeDtypeStruct(g.shape, g.dtype) for g in gathered]

    def copies(outs, scr):
        send_sems, recv_sems = scr
        sibling, _ = _peer(SIBLING)
        out = []
        for a in range(na):
            for n, k in enumerate(SAME_CORE):
                _, mine = _peer(k)
                _, theirs = _peer(k ^ SIBLING)
                send = functools.partial(
                    pltpu.make_async_remote_copy,
                    src_ref=views[a](outs[a], mine), dst_ref=views[a](outs[a], mine),
                    send_sem=send_sems.at[3 * a + n], recv_sem=recv_sems.at[3 * a + n],
                    device_id=sibling, device_id_type=MESH)
                recv = functools.partial(
                    pltpu.make_async_remote_copy,
                    src_ref=views[a](outs[a], mine), dst_ref=views[a](outs[a], theirs),
                    send_sem=send_sems.at[3 * a + n], recv_sem=recv_sems.at[3 * a + n],
                    device_id=sibling, device_id_type=MESH)
                out.append((send, recv))
        return out

    def start(ins, outs, scr):
        for send, _ in copies(outs, scr):
            send().start()

    def wait(ins, outs, scr):
        for _, recv in copies(outs, scr):
            recv().wait_recv()
        for send, _ in copies(outs, scr):
            send().wait_send()

    return _Comm(gathered, [HBM_SPEC] * na, shapes, [HBM_SPEC] * na,
                 [pltpu.SemaphoreType.DMA((3 * na,)), pltpu.SemaphoreType.DMA((3 * na,))], start, wait,
                 aliases={a: a for a in range(na)})


def _all_gather(shards, views, out_shapes, name):
    na = len(shards)
    level1 = _gather_comm(shards, views, out_shapes, [(SIBLING,) + SAME_CORE] * na)

    def body(*refs):
        ins, outs = refs[:na], refs[na:2 * na]
        send_sems, recv_sems, local_sems, fwd_send, fwd_recv = refs[2 * na:]
        sibling, _ = _peer(SIBLING)
        scr = (send_sems, recv_sems, local_sems)
        level1.start(ins, outs, scr)
        passed, landing = [], []
        for a in range(na):
            for n, k in enumerate(SAME_CORE):
                peer, mine = _peer(k)
                _, theirs = _peer(k ^ SIBLING)
                pltpu.make_async_remote_copy(
                    src_ref=ins[a], dst_ref=views[a](outs[a], mine),
                    send_sem=send_sems.at[7 * a + k - 1], recv_sem=recv_sems.at[7 * a + k - 1],
                    device_id=peer, device_id_type=MESH).wait_recv()
                fwd = pltpu.make_async_remote_copy(
                    src_ref=views[a](outs[a], mine), dst_ref=views[a](outs[a], mine),
                    send_sem=fwd_send.at[3 * a + n], recv_sem=fwd_recv.at[3 * a + n],
                    device_id=sibling, device_id_type=MESH)
                fwd.start()
                passed.append(fwd)
                landing.append(pltpu.make_async_remote_copy(
                    src_ref=views[a](outs[a], mine), dst_ref=views[a](outs[a], theirs),
                    send_sem=fwd_send.at[3 * a + n], recv_sem=fwd_recv.at[3 * a + n],
                    device_id=sibling, device_id_type=MESH))
        for a in range(na):
            _, theirs = _peer(SIBLING)
            pltpu.make_async_remote_copy(
                src_ref=ins[a], dst_ref=views[a](outs[a], theirs),
                send_sem=send_sems.at[7 * a + SIBLING - 1], recv_sem=recv_sems.at[7 * a + SIBLING - 1],
                device_id=sibling, device_id_type=MESH).wait_recv()
        for cp in landing:
            cp.wait_recv()
        for cp in passed:
            cp.wait_send()
        _, me = _peer(0)
        for a in range(na):
            for k in (SIBLING,) + SAME_CORE:
                peer, _ = _peer(k)
                pltpu.make_async_remote_copy(
                    src_ref=ins[a], dst_ref=views[a](outs[a], me),
                    send_sem=send_sems.at[7 * a + k - 1], recv_sem=recv_sems.at[7 * a + k - 1],
                    device_id=peer, device_id_type=MESH).wait_send()
            pltpu.make_async_copy(ins[a], views[a](outs[a], me), local_sems.at[a]).wait()

    return pl.pallas_call(
        body, name=name,
        in_specs=[HBM_SPEC] * na, out_specs=[HBM_SPEC] * na, out_shape=out_shapes,
        scratch_shapes=level1.scratch + [pltpu.SemaphoreType.DMA((3 * na,)), pltpu.SemaphoreType.DMA((3 * na,))],
    )(*[_in_hbm(s) for s in shards])


def _sibling_comm(grads):
    na = len(grads)
    shapes = [jax.ShapeDtypeStruct(g.shape[:2] + g.shape[3:], g.dtype) for g in grads]

    def copies(ins, outs, scr):
        x, y, c = _place()
        return [pltpu.make_async_remote_copy(
            src_ref=ins[a].at[:, :, 1 - c], dst_ref=outs[a],
            send_sem=scr[0].at[a], recv_sem=scr[1].at[a],
            device_id=(x, y, 1 - c), device_id_type=MESH) for a in range(na)]

    def start(ins, outs, scr):
        for cp in copies(ins, outs, scr):
            cp.start()

    def wait(ins, outs, scr):
        for cp in copies(ins, outs, scr):
            cp.wait()

    return _Comm(grads, [HBM_SPEC] * na, shapes, [HBM_SPEC] * na,
                 [pltpu.SemaphoreType.DMA((na,)), pltpu.SemaphoreType.DMA((na,))], start, wait)


def _chips_comm(sums):
    na = len(sums)
    shapes = [jax.ShapeDtypeStruct(s.shape, s.dtype) for s in sums]

    def copies(ins, outs, scr):
        x, y, c = _place()
        mine = 2 * x + y
        out = []
        for a in range(na):
            for n, k in enumerate(SAME_CORE):
                (px, py, pc), _ = _peer(k)
                theirs = 2 * px + py
                send = functools.partial(
                    pltpu.make_async_remote_copy,
                    src_ref=ins[a].at[:, theirs], dst_ref=outs[a].at[:, mine],
                    send_sem=scr[0].at[3 * a + n], recv_sem=scr[1].at[3 * a + n],
                    device_id=(px, py, pc), device_id_type=MESH)
                recv = functools.partial(
                    pltpu.make_async_remote_copy,
                    src_ref=ins[a].at[:, mine], dst_ref=outs[a].at[:, theirs],
                    send_sem=scr[0].at[3 * a + n], recv_sem=scr[1].at[3 * a + n],
                    device_id=(px, py, pc), device_id_type=MESH)
                out.append((send, recv))
        return out

    def local(ins, outs, scr):
        x, y, _ = _place()
        mine = 2 * x + y
        return [pltpu.make_async_copy(ins[a].at[:, mine], outs[a].at[:, mine], scr[2].at[a]) for a in range(na)]

    def start(ins, outs, scr):
        for cp in local(ins, outs, scr):
            cp.start()
        for send, _ in copies(ins, outs, scr):
            send().start()

    def wait(ins, outs, scr):
        for _, recv in copies(ins, outs, scr):
            recv().wait_recv()
        for send, _ in copies(ins, outs, scr):
            send().wait_send()
        for cp in local(ins, outs, scr):
            cp.wait()

    return _Comm(sums, [HBM_SPEC] * na, shapes, [HBM_SPEC] * na,
                 [pltpu.SemaphoreType.DMA((3 * na,)), pltpu.SemaphoreType.DMA((3 * na,)),
                  pltpu.SemaphoreType.DMA((na,))], start, wait)


def _small_sum_comm(s):
    R, C = s.shape

    def copies(ins, scr):
        buf, send_sems, recv_sems = scr
        _, me = _peer(0)
        out = []
        for k in EVERYONE:
            peer, theirs = _peer(k)
            send = functools.partial(
                    pltpu.make_async_remote_copy,
                src_ref=ins[0], dst_ref=buf.at[me], send_sem=send_sems.at[k - 1], recv_sem=recv_sems.at[k - 1],
                device_id=peer, device_id_type=MESH)
            recv = functools.partial(
                    pltpu.make_async_remote_copy,
                src_ref=ins[0], dst_ref=buf.at[theirs], send_sem=send_sems.at[k - 1], recv_sem=recv_sems.at[k - 1],
                device_id=peer, device_id_type=MESH)
            out.append((send, recv))
        return out

    def start(ins, outs, scr):
        _, me = _peer(0)
        scr[0][me] = ins[0][...]
        for send, _ in copies(ins, scr):
            send().start()

    def wait(ins, outs, scr):
        for _, recv in copies(ins, scr):
            recv().wait_recv()
        for send, _ in copies(ins, scr):
            send().wait_send()
        acc = scr[0][0]
        for j in range(1, N_DEV):
            acc = acc + scr[0][j]
        outs[0][...] = acc

    return _Comm([s], [VMEM_SPEC], [jax.ShapeDtypeStruct((R, C), F32)], [VMEM_SPEC],
                 [pltpu.VMEM((N_DEV, R, C), F32), pltpu.SemaphoreType.DMA((N_DEV - 1,)),
                  pltpu.SemaphoreType.DMA((N_DEV - 1,))], start, wait)


def _prep_weights(ffn1, ffn2, w_in, w_out, name):
    rf, D = ffn1[2].shape
    ri, ro = w_in.shape[1], w_out.shape[0]

    def body(g1, u1, d1, g2, u2, d2, wi, wo, p1_ref, p2_ref, pi_ref, po_ref):
        for p_ref, shards in ((p1_ref, (g1, u1, d1)), (p2_ref, (g2, u2, d2))):
            for k, shard in enumerate(shards):
                p_ref[k] = shard[...].astype(BF16)
        pi_ref[...] = wi[...].T.astype(BF16)
        po_ref[...] = wo[...].astype(BF16)

    args = (*ffn1, *ffn2, w_in, w_out)
    whole = lambda shape: pl.BlockSpec(shape, lambda i: (0,) * len(shape))
    out_shapes = [(3, rf, D), (3, rf, D), (ri, D), (ro, D)]
    return _call(
        body, name=name, grid=(1,),
        in_specs=[whole(a.shape) for a in args], out_specs=[whole(s) for s in out_shapes],
        out_shape=[jax.ShapeDtypeStruct(s, BF16) for s in out_shapes],
        scratch_shapes=[], vmem_mib=48, args=args)


def _load_weights(w_hbm, w_vmem, sem):
    @pl.when(pl.program_id(0) == 0)
    def _():
        copies = [pltpu.make_async_copy(w_hbm.at[k], w_vmem.at[k], sem.at[k]) for k in range(3)]
        for cp in copies:
            cp.start()
        for cp in copies:
            cp.wait()


def _ffn_fwd(x, g, w3, tm, cf, name, rider=None, head=None):
    T, D = x.shape
    F = w3.shape[1]
    n_head = 0 if head is None else 2

    def body(x_ref, g_ref, w_hbm, *refs):
        head_refs, refs = refs[:n_head], refs[n_head:]
        if head is None:
            (xo_ref, h_ref, dau_ref, dag_ref, act_ref, wv, sem) = refs
        else:
            (dx_ref, dob_ref, dgf_ref, loss_ref, h_ref, dau_ref, dag_ref, act_ref, wv, sem) = refs
        _load_weights(w_hbm, wv, sem)
        xhat, _ = _rms_stats(x_ref[...])
        hb = (xhat * g_ref[...]).astype(BF16)
        h_ref[...] = hb
        for lo in range(0, F, cf):
            gate = _dot(hb, wv[0, lo:lo + cf, :], NT)
            up = _dot(hb, wv[1, lo:lo + cf, :], NT)
            sig = _sigmoid(gate)
            silu = gate * sig
            dau_ref[:, lo:lo + cf] = silu.astype(BF16)
            dag_ref[:, lo:lo + cf] = (up * (sig * (1.0 + gate * (1.0 - sig)))).astype(BF16)
            act_ref[:, lo:lo + cf] = (silu * up).astype(BF16)
        x_out = x_ref[...] + FFN_RES * _dot(act_ref[...], wv[2], NN)
        if head is None:
            xo_ref[...] = x_out
            return

        @pl.when(pl.program_id(0) == 0)
        def _():
            dgf_ref[...] = jnp.zeros_like(dgf_ref)
            loss_ref[...] = jnp.zeros_like(loss_ref)

        gf_ref, tgt_ref = head_refs
        yhat, rstd = _rms_stats(x_out)
        gf = gf_ref[...]
        err = yhat * gf - tgt_ref[...]
        loss_ref[...] += (0.5 / D) * jnp.sum(err * err)
        dx, dgf = _rms_bwd(yhat, rstd, gf, err * (1.0 / D))
        dx_ref[...] = dx
        dob_ref[...] = (FFN_RES * dx).astype(BF16)
        dgf_ref[...] += dgf

    row = pl.BlockSpec((tm, D), lambda i: (i, 0))
    hid = pl.BlockSpec((tm, F), lambda i: (i, 0))
    vec = pl.BlockSpec((1, D), lambda i: (0, 0))
    row_f32, row_bf16 = jax.ShapeDtypeStruct((T, D), F32), jax.ShapeDtypeStruct((T, D), BF16)
    if head is None:
        first_specs, first_shapes = [row], [row_f32]
    else:
        first_specs = [row, row, vec, pl.BlockSpec((1, 128), lambda i: (0, 0))]
        first_shapes = [row_f32, row_bf16, jax.ShapeDtypeStruct((1, D), F32), jax.ShapeDtypeStruct((1, 128), F32)]
    return _call(
        body, name=name, grid=(T // tm,),
        in_specs=[row, vec, HBM_SPEC] + ([] if head is None else [vec, row]),
        out_specs=first_specs + [row, hid, hid, hid],
        out_shape=first_shapes + [row_bf16] + [jax.ShapeDtypeStruct((T, F), BF16)] * 3,
        scratch_shapes=[pltpu.VMEM((3, F, D), BF16), pltpu.SemaphoreType.DMA((3,))],
        vmem_mib=60, args=(x, g, w3) + (() if head is None else tuple(head)), rider=rider)


def _ffn_dgrad(dout, x, g, dau, dag, w3, tm, cf, name, rider=None):
    T, D = x.shape
    F = w3.shape[1]

    def body(do_ref, x_ref, g_ref, dau_ref, dag_ref, w_hbm, dx_ref, dgate_ref, dup_ref, dg_ref, wv, sem):
        _load_weights(w_hbm, wv, sem)

        @pl.when(pl.program_id(0) == 0)
        def _():
            dg_ref[...] = jnp.zeros_like(dg_ref)

        dob = (FFN_RES * do_ref[...]).astype(BF16)
        for lo in range(0, F, cf):
            dact = _dot(dob, wv[2, lo:lo + cf, :], NT)
            dup_ref[:, lo:lo + cf] = (dact * dau_ref[:, lo:lo + cf].astype(F32)).astype(BF16)
            dgate_ref[:, lo:lo + cf] = (dact * dag_ref[:, lo:lo + cf].astype(F32)).astype(BF16)
        dh = _dot(dgate_ref[...], wv[0], NN) + _dot(dup_ref[...], wv[1], NN)
        xhat, rstd = _rms_stats(x_ref[...])
        dx, dg = _rms_bwd(xhat, rstd, g_ref[...], dh)
        dx_ref[...] = do_ref[...] + dx
        dg_ref[...] += dg

    row = pl.BlockSpec((tm, D), lambda i: (i, 0))
    hid = pl.BlockSpec((tm, F), lambda i: (i, 0))
    vec = pl.BlockSpec((1, D), lambda i: (0, 0))
    return _call(
        body, name=name, grid=(T // tm,),
        in_specs=[row, row, vec, hid, hid, HBM_SPEC],
        out_specs=[row, hid, hid, vec],
        out_shape=[jax.ShapeDtypeStruct((T, D), F32), jax.ShapeDtypeStruct((T, F), BF16),
                   jax.ShapeDtypeStruct((T, F), BF16), jax.ShapeDtypeStruct((1, D), F32)],
        scratch_shapes=[pltpu.VMEM((3, F, D), BF16), pltpu.SemaphoreType.DMA((3,))],
        vmem_mib=52, args=(dout, x, g, dau, dag, w3), rider=rider)


def _wgrad(lhs, rhs, tm, tf, name, rider=None):
    T, F = lhs[0].shape
    D = rhs.shape[1]
    K = len(lhs)

    def body(*refs):
        lhs_refs, rhs_ref, dw_ref, accs = refs[:K], refs[K], refs[K + 1], refs[K + 2:]
        i = pl.program_id(1)

        @pl.when(i == 0)
        def _():
            for acc in accs:
                acc[...] = jnp.zeros_like(acc)

        rv = rhs_ref[...]
        for acc, lhs_ref in zip(accs, lhs_refs):
            acc[...] += _dot(lhs_ref[...], rv, TN)

        @pl.when(i == pl.num_programs(1) - 1)
        def _():
            for k, acc in enumerate(accs):
                dw_ref[k] = acc[...].astype(BF16)

    hid = pl.BlockSpec((tm, tf), lambda f, i: (i, f))
    return _call(
        body, name=name, grid=(F // tf, T // tm),
        in_specs=[hid] * K + [pl.BlockSpec((tm, D), lambda f, i: (i, 0))],
        out_specs=[pl.BlockSpec((K, tf, D), lambda f, i: (0, f, 0))],
        out_shape=[jax.ShapeDtypeStruct((K, F, D), BF16)],
        scratch_shapes=[pltpu.VMEM((tf, D), F32)] * K,
        vmem_mib=56, args=(*lhs, rhs), rider=rider)


def _lru_gates(xr, bda_ref, bdx_ref, vec_ref):
    xrb = xr.astype(BF16)
    r = _sigmoid(_dot(xrb, bda_ref[...], NN) + vec_ref[V_BA:V_BA + 1, :])
    ig = _sigmoid(_dot(xrb, bdx_ref[...], NN) + vec_ref[V_BX:V_BX + 1, :])
    sp = _softplus_neg(vec_ref[V_LAM:V_LAM + 1, :])
    log_a = (-LRU_C * sp) * r
    a = jnp.exp(log_a)
    mult = jnp.sqrt(_neg_expm1(2.0 * log_a))
    return xrb, r, ig, sp, a, mult


def _layernorm_stats(u1):
    xc = u1 - jnp.mean(u1, axis=-1, keepdims=True)
    rs = lax.rsqrt(jnp.mean(xc * xc, axis=-1, keepdims=True) + LN_EPS)
    return xc * rs, rs


def _mix_core_fwd(x1, g, w_in_t, w_out, bda, bdx, cw, lw, vec, tm, name, rider=None):
    T, D = x1.shape
    W = cw.shape[1]
    assert tm >= CONV_HALO and w_in_t.shape[0] == 4 * W

    def body(x1_ref, g_ref, wi_ref, wo_ref, bda_ref, bdx_ref, cw_ref, lw_ref, vec_ref,
             x2_ref, z_ref, mix_ref, u1_ref, xr_ref, hst_ref, ubuf, rbuf, hc):
        @pl.when(pl.program_id(0) == 0)
        def _():
            ubuf[0:CONV_HALO, :] = jnp.zeros((CONV_HALO, W), F32)
            rbuf[0:LRU_HALO, :] = jnp.zeros((LRU_HALO, W), F32)
            hc[...] = jnp.zeros_like(hc)

        xhat, _ = _rms_stats(x1_ref[...])
        z_ref[...] = _dot((xhat * g_ref[...]).astype(BF16), wi_ref[...], NT)

        ubuf[CONV_HALO:CONV_HALO + tm, :] = z_ref[:, 0:W] * _sigmoid(z_ref[:, W:2 * W])
        u1 = jnp.zeros((tm, W), F32) + vec_ref[V_CB:V_CB + 1, :]
        base = CONV_HALO - (CONV_K - 1)
        for off, win in _row_windows(ubuf, tm, range(base, base + CONV_K)):
            u1 = u1 + cw_ref[off - base:off - base + 1, :] * win
        ubuf[0:CONV_HALO, :] = ubuf[tm:tm + CONV_HALO, :]
        u1_ref[...] = u1
        xh, _ = _layernorm_stats(u1)
        u2 = xh * vec_ref[V_LNG:V_LNG + 1, :] + vec_ref[V_LNB:V_LNB + 1, :]
        ub = (u2 * _sigmoid(u2)).astype(BF16)
        mix_ref[:, 0:W] = ub

        rbuf[LRU_HALO:LRU_HALO + tm, :] = z_ref[:, 2 * W:3 * W]
        xr = jnp.zeros((tm, W), F32) + vec_ref[V_LCB:V_LCB + 1, :]
        for k in range(LRU_K):
            off = LRU_HALO - (LRU_K - 1) + k
            xr = xr + lw_ref[k:k + 1, :] * rbuf[off:off + tm, :]
        rbuf[0:LRU_HALO, :] = rbuf[tm:tm + LRU_HALO, :]
        xr_ref[...] = xr
        _, _, ig, _, a, mult = _lru_gates(xr, bda_ref, bdx_ref, vec_ref)
        hc[0:1, :] = _scan_rows(a, mult * (ig * xr), hc[0:1, :], hst_ref)
        gl, _ = _gelu_parts(z_ref[:, 3 * W:4 * W])
        yb = (hst_ref[...] * gl).astype(BF16)
        mix_ref[:, W:2 * W] = yb

        x2_ref[...] = x1_ref[...] + _dot(ub, wo_ref[0:W, :], NN) + _dot(yb, wo_ref[W:2 * W, :], NN)

    full = lambda a: pl.BlockSpec(a.shape, lambda i: (0,) * a.ndim)
    tile = lambda n: pl.BlockSpec((tm, n), lambda i: (i, 0))
    return _call(
        body, name=name, grid=(T // tm,),
        in_specs=[tile(D), full(g), full(w_in_t), full(w_out), full(bda), full(bdx), full(cw), full(lw), full(vec)],
        out_specs=[tile(D), tile(4 * W), tile(2 * W), tile(W), tile(W), tile(W)],
        out_shape=[jax.ShapeDtypeStruct((T, D), F32), jax.ShapeDtypeStruct((T, 4 * W), F32),
                   jax.ShapeDtypeStruct((T, 2 * W), BF16), jax.ShapeDtypeStruct((T, W), F32),
                   jax.ShapeDtypeStruct((T, W), F32), jax.ShapeDtypeStruct((T, W), F32)],
        scratch_shapes=[pltpu.VMEM((tm + CONV_HALO, W), F32), pltpu.VMEM((tm + LRU_HALO, W), F32),
                        pltpu.VMEM((8, W), F32)],
        vmem_mib=56, args=(x1, g, w_in_t, w_out, bda, bdx, cw, lw, vec), rider=rider)


def _mix_bwd(dx2, z, u1, xr, hst, x1, mix, g, w_in_t, w_out, bda, bdx, cw, lw, vec, tm, name, rider=None):
    T, D = dx2.shape
    W = cw.shape[1]
    nt = T // tm
    assert tm >= CONV_HALO and tm % CONV_HALO == 0

    def body(dx_ref, z_ref, zh_ref, u1_ref, xr_ref, h_ref, hh_ref, wo_ref, bda_ref, bdx_ref, cw_ref, lw_ref, vec_ref,
             x1_ref, mix_ref, g_ref, wi_ref,
             dx1_ref, sg_ref, dbda_ref, dbdx_ref, dob_ref, dg_ref, dwi_ref, dwo_ref,
             u0buf, du1buf, rxbuf, dxrbuf, gbuf, gc, spacc, dz_ref, ai_ref, ao_ref):
        i = pl.program_id(0)
        first = i == nt - 1
        row = lax.broadcasted_iota(jnp.int32, (tm, W), 0)

        @pl.when(i == 0)
        def _():
            sg_ref[...] = jnp.zeros_like(sg_ref)
            dbda_ref[...] = jnp.zeros_like(dbda_ref)
            dbdx_ref[...] = jnp.zeros_like(dbdx_ref)
            du1buf[tm:tm + CONV_HALO, :] = jnp.zeros((CONV_HALO, W), F32)
            dxrbuf[tm:tm + LRU_HALO, :] = jnp.zeros((LRU_HALO, W), F32)
            gc[...] = jnp.zeros_like(gc)
            spacc[...] = jnp.zeros_like(spacc)
            dg_ref[...] = jnp.zeros_like(dg_ref)
            ai_ref[...] = jnp.zeros_like(ai_ref)
            ao_ref[...] = jnp.zeros_like(ao_ref)

        def accum(r, val):
            sg_ref[r:r + 1, :] += jnp.sum(val, axis=0, keepdims=True)

        x1hat, x1rstd = _rms_stats(x1_ref[...])
        gain = g_ref[...]
        hb = (x1hat * gain).astype(BF16)

        def in_proj_bwd(lo, hi):
            dzb = dz_ref[:, lo:hi]
            ai_ref[lo:hi, :] += _dot(dzb, hb, TN)
            return _dot(dzb, wi_ref[lo:hi, :], NN)

        dxb = dx_ref[...].astype(BF16)
        ao_ref[...] += _dot(mix_ref[...], dxb, TN)
        dmix = _dot(dxb, wo_ref[...], NT)
        d_u = dmix[:, 0:W]
        d_yr = dmix[:, W:2 * W]

        xh, rs = _layernorm_stats(u1_ref[...])
        ln_g = vec_ref[V_LNG:V_LNG + 1, :]
        u2 = xh * ln_g + vec_ref[V_LNB:V_LNB + 1, :]
        s2 = _sigmoid(u2)
        d_u2 = d_u * (s2 * (1.0 + u2 * (1.0 - s2)))
        accum(G_LNG, d_u2 * xh)
        accum(G_LNB, d_u2)
        d_xh = d_u2 * ln_g
        d_u1 = rs * (d_xh - jnp.mean(d_xh, axis=-1, keepdims=True)
                     - xh * jnp.mean(d_xh * xh, axis=-1, keepdims=True))
        accum(G_CB, d_u1)
        halo_on = jnp.where(first, 0.0, 1.0)
        u0buf[0:CONV_HALO, :] = halo_on * (zh_ref[:, 0:W] * _sigmoid(zh_ref[:, W:2 * W]))
        cv = z_ref[:, 0:W]
        sgc = _sigmoid(z_ref[:, W:2 * W])
        u0buf[CONV_HALO:CONV_HALO + tm, :] = cv * sgc
        du1buf[0:tm, :] = d_u1
        base = CONV_HALO - (CONV_K - 1)
        for off, win in _row_windows(u0buf, tm, range(base, base + CONV_K)):
            accum(G_CW + off - base, d_u1 * win)
        d_u0 = jnp.zeros((tm, W), F32)
        for off, win in _row_windows(du1buf, tm, range(0, CONV_K)):
            d_u0 = d_u0 + cw_ref[CONV_K - 1 - off:CONV_K - off, :] * win
        du1buf[tm:tm + CONV_HALO, :] = du1buf[0:CONV_HALO, :]
        dz_ref[:, 0:W] = (d_u0 * sgc).astype(BF16)
        dz_ref[:, W:2 * W] = (d_u0 * cv * (sgc * (1.0 - sgc))).astype(BF16)
        dh = in_proj_bwd(0, 2 * W)

        xrv = xr_ref[...]
        xrb, r, ig, sp, a, mult = _lru_gates(xrv, bda_ref, bdx_ref, vec_ref)
        h = h_ref[...]
        gl, dgl = _gelu_parts(z_ref[:, 3 * W:4 * W])
        dz_ref[:, 3 * W:4 * W] = (d_yr * h * dgl).astype(BF16)
        dh = dh + in_proj_bwd(3 * W, 4 * W)
        a_next = jnp.where(row == tm - 1, 1.0, pltpu.roll(a, tm - 1, 0))
        g_first = _scan_rows(a_next, d_yr * gl, gc[0:1, :], gbuf, reverse=True)
        g = gbuf[...]
        gc[0:1, :] = a[0:1, :] * g_first
        hprev = jnp.where(row == 0, halo_on * hh_ref[LRU_HALO - 1:LRU_HALO, :], pltpu.roll(h, 1, 0))
        d_log_a = (g * hprev) * a - (g * ig * xrv) * (a * a) / mult
        d_ig = g * mult * xrv
        d_xr = g * mult * ig
        spacc[0:1, :] += jnp.sum(d_log_a * r, axis=0, keepdims=True)
        d_pa32 = (d_log_a * (-LRU_C * sp)) * (r * (1.0 - r))
        d_px32 = d_ig * (ig * (1.0 - ig))
        accum(G_BA, d_pa32)
        accum(G_BX, d_px32)
        d_pa = d_pa32.astype(BF16)
        d_px = d_px32.astype(BF16)
        d_xr = d_xr + _dot(d_pa, bda_ref[...], NT) + _dot(d_px, bdx_ref[...], NT)
        dbda_ref[...] += _dot(xrb, d_pa, TN)
        dbdx_ref[...] += _dot(xrb, d_px, TN)
        accum(G_LCB, d_xr)
        rxbuf[0:LRU_HALO, :] = halo_on * zh_ref[CONV_HALO - LRU_HALO:CONV_HALO, 2 * W:3 * W]
        rxbuf[LRU_HALO:LRU_HALO + tm, :] = z_ref[:, 2 * W:3 * W]
        dxrbuf[0:tm, :] = d_xr
        d_rx = jnp.zeros((tm, W), F32)
        for k in range(LRU_K):
            off = LRU_HALO - (LRU_K - 1) + k
            accum(G_LW + k, d_xr * rxbuf[off:off + tm, :])
            d_rx = d_rx + lw_ref[k:k + 1, :] * dxrbuf[LRU_K - 1 - k:LRU_K - 1 - k + tm, :]
        dxrbuf[tm:tm + LRU_HALO, :] = dxrbuf[0:LRU_HALO, :]
        dz_ref[:, 2 * W:3 * W] = d_rx.astype(BF16)
        dh = dh + in_proj_bwd(2 * W, 3 * W)

        dx, dg = _rms_bwd(x1hat, x1rstd, gain, dh)
        dx1 = dx_ref[...] + dx
        dx1_ref[...] = dx1
        dob_ref[...] = (FFN_RES * dx1).astype(BF16)
        dg_ref[...] += dg

        @pl.when(first)
        def _():
            lam = vec_ref[V_LAM:V_LAM + 1, :]
            sg_ref[G_LAM:G_LAM + 1, :] = LRU_C * _sigmoid(-lam) * spacc[0:1, :]
            dwi_ref[...] = ai_ref[...].astype(BF16)
            dwo_ref[...] = ao_ref[...].astype(BF16)

    full = lambda a: pl.BlockSpec(a.shape, lambda i: (0,) * a.ndim)
    tile = lambda n: pl.BlockSpec((tm, n), lambda i: (nt - 1 - i, 0))
    halo = lambda rows, n: pl.BlockSpec(
        (rows, n), lambda i: (jnp.maximum((nt - 1 - i) * (tm // rows) - 1, 0), 0))
    const = lambda r, c: pl.BlockSpec((r, c), lambda i: (0, 0))
    return _call(
        body, name=name, grid=(nt,),
        in_specs=[tile(D), tile(4 * W), halo(CONV_HALO, 4 * W), tile(W), tile(W), tile(W), halo(LRU_HALO, W),
                  full(w_out), full(bda), full(bdx), full(cw), full(lw), full(vec),
                  tile(D), tile(2 * W), full(g), full(w_in_t)],
        out_specs=[tile(D), const(G_ROWS, W), const(W, W), const(W, W),
                   tile(D), const(1, D), const(4 * W, D), const(2 * W, D)],
        out_shape=[jax.ShapeDtypeStruct((T, D), F32), jax.ShapeDtypeStruct((G_ROWS, W), F32),
                   jax.ShapeDtypeStruct((W, W), F32), jax.ShapeDtypeStruct((W, W), F32),
                   jax.ShapeDtypeStruct((T, D), BF16), jax.ShapeDtypeStruct((1, D), F32),
                   jax.ShapeDtypeStruct((4 * W, D), BF16), jax.ShapeDtypeStruct((2 * W, D), BF16)],
        scratch_shapes=[pltpu.VMEM((tm + CONV_HALO, W), F32), pltpu.VMEM((tm + CONV_HALO, W), F32),
                        pltpu.VMEM((tm + LRU_HALO, W), F32), pltpu.VMEM((tm + LRU_HALO, W), F32),
                        pltpu.VMEM((tm, W), F32), pltpu.VMEM((8, W), F32), pltpu.VMEM((8, W), F32),
                        pltpu.VMEM((tm, 4 * W), BF16), pltpu.VMEM((4 * W, D), F32), pltpu.VMEM((2 * W, D), F32)],
        vmem_mib=60, args=(dx2, z, z, u1, xr, hst, hst, w_out, bda, bdx, cw, lw, vec, x1, mix, g, w_in_t),
        rider=rider)


def _pair_add(full, recv, name):
    K, _, _, rows, D = full.shape

    def body(c_ref, a_ref, b_ref, o_ref):
        o_ref[...] = (a_ref[...].astype(F32) + b_ref[...].astype(F32)).astype(BF16)

    c = lax.axis_index("c").astype(jnp.int32).reshape((1,))
    return _call(
        body, name=name, grid=(K, N_CHIP), num_scalar_prefetch=1,
        in_specs=[pl.BlockSpec((None, None, None, rows, D), lambda k, q, c_ref: (k, q, c_ref[0], 0, 0)),
                  pl.BlockSpec((None, None, rows, D), lambda k, q, c_ref: (k, q, 0, 0))],
        out_specs=pl.BlockSpec((None, None, rows, D), lambda k, q, c_ref: (k, q, 0, 0)),
        out_shape=jax.ShapeDtypeStruct(recv.shape, BF16),
        scratch_shapes=[], vmem_mib=16, args=(c, full, recv))


def _adamw_update(wv, gv, mv, vv):
    m2 = ADAM_B1 * mv + (1.0 - ADAM_B1) * gv
    v2 = ADAM_B2 * vv + (1.0 - ADAM_B2) * (gv * gv)
    m_hat = m2 / (1.0 - ADAM_B1 ** ADAM_STEP)
    v_hat = v2 / (1.0 - ADAM_B2 ** ADAM_STEP)
    return -ADAM_LR * (m_hat / (jnp.sqrt(v_hat) + ADAM_EPS) + ADAM_WD * wv), m2, v2


def _finish(parts, k, w, m, v, transpose, name):
    _, _, rows, D = parts.shape

    def body(p_ref, w_ref, m_ref, v_ref, g_ref, d_ref, mo_ref, vo_ref):
        acc = p_ref[0].astype(F32)
        for q in range(1, N_CHIP):
            acc = acc + p_ref[q].astype(F32)
        gv = acc.T if transpose else acc
        g_ref[...] = gv
        d_ref[...], mo_ref[...], vo_ref[...] = _adamw_update(w_ref[...], gv, m_ref[...], v_ref[...])

    whole = pl.BlockSpec(w.shape, lambda i: (0, 0))
    return _call(
        body, name=name, grid=(1,),
        in_specs=[pl.BlockSpec((None, N_CHIP, rows, D), lambda i: (k, 0, 0, 0)), whole, whole, whole],
        out_specs=[whole] * 4, out_shape=[pltpu.HBM(w.shape, F32)] * 4,
        scratch_shapes=[], vmem_mib=40, args=(parts, w, m, v))


def _adamw(w, g, m, v, name):
    def body(w_ref, g_ref, m_ref, v_ref, d_ref, mo_ref, vo_ref):
        d_ref[...], mo_ref[...], vo_ref[...] = _adamw_update(w_ref[...], g_ref[...], m_ref[...], v_ref[...])

    shape = jax.ShapeDtypeStruct(w.shape, F32)
    return pl.pallas_call(
        body, name=name,
        in_specs=[VMEM_SPEC] * 4, out_specs=[VMEM_SPEC] * 3, out_shape=[shape] * 3,
        compiler_params=pltpu.CompilerParams(vmem_limit_bytes=32 * MIB),
    )(w, g, m, v)


def _block_diag(w):
    h, d, _ = w.shape
    onto = jnp.eye(h, dtype=w.dtype)
    return (w[:, :, None, :] * onto[:, None, :, None]).reshape(h * d, h * d)


def _diag_blocks(m, h):
    d = m.shape[0] // h
    return jnp.stack([m[k * d:(k + 1) * d, k * d:(k + 1) * d] for k in range(h)])


def _reduce_level1(full, tag):
    got = _run_comm(_sibling_comm(full), "rs_sibling_" + tag)
    return [_pair_add(a, b, "rs_pair_add_%s%d" % (tag, n)) for n, (a, b) in enumerate(zip(full, got))]


def kernel(x, ffn1_norm, ffn1_w_gate, ffn1_w_up, ffn1_w_down, mix_norm, w_in, conv_dw, conv_dw_bias, conv_ln_g, conv_ln_b, lru_conv_w, lru_conv_b, lru_w_a, lru_b_a, lru_w_x, lru_b_x, lru_lambda, w_out, ffn2_norm, ffn2_w_gate, ffn2_w_up, ffn2_w_down, final_norm, loss_target, m_ffn1_norm, m_ffn1_w_gate, m_ffn1_w_up, m_ffn1_w_down, m_mix_norm, m_w_in, m_conv_dw, m_conv_dw_bias, m_conv_ln_g, m_conv_ln_b, m_lru_conv_w, m_lru_conv_b, m_lru_w_a, m_lru_b_a, m_lru_w_x, m_lru_b_x, m_lru_lambda, m_w_out, m_ffn2_norm, m_ffn2_w_gate, m_ffn2_w_up, m_ffn2_w_down, m_final_norm, v_ffn1_norm, v_ffn1_w_gate, v_ffn1_w_up, v_ffn1_w_down, v_mix_norm, v_w_in, v_conv_dw, v_conv_dw_bias, v_conv_ln_g, v_conv_ln_b, v_lru_conv_w, v_lru_conv_b, v_lru_w_a, v_lru_b_a, v_lru_w_x, v_lru_b_x, v_lru_lambda, v_w_out, v_ffn2_norm, v_ffn2_w_gate, v_ffn2_w_up, v_ffn2_w_down, v_final_norm):
    T, D = x.shape[1], x.shape[2]
    F = ffn1_w_down.shape[0] * N_DEV
    rf = ffn1_w_down.shape[0]
    ri = w_in.shape[1]
    ro = w_out.shape[0]
    W = conv_dw_bias.shape[0]
    wc = conv_dw.shape[1]
    H = lru_w_a.shape[0]
    xs = x.reshape(T, D)
    tgt = loss_target.reshape(T, D)
    tm_ffn = min(256, T)
    tm_fwd = min(512, T)
    cf = 256
    tm_w = min(1024, T)
    tm_w1 = min(2048, T)
    tm_mix = min(256, T)
    tf_w = F // 2
    row = lambda v: v.reshape(1, -1)
    by_owner = lambda a, rows: a.reshape(a.shape[0], N_CHIP, 2, rows, D)

    p3a, p3b, p_in, p_out = _prep_weights(
        (ffn1_w_gate.T, ffn1_w_up.T, ffn1_w_down), (ffn2_w_gate.T, ffn2_w_up.T, ffn2_w_down), w_in, w_out,
        "prep_weights")
    tile_rows = lambda a: jnp.pad(a, ((0, -a.shape[0] % SUBLANES), (0, 0)))
    p_cw = jnp.concatenate([tile_rows(conv_dw), tile_rows(lru_conv_w)], axis=0)
    lw_row = p_cw.shape[0] - SUBLANES
    stacked = lambda r, j: r.at[:, j]
    plain = lambda r, j: r.at[j]
    g3_shape = jax.ShapeDtypeStruct((3, N_DEV, rf, D), BF16)
    (g3a,) = _all_gather([p3a], [stacked], [g3_shape], "ag_ffn1")
    w3a = g3a.reshape(3, F, D)
    bda = _block_diag(lru_w_a).astype(BF16)
    bdx = _block_diag(lru_w_x).astype(BF16)
    vec = jnp.concatenate([tile_rows(v[None]) for v in
                           (conv_dw_bias, conv_ln_g, conv_ln_b, lru_conv_b, lru_b_a, lru_b_x, lru_lambda)], axis=0)

    gather_rest = _gather_comm(
        [p3b, p_in, p_out, p_cw], [stacked, plain, plain, plain],
        [g3_shape, jax.ShapeDtypeStruct((N_DEV, ri, D), BF16), jax.ShapeDtypeStruct((N_DEV, ro, D), BF16),
         jax.ShapeDtypeStruct((N_DEV,) + p_cw.shape, F32)],
        [(SIBLING,) + SAME_CORE, EVERYONE, EVERYONE, EVERYONE])
    (x1, h1, dau1, dag1, act1), (g3b_half, g_in, g_out, g_cw) = _ffn_fwd(
        xs, row(ffn1_norm), w3a, tm_fwd, cf, "ffn1_fwd", rider=gather_rest)
    w_in_t = g_in.reshape(N_DEV * ri, D)
    w_out_f = g_out.reshape(N_DEV * ro, D)
    cw_all = jnp.transpose(g_cw, (1, 0, 2)).reshape(p_cw.shape[0], N_DEV * wc)
    cw = cw_all[0:CONV_K]
    lw = cw_all[lw_row:lw_row + LRU_K]
    (x2, z, mix, u1, xr, hst), (g3b,) = _mix_core_fwd(
        x1, row(mix_norm), w_in_t, w_out_f, bda, bdx, cw, lw, vec, tm_mix, "mix_core_fwd",
        rider=_forward_comm([g3b_half], [stacked]))
    w3b = g3b.reshape(3, F, D)
    dx3, dob2, d_final_norm, loss_part, h3, dau2, dag2, act2 = _ffn_fwd(
        x2, row(ffn2_norm), w3b, tm_fwd, cf, "ffn2_fwd_loss", head=(row(final_norm), tgt))

    dx2, dgate2, dup2, d_ffn2_norm = _ffn_dgrad(dx3, x2, row(ffn2_norm), dau2, dag2, w3b, tm_ffn, cf, "ffn2_dgrad")
    (dw_gu2,) = _wgrad([dgate2, dup2], h3, tm_w, tf_w, "ffn2_wgrad_gu")
    (dw_d2,) = _wgrad([act2], dob2, tm_w1, tf_w, "ffn2_wgrad_d")
    sums_f2 = _reduce_level1([by_owner(dw_gu2, rf), by_owner(dw_d2, rf)], "f2")
    (dx1, sg, dbda, dbdx, dob1, d_mix_norm, dw_in_t, dw_out), parts_f2 = _mix_bwd(
        dx2, z, u1, xr, hst, x1, mix, row(mix_norm), w_in_t, w_out_f, bda, bdx, cw, lw, vec, tm_mix, "mix_bwd",
        rider=_chips_comm(sums_f2))
    sums_io = _reduce_level1([by_owner(dw_in_t[None], ri), by_owner(dw_out[None], ro)], "io")
    (dw_d1,), parts_io = _wgrad([act1], dob1, tm_w1, tf_w, "ffn1_wgrad_d", rider=_chips_comm(sums_io))
    sums_d1 = _reduce_level1([by_owner(dw_d1, rf)], "d1")
    dx0, dgate1, dup1, d_ffn1_norm = _ffn_dgrad(dx1, xs, row(ffn1_norm), dau1, dag1, w3a, tm_ffn, cf, "ffn1_dgrad")

    d_lru_w_a = _diag_blocks(dbda, H)
    d_lru_w_x = _diag_blocks(dbdx, H)
    small = [d_ffn1_norm, d_mix_norm, d_ffn2_norm, d_final_norm,
             sg[G_CB], sg[G_LNG], sg[G_LNB], sg[G_LCB], sg[G_BA], sg[G_BX], sg[G_LAM],
             d_lru_w_a, d_lru_w_x,
             sg[G_CW:G_CW + CONV_K], sg[G_LW:G_LW + LRU_K],
             loss_part]
    sizes = [a.size for a in small]
    flat = jnp.concatenate([a.reshape(-1) for a in small])
    n_rows = -(-flat.size // (8 * D)) * 8
    packed = jnp.concatenate([flat, jnp.zeros((n_rows * D - flat.size,), F32)]).reshape(n_rows, D)
    (dw_g1,), (red, *parts_d1) = _wgrad(
        [dgate1], h1, tm_w1, tf_w, "ffn1_wgrad_g", rider=_both(_small_sum_comm(packed), _chips_comm(sums_d1)))
    sums_g1 = _reduce_level1([by_owner(dw_g1, rf)], "g1")
    (dw_u1,), parts_g1 = _wgrad([dup1], h1, tm_w1, tf_w, "ffn1_wgrad_u", rider=_chips_comm(sums_g1))
    sums_u1 = _reduce_level1([by_owner(dw_u1, rf)], "u1")
    parts_u1 = _run_comm(_chips_comm(sums_u1), "rs_chips_u1")

    red = red.reshape(-1)
    offs = [0]
    for n in sizes:
        offs.append(offs[-1] + n)
    piece = lambda k, shape: red[offs[k]:offs[k + 1]].reshape(shape)
    loss = red[offs[15]]

    me = 4 * lax.axis_index("x") + 2 * lax.axis_index("y") + lax.axis_index("c")
    chan = lambda full_g: lax.dynamic_slice_in_dim(full_g, me * wc, wc, axis=1)
    grads = {
        "ffn1_norm": piece(0, (D,)), "mix_norm": piece(1, (D,)), "ffn2_norm": piece(2, (D,)), "final_norm": piece(3, (D,)),
        "conv_dw_bias": piece(4, (W,)), "conv_ln_g": piece(5, (W,)), "conv_ln_b": piece(6, (W,)),
        "lru_conv_b": piece(7, (W,)), "lru_b_a": piece(8, (W,)), "lru_b_x": piece(9, (W,)), "lru_lambda": piece(10, (W,)),
        "lru_w_a": piece(11, lru_w_a.shape), "lru_w_x": piece(12, lru_w_x.shape),
        "conv_dw": chan(piece(13, (CONV_K, W))), "lru_conv_w": chan(piece(14, (LRU_K, W))),
    }

    weights = dict(ffn1_norm=ffn1_norm, ffn1_w_gate=ffn1_w_gate, ffn1_w_up=ffn1_w_up, ffn1_w_down=ffn1_w_down, mix_norm=mix_norm, w_in=w_in, conv_dw=conv_dw, conv_dw_bias=conv_dw_bias, conv_ln_g=conv_ln_g, conv_ln_b=conv_ln_b, lru_conv_w=lru_conv_w, lru_conv_b=lru_conv_b, lru_w_a=lru_w_a, lru_b_a=lru_b_a, lru_w_x=lru_w_x, lru_b_x=lru_b_x, lru_lambda=lru_lambda, w_out=w_out, ffn2_norm=ffn2_norm, ffn2_w_gate=ffn2_w_gate, ffn2_w_up=ffn2_w_up, ffn2_w_down=ffn2_w_down, final_norm=final_norm)
    moment1 = dict(ffn1_norm=m_ffn1_norm, ffn1_w_gate=m_ffn1_w_gate, ffn1_w_up=m_ffn1_w_up, ffn1_w_down=m_ffn1_w_down, mix_norm=m_mix_norm, w_in=m_w_in, conv_dw=m_conv_dw, conv_dw_bias=m_conv_dw_bias, conv_ln_g=m_conv_ln_g, conv_ln_b=m_conv_ln_b, lru_conv_w=m_lru_conv_w, lru_conv_b=m_lru_conv_b, lru_w_a=m_lru_w_a, lru_b_a=m_lru_b_a, lru_w_x=m_lru_w_x, lru_b_x=m_lru_b_x, lru_lambda=m_lru_lambda, w_out=m_w_out, ffn2_norm=m_ffn2_norm, ffn2_w_gate=m_ffn2_w_gate, ffn2_w_up=m_ffn2_w_up, ffn2_w_down=m_ffn2_w_down, final_norm=m_final_norm)
    moment2 = dict(ffn1_norm=v_ffn1_norm, ffn1_w_gate=v_ffn1_w_gate, ffn1_w_up=v_ffn1_w_up, ffn1_w_down=v_ffn1_w_down, mix_norm=v_mix_norm, w_in=v_w_in, conv_dw=v_conv_dw, conv_dw_bias=v_conv_dw_bias, conv_ln_g=v_conv_ln_g, conv_ln_b=v_conv_ln_b, lru_conv_w=v_lru_conv_w, lru_conv_b=v_lru_conv_b, lru_w_a=v_lru_w_a, lru_b_a=v_lru_b_a, lru_w_x=v_lru_w_x, lru_b_x=v_lru_b_x, lru_lambda=v_lru_lambda, w_out=v_w_out, ffn2_norm=v_ffn2_norm, ffn2_w_gate=v_ffn2_w_gate, ffn2_w_up=v_ffn2_w_up, ffn2_w_down=v_ffn2_w_down, final_norm=v_final_norm)
    order = list(weights)
    big = {"ffn1_w_gate": (parts_g1[0], 0, True), "ffn1_w_up": (parts_u1[0], 0, True),
           "ffn1_w_down": (parts_d1[0], 0, False), "w_in": (parts_io[0], 0, True), "w_out": (parts_io[1], 0, False),
           "ffn2_w_gate": (parts_f2[0], 0, True), "ffn2_w_up": (parts_f2[0], 1, True),
           "ffn2_w_down": (parts_f2[1], 0, False)}
    delta, new_m, new_v = {}, {}, {}
    for n, (parts, k, d_major) in big.items():
        operands = weights[n], moment1[n], moment2[n]
        if d_major and n != "w_in":
            results = _finish(parts, k, *[a.T for a in operands], False, "finish_" + n)
            grads[n], delta[n], new_m[n], new_v[n] = [r.T for r in results]
        else:
            grads[n], delta[n], new_m[n], new_v[n] = _finish(parts, k, *operands, d_major, "finish_" + n)
    rest = [n for n in order if n not in big]
    rest_sizes = [weights[n].size for n in rest]
    n_flat = sum(rest_sizes)
    pad_rows = -(-n_flat // (8 * 128)) * 8

    def pack(d):
        f = jnp.concatenate([d[n].reshape(-1) for n in rest])
        return jnp.concatenate([f, jnp.zeros((pad_rows * 128 - n_flat,), F32)]).reshape(pad_rows, 128)

    d_s, m_s, v_s = _adamw(pack(weights), pack(grads), pack(moment1), pack(moment2), "adamw_small")
    off = 0
    for n, size in zip(rest, rest_sizes):
        shape = weights[n].shape
        delta[n] = d_s.reshape(-1)[off:off + size].reshape(shape)
        new_m[n] = m_s.reshape(-1)[off:off + size].reshape(shape)
        new_v[n] = v_s.reshape(-1)[off:off + size].reshape(shape)
        off += size

    return (loss, dx0.reshape(x.shape), *[grads[n] for n in order], *[delta[n] for n in order],
            *[new_m[n] for n in order], *[new_v[n] for n in order])
```

```python
import functools
import math

import jax
import jax.numpy as jnp
from jax import lax
from jax.experimental import pallas as pl
from jax.experimental.pallas import tpu as pltpu

F32 = jnp.float32
BF16 = jnp.bfloat16
MESH = pl.DeviceIdType.MESH

N_DEV = 8
N_CHIP = 4
SUBLANES = 8
RMS_EPS = 1e-6
LN_EPS = 1e-5
LRU_C = 8.0
CONV_K = 31
LRU_K = 4
CONV_HALO = 32
LRU_HALO = 8
FFN_RES = 0.5
ADAM_LR, ADAM_B1, ADAM_B2, ADAM_EPS, ADAM_WD, ADAM_STEP = 0.001, 0.9, 0.999, 1e-08, 0.01, 10
GELU_K = math.sqrt(2.0 / math.pi)
GELU_C = 0.044715

MIB = 1024 * 1024
NT = (((1,), (1,)), ((), ()))
NN = (((1,), (0,)), ((), ()))
TN = (((0,), (0,)), ((), ()))

V_CB, V_LNG, V_LNB, V_LCB, V_BA, V_BX, V_LAM = range(0, 7 * SUBLANES, SUBLANES)
G_CW = 0
G_CB, G_LNG, G_LNB = 31, 32, 33
G_LW = 34
G_LCB, G_BA, G_BX, G_LAM = 38, 39, 40, 41
G_ROWS = 48

HBM_SPEC = pl.BlockSpec(memory_space=pltpu.HBM)
VMEM_SPEC = pl.BlockSpec(memory_space=pltpu.VMEM)


def _dot(a, b, dims):
    return lax.dot_general(a, b, dims, preferred_element_type=F32)


def _sigmoid(x):
    return 1.0 / (1.0 + jnp.exp(-x))


def _gelu_parts(x):
    x2 = x * x
    th = jnp.tanh(GELU_K * x * (1.0 + GELU_C * x2))
    gl = 0.5 * x * (1.0 + th)
    dgl = 0.5 * (1.0 + th) + 0.5 * x * (1.0 - th * th) * GELU_K * (1.0 + 3.0 * GELU_C * x2)
    return gl, dgl


def _neg_expm1(y):
    series = -y * (1.0 + y * (1.0 / 2) * (1.0 + y * (1.0 / 3) * (1.0 + y * (1.0 / 4) * (1.0 + y * (1.0 / 5) * (1.0 + y * (1.0 / 6))))))
    return jnp.where(y > -0.25, series, 1.0 - jnp.exp(y))


def _softplus_neg(lam):
    t = -lam
    e = jnp.exp(-jnp.abs(t))
    s = 1.0 + e
    log1p_e = jnp.log(s) - ((s - 1.0) - e) / s
    return jnp.maximum(t, 0.0) + log1p_e


def _rms_stats(xv):
    rstd = lax.rsqrt(jnp.mean(xv * xv, axis=-1, keepdims=True) + RMS_EPS)
    return xv * rstd, rstd


def _rms_bwd(xhat, rstd, g, dh):
    dxhat = dh * g
    dx = rstd * (dxhat - xhat * jnp.mean(dxhat * xhat, axis=-1, keepdims=True))
    return dx, jnp.sum(dh * xhat, axis=0, keepdims=True)


def _row_windows(buf_ref, n_rows, offsets):
    total = buf_ref.shape[0]
    full = buf_ref[...]
    for b in range(SUBLANES):
        offs = [o for o in offsets if o % SUBLANES == b]
        if not offs:
            continue
        assert max(offs) + n_rows <= total
        moved = full if b == 0 else pltpu.roll(full, total - b, 0)
        for o in offs:
            yield o, moved[o - b:o - b + n_rows, :]


def _scan_rows(av, bv, edge, out_ref, reverse=False):
    tm, W = av.shape
    sub = lax.broadcasted_iota(jnp.int32, (tm, W), 0) % SUBLANES
    s = 1
    while s < SUBLANES:
        keep = (sub < SUBLANES - s) if reverse else (sub >= s)
        shift = tm - s if reverse else s
        bv = jnp.where(keep, av * pltpu.roll(bv, shift, 0) + bv, bv)
        av = jnp.where(keep, av * pltpu.roll(av, shift, 0), av)
        s *= 2
    starts = range(0, tm, SUBLANES)
    for r0 in (reversed(starts) if reverse else starts):
        group = av[r0:r0 + SUBLANES, :] * edge + bv[r0:r0 + SUBLANES, :]
        out_ref[r0:r0 + SUBLANES, :] = group
        edge = group[0:1, :] if reverse else group[SUBLANES - 1:SUBLANES, :]
    return edge


class _Comm:
    def __init__(self, arrays, in_specs, out_shapes, out_specs, scratch, start, wait, aliases=None):
        self.arrays, self.in_specs = list(arrays), list(in_specs)
        self.out_shapes, self.out_specs = list(out_shapes), list(out_specs)
        self.scratch, self.start, self.wait = list(scratch), start, wait
        self.aliases = dict(aliases or {})


def _in_hbm(a):
    return pltpu.with_memory_space_constraint(a, pltpu.HBM)


def _operands(comm):
    return [a if spec is VMEM_SPEC else _in_hbm(a) for a, spec in zip(comm.arrays, comm.in_specs)]


def _call(body, *, name, grid, in_specs, out_specs, out_shape, scratch_shapes, vmem_mib, args, rider=None,
          num_scalar_prefetch=0):
    params = pltpu.CompilerParams(dimension_semantics=("arbitrary",) * len(grid), vmem_limit_bytes=vmem_mib * MIB)
    args = [a if k < num_scalar_prefetch else _in_hbm(a) for k, a in enumerate(args)]
    if rider is None:
        return pl.pallas_call(
            body, name=name,
            grid_spec=pltpu.PrefetchScalarGridSpec(
                num_scalar_prefetch=num_scalar_prefetch, grid=grid, in_specs=in_specs, out_specs=out_specs,
                scratch_shapes=scratch_shapes),
            out_shape=out_shape, compiler_params=params)(*args)
    assert num_scalar_prefetch == 0
    n_in, n_out, n_scr = len(in_specs), len(out_specs), len(scratch_shapes)
    r_in, r_out = len(rider.arrays), len(rider.out_shapes)
    n_axes = len(grid)

    def carried(*refs):
        pos = [0]

        def take(n):
            pos[0] += n
            return refs[pos[0] - n:pos[0]]

        ins, r_ins, outs, r_outs, scr, r_scr = take(n_in), take(r_in), take(n_out), take(r_out), take(n_scr), take(len(rider.scratch))
        first = pl.program_id(0) == 0
        last = pl.program_id(0) == grid[0] - 1
        for ax in range(1, n_axes):
            first = first & (pl.program_id(ax) == 0)
            last = last & (pl.program_id(ax) == grid[ax] - 1)

        @pl.when(first)
        def _():
            rider.start(r_ins, r_outs, r_scr)

        body(*ins, *outs, *scr)

        @pl.when(last)
        def _():
            rider.wait(r_ins, r_outs, r_scr)

    res = pl.pallas_call(
        carried, name=name,
        grid=grid,
        in_specs=list(in_specs) + rider.in_specs,
        out_specs=list(out_specs) + rider.out_specs,
        out_shape=list(out_shape) + rider.out_shapes,
        scratch_shapes=list(scratch_shapes) + rider.scratch,
        input_output_aliases={n_in + i: n_out + o for i, o in rider.aliases.items()},
        compiler_params=params)(*args, *_operands(rider))
    return res[:n_out], res[n_out:]


def _run_comm(comm, name):
    n_in, n_out = len(comm.arrays), len(comm.out_shapes)

    def body(*refs):
        ins, outs, scr = refs[:n_in], refs[n_in:n_in + n_out], refs[n_in + n_out:]
        comm.start(ins, outs, scr)
        comm.wait(ins, outs, scr)

    return pl.pallas_call(
        body, name=name,
        in_specs=comm.in_specs, out_specs=comm.out_specs, out_shape=comm.out_shapes,
        scratch_shapes=comm.scratch, input_output_aliases=comm.aliases,
        compiler_params=pltpu.CompilerParams(vmem_limit_bytes=24 * MIB))(*_operands(comm))


def _both(a, b):
    ni, no, ns = len(a.arrays), len(a.out_shapes), len(a.scratch)

    def start(ins, outs, scr):
        a.start(ins[:ni], outs[:no], scr[:ns])
        b.start(ins[ni:], outs[no:], scr[ns:])

    def wait(ins, outs, scr):
        a.wait(ins[:ni], outs[:no], scr[:ns])
        b.wait(ins[ni:], outs[no:], scr[ns:])

    aliases = dict(a.aliases)
    aliases.update({ni + i: no + o for i, o in b.aliases.items()})
    return _Comm(a.arrays + b.arrays, a.in_specs + b.in_specs, a.out_shapes + b.out_shapes,
                 a.out_specs + b.out_specs, a.scratch + b.scratch, start, wait, aliases)


def _place():
    return lax.axis_index("x"), lax.axis_index("y"), lax.axis_index("c")


def _peer(k):
    x, y, c = _place()
    px, py, pc = x ^ ((k >> 2) & 1), y ^ ((k >> 1) & 1), c ^ (k & 1)
    return (px, py, pc), 4 * px + 2 * py + pc


SIBLING = 1
SAME_CORE = (2, 4, 6)
EVERYONE = tuple(range(1, N_DEV))


def _gather_comm(shards, views, out_shapes, relations):
    na = len(shards)

    def copies(ins, outs, scr):
        send_sems, recv_sems, _ = scr
        _, me = _peer(0)
        out = []
        for a in range(na):
            for k in relations[a]:
                peer, theirs = _peer(k)
                send = functools.partial(
                    pltpu.make_async_remote_copy,
                    src_ref=ins[a], dst_ref=views[a](outs[a], me),
                    send_sem=send_sems.at[7 * a + k - 1], recv_sem=recv_sems.at[7 * a + k - 1],
                    device_id=peer, device_id_type=MESH)
                recv = functools.partial(
                    pltpu.make_async_remote_copy,
                    src_ref=ins[a], dst_ref=views[a](outs[a], theirs),
                    send_sem=send_sems.at[7 * a + k - 1], recv_sem=recv_sems.at[7 * a + k - 1],
                    device_id=peer, device_id_type=MESH)
                out.append((send, recv))
        return out

    def local(ins, outs, scr):
        _, me = _peer(0)
        return [pltpu.make_async_copy(ins[a], views[a](outs[a], me), scr[2].at[a]) for a in range(na)]

    def start(ins, outs, scr):
        for cp in local(ins, outs, scr):
            cp.start()
        for send, _ in copies(ins, outs, scr):
            send().start()

    def wait(ins, outs, scr):
        for _, recv in copies(ins, outs, scr):
            recv().wait_recv()
        for send, _ in copies(ins, outs, scr):
            send().wait_send()
        for cp in local(ins, outs, scr):
            cp.wait()

    return _Comm(shards, [HBM_SPEC] * na, out_shapes, [HBM_SPEC] * na,
                 [pltpu.SemaphoreType.DMA((7 * na,)), pltpu.SemaphoreType.DMA((7 * na,)),
                  pltpu.SemaphoreType.DMA((na,))], start, wait)


def _forward_comm(gathered, views):
    na = len(gathered)
    shapes = [jax.ShapeDtypeStruct(g.shape, g.dtype) for g in gathered]

    def copies(outs, scr):
        send_sems, recv_sems = scr
        sibling, _ = _peer(SIBLING)
        out = []
        for a in range(na):
            for n, k in enumerate(SAME_CORE):
                _, mine = _peer(k)
                _, theirs = _peer(k ^ SIBLING)
                send = functools.partial(
                    pltpu.make_async_remote_copy,
                    src_ref=views[a](outs[a], mine), dst_ref=views[a](outs[a], mine),
                    send_sem=send_sems.at[3 * a + n], recv_sem=recv_sems.at[3 * a + n],
                    device_id=sibling, device_id_type=MESH)
                recv = functools.partial(
                    pltpu.make_async_remote_copy,
                    src_ref=views[a](outs[a], mine), dst_ref=views[a](outs[a], theirs),
                    send_sem=send_sems.at[3 * a + n], recv_sem=recv_sems.at[3 * a + n],
                    device_id=sibling, device_id_type=MESH)
                out.append((send, recv))
        return out

    def start(ins, outs, scr):
        for send, _ in copies(outs, scr):
            send().start()

    def wait(ins, outs, scr):
        for _, recv in copies(outs, scr):
            recv().wait_recv()
        for send, _ in copies(outs, scr):
            send().wait_send()

    return _Comm(gathered, [HBM_SPEC] * na, shapes, [HBM_SPEC] * na,
                 [pltpu.SemaphoreType.DMA((3 * na,)), pltpu.SemaphoreType.DMA((3 * na,))], start, wait,
                 aliases={a: a for a in range(na)})


def _all_gather(shards, views, out_shapes, name):
    na = len(shards)
    level1 = _gather_comm(shards, views, out_shapes, [(SIBLING,) + SAME_CORE] * na)

    def body(*refs):
        ins, outs = refs[:na], refs[na:2 * na]
        send_sems, recv_sems, local_sems, fwd_send, fwd_recv = refs[2 * na:]
        sibling, _ = _peer(SIBLING)
        scr = (send_sems, recv_sems, local_sems)
        level1.start(ins, outs, scr)
        passed, landing = [], []
        for a in range(na):
            for n, k in enumerate(SAME_CORE):
                peer, mine = _peer(k)
                _, theirs = _peer(k ^ SIBLING)
                pltpu.make_async_remote_copy(
                    src_ref=ins[a], dst_ref=views[a](outs[a], mine),
                    send_sem=send_sems.at[7 * a + k - 1], recv_sem=recv_sems.at[7 * a + k - 1],
                    device_id=peer, device_id_type=MESH).wait_recv()
                fwd = pltpu.make_async_remote_copy(
                    src_ref=views[a](outs[a], mine), dst_ref=views[a](outs[a], mine),
                    send_sem=fwd_send.at[3 * a + n], recv_sem=fwd_recv.at[3 * a + n],
                    device_id=sibling, device_id_type=MESH)
                fwd.start()
                passed.append(fwd)
                landing.append(pltpu.make_async_remote_copy(
                    src_ref=views[a](outs[a], mine), dst_ref=views[a](outs[a], theirs),
                    send_sem=fwd_send.at[3 * a + n], recv_sem=fwd_recv.at[3 * a + n],
                    device_id=sibling, device_id_type=MESH))
        for a in range(na):
            _, theirs = _peer(SIBLING)
            pltpu.make_async_remote_copy(
                src_ref=ins[a], dst_ref=views[a](outs[a], theirs),
                send_sem=send_sems.at[7 * a + SIBLING - 1], recv_sem=recv_sems.at[7 * a + SIBLING - 1],
                device_id=sibling, device_id_type=MESH).wait_recv()
        for cp in landing:
            cp.wait_recv()
        for cp in passed:
            cp.wait_send()
        _, me = _peer(0)
        for a in range(na):
            for k in (SIBLING,) + SAME_CORE:
                peer, _ = _peer(k)
                pltpu.make_async_remote_copy(
                    src_ref=ins[a], dst_ref=views[a](outs[a], me),
                    send_sem=send_sems.at[7 * a + k - 1], recv_sem=recv_sems.at[7 * a + k - 1],
                    device_id=peer, device_id_type=MESH).wait_send()
            pltpu.make_async_copy(ins[a], views[a](outs[a], me), local_sems.at[a]).wait()

    return pl.pallas_call(
        body, name=name,
        in_specs=[HBM_SPEC] * na, out_specs=[HBM_SPEC] * na, out_shape=out_shapes,
        scratch_shapes=level1.scratch + [pltpu.SemaphoreType.DMA((3 * na,)), pltpu.SemaphoreType.DMA((3 * na,))],
    )(*[_in_hbm(s) for s in shards])


def _sibling_comm(grads):
    na = len(grads)
    shapes = [jax.ShapeDtypeStruct(g.shape[:2] + g.shape[3:], g.dtype) for g in grads]

    def copies(ins, outs, scr):
        x, y, c = _place()
        return [pltpu.make_async_remote_copy(
            src_ref=ins[a].at[:, :, 1 - c], dst_ref=outs[a],
            send_sem=scr[0].at[a], recv_sem=scr[1].at[a],
            device_id=(x, y, 1 - c), device_id_type=MESH) for a in range(na)]

    def start(ins, outs, scr):
        for cp in copies(ins, outs, scr):
            cp.start()

    def wait(ins, outs, scr):
        for cp in copies(ins, outs, scr):
            cp.wait()

    return _Comm(grads, [HBM_SPEC] * na, shapes, [HBM_SPEC] * na,
                 [pltpu.SemaphoreType.DMA((na,)), pltpu.SemaphoreType.DMA((na,))], start, wait)


def _chips_comm(sums):
    na = len(sums)
    shapes = [jax.ShapeDtypeStruct(s.shape, s.dtype) for s in sums]

    def copies(ins, outs, scr):
        x, y, c = _place()
        mine = 2 * x + y
        out = []
        for a in range(na):
            for n, k in enumerate(SAME_CORE):
                (px, py, pc), _ = _peer(k)
                theirs = 2 * px + py
                send = functools.partial(
                    pltpu.make_async_remote_copy,
                    src_ref=ins[a].at[:, theirs], dst_ref=outs[a].at[:, mine],
                    send_sem=scr[0].at[3 * a + n], recv_sem=scr[1].at[3 * a + n],
                    device_id=(px, py, pc), device_id_type=MESH)
                recv = functools.partial(
                    pltpu.make_async_remote_copy,
                    src_ref=ins[a].at[:, mine], dst_ref=outs[a].at[:, theirs],
                    send_sem=scr[0].at[3 * a + n], recv_sem=scr[1].at[3 * a + n],
                    device_id=(px, py, pc), device_id_type=MESH)
                out.append((send, recv))
        return out

    def local(ins, outs, scr):
        x, y, _ = _place()
        mine = 2 * x + y
        return [pltpu.make_async_copy(ins[a].at[:, mine], outs[a].at[:, mine], scr[2].at[a]) for a in range(na)]

    def start(ins, outs, scr):
        for cp in local(ins, outs, scr):
            cp.start()
        for send, _ in copies(ins, outs, scr):
            send().start()

    def wait(ins, outs, scr):
        for _, recv in copies(ins, outs, scr):
            recv().wait_recv()
        for send, _ in copies(ins, outs, scr):
            send().wait_send()
        for cp in local(ins, outs, scr):
            cp.wait()

    return _Comm(sums, [HBM_SPEC] * na, shapes, [HBM_SPEC] * na,
                 [pltpu.SemaphoreType.DMA((3 * na,)), pltpu.SemaphoreType.DMA((3 * na,)),
                  pltpu.SemaphoreType.DMA((na,))], start, wait)


def _small_sum_comm(s):
    R, C = s.shape

    def copies(ins, scr):
        buf, send_sems, recv_sems = scr
        _, me = _peer(0)
        out = []
        for k in EVERYONE:
            peer, theirs = _peer(k)
            send = functools.partial(
                    pltpu.make_async_remote_copy,
                src_ref=ins[0], dst_ref=buf.at[me], send_sem=send_sems.at[k - 1], recv_sem=recv_sems.at[k - 1],
                device_id=peer, device_id_type=MESH)
            recv = functools.partial(
                    pltpu.make_async_remote_copy,
                src_ref=ins[0], dst_ref=buf.at[theirs], send_sem=send_sems.at[k - 1], recv_sem=recv_sems.at[k - 1],
                device_id=peer, device_id_type=MESH)
            out.append((send, recv))
        return out

    def start(ins, outs, scr):
        _, me = _peer(0)
        scr[0][me] = ins[0][...]
        for send, _ in copies(ins, scr):
            send().start()

    def wait(ins, outs, scr):
        for _, recv in copies(ins, scr):
            recv().wait_recv()
        for send, _ in copies(ins, scr):
            send().wait_send()
        acc = scr[0][0]
        for j in range(1, N_DEV):
            acc = acc + scr[0][j]
        outs[0][...] = acc

    return _Comm([s], [VMEM_SPEC], [jax.ShapeDtypeStruct((R, C), F32)], [VMEM_SPEC],
                 [pltpu.VMEM((N_DEV, R, C), F32), pltpu.SemaphoreType.DMA((N_DEV - 1,)),
                  pltpu.SemaphoreType.DMA((N_DEV - 1,))], start, wait)


def _prep_weights(ffn1, ffn2, w_in, w_out, name):
    rf, D = ffn1[2].shape
    ri, ro = w_in.shape[1], w_out.shape[0]

    def body(g1, u1, d1, g2, u2, d2, wi, wo, p1_ref, p2_ref, pi_ref, po_ref):
        for p_ref, shards in ((p1_ref, (g1, u1, d1)), (p2_ref, (g2, u2, d2))):
            for k, shard in enumerate(shards):
                p_ref[k] = shard[...].astype(BF16)
        pi_ref[...] = wi[...].T.astype(BF16)
        po_ref[...] = wo[...].astype(BF16)

    args = (*ffn1, *ffn2, w_in, w_out)
    whole = lambda shape: pl.BlockSpec(shape, lambda i: (0,) * len(shape))
    out_shapes = [(3, rf, D), (3, rf, D), (ri, D), (ro, D)]
    return _call(
        body, name=name, grid=(1,),
        in_specs=[whole(a.shape) for a in args], out_specs=[whole(s) for s in out_shapes],
        out_shape=[jax.ShapeDtypeStruct(s, BF16) for s in out_shapes],
        scratch_shapes=[], vmem_mib=48, args=args)


def _load_weights(w_hbm, w_vmem, sem):
    @pl.when(pl.program_id(0) == 0)
    def _():
        copies = [pltpu.make_async_copy(w_hbm.at[k], w_vmem.at[k], sem.at[k]) for k in range(3)]
        for cp in copies:
            cp.start()
        for cp in copies:
            cp.wait()


def _ffn_fwd(x, g, w3, tm, cf, name, rider=None, head=None):
    T, D = x.shape
    F = w3.shape[1]
    n_head = 0 if head is None else 2
    sub = min(256, tm)

    def body(x_ref, g_ref, w_hbm, *refs):
        head_refs, refs = refs[:n_head], refs[n_head:]
        if head is None:
            (xo_ref, h_ref, dau_ref, dag_ref, act_ref, wv, sem) = refs
        else:
            (dx_ref, dob_ref, dgf_ref, loss_ref, h_ref, dau_ref, dag_ref, act_ref, wv, sem) = refs
        _load_weights(w_hbm, wv, sem)
        if head is not None:
            @pl.when(pl.program_id(0) == 0)
            def _():
                dgf_ref[...] = jnp.zeros_like(dgf_ref)
                loss_ref[...] = jnp.zeros_like(loss_ref)

        subs = [slice(r0, r0 + sub) for r0 in range(0, tm, sub)]
        hbs = []
        for rows in subs:
            xhat, _ = _rms_stats(x_ref[rows, :])
            hbs.append((xhat * g_ref[...]).astype(BF16))
            h_ref[rows, :] = hbs[-1]
        for lo in range(0, F, cf):
            for rows, hb in zip(subs, hbs):
                gate = _dot(hb, wv[0, lo:lo + cf, :], NT)
                up = _dot(hb, wv[1, lo:lo + cf, :], NT)
                sig = _sigmoid(gate)
                silu = gate * sig
                dau_ref[rows, lo:lo + cf] = silu.astype(BF16)
                dag_ref[rows, lo:lo + cf] = (up * (sig * (1.0 + gate * (1.0 - sig)))).astype(BF16)
                act_ref[rows, lo:lo + cf] = (silu * up).astype(BF16)
        for rows in subs:
            x_out = x_ref[rows, :] + FFN_RES * _dot(act_ref[rows, :], wv[2], NN)
            if head is None:
                xo_ref[rows, :] = x_out
                continue
            gf_ref, tgt_ref = head_refs
            yhat, rstd = _rms_stats(x_out)
            gf = gf_ref[...]
            err = yhat * gf - tgt_ref[rows, :]
            loss_ref[...] += (0.5 / D) * jnp.sum(err * err)
            dx, dgf = _rms_bwd(yhat, rstd, gf, err * (1.0 / D))
            dx_ref[rows, :] = dx
            dob_ref[rows, :] = (FFN_RES * dx).astype(BF16)
            dgf_ref[...] += dgf

    row = pl.BlockSpec((tm, D), lambda i: (i, 0))
    hid = pl.BlockSpec((tm, F), lambda i: (i, 0))
    vec = pl.BlockSpec((1, D), lambda i: (0, 0))
    row_f32, row_bf16 = jax.ShapeDtypeStruct((T, D), F32), jax.ShapeDtypeStruct((T, D), BF16)
    if head is None:
        first_specs, first_shapes = [row], [row_f32]
    else:
        first_specs = [row, row, vec, pl.BlockSpec((1, 128), lambda i: (0, 0))]
        first_shapes = [row_f32, row_bf16, jax.ShapeDtypeStruct((1, D), F32), jax.ShapeDtypeStruct((1, 128), F32)]
    return _call(
        body, name=name, grid=(T // tm,),
        in_specs=[row, vec, HBM_SPEC] + ([] if head is None else [vec, row]),
        out_specs=first_specs + [row, hid, hid, hid],
        out_shape=first_shapes + [row_bf16] + [jax.ShapeDtypeStruct((T, F), BF16)] * 3,
        scratch_shapes=[pltpu.VMEM((3, F, D), BF16), pltpu.SemaphoreType.DMA((3,))],
        vmem_mib=60, args=(x, g, w3) + (() if head is None else tuple(head)), rider=rider)


def _ffn_dgrad(dout, x, g, dau, dag, w3, tm, cf, name, rider=None):
    T, D = x.shape
    F = w3.shape[1]

    def body(do_ref, x_ref, g_ref, dau_ref, dag_ref, w_hbm, dx_ref, dgate_ref, dup_ref, dg_ref, wv, sem):
        _load_weights(w_hbm, wv, sem)

        @pl.when(pl.program_id(0) == 0)
        def _():
            dg_ref[...] = jnp.zeros_like(dg_ref)

        dob = (FFN_RES * do_ref[...]).astype(BF16)
        for lo in range(0, F, cf):
            dact = _dot(dob, wv[2, lo:lo + cf, :], NT)
            dup_ref[:, lo:lo + cf] = (dact * dau_ref[:, lo:lo + cf].astype(F32)).astype(BF16)
            dgate_ref[:, lo:lo + cf] = (dact * dag_ref[:, lo:lo + cf].astype(F32)).astype(BF16)
        dh = _dot(dgate_ref[...], wv[0], NN) + _dot(dup_ref[...], wv[1], NN)
        xhat, rstd = _rms_stats(x_ref[...])
        dx, dg = _rms_bwd(xhat, rstd, g_ref[...], dh)
        dx_ref[...] = do_ref[...] + dx
        dg_ref[...] += dg

    row = pl.BlockSpec((tm, D), lambda i: (i, 0))
    hid = pl.BlockSpec((tm, F), lambda i: (i, 0))
    vec = pl.BlockSpec((1, D), lambda i: (0, 0))
    return _call(
        body, name=name, grid=(T // tm,),
        in_specs=[row, row, vec, hid, hid, HBM_SPEC],
        out_specs=[row, hid, hid, vec],
        out_shape=[jax.ShapeDtypeStruct((T, D), F32), jax.ShapeDtypeStruct((T, F), BF16),
                   jax.ShapeDtypeStruct((T, F), BF16), jax.ShapeDtypeStruct((1, D), F32)],
        scratch_shapes=[pltpu.VMEM((3, F, D), BF16), pltpu.SemaphoreType.DMA((3,))],
        vmem_mib=52, args=(dout, x, g, dau, dag, w3), rider=rider)


def _wgrad(lhs, rhs, tm, tf, name, rider=None):
    T, F = lhs[0].shape
    D = rhs.shape[1]
    K = len(lhs)

    def body(*refs):
        lhs_refs, rhs_ref, dw_ref, accs = refs[:K], refs[K], refs[K + 1], refs[K + 2:]
        i = pl.program_id(1)

        @pl.when(i == 0)
        def _():
            for acc in accs:
                acc[...] = jnp.zeros_like(acc)

        rv = rhs_ref[...]
        for acc, lhs_ref in zip(accs, lhs_refs):
            acc[...] += _dot(lhs_ref[...], rv, TN)

        @pl.when(i == pl.num_programs(1) - 1)
        def _():
            for k, acc in enumerate(accs):
                dw_ref[k] = acc[...].astype(BF16)

    hid = pl.BlockSpec((tm, tf), lambda f, i: (i, f))
    return _call(
        body, name=name, grid=(F // tf, T // tm),
        in_specs=[hid] * K + [pl.BlockSpec((tm, D), lambda f, i: (i, 0))],
        out_specs=[pl.BlockSpec((K, tf, D), lambda f, i: (0, f, 0))],
        out_shape=[jax.ShapeDtypeStruct((K, F, D), BF16)],
        scratch_shapes=[pltpu.VMEM((tf, D), F32)] * K,
        vmem_mib=56, args=(*lhs, rhs), rider=rider)


def _lru_gates(xr, bda_ref, bdx_ref, vec_ref):
    xrb = xr.astype(BF16)
    r = _sigmoid(_dot(xrb, bda_ref[...], NN) + vec_ref[V_BA:V_BA + 1, :])
    ig = _sigmoid(_dot(xrb, bdx_ref[...], NN) + vec_ref[V_BX:V_BX + 1, :])
    sp = _softplus_neg(vec_ref[V_LAM:V_LAM + 1, :])
    log_a = (-LRU_C * sp) * r
    a = jnp.exp(log_a)
    mult = jnp.sqrt(_neg_expm1(2.0 * log_a))
    return xrb, r, ig, sp, a, mult


def _layernorm_stats(u1):
    xc = u1 - jnp.mean(u1, axis=-1, keepdims=True)
    rs = lax.rsqrt(jnp.mean(xc * xc, axis=-1, keepdims=True) + LN_EPS)
    return xc * rs, rs


def _mix_core_fwd(x1, g, w_in_t, w_out, bda, bdx, cw, lw, vec, tm, name, rider=None):
    T, D = x1.shape
    W = cw.shape[1]
    assert tm >= CONV_HALO and w_in_t.shape[0] == 4 * W

    def body(x1_ref, g_ref, wi_ref, wo_ref, bda_ref, bdx_ref, cw_ref, lw_ref, vec_ref,
             x2_ref, z_ref, mix_ref, u1_ref, xr_ref, hst_ref, ubuf, rbuf, hc):
        @pl.when(pl.program_id(0) == 0)
        def _():
            ubuf[0:CONV_HALO, :] = jnp.zeros((CONV_HALO, W), F32)
            rbuf[0:LRU_HALO, :] = jnp.zeros((LRU_HALO, W), F32)
            hc[...] = jnp.zeros_like(hc)

        xhat, _ = _rms_stats(x1_ref[...])
        z_ref[...] = _dot((xhat * g_ref[...]).astype(BF16), wi_ref[...], NT)

        ubuf[CONV_HALO:CONV_HALO + tm, :] = z_ref[:, 0:W] * _sigmoid(z_ref[:, W:2 * W])
        u1 = jnp.zeros((tm, W), F32) + vec_ref[V_CB:V_CB + 1, :]
        base = CONV_HALO - (CONV_K - 1)
        for off, win in _row_windows(ubuf, tm, range(base, base + CONV_K)):
            u1 = u1 + cw_ref[off - base:off - base + 1, :] * win
        ubuf[0:CONV_HALO, :] = ubuf[tm:tm + CONV_HALO, :]
        u1_ref[...] = u1
        xh, _ = _layernorm_stats(u1)
        u2 = xh * vec_ref[V_LNG:V_LNG + 1, :] + vec_ref[V_LNB:V_LNB + 1, :]
        ub = (u2 * _sigmoid(u2)).astype(BF16)
        mix_ref[:, 0:W] = ub

        rbuf[LRU_HALO:LRU_HALO + tm, :] = z_ref[:, 2 * W:3 * W]
        xr = jnp.zeros((tm, W), F32) + vec_ref[V_LCB:V_LCB + 1, :]
        for k in range(LRU_K):
            off = LRU_HALO - (LRU_K - 1) + k
            xr = xr + lw_ref[k:k + 1, :] * rbuf[off:off + tm, :]
        rbuf[0:LRU_HALO, :] = rbuf[tm:tm + LRU_HALO, :]
        xr_ref[...] = xr
        _, _, ig, _, a, mult = _lru_gates(xr, bda_ref, bdx_ref, vec_ref)
        hc[0:1, :] = _scan_rows(a, mult * (ig * xr), hc[0:1, :], hst_ref)
        gl, _ = _gelu_parts(z_ref[:, 3 * W:4 * W])
        yb = (hst_ref[...] * gl).astype(BF16)
        mix_ref[:, W:2 * W] = yb

        x2_ref[...] = x1_ref[...] + _dot(ub, wo_ref[0:W, :], NN) + _dot(yb, wo_ref[W:2 * W, :], NN)

    full = lambda a: pl.BlockSpec(a.shape, lambda i: (0,) * a.ndim)
    tile = lambda n: pl.BlockSpec((tm, n), lambda i: (i, 0))
    return _call(
        body, name=name, grid=(T // tm,),
        in_specs=[tile(D), full(g), full(w_in_t), full(w_out), full(bda), full(bdx), full(cw), full(lw), full(vec)],
        out_specs=[tile(D), tile(4 * W), tile(2 * W), tile(W), tile(W), tile(W)],
        out_shape=[jax.ShapeDtypeStruct((T, D), F32), jax.ShapeDtypeStruct((T, 4 * W), F32),
                   jax.ShapeDtypeStruct((T, 2 * W), BF16), jax.ShapeDtypeStruct((T, W), F32),
                   jax.ShapeDtypeStruct((T, W), F32), jax.ShapeDtypeStruct((T, W), F32)],
        scratch_shapes=[pltpu.VMEM((tm + CONV_HALO, W), F32), pltpu.VMEM((tm + LRU_HALO, W), F32),
                        pltpu.VMEM((8, W), F32)],
        vmem_mib=56, args=(x1, g, w_in_t, w_out, bda, bdx, cw, lw, vec), rider=rider)


def _mix_bwd(dx2, z, u1, xr, hst, x1, mix, g, w_in_t, w_out, bda, bdx, cw, lw, vec, tm, name, rider=None):
    T, D = dx2.shape
    W = cw.shape[1]
    nt = T // tm
    assert tm >= CONV_HALO and tm % CONV_HALO == 0

    def body(dx_ref, z_ref, zh_ref, u1_ref, xr_ref, h_ref, hh_ref, wo_ref, bda_ref, bdx_ref, cw_ref, lw_ref, vec_ref,
             x1_ref, mix_ref, g_ref, wi_ref,
             dx1_ref, sg_ref, dbda_ref, dbdx_ref, dob_ref, dg_ref, dwi_ref, dwo_ref,
             u0buf, du1buf, rxbuf, dxrbuf, gbuf, gc, spacc, dz_ref, ai_ref, ao_ref):
        i = pl.program_id(0)
        first = i == nt - 1
        row = lax.broadcasted_iota(jnp.int32, (tm, W), 0)

        @pl.when(i == 0)
        def _():
            sg_ref[...] = jnp.zeros_like(sg_ref)
            dbda_ref[...] = jnp.zeros_like(dbda_ref)
            dbdx_ref[...] = jnp.zeros_like(dbdx_ref)
            du1buf[tm:tm + CONV_HALO, :] = jnp.zeros((CONV_HALO, W), F32)
            dxrbuf[tm:tm + LRU_HALO, :] = jnp.zeros((LRU_HALO, W), F32)
            gc[...] = jnp.zeros_like(gc)
            spacc[...] = jnp.zeros_like(spacc)
            dg_ref[...] = jnp.zeros_like(dg_ref)
            ai_ref[...] = jnp.zeros_like(ai_ref)
            ao_ref[...] = jnp.zeros_like(ao_ref)

        def accum(r, val):
            sg_ref[r:r + 1, :] += jnp.sum(val, axis=0, keepdims=True)

        x1hat, x1rstd = _rms_stats(x1_ref[...])
        gain = g_ref[...]
        hb = (x1hat * gain).astype(BF16)

        def in_proj_bwd(lo, hi):
            dzb = dz_ref[:, lo:hi]
            ai_ref[lo:hi, :] += _dot(dzb, hb, TN)
            return _dot(dzb, wi_ref[lo:hi, :], NN)

        dxb = dx_ref[...].astype(BF16)
        ao_ref[...] += _dot(mix_ref[...], dxb, TN)
        dmix = _dot(dxb, wo_ref[...], NT)
        d_u = dmix[:, 0:W]
        d_yr = dmix[:, W:2 * W]

        xh, rs = _layernorm_stats(u1_ref[...])
        ln_g = vec_ref[V_LNG:V_LNG + 1, :]
        u2 = xh * ln_g + vec_ref[V_LNB:V_LNB + 1, :]
        s2 = _sigmoid(u2)
        d_u2 = d_u * (s2 * (1.0 + u2 * (1.0 - s2)))
        accum(G_LNG, d_u2 * xh)
        accum(G_LNB, d_u2)
        d_xh = d_u2 * ln_g
        d_u1 = rs * (d_xh - jnp.mean(d_xh, axis=-1, keepdims=True)
                     - xh * jnp.mean(d_xh * xh, axis=-1, keepdims=True))
        accum(G_CB, d_u1)
        halo_on = jnp.where(first, 0.0, 1.0)
        u0buf[0:CONV_HALO, :] = halo_on * (zh_ref[:, 0:W] * _sigmoid(zh_ref[:, W:2 * W]))
        cv = z_ref[:, 0:W]
        sgc = _sigmoid(z_ref[:, W:2 * W])
        u0buf[CONV_HALO:CONV_HALO + tm, :] = cv * sgc
        du1buf[0:tm, :] = d_u1
        base = CONV_HALO - (CONV_K - 1)
        for off, win in _row_windows(u0buf, tm, range(base, base + CONV_K)):
            accum(G_CW + off - base, d_u1 * win)
        d_u0 = jnp.zeros((tm, W), F32)
        for off, win in _row_windows(du1buf, tm, range(0, CONV_K)):
            d_u0 = d_u0 + cw_ref[CONV_K - 1 - off:CONV_K - off, :] * win
        du1buf[tm:tm + CONV_HALO, :] = du1buf[0:CONV_HALO, :]
        dz_ref[:, 0:W] = (d_u0 * sgc).astype(BF16)
        dz_ref[:, W:2 * W] = (d_u0 * cv * (sgc * (1.0 - sgc))).astype(BF16)
        dh = in_proj_bwd(0, 2 * W)

        xrv = xr_ref[...]
        xrb, r, ig, sp, a, mult = _lru_gates(xrv, bda_ref, bdx_ref, vec_ref)
        h = h_ref[...]
        gl, dgl = _gelu_parts(z_ref[:, 3 * W:4 * W])
        dz_ref[:, 3 * W:4 * W] = (d_yr * h * dgl).astype(BF16)
        dh = dh + in_proj_bwd(3 * W, 4 * W)
        a_next = jnp.where(row == tm - 1, 1.0, pltpu.roll(a, tm - 1, 0))
        g_first = _scan_rows(a_next, d_yr * gl, gc[0:1, :], gbuf, reverse=True)
        g = gbuf[...]
        gc[0:1, :] = a[0:1, :] * g_first
        hprev = jnp.where(row == 0, halo_on * hh_ref[LRU_HALO - 1:LRU_HALO, :], pltpu.roll(h, 1, 0))
        d_log_a = (g * hprev) * a - (g * ig * xrv) * (a * a) / mult
        d_ig = g * mult * xrv
        d_xr = g * mult * ig
        spacc[0:1, :] += jnp.sum(d_log_a * r, axis=0, keepdims=True)
        d_pa32 = (d_log_a * (-LRU_C * sp)) * (r * (1.0 - r))
        d_px32 = d_ig * (ig * (1.0 - ig))
        accum(G_BA, d_pa32)
        accum(G_BX, d_px32)
        d_pa = d_pa32.astype(BF16)
        d_px = d_px32.astype(BF16)
        d_xr = d_xr + _dot(d_pa, bda_ref[...], NT) + _dot(d_px, bdx_ref[...], NT)
        dbda_ref[...] += _dot(xrb, d_pa, TN)
        dbdx_ref[...] += _dot(xrb, d_px, TN)
        accum(G_LCB, d_xr)
        rxbuf[0:LRU_HALO, :] = halo_on * zh_ref[CONV_HALO - LRU_HALO:CONV_HALO, 2 * W:3 * W]
        rxbuf[LRU_HALO:LRU_HALO + tm, :] = z_ref[:, 2 * W:3 * W]
        dxrbuf[0:tm, :] = d_xr
        d_rx = jnp.zeros((tm, W), F32)
        for k in range(LRU_K):
            off = LRU_HALO - (LRU_K - 1) + k
            accum(G_LW + k, d_xr * rxbuf[off:off + tm, :])
            d_rx = d_rx + lw_ref[k:k + 1, :] * dxrbuf[LRU_K - 1 - k:LRU_K - 1 - k + tm, :]
        dxrbuf[tm:tm + LRU_HALO, :] = dxrbuf[0:LRU_HALO, :]
        dz_ref[:, 2 * W:3 * W] = d_rx.astype(BF16)
        dh = dh + in_proj_bwd(2 * W, 3 * W)

        dx, dg = _rms_bwd(x1hat, x1rstd, gain, dh)
        dx1 = dx_ref[...] + dx
        dx1_ref[...] = dx1
        dob_ref[...] = (FFN_RES * dx1).astype(BF16)
        dg_ref[...] += dg

        @pl.when(first)
        def _():
            lam = vec_ref[V_LAM:V_LAM + 1, :]
            sg_ref[G_LAM:G_LAM + 1, :] = LRU_C * _sigmoid(-lam) * spacc[0:1, :]
            dwi_ref[...] = ai_ref[...].astype(BF16)
            dwo_ref[...] = ao_ref[...].astype(BF16)

    full = lambda a: pl.BlockSpec(a.shape, lambda i: (0,) * a.ndim)
    tile = lambda n: pl.BlockSpec((tm, n), lambda i: (nt - 1 - i, 0))
    halo = lambda rows, n: pl.BlockSpec(
        (rows, n), lambda i: (jnp.maximum((nt - 1 - i) * (tm // rows) - 1, 0), 0))
    const = lambda r, c: pl.BlockSpec((r, c), lambda i: (0, 0))
    return _call(
        body, name=name, grid=(nt,),
        in_specs=[tile(D), tile(4 * W), halo(CONV_HALO, 4 * W), tile(W), tile(W), tile(W), halo(LRU_HALO, W),
                  full(w_out), full(bda), full(bdx), full(cw), full(lw), full(vec),
                  tile(D), tile(2 * W), full(g), full(w_in_t)],
        out_specs=[tile(D), const(G_ROWS, W), const(W, W), const(W, W),
                   tile(D), const(1, D), const(4 * W, D), const(2 * W, D)],
        out_shape=[jax.ShapeDtypeStruct((T, D), F32), jax.ShapeDtypeStruct((G_ROWS, W), F32),
                   jax.ShapeDtypeStruct((W, W), F32), jax.ShapeDtypeStruct((W, W), F32),
                   jax.ShapeDtypeStruct((T, D), BF16), jax.ShapeDtypeStruct((1, D), F32),
                   jax.ShapeDtypeStruct((4 * W, D), BF16), jax.ShapeDtypeStruct((2 * W, D), BF16)],
        scratch_shapes=[pltpu.VMEM((tm + CONV_HALO, W), F32), pltpu.VMEM((tm + CONV_HALO, W), F32),
                        pltpu.VMEM((tm + LRU_HALO, W), F32), pltpu.VMEM((tm + LRU_HALO, W), F32),
                        pltpu.VMEM((tm, W), F32), pltpu.VMEM((8, W), F32), pltpu.VMEM((8, W), F32),
                        pltpu.VMEM((tm, 4 * W), BF16), pltpu.VMEM((4 * W, D), F32), pltpu.VMEM((2 * W, D), F32)],
        vmem_mib=60, args=(dx2, z, z, u1, xr, hst, hst, w_out, bda, bdx, cw, lw, vec, x1, mix, g, w_in_t),
        rider=rider)


def _pair_add(full, recv, name):
    K, _, _, rows, D = full.shape

    def body(c_ref, a_ref, b_ref, o_ref):
        o_ref[...] = (a_ref[...].astype(F32) + b_ref[...].astype(F32)).astype(BF16)

    c = lax.axis_index("c").astype(jnp.int32).reshape((1,))
    return _call(
        body, name=name, grid=(K, N_CHIP), num_scalar_prefetch=1,
        in_specs=[pl.BlockSpec((None, None, None, rows, D), lambda k, q, c_ref: (k, q, c_ref[0], 0, 0)),
                  pl.BlockSpec((None, None, rows, D), lambda k, q, c_ref: (k, q, 0, 0))],
        out_specs=pl.BlockSpec((None, None, rows, D), lambda k, q, c_ref: (k, q, 0, 0)),
        out_shape=jax.ShapeDtypeStruct(recv.shape, BF16),
        scratch_shapes=[], vmem_mib=16, args=(c, full, recv))


def _adamw_update(wv, gv, mv, vv):
    m2 = ADAM_B1 * mv + (1.0 - ADAM_B1) * gv
    v2 = ADAM_B2 * vv + (1.0 - ADAM_B2) * (gv * gv)
    m_hat = m2 / (1.0 - ADAM_B1 ** ADAM_STEP)
    v_hat = v2 / (1.0 - ADAM_B2 ** ADAM_STEP)
    return -ADAM_LR * (m_hat / (jnp.sqrt(v_hat) + ADAM_EPS) + ADAM_WD * wv), m2, v2


def _finish(parts, k, w, m, v, transpose, name):
    _, _, rows, D = parts.shape

    def body(p_ref, w_ref, m_ref, v_ref, g_ref, d_ref, mo_ref, vo_ref):
        acc = p_ref[0].astype(F32)
        for q in range(1, N_CHIP):
            acc = acc + p_ref[q].astype(F32)
        gv = acc.T if transpose else acc
        g_ref[...] = gv
        d_ref[...], mo_ref[...], vo_ref[...] = _adamw_update(w_ref[...], gv, m_ref[...], v_ref[...])

    whole = pl.BlockSpec(w.shape, lambda i: (0, 0))
    return _call(
        body, name=name, grid=(1,),
        in_specs=[pl.BlockSpec((None, N_CHIP, rows, D), lambda i: (k, 0, 0, 0)), whole, whole, whole],
        out_specs=[whole] * 4, out_shape=[pltpu.HBM(w.shape, F32)] * 4,
        scratch_shapes=[], vmem_mib=40, args=(parts, w, m, v))


def _adamw(w, g, m, v, name):
    def body(w_ref, g_ref, m_ref, v_ref, d_ref, mo_ref, vo_ref):
        d_ref[...], mo_ref[...], vo_ref[...] = _adamw_update(w_ref[...], g_ref[...], m_ref[...], v_ref[...])

    shape = jax.ShapeDtypeStruct(w.shape, F32)
    return pl.pallas_call(
        body, name=name,
        in_specs=[VMEM_SPEC] * 4, out_specs=[VMEM_SPEC] * 3, out_shape=[shape] * 3,
        compiler_params=pltpu.CompilerParams(vmem_limit_bytes=32 * MIB),
    )(w, g, m, v)


def _block_diag(w):
    h, d, _ = w.shape
    onto = jnp.eye(h, dtype=w.dtype)
    return (w[:, :, None, :] * onto[:, None, :, None]).reshape(h * d, h * d)


def _diag_blocks(m, h):
    d = m.shape[0] // h
    return jnp.stack([m[k * d:(k + 1) * d, k * d:(k + 1) * d] for k in range(h)])


def _reduce_level1(full, tag):
    got = _run_comm(_sibling_comm(full), "rs_sibling_" + tag)
    return [_pair_add(a, b, "rs_pair_add_%s%d" % (tag, n)) for n, (a, b) in enumerate(zip(full, got))]


def kernel(x, ffn1_norm, ffn1_w_gate, ffn1_w_up, ffn1_w_down, mix_norm, w_in, conv_dw, conv_dw_bias, conv_ln_g, conv_ln_b, lru_conv_w, lru_conv_b, lru_w_a, lru_b_a, lru_w_x, lru_b_x, lru_lambda, w_out, ffn2_norm, ffn2_w_gate, ffn2_w_up, ffn2_w_down, final_norm, loss_target, m_ffn1_norm, m_ffn1_w_gate, m_ffn1_w_up, m_ffn1_w_down, m_mix_norm, m_w_in, m_conv_dw, m_conv_dw_bias, m_conv_ln_g, m_conv_ln_b, m_lru_conv_w, m_lru_conv_b, m_lru_w_a, m_lru_b_a, m_lru_w_x, m_lru_b_x, m_lru_lambda, m_w_out, m_ffn2_norm, m_ffn2_w_gate, m_ffn2_w_up, m_ffn2_w_down, m_final_norm, v_ffn1_norm, v_ffn1_w_gate, v_ffn1_w_up, v_ffn1_w_down, v_mix_norm, v_w_in, v_conv_dw, v_conv_dw_bias, v_conv_ln_g, v_conv_ln_b, v_lru_conv_w, v_lru_conv_b, v_lru_w_a, v_lru_b_a, v_lru_w_x, v_lru_b_x, v_lru_lambda, v_w_out, v_ffn2_norm, v_ffn2_w_gate, v_ffn2_w_up, v_ffn2_w_down, v_final_norm):
    T, D = x.shape[1], x.shape[2]
    F = ffn1_w_down.shape[0] * N_DEV
    rf = ffn1_w_down.shape[0]
    ri = w_in.shape[1]
    ro = w_out.shape[0]
    W = conv_dw_bias.shape[0]
    wc = conv_dw.shape[1]
    H = lru_w_a.shape[0]
    xs = x.reshape(T, D)
    tgt = loss_target.reshape(T, D)
    tm_ffn = min(256, T)
    tm_fwd = min(512, T)
    cf = 256
    tm_w = min(1024, T)
    tm_w1 = min(2048, T)
    tm_mix = min(256, T)
    tf_w = F // 2
    row = lambda v: v.reshape(1, -1)
    by_owner = lambda a, rows: a.reshape(a.shape[0], N_CHIP, 2, rows, D)

    p3a, p3b, p_in, p_out = _prep_weights(
        (ffn1_w_gate.T, ffn1_w_up.T, ffn1_w_down), (ffn2_w_gate.T, ffn2_w_up.T, ffn2_w_down), w_in, w_out,
        "prep_weights")
    tile_rows = lambda a: jnp.pad(a, ((0, -a.shape[0] % SUBLANES), (0, 0)))
    p_cw = jnp.concatenate([tile_rows(conv_dw), tile_rows(lru_conv_w)], axis=0)
    lw_row = p_cw.shape[0] - SUBLANES
    stacked = lambda r, j: r.at[:, j]
    plain = lambda r, j: r.at[j]
    g3_shape = jax.ShapeDtypeStruct((3, N_DEV, rf, D), BF16)
    (g3a,) = _all_gather([p3a], [stacked], [g3_shape], "ag_ffn1")
    w3a = g3a.reshape(3, F, D)
    bda = _block_diag(lru_w_a).astype(BF16)
    bdx = _block_diag(lru_w_x).astype(BF16)
    vec = jnp.concatenate([tile_rows(v[None]) for v in
                           (conv_dw_bias, conv_ln_g, conv_ln_b, lru_conv_b, lru_b_a, lru_b_x, lru_lambda)], axis=0)

    gather_rest = _gather_comm(
        [p3b, p_in, p_out, p_cw], [stacked, plain, plain, plain],
        [g3_shape, jax.ShapeDtypeStruct((N_DEV, ri, D), BF16), jax.ShapeDtypeStruct((N_DEV, ro, D), BF16),
         jax.ShapeDtypeStruct((N_DEV,) + p_cw.shape, F32)],
        [(SIBLING,) + SAME_CORE, EVERYONE, EVERYONE, EVERYONE])
    (x1, h1, dau1, dag1, act1), (g3b_half, g_in, g_out, g_cw) = _ffn_fwd(
        xs, row(ffn1_norm), w3a, tm_fwd, cf, "ffn1_fwd", rider=gather_rest)
    w_in_t = g_in.reshape(N_DEV * ri, D)
    w_out_f = g_out.reshape(N_DEV * ro, D)
    cw_all = jnp.transpose(g_cw, (1, 0, 2)).reshape(p_cw.shape[0], N_DEV * wc)
    cw = cw_all[0:CONV_K]
    lw = cw_all[lw_row:lw_row + LRU_K]
    (x2, z, mix, u1, xr, hst), (g3b,) = _mix_core_fwd(
        x1, row(mix_norm), w_in_t, w_out_f, bda, bdx, cw, lw, vec, tm_mix, "mix_core_fwd",
        rider=_forward_comm([g3b_half], [stacked]))
    w3b = g3b.reshape(3, F, D)
    dx3, dob2, d_final_norm, loss_part, h3, dau2, dag2, act2 = _ffn_fwd(
        x2, row(ffn2_norm), w3b, tm_fwd, cf, "ffn2_fwd_loss", head=(row(final_norm), tgt))

    dx2, dgate2, dup2, d_ffn2_norm = _ffn_dgrad(dx3, x2, row(ffn2_norm), dau2, dag2, w3b, tm_ffn, cf, "ffn2_dgrad")
    (dw_gu2,) = _wgrad([dgate2, dup2], h3, tm_w, tf_w, "ffn2_wgrad_gu")
    (dw_d2,) = _wgrad([act2], dob2, tm_w1, tf_w, "ffn2_wgrad_d")
    sums_f2 = _reduce_level1([by_owner(dw_gu2, rf), by_owner(dw_d2, rf)], "f2")
    (dx1, sg, dbda, dbdx, dob1, d_mix_norm, dw_in_t, dw_out), parts_f2 = _mix_bwd(
        dx2, z, u1, xr, hst, x1, mix, row(mix_norm), w_in_t, w_out_f, bda, bdx, cw, lw, vec, tm_mix, "mix_bwd",
        rider=_chips_comm(sums_f2))
    sums_io = _reduce_level1([by_owner(dw_in_t[None], ri), by_owner(dw_out[None], ro)], "io")
    (dw_d1,), parts_io = _wgrad([act1], dob1, tm_w1, tf_w, "ffn1_wgrad_d", rider=_chips_comm(sums_io))
    sums_d1 = _reduce_level1([by_owner(dw_d1, rf)], "d1")
    dx0, dgate1, dup1, d_ffn1_norm = _ffn_dgrad(dx1, xs, row(ffn1_norm), dau1, dag1, w3a, tm_ffn, cf, "ffn1_dgrad")

    d_lru_w_a = _diag_blocks(dbda, H)
    d_lru_w_x = _diag_blocks(dbdx, H)
    small = [d_ffn1_norm, d_mix_norm, d_ffn2_norm, d_final_norm,
             sg[G_CB], sg[G_LNG], sg[G_LNB], sg[G_LCB], sg[G_BA], sg[G_BX], sg[G_LAM],
             d_lru_w_a, d_lru_w_x,
             sg[G_CW:G_CW + CONV_K], sg[G_LW:G_LW + LRU_K],
             loss_part]
    sizes = [a.size for a in small]
    flat = jnp.concatenate([a.reshape(-1) for a in small])
    n_rows = -(-flat.size // (8 * D)) * 8
    packed = jnp.concatenate([flat, jnp.zeros((n_rows * D - flat.size,), F32)]).reshape(n_rows, D)
    (dw_g1,), (red, *parts_d1) = _wgrad(
        [dgate1], h1, tm_w1, tf_w, "ffn1_wgrad_g", rider=_both(_small_sum_comm(packed), _chips_comm(sums_d1)))
    sums_g1 = _reduce_level1([by_owner(dw_g1, rf)], "g1")
    (dw_u1,), parts_g1 = _wgrad([dup1], h1, tm_w1, tf_w, "ffn1_wgrad_u", rider=_chips_comm(sums_g1))
    sums_u1 = _reduce_level1([by_owner(dw_u1, rf)], "u1")
    parts_u1 = _run_comm(_chips_comm(sums_u1), "rs_chips_u1")

    red = red.reshape(-1)
    offs = [0]
    for n in sizes:
        offs.append(offs[-1] + n)
    piece = lambda k, shape: red[offs[k]:offs[k + 1]].reshape(shape)
    loss = red[offs[15]]

    me = 4 * lax.axis_index("x") + 2 * lax.axis_index("y") + lax.axis_index("c")
    chan = lambda full_g: lax.dynamic_slice_in_dim(full_g, me * wc, wc, axis=1)
    grads = {
        "ffn1_norm": piece(0, (D,)), "mix_norm": piece(1, (D,)), "ffn2_norm": piece(2, (D,)), "final_norm": piece(3, (D,)),
        "conv_dw_bias": piece(4, (W,)), "conv_ln_g": piece(5, (W,)), "conv_ln_b": piece(6, (W,)),
        "lru_conv_b": piece(7, (W,)), "lru_b_a": piece(8, (W,)), "lru_b_x": piece(9, (W,)), "lru_lambda": piece(10, (W,)),
        "lru_w_a": piece(11, lru_w_a.shape), "lru_w_x": piece(12, lru_w_x.shape),
        "conv_dw": chan(piece(13, (CONV_K, W))), "lru_conv_w": chan(piece(14, (LRU_K, W))),
    }

    weights = dict(ffn1_norm=ffn1_norm, ffn1_w_gate=ffn1_w_gate, ffn1_w_up=ffn1_w_up, ffn1_w_down=ffn1_w_down, mix_norm=mix_norm, w_in=w_in, conv_dw=conv_dw, conv_dw_bias=conv_dw_bias, conv_ln_g=conv_ln_g, conv_ln_b=conv_ln_b, lru_conv_w=lru_conv_w, lru_conv_b=lru_conv_b, lru_w_a=lru_w_a, lru_b_a=lru_b_a, lru_w_x=lru_w_x, lru_b_x=lru_b_x, lru_lambda=lru_lambda, w_out=w_out, ffn2_norm=ffn2_norm, ffn2_w_gate=ffn2_w_gate, ffn2_w_up=ffn2_w_up, ffn2_w_down=ffn2_w_down, final_norm=final_norm)
    moment1 = dict(ffn1_norm=m_ffn1_norm, ffn1_w_gate=m_ffn1_w_gate, ffn1_w_up=m_ffn1_w_up, ffn1_w_down=m_ffn1_w_down, mix_norm=m_mix_norm, w_in=m_w_in, conv_dw=m_conv_dw, conv_dw_bias=m_conv_dw_bias, conv_ln_g=m_conv_ln_g, conv_ln_b=m_conv_ln_b, lru_conv_w=m_lru_conv_w, lru_conv_b=m_lru_conv_b, lru_w_a=m_lru_w_a, lru_b_a=m_lru_b_a, lru_w_x=m_lru_w_x, lru_b_x=m_lru_b_x, lru_lambda=m_lru_lambda, w_out=m_w_out, ffn2_norm=m_ffn2_norm, ffn2_w_gate=m_ffn2_w_gate, ffn2_w_up=m_ffn2_w_up, ffn2_w_down=m_ffn2_w_down, final_norm=m_final_norm)
    moment2 = dict(ffn1_norm=v_ffn1_norm, ffn1_w_gate=v_ffn1_w_gate, ffn1_w_up=v_ffn1_w_up, ffn1_w_down=v_ffn1_w_down, mix_norm=v_mix_norm, w_in=v_w_in, conv_dw=v_conv_dw, conv_dw_bias=v_conv_dw_bias, conv_ln_g=v_conv_ln_g, conv_ln_b=v_conv_ln_b, lru_conv_w=v_lru_conv_w, lru_conv_b=v_lru_conv_b, lru_w_a=v_lru_w_a, lru_b_a=v_lru_b_a, lru_w_x=v_lru_w_x, lru_b_x=v_lru_b_x, lru_lambda=v_lru_lambda, w_out=v_w_out, ffn2_norm=v_ffn2_norm, ffn2_w_gate=v_ffn2_w_gate, ffn2_w_up=v_ffn2_w_up, ffn2_w_down=v_ffn2_w_down, final_norm=v_final_norm)
    order = list(weights)
    big = {"ffn1_w_gate": (parts_g1[0], 0, True), "ffn1_w_up": (parts_u1[0], 0, True),
           "ffn1_w_down": (parts_d1[0], 0, False), "w_in": (parts_io[0], 0, True), "w_out": (parts_io[1], 0, False),
           "ffn2_w_gate": (parts_f2[0], 0, True), "ffn2_w_up": (parts_f2[0], 1, True),
           "ffn2_w_down": (parts_f2[1], 0, False)}
    delta, new_m, new_v = {}, {}, {}
    for n, (parts, k, d_major) in big.items():
        operands = weights[n], moment1[n], moment2[n]
        if d_major and n != "w_in":
            results = _finish(parts, k, *[a.T for a in operands], False, "finish_" + n)
            grads[n], delta[n], new_m[n], new_v[n] = [r.T for r in results]
        else:
            grads[n], delta[n], new_m[n], new_v[n] = _finish(parts, k, *operands, d_major, "finish_" + n)
    rest = [n for n in order if n not in big]
    rest_sizes = [weights[n].size for n in rest]
    n_flat = sum(rest_sizes)
    pad_rows = -(-n_flat // (8 * 128)) * 8

    def pack(d):
        f = jnp.concatenate([d[n].reshape(-1) for n in rest])
        return jnp.concatenate([f, jnp.zeros((pad_rows * 128 - n_flat,), F32)]).reshape(pad_rows, 128)

    d_s, m_s, v_s = _adamw(pack(weights), pack(grads), pack(moment1), pack(moment2), "adamw_small")
    off = 0
    for n, size in zip(rest, rest_sizes):
        shape = weights[n].shape
        delta[n] = d_s.reshape(-1)[off:off + size].reshape(shape)
        new_m[n] = m_s.reshape(-1)[off:off + size].reshape(shape)
        new_v[n] = v_s.reshape(-1)[off:off + size].reshape(shape)
        off += size

    return (loss, dx0.reshape(x.shape), *[grads[n] for n in order], *[delta[n] for n in order],
            *[new_m[n] for n in order], *[new_v[n] for n in order])
```

```python
import functools
import math

import jax
import jax.numpy as jnp
from jax import lax
from jax.experimental import pallas as pl
from jax.experimental.pallas import tpu as pltpu

F32 = jnp.float32
BF16 = jnp.bfloat16
MESH = pl.DeviceIdType.MESH

N_DEV = 8
N_CHIP = 4
SUBLANES = 8
RMS_EPS = 1e-6
LN_EPS = 1e-5
LRU_C = 8.0
CONV_K = 31
LRU_K = 4
CONV_HALO = 32
LRU_HALO = 8
FFN_RES = 0.5
ADAM_LR, ADAM_B1, ADAM_B2, ADAM_EPS, ADAM_WD, ADAM_STEP = 0.001, 0.9, 0.999, 1e-08, 0.01, 10
GELU_K = math.sqrt(2.0 / math.pi)
GELU_C = 0.044715

MIB = 1024 * 1024
NT = (((1,), (1,)), ((), ()))
NN = (((1,), (0,)), ((), ()))
TN = (((0,), (0,)), ((), ()))

V_CB, V_LNG, V_LNB, V_LCB, V_BA, V_BX, V_LAM = range(0, 7 * SUBLANES, SUBLANES)
G_CW = 0
G_CB, G_LNG, G_LNB = 31, 32, 33
G_LW = 34
G_LCB, G_BA, G_BX, G_LAM = 38, 39, 40, 41
G_ROWS = 48

HBM_SPEC = pl.BlockSpec(memory_space=pltpu.HBM)
VMEM_SPEC = pl.BlockSpec(memory_space=pltpu.VMEM)


def _dot(a, b, dims):
    return lax.dot_general(a, b, dims, preferred_element_type=F32)


def _sigmoid(x):
    return 1.0 / (1.0 + jnp.exp(-x))


def _gelu_parts(x):
    x2 = x * x
    th = jnp.tanh(GELU_K * x * (1.0 + GELU_C * x2))
    gl = 0.5 * x * (1.0 + th)
    dgl = 0.5 * (1.0 + th) + 0.5 * x * (1.0 - th * th) * GELU_K * (1.0 + 3.0 * GELU_C * x2)
    return gl, dgl


def _neg_expm1(y):
    series = -y * (1.0 + y * (1.0 / 2) * (1.0 + y * (1.0 / 3) * (1.0 + y * (1.0 / 4) * (1.0 + y * (1.0 / 5) * (1.0 + y * (1.0 / 6))))))
    return jnp.where(y > -0.25, series, 1.0 - jnp.exp(y))


def _softplus_neg(lam):
    t = -lam
    e = jnp.exp(-jnp.abs(t))
    s = 1.0 + e
    log1p_e = jnp.log(s) - ((s - 1.0) - e) / s
    return jnp.maximum(t, 0.0) + log1p_e


def _rms_stats(xv):
    rstd = lax.rsqrt(jnp.mean(xv * xv, axis=-1, keepdims=True) + RMS_EPS)
    return xv * rstd, rstd


def _rms_bwd(xhat, rstd, g, dh):
    dxhat = dh * g
    dx = rstd * (dxhat - xhat * jnp.mean(dxhat * xhat, axis=-1, keepdims=True))
    return dx, jnp.sum(dh * xhat, axis=0, keepdims=True)


def _row_windows(buf_ref, n_rows, offsets):
    total = buf_ref.shape[0]
    full = buf_ref[...]
    for b in range(SUBLANES):
        offs = [o for o in offsets if o % SUBLANES == b]
        if not offs:
            continue
        assert max(offs) + n_rows <= total
        moved = full if b == 0 else pltpu.roll(full, total - b, 0)
        for o in offs:
            yield o, moved[o - b:o - b + n_rows, :]


def _scan_rows(av, bv, edge, out_ref, reverse=False):
    tm, W = av.shape
    sub = lax.broadcasted_iota(jnp.int32, (tm, W), 0) % SUBLANES
    s = 1
    while s < SUBLANES:
        keep = (sub < SUBLANES - s) if reverse else (sub >= s)
        shift = tm - s if reverse else s
        bv = jnp.where(keep, av * pltpu.roll(bv, shift, 0) + bv, bv)
        av = jnp.where(keep, av * pltpu.roll(av, shift, 0), av)
        s *= 2
    starts = range(0, tm, SUBLANES)
    for r0 in (reversed(starts) if reverse else starts):
        group = av[r0:r0 + SUBLANES, :] * edge + bv[r0:r0 + SUBLANES, :]
        out_ref[r0:r0 + SUBLANES, :] = group
        edge = group[0:1, :] if reverse else group[SUBLANES - 1:SUBLANES, :]
    return edge


class _Comm:
    def __init__(self, arrays, in_specs, out_shapes, out_specs, scratch, start, wait, aliases=None):
        self.arrays, self.in_specs = list(arrays), list(in_specs)
        self.out_shapes, self.out_specs = list(out_shapes), list(out_specs)
        self.scratch, self.start, self.wait = list(scratch), start, wait
        self.aliases = dict(aliases or {})


def _in_hbm(a):
    return pltpu.with_memory_space_constraint(a, pltpu.HBM)


def _operands(comm):
    return [a if spec is VMEM_SPEC else _in_hbm(a) for a, spec in zip(comm.arrays, comm.in_specs)]


def _call(body, *, name, grid, in_specs, out_specs, out_shape, scratch_shapes, vmem_mib, args, rider=None,
          num_scalar_prefetch=0):
    params = pltpu.CompilerParams(dimension_semantics=("arbitrary",) * len(grid), vmem_limit_bytes=vmem_mib * MIB)
    args = [a if k < num_scalar_prefetch else _in_hbm(a) for k, a in enumerate(args)]
    if rider is None:
        return pl.pallas_call(
            body, name=name,
            grid_spec=pltpu.PrefetchScalarGridSpec(
                num_scalar_prefetch=num_scalar_prefetch, grid=grid, in_specs=in_specs, out_specs=out_specs,
                scratch_shapes=scratch_shapes),
            out_shape=out_shape, compiler_params=params)(*args)
    assert num_scalar_prefetch == 0
    n_in, n_out, n_scr = len(in_specs), len(out_specs), len(scratch_shapes)
    r_in, r_out = len(rider.arrays), len(rider.out_shapes)
    n_axes = len(grid)

    def carried(*refs):
        pos = [0]

        def take(n):
            pos[0] += n
            return refs[pos[0] - n:pos[0]]

        ins, r_ins, outs, r_outs, scr, r_scr = take(n_in), take(r_in), take(n_out), take(r_out), take(n_scr), take(len(rider.scratch))
        first = pl.program_id(0) == 0
        last = pl.program_id(0) == grid[0] - 1
        for ax in range(1, n_axes):
            first = first & (pl.program_id(ax) == 0)
            last = last & (pl.program_id(ax) == grid[ax] - 1)

        @pl.when(first)
        def _():
            rider.start(r_ins, r_outs, r_scr)

        body(*ins, *outs, *scr)

        @pl.when(last)
        def _():
            rider.wait(r_ins, r_outs, r_scr)

    res = pl.pallas_call(
        carried, name=name,
        grid=grid,
        in_specs=list(in_specs) + rider.in_specs,
        out_specs=list(out_specs) + rider.out_specs,
        out_shape=list(out_shape) + rider.out_shapes,
        scratch_shapes=list(scratch_shapes) + rider.scratch,
        input_output_aliases={n_in + i: n_out + o for i, o in rider.aliases.items()},
        compiler_params=params)(*args, *_operands(rider))
    return res[:n_out], res[n_out:]


def _run_comm(comm, name):
    n_in, n_out = len(comm.arrays), len(comm.out_shapes)

    def body(*refs):
        ins, outs, scr = refs[:n_in], refs[n_in:n_in + n_out], refs[n_in + n_out:]
        comm.start(ins, outs, scr)
        comm.wait(ins, outs, scr)

    return pl.pallas_call(
        body, name=name,
        in_specs=comm.in_specs, out_specs=comm.out_specs, out_shape=comm.out_shapes,
        scratch_shapes=comm.scratch, input_output_aliases=comm.aliases,
        compiler_params=pltpu.CompilerParams(vmem_limit_bytes=24 * MIB))(*_operands(comm))


def _both(a, b):
    ni, no, ns = len(a.arrays), len(a.out_shapes), len(a.scratch)

    def start(ins, outs, scr):
        a.start(ins[:ni], outs[:no], scr[:ns])
        b.start(ins[ni:], outs[no:], scr[ns:])

    def wait(ins, outs, scr):
        a.wait(ins[:ni], outs[:no], scr[:ns])
        b.wait(ins[ni:], outs[no:], scr[ns:])

    aliases = dict(a.aliases)
    aliases.update({ni + i: no + o for i, o in b.aliases.items()})
    return _Comm(a.arrays + b.arrays, a.in_specs + b.in_specs, a.out_shapes + b.out_shapes,
                 a.out_specs + b.out_specs, a.scratch + b.scratch, start, wait, aliases)


def _place():
    return lax.axis_index("x"), lax.axis_index("y"), lax.axis_index("c")


def _peer(k):
    x, y, c = _place()
    px, py, pc = x ^ ((k >> 2) & 1), y ^ ((k >> 1) & 1), c ^ (k & 1)
    return (px, py, pc), 4 * px + 2 * py + pc


SIBLING = 1
SAME_CORE = (2, 4, 6)
EVERYONE = tuple(range(1, N_DEV))


def _gather_comm(shards, views, out_shapes, relations):
    na = len(shards)

    def copies(ins, outs, scr):
        send_sems, recv_sems, _ = scr
        _, me = _peer(0)
        out = []
        for a in range(na):
            for k in relations[a]:
                peer, theirs = _peer(k)
                send = functools.partial(
                    pltpu.make_async_remote_copy,
                    src_ref=ins[a], dst_ref=views[a](outs[a], me),
                    send_sem=send_sems.at[7 * a + k - 1], recv_sem=recv_sems.at[7 * a + k - 1],
                    device_id=peer, device_id_type=MESH)
                recv = functools.partial(
                    pltpu.make_async_remote_copy,
                    src_ref=ins[a], dst_ref=views[a](outs[a], theirs),
                    send_sem=send_sems.at[7 * a + k - 1], recv_sem=recv_sems.at[7 * a + k - 1],
                    device_id=peer, device_id_type=MESH)
                out.append((send, recv))
        return out

    def local(ins, outs, scr):
        _, me = _peer(0)
        return [pltpu.make_async_copy(ins[a], views[a](outs[a], me), scr[2].at[a]) for a in range(na)]

    def start(ins, outs, scr):
        for cp in local(ins, outs, scr):
            cp.start()
        for send, _ in copies(ins, outs, scr):
            send().start()

    def wait(ins, outs, scr):
        for _, recv in copies(ins, outs, scr):
            recv().wait_recv()
        for send, _ in copies(ins, outs, scr):
            send().wait_send()
        for cp in local(ins, outs, scr):
            cp.wait()

    return _Comm(shards, [HBM_SPEC] * na, out_shapes, [HBM_SPEC] * na,
                 [pltpu.SemaphoreType.DMA((7 * na,)), pltpu.SemaphoreType.DMA((7 * na,)),
                  pltpu.SemaphoreType.DMA((na,))], start, wait)


def _forward_comm(gathered, views):
    na = len(gathered)
    shapes = [jax.ShapeDtypeStruct(g.shape, g.dtype) for g in gathered]

    def copies(outs, scr):
        send_sems, recv_sems = scr
        sibling, _ = _peer(SIBLING)
        out = []
        for a in range(na):
            for n, k in enumerate(SAME_CORE):
                _, mine = _peer(k)
                _, theirs = _peer(k ^ SIBLING)
                send = functools.partial(
                    pltpu.make_async_remote_copy,
                    src_ref=views[a](outs[a], mine), dst_ref=views[a](outs[a], mine),
                    send_sem=send_sems.at[3 * a + n], recv_sem=recv_sems.at[3 * a + n],
                    device_id=sibling, device_id_type=MESH)
                recv = functools.partial(
                    pltpu.make_async_remote_copy,
                    src_ref=views[a](outs[a], mine), dst_ref=views[a](outs[a], theirs),
                    send_sem=send_sems.at[3 * a + n], recv_sem=recv_sems.at[3 * a + n],
                    device_id=sibling, device_id_type=MESH)
                out.append((send, recv))
        return out

    def start(ins, outs, scr):
        for send, _ in copies(outs, scr):
            send().start()

    def wait(ins, outs, scr):
        for _, recv in copies(outs, scr):
            recv().wait_recv()
        for send, _ in copies(outs, scr):
            send().wait_send()

    return _Comm(gathered, [HBM_SPEC] * na, shapes, [HBM_SPEC] * na,
                 [pltpu.SemaphoreType.DMA((3 * na,)), pltpu.SemaphoreType.DMA((3 * na,))], start, wait,
                 aliases={a: a for a in range(na)})


def _all_gather(shards, views, out_shapes, name):
    na = len(shards)
    level1 = _gather_comm(shards, views, out_shapes, [(SIBLING,) + SAME_CORE] * na)

    def body(*refs):
        ins, outs = refs[:na], refs[na:2 * na]
        send_sems, recv_sems, local_sems, fwd_send, fwd_recv = refs[2 * na:]
        sibling, _ = _peer(SIBLING)
        scr = (send_sems, recv_sems, local_sems)
        level1.start(ins, outs, scr)
        passed, landing = [], []
        for a in range(na):
            for n, k in enumerate(SAME_CORE):
                peer, mine = _peer(k)
                _, theirs = _peer(k ^ SIBLING)
                pltpu.make_async_remote_copy(
                    src_ref=ins[a], dst_ref=views[a](outs[a], mine),
                    send_sem=send_sems.at[7 * a + k - 1], recv_sem=recv_sems.at[7 * a + k - 1],
                    device_id=peer, device_id_type=MESH).wait_recv()
                fwd = pltpu.make_async_remote_copy(
                    src_ref=views[a](outs[a], mine), dst_ref=views[a](outs[a], mine),
                    send_sem=fwd_send.at[3 * a + n], recv_sem=fwd_recv.at[3 * a + n],
                    device_id=sibling, device_id_type=MESH)
                fwd.start()
                passed.append(fwd)
                landing.append(pltpu.make_async_remote_copy(
                    src_ref=views[a](outs[a], mine), dst_ref=views[a](outs[a], theirs),
                    send_sem=fwd_send.at[3 * a + n], recv_sem=fwd_recv.at[3 * a + n],
                    device_id=sibling, device_id_type=MESH))
        for a in range(na):
            _, theirs = _peer(SIBLING)
            pltpu.make_async_remote_copy(
                src_ref=ins[a], dst_ref=views[a](outs[a], theirs),
                send_sem=send_sems.at[7 * a + SIBLING - 1], recv_sem=recv_sems.at[7 * a + SIBLING - 1],
                device_id=sibling, device_id_type=MESH).wait_recv()
        for cp in landing:
            cp.wait_recv()
        for cp in passed:
            cp.wait_send()
        _, me = _peer(0)
        for a in range(na):
            for k in (SIBLING,) + SAME_CORE:
                peer, _ = _peer(k)
                pltpu.make_async_remote_copy(
                    src_ref=ins[a], dst_ref=views[a](outs[a], me),
                    send_sem=send_sems.at[7 * a + k - 1], recv_sem=recv_sems.at[7 * a + k - 1],
                    device_id=peer, device_id_type=MESH).wait_send()
            pltpu.make_async_copy(ins[a], views[a](outs[a], me), local_sems.at[a]).wait()

    return pl.pallas_call(
        body, name=name,
        in_specs=[HBM_SPEC] * na, out_specs=[HBM_SPEC] * na, out_shape=out_shapes,
        scratch_shapes=level1.scratch + [pltpu.SemaphoreType.DMA((3 * na,)), pltpu.SemaphoreType.DMA((3 * na,))],
    )(*[_in_hbm(s) for s in shards])


def _sibling_comm(grads):
    na = len(grads)
    shapes = [jax.ShapeDtypeStruct(g.shape[:2] + g.shape[3:], g.dtype) for g in grads]

    def copies(ins, outs, scr):
        x, y, c = _place()
        return [pltpu.make_async_remote_copy(
            src_ref=ins[a].at[:, :, 1 - c], dst_ref=outs[a],
            send_sem=scr[0].at[a], recv_sem=scr[1].at[a],
            device_id=(x, y, 1 - c), device_id_type=MESH) for a in range(na)]

    def start(ins, outs, scr):
        for cp in copies(ins, outs, scr):
            cp.start()

    def wait(ins, outs, scr):
        for cp in copies(ins, outs, scr):
            cp.wait()

    return _Comm(grads, [HBM_SPEC] * na, shapes, [HBM_SPEC] * na,
                 [pltpu.SemaphoreType.DMA((na,)), pltpu.SemaphoreType.DMA((na,))], start, wait)


def _chips_comm(sums):
    na = len(sums)
    shapes = [jax.ShapeDtypeStruct(s.shape, s.dtype) for s in sums]

    def copies(ins, outs, scr):
        x, y, c = _place()
        mine = 2 * x + y
        out = []
        for a in range(na):
            for n, k in enumerate(SAME_CORE):
                (px, py, pc), _ = _peer(k)
                theirs = 2 * px + py
                send = functools.partial(
                    pltpu.make_async_remote_copy,
                    src_ref=ins[a].at[:, theirs], dst_ref=outs[a].at[:, mine],
                    send_sem=scr[0].at[3 * a + n], recv_sem=scr[1].at[3 * a + n],
                    device_id=(px, py, pc), device_id_type=MESH)
                recv = functools.partial(
                    pltpu.make_async_remote_copy,
                    src_ref=ins[a].at[:, mine], dst_ref=outs[a].at[:, theirs],
                    send_sem=scr[0].at[3 * a + n], recv_sem=scr[1].at[3 * a + n],
                    device_id=(px, py, pc), device_id_type=MESH)
                out.append((send, recv))
        return out

    def local(ins, outs, scr):
        x, y, _ = _place()
        mine = 2 * x + y
        return [pltpu.make_async_copy(ins[a].at[:, mine], outs[a].at[:, mine], scr[2].at[a]) for a in range(na)]

    def start(ins, outs, scr):
        for cp in local(ins, outs, scr):
            cp.start()
        for send, _ in copies(ins, outs, scr):
            send().start()

    def wait(ins, outs, scr):
        for _, recv in copies(ins, outs, scr):
            recv().wait_recv()
        for send, _ in copies(ins, outs, scr):
            send().wait_send()
        for cp in local(ins, outs, scr):
            cp.wait()

    return _Comm(sums, [HBM_SPEC] * na, shapes, [HBM_SPEC] * na,
                 [pltpu.SemaphoreType.DMA((3 * na,)), pltpu.SemaphoreType.DMA((3 * na,)),
                  pltpu.SemaphoreType.DMA((na,))], start, wait)


def _small_sum_comm(arrays):
    na = len(arrays)

    def copies(ins, scr):
        bufs, send_sems, recv_sems = scr[:na], scr[na], scr[na + 1]
        _, me = _peer(0)
        out = []
        for a in range(na):
            for k in EVERYONE:
                peer, theirs = _peer(k)
                sems = dict(send_sem=send_sems.at[7 * a + k - 1], recv_sem=recv_sems.at[7 * a + k - 1])
                send = functools.partial(
                    pltpu.make_async_remote_copy,
                    src_ref=ins[a], dst_ref=bufs[a].at[me], device_id=peer, device_id_type=MESH, **sems)
                recv = functools.partial(
                    pltpu.make_async_remote_copy,
                    src_ref=ins[a], dst_ref=bufs[a].at[theirs], device_id=peer, device_id_type=MESH, **sems)
                out.append((send, recv))
        return out

    def start(ins, outs, scr):
        _, me = _peer(0)
        for a in range(na):
            scr[a][me] = ins[a][...]
        for send, _ in copies(ins, scr):
            send().start()

    def wait(ins, outs, scr):
        for _, recv in copies(ins, scr):
            recv().wait_recv()
        for send, _ in copies(ins, scr):
            send().wait_send()
        for a in range(na):
            acc = scr[a][0]
            for j in range(1, N_DEV):
                acc = acc + scr[a][j]
            outs[a][...] = acc

    return _Comm(arrays, [VMEM_SPEC] * na, [jax.ShapeDtypeStruct(s.shape, F32) for s in arrays], [VMEM_SPEC] * na,
                 [pltpu.VMEM((N_DEV,) + s.shape, F32) for s in arrays]
                 + [pltpu.SemaphoreType.DMA((7 * na,)), pltpu.SemaphoreType.DMA((7 * na,))], start, wait)


def _prep_weights(ffn1, ffn2, w_in, w_out, name):
    rf, D = ffn1[2].shape
    ri, ro = w_in.shape[1], w_out.shape[0]

    def body(g1, u1, d1, g2, u2, d2, wi, wo, p1_ref, p2_ref, pi_ref, po_ref):
        for p_ref, shards in ((p1_ref, (g1, u1, d1)), (p2_ref, (g2, u2, d2))):
            for k, shard in enumerate(shards):
                p_ref[k] = shard[...].astype(BF16)
        pi_ref[...] = wi[...].T.astype(BF16)
        po_ref[...] = wo[...].astype(BF16)

    args = (*ffn1, *ffn2, w_in, w_out)
    whole = lambda shape: pl.BlockSpec(shape, lambda i: (0,) * len(shape))
    out_shapes = [(3, rf, D), (3, rf, D), (ri, D), (ro, D)]
    return _call(
        body, name=name, grid=(1,),
        in_specs=[whole(a.shape) for a in args], out_specs=[whole(s) for s in out_shapes],
        out_shape=[jax.ShapeDtypeStruct(s, BF16) for s in out_shapes],
        scratch_shapes=[], vmem_mib=48, args=args)


def _load_weights(w_hbm, w_vmem, sem):
    @pl.when(pl.program_id(0) == 0)
    def _():
        copies = [pltpu.make_async_copy(w_hbm.at[k], w_vmem.at[k], sem.at[k]) for k in range(3)]
        for cp in copies:
            cp.start()
        for cp in copies:
            cp.wait()


def _ffn_fwd(x, g, w3, tm, cf, name, rider=None, head=None):
    T, D = x.shape
    F = w3.shape[1]
    n_head = 0 if head is None else 2

    def body(x_ref, g_ref, w_hbm, *refs):
        head_refs, refs = refs[:n_head], refs[n_head:]
        if head is None:
            (xo_ref, h_ref, dau_ref, dag_ref, act_ref, wv, sem) = refs
        else:
            (dx_ref, dob_ref, dgf_ref, loss_ref, h_ref, dau_ref, dag_ref, act_ref, wv, sem) = refs
        _load_weights(w_hbm, wv, sem)
        xhat, _ = _rms_stats(x_ref[...])
        hb = (xhat * g_ref[...]).astype(BF16)
        h_ref[...] = hb
        for lo in range(0, F, cf):
            gate = _dot(hb, wv[0, lo:lo + cf, :], NT)
            up = _dot(hb, wv[1, lo:lo + cf, :], NT)
            sig = _sigmoid(gate)
            silu = gate * sig
            dau_ref[:, lo:lo + cf] = silu.astype(BF16)
            dag_ref[:, lo:lo + cf] = (up * (sig * (1.0 + gate * (1.0 - sig)))).astype(BF16)
            act_ref[:, lo:lo + cf] = (silu * up).astype(BF16)
        x_out = x_ref[...] + FFN_RES * _dot(act_ref[...], wv[2], NN)
        if head is None:
            xo_ref[...] = x_out
            return

        @pl.when(pl.program_id(0) == 0)
        def _():
            dgf_ref[...] = jnp.zeros_like(dgf_ref)
            loss_ref[...] = jnp.zeros_like(loss_ref)

        gf_ref, tgt_ref = head_refs
        yhat, rstd = _rms_stats(x_out)
        gf = gf_ref[...]
        err = yhat * gf - tgt_ref[...]
        loss_ref[...] += (0.5 / D) * jnp.sum(err * err)
        dx, dgf = _rms_bwd(yhat, rstd, gf, err * (1.0 / D))
        dx_ref[...] = dx
        dob_ref[...] = (FFN_RES * dx).astype(BF16)
        dgf_ref[...] += dgf

    row = pl.BlockSpec((tm, D), lambda i: (i, 0))
    hid = pl.BlockSpec((tm, F), lambda i: (i, 0))
    vec = pl.BlockSpec((1, D), lambda i: (0, 0))
    row_f32, row_bf16 = jax.ShapeDtypeStruct((T, D), F32), jax.ShapeDtypeStruct((T, D), BF16)
    if head is None:
        first_specs, first_shapes = [row], [row_f32]
    else:
        first_specs = [row, row, vec, pl.BlockSpec((1, 128), lambda i: (0, 0))]
        first_shapes = [row_f32, row_bf16, jax.ShapeDtypeStruct((1, D), F32), jax.ShapeDtypeStruct((1, 128), F32)]
    return _call(
        body, name=name, grid=(T // tm,),
        in_specs=[row, vec, HBM_SPEC] + ([] if head is None else [vec, row]),
        out_specs=first_specs + [row, hid, hid, hid],
        out_shape=first_shapes + [row_bf16] + [jax.ShapeDtypeStruct((T, F), BF16)] * 3,
        scratch_shapes=[pltpu.VMEM((3, F, D), BF16), pltpu.SemaphoreType.DMA((3,))],
        vmem_mib=60, args=(x, g, w3) + (() if head is None else tuple(head)), rider=rider)


def _ffn_dgrad(dout, x, g, dau, dag, w3, tm, cf, name, rider=None):
    T, D = x.shape
    F = w3.shape[1]

    def body(do_ref, x_ref, g_ref, dau_ref, dag_ref, w_hbm, dx_ref, dgate_ref, dup_ref, dg_ref, wv, sem):
        _load_weights(w_hbm, wv, sem)

        @pl.when(pl.program_id(0) == 0)
        def _():
            dg_ref[...] = jnp.zeros_like(dg_ref)

        dob = (FFN_RES * do_ref[...]).astype(BF16)
        for lo in range(0, F, cf):
            dact = _dot(dob, wv[2, lo:lo + cf, :], NT)
            dup_ref[:, lo:lo + cf] = (dact * dau_ref[:, lo:lo + cf].astype(F32)).astype(BF16)
            dgate_ref[:, lo:lo + cf] = (dact * dag_ref[:, lo:lo + cf].astype(F32)).astype(BF16)
        dh = _dot(dgate_ref[...], wv[0], NN) + _dot(dup_ref[...], wv[1], NN)
        xhat, rstd = _rms_stats(x_ref[...])
        dx, dg = _rms_bwd(xhat, rstd, g_ref[...], dh)
        dx_ref[...] = do_ref[...] + dx
        dg_ref[...] += dg

    row = pl.BlockSpec((tm, D), lambda i: (i, 0))
    hid = pl.BlockSpec((tm, F), lambda i: (i, 0))
    vec = pl.BlockSpec((1, D), lambda i: (0, 0))
    return _call(
        body, name=name, grid=(T // tm,),
        in_specs=[row, row, vec, hid, hid, HBM_SPEC],
        out_specs=[row, hid, hid, vec],
        out_shape=[jax.ShapeDtypeStruct((T, D), F32), jax.ShapeDtypeStruct((T, F), BF16),
                   jax.ShapeDtypeStruct((T, F), BF16), jax.ShapeDtypeStruct((1, D), F32)],
        scratch_shapes=[pltpu.VMEM((3, F, D), BF16), pltpu.SemaphoreType.DMA((3,))],
        vmem_mib=52, args=(dout, x, g, dau, dag, w3), rider=rider)


def _wgrad(lhs, rhs, tm, tf, name, rider=None):
    T, F = lhs[0].shape
    D = rhs.shape[1]
    K = len(lhs)

    def body(*refs):
        lhs_refs, rhs_ref, dw_ref, accs = refs[:K], refs[K], refs[K + 1], refs[K + 2:]
        i = pl.program_id(1)

        @pl.when(i == 0)
        def _():
            for acc in accs:
                acc[...] = jnp.zeros_like(acc)

        rv = rhs_ref[...]
        for acc, lhs_ref in zip(accs, lhs_refs):
            acc[...] += _dot(lhs_ref[...], rv, TN)

        @pl.when(i == pl.num_programs(1) - 1)
        def _():
            for k, acc in enumerate(accs):
                dw_ref[k] = acc[...].astype(BF16)

    hid = pl.BlockSpec((tm, tf), lambda f, i: (i, f))
    return _call(
        body, name=name, grid=(F // tf, T // tm),
        in_specs=[hid] * K + [pl.BlockSpec((tm, D), lambda f, i: (i, 0))],
        out_specs=[pl.BlockSpec((K, tf, D), lambda f, i: (0, f, 0))],
        out_shape=[jax.ShapeDtypeStruct((K, F, D), BF16)],
        scratch_shapes=[pltpu.VMEM((tf, D), F32)] * K,
        vmem_mib=56, args=(*lhs, rhs), rider=rider)


def _lru_gates(xr, bda_ref, bdx_ref, vec_ref):
    xrb = xr.astype(BF16)
    r = _sigmoid(_dot(xrb, bda_ref[...], NN) + vec_ref[V_BA:V_BA + 1, :])
    ig = _sigmoid(_dot(xrb, bdx_ref[...], NN) + vec_ref[V_BX:V_BX + 1, :])
    sp = _softplus_neg(vec_ref[V_LAM:V_LAM + 1, :])
    log_a = (-LRU_C * sp) * r
    a = jnp.exp(log_a)
    mult = jnp.sqrt(_neg_expm1(2.0 * log_a))
    return xrb, r, ig, sp, a, mult


def _layernorm_stats(u1):
    xc = u1 - jnp.mean(u1, axis=-1, keepdims=True)
    rs = lax.rsqrt(jnp.mean(xc * xc, axis=-1, keepdims=True) + LN_EPS)
    return xc * rs, rs


def _mix_core_fwd(x1, g, w_in_t, w_out, bda, bdx, cw, lw, vec, tm, name, rider=None):
    T, D = x1.shape
    W = cw.shape[1]
    assert tm >= CONV_HALO and w_in_t.shape[0] == 4 * W

    def body(x1_ref, g_ref, wi_ref, wo_ref, bda_ref, bdx_ref, cw_ref, lw_ref, vec_ref,
             x2_ref, z_ref, mix_ref, u1_ref, xr_ref, hst_ref, ubuf, rbuf, hc):
        @pl.when(pl.program_id(0) == 0)
        def _():
            ubuf[0:CONV_HALO, :] = jnp.zeros((CONV_HALO, W), F32)
            rbuf[0:LRU_HALO, :] = jnp.zeros((LRU_HALO, W), F32)
            hc[...] = jnp.zeros_like(hc)

        xhat, _ = _rms_stats(x1_ref[...])
        z_ref[...] = _dot((xhat * g_ref[...]).astype(BF16), wi_ref[...], NT)

        ubuf[CONV_HALO:CONV_HALO + tm, :] = z_ref[:, 0:W] * _sigmoid(z_ref[:, W:2 * W])
        u1 = jnp.zeros((tm, W), F32) + vec_ref[V_CB:V_CB + 1, :]
        base = CONV_HALO - (CONV_K - 1)
        for off, win in _row_windows(ubuf, tm, range(base, base + CONV_K)):
            u1 = u1 + cw_ref[off - base:off - base + 1, :] * win
        ubuf[0:CONV_HALO, :] = ubuf[tm:tm + CONV_HALO, :]
        u1_ref[...] = u1
        xh, _ = _layernorm_stats(u1)
        u2 = xh * vec_ref[V_LNG:V_LNG + 1, :] + vec_ref[V_LNB:V_LNB + 1, :]
        ub = (u2 * _sigmoid(u2)).astype(BF16)
        mix_ref[:, 0:W] = ub

        rbuf[LRU_HALO:LRU_HALO + tm, :] = z_ref[:, 2 * W:3 * W]
        xr = jnp.zeros((tm, W), F32) + vec_ref[V_LCB:V_LCB + 1, :]
        for k in range(LRU_K):
            off = LRU_HALO - (LRU_K - 1) + k
            xr = xr + lw_ref[k:k + 1, :] * rbuf[off:off + tm, :]
        rbuf[0:LRU_HALO, :] = rbuf[tm:tm + LRU_HALO, :]
        xr_ref[...] = xr
        _, _, ig, _, a, mult = _lru_gates(xr, bda_ref, bdx_ref, vec_ref)
        hc[0:1, :] = _scan_rows(a, mult * (ig * xr), hc[0:1, :], hst_ref)
        gl, _ = _gelu_parts(z_ref[:, 3 * W:4 * W])
        yb = (hst_ref[...] * gl).astype(BF16)
        mix_ref[:, W:2 * W] = yb

        x2_ref[...] = x1_ref[...] + _dot(ub, wo_ref[0:W, :], NN) + _dot(yb, wo_ref[W:2 * W, :], NN)

    full = lambda a: pl.BlockSpec(a.shape, lambda i: (0,) * a.ndim)
    tile = lambda n: pl.BlockSpec((tm, n), lambda i: (i, 0))
    return _call(
        body, name=name, grid=(T // tm,),
        in_specs=[tile(D), full(g), full(w_in_t), full(w_out), full(bda), full(bdx), full(cw), full(lw), full(vec)],
        out_specs=[tile(D), tile(4 * W), tile(2 * W), tile(W), tile(W), tile(W)],
        out_shape=[jax.ShapeDtypeStruct((T, D), F32), jax.ShapeDtypeStruct((T, 4 * W), F32),
                   jax.ShapeDtypeStruct((T, 2 * W), BF16), jax.ShapeDtypeStruct((T, W), F32),
                   jax.ShapeDtypeStruct((T, W), F32), jax.ShapeDtypeStruct((T, W), F32)],
        scratch_shapes=[pltpu.VMEM((tm + CONV_HALO, W), F32), pltpu.VMEM((tm + LRU_HALO, W), F32),
                        pltpu.VMEM((8, W), F32)],
        vmem_mib=56, args=(x1, g, w_in_t, w_out, bda, bdx, cw, lw, vec), rider=rider)


def _mix_bwd(dx2, z, u1, xr, hst, x1, mix, g, w_in_t, w_out, bda, bdx, cw, lw, vec, tm, name, rider=None):
    T, D = dx2.shape
    W = cw.shape[1]
    nt = T // tm
    assert tm >= CONV_HALO and tm % CONV_HALO == 0

    def body(dx_ref, z_ref, zh_ref, u1_ref, xr_ref, h_ref, hh_ref, wo_ref, bda_ref, bdx_ref, cw_ref, lw_ref, vec_ref,
             x1_ref, mix_ref, g_ref, wi_ref,
             dx1_ref, sg_ref, dbda_ref, dbdx_ref, dob_ref, dg_ref, dwi_ref, dwo_ref,
             u0buf, du1buf, rxbuf, dxrbuf, gbuf, gc, spacc, dz_ref, ai_ref, ao_ref):
        i = pl.program_id(0)
        first = i == nt - 1
        row = lax.broadcasted_iota(jnp.int32, (tm, W), 0)

        @pl.when(i == 0)
        def _():
            sg_ref[...] = jnp.zeros_like(sg_ref)
            dbda_ref[...] = jnp.zeros_like(dbda_ref)
            dbdx_ref[...] = jnp.zeros_like(dbdx_ref)
            du1buf[tm:tm + CONV_HALO, :] = jnp.zeros((CONV_HALO, W), F32)
            dxrbuf[tm:tm + LRU_HALO, :] = jnp.zeros((LRU_HALO, W), F32)
            gc[...] = jnp.zeros_like(gc)
            spacc[...] = jnp.zeros_like(spacc)
            dg_ref[...] = jnp.zeros_like(dg_ref)
            ai_ref[...] = jnp.zeros_like(ai_ref)
            ao_ref[...] = jnp.zeros_like(ao_ref)

        def accum(r, val):
            sg_ref[r:r + 1, :] += jnp.sum(val, axis=0, keepdims=True)

        x1hat, x1rstd = _rms_stats(x1_ref[...])
        gain = g_ref[...]
        hb = (x1hat * gain).astype(BF16)

        def in_proj_bwd(lo, hi):
            dzb = dz_ref[:, lo:hi]
            ai_ref[lo:hi, :] += _dot(dzb, hb, TN)
            return _dot(dzb, wi_ref[lo:hi, :], NN)

        dxb = dx_ref[...].astype(BF16)
        ao_ref[...] += _dot(mix_ref[...], dxb, TN)
        dmix = _dot(dxb, wo_ref[...], NT)
        d_u = dmix[:, 0:W]
        d_yr = dmix[:, W:2 * W]

        xh, rs = _layernorm_stats(u1_ref[...])
        ln_g = vec_ref[V_LNG:V_LNG + 1, :]
        u2 = xh * ln_g + vec_ref[V_LNB:V_LNB + 1, :]
        s2 = _sigmoid(u2)
        d_u2 = d_u * (s2 * (1.0 + u2 * (1.0 - s2)))
        accum(G_LNG, d_u2 * xh)
        accum(G_LNB, d_u2)
        d_xh = d_u2 * ln_g
        d_u1 = rs * (d_xh - jnp.mean(d_xh, axis=-1, keepdims=True)
                     - xh * jnp.mean(d_xh * xh, axis=-1, keepdims=True))
        accum(G_CB, d_u1)
        halo_on = jnp.where(first, 0.0, 1.0)
        u0buf[0:CONV_HALO, :] = halo_on * (zh_ref[:, 0:W] * _sigmoid(zh_ref[:, W:2 * W]))
        cv = z_ref[:, 0:W]
        sgc = _sigmoid(z_ref[:, W:2 * W])
        u0buf[CONV_HALO:CONV_HALO + tm, :] = cv * sgc
        du1buf[0:tm, :] = d_u1
        base = CONV_HALO - (CONV_K - 1)
        for off, win in _row_windows(u0buf, tm, range(base, base + CONV_K)):
            accum(G_CW + off - base, d_u1 * win)
        d_u0 = jnp.zeros((tm, W), F32)
        for off, win in _row_windows(du1buf, tm, range(0, CONV_K)):
            d_u0 = d_u0 + cw_ref[CONV_K - 1 - off:CONV_K - off, :] * win
        du1buf[tm:tm + CONV_HALO, :] = du1buf[0:CONV_HALO, :]
        dz_ref[:, 0:W] = (d_u0 * sgc).astype(BF16)
        dz_ref[:, W:2 * W] = (d_u0 * cv * (sgc * (1.0 - sgc))).astype(BF16)
        dh = in_proj_bwd(0, 2 * W)

        xrv = xr_ref[...]
        xrb, r, ig, sp, a, mult = _lru_gates(xrv, bda_ref, bdx_ref, vec_ref)
        h = h_ref[...]
        gl, dgl = _gelu_parts(z_ref[:, 3 * W:4 * W])
        dz_ref[:, 3 * W:4 * W] = (d_yr * h * dgl).astype(BF16)
        dh = dh + in_proj_bwd(3 * W, 4 * W)
        a_next = jnp.where(row == tm - 1, 1.0, pltpu.roll(a, tm - 1, 0))
        g_first = _scan_rows(a_next, d_yr * gl, gc[0:1, :], gbuf, reverse=True)
        g = gbuf[...]
        gc[0:1, :] = a[0:1, :] * g_first
        hprev = jnp.where(row == 0, halo_on * hh_ref[LRU_HALO - 1:LRU_HALO, :], pltpu.roll(h, 1, 0))
        d_log_a = (g * hprev) * a - (g * ig * xrv) * (a * a) / mult
        d_ig = g * mult * xrv
        d_xr = g * mult * ig
        spacc[0:1, :] += jnp.sum(d_log_a * r, axis=0, keepdims=True)
        d_pa32 = (d_log_a * (-LRU_C * sp)) * (r * (1.0 - r))
        d_px32 = d_ig * (ig * (1.0 - ig))
        accum(G_BA, d_pa32)
        accum(G_BX, d_px32)
        d_pa = d_pa32.astype(BF16)
        d_px = d_px32.astype(BF16)
        d_xr = d_xr + _dot(d_pa, bda_ref[...], NT) + _dot(d_px, bdx_ref[...], NT)
        dbda_ref[...] += _dot(xrb, d_pa, TN)
        dbdx_ref[...] += _dot(xrb, d_px, TN)
        accum(G_LCB, d_xr)
        rxbuf[0:LRU_HALO, :] = halo_on * zh_ref[CONV_HALO - LRU_HALO:CONV_HALO, 2 * W:3 * W]
        rxbuf[LRU_HALO:LRU_HALO + tm, :] = z_ref[:, 2 * W:3 * W]
        dxrbuf[0:tm, :] = d_xr
        d_rx = jnp.zeros((tm, W), F32)
        for k in range(LRU_K):
            off = LRU_HALO - (LRU_K - 1) + k
            accum(G_LW + k, d_xr * rxbuf[off:off + tm, :])
            d_rx = d_rx + lw_ref[k:k + 1, :] * dxrbuf[LRU_K - 1 - k:LRU_K - 1 - k + tm, :]
        dxrbuf[tm:tm + LRU_HALO, :] = dxrbuf[0:LRU_HALO, :]
        dz_ref[:, 2 * W:3 * W] = d_rx.astype(BF16)
        dh = dh + in_proj_bwd(2 * W, 3 * W)

        dx, dg = _rms_bwd(x1hat, x1rstd, gain, dh)
        dx1 = dx_ref[...] + dx
        dx1_ref[...] = dx1
        dob_ref[...] = (FFN_RES * dx1).astype(BF16)
        dg_ref[...] += dg

        @pl.when(first)
        def _():
            lam = vec_ref[V_LAM:V_LAM + 1, :]
            sg_ref[G_LAM:G_LAM + 1, :] = LRU_C * _sigmoid(-lam) * spacc[0:1, :]
            dwi_ref[...] = ai_ref[...].astype(BF16)
            dwo_ref[...] = ao_ref[...].astype(BF16)

    full = lambda a: pl.BlockSpec(a.shape, lambda i: (0,) * a.ndim)
    tile = lambda n: pl.BlockSpec((tm, n), lambda i: (nt - 1 - i, 0))
    halo = lambda rows, n: pl.BlockSpec(
        (rows, n), lambda i: (jnp.maximum((nt - 1 - i) * (tm // rows) - 1, 0), 0))
    const = lambda r, c: pl.BlockSpec((r, c), lambda i: (0, 0))
    return _call(
        body, name=name, grid=(nt,),
        in_specs=[tile(D), tile(4 * W), halo(CONV_HALO, 4 * W), tile(W), tile(W), tile(W), halo(LRU_HALO, W),
                  full(w_out), full(bda), full(bdx), full(cw), full(lw), full(vec),
                  tile(D), tile(2 * W), full(g), full(w_in_t)],
        out_specs=[tile(D), const(G_ROWS, W), const(W, W), const(W, W),
                   tile(D), const(1, D), const(4 * W, D), const(2 * W, D)],
        out_shape=[jax.ShapeDtypeStruct((T, D), F32), jax.ShapeDtypeStruct((G_ROWS, W), F32),
                   jax.ShapeDtypeStruct((W, W), F32), jax.ShapeDtypeStruct((W, W), F32),
                   jax.ShapeDtypeStruct((T, D), BF16), jax.ShapeDtypeStruct((1, D), F32),
                   jax.ShapeDtypeStruct((4 * W, D), BF16), jax.ShapeDtypeStruct((2 * W, D), BF16)],
        scratch_shapes=[pltpu.VMEM((tm + CONV_HALO, W), F32), pltpu.VMEM((tm + CONV_HALO, W), F32),
                        pltpu.VMEM((tm + LRU_HALO, W), F32), pltpu.VMEM((tm + LRU_HALO, W), F32),
                        pltpu.VMEM((tm, W), F32), pltpu.VMEM((8, W), F32), pltpu.VMEM((8, W), F32),
                        pltpu.VMEM((tm, 4 * W), BF16), pltpu.VMEM((4 * W, D), F32), pltpu.VMEM((2 * W, D), F32)],
        vmem_mib=60, args=(dx2, z, z, u1, xr, hst, hst, w_out, bda, bdx, cw, lw, vec, x1, mix, g, w_in_t),
        rider=rider)


def _pair_add(full, recv, name):
    K, _, _, rows, D = full.shape

    def body(c_ref, a_ref, b_ref, o_ref):
        o_ref[...] = (a_ref[...].astype(F32) + b_ref[...].astype(F32)).astype(BF16)

    c = lax.axis_index("c").astype(jnp.int32).reshape((1,))
    return _call(
        body, name=name, grid=(K, N_CHIP), num_scalar_prefetch=1,
        in_specs=[pl.BlockSpec((None, None, None, rows, D), lambda k, q, c_ref: (k, q, c_ref[0], 0, 0)),
                  pl.BlockSpec((None, None, rows, D), lambda k, q, c_ref: (k, q, 0, 0))],
        out_specs=pl.BlockSpec((None, None, rows, D), lambda k, q, c_ref: (k, q, 0, 0)),
        out_shape=jax.ShapeDtypeStruct(recv.shape, BF16),
        scratch_shapes=[], vmem_mib=16, args=(c, full, recv))


def _adamw_update(wv, gv, mv, vv):
    m2 = ADAM_B1 * mv + (1.0 - ADAM_B1) * gv
    v2 = ADAM_B2 * vv + (1.0 - ADAM_B2) * (gv * gv)
    m_hat = m2 / (1.0 - ADAM_B1 ** ADAM_STEP)
    v_hat = v2 / (1.0 - ADAM_B2 ** ADAM_STEP)
    return -ADAM_LR * (m_hat / (jnp.sqrt(v_hat) + ADAM_EPS) + ADAM_WD * wv), m2, v2


def _finish(parts, k, w, m, v, transpose, name):
    _, _, rows, D = parts.shape

    def body(p_ref, w_ref, m_ref, v_ref, g_ref, d_ref, mo_ref, vo_ref):
        acc = p_ref[0].astype(F32)
        for q in range(1, N_CHIP):
            acc = acc + p_ref[q].astype(F32)
        gv = acc.T if transpose else acc
        g_ref[...] = gv
        d_ref[...], mo_ref[...], vo_ref[...] = _adamw_update(w_ref[...], gv, m_ref[...], v_ref[...])

    whole = pl.BlockSpec(w.shape, lambda i: (0, 0))
    return _call(
        body, name=name, grid=(1,),
        in_specs=[pl.BlockSpec((None, N_CHIP, rows, D), lambda i: (k, 0, 0, 0)), whole, whole, whole],
        out_specs=[whole] * 4, out_shape=[pltpu.HBM(w.shape, F32)] * 4,
        scratch_shapes=[], vmem_mib=40, args=(parts, w, m, v))


def _adamw_each(ws, gs, ms, vs, name):
    n = len(ws)

    def body(*refs):
        w_refs, g_refs, m_refs, v_refs, outs = refs[:n], refs[n:2 * n], refs[2 * n:3 * n], refs[3 * n:4 * n], refs[4 * n:]
        for k in range(n):
            outs[k][...], outs[n + k][...], outs[2 * n + k][...] = _adamw_update(
                w_refs[k][...], g_refs[k][...], m_refs[k][...], v_refs[k][...])

    shapes = [jax.ShapeDtypeStruct(w.shape, F32) for w in ws]
    return pl.pallas_call(
        body, name=name,
        in_specs=[VMEM_SPEC] * (4 * n), out_specs=[VMEM_SPEC] * (3 * n), out_shape=shapes * 3,
        compiler_params=pltpu.CompilerParams(vmem_limit_bytes=32 * MIB),
    )(*ws, *gs, *ms, *vs)


def _block_diag(w):
    h, d, _ = w.shape
    onto = jnp.eye(h, dtype=w.dtype)
    return (w[:, :, None, :] * onto[:, None, :, None]).reshape(h * d, h * d)


def _diag_blocks(m, h):
    d = m.shape[0] // h
    onto = jnp.eye(h, dtype=m.dtype)
    return (m.reshape(h, d, h, d) * onto[:, None, :, None]).sum(axis=2)


def _reduce_level1(full, tag):
    got = _run_comm(_sibling_comm(full), "rs_sibling_" + tag)
    return [_pair_add(a, b, "rs_pair_add_%s%d" % (tag, n)) for n, (a, b) in enumerate(zip(full, got))]


def kernel(x, ffn1_norm, ffn1_w_gate, ffn1_w_up, ffn1_w_down, mix_norm, w_in, conv_dw, conv_dw_bias, conv_ln_g, conv_ln_b, lru_conv_w, lru_conv_b, lru_w_a, lru_b_a, lru_w_x, lru_b_x, lru_lambda, w_out, ffn2_norm, ffn2_w_gate, ffn2_w_up, ffn2_w_down, final_norm, loss_target, m_ffn1_norm, m_ffn1_w_gate, m_ffn1_w_up, m_ffn1_w_down, m_mix_norm, m_w_in, m_conv_dw, m_conv_dw_bias, m_conv_ln_g, m_conv_ln_b, m_lru_conv_w, m_lru_conv_b, m_lru_w_a, m_lru_b_a, m_lru_w_x, m_lru_b_x, m_lru_lambda, m_w_out, m_ffn2_norm, m_ffn2_w_gate, m_ffn2_w_up, m_ffn2_w_down, m_final_norm, v_ffn1_norm, v_ffn1_w_gate, v_ffn1_w_up, v_ffn1_w_down, v_mix_norm, v_w_in, v_conv_dw, v_conv_dw_bias, v_conv_ln_g, v_conv_ln_b, v_lru_conv_w, v_lru_conv_b, v_lru_w_a, v_lru_b_a, v_lru_w_x, v_lru_b_x, v_lru_lambda, v_w_out, v_ffn2_norm, v_ffn2_w_gate, v_ffn2_w_up, v_ffn2_w_down, v_final_norm):
    T, D = x.shape[1], x.shape[2]
    F = ffn1_w_down.shape[0] * N_DEV
    rf = ffn1_w_down.shape[0]
    ri = w_in.shape[1]
    ro = w_out.shape[0]
    W = conv_dw_bias.shape[0]
    wc = conv_dw.shape[1]
    H = lru_w_a.shape[0]
    xs = x.reshape(T, D)
    tgt = loss_target.reshape(T, D)
    tm_ffn = min(256, T)
    tm_fwd = min(512, T)
    cf = 256
    tm_w = min(1024, T)
    tm_w1 = min(2048, T)
    tm_mix = min(256, T)
    tf_w = F // 2
    row = lambda v: v.reshape(1, -1)
    by_owner = lambda a, rows: a.reshape(a.shape[0], N_CHIP, 2, rows, D)

    p3a, p3b, p_in, p_out = _prep_weights(
        (ffn1_w_gate.T, ffn1_w_up.T, ffn1_w_down), (ffn2_w_gate.T, ffn2_w_up.T, ffn2_w_down), w_in, w_out,
        "prep_weights")
    tile_rows = lambda a: jnp.pad(a, ((0, -a.shape[0] % SUBLANES), (0, 0)))
    p_cw = jnp.concatenate([tile_rows(conv_dw), tile_rows(lru_conv_w)], axis=0)
    lw_row = p_cw.shape[0] - SUBLANES
    stacked = lambda r, j: r.at[:, j]
    plain = lambda r, j: r.at[j]
    g3_shape = jax.ShapeDtypeStruct((3, N_DEV, rf, D), BF16)
    (g3a,) = _all_gather([p3a], [stacked], [g3_shape], "ag_ffn1")
    w3a = g3a.reshape(3, F, D)
    bda = _block_diag(lru_w_a).astype(BF16)
    bdx = _block_diag(lru_w_x).astype(BF16)
    vec = jnp.concatenate([tile_rows(v[None]) for v in
                           (conv_dw_bias, conv_ln_g, conv_ln_b, lru_conv_b, lru_b_a, lru_b_x, lru_lambda)], axis=0)

    gather_rest = _gather_comm(
        [p3b, p_in, p_out, p_cw], [stacked, plain, plain, plain],
        [g3_shape, jax.ShapeDtypeStruct((N_DEV, ri, D), BF16), jax.ShapeDtypeStruct((N_DEV, ro, D), BF16),
         jax.ShapeDtypeStruct((N_DEV,) + p_cw.shape, F32)],
        [(SIBLING,) + SAME_CORE, EVERYONE, EVERYONE, EVERYONE])
    (x1, h1, dau1, dag1, act1), (g3b_half, g_in, g_out, g_cw) = _ffn_fwd(
        xs, row(ffn1_norm), w3a, tm_fwd, cf, "ffn1_fwd", rider=gather_rest)
    w_in_t = g_in.reshape(N_DEV * ri, D)
    w_out_f = g_out.reshape(N_DEV * ro, D)
    cw_all = jnp.transpose(g_cw, (1, 0, 2)).reshape(p_cw.shape[0], N_DEV * wc)
    cw = cw_all[0:CONV_K]
    lw = cw_all[lw_row:lw_row + LRU_K]
    (x2, z, mix, u1, xr, hst), (g3b,) = _mix_core_fwd(
        x1, row(mix_norm), w_in_t, w_out_f, bda, bdx, cw, lw, vec, tm_mix, "mix_core_fwd",
        rider=_forward_comm([g3b_half], [stacked]))
    w3b = g3b.reshape(3, F, D)
    dx3, dob2, d_final_norm, loss_part, h3, dau2, dag2, act2 = _ffn_fwd(
        x2, row(ffn2_norm), w3b, tm_fwd, cf, "ffn2_fwd_loss", head=(row(final_norm), tgt))

    dx2, dgate2, dup2, d_ffn2_norm = _ffn_dgrad(dx3, x2, row(ffn2_norm), dau2, dag2, w3b, tm_ffn, cf, "ffn2_dgrad")
    (dw_gu2,) = _wgrad([dgate2, dup2], h3, tm_w, tf_w, "ffn2_wgrad_gu")
    (dw_d2,) = _wgrad([act2], dob2, tm_w1, tf_w, "ffn2_wgrad_d")
    sums_f2 = _reduce_level1([by_owner(dw_gu2, rf), by_owner(dw_d2, rf)], "f2")
    (dx1, sg, dbda, dbdx, dob1, d_mix_norm, dw_in_t, dw_out), parts_f2 = _mix_bwd(
        dx2, z, u1, xr, hst, x1, mix, row(mix_norm), w_in_t, w_out_f, bda, bdx, cw, lw, vec, tm_mix, "mix_bwd",
        rider=_chips_comm(sums_f2))
    sums_io = _reduce_level1([by_owner(dw_in_t[None], ri), by_owner(dw_out[None], ro)], "io")
    (dw_d1,), parts_io = _wgrad([act1], dob1, tm_w1, tf_w, "ffn1_wgrad_d", rider=_chips_comm(sums_io))
    sums_d1 = _reduce_level1([by_owner(dw_d1, rf)], "d1")
    dx0, dgate1, dup1, d_ffn1_norm = _ffn_dgrad(dx1, xs, row(ffn1_norm), dau1, dag1, w3a, tm_ffn, cf, "ffn1_dgrad")

    small = [d_ffn1_norm, d_mix_norm, d_ffn2_norm, d_final_norm, sg, _diag_blocks(dbda, H), _diag_blocks(dbdx, H),
             loss_part]
    (dw_g1,), summed_and_parts = _wgrad(
        [dgate1], h1, tm_w1, tf_w, "ffn1_wgrad_g", rider=_both(_small_sum_comm(small), _chips_comm(sums_d1)))
    summed, parts_d1 = summed_and_parts[:len(small)], summed_and_parts[len(small):]
    sums_g1 = _reduce_level1([by_owner(dw_g1, rf)], "g1")
    (dw_u1,), parts_g1 = _wgrad([dup1], h1, tm_w1, tf_w, "ffn1_wgrad_u", rider=_chips_comm(sums_g1))
    sums_u1 = _reduce_level1([by_owner(dw_u1, rf)], "u1")
    parts_u1 = _run_comm(_chips_comm(sums_u1), "rs_chips_u1")

    g_norm1, g_norm_mix, g_norm2, g_norm_final, g_sg, g_w_a, g_w_x, g_loss = summed
    loss = g_loss[0, 0]
    me = 4 * lax.axis_index("x") + 2 * lax.axis_index("y") + lax.axis_index("c")
    chan = lambda full_g: lax.dynamic_slice_in_dim(full_g, me * wc, wc, axis=1)
    grads = {
        "ffn1_norm": g_norm1.reshape(D), "mix_norm": g_norm_mix.reshape(D), "ffn2_norm": g_norm2.reshape(D),
        "final_norm": g_norm_final.reshape(D),
        "conv_dw_bias": g_sg[G_CB], "conv_ln_g": g_sg[G_LNG], "conv_ln_b": g_sg[G_LNB],
        "lru_conv_b": g_sg[G_LCB], "lru_b_a": g_sg[G_BA], "lru_b_x": g_sg[G_BX], "lru_lambda": g_sg[G_LAM],
        "lru_w_a": g_w_a, "lru_w_x": g_w_x,
        "conv_dw": chan(g_sg[G_CW:G_CW + CONV_K]), "lru_conv_w": chan(g_sg[G_LW:G_LW + LRU_K]),
    }

    weights = dict(ffn1_norm=ffn1_norm, ffn1_w_gate=ffn1_w_gate, ffn1_w_up=ffn1_w_up, ffn1_w_down=ffn1_w_down, mix_norm=mix_norm, w_in=w_in, conv_dw=conv_dw, conv_dw_bias=conv_dw_bias, conv_ln_g=conv_ln_g, conv_ln_b=conv_ln_b, lru_conv_w=lru_conv_w, lru_conv_b=lru_conv_b, lru_w_a=lru_w_a, lru_b_a=lru_b_a, lru_w_x=lru_w_x, lru_b_x=lru_b_x, lru_lambda=lru_lambda, w_out=w_out, ffn2_norm=ffn2_norm, ffn2_w_gate=ffn2_w_gate, ffn2_w_up=ffn2_w_up, ffn2_w_down=ffn2_w_down, final_norm=final_norm)
    moment1 = dict(ffn1_norm=m_ffn1_norm, ffn1_w_gate=m_ffn1_w_gate, ffn1_w_up=m_ffn1_w_up, ffn1_w_down=m_ffn1_w_down, mix_norm=m_mix_norm, w_in=m_w_in, conv_dw=m_conv_dw, conv_dw_bias=m_conv_dw_bias, conv_ln_g=m_conv_ln_g, conv_ln_b=m_conv_ln_b, lru_conv_w=m_lru_conv_w, lru_conv_b=m_lru_conv_b, lru_w_a=m_lru_w_a, lru_b_a=m_lru_b_a, lru_w_x=m_lru_w_x, lru_b_x=m_lru_b_x, lru_lambda=m_lru_lambda, w_out=m_w_out, ffn2_norm=m_ffn2_norm, ffn2_w_gate=m_ffn2_w_gate, ffn2_w_up=m_ffn2_w_up, ffn2_w_down=m_ffn2_w_down, final_norm=m_final_norm)
    moment2 = dict(ffn1_norm=v_ffn1_norm, ffn1_w_gate=v_ffn1_w_gate, ffn1_w_up=v_ffn1_w_up, ffn1_w_down=v_ffn1_w_down, mix_norm=v_mix_norm, w_in=v_w_in, conv_dw=v_conv_dw, conv_dw_bias=v_conv_dw_bias, conv_ln_g=v_conv_ln_g, conv_ln_b=v_conv_ln_b, lru_conv_w=v_lru_conv_w, lru_conv_b=v_lru_conv_b, lru_w_a=v_lru_w_a, lru_b_a=v_lru_b_a, lru_w_x=v_lru_w_x, lru_b_x=v_lru_b_x, lru_lambda=v_lru_lambda, w_out=v_w_out, ffn2_norm=v_ffn2_norm, ffn2_w_gate=v_ffn2_w_gate, ffn2_w_up=v_ffn2_w_up, ffn2_w_down=v_ffn2_w_down, final_norm=v_final_norm)
    order = list(weights)
    big = {"ffn1_w_gate": (parts_g1[0], 0, True), "ffn1_w_up": (parts_u1[0], 0, True),
           "ffn1_w_down": (parts_d1[0], 0, False), "w_in": (parts_io[0], 0, True), "w_out": (parts_io[1], 0, False),
           "ffn2_w_gate": (parts_f2[0], 0, True), "ffn2_w_up": (parts_f2[0], 1, True),
           "ffn2_w_down": (parts_f2[1], 0, False)}
    delta, new_m, new_v = {}, {}, {}
    for n, (parts, k, d_major) in big.items():
        operands = weights[n], moment1[n], moment2[n]
        if d_major and n != "w_in":
            results = _finish(parts, k, *[a.T for a in operands], False, "finish_" + n)
            grads[n], delta[n], new_m[n], new_v[n] = [r.T for r in results]
        else:
            grads[n], delta[n], new_m[n], new_v[n] = _finish(parts, k, *operands, d_major, "finish_" + n)
    rest = [n for n in order if n not in big]
    updates = _adamw_each([weights[n] for n in rest], [grads[n] for n in rest], [moment1[n] for n in rest],
                          [moment2[n] for n in rest], "adamw_small")
    for k, n in enumerate(rest):
        delta[n], new_m[n], new_v[n] = updates[k], updates[len(rest) + k], updates[2 * len(rest) + k]

    return (loss, dx0.reshape(x.shape), *[grads[n] for n in order], *[delta[n] for n in order],
            *[new_m[n] for n in order], *[new_v[n] for n in order])
```

```python
import functools
import math

import jax
import jax.numpy as jnp
from jax import lax
from jax.experimental import pallas as pl
from jax.experimental.pallas import tpu as pltpu

F32 = jnp.float32
BF16 = jnp.bfloat16
MESH = pl.DeviceIdType.MESH

N_DEV = 8
N_CHIP = 4
SUBLANES = 8
RMS_EPS = 1e-6
LN_EPS = 1e-5
LRU_C = 8.0
CONV_K = 31
LRU_K = 4
CONV_HALO = 32
LRU_HALO = 8
FFN_RES = 0.5
ADAM_LR, ADAM_B1, ADAM_B2, ADAM_EPS, ADAM_WD, ADAM_STEP = 0.001, 0.9, 0.999, 1e-08, 0.01, 10
GELU_K = math.sqrt(2.0 / math.pi)
GELU_C = 0.044715

MIB = 1024 * 1024
NT = (((1,), (1,)), ((), ()))
NN = (((1,), (0,)), ((), ()))
TN = (((0,), (0,)), ((), ()))

V_CB, V_LNG, V_LNB, V_LCB, V_BA, V_BX, V_LAM = range(0, 7 * SUBLANES, SUBLANES)
G_CW = 0
G_CB, G_LNG, G_LNB = 31, 32, 33
G_LW = 34
G_LCB, G_BA, G_BX, G_LAM = 38, 39, 40, 41
G_ROWS = 48

HBM_SPEC = pl.BlockSpec(memory_space=pltpu.HBM)
VMEM_SPEC = pl.BlockSpec(memory_space=pltpu.VMEM)


def _dot(a, b, dims):
    return lax.dot_general(a, b, dims, preferred_element_type=F32)


def _sigmoid(x):
    return 1.0 / (1.0 + jnp.exp(-x))


def _gelu_parts(x):
    x2 = x * x
    th = jnp.tanh(GELU_K * x * (1.0 + GELU_C * x2))
    gl = 0.5 * x * (1.0 + th)
    dgl = 0.5 * (1.0 + th) + 0.5 * x * (1.0 - th * th) * GELU_K * (1.0 + 3.0 * GELU_C * x2)
    return gl, dgl


def _neg_expm1(y):
    series = -y * (1.0 + y * (1.0 / 2) * (1.0 + y * (1.0 / 3) * (1.0 + y * (1.0 / 4) * (1.0 + y * (1.0 / 5) * (1.0 + y * (1.0 / 6))))))
    return jnp.where(y > -0.25, series, 1.0 - jnp.exp(y))


def _softplus_neg(lam):
    t = -lam
    e = jnp.exp(-jnp.abs(t))
    s = 1.0 + e
    log1p_e = jnp.log(s) - ((s - 1.0) - e) / s
    return jnp.maximum(t, 0.0) + log1p_e


def _rms_stats(xv):
    rstd = lax.rsqrt(jnp.mean(xv * xv, axis=-1, keepdims=True) + RMS_EPS)
    return xv * rstd, rstd


def _rms_bwd(xhat, rstd, g, dh):
    dxhat = dh * g
    dx = rstd * (dxhat - xhat * jnp.mean(dxhat * xhat, axis=-1, keepdims=True))
    return dx, jnp.sum(dh * xhat, axis=0, keepdims=True)


def _row_windows(buf_ref, n_rows, offsets):
    total = buf_ref.shape[0]
    full = buf_ref[...]
    for b in range(SUBLANES):
        offs = [o for o in offsets if o % SUBLANES == b]
        if not offs:
            continue
        assert max(offs) + n_rows <= total
        moved = full if b == 0 else pltpu.roll(full, total - b, 0)
        for o in offs:
            yield o, moved[o - b:o - b + n_rows, :]


def _scan_rows(av, bv, edge, out_ref, reverse=False):
    tm, W = av.shape
    sub = lax.broadcasted_iota(jnp.int32, (tm, W), 0) % SUBLANES
    s = 1
    while s < SUBLANES:
        keep = (sub < SUBLANES - s) if reverse else (sub >= s)
        shift = tm - s if reverse else s
        bv = jnp.where(keep, av * pltpu.roll(bv, shift, 0) + bv, bv)
        av = jnp.where(keep, av * pltpu.roll(av, shift, 0), av)
        s *= 2
    starts = range(0, tm, SUBLANES)
    for r0 in (reversed(starts) if reverse else starts):
        group = av[r0:r0 + SUBLANES, :] * edge + bv[r0:r0 + SUBLANES, :]
        out_ref[r0:r0 + SUBLANES, :] = group
        edge = group[0:1, :] if reverse else group[SUBLANES - 1:SUBLANES, :]
    return edge


class _Comm:
    def __init__(self, arrays, in_specs, out_shapes, out_specs, scratch, start, wait, aliases=None):
        self.arrays, self.in_specs = list(arrays), list(in_specs)
        self.out_shapes, self.out_specs = list(out_shapes), list(out_specs)
        self.scratch, self.start, self.wait = list(scratch), start, wait
        self.aliases = dict(aliases or {})


def _in_hbm(a):
    return pltpu.with_memory_space_constraint(a, pltpu.HBM)


def _operands(comm):
    return [a if spec is VMEM_SPEC else _in_hbm(a) for a, spec in zip(comm.arrays, comm.in_specs)]


def _call(body, *, name, grid, in_specs, out_specs, out_shape, scratch_shapes, vmem_mib, args, rider=None,
          num_scalar_prefetch=0):
    params = pltpu.CompilerParams(dimension_semantics=("arbitrary",) * len(grid), vmem_limit_bytes=vmem_mib * MIB)
    args = [a if k < num_scalar_prefetch else _in_hbm(a) for k, a in enumerate(args)]
    if rider is None:
        return pl.pallas_call(
            body, name=name,
            grid_spec=pltpu.PrefetchScalarGridSpec(
                num_scalar_prefetch=num_scalar_prefetch, grid=grid, in_specs=in_specs, out_specs=out_specs,
                scratch_shapes=scratch_shapes),
            out_shape=out_shape, compiler_params=params)(*args)
    assert num_scalar_prefetch == 0
    n_in, n_out, n_scr = len(in_specs), len(out_specs), len(scratch_shapes)
    r_in, r_out = len(rider.arrays), len(rider.out_shapes)
    n_axes = len(grid)

    def carried(*refs):
        pos = [0]

        def take(n):
            pos[0] += n
            return refs[pos[0] - n:pos[0]]

        ins, r_ins, outs, r_outs, scr, r_scr = take(n_in), take(r_in), take(n_out), take(r_out), take(n_scr), take(len(rider.scratch))
        first = pl.program_id(0) == 0
        last = pl.program_id(0) == grid[0] - 1
        for ax in range(1, n_axes):
            first = first & (pl.program_id(ax) == 0)
            last = last & (pl.program_id(ax) == grid[ax] - 1)

        @pl.when(first)
        def _():
            rider.start(r_ins, r_outs, r_scr)

        body(*ins, *outs, *scr)

        @pl.when(last)
        def _():
            rider.wait(r_ins, r_outs, r_scr)

    res = pl.pallas_call(
        carried, name=name,
        grid=grid,
        in_specs=list(in_specs) + rider.in_specs,
        out_specs=list(out_specs) + rider.out_specs,
        out_shape=list(out_shape) + rider.out_shapes,
        scratch_shapes=list(scratch_shapes) + rider.scratch,
        input_output_aliases={n_in + i: n_out + o for i, o in rider.aliases.items()},
        compiler_params=params)(*args, *_operands(rider))
    return res[:n_out], res[n_out:]


def _run_comm(comm, name):
    n_in, n_out = len(comm.arrays), len(comm.out_shapes)

    def body(*refs):
        ins, outs, scr = refs[:n_in], refs[n_in:n_in + n_out], refs[n_in + n_out:]
        comm.start(ins, outs, scr)
        comm.wait(ins, outs, scr)

    return pl.pallas_call(
        body, name=name,
        in_specs=comm.in_specs, out_specs=comm.out_specs, out_shape=comm.out_shapes,
        scratch_shapes=comm.scratch, input_output_aliases=comm.aliases,
        compiler_params=pltpu.CompilerParams(vmem_limit_bytes=24 * MIB))(*_operands(comm))


def _both(a, b):
    ni, no, ns = len(a.arrays), len(a.out_shapes), len(a.scratch)

    def start(ins, outs, scr):
        a.start(ins[:ni], outs[:no], scr[:ns])
        b.start(ins[ni:], outs[no:], scr[ns:])

    def wait(ins, outs, scr):
        a.wait(ins[:ni], outs[:no], scr[:ns])
        b.wait(ins[ni:], outs[no:], scr[ns:])

    aliases = dict(a.aliases)
    aliases.update({ni + i: no + o for i, o in b.aliases.items()})
    return _Comm(a.arrays + b.arrays, a.in_specs + b.in_specs, a.out_shapes + b.out_shapes,
                 a.out_specs + b.out_specs, a.scratch + b.scratch, start, wait, aliases)


def _place():
    return lax.axis_index("x"), lax.axis_index("y"), lax.axis_index("c")


def _peer(k):
    x, y, c = _place()
    px, py, pc = x ^ ((k >> 2) & 1), y ^ ((k >> 1) & 1), c ^ (k & 1)
    return (px, py, pc), 4 * px + 2 * py + pc


SIBLING = 1
SAME_CORE = (2, 4, 6)
EVERYONE = tuple(range(1, N_DEV))


def _gather_comm(shards, views, out_shapes, relations):
    na = len(shards)

    def copies(ins, outs, scr):
        send_sems, recv_sems, _ = scr
        _, me = _peer(0)
        out = []
        for a in range(na):
            for k in relations[a]:
                peer, theirs = _peer(k)
                send = functools.partial(
                    pltpu.make_async_remote_copy,
                    src_ref=ins[a], dst_ref=views[a](outs[a], me),
                    send_sem=send_sems.at[7 * a + k - 1], recv_sem=recv_sems.at[7 * a + k - 1],
                    device_id=peer, device_id_type=MESH)
                recv = functools.partial(
                    pltpu.make_async_remote_copy,
                    src_ref=ins[a], dst_ref=views[a](outs[a], theirs),
                    send_sem=send_sems.at[7 * a + k - 1], recv_sem=recv_sems.at[7 * a + k - 1],
                    device_id=peer, device_id_type=MESH)
                out.append((send, recv))
        return out

    def local(ins, outs, scr):
        _, me = _peer(0)
        return [pltpu.make_async_copy(ins[a], views[a](outs[a], me), scr[2].at[a]) for a in range(na)]

    def start(ins, outs, scr):
        for cp in local(ins, outs, scr):
            cp.start()
        for send, _ in copies(ins, outs, scr):
            send().start()

    def wait(ins, outs, scr):
        for _, recv in copies(ins, outs, scr):
            recv().wait_recv()
        for send, _ in copies(ins, outs, scr):
            send().wait_send()
        for cp in local(ins, outs, scr):
            cp.wait()

    return _Comm(shards, [HBM_SPEC] * na, out_shapes, [HBM_SPEC] * na,
                 [pltpu.SemaphoreType.DMA((7 * na,)), pltpu.SemaphoreType.DMA((7 * na,)),
                  pltpu.SemaphoreType.DMA((na,))], start, wait)


def _forward_comm(gathered, views):
    na = len(gathered)
    shapes = [jax.ShapeDtypeStruct(g.shape, g.dtype) for g in gathered]

    def copies(outs, scr):
        send_sems, recv_sems = scr
        sibling, _ = _peer(SIBLING)
        out = []
        for a in range(na):
            for n, k in enumerate(SAME_CORE):
                _, mine = _peer(k)
                _, theirs = _peer(k ^ SIBLING)
                send = functools.partial(
                    pltpu.make_async_remote_copy,
                    src_ref=views[a](outs[a], mine), dst_ref=views[a](outs[a], mine),
                    send_sem=send_sems.at[3 * a + n], recv_sem=recv_sems.at[3 * a + n],
                    device_id=sibling, device_id_type=MESH)
                recv = functools.partial(
                    pltpu.make_async_remote_copy,
                    src_ref=views[a](outs[a], mine), dst_ref=views[a](outs[a], theirs),
                    send_sem=send_sems.at[3 * a + n], recv_sem=recv_sems.at[3 * a + n],
                    device_id=sibling, device_id_type=MESH)
                out.append((send, recv))
        return out

    def start(ins, outs, scr):
        for send, _ in copies(outs, scr):
            send().start()

    def wait(ins, outs, scr):
        for _, recv in copies(outs, scr):
            recv().wait_recv()
        for send, _ in copies(outs, scr):
            send().wait_send()

    return _Comm(gathered, [HBM_SPEC] * na, shapes, [HBM_SPEC] * na,
                 [pltpu.SemaphoreType.DMA((3 * na,)), pltpu.SemaphoreType.DMA((3 * na,))], start, wait,
                 aliases={a: a for a in range(na)})


def _all_gather(shards, views, out_shapes, name):
    na = len(shards)
    level1 = _gather_comm(shards, views, out_shapes, [(SIBLING,) + SAME_CORE] * na)

    def body(*refs):
        ins, outs = refs[:na], refs[na:2 * na]
        send_sems, recv_sems, local_sems, fwd_send, fwd_recv = refs[2 * na:]
        sibling, _ = _peer(SIBLING)
        scr = (send_sems, recv_sems, local_sems)
        level1.start(ins, outs, scr)
        passed, landing = [], []
        for a in range(na):
            for n, k in enumerate(SAME_CORE):
                peer, mine = _peer(k)
                _, theirs = _peer(k ^ SIBLING)
                pltpu.make_async_remote_copy(
                    src_ref=ins[a], dst_ref=views[a](outs[a], mine),
                    send_sem=send_sems.at[7 * a + k - 1], recv_sem=recv_sems.at[7 * a + k - 1],
                    device_id=peer, device_id_type=MESH).wait_recv()
                fwd = pltpu.make_async_remote_copy(
                    src_ref=views[a](outs[a], mine), dst_ref=views[a](outs[a], mine),
                    send_sem=fwd_send.at[3 * a + n], recv_sem=fwd_recv.at[3 * a + n],
                    device_id=sibling, device_id_type=MESH)
                fwd.start()
                passed.append(fwd)
                landing.append(pltpu.make_async_remote_copy(
                    src_ref=views[a](outs[a], mine), dst_ref=views[a](outs[a], theirs),
                    send_sem=fwd_send.at[3 * a + n], recv_sem=fwd_recv.at[3 * a + n],
                    device_id=sibling, device_id_type=MESH))
        for a in range(na):
            _, theirs = _peer(SIBLING)
            pltpu.make_async_remote_copy(
                src_ref=ins[a], dst_ref=views[a](outs[a], theirs),
                send_sem=send_sems.at[7 * a + SIBLING - 1], recv_sem=recv_sems.at[7 * a + SIBLING - 1],
                device_id=sibling, device_id_type=MESH).wait_recv()
        for cp in landing:
            cp.wait_recv()
        for cp in passed:
            cp.wait_send()
        _, me = _peer(0)
        for a in range(na):
            for k in (SIBLING,) + SAME_CORE:
                peer, _ = _peer(k)
                pltpu.make_async_remote_copy(
                    src_ref=ins[a], dst_ref=views[a](outs[a], me),
                    send_sem=send_sems.at[7 * a + k - 1], recv_sem=recv_sems.at[7 * a + k - 1],
                    device_id=peer, device_id_type=MESH).wait_send()
            pltpu.make_async_copy(ins[a], views[a](outs[a], me), local_sems.at[a]).wait()

    return pl.pallas_call(
        body, name=name,
        in_specs=[HBM_SPEC] * na, out_specs=[HBM_SPEC] * na, out_shape=out_shapes,
        scratch_shapes=level1.scratch + [pltpu.SemaphoreType.DMA((3 * na,)), pltpu.SemaphoreType.DMA((3 * na,))],
    )(*[_in_hbm(s) for s in shards])


def _sibling_comm(grads):
    na = len(grads)
    shapes = [jax.ShapeDtypeStruct(g.shape[:2] + g.shape[3:], g.dtype) for g in grads]

    def copies(ins, outs, scr):
        x, y, c = _place()
        return [pltpu.make_async_remote_copy(
            src_ref=ins[a].at[:, :, 1 - c], dst_ref=outs[a],
            send_sem=scr[0].at[a], recv_sem=scr[1].at[a],
            device_id=(x, y, 1 - c), device_id_type=MESH) for a in range(na)]

    def start(ins, outs, scr):
        for cp in copies(ins, outs, scr):
            cp.start()

    def wait(ins, outs, scr):
        for cp in copies(ins, outs, scr):
            cp.wait()

    return _Comm(grads, [HBM_SPEC] * na, shapes, [HBM_SPEC] * na,
                 [pltpu.SemaphoreType.DMA((na,)), pltpu.SemaphoreType.DMA((na,))], start, wait)


def _chips_comm(sums, every_device=False):
    na = len(sums)
    shapes = [jax.ShapeDtypeStruct(s.shape, s.dtype) for s in sums]
    relations = EVERYONE if every_device else SAME_CORE
    nr = len(relations)

    def block(px, py, pc):
        return 4 * px + 2 * py + pc if every_device else 2 * px + py

    def copies(ins, outs, scr):
        mine = block(*_place())
        out = []
        for a in range(na):
            for n, k in enumerate(relations):
                peer, _ = _peer(k)
                theirs = block(*peer)
                send = functools.partial(
                    pltpu.make_async_remote_copy,
                    src_ref=ins[a].at[:, theirs], dst_ref=outs[a].at[:, mine],
                    send_sem=scr[0].at[nr * a + n], recv_sem=scr[1].at[nr * a + n],
                    device_id=peer, device_id_type=MESH)
                recv = functools.partial(
                    pltpu.make_async_remote_copy,
                    src_ref=ins[a].at[:, mine], dst_ref=outs[a].at[:, theirs],
                    send_sem=scr[0].at[nr * a + n], recv_sem=scr[1].at[nr * a + n],
                    device_id=peer, device_id_type=MESH)
                out.append((send, recv))
        return out

    def local(ins, outs, scr):
        mine = block(*_place())
        return [pltpu.make_async_copy(ins[a].at[:, mine], outs[a].at[:, mine], scr[2].at[a]) for a in range(na)]

    def start(ins, outs, scr):
        for cp in local(ins, outs, scr):
            cp.start()
        for send, _ in copies(ins, outs, scr):
            send().start()

    def wait(ins, outs, scr):
        for _, recv in copies(ins, outs, scr):
            recv().wait_recv()
        for send, _ in copies(ins, outs, scr):
            send().wait_send()
        for cp in local(ins, outs, scr):
            cp.wait()

    return _Comm(sums, [HBM_SPEC] * na, shapes, [HBM_SPEC] * na,
                 [pltpu.SemaphoreType.DMA((nr * na,)), pltpu.SemaphoreType.DMA((nr * na,)),
                  pltpu.SemaphoreType.DMA((na,))], start, wait)


def _small_sum_comm(arrays):
    na = len(arrays)

    def copies(ins, scr):
        bufs, send_sems, recv_sems = scr[:na], scr[na], scr[na + 1]
        _, me = _peer(0)
        out = []
        for a in range(na):
            for k in EVERYONE:
                peer, theirs = _peer(k)
                sems = dict(send_sem=send_sems.at[7 * a + k - 1], recv_sem=recv_sems.at[7 * a + k - 1])
                send = functools.partial(
                    pltpu.make_async_remote_copy,
                    src_ref=ins[a], dst_ref=bufs[a].at[me], device_id=peer, device_id_type=MESH, **sems)
                recv = functools.partial(
                    pltpu.make_async_remote_copy,
                    src_ref=ins[a], dst_ref=bufs[a].at[theirs], device_id=peer, device_id_type=MESH, **sems)
                out.append((send, recv))
        return out

    def start(ins, outs, scr):
        _, me = _peer(0)
        for a in range(na):
            scr[a][me] = ins[a][...]
        for send, _ in copies(ins, scr):
            send().start()

    def wait(ins, outs, scr):
        for _, recv in copies(ins, scr):
            recv().wait_recv()
        for send, _ in copies(ins, scr):
            send().wait_send()
        for a in range(na):
            acc = scr[a][0]
            for j in range(1, N_DEV):
                acc = acc + scr[a][j]
            outs[a][...] = acc

    return _Comm(arrays, [VMEM_SPEC] * na, [jax.ShapeDtypeStruct(s.shape, F32) for s in arrays], [VMEM_SPEC] * na,
                 [pltpu.VMEM((N_DEV,) + s.shape, F32) for s in arrays]
                 + [pltpu.SemaphoreType.DMA((7 * na,)), pltpu.SemaphoreType.DMA((7 * na,))], start, wait)


def _prep_weights(ffn1, ffn2, w_in, w_out, name):
    rf, D = ffn1[2].shape
    ri, ro = w_in.shape[1], w_out.shape[0]

    def body(g1, u1, d1, g2, u2, d2, wi, wo, p1_ref, p2_ref, pi_ref, po_ref):
        for p_ref, shards in ((p1_ref, (g1, u1, d1)), (p2_ref, (g2, u2, d2))):
            for k, shard in enumerate(shards):
                p_ref[k] = shard[...].astype(BF16)
        pi_ref[...] = wi[...].T.astype(BF16)
        po_ref[...] = wo[...].astype(BF16)

    args = (*ffn1, *ffn2, w_in, w_out)
    whole = lambda shape: pl.BlockSpec(shape, lambda i: (0,) * len(shape))
    out_shapes = [(3, rf, D), (3, rf, D), (ri, D), (ro, D)]
    return _call(
        body, name=name, grid=(1,),
        in_specs=[whole(a.shape) for a in args], out_specs=[whole(s) for s in out_shapes],
        out_shape=[jax.ShapeDtypeStruct(s, BF16) for s in out_shapes],
        scratch_shapes=[], vmem_mib=48, args=args)


def _load_weights(w_hbm, w_vmem, sem):
    @pl.when(pl.program_id(0) == 0)
    def _():
        copies = [pltpu.make_async_copy(w_hbm.at[k], w_vmem.at[k], sem.at[k]) for k in range(3)]
        for cp in copies:
            cp.start()
        for cp in copies:
            cp.wait()


def _ffn_fwd(x, g, w3, tm, cf, name, rider=None, head=None):
    T, D = x.shape
    F = w3.shape[1]
    n_head = 0 if head is None else 2

    def body(x_ref, g_ref, w_hbm, *refs):
        head_refs, refs = refs[:n_head], refs[n_head:]
        if head is None:
            (xo_ref, h_ref, dau_ref, dag_ref, act_ref, wv, sem) = refs
        else:
            (dx_ref, dob_ref, dgf_ref, loss_ref, h_ref, dau_ref, dag_ref, act_ref, wv, sem) = refs
        _load_weights(w_hbm, wv, sem)
        xhat, _ = _rms_stats(x_ref[...])
        hb = (xhat * g_ref[...]).astype(BF16)
        h_ref[...] = hb
        for lo in range(0, F, cf):
            gate = _dot(hb, wv[0, lo:lo + cf, :], NT)
            up = _dot(hb, wv[1, lo:lo + cf, :], NT)
            sig = _sigmoid(gate)
            silu = gate * sig
            dau_ref[:, lo:lo + cf] = silu.astype(BF16)
            dag_ref[:, lo:lo + cf] = (up * (sig * (1.0 + gate * (1.0 - sig)))).astype(BF16)
            act_ref[:, lo:lo + cf] = (silu * up).astype(BF16)
        x_out = x_ref[...] + FFN_RES * _dot(act_ref[...], wv[2], NN)
        if head is None:
            xo_ref[...] = x_out
            return

        @pl.when(pl.program_id(0) == 0)
        def _():
            dgf_ref[...] = jnp.zeros_like(dgf_ref)
            loss_ref[...] = jnp.zeros_like(loss_ref)

        gf_ref, tgt_ref = head_refs
        yhat, rstd = _rms_stats(x_out)
        gf = gf_ref[...]
        err = yhat * gf - tgt_ref[...]
        loss_ref[...] += (0.5 / D) * jnp.sum(err * err)
        dx, dgf = _rms_bwd(yhat, rstd, gf, err * (1.0 / D))
        dx_ref[...] = dx
        dob_ref[...] = (FFN_RES * dx).astype(BF16)
        dgf_ref[...] += dgf

    row = pl.BlockSpec((tm, D), lambda i: (i, 0))
    hid = pl.BlockSpec((tm, F), lambda i: (i, 0))
    vec = pl.BlockSpec((1, D), lambda i: (0, 0))
    row_f32, row_bf16 = jax.ShapeDtypeStruct((T, D), F32), jax.ShapeDtypeStruct((T, D), BF16)
    if head is None:
        first_specs, first_shapes = [row], [row_f32]
    else:
        first_specs = [row, row, vec, pl.BlockSpec((1, 128), lambda i: (0, 0))]
        first_shapes = [row_f32, row_bf16, jax.ShapeDtypeStruct((1, D), F32), jax.ShapeDtypeStruct((1, 128), F32)]
    return _call(
        body, name=name, grid=(T // tm,),
        in_specs=[row, vec, HBM_SPEC] + ([] if head is None else [vec, row]),
        out_specs=first_specs + [row, hid, hid, hid],
        out_shape=first_shapes + [row_bf16] + [jax.ShapeDtypeStruct((T, F), BF16)] * 3,
        scratch_shapes=[pltpu.VMEM((3, F, D), BF16), pltpu.SemaphoreType.DMA((3,))],
        vmem_mib=60, args=(x, g, w3) + (() if head is None else tuple(head)), rider=rider)


def _ffn_dgrad(dout, x, g, dau, dag, w3, tm, cf, name, rider=None):
    T, D = x.shape
    F = w3.shape[1]

    def body(do_ref, x_ref, g_ref, dau_ref, dag_ref, w_hbm, dx_ref, dgate_ref, dup_ref, dg_ref, wv, sem):
        _load_weights(w_hbm, wv, sem)

        @pl.when(pl.program_id(0) == 0)
        def _():
            dg_ref[...] = jnp.zeros_like(dg_ref)

        dob = (FFN_RES * do_ref[...]).astype(BF16)
        for lo in range(0, F, cf):
            dact = _dot(dob, wv[2, lo:lo + cf, :], NT)
            dup_ref[:, lo:lo + cf] = (dact * dau_ref[:, lo:lo + cf].astype(F32)).astype(BF16)
            dgate_ref[:, lo:lo + cf] = (dact * dag_ref[:, lo:lo + cf].astype(F32)).astype(BF16)
        dh = _dot(dgate_ref[...], wv[0], NN) + _dot(dup_ref[...], wv[1], NN)
        xhat, rstd = _rms_stats(x_ref[...])
        dx, dg = _rms_bwd(xhat, rstd, g_ref[...], dh)
        dx_ref[...] = do_ref[...] + dx
        dg_ref[...] += dg

    row = pl.BlockSpec((tm, D), lambda i: (i, 0))
    hid = pl.BlockSpec((tm, F), lambda i: (i, 0))
    vec = pl.BlockSpec((1, D), lambda i: (0, 0))
    return _call(
        body, name=name, grid=(T // tm,),
        in_specs=[row, row, vec, hid, hid, HBM_SPEC],
        out_specs=[row, hid, hid, vec],
        out_shape=[jax.ShapeDtypeStruct((T, D), F32), jax.ShapeDtypeStruct((T, F), BF16),
                   jax.ShapeDtypeStruct((T, F), BF16), jax.ShapeDtypeStruct((1, D), F32)],
        scratch_shapes=[pltpu.VMEM((3, F, D), BF16), pltpu.SemaphoreType.DMA((3,))],
        vmem_mib=52, args=(dout, x, g, dau, dag, w3), rider=rider)


def _wgrad(lhs, rhs, tm, tf, name, rider=None):
    T, F = lhs[0].shape
    D = rhs.shape[1]
    K = len(lhs)

    def body(*refs):
        lhs_refs, rhs_ref, dw_ref, accs = refs[:K], refs[K], refs[K + 1], refs[K + 2:]
        i = pl.program_id(1)

        @pl.when(i == 0)
        def _():
            for acc in accs:
                acc[...] = jnp.zeros_like(acc)

        rv = rhs_ref[...]
        for acc, lhs_ref in zip(accs, lhs_refs):
            acc[...] += _dot(lhs_ref[...], rv, TN)

        @pl.when(i == pl.num_programs(1) - 1)
        def _():
            for k, acc in enumerate(accs):
                dw_ref[k] = acc[...].astype(BF16)

    hid = pl.BlockSpec((tm, tf), lambda f, i: (i, f))
    return _call(
        body, name=name, grid=(F // tf, T // tm),
        in_specs=[hid] * K + [pl.BlockSpec((tm, D), lambda f, i: (i, 0))],
        out_specs=[pl.BlockSpec((K, tf, D), lambda f, i: (0, f, 0))],
        out_shape=[jax.ShapeDtypeStruct((K, F, D), BF16)],
        scratch_shapes=[pltpu.VMEM((tf, D), F32)] * K,
        vmem_mib=56, args=(*lhs, rhs), rider=rider)


def _lru_gates(xr, bda_ref, bdx_ref, vec_ref):
    xrb = xr.astype(BF16)
    r = _sigmoid(_dot(xrb, bda_ref[...], NN) + vec_ref[V_BA:V_BA + 1, :])
    ig = _sigmoid(_dot(xrb, bdx_ref[...], NN) + vec_ref[V_BX:V_BX + 1, :])
    sp = _softplus_neg(vec_ref[V_LAM:V_LAM + 1, :])
    log_a = (-LRU_C * sp) * r
    a = jnp.exp(log_a)
    mult = jnp.sqrt(_neg_expm1(2.0 * log_a))
    return xrb, r, ig, sp, a, mult


def _layernorm_stats(u1):
    xc = u1 - jnp.mean(u1, axis=-1, keepdims=True)
    rs = lax.rsqrt(jnp.mean(xc * xc, axis=-1, keepdims=True) + LN_EPS)
    return xc * rs, rs


def _mix_core_fwd(x1, g, w_in_t, w_out, bda, bdx, cw, lw, vec, tm, name, rider=None):
    T, D = x1.shape
    W = cw.shape[1]
    assert tm >= CONV_HALO and w_in_t.shape[0] == 4 * W

    def body(x1_ref, g_ref, wi_ref, wo_ref, bda_ref, bdx_ref, cw_ref, lw_ref, vec_ref,
             x2_ref, z_ref, mix_ref, u1_ref, xr_ref, hst_ref, ubuf, rbuf, hc):
        @pl.when(pl.program_id(0) == 0)
        def _():
            ubuf[0:CONV_HALO, :] = jnp.zeros((CONV_HALO, W), F32)
            rbuf[0:LRU_HALO, :] = jnp.zeros((LRU_HALO, W), F32)
            hc[...] = jnp.zeros_like(hc)

        xhat, _ = _rms_stats(x1_ref[...])
        z_ref[...] = _dot((xhat * g_ref[...]).astype(BF16), wi_ref[...], NT)

        ubuf[CONV_HALO:CONV_HALO + tm, :] = z_ref[:, 0:W] * _sigmoid(z_ref[:, W:2 * W])
        u1 = jnp.zeros((tm, W), F32) + vec_ref[V_CB:V_CB + 1, :]
        base = CONV_HALO - (CONV_K - 1)
        for off, win in _row_windows(ubuf, tm, range(base, base + CONV_K)):
            u1 = u1 + cw_ref[off - base:off - base + 1, :] * win
        ubuf[0:CONV_HALO, :] = ubuf[tm:tm + CONV_HALO, :]
        u1_ref[...] = u1
        xh, _ = _layernorm_stats(u1)
        u2 = xh * vec_ref[V_LNG:V_LNG + 1, :] + vec_ref[V_LNB:V_LNB + 1, :]
        ub = (u2 * _sigmoid(u2)).astype(BF16)
        mix_ref[:, 0:W] = ub

        rbuf[LRU_HALO:LRU_HALO + tm, :] = z_ref[:, 2 * W:3 * W]
        xr = jnp.zeros((tm, W), F32) + vec_ref[V_LCB:V_LCB + 1, :]
        for k in range(LRU_K):
            off = LRU_HALO - (LRU_K - 1) + k
            xr = xr + lw_ref[k:k + 1, :] * rbuf[off:off + tm, :]
        rbuf[0:LRU_HALO, :] = rbuf[tm:tm + LRU_HALO, :]
        xr_ref[...] = xr
        _, _, ig, _, a, mult = _lru_gates(xr, bda_ref, bdx_ref, vec_ref)
        hc[0:1, :] = _scan_rows(a, mult * (ig * xr), hc[0:1, :], hst_ref)
        gl, _ = _gelu_parts(z_ref[:, 3 * W:4 * W])
        yb = (hst_ref[...] * gl).astype(BF16)
        mix_ref[:, W:2 * W] = yb

        x2_ref[...] = x1_ref[...] + _dot(ub, wo_ref[0:W, :], NN) + _dot(yb, wo_ref[W:2 * W, :], NN)

    full = lambda a: pl.BlockSpec(a.shape, lambda i: (0,) * a.ndim)
    tile = lambda n: pl.BlockSpec((tm, n), lambda i: (i, 0))
    return _call(
        body, name=name, grid=(T // tm,),
        in_specs=[tile(D), full(g), full(w_in_t), full(w_out), full(bda), full(bdx), full(cw), full(lw), full(vec)],
        out_specs=[tile(D), tile(4 * W), tile(2 * W), tile(W), tile(W), tile(W)],
        out_shape=[jax.ShapeDtypeStruct((T, D), F32), jax.ShapeDtypeStruct((T, 4 * W), F32),
                   jax.ShapeDtypeStruct((T, 2 * W), BF16), jax.ShapeDtypeStruct((T, W), F32),
                   jax.ShapeDtypeStruct((T, W), F32), jax.ShapeDtypeStruct((T, W), F32)],
        scratch_shapes=[pltpu.VMEM((tm + CONV_HALO, W), F32), pltpu.VMEM((tm + LRU_HALO, W), F32),
                        pltpu.VMEM((8, W), F32)],
        vmem_mib=56, args=(x1, g, w_in_t, w_out, bda, bdx, cw, lw, vec), rider=rider)


def _mix_bwd(dx2, z, u1, xr, hst, x1, mix, g, w_in_t, w_out, bda, bdx, cw, lw, vec, tm, name, rider=None):
    T, D = dx2.shape
    W = cw.shape[1]
    nt = T // tm
    assert tm >= CONV_HALO and tm % CONV_HALO == 0

    def body(dx_ref, z_ref, zh_ref, u1_ref, xr_ref, h_ref, hh_ref, wo_ref, bda_ref, bdx_ref, cw_ref, lw_ref, vec_ref,
             x1_ref, mix_ref, g_ref, wi_ref,
             dx1_ref, sg_ref, dbda_ref, dbdx_ref, dob_ref, dg_ref, dwi_ref, dwo_ref,
             u0buf, du1buf, rxbuf, dxrbuf, gbuf, gc, spacc, dz_ref, ai_ref, ao_ref):
        i = pl.program_id(0)
        first = i == nt - 1
        row = lax.broadcasted_iota(jnp.int32, (tm, W), 0)

        @pl.when(i == 0)
        def _():
            sg_ref[...] = jnp.zeros_like(sg_ref)
            dbda_ref[...] = jnp.zeros_like(dbda_ref)
            dbdx_ref[...] = jnp.zeros_like(dbdx_ref)
            du1buf[tm:tm + CONV_HALO, :] = jnp.zeros((CONV_HALO, W), F32)
            dxrbuf[tm:tm + LRU_HALO, :] = jnp.zeros((LRU_HALO, W), F32)
            gc[...] = jnp.zeros_like(gc)
            spacc[...] = jnp.zeros_like(spacc)
            dg_ref[...] = jnp.zeros_like(dg_ref)
            ai_ref[...] = jnp.zeros_like(ai_ref)
            ao_ref[...] = jnp.zeros_like(ao_ref)

        def accum(r, val):
            sg_ref[r:r + 1, :] += jnp.sum(val, axis=0, keepdims=True)

        x1hat, x1rstd = _rms_stats(x1_ref[...])
        gain = g_ref[...]
        hb = (x1hat * gain).astype(BF16)

        def in_proj_bwd(lo, hi):
            dzb = dz_ref[:, lo:hi]
            ai_ref[lo:hi, :] += _dot(dzb, hb, TN)
            return _dot(dzb, wi_ref[lo:hi, :], NN)

        dxb = dx_ref[...].astype(BF16)
        ao_ref[...] += _dot(mix_ref[...], dxb, TN)
        dmix = _dot(dxb, wo_ref[...], NT)
        d_u = dmix[:, 0:W]
        d_yr = dmix[:, W:2 * W]

        xh, rs = _layernorm_stats(u1_ref[...])
        ln_g = vec_ref[V_LNG:V_LNG + 1, :]
        u2 = xh * ln_g + vec_ref[V_LNB:V_LNB + 1, :]
        s2 = _sigmoid(u2)
        d_u2 = d_u * (s2 * (1.0 + u2 * (1.0 - s2)))
        accum(G_LNG, d_u2 * xh)
        accum(G_LNB, d_u2)
        d_xh = d_u2 * ln_g
        d_u1 = rs * (d_xh - jnp.mean(d_xh, axis=-1, keepdims=True)
                     - xh * jnp.mean(d_xh * xh, axis=-1, keepdims=True))
        accum(G_CB, d_u1)
        halo_on = jnp.where(first, 0.0, 1.0)
        u0buf[0:CONV_HALO, :] = halo_on * (zh_ref[:, 0:W] * _sigmoid(zh_ref[:, W:2 * W]))
        cv = z_ref[:, 0:W]
        sgc = _sigmoid(z_ref[:, W:2 * W])
        u0buf[CONV_HALO:CONV_HALO + tm, :] = cv * sgc
        du1buf[0:tm, :] = d_u1
        base = CONV_HALO - (CONV_K - 1)
        for off, win in _row_windows(u0buf, tm, range(base, base + CONV_K)):
            accum(G_CW + off - base, d_u1 * win)
        d_u0 = jnp.zeros((tm, W), F32)
        for off, win in _row_windows(du1buf, tm, range(0, CONV_K)):
            d_u0 = d_u0 + cw_ref[CONV_K - 1 - off:CONV_K - off, :] * win
        du1buf[tm:tm + CONV_HALO, :] = du1buf[0:CONV_HALO, :]
        dz_ref[:, 0:W] = (d_u0 * sgc).astype(BF16)
        dz_ref[:, W:2 * W] = (d_u0 * cv * (sgc * (1.0 - sgc))).astype(BF16)
        dh = in_proj_bwd(0, 2 * W)

        xrv = xr_ref[...]
        xrb, r, ig, sp, a, mult = _lru_gates(xrv, bda_ref, bdx_ref, vec_ref)
        h = h_ref[...]
        gl, dgl = _gelu_parts(z_ref[:, 3 * W:4 * W])
        dz_ref[:, 3 * W:4 * W] = (d_yr * h * dgl).astype(BF16)
        dh = dh + in_proj_bwd(3 * W, 4 * W)
        a_next = jnp.where(row == tm - 1, 1.0, pltpu.roll(a, tm - 1, 0))
        g_first = _scan_rows(a_next, d_yr * gl, gc[0:1, :], gbuf, reverse=True)
        g = gbuf[...]
        gc[0:1, :] = a[0:1, :] * g_first
        hprev = jnp.where(row == 0, halo_on * hh_ref[LRU_HALO - 1:LRU_HALO, :], pltpu.roll(h, 1, 0))
        d_log_a = (g * hprev) * a - (g * ig * xrv) * (a * a) / mult
        d_ig = g * mult * xrv
        d_xr = g * mult * ig
        spacc[0:1, :] += jnp.sum(d_log_a * r, axis=0, keepdims=True)
        d_pa32 = (d_log_a * (-LRU_C * sp)) * (r * (1.0 - r))
        d_px32 = d_ig * (ig * (1.0 - ig))
        accum(G_BA, d_pa32)
        accum(G_BX, d_px32)
        d_pa = d_pa32.astype(BF16)
        d_px = d_px32.astype(BF16)
        d_xr = d_xr + _dot(d_pa, bda_ref[...], NT) + _dot(d_px, bdx_ref[...], NT)
        dbda_ref[...] += _dot(xrb, d_pa, TN)
        dbdx_ref[...] += _dot(xrb, d_px, TN)
        accum(G_LCB, d_xr)
        rxbuf[0:LRU_HALO, :] = halo_on * zh_ref[CONV_HALO - LRU_HALO:CONV_HALO, 2 * W:3 * W]
        rxbuf[LRU_HALO:LRU_HALO + tm, :] = z_ref[:, 2 * W:3 * W]
        dxrbuf[0:tm, :] = d_xr
        d_rx = jnp.zeros((tm, W), F32)
        for k in range(LRU_K):
            off = LRU_HALO - (LRU_K - 1) + k
            accum(G_LW + k, d_xr * rxbuf[off:off + tm, :])
            d_rx = d_rx + lw_ref[k:k + 1, :] * dxrbuf[LRU_K - 1 - k:LRU_K - 1 - k + tm, :]
        dxrbuf[tm:tm + LRU_HALO, :] = dxrbuf[0:LRU_HALO, :]
        dz_ref[:, 2 * W:3 * W] = d_rx.astype(BF16)
        dh = dh + in_proj_bwd(2 * W, 3 * W)

        dx, dg = _rms_bwd(x1hat, x1rstd, gain, dh)
        dx1 = dx_ref[...] + dx
        dx1_ref[...] = dx1
        dob_ref[...] = (FFN_RES * dx1).astype(BF16)
        dg_ref[...] += dg

        @pl.when(first)
        def _():
            lam = vec_ref[V_LAM:V_LAM + 1, :]
            sg_ref[G_LAM:G_LAM + 1, :] = LRU_C * _sigmoid(-lam) * spacc[0:1, :]
            dwi_ref[...] = ai_ref[...].astype(BF16)
            dwo_ref[...] = ao_ref[...].astype(BF16)

    full = lambda a: pl.BlockSpec(a.shape, lambda i: (0,) * a.ndim)
    tile = lambda n: pl.BlockSpec((tm, n), lambda i: (nt - 1 - i, 0))
    halo = lambda rows, n: pl.BlockSpec(
        (rows, n), lambda i: (jnp.maximum((nt - 1 - i) * (tm // rows) - 1, 0), 0))
    const = lambda r, c: pl.BlockSpec((r, c), lambda i: (0, 0))
    return _call(
        body, name=name, grid=(nt,),
        in_specs=[tile(D), tile(4 * W), halo(CONV_HALO, 4 * W), tile(W), tile(W), tile(W), halo(LRU_HALO, W),
                  full(w_out), full(bda), full(bdx), full(cw), full(lw), full(vec),
                  tile(D), tile(2 * W), full(g), full(w_in_t)],
        out_specs=[tile(D), const(G_ROWS, W), const(W, W), const(W, W),
                   tile(D), const(1, D), const(4 * W, D), const(2 * W, D)],
        out_shape=[jax.ShapeDtypeStruct((T, D), F32), jax.ShapeDtypeStruct((G_ROWS, W), F32),
                   jax.ShapeDtypeStruct((W, W), F32), jax.ShapeDtypeStruct((W, W), F32),
                   jax.ShapeDtypeStruct((T, D), BF16), jax.ShapeDtypeStruct((1, D), F32),
                   jax.ShapeDtypeStruct((4 * W, D), BF16), jax.ShapeDtypeStruct((2 * W, D), BF16)],
        scratch_shapes=[pltpu.VMEM((tm + CONV_HALO, W), F32), pltpu.VMEM((tm + CONV_HALO, W), F32),
                        pltpu.VMEM((tm + LRU_HALO, W), F32), pltpu.VMEM((tm + LRU_HALO, W), F32),
                        pltpu.VMEM((tm, W), F32), pltpu.VMEM((8, W), F32), pltpu.VMEM((8, W), F32),
                        pltpu.VMEM((tm, 4 * W), BF16), pltpu.VMEM((4 * W, D), F32), pltpu.VMEM((2 * W, D), F32)],
        vmem_mib=60, args=(dx2, z, z, u1, xr, hst, hst, w_out, bda, bdx, cw, lw, vec, x1, mix, g, w_in_t),
        rider=rider)


def _pair_add(full, recv, name):
    K, _, _, rows, D = full.shape

    def body(c_ref, a_ref, b_ref, o_ref):
        o_ref[...] = (a_ref[...].astype(F32) + b_ref[...].astype(F32)).astype(BF16)

    c = lax.axis_index("c").astype(jnp.int32).reshape((1,))
    return _call(
        body, name=name, grid=(K, N_CHIP), num_scalar_prefetch=1,
        in_specs=[pl.BlockSpec((None, None, None, rows, D), lambda k, q, c_ref: (k, q, c_ref[0], 0, 0)),
                  pl.BlockSpec((None, None, rows, D), lambda k, q, c_ref: (k, q, 0, 0))],
        out_specs=pl.BlockSpec((None, None, rows, D), lambda k, q, c_ref: (k, q, 0, 0)),
        out_shape=jax.ShapeDtypeStruct(recv.shape, BF16),
        scratch_shapes=[], vmem_mib=16, args=(c, full, recv))


def _adamw_update(wv, gv, mv, vv):
    m2 = ADAM_B1 * mv + (1.0 - ADAM_B1) * gv
    v2 = ADAM_B2 * vv + (1.0 - ADAM_B2) * (gv * gv)
    m_hat = m2 / (1.0 - ADAM_B1 ** ADAM_STEP)
    v_hat = v2 / (1.0 - ADAM_B2 ** ADAM_STEP)
    return -ADAM_LR * (m_hat / (jnp.sqrt(v_hat) + ADAM_EPS) + ADAM_WD * wv), m2, v2


def _finish(parts, k, w, m, v, transpose, name):
    _, n_parts, rows, D = parts.shape

    def body(p_ref, w_ref, m_ref, v_ref, g_ref, d_ref, mo_ref, vo_ref):
        acc = p_ref[0].astype(F32)
        for q in range(1, n_parts):
            acc = acc + p_ref[q].astype(F32)
        gv = acc.T if transpose else acc
        g_ref[...] = gv
        d_ref[...], mo_ref[...], vo_ref[...] = _adamw_update(w_ref[...], gv, m_ref[...], v_ref[...])

    whole = pl.BlockSpec(w.shape, lambda i: (0, 0))
    return _call(
        body, name=name, grid=(1,),
        in_specs=[pl.BlockSpec((None, n_parts, rows, D), lambda i: (k, 0, 0, 0)), whole, whole, whole],
        out_specs=[whole] * 4, out_shape=[pltpu.HBM(w.shape, F32)] * 4,
        scratch_shapes=[], vmem_mib=40, args=(parts, w, m, v))


def _adamw_each(ws, gs, ms, vs, name):
    n = len(ws)

    def body(*refs):
        w_refs, g_refs, m_refs, v_refs, outs = refs[:n], refs[n:2 * n], refs[2 * n:3 * n], refs[3 * n:4 * n], refs[4 * n:]
        for k in range(n):
            outs[k][...], outs[n + k][...], outs[2 * n + k][...] = _adamw_update(
                w_refs[k][...], g_refs[k][...], m_refs[k][...], v_refs[k][...])

    shapes = [jax.ShapeDtypeStruct(w.shape, F32) for w in ws]
    return pl.pallas_call(
        body, name=name,
        in_specs=[VMEM_SPEC] * (4 * n), out_specs=[VMEM_SPEC] * (3 * n), out_shape=shapes * 3,
        compiler_params=pltpu.CompilerParams(vmem_limit_bytes=32 * MIB),
    )(*ws, *gs, *ms, *vs)


def _block_diag(w):
    h, d, _ = w.shape
    onto = jnp.eye(h, dtype=w.dtype)
    return (w[:, :, None, :] * onto[:, None, :, None]).reshape(h * d, h * d)


def _diag_blocks(m, h):
    d = m.shape[0] // h
    onto = jnp.eye(h, dtype=m.dtype)
    return (m.reshape(h, d, h, d) * onto[:, None, :, None]).sum(axis=2)


def _reduce_level1(full, tag):
    got = _run_comm(_sibling_comm(full), "rs_sibling_" + tag)
    return [_pair_add(a, b, "rs_pair_add_%s%d" % (tag, n)) for n, (a, b) in enumerate(zip(full, got))]


def kernel(x, ffn1_norm, ffn1_w_gate, ffn1_w_up, ffn1_w_down, mix_norm, w_in, conv_dw, conv_dw_bias, conv_ln_g, conv_ln_b, lru_conv_w, lru_conv_b, lru_w_a, lru_b_a, lru_w_x, lru_b_x, lru_lambda, w_out, ffn2_norm, ffn2_w_gate, ffn2_w_up, ffn2_w_down, final_norm, loss_target, m_ffn1_norm, m_ffn1_w_gate, m_ffn1_w_up, m_ffn1_w_down, m_mix_norm, m_w_in, m_conv_dw, m_conv_dw_bias, m_conv_ln_g, m_conv_ln_b, m_lru_conv_w, m_lru_conv_b, m_lru_w_a, m_lru_b_a, m_lru_w_x, m_lru_b_x, m_lru_lambda, m_w_out, m_ffn2_norm, m_ffn2_w_gate, m_ffn2_w_up, m_ffn2_w_down, m_final_norm, v_ffn1_norm, v_ffn1_w_gate, v_ffn1_w_up, v_ffn1_w_down, v_mix_norm, v_w_in, v_conv_dw, v_conv_dw_bias, v_conv_ln_g, v_conv_ln_b, v_lru_conv_w, v_lru_conv_b, v_lru_w_a, v_lru_b_a, v_lru_w_x, v_lru_b_x, v_lru_lambda, v_w_out, v_ffn2_norm, v_ffn2_w_gate, v_ffn2_w_up, v_ffn2_w_down, v_final_norm):
    T, D = x.shape[1], x.shape[2]
    F = ffn1_w_down.shape[0] * N_DEV
    rf = ffn1_w_down.shape[0]
    ri = w_in.shape[1]
    ro = w_out.shape[0]
    W = conv_dw_bias.shape[0]
    wc = conv_dw.shape[1]
    H = lru_w_a.shape[0]
    xs = x.reshape(T, D)
    tgt = loss_target.reshape(T, D)
    tm_ffn = min(256, T)
    tm_fwd = min(512, T)
    cf = 256
    tm_w = min(1024, T)
    tm_w1 = min(2048, T)
    tm_mix = min(256, T)
    tf_w = F // 2
    row = lambda v: v.reshape(1, -1)
    by_owner = lambda a, rows: a.reshape(a.shape[0], N_CHIP, 2, rows, D)

    p3a, p3b, p_in, p_out = _prep_weights(
        (ffn1_w_gate.T, ffn1_w_up.T, ffn1_w_down), (ffn2_w_gate.T, ffn2_w_up.T, ffn2_w_down), w_in, w_out,
        "prep_weights")
    tile_rows = lambda a: jnp.pad(a, ((0, -a.shape[0] % SUBLANES), (0, 0)))
    p_cw = jnp.concatenate([tile_rows(conv_dw), tile_rows(lru_conv_w)], axis=0)
    lw_row = p_cw.shape[0] - SUBLANES
    stacked = lambda r, j: r.at[:, j]
    plain = lambda r, j: r.at[j]
    g3_shape = jax.ShapeDtypeStruct((3, N_DEV, rf, D), BF16)
    (g3a,) = _all_gather([p3a], [stacked], [g3_shape], "ag_ffn1")
    w3a = g3a.reshape(3, F, D)
    bda = _block_diag(lru_w_a).astype(BF16)
    bdx = _block_diag(lru_w_x).astype(BF16)
    vec = jnp.concatenate([tile_rows(v[None]) for v in
                           (conv_dw_bias, conv_ln_g, conv_ln_b, lru_conv_b, lru_b_a, lru_b_x, lru_lambda)], axis=0)

    gather_rest = _gather_comm(
        [p3b, p_in, p_out, p_cw], [stacked, plain, plain, plain],
        [g3_shape, jax.ShapeDtypeStruct((N_DEV, ri, D), BF16), jax.ShapeDtypeStruct((N_DEV, ro, D), BF16),
         jax.ShapeDtypeStruct((N_DEV,) + p_cw.shape, F32)],
        [(SIBLING,) + SAME_CORE, EVERYONE, EVERYONE, EVERYONE])
    (x1, h1, dau1, dag1, act1), (g3b_half, g_in, g_out, g_cw) = _ffn_fwd(
        xs, row(ffn1_norm), w3a, tm_fwd, cf, "ffn1_fwd", rider=gather_rest)
    w_in_t = g_in.reshape(N_DEV * ri, D)
    w_out_f = g_out.reshape(N_DEV * ro, D)
    cw_all = jnp.transpose(g_cw, (1, 0, 2)).reshape(p_cw.shape[0], N_DEV * wc)
    cw = cw_all[0:CONV_K]
    lw = cw_all[lw_row:lw_row + LRU_K]
    (x2, z, mix, u1, xr, hst), (g3b,) = _mix_core_fwd(
        x1, row(mix_norm), w_in_t, w_out_f, bda, bdx, cw, lw, vec, tm_mix, "mix_core_fwd",
        rider=_forward_comm([g3b_half], [stacked]))
    w3b = g3b.reshape(3, F, D)
    dx3, dob2, d_final_norm, loss_part, h3, dau2, dag2, act2 = _ffn_fwd(
        x2, row(ffn2_norm), w3b, tm_fwd, cf, "ffn2_fwd_loss", head=(row(final_norm), tgt))

    dx2, dgate2, dup2, d_ffn2_norm = _ffn_dgrad(dx3, x2, row(ffn2_norm), dau2, dag2, w3b, tm_ffn, cf, "ffn2_dgrad")
    (dw_gu2,) = _wgrad([dgate2, dup2], h3, tm_w, tf_w, "ffn2_wgrad_gu")
    (dw_d2,) = _wgrad([act2], dob2, tm_w1, tf_w, "ffn2_wgrad_d")
    by_device = lambda a: a.reshape(a.shape[0], N_DEV, rf, D)
    (dx1, sg, dbda, dbdx, dob1, d_mix_norm, dw_in_t, dw_out), parts_f2 = _mix_bwd(
        dx2, z, u1, xr, hst, x1, mix, row(mix_norm), w_in_t, w_out_f, bda, bdx, cw, lw, vec, tm_mix, "mix_bwd",
        rider=_chips_comm([by_device(dw_gu2), by_device(dw_d2)], every_device=True))
    sums_io = _reduce_level1([by_owner(dw_in_t[None], ri), by_owner(dw_out[None], ro)], "io")
    (dw_d1,), parts_io = _wgrad([act1], dob1, tm_w1, tf_w, "ffn1_wgrad_d", rider=_chips_comm(sums_io))
    sums_d1 = _reduce_level1([by_owner(dw_d1, rf)], "d1")
    dx0, dgate1, dup1, d_ffn1_norm = _ffn_dgrad(dx1, xs, row(ffn1_norm), dau1, dag1, w3a, tm_ffn, cf, "ffn1_dgrad")

    small = [d_ffn1_norm, d_mix_norm, d_ffn2_norm, d_final_norm, sg, _diag_blocks(dbda, H).reshape(-1, D),
             _diag_blocks(dbdx, H).reshape(-1, D), loss_part]
    (dw_g1,), summed_and_parts = _wgrad(
        [dgate1], h1, tm_w1, tf_w, "ffn1_wgrad_g", rider=_both(_small_sum_comm(small), _chips_comm(sums_d1)))
    summed, parts_d1 = summed_and_parts[:len(small)], summed_and_parts[len(small):]
    sums_g1 = _reduce_level1([by_owner(dw_g1, rf)], "g1")
    (dw_u1,), parts_g1 = _wgrad([dup1], h1, tm_w1, tf_w, "ffn1_wgrad_u", rider=_chips_comm(sums_g1))
    sums_u1 = _reduce_level1([by_owner(dw_u1, rf)], "u1")
    parts_u1 = _run_comm(_chips_comm(sums_u1), "rs_chips_u1")

    g_norm1, g_norm_mix, g_norm2, g_norm_final, g_sg, g_w_a, g_w_x, g_loss = summed
    loss = g_loss[0, 0]
    me = 4 * lax.axis_index("x") + 2 * lax.axis_index("y") + lax.axis_index("c")
    chan = lambda full_g: lax.dynamic_slice_in_dim(full_g, me * wc, wc, axis=1)
    grads = {
        "ffn1_norm": g_norm1.reshape(D), "mix_norm": g_norm_mix.reshape(D), "ffn2_norm": g_norm2.reshape(D),
        "final_norm": g_norm_final.reshape(D),
        "conv_dw_bias": g_sg[G_CB], "conv_ln_g": g_sg[G_LNG], "conv_ln_b": g_sg[G_LNB],
        "lru_conv_b": g_sg[G_LCB], "lru_b_a": g_sg[G_BA], "lru_b_x": g_sg[G_BX], "lru_lambda": g_sg[G_LAM],
        "lru_w_a": g_w_a.reshape(lru_w_a.shape), "lru_w_x": g_w_x.reshape(lru_w_x.shape),
        "conv_dw": chan(g_sg[G_CW:G_CW + CONV_K]), "lru_conv_w": chan(g_sg[G_LW:G_LW + LRU_K]),
    }

    weights = dict(ffn1_norm=ffn1_norm, ffn1_w_gate=ffn1_w_gate, ffn1_w_up=ffn1_w_up, ffn1_w_down=ffn1_w_down, mix_norm=mix_norm, w_in=w_in, conv_dw=conv_dw, conv_dw_bias=conv_dw_bias, conv_ln_g=conv_ln_g, conv_ln_b=conv_ln_b, lru_conv_w=lru_conv_w, lru_conv_b=lru_conv_b, lru_w_a=lru_w_a, lru_b_a=lru_b_a, lru_w_x=lru_w_x, lru_b_x=lru_b_x, lru_lambda=lru_lambda, w_out=w_out, ffn2_norm=ffn2_norm, ffn2_w_gate=ffn2_w_gate, ffn2_w_up=ffn2_w_up, ffn2_w_down=ffn2_w_down, final_norm=final_norm)
    moment1 = dict(ffn1_norm=m_ffn1_norm, ffn1_w_gate=m_ffn1_w_gate, ffn1_w_up=m_ffn1_w_up, ffn1_w_down=m_ffn1_w_down, mix_norm=m_mix_norm, w_in=m_w_in, conv_dw=m_conv_dw, conv_dw_bias=m_conv_dw_bias, conv_ln_g=m_conv_ln_g, conv_ln_b=m_conv_ln_b, lru_conv_w=m_lru_conv_w, lru_conv_b=m_lru_conv_b, lru_w_a=m_lru_w_a, lru_b_a=m_lru_b_a, lru_w_x=m_lru_w_x, lru_b_x=m_lru_b_x, lru_lambda=m_lru_lambda, w_out=m_w_out, ffn2_norm=m_ffn2_norm, ffn2_w_gate=m_ffn2_w_gate, ffn2_w_up=m_ffn2_w_up, ffn2_w_down=m_ffn2_w_down, final_norm=m_final_norm)
    moment2 = dict(ffn1_norm=v_ffn1_norm, ffn1_w_gate=v_ffn1_w_gate, ffn1_w_up=v_ffn1_w_up, ffn1_w_down=v_ffn1_w_down, mix_norm=v_mix_norm, w_in=v_w_in, conv_dw=v_conv_dw, conv_dw_bias=v_conv_dw_bias, conv_ln_g=v_conv_ln_g, conv_ln_b=v_conv_ln_b, lru_conv_w=v_lru_conv_w, lru_conv_b=v_lru_conv_b, lru_w_a=v_lru_w_a, lru_b_a=v_lru_b_a, lru_w_x=v_lru_w_x, lru_b_x=v_lru_b_x, lru_lambda=v_lru_lambda, w_out=v_w_out, ffn2_norm=v_ffn2_norm, ffn2_w_gate=v_ffn2_w_gate, ffn2_w_up=v_ffn2_w_up, ffn2_w_down=v_ffn2_w_down, final_norm=v_final_norm)
    order = list(weights)
    big = {"ffn1_w_gate": (parts_g1[0], 0, True), "ffn1_w_up": (parts_u1[0], 0, True),
           "ffn1_w_down": (parts_d1[0], 0, False), "w_in": (parts_io[0], 0, True), "w_out": (parts_io[1], 0, False),
           "ffn2_w_gate": (parts_f2[0], 0, True), "ffn2_w_up": (parts_f2[0], 1, True),
           "ffn2_w_down": (parts_f2[1], 0, False)}
    delta, new_m, new_v = {}, {}, {}
    for n, (parts, k, d_major) in big.items():
        operands = weights[n], moment1[n], moment2[n]
        if d_major and n != "w_in":
            results = _finish(parts, k, *[a.T for a in operands], False, "finish_" + n)
            grads[n], delta[n], new_m[n], new_v[n] = [r.T for r in results]
        else:
            grads[n], delta[n], new_m[n], new_v[n] = _finish(parts, k, *operands, d_major, "finish_" + n)
    rest = [n for n in order if n not in big]
    updates = _adamw_each([weights[n] for n in rest], [grads[n] for n in rest], [moment1[n] for n in rest],
                          [moment2[n] for n in rest], "adamw_small")
    for k, n in enumerate(rest):
        delta[n], new_m[n], new_v[n] = updates[k], updates[len(rest) + k], updates[2 * len(rest) + k]

    return (loss, dx0.reshape(x.shape), *[grads[n] for n in order], *[delta[n] for n in order],
            *[new_m[n] for n in order], *[new_v[n] for n in order])
```

```python
import functools
import math

import jax
import jax.numpy as jnp
from jax import lax
from jax.experimental import pallas as pl
from jax.experimental.pallas import tpu as pltpu

F32 = jnp.float32
BF16 = jnp.bfloat16
MESH = pl.DeviceIdType.MESH

N_DEV = 8
N_CHIP = 4
SUBLANES = 8
RMS_EPS = 1e-6
LN_EPS = 1e-5
LRU_C = 8.0
CONV_K = 31
LRU_K = 4
CONV_HALO = 32
LRU_HALO = 8
FFN_RES = 0.5
ADAM_LR, ADAM_B1, ADAM_B2, ADAM_EPS, ADAM_WD, ADAM_STEP = 0.001, 0.9, 0.999, 1e-08, 0.01, 10
GELU_K = math.sqrt(2.0 / math.pi)
GELU_C = 0.044715

MIB = 1024 * 1024
NT = (((1,), (1,)), ((), ()))
NN = (((1,), (0,)), ((), ()))
TN = (((0,), (0,)), ((), ()))

V_CB, V_LNG, V_LNB, V_LCB, V_BA, V_BX, V_LAM = range(0, 7 * SUBLANES, SUBLANES)
G_CW = 0
G_CB, G_LNG, G_LNB = 31, 32, 33
G_LW = 34
G_LCB, G_BA, G_BX, G_LAM = 38, 39, 40, 41
G_ROWS = 48

HBM_SPEC = pl.BlockSpec(memory_space=pltpu.HBM)
VMEM_SPEC = pl.BlockSpec(memory_space=pltpu.VMEM)


def _dot(a, b, dims):
    return lax.dot_general(a, b, dims, preferred_element_type=F32)


def _sigmoid(x):
    return 1.0 / (1.0 + jnp.exp(-x))


def _gelu_parts(x):
    x2 = x * x
    th = jnp.tanh(GELU_K * x * (1.0 + GELU_C * x2))
    gl = 0.5 * x * (1.0 + th)
    dgl = 0.5 * (1.0 + th) + 0.5 * x * (1.0 - th * th) * GELU_K * (1.0 + 3.0 * GELU_C * x2)
    return gl, dgl


def _neg_expm1(y):
    series = -y * (1.0 + y * (1.0 / 2) * (1.0 + y * (1.0 / 3) * (1.0 + y * (1.0 / 4) * (1.0 + y * (1.0 / 5) * (1.0 + y * (1.0 / 6))))))
    return jnp.where(y > -0.25, series, 1.0 - jnp.exp(y))


def _softplus_neg(lam):
    t = -lam
    e = jnp.exp(-jnp.abs(t))
    s = 1.0 + e
    log1p_e = jnp.log(s) - ((s - 1.0) - e) / s
    return jnp.maximum(t, 0.0) + log1p_e


def _rms_stats(xv):
    rstd = lax.rsqrt(jnp.mean(xv * xv, axis=-1, keepdims=True) + RMS_EPS)
    return xv * rstd, rstd


def _rms_bwd(xhat, rstd, g, dh):
    dxhat = dh * g
    dx = rstd * (dxhat - xhat * jnp.mean(dxhat * xhat, axis=-1, keepdims=True))
    return dx, jnp.sum(dh * xhat, axis=0, keepdims=True)


def _row_windows(buf_ref, n_rows, offsets):
    total = buf_ref.shape[0]
    full = buf_ref[...]
    for b in range(SUBLANES):
        offs = [o for o in offsets if o % SUBLANES == b]
        if not offs:
            continue
        assert max(offs) + n_rows <= total
        moved = full if b == 0 else pltpu.roll(full, total - b, 0)
        for o in offs:
            yield o, moved[o - b:o - b + n_rows, :]


def _scan_rows(av, bv, edge, out_ref, reverse=False):
    tm, W = av.shape
    sub = lax.broadcasted_iota(jnp.int32, (tm, W), 0) % SUBLANES
    s = 1
    while s < SUBLANES:
        keep = (sub < SUBLANES - s) if reverse else (sub >= s)
        shift = tm - s if reverse else s
        bv = jnp.where(keep, av * pltpu.roll(bv, shift, 0) + bv, bv)
        av = jnp.where(keep, av * pltpu.roll(av, shift, 0), av)
        s *= 2
    starts = range(0, tm, SUBLANES)
    for r0 in (reversed(starts) if reverse else starts):
        group = av[r0:r0 + SUBLANES, :] * edge + bv[r0:r0 + SUBLANES, :]
        out_ref[r0:r0 + SUBLANES, :] = group
        edge = group[0:1, :] if reverse else group[SUBLANES - 1:SUBLANES, :]
    return edge


class _Comm:
    def __init__(self, arrays, in_specs, out_shapes, out_specs, scratch, start, wait, aliases=None):
        self.arrays, self.in_specs = list(arrays), list(in_specs)
        self.out_shapes, self.out_specs = list(out_shapes), list(out_specs)
        self.scratch, self.start, self.wait = list(scratch), start, wait
        self.aliases = dict(aliases or {})


def _in_hbm(a):
    return pltpu.with_memory_space_constraint(a, pltpu.HBM)


def _operands(comm):
    return [a if spec is VMEM_SPEC else _in_hbm(a) for a, spec in zip(comm.arrays, comm.in_specs)]


def _call(body, *, name, grid, in_specs, out_specs, out_shape, scratch_shapes, vmem_mib, args, rider=None,
          num_scalar_prefetch=0):
    params = pltpu.CompilerParams(dimension_semantics=("arbitrary",) * len(grid), vmem_limit_bytes=vmem_mib * MIB)
    args = [a if k < num_scalar_prefetch else _in_hbm(a) for k, a in enumerate(args)]
    if rider is None:
        return pl.pallas_call(
            body, name=name,
            grid_spec=pltpu.PrefetchScalarGridSpec(
                num_scalar_prefetch=num_scalar_prefetch, grid=grid, in_specs=in_specs, out_specs=out_specs,
                scratch_shapes=scratch_shapes),
            out_shape=out_shape, compiler_params=params)(*args)
    assert num_scalar_prefetch == 0
    n_in, n_out, n_scr = len(in_specs), len(out_specs), len(scratch_shapes)
    r_in, r_out = len(rider.arrays), len(rider.out_shapes)
    n_axes = len(grid)

    def carried(*refs):
        pos = [0]

        def take(n):
            pos[0] += n
            return refs[pos[0] - n:pos[0]]

        ins, r_ins, outs, r_outs, scr, r_scr = take(n_in), take(r_in), take(n_out), take(r_out), take(n_scr), take(len(rider.scratch))
        first = pl.program_id(0) == 0
        last = pl.program_id(0) == grid[0] - 1
        for ax in range(1, n_axes):
            first = first & (pl.program_id(ax) == 0)
            last = last & (pl.program_id(ax) == grid[ax] - 1)

        @pl.when(first)
        def _():
            rider.start(r_ins, r_outs, r_scr)

        body(*ins, *outs, *scr)

        @pl.when(last)
        def _():
            rider.wait(r_ins, r_outs, r_scr)

    res = pl.pallas_call(
        carried, name=name,
        grid=grid,
        in_specs=list(in_specs) + rider.in_specs,
        out_specs=list(out_specs) + rider.out_specs,
        out_shape=list(out_shape) + rider.out_shapes,
        scratch_shapes=list(scratch_shapes) + rider.scratch,
        input_output_aliases={n_in + i: n_out + o for i, o in rider.aliases.items()},
        compiler_params=params)(*args, *_operands(rider))
    return res[:n_out], res[n_out:]


def _run_comm(comm, name):
    n_in, n_out = len(comm.arrays), len(comm.out_shapes)

    def body(*refs):
        ins, outs, scr = refs[:n_in], refs[n_in:n_in + n_out], refs[n_in + n_out:]
        comm.start(ins, outs, scr)
        comm.wait(ins, outs, scr)

    return pl.pallas_call(
        body, name=name,
        in_specs=comm.in_specs, out_specs=comm.out_specs, out_shape=comm.out_shapes,
        scratch_shapes=comm.scratch, input_output_aliases=comm.aliases,
        compiler_params=pltpu.CompilerParams(vmem_limit_bytes=24 * MIB))(*_operands(comm))


def _both(a, b):
    ni, no, ns = len(a.arrays), len(a.out_shapes), len(a.scratch)

    def start(ins, outs, scr):
        a.start(ins[:ni], outs[:no], scr[:ns])
        b.start(ins[ni:], outs[no:], scr[ns:])

    def wait(ins, outs, scr):
        a.wait(ins[:ni], outs[:no], scr[:ns])
        b.wait(ins[ni:], outs[no:], scr[ns:])

    aliases = dict(a.aliases)
    aliases.update({ni + i: no + o for i, o in b.aliases.items()})
    return _Comm(a.arrays + b.arrays, a.in_specs + b.in_specs, a.out_shapes + b.out_shapes,
                 a.out_specs + b.out_specs, a.scratch + b.scratch, start, wait, aliases)


def _place():
    return lax.axis_index("x"), lax.axis_index("y"), lax.axis_index("c")


def _peer(k):
    x, y, c = _place()
    px, py, pc = x ^ ((k >> 2) & 1), y ^ ((k >> 1) & 1), c ^ (k & 1)
    return (px, py, pc), 4 * px + 2 * py + pc


SIBLING = 1
SAME_CORE = (2, 4, 6)
EVERYONE = tuple(range(1, N_DEV))


def _gather_comm(shards, views, out_shapes, relations):
    na = len(shards)

    def copies(ins, outs, scr):
        send_sems, recv_sems, _ = scr
        _, me = _peer(0)
        out = []
        for a in range(na):
            for k in relations[a]:
                peer, theirs = _peer(k)
                send = functools.partial(
                    pltpu.make_async_remote_copy,
                    src_ref=ins[a], dst_ref=views[a](outs[a], me),
                    send_sem=send_sems.at[7 * a + k - 1], recv_sem=recv_sems.at[7 * a + k - 1],
                    device_id=peer, device_id_type=MESH)
                recv = functools.partial(
                    pltpu.make_async_remote_copy,
                    src_ref=ins[a], dst_ref=views[a](outs[a], theirs),
                    send_sem=send_sems.at[7 * a + k - 1], recv_sem=recv_sems.at[7 * a + k - 1],
                    device_id=peer, device_id_type=MESH)
                out.append((send, recv))
        return out

    def local(ins, outs, scr):
        _, me = _peer(0)
        return [pltpu.make_async_copy(ins[a], views[a](outs[a], me), scr[2].at[a]) for a in range(na)]

    def start(ins, outs, scr):
        for cp in local(ins, outs, scr):
            cp.start()
        for send, _ in copies(ins, outs, scr):
            send().start()

    def wait(ins, outs, scr):
        for _, recv in copies(ins, outs, scr):
            recv().wait_recv()
        for send, _ in copies(ins, outs, scr):
            send().wait_send()
        for cp in local(ins, outs, scr):
            cp.wait()

    return _Comm(shards, [HBM_SPEC] * na, out_shapes, [HBM_SPEC] * na,
                 [pltpu.SemaphoreType.DMA((7 * na,)), pltpu.SemaphoreType.DMA((7 * na,)),
                  pltpu.SemaphoreType.DMA((na,))], start, wait)


def _forward_comm(gathered, views):
    na = len(gathered)
    shapes = [jax.ShapeDtypeStruct(g.shape, g.dtype) for g in gathered]

    def copies(outs, scr):
        send_sems, recv_sems = scr
        sibling, _ = _peer(SIBLING)
        out = []
        for a in range(na):
            for n, k in enumerate(SAME_CORE):
                _, mine = _peer(k)
                _, theirs = _peer(k ^ SIBLING)
                send = functools.partial(
                    pltpu.make_async_remote_copy,
                    src_ref=views[a](outs[a], mine), dst_ref=views[a](outs[a], mine),
                    send_sem=send_sems.at[3 * a + n], recv_sem=recv_sems.at[3 * a + n],
                    device_id=sibling, device_id_type=MESH)
                recv = functools.partial(
                    pltpu.make_async_remote_copy,
                    src_ref=views[a](outs[a], mine), dst_ref=views[a](outs[a], theirs),
                    send_sem=send_sems.at[3 * a + n], recv_sem=recv_sems.at[3 * a + n],
                    device_id=sibling, device_id_type=MESH)
                out.append((send, recv))
        return out

    def start(ins, outs, scr):
        for send, _ in copies(outs, scr):
            send().start()

    def wait(ins, outs, scr):
        for _, recv in copies(outs, scr):
            recv().wait_recv()
        for send, _ in copies(outs, scr):
            send().wait_send()

    return _Comm(gathered, [HBM_SPEC] * na, shapes, [HBM_SPEC] * na,
                 [pltpu.SemaphoreType.DMA((3 * na,)), pltpu.SemaphoreType.DMA((3 * na,))], start, wait,
                 aliases={a: a for a in range(na)})


Y_NEIGHBOUR, X_NEIGHBOUR, DIAGONAL = SAME_CORE


def _all_gather(shards, views, out_shapes, name):
    na = len(shards)
    near = (SIBLING, Y_NEIGHBOUR, X_NEIGHBOUR)
    level1 = _gather_comm(shards, views, out_shapes, [near] * na)

    def body(*refs):
        ins, outs = refs[:na], refs[na:2 * na]
        send_sems, recv_sems, local_sems, fwd_send, fwd_recv, relay_send, relay_recv = refs[2 * na:]
        sibling, _ = _peer(SIBLING)
        c = lax.axis_index("c")
        level1.start(ins, outs, (send_sems, recv_sems, local_sems))

        def block_copy(a, block, to, send_sem, recv_sem):
            return pltpu.make_async_remote_copy(
                src_ref=views[a](outs[a], block), dst_ref=views[a](outs[a], block),
                send_sem=send_sem, recv_sem=recv_sem, device_id=to, device_id_type=MESH)

        def to_sibling(a, n, k):
            _, mine = _peer(k)
            _, theirs = _peer(k ^ SIBLING)
            fwd = block_copy(a, mine, sibling, fwd_send.at[3 * a + n], fwd_recv.at[3 * a + n])
            fwd.start()
            return fwd, block_copy(a, theirs, sibling, fwd_send.at[3 * a + n], fwd_recv.at[3 * a + n])

        passed, landing = [], []
        for a in range(na):
            for n, k in enumerate((Y_NEIGHBOUR, X_NEIGHBOUR)):
                peer, origin = _peer(k)
                pltpu.make_async_remote_copy(
                    src_ref=ins[a], dst_ref=views[a](outs[a], origin),
                    send_sem=send_sems.at[7 * a + k - 1], recv_sem=recv_sems.at[7 * a + k - 1],
                    device_id=peer, device_id_type=MESH).wait_recv()

                @pl.when(c == (0 if k == X_NEIGHBOUR else 1))
                def _():
                    other, _ = _peer(DIAGONAL ^ k)
                    block_copy(a, origin, other, relay_send.at[a], relay_recv.at[a]).start()

                fwd, lands = to_sibling(a, n, k)
                passed.append(fwd)
                landing.append(lands)
            _, far = _peer(DIAGONAL)
            block_copy(a, far, sibling, relay_send.at[a], relay_recv.at[a]).wait_recv()
            fwd, lands = to_sibling(a, 2, DIAGONAL)
            passed.append(fwd)
            landing.append(lands)
        for a in range(na):
            _, theirs = _peer(SIBLING)
            pltpu.make_async_remote_copy(
                src_ref=ins[a], dst_ref=views[a](outs[a], theirs),
                send_sem=send_sems.at[7 * a + SIBLING - 1], recv_sem=recv_sems.at[7 * a + SIBLING - 1],
                device_id=sibling, device_id_type=MESH).wait_recv()
        for cp in landing:
            cp.wait_recv()
        for cp in passed:
            cp.wait_send()
        _, me = _peer(0)
        for a in range(na):
            block_copy(a, me, sibling, relay_send.at[a], relay_recv.at[a]).wait_send()
            for k in near:
                peer, _ = _peer(k)
                pltpu.make_async_remote_copy(
                    src_ref=ins[a], dst_ref=views[a](outs[a], me),
                    send_sem=send_sems.at[7 * a + k - 1], recv_sem=recv_sems.at[7 * a + k - 1],
                    device_id=peer, device_id_type=MESH).wait_send()
            pltpu.make_async_copy(ins[a], views[a](outs[a], me), local_sems.at[a]).wait()

    return pl.pallas_call(
        body, name=name,
        in_specs=[HBM_SPEC] * na, out_specs=[HBM_SPEC] * na, out_shape=out_shapes,
        scratch_shapes=level1.scratch + [pltpu.SemaphoreType.DMA((3 * na,)), pltpu.SemaphoreType.DMA((3 * na,)),
                                         pltpu.SemaphoreType.DMA((na,)), pltpu.SemaphoreType.DMA((na,))],
    )(*[_in_hbm(s) for s in shards])


def _sibling_comm(grads):
    na = len(grads)
    shapes = [jax.ShapeDtypeStruct(g.shape[:2] + g.shape[3:], g.dtype) for g in grads]

    def copies(ins, outs, scr):
        x, y, c = _place()
        return [pltpu.make_async_remote_copy(
            src_ref=ins[a].at[:, :, 1 - c], dst_ref=outs[a],
            send_sem=scr[0].at[a], recv_sem=scr[1].at[a],
            device_id=(x, y, 1 - c), device_id_type=MESH) for a in range(na)]

    def start(ins, outs, scr):
        for cp in copies(ins, outs, scr):
            cp.start()

    def wait(ins, outs, scr):
        for cp in copies(ins, outs, scr):
            cp.wait()

    return _Comm(grads, [HBM_SPEC] * na, shapes, [HBM_SPEC] * na,
                 [pltpu.SemaphoreType.DMA((na,)), pltpu.SemaphoreType.DMA((na,))], start, wait)


def _chips_comm(sums, every_device=False):
    na = len(sums)
    shapes = [jax.ShapeDtypeStruct(s.shape, s.dtype) for s in sums]
    relations = EVERYONE if every_device else SAME_CORE
    nr = len(relations)

    def block(px, py, pc):
        return 4 * px + 2 * py + pc if every_device else 2 * px + py

    def copies(ins, outs, scr):
        mine = block(*_place())
        out = []
        for a in range(na):
            for n, k in enumerate(relations):
                peer, _ = _peer(k)
                theirs = block(*peer)
                send = functools.partial(
                    pltpu.make_async_remote_copy,
                    src_ref=ins[a].at[:, theirs], dst_ref=outs[a].at[:, mine],
                    send_sem=scr[0].at[nr * a + n], recv_sem=scr[1].at[nr * a + n],
                    device_id=peer, device_id_type=MESH)
                recv = functools.partial(
                    pltpu.make_async_remote_copy,
                    src_ref=ins[a].at[:, mine], dst_ref=outs[a].at[:, theirs],
                    send_sem=scr[0].at[nr * a + n], recv_sem=scr[1].at[nr * a + n],
                    device_id=peer, device_id_type=MESH)
                out.append((send, recv))
        return out

    def local(ins, outs, scr):
        mine = block(*_place())
        return [pltpu.make_async_copy(ins[a].at[:, mine], outs[a].at[:, mine], scr[2].at[a]) for a in range(na)]

    def start(ins, outs, scr):
        for cp in local(ins, outs, scr):
            cp.start()
        for send, _ in copies(ins, outs, scr):
            send().start()

    def wait(ins, outs, scr):
        for _, recv in copies(ins, outs, scr):
            recv().wait_recv()
        for send, _ in copies(ins, outs, scr):
            send().wait_send()
        for cp in local(ins, outs, scr):
            cp.wait()

    return _Comm(sums, [HBM_SPEC] * na, shapes, [HBM_SPEC] * na,
                 [pltpu.SemaphoreType.DMA((nr * na,)), pltpu.SemaphoreType.DMA((nr * na,)),
                  pltpu.SemaphoreType.DMA((na,))], start, wait)


def _small_sum_comm(arrays):
    na = len(arrays)

    def copies(ins, scr):
        bufs, send_sems, recv_sems = scr[:na], scr[na], scr[na + 1]
        _, me = _peer(0)
        out = []
        for a in range(na):
            for k in EVERYONE:
                peer, theirs = _peer(k)
                sems = dict(send_sem=send_sems.at[7 * a + k - 1], recv_sem=recv_sems.at[7 * a + k - 1])
                send = functools.partial(
                    pltpu.make_async_remote_copy,
                    src_ref=ins[a], dst_ref=bufs[a].at[me], device_id=peer, device_id_type=MESH, **sems)
                recv = functools.partial(
                    pltpu.make_async_remote_copy,
                    src_ref=ins[a], dst_ref=bufs[a].at[theirs], device_id=peer, device_id_type=MESH, **sems)
                out.append((send, recv))
        return out

    def start(ins, outs, scr):
        _, me = _peer(0)
        for a in range(na):
            scr[a][me] = ins[a][...]
        for send, _ in copies(ins, scr):
            send().start()

    def wait(ins, outs, scr):
        for _, recv in copies(ins, scr):
            recv().wait_recv()
        for send, _ in copies(ins, scr):
            send().wait_send()
        for a in range(na):
            acc = scr[a][0]
            for j in range(1, N_DEV):
                acc = acc + scr[a][j]
            outs[a][...] = acc

    return _Comm(arrays, [VMEM_SPEC] * na, [jax.ShapeDtypeStruct(s.shape, F32) for s in arrays], [VMEM_SPEC] * na,
                 [pltpu.VMEM((N_DEV,) + s.shape, F32) for s in arrays]
                 + [pltpu.SemaphoreType.DMA((7 * na,)), pltpu.SemaphoreType.DMA((7 * na,))], start, wait)


def _prep_weights(ffn1, ffn2, w_in, w_out, name):
    rf, D = ffn1[2].shape
    ri, ro = w_in.shape[1], w_out.shape[0]

    def body(g1, u1, d1, g2, u2, d2, wi, wo, p1_ref, p2_ref, pi_ref, po_ref):
        for p_ref, shards in ((p1_ref, (g1, u1, d1)), (p2_ref, (g2, u2, d2))):
            for k, shard in enumerate(shards):
                p_ref[k] = shard[...].astype(BF16)
        pi_ref[...] = wi[...].T.astype(BF16)
        po_ref[...] = wo[...].astype(BF16)

    args = (*ffn1, *ffn2, w_in, w_out)
    whole = lambda shape: pl.BlockSpec(shape, lambda i: (0,) * len(shape))
    out_shapes = [(3, rf, D), (3, rf, D), (ri, D), (ro, D)]
    return _call(
        body, name=name, grid=(1,),
        in_specs=[whole(a.shape) for a in args], out_specs=[whole(s) for s in out_shapes],
        out_shape=[jax.ShapeDtypeStruct(s, BF16) for s in out_shapes],
        scratch_shapes=[], vmem_mib=48, args=args)


def _load_weights(w_hbm, w_vmem, sem):
    @pl.when(pl.program_id(0) == 0)
    def _():
        copies = [pltpu.make_async_copy(w_hbm.at[k], w_vmem.at[k], sem.at[k]) for k in range(3)]
        for cp in copies:
            cp.start()
        for cp in copies:
            cp.wait()


def _ffn_fwd(x, g, w3, tm, cf, name, rider=None, head=None):
    T, D = x.shape
    F = w3.shape[1]
    n_head = 0 if head is None else 2

    def body(x_ref, g_ref, w_hbm, *refs):
        head_refs, refs = refs[:n_head], refs[n_head:]
        if head is None:
            (xo_ref, h_ref, dau_ref, dag_ref, act_ref, wv, sem) = refs
        else:
            (dx_ref, dob_ref, dgf_ref, loss_ref, h_ref, dau_ref, dag_ref, act_ref, wv, sem) = refs
        _load_weights(w_hbm, wv, sem)
        xhat, _ = _rms_stats(x_ref[...])
        hb = (xhat * g_ref[...]).astype(BF16)
        h_ref[...] = hb
        for lo in range(0, F, cf):
            gate = _dot(hb, wv[0, lo:lo + cf, :], NT)
            up = _dot(hb, wv[1, lo:lo + cf, :], NT)
            sig = _sigmoid(gate)
            silu = gate * sig
            dau_ref[:, lo:lo + cf] = silu.astype(BF16)
            dag_ref[:, lo:lo + cf] = (up * (sig * (1.0 + gate * (1.0 - sig)))).astype(BF16)
            act_ref[:, lo:lo + cf] = (silu * up).astype(BF16)
        x_out = x_ref[...] + FFN_RES * _dot(act_ref[...], wv[2], NN)
        if head is None:
            xo_ref[...] = x_out
            return

        @pl.when(pl.program_id(0) == 0)
        def _():
            dgf_ref[...] = jnp.zeros_like(dgf_ref)
            loss_ref[...] = jnp.zeros_like(loss_ref)

        gf_ref, tgt_ref = head_refs
        yhat, rstd = _rms_stats(x_out)
        gf = gf_ref[...]
        err = yhat * gf - tgt_ref[...]
        loss_ref[...] += (0.5 / D) * jnp.sum(err * err)
        dx, dgf = _rms_bwd(yhat, rstd, gf, err * (1.0 / D))
        dx_ref[...] = dx
        dob_ref[...] = (FFN_RES * dx).astype(BF16)
        dgf_ref[...] += dgf

    row = pl.BlockSpec((tm, D), lambda i: (i, 0))
    hid = pl.BlockSpec((tm, F), lambda i: (i, 0))
    vec = pl.BlockSpec((1, D), lambda i: (0, 0))
    row_f32, row_bf16 = jax.ShapeDtypeStruct((T, D), F32), jax.ShapeDtypeStruct((T, D), BF16)
    if head is None:
        first_specs, first_shapes = [row], [row_f32]
    else:
        first_specs = [row, row, vec, pl.BlockSpec((1, 128), lambda i: (0, 0))]
        first_shapes = [row_f32, row_bf16, jax.ShapeDtypeStruct((1, D), F32), jax.ShapeDtypeStruct((1, 128), F32)]
    return _call(
        body, name=name, grid=(T // tm,),
        in_specs=[row, vec, HBM_SPEC] + ([] if head is None else [vec, row]),
        out_specs=first_specs + [row, hid, hid, hid],
        out_shape=first_shapes + [row_bf16] + [jax.ShapeDtypeStruct((T, F), BF16)] * 3,
        scratch_shapes=[pltpu.VMEM((3, F, D), BF16), pltpu.SemaphoreType.DMA((3,))],
        vmem_mib=60, args=(x, g, w3) + (() if head is None else tuple(head)), rider=rider)


def _ffn_dgrad(dout, x, g, dau, dag, w3, tm, cf, name, rider=None):
    T, D = x.shape
    F = w3.shape[1]

    def body(do_ref, x_ref, g_ref, dau_ref, dag_ref, w_hbm, dx_ref, dgate_ref, dup_ref, dg_ref, wv, sem):
        _load_weights(w_hbm, wv, sem)

        @pl.when(pl.program_id(0) == 0)
        def _():
            dg_ref[...] = jnp.zeros_like(dg_ref)

        dob = (FFN_RES * do_ref[...]).astype(BF16)
        for lo in range(0, F, cf):
            dact = _dot(dob, wv[2, lo:lo + cf, :], NT)
            dup_ref[:, lo:lo + cf] = (dact * dau_ref[:, lo:lo + cf].astype(F32)).astype(BF16)
            dgate_ref[:, lo:lo + cf] = (dact * dag_ref[:, lo:lo + cf].astype(F32)).astype(BF16)
        dh = _dot(dgate_ref[...], wv[0], NN) + _dot(dup_ref[...], wv[1], NN)
        xhat, rstd = _rms_stats(x_ref[...])
        dx, dg = _rms_bwd(xhat, rstd, g_ref[...], dh)
        dx_ref[...] = do_ref[...] + dx
        dg_ref[...] += dg

    row = pl.BlockSpec((tm, D), lambda i: (i, 0))
    hid = pl.BlockSpec((tm, F), lambda i: (i, 0))
    vec = pl.BlockSpec((1, D), lambda i: (0, 0))
    return _call(
        body, name=name, grid=(T // tm,),
        in_specs=[row, row, vec, hid, hid, HBM_SPEC],
        out_specs=[row, hid, hid, vec],
        out_shape=[jax.ShapeDtypeStruct((T, D), F32), jax.ShapeDtypeStruct((T, F), BF16),
                   jax.ShapeDtypeStruct((T, F), BF16), jax.ShapeDtypeStruct((1, D), F32)],
        scratch_shapes=[pltpu.VMEM((3, F, D), BF16), pltpu.SemaphoreType.DMA((3,))],
        vmem_mib=52, args=(dout, x, g, dau, dag, w3), rider=rider)


def _wgrad(lhs, rhs, tm, tf, name, rider=None):
    T, F = lhs[0].shape
    D = rhs.shape[1]
    K = len(lhs)

    def body(*refs):
        lhs_refs, rhs_ref, dw_ref, accs = refs[:K], refs[K], refs[K + 1], refs[K + 2:]
        i = pl.program_id(1)

        @pl.when(i == 0)
        def _():
            for acc in accs:
                acc[...] = jnp.zeros_like(acc)

        rv = rhs_ref[...]
        for acc, lhs_ref in zip(accs, lhs_refs):
            acc[...] += _dot(lhs_ref[...], rv, TN)

        @pl.when(i == pl.num_programs(1) - 1)
        def _():
            for k, acc in enumerate(accs):
                dw_ref[k] = acc[...].astype(BF16)

    hid = pl.BlockSpec((tm, tf), lambda f, i: (i, f))
    return _call(
        body, name=name, grid=(F // tf, T // tm),
        in_specs=[hid] * K + [pl.BlockSpec((tm, D), lambda f, i: (i, 0))],
        out_specs=[pl.BlockSpec((K, tf, D), lambda f, i: (0, f, 0))],
        out_shape=[jax.ShapeDtypeStruct((K, F, D), BF16)],
        scratch_shapes=[pltpu.VMEM((tf, D), F32)] * K,
        vmem_mib=56, args=(*lhs, rhs), rider=rider)


def _lru_gates(xr, bda_ref, bdx_ref, vec_ref):
    xrb = xr.astype(BF16)
    r = _sigmoid(_dot(xrb, bda_ref[...], NN) + vec_ref[V_BA:V_BA + 1, :])
    ig = _sigmoid(_dot(xrb, bdx_ref[...], NN) + vec_ref[V_BX:V_BX + 1, :])
    sp = _softplus_neg(vec_ref[V_LAM:V_LAM + 1, :])
    log_a = (-LRU_C * sp) * r
    a = jnp.exp(log_a)
    mult = jnp.sqrt(_neg_expm1(2.0 * log_a))
    return xrb, r, ig, sp, a, mult


def _layernorm_stats(u1):
    xc = u1 - jnp.mean(u1, axis=-1, keepdims=True)
    rs = lax.rsqrt(jnp.mean(xc * xc, axis=-1, keepdims=True) + LN_EPS)
    return xc * rs, rs


def _mix_core_fwd(x1, g, w_in_t, w_out, bda, bdx, cw, lw, vec, tm, name, rider=None):
    T, D = x1.shape
    W = cw.shape[1]
    assert tm >= CONV_HALO and w_in_t.shape[0] == 4 * W

    def body(x1_ref, g_ref, wi_ref, wo_ref, bda_ref, bdx_ref, cw_ref, lw_ref, vec_ref,
             x2_ref, z_ref, mix_ref, u1_ref, xr_ref, hst_ref, ubuf, rbuf, hc):
        @pl.when(pl.program_id(0) == 0)
        def _():
            ubuf[0:CONV_HALO, :] = jnp.zeros((CONV_HALO, W), F32)
            rbuf[0:LRU_HALO, :] = jnp.zeros((LRU_HALO, W), F32)
            hc[...] = jnp.zeros_like(hc)

        xhat, _ = _rms_stats(x1_ref[...])
        z_ref[...] = _dot((xhat * g_ref[...]).astype(BF16), wi_ref[...], NT)

        ubuf[CONV_HALO:CONV_HALO + tm, :] = z_ref[:, 0:W] * _sigmoid(z_ref[:, W:2 * W])
        u1 = jnp.zeros((tm, W), F32) + vec_ref[V_CB:V_CB + 1, :]
        base = CONV_HALO - (CONV_K - 1)
        for off, win in _row_windows(ubuf, tm, range(base, base + CONV_K)):
            u1 = u1 + cw_ref[off - base:off - base + 1, :] * win
        ubuf[0:CONV_HALO, :] = ubuf[tm:tm + CONV_HALO, :]
        u1_ref[...] = u1
        xh, _ = _layernorm_stats(u1)
        u2 = xh * vec_ref[V_LNG:V_LNG + 1, :] + vec_ref[V_LNB:V_LNB + 1, :]
        ub = (u2 * _sigmoid(u2)).astype(BF16)
        mix_ref[:, 0:W] = ub

        rbuf[LRU_HALO:LRU_HALO + tm, :] = z_ref[:, 2 * W:3 * W]
        xr = jnp.zeros((tm, W), F32) + vec_ref[V_LCB:V_LCB + 1, :]
        for k in range(LRU_K):
            off = LRU_HALO - (LRU_K - 1) + k
            xr = xr + lw_ref[k:k + 1, :] * rbuf[off:off + tm, :]
        rbuf[0:LRU_HALO, :] = rbuf[tm:tm + LRU_HALO, :]
        xr_ref[...] = xr
        _, _, ig, _, a, mult = _lru_gates(xr, bda_ref, bdx_ref, vec_ref)
        hc[0:1, :] = _scan_rows(a, mult * (ig * xr), hc[0:1, :], hst_ref)
        gl, _ = _gelu_parts(z_ref[:, 3 * W:4 * W])
        yb = (hst_ref[...] * gl).astype(BF16)
        mix_ref[:, W:2 * W] = yb

        x2_ref[...] = x1_ref[...] + _dot(ub, wo_ref[0:W, :], NN) + _dot(yb, wo_ref[W:2 * W, :], NN)

    full = lambda a: pl.BlockSpec(a.shape, lambda i: (0,) * a.ndim)
    tile = lambda n: pl.BlockSpec((tm, n), lambda i: (i, 0))
    return _call(
        body, name=name, grid=(T // tm,),
        in_specs=[tile(D), full(g), full(w_in_t), full(w_out), full(bda), full(bdx), full(cw), full(lw), full(vec)],
        out_specs=[tile(D), tile(4 * W), tile(2 * W), tile(W), tile(W), tile(W)],
        out_shape=[jax.ShapeDtypeStruct((T, D), F32), jax.ShapeDtypeStruct((T, 4 * W), F32),
                   jax.ShapeDtypeStruct((T, 2 * W), BF16), jax.ShapeDtypeStruct((T, W), F32),
                   jax.ShapeDtypeStruct((T, W), F32), jax.ShapeDtypeStruct((T, W), F32)],
        scratch_shapes=[pltpu.VMEM((tm + CONV_HALO, W), F32), pltpu.VMEM((tm + LRU_HALO, W), F32),
                        pltpu.VMEM((8, W), F32)],
        vmem_mib=56, args=(x1, g, w_in_t, w_out, bda, bdx, cw, lw, vec), rider=rider)


def _mix_bwd(dx2, z, u1, xr, hst, x1, mix, g, w_in_t, w_out, bda, bdx, cw, lw, vec, tm, name, rider=None):
    T, D = dx2.shape
    W = cw.shape[1]
    nt = T // tm
    assert tm >= CONV_HALO and tm % CONV_HALO == 0

    def body(dx_ref, z_ref, zh_ref, u1_ref, xr_ref, h_ref, hh_ref, wo_ref, bda_ref, bdx_ref, cw_ref, lw_ref, vec_ref,
             x1_ref, mix_ref, g_ref, wi_ref,
             dx1_ref, sg_ref, dbda_ref, dbdx_ref, dob_ref, dg_ref, dwi_ref, dwo_ref,
             u0buf, du1buf, rxbuf, dxrbuf, gbuf, gc, spacc, dz_ref, ai_ref, ao_ref):
        i = pl.program_id(0)
        first = i == nt - 1
        row = lax.broadcasted_iota(jnp.int32, (tm, W), 0)

        @pl.when(i == 0)
        def _():
            sg_ref[...] = jnp.zeros_like(sg_ref)
            dbda_ref[...] = jnp.zeros_like(dbda_ref)
            dbdx_ref[...] = jnp.zeros_like(dbdx_ref)
            du1buf[tm:tm + CONV_HALO, :] = jnp.zeros((CONV_HALO, W), F32)
            dxrbuf[tm:tm + LRU_HALO, :] = jnp.zeros((LRU_HALO, W), F32)
            gc[...] = jnp.zeros_like(gc)
            spacc[...] = jnp.zeros_like(spacc)
            dg_ref[...] = jnp.zeros_like(dg_ref)
            ai_ref[...] = jnp.zeros_like(ai_ref)
            ao_ref[...] = jnp.zeros_like(ao_ref)

        def accum(r, val):
            sg_ref[r:r + 1, :] += jnp.sum(val, axis=0, keepdims=True)

        x1hat, x1rstd = _rms_stats(x1_ref[...])
        gain = g_ref[...]
        hb = (x1hat * gain).astype(BF16)

        def in_proj_bwd(lo, hi):
            dzb = dz_ref[:, lo:hi]
            ai_ref[lo:hi, :] += _dot(dzb, hb, TN)
            return _dot(dzb, wi_ref[lo:hi, :], NN)

        dxb = dx_ref[...].astype(BF16)
        ao_ref[...] += _dot(mix_ref[...], dxb, TN)
        dmix = _dot(dxb, wo_ref[...], NT)
        d_u = dmix[:, 0:W]
        d_yr = dmix[:, W:2 * W]

        xh, rs = _layernorm_stats(u1_ref[...])
        ln_g = vec_ref[V_LNG:V_LNG + 1, :]
        u2 = xh * ln_g + vec_ref[V_LNB:V_LNB + 1, :]
        s2 = _sigmoid(u2)
        d_u2 = d_u * (s2 * (1.0 + u2 * (1.0 - s2)))
        accum(G_LNG, d_u2 * xh)
        accum(G_LNB, d_u2)
        d_xh = d_u2 * ln_g
        d_u1 = rs * (d_xh - jnp.mean(d_xh, axis=-1, keepdims=True)
                     - xh * jnp.mean(d_xh * xh, axis=-1, keepdims=True))
        accum(G_CB, d_u1)
        halo_on = jnp.where(first, 0.0, 1.0)
        u0buf[0:CONV_HALO, :] = halo_on * (zh_ref[:, 0:W] * _sigmoid(zh_ref[:, W:2 * W]))
        cv = z_ref[:, 0:W]
        sgc = _sigmoid(z_ref[:, W:2 * W])
        u0buf[CONV_HALO:CONV_HALO + tm, :] = cv * sgc
        du1buf[0:tm, :] = d_u1
        base = CONV_HALO - (CONV_K - 1)
        for off, win in _row_windows(u0buf, tm, range(base, base + CONV_K)):
            accum(G_CW + off - base, d_u1 * win)
        d_u0 = jnp.zeros((tm, W), F32)
        for off, win in _row_windows(du1buf, tm, range(0, CONV_K)):
            d_u0 = d_u0 + cw_ref[CONV_K - 1 - off:CONV_K - off, :] * win
        du1buf[tm:tm + CONV_HALO, :] = du1buf[0:CONV_HALO, :]
        dz_ref[:, 0:W] = (d_u0 * sgc).astype(BF16)
        dz_ref[:, W:2 * W] = (d_u0 * cv * (sgc * (1.0 - sgc))).astype(BF16)
        dh = in_proj_bwd(0, 2 * W)

        xrv = xr_ref[...]
        xrb, r, ig, sp, a, mult = _lru_gates(xrv, bda_ref, bdx_ref, vec_ref)
        h = h_ref[...]
        gl, dgl = _gelu_parts(z_ref[:, 3 * W:4 * W])
        dz_ref[:, 3 * W:4 * W] = (d_yr * h * dgl).astype(BF16)
        dh = dh + in_proj_bwd(3 * W, 4 * W)
        a_next = jnp.where(row == tm - 1, 1.0, pltpu.roll(a, tm - 1, 0))
        g_first = _scan_rows(a_next, d_yr * gl, gc[0:1, :], gbuf, reverse=True)
        g = gbuf[...]
        gc[0:1, :] = a[0:1, :] * g_first
        hprev = jnp.where(row == 0, halo_on * hh_ref[LRU_HALO - 1:LRU_HALO, :], pltpu.roll(h, 1, 0))
        d_log_a = (g * hprev) * a - (g * ig * xrv) * (a * a) / mult
        d_ig = g * mult * xrv
        d_xr = g * mult * ig
        spacc[0:1, :] += jnp.sum(d_log_a * r, axis=0, keepdims=True)
        d_pa32 = (d_log_a * (-LRU_C * sp)) * (r * (1.0 - r))
        d_px32 = d_ig * (ig * (1.0 - ig))
        accum(G_BA, d_pa32)
        accum(G_BX, d_px32)
        d_pa = d_pa32.astype(BF16)
        d_px = d_px32.astype(BF16)
        d_xr = d_xr + _dot(d_pa, bda_ref[...], NT) + _dot(d_px, bdx_ref[...], NT)
        dbda_ref[...] += _dot(xrb, d_pa, TN)
        dbdx_ref[...] += _dot(xrb, d_px, TN)
        accum(G_LCB, d_xr)
        rxbuf[0:LRU_HALO, :] = halo_on * zh_ref[CONV_HALO - LRU_HALO:CONV_HALO, 2 * W:3 * W]
        rxbuf[LRU_HALO:LRU_HALO + tm, :] = z_ref[:, 2 * W:3 * W]
        dxrbuf[0:tm, :] = d_xr
        d_rx = jnp.zeros((tm, W), F32)
        for k in range(LRU_K):
            off = LRU_HALO - (LRU_K - 1) + k
            accum(G_LW + k, d_xr * rxbuf[off:off + tm, :])
            d_rx = d_rx + lw_ref[k:k + 1, :] * dxrbuf[LRU_K - 1 - k:LRU_K - 1 - k + tm, :]
        dxrbuf[tm:tm + LRU_HALO, :] = dxrbuf[0:LRU_HALO, :]
        dz_ref[:, 2 * W:3 * W] = d_rx.astype(BF16)
        dh = dh + in_proj_bwd(2 * W, 3 * W)

        dx, dg = _rms_bwd(x1hat, x1rstd, gain, dh)
        dx1 = dx_ref[...] + dx
        dx1_ref[...] = dx1
        dob_ref[...] = (FFN_RES * dx1).astype(BF16)
        dg_ref[...] += dg

        @pl.when(first)
        def _():
            lam = vec_ref[V_LAM:V_LAM + 1, :]
            sg_ref[G_LAM:G_LAM + 1, :] = LRU_C * _sigmoid(-lam) * spacc[0:1, :]
            dwi_ref[...] = ai_ref[...].astype(BF16)
            dwo_ref[...] = ao_ref[...].astype(BF16)

    full = lambda a: pl.BlockSpec(a.shape, lambda i: (0,) * a.ndim)
    tile = lambda n: pl.BlockSpec((tm, n), lambda i: (nt - 1 - i, 0))
    halo = lambda rows, n: pl.BlockSpec(
        (rows, n), lambda i: (jnp.maximum((nt - 1 - i) * (tm // rows) - 1, 0), 0))
    const = lambda r, c: pl.BlockSpec((r, c), lambda i: (0, 0))
    return _call(
        body, name=name, grid=(nt,),
        in_specs=[tile(D), tile(4 * W), halo(CONV_HALO, 4 * W), tile(W), tile(W), tile(W), halo(LRU_HALO, W),
                  full(w_out), full(bda), full(bdx), full(cw), full(lw), full(vec),
                  tile(D), tile(2 * W), full(g), full(w_in_t)],
        out_specs=[tile(D), const(G_ROWS, W), const(W, W), const(W, W),
                   tile(D), const(1, D), const(4 * W, D), const(2 * W, D)],
        out_shape=[jax.ShapeDtypeStruct((T, D), F32), jax.ShapeDtypeStruct((G_ROWS, W), F32),
                   jax.ShapeDtypeStruct((W, W), F32), jax.ShapeDtypeStruct((W, W), F32),
                   jax.ShapeDtypeStruct((T, D), BF16), jax.ShapeDtypeStruct((1, D), F32),
                   jax.ShapeDtypeStruct((4 * W, D), BF16), jax.ShapeDtypeStruct((2 * W, D), BF16)],
        scratch_shapes=[pltpu.VMEM((tm + CONV_HALO, W), F32), pltpu.VMEM((tm + CONV_HALO, W), F32),
                        pltpu.VMEM((tm + LRU_HALO, W), F32), pltpu.VMEM((tm + LRU_HALO, W), F32),
                        pltpu.VMEM((tm, W), F32), pltpu.VMEM((8, W), F32), pltpu.VMEM((8, W), F32),
                        pltpu.VMEM((tm, 4 * W), BF16), pltpu.VMEM((4 * W, D), F32), pltpu.VMEM((2 * W, D), F32)],
        vmem_mib=60, args=(dx2, z, z, u1, xr, hst, hst, w_out, bda, bdx, cw, lw, vec, x1, mix, g, w_in_t),
        rider=rider)


def _pair_add(full, recv, name):
    K, _, _, rows, D = full.shape

    def body(c_ref, a_ref, b_ref, o_ref):
        o_ref[...] = (a_ref[...].astype(F32) + b_ref[...].astype(F32)).astype(BF16)

    c = lax.axis_index("c").astype(jnp.int32).reshape((1,))
    return _call(
        body, name=name, grid=(K, N_CHIP), num_scalar_prefetch=1,
        in_specs=[pl.BlockSpec((None, None, None, rows, D), lambda k, q, c_ref: (k, q, c_ref[0], 0, 0)),
                  pl.BlockSpec((None, None, rows, D), lambda k, q, c_ref: (k, q, 0, 0))],
        out_specs=pl.BlockSpec((None, None, rows, D), lambda k, q, c_ref: (k, q, 0, 0)),
        out_shape=jax.ShapeDtypeStruct(recv.shape, BF16),
        scratch_shapes=[], vmem_mib=16, args=(c, full, recv))


def _adamw_update(wv, gv, mv, vv):
    m2 = ADAM_B1 * mv + (1.0 - ADAM_B1) * gv
    v2 = ADAM_B2 * vv + (1.0 - ADAM_B2) * (gv * gv)
    m_hat = m2 / (1.0 - ADAM_B1 ** ADAM_STEP)
    v_hat = v2 / (1.0 - ADAM_B2 ** ADAM_STEP)
    return -ADAM_LR * (m_hat / (jnp.sqrt(v_hat) + ADAM_EPS) + ADAM_WD * wv), m2, v2


def _finish(parts, k, w, m, v, transpose, name):
    _, n_parts, rows, D = parts.shape

    def body(p_ref, w_ref, m_ref, v_ref, g_ref, d_ref, mo_ref, vo_ref):
        acc = p_ref[0].astype(F32)
        for q in range(1, n_parts):
            acc = acc + p_ref[q].astype(F32)
        gv = acc.T if transpose else acc
        g_ref[...] = gv
        d_ref[...], mo_ref[...], vo_ref[...] = _adamw_update(w_ref[...], gv, m_ref[...], v_ref[...])

    whole = pl.BlockSpec(w.shape, lambda i: (0, 0))
    return _call(
        body, name=name, grid=(1,),
        in_specs=[pl.BlockSpec((None, n_parts, rows, D), lambda i: (k, 0, 0, 0)), whole, whole, whole],
        out_specs=[whole] * 4, out_shape=[pltpu.HBM(w.shape, F32)] * 4,
        scratch_shapes=[], vmem_mib=40, args=(parts, w, m, v))


def _adamw_each(ws, gs, ms, vs, name):
    n = len(ws)

    def body(*refs):
        w_refs, g_refs, m_refs, v_refs, outs = refs[:n], refs[n:2 * n], refs[2 * n:3 * n], refs[3 * n:4 * n], refs[4 * n:]
        for k in range(n):
            outs[k][...], outs[n + k][...], outs[2 * n + k][...] = _adamw_update(
                w_refs[k][...], g_refs[k][...], m_refs[k][...], v_refs[k][...])

    shapes = [jax.ShapeDtypeStruct(w.shape, F32) for w in ws]
    return pl.pallas_call(
        body, name=name,
        in_specs=[VMEM_SPEC] * (4 * n), out_specs=[VMEM_SPEC] * (3 * n), out_shape=shapes * 3,
        compiler_params=pltpu.CompilerParams(vmem_limit_bytes=32 * MIB),
    )(*ws, *gs, *ms, *vs)


def _block_diag(w):
    h, d, _ = w.shape
    onto = jnp.eye(h, dtype=w.dtype)
    return (w[:, :, None, :] * onto[:, None, :, None]).reshape(h * d, h * d)


def _diag_blocks(m, h):
    d = m.shape[0] // h
    onto = jnp.eye(h, dtype=m.dtype)
    return (m.reshape(h, d, h, d) * onto[:, None, :, None]).sum(axis=2)


def _reduce_level1(full, tag):
    got = _run_comm(_sibling_comm(full), "rs_sibling_" + tag)
    return [_pair_add(a, b, "rs_pair_add_%s%d" % (tag, n)) for n, (a, b) in enumerate(zip(full, got))]


def kernel(x, ffn1_norm, ffn1_w_gate, ffn1_w_up, ffn1_w_down, mix_norm, w_in, conv_dw, conv_dw_bias, conv_ln_g, conv_ln_b, lru_conv_w, lru_conv_b, lru_w_a, lru_b_a, lru_w_x, lru_b_x, lru_lambda, w_out, ffn2_norm, ffn2_w_gate, ffn2_w_up, ffn2_w_down, final_norm, loss_target, m_ffn1_norm, m_ffn1_w_gate, m_ffn1_w_up, m_ffn1_w_down, m_mix_norm, m_w_in, m_conv_dw, m_conv_dw_bias, m_conv_ln_g, m_conv_ln_b, m_lru_conv_w, m_lru_conv_b, m_lru_w_a, m_lru_b_a, m_lru_w_x, m_lru_b_x, m_lru_lambda, m_w_out, m_ffn2_norm, m_ffn2_w_gate, m_ffn2_w_up, m_ffn2_w_down, m_final_norm, v_ffn1_norm, v_ffn1_w_gate, v_ffn1_w_up, v_ffn1_w_down, v_mix_norm, v_w_in, v_conv_dw, v_conv_dw_bias, v_conv_ln_g, v_conv_ln_b, v_lru_conv_w, v_lru_conv_b, v_lru_w_a, v_lru_b_a, v_lru_w_x, v_lru_b_x, v_lru_lambda, v_w_out, v_ffn2_norm, v_ffn2_w_gate, v_ffn2_w_up, v_ffn2_w_down, v_final_norm):
    T, D = x.shape[1], x.shape[2]
    F = ffn1_w_down.shape[0] * N_DEV
    rf = ffn1_w_down.shape[0]
    ri = w_in.shape[1]
    ro = w_out.shape[0]
    W = conv_dw_bias.shape[0]
    wc = conv_dw.shape[1]
    H = lru_w_a.shape[0]
    xs = x.reshape(T, D)
    tgt = loss_target.reshape(T, D)
    tm_ffn = min(256, T)
    tm_fwd = min(512, T)
    cf = 256
    tm_w = min(1024, T)
    tm_w1 = min(2048, T)
    tm_mix = min(256, T)
    tf_w = F // 2
    row = lambda v: v.reshape(1, -1)
    by_owner = lambda a, rows: a.reshape(a.shape[0], N_CHIP, 2, rows, D)

    p3a, p3b, p_in, p_out = _prep_weights(
        (ffn1_w_gate.T, ffn1_w_up.T, ffn1_w_down), (ffn2_w_gate.T, ffn2_w_up.T, ffn2_w_down), w_in, w_out,
        "prep_weights")
    tile_rows = lambda a: jnp.pad(a, ((0, -a.shape[0] % SUBLANES), (0, 0)))
    p_cw = jnp.concatenate([tile_rows(conv_dw), tile_rows(lru_conv_w)], axis=0)
    lw_row = p_cw.shape[0] - SUBLANES
    stacked = lambda r, j: r.at[:, j]
    plain = lambda r, j: r.at[j]
    g3_shape = jax.ShapeDtypeStruct((3, N_DEV, rf, D), BF16)
    (g3a,) = _all_gather([p3a], [stacked], [g3_shape], "ag_ffn1")
    w3a = g3a.reshape(3, F, D)
    bda = _block_diag(lru_w_a).astype(BF16)
    bdx = _block_diag(lru_w_x).astype(BF16)
    vec = jnp.concatenate([tile_rows(v[None]) for v in
                           (conv_dw_bias, conv_ln_g, conv_ln_b, lru_conv_b, lru_b_a, lru_b_x, lru_lambda)], axis=0)

    gather_rest = _gather_comm(
        [p3b, p_in, p_out, p_cw], [stacked, plain, plain, plain],
        [g3_shape, jax.ShapeDtypeStruct((N_DEV, ri, D), BF16), jax.ShapeDtypeStruct((N_DEV, ro, D), BF16),
         jax.ShapeDtypeStruct((N_DEV,) + p_cw.shape, F32)],
        [(SIBLING,) + SAME_CORE, EVERYONE, EVERYONE, EVERYONE])
    (x1, h1, dau1, dag1, act1), (g3b_half, g_in, g_out, g_cw) = _ffn_fwd(
        xs, row(ffn1_norm), w3a, tm_fwd, cf, "ffn1_fwd", rider=gather_rest)
    w_in_t = g_in.reshape(N_DEV * ri, D)
    w_out_f = g_out.reshape(N_DEV * ro, D)
    cw_all = jnp.transpose(g_cw, (1, 0, 2)).reshape(p_cw.shape[0], N_DEV * wc)
    cw = cw_all[0:CONV_K]
    lw = cw_all[lw_row:lw_row + LRU_K]
    (x2, z, mix, u1, xr, hst), (g3b,) = _mix_core_fwd(
        x1, row(mix_norm), w_in_t, w_out_f, bda, bdx, cw, lw, vec, tm_mix, "mix_core_fwd",
        rider=_forward_comm([g3b_half], [stacked]))
    w3b = g3b.reshape(3, F, D)
    dx3, dob2, d_final_norm, loss_part, h3, dau2, dag2, act2 = _ffn_fwd(
        x2, row(ffn2_norm), w3b, tm_fwd, cf, "ffn2_fwd_loss", head=(row(final_norm), tgt))

    dx2, dgate2, dup2, d_ffn2_norm = _ffn_dgrad(dx3, x2, row(ffn2_norm), dau2, dag2, w3b, tm_ffn, cf, "ffn2_dgrad")
    (dw_gu2,) = _wgrad([dgate2, dup2], h3, tm_w, tf_w, "ffn2_wgrad_gu")
    (dw_d2,) = _wgrad([act2], dob2, tm_w1, tf_w, "ffn2_wgrad_d")
    by_device = lambda a: a.reshape(a.shape[0], N_DEV, rf, D)
    (dx1, sg, dbda, dbdx, dob1, d_mix_norm, dw_in_t, dw_out), parts_f2 = _mix_bwd(
        dx2, z, u1, xr, hst, x1, mix, row(mix_norm), w_in_t, w_out_f, bda, bdx, cw, lw, vec, tm_mix, "mix_bwd",
        rider=_chips_comm([by_device(dw_gu2), by_device(dw_d2)], every_device=True))
    sums_io = _reduce_level1([by_owner(dw_in_t[None], ri), by_owner(dw_out[None], ro)], "io")
    (dw_d1,), parts_io = _wgrad([act1], dob1, tm_w1, tf_w, "ffn1_wgrad_d", rider=_chips_comm(sums_io))
    sums_d1 = _reduce_level1([by_owner(dw_d1, rf)], "d1")
    dx0, dgate1, dup1, d_ffn1_norm = _ffn_dgrad(dx1, xs, row(ffn1_norm), dau1, dag1, w3a, tm_ffn, cf, "ffn1_dgrad")

    small = [d_ffn1_norm, d_mix_norm, d_ffn2_norm, d_final_norm, sg, _diag_blocks(dbda, H).reshape(-1, D),
             _diag_blocks(dbdx, H).reshape(-1, D), loss_part]
    (dw_g1,), summed_and_parts = _wgrad(
        [dgate1], h1, tm_w1, tf_w, "ffn1_wgrad_g", rider=_both(_small_sum_comm(small), _chips_comm(sums_d1)))
    summed, parts_d1 = summed_and_parts[:len(small)], summed_and_parts[len(small):]
    sums_g1 = _reduce_level1([by_owner(dw_g1, rf)], "g1")
    (dw_u1,), parts_g1 = _wgrad([dup1], h1, tm_w1, tf_w, "ffn1_wgrad_u", rider=_chips_comm(sums_g1))
    sums_u1 = _reduce_level1([by_owner(dw_u1, rf)], "u1")
    parts_u1 = _run_comm(_chips_comm(sums_u1), "rs_chips_u1")

    g_norm1, g_norm_mix, g_norm2, g_norm_final, g_sg, g_w_a, g_w_x, g_loss = summed
    loss = g_loss[0, 0]
    me = 4 * lax.axis_index("x") + 2 * lax.axis_index("y") + lax.axis_index("c")
    chan = lambda full_g: lax.dynamic_slice_in_dim(full_g, me * wc, wc, axis=1)
    grads = {
        "ffn1_norm": g_norm1.reshape(D), "mix_norm": g_norm_mix.reshape(D), "ffn2_norm": g_norm2.reshape(D),
        "final_norm": g_norm_final.reshape(D),
        "conv_dw_bias": g_sg[G_CB], "conv_ln_g": g_sg[G_LNG], "conv_ln_b": g_sg[G_LNB],
        "lru_conv_b": g_sg[G_LCB], "lru_b_a": g_sg[G_BA], "lru_b_x": g_sg[G_BX], "lru_lambda": g_sg[G_LAM],
        "lru_w_a": g_w_a.reshape(lru_w_a.shape), "lru_w_x": g_w_x.reshape(lru_w_x.shape),
        "conv_dw": chan(g_sg[G_CW:G_CW + CONV_K]), "lru_conv_w": chan(g_sg[G_LW:G_LW + LRU_K]),
    }

    weights = dict(ffn1_norm=ffn1_norm, ffn1_w_gate=ffn1_w_gate, ffn1_w_up=ffn1_w_up, ffn1_w_down=ffn1_w_down, mix_norm=mix_norm, w_in=w_in, conv_dw=conv_dw, conv_dw_bias=conv_dw_bias, conv_ln_g=conv_ln_g, conv_ln_b=conv_ln_b, lru_conv_w=lru_conv_w, lru_conv_b=lru_conv_b, lru_w_a=lru_w_a, lru_b_a=lru_b_a, lru_w_x=lru_w_x, lru_b_x=lru_b_x, lru_lambda=lru_lambda, w_out=w_out, ffn2_norm=ffn2_norm, ffn2_w_gate=ffn2_w_gate, ffn2_w_up=ffn2_w_up, ffn2_w_down=ffn2_w_down, final_norm=final_norm)
    moment1 = dict(ffn1_norm=m_ffn1_norm, ffn1_w_gate=m_ffn1_w_gate, ffn1_w_up=m_ffn1_w_up, ffn1_w_down=m_ffn1_w_down, mix_norm=m_mix_norm, w_in=m_w_in, conv_dw=m_conv_dw, conv_dw_bias=m_conv_dw_bias, conv_ln_g=m_conv_ln_g, conv_ln_b=m_conv_ln_b, lru_conv_w=m_lru_conv_w, lru_conv_b=m_lru_conv_b, lru_w_a=m_lru_w_a, lru_b_a=m_lru_b_a, lru_w_x=m_lru_w_x, lru_b_x=m_lru_b_x, lru_lambda=m_lru_lambda, w_out=m_w_out, ffn2_norm=m_ffn2_norm, ffn2_w_gate=m_ffn2_w_gate, ffn2_w_up=m_ffn2_w_up, ffn2_w_down=m_ffn2_w_down, final_norm=m_final_norm)
    moment2 = dict(ffn1_norm=v_ffn1_norm, ffn1_w_gate=v_ffn1_w_gate, ffn1_w_up=v_ffn1_w_up, ffn1_w_down=v_ffn1_w_down, mix_norm=v_mix_norm, w_in=v_w_in, conv_dw=v_conv_dw, conv_dw_bias=v_conv_dw_bias, conv_ln_g=v_conv_ln_g, conv_ln_b=v_conv_ln_b, lru_conv_w=v_lru_conv_w, lru_conv_b=v_lru_conv_b, lru_w_a=v_lru_w_a, lru_b_a=v_lru_b_a, lru_w_x=v_lru_w_x, lru_b_x=v_lru_b_x, lru_lambda=v_lru_lambda, w_out=v_w_out, ffn2_norm=v_ffn2_norm, ffn2_w_gate=v_ffn2_w_gate, ffn2_w_up=v_ffn2_w_up, ffn2_w_down=v_ffn2_w_down, final_norm=v_final_norm)
    order = list(weights)
    big = {"ffn1_w_gate": (parts_g1[0], 0, True), "ffn1_w_up": (parts_u1[0], 0, True),
           "ffn1_w_down": (parts_d1[0], 0, False), "w_in": (parts_io[0], 0, True), "w_out": (parts_io[1], 0, False),
           "ffn2_w_gate": (parts_f2[0], 0, True), "ffn2_w_up": (parts_f2[0], 1, True),
           "ffn2_w_down": (parts_f2[1], 0, False)}
    delta, new_m, new_v = {}, {}, {}
    for n, (parts, k, d_major) in big.items():
        operands = weights[n], moment1[n], moment2[n]
        if d_major and n != "w_in":
            results = _finish(parts, k, *[a.T for a in operands], False, "finish_" + n)
            grads[n], delta[n], new_m[n], new_v[n] = [r.T for r in results]
        else:
            grads[n], delta[n], new_m[n], new_v[n] = _finish(parts, k, *operands, d_major, "finish_" + n)
    rest = [n for n in order if n not in big]
    updates = _adamw_each([weights[n] for n in rest], [grads[n] for n in rest], [moment1[n] for n in rest],
                          [moment2[n] for n in rest], "adamw_small")
    for k, n in enumerate(rest):
        delta[n], new_m[n], new_v[n] = updates[k], updates[len(rest) + k], updates[2 * len(rest) + k]

    return (loss, dx0.reshape(x.shape), *[grads[n] for n in order], *[delta[n] for n in order],
            *[new_m[n] for n in order], *[new_v[n] for n in order])
```

```python
import functools
import math

import jax
import jax.numpy as jnp
from jax import lax
from jax.experimental import pallas as pl
from jax.experimental.pallas import tpu as pltpu

F32 = jnp.float32
BF16 = jnp.bfloat16
MESH = pl.DeviceIdType.MESH

N_DEV = 8
N_CHIP = 4
SUBLANES = 8
RMS_EPS = 1e-6
LN_EPS = 1e-5
LRU_C = 8.0
CONV_K = 31
LRU_K = 4
CONV_HALO = 32
LRU_HALO = 8
FFN_RES = 0.5
ADAM_LR, ADAM_B1, ADAM_B2, ADAM_EPS, ADAM_WD, ADAM_STEP = 0.001, 0.9, 0.999, 1e-08, 0.01, 10
GELU_K = math.sqrt(2.0 / math.pi)
GELU_C = 0.044715

MIB = 1024 * 1024
NT = (((1,), (1,)), ((), ()))
NN = (((1,), (0,)), ((), ()))
TN = (((0,), (0,)), ((), ()))

V_CB, V_LNG, V_LNB, V_LCB, V_BA, V_BX, V_LAM = range(0, 7 * SUBLANES, SUBLANES)
G_CW = 0
G_CB, G_LNG, G_LNB = 31, 32, 33
G_LW = 34
G_LCB, G_BA, G_BX, G_LAM = 38, 39, 40, 41
G_ROWS = 48

HBM_SPEC = pl.BlockSpec(memory_space=pltpu.HBM)
VMEM_SPEC = pl.BlockSpec(memory_space=pltpu.VMEM)


def _dot(a, b, dims):
    return lax.dot_general(a, b, dims, preferred_element_type=F32)


def _sigmoid(x):
    return 1.0 / (1.0 + jnp.exp(-x))


def _gelu_parts(x):
    x2 = x * x
    th = jnp.tanh(GELU_K * x * (1.0 + GELU_C * x2))
    gl = 0.5 * x * (1.0 + th)
    dgl = 0.5 * (1.0 + th) + 0.5 * x * (1.0 - th * th) * GELU_K * (1.0 + 3.0 * GELU_C * x2)
    return gl, dgl


def _neg_expm1(y):
    series = -y * (1.0 + y * (1.0 / 2) * (1.0 + y * (1.0 / 3) * (1.0 + y * (1.0 / 4) * (1.0 + y * (1.0 / 5) * (1.0 + y * (1.0 / 6))))))
    return jnp.where(y > -0.25, series, 1.0 - jnp.exp(y))


def _softplus_neg(lam):
    t = -lam
    e = jnp.exp(-jnp.abs(t))
    s = 1.0 + e
    log1p_e = jnp.log(s) - ((s - 1.0) - e) / s
    return jnp.maximum(t, 0.0) + log1p_e


def _rms_stats(xv):
    rstd = lax.rsqrt(jnp.mean(xv * xv, axis=-1, keepdims=True) + RMS_EPS)
    return xv * rstd, rstd


def _rms_bwd(xhat, rstd, g, dh):
    dxhat = dh * g
    dx = rstd * (dxhat - xhat * jnp.mean(dxhat * xhat, axis=-1, keepdims=True))
    return dx, jnp.sum(dh * xhat, axis=0, keepdims=True)


def _row_windows(buf_ref, n_rows, offsets):
    total = buf_ref.shape[0]
    full = buf_ref[...]
    for b in range(SUBLANES):
        offs = [o for o in offsets if o % SUBLANES == b]
        if not offs:
            continue
        assert max(offs) + n_rows <= total
        moved = full if b == 0 else pltpu.roll(full, total - b, 0)
        for o in offs:
            yield o, moved[o - b:o - b + n_rows, :]


def _scan_rows(av, bv, edge, out_ref, reverse=False):
    tm, W = av.shape
    sub = lax.broadcasted_iota(jnp.int32, (tm, W), 0) % SUBLANES
    s = 1
    while s < SUBLANES:
        keep = (sub < SUBLANES - s) if reverse else (sub >= s)
        shift = tm - s if reverse else s
        bv = jnp.where(keep, av * pltpu.roll(bv, shift, 0) + bv, bv)
        av = jnp.where(keep, av * pltpu.roll(av, shift, 0), av)
        s *= 2
    starts = range(0, tm, SUBLANES)
    for r0 in (reversed(starts) if reverse else starts):
        group = av[r0:r0 + SUBLANES, :] * edge + bv[r0:r0 + SUBLANES, :]
        out_ref[r0:r0 + SUBLANES, :] = group
        edge = group[0:1, :] if reverse else group[SUBLANES - 1:SUBLANES, :]
    return edge


class _Comm:
    def __init__(self, arrays, in_specs, out_shapes, out_specs, scratch, start, wait, aliases=None):
        self.arrays, self.in_specs = list(arrays), list(in_specs)
        self.out_shapes, self.out_specs = list(out_shapes), list(out_specs)
        self.scratch, self.start, self.wait = list(scratch), start, wait
        self.aliases = dict(aliases or {})


def _in_hbm(a):
    return pltpu.with_memory_space_constraint(a, pltpu.HBM)


def _operands(comm):
    return [a if spec is VMEM_SPEC else _in_hbm(a) for a, spec in zip(comm.arrays, comm.in_specs)]


def _call(body, *, name, grid, in_specs, out_specs, out_shape, scratch_shapes, vmem_mib, args, rider=None,
          num_scalar_prefetch=0):
    params = pltpu.CompilerParams(dimension_semantics=("arbitrary",) * len(grid), vmem_limit_bytes=vmem_mib * MIB)
    args = [a if k < num_scalar_prefetch else _in_hbm(a) for k, a in enumerate(args)]
    if rider is None:
        return pl.pallas_call(
            body, name=name,
            grid_spec=pltpu.PrefetchScalarGridSpec(
                num_scalar_prefetch=num_scalar_prefetch, grid=grid, in_specs=in_specs, out_specs=out_specs,
                scratch_shapes=scratch_shapes),
            out_shape=out_shape, compiler_params=params)(*args)
    assert num_scalar_prefetch == 0
    n_in, n_out, n_scr = len(in_specs), len(out_specs), len(scratch_shapes)
    r_in, r_out = len(rider.arrays), len(rider.out_shapes)
    n_axes = len(grid)

    def carried(*refs):
        pos = [0]

        def take(n):
            pos[0] += n
            return refs[pos[0] - n:pos[0]]

        ins, r_ins, outs, r_outs, scr, r_scr = take(n_in), take(r_in), take(n_out), take(r_out), take(n_scr), take(len(rider.scratch))
        first = pl.program_id(0) == 0
        last = pl.program_id(0) == grid[0] - 1
        for ax in range(1, n_axes):
            first = first & (pl.program_id(ax) == 0)
            last = last & (pl.program_id(ax) == grid[ax] - 1)

        @pl.when(first)
        def _():
            rider.start(r_ins, r_outs, r_scr)

        body(*ins, *outs, *scr)

        @pl.when(last)
        def _():
            rider.wait(r_ins, r_outs, r_scr)

    res = pl.pallas_call(
        carried, name=name,
        grid=grid,
        in_specs=list(in_specs) + rider.in_specs,
        out_specs=list(out_specs) + rider.out_specs,
        out_shape=list(out_shape) + rider.out_shapes,
        scratch_shapes=list(scratch_shapes) + rider.scratch,
        input_output_aliases={n_in + i: n_out + o for i, o in rider.aliases.items()},
        compiler_params=params)(*args, *_operands(rider))
    return res[:n_out], res[n_out:]


def _run_comm(comm, name):
    n_in, n_out = len(comm.arrays), len(comm.out_shapes)

    def body(*refs):
        ins, outs, scr = refs[:n_in], refs[n_in:n_in + n_out], refs[n_in + n_out:]
        comm.start(ins, outs, scr)
        comm.wait(ins, outs, scr)

    return pl.pallas_call(
        body, name=name,
        in_specs=comm.in_specs, out_specs=comm.out_specs, out_shape=comm.out_shapes,
        scratch_shapes=comm.scratch, input_output_aliases=comm.aliases,
        compiler_params=pltpu.CompilerParams(vmem_limit_bytes=24 * MIB))(*_operands(comm))


def _both(a, b):
    ni, no, ns = len(a.arrays), len(a.out_shapes), len(a.scratch)

    def start(ins, outs, scr):
        a.start(ins[:ni], outs[:no], scr[:ns])
        b.start(ins[ni:], outs[no:], scr[ns:])

    def wait(ins, outs, scr):
        a.wait(ins[:ni], outs[:no], scr[:ns])
        b.wait(ins[ni:], outs[no:], scr[ns:])

    aliases = dict(a.aliases)
    aliases.update({ni + i: no + o for i, o in b.aliases.items()})
    return _Comm(a.arrays + b.arrays, a.in_specs + b.in_specs, a.out_shapes + b.out_shapes,
                 a.out_specs + b.out_specs, a.scratch + b.scratch, start, wait, aliases)


def _place():
    return lax.axis_index("x"), lax.axis_index("y"), lax.axis_index("c")


def _peer(k):
    x, y, c = _place()
    px, py, pc = x ^ ((k >> 2) & 1), y ^ ((k >> 1) & 1), c ^ (k & 1)
    return (px, py, pc), 4 * px + 2 * py + pc


SIBLING = 1
SAME_CORE = (2, 4, 6)
EVERYONE = tuple(range(1, N_DEV))


def _gather_comm(shards, views, out_shapes, relations):
    na = len(shards)

    def copies(ins, outs, scr):
        send_sems, recv_sems, _ = scr
        _, me = _peer(0)
        out = []
        for a in range(na):
            for k in relations[a]:
                peer, theirs = _peer(k)
                send = functools.partial(
                    pltpu.make_async_remote_copy,
                    src_ref=ins[a], dst_ref=views[a](outs[a], me),
                    send_sem=send_sems.at[7 * a + k - 1], recv_sem=recv_sems.at[7 * a + k - 1],
                    device_id=peer, device_id_type=MESH)
                recv = functools.partial(
                    pltpu.make_async_remote_copy,
                    src_ref=ins[a], dst_ref=views[a](outs[a], theirs),
                    send_sem=send_sems.at[7 * a + k - 1], recv_sem=recv_sems.at[7 * a + k - 1],
                    device_id=peer, device_id_type=MESH)
                out.append((send, recv))
        return out

    def local(ins, outs, scr):
        _, me = _peer(0)
        return [pltpu.make_async_copy(ins[a], views[a](outs[a], me), scr[2].at[a]) for a in range(na)]

    def start(ins, outs, scr):
        for cp in local(ins, outs, scr):
            cp.start()
        for send, _ in copies(ins, outs, scr):
            send().start()

    def wait(ins, outs, scr):
        for _, recv in copies(ins, outs, scr):
            recv().wait_recv()
        for send, _ in copies(ins, outs, scr):
            send().wait_send()
        for cp in local(ins, outs, scr):
            cp.wait()

    return _Comm(shards, [HBM_SPEC] * na, out_shapes, [HBM_SPEC] * na,
                 [pltpu.SemaphoreType.DMA((7 * na,)), pltpu.SemaphoreType.DMA((7 * na,)),
                  pltpu.SemaphoreType.DMA((na,))], start, wait)


def _forward_comm(gathered, views):
    na = len(gathered)
    shapes = [jax.ShapeDtypeStruct(g.shape, g.dtype) for g in gathered]

    def copies(outs, scr):
        send_sems, recv_sems = scr
        sibling, _ = _peer(SIBLING)
        out = []
        for a in range(na):
            for n, k in enumerate(SAME_CORE):
                _, mine = _peer(k)
                _, theirs = _peer(k ^ SIBLING)
                send = functools.partial(
                    pltpu.make_async_remote_copy,
                    src_ref=views[a](outs[a], mine), dst_ref=views[a](outs[a], mine),
                    send_sem=send_sems.at[3 * a + n], recv_sem=recv_sems.at[3 * a + n],
                    device_id=sibling, device_id_type=MESH)
                recv = functools.partial(
                    pltpu.make_async_remote_copy,
                    src_ref=views[a](outs[a], mine), dst_ref=views[a](outs[a], theirs),
                    send_sem=send_sems.at[3 * a + n], recv_sem=recv_sems.at[3 * a + n],
                    device_id=sibling, device_id_type=MESH)
                out.append((send, recv))
        return out

    def start(ins, outs, scr):
        for send, _ in copies(outs, scr):
            send().start()

    def wait(ins, outs, scr):
        for _, recv in copies(outs, scr):
            recv().wait_recv()
        for send, _ in copies(outs, scr):
            send().wait_send()

    return _Comm(gathered, [HBM_SPEC] * na, shapes, [HBM_SPEC] * na,
                 [pltpu.SemaphoreType.DMA((3 * na,)), pltpu.SemaphoreType.DMA((3 * na,))], start, wait,
                 aliases={a: a for a in range(na)})


Y_NEIGHBOUR, X_NEIGHBOUR, DIAGONAL = SAME_CORE


def _all_gather(shards, views, out_shapes, name):
    na = len(shards)
    near = (SIBLING, Y_NEIGHBOUR, X_NEIGHBOUR)
    level1 = _gather_comm(shards, views, out_shapes, [near] * na)

    def body(*refs):
        ins, outs = refs[:na], refs[na:2 * na]
        send_sems, recv_sems, local_sems, fwd_send, fwd_recv, relay_send, relay_recv = refs[2 * na:]
        sibling, _ = _peer(SIBLING)
        c = lax.axis_index("c")
        level1.start(ins, outs, (send_sems, recv_sems, local_sems))

        def block_copy(a, block, to, send_sem, recv_sem):
            return pltpu.make_async_remote_copy(
                src_ref=views[a](outs[a], block), dst_ref=views[a](outs[a], block),
                send_sem=send_sem, recv_sem=recv_sem, device_id=to, device_id_type=MESH)

        def to_sibling(a, n, k):
            _, mine = _peer(k)
            _, theirs = _peer(k ^ SIBLING)
            fwd = block_copy(a, mine, sibling, fwd_send.at[3 * a + n], fwd_recv.at[3 * a + n])
            fwd.start()
            return fwd, block_copy(a, theirs, sibling, fwd_send.at[3 * a + n], fwd_recv.at[3 * a + n])

        passed, landing = [], []
        for a in range(na):
            for n, k in enumerate((Y_NEIGHBOUR, X_NEIGHBOUR)):
                peer, origin = _peer(k)
                pltpu.make_async_remote_copy(
                    src_ref=ins[a], dst_ref=views[a](outs[a], origin),
                    send_sem=send_sems.at[7 * a + k - 1], recv_sem=recv_sems.at[7 * a + k - 1],
                    device_id=peer, device_id_type=MESH).wait_recv()

                @pl.when(c == (0 if k == X_NEIGHBOUR else 1))
                def _():
                    other, _ = _peer(DIAGONAL ^ k)
                    block_copy(a, origin, other, relay_send.at[a], relay_recv.at[a]).start()

                fwd, lands = to_sibling(a, n, k)
                passed.append(fwd)
                landing.append(lands)
            _, far = _peer(DIAGONAL)
            block_copy(a, far, sibling, relay_send.at[a], relay_recv.at[a]).wait_recv()
            fwd, lands = to_sibling(a, 2, DIAGONAL)
            passed.append(fwd)
            landing.append(lands)
        for a in range(na):
            _, theirs = _peer(SIBLING)
            pltpu.make_async_remote_copy(
                src_ref=ins[a], dst_ref=views[a](outs[a], theirs),
                send_sem=send_sems.at[7 * a + SIBLING - 1], recv_sem=recv_sems.at[7 * a + SIBLING - 1],
                device_id=sibling, device_id_type=MESH).wait_recv()
        for cp in landing:
            cp.wait_recv()
        for cp in passed:
            cp.wait_send()
        _, me = _peer(0)
        for a in range(na):
            block_copy(a, me, sibling, relay_send.at[a], relay_recv.at[a]).wait_send()
            for k in near:
                peer, _ = _peer(k)
                pltpu.make_async_remote_copy(
                    src_ref=ins[a], dst_ref=views[a](outs[a], me),
                    send_sem=send_sems.at[7 * a + k - 1], recv_sem=recv_sems.at[7 * a + k - 1],
                    device_id=peer, device_id_type=MESH).wait_send()
            pltpu.make_async_copy(ins[a], views[a](outs[a], me), local_sems.at[a]).wait()

    return pl.pallas_call(
        body, name=name,
        in_specs=[HBM_SPEC] * na, out_specs=[HBM_SPEC] * na, out_shape=out_shapes,
        scratch_shapes=level1.scratch + [pltpu.SemaphoreType.DMA((3 * na,)), pltpu.SemaphoreType.DMA((3 * na,)),
                                         pltpu.SemaphoreType.DMA((na,)), pltpu.SemaphoreType.DMA((na,))],
    )(*[_in_hbm(s) for s in shards])


def _sibling_comm(grads):
    na = len(grads)
    shapes = [jax.ShapeDtypeStruct(g.shape[:2] + g.shape[3:], g.dtype) for g in grads]

    def copies(ins, outs, scr):
        x, y, c = _place()
        return [pltpu.make_async_remote_copy(
            src_ref=ins[a].at[:, :, 1 - c], dst_ref=outs[a],
            send_sem=scr[0].at[a], recv_sem=scr[1].at[a],
            device_id=(x, y, 1 - c), device_id_type=MESH) for a in range(na)]

    def start(ins, outs, scr):
        for cp in copies(ins, outs, scr):
            cp.start()

    def wait(ins, outs, scr):
        for cp in copies(ins, outs, scr):
            cp.wait()

    return _Comm(grads, [HBM_SPEC] * na, shapes, [HBM_SPEC] * na,
                 [pltpu.SemaphoreType.DMA((na,)), pltpu.SemaphoreType.DMA((na,))], start, wait)


def _chips_comm(sums, every_device=False):
    na = len(sums)
    shapes = [jax.ShapeDtypeStruct(s.shape, s.dtype) for s in sums]
    relations = EVERYONE if every_device else SAME_CORE
    nr = len(relations)

    def block(px, py, pc):
        return 4 * px + 2 * py + pc if every_device else 2 * px + py

    def copies(ins, outs, scr):
        mine = block(*_place())
        out = []
        for a in range(na):
            for n, k in enumerate(relations):
                peer, _ = _peer(k)
                theirs = block(*peer)
                send = functools.partial(
                    pltpu.make_async_remote_copy,
                    src_ref=ins[a].at[:, theirs], dst_ref=outs[a].at[:, mine],
                    send_sem=scr[0].at[nr * a + n], recv_sem=scr[1].at[nr * a + n],
                    device_id=peer, device_id_type=MESH)
                recv = functools.partial(
                    pltpu.make_async_remote_copy,
                    src_ref=ins[a].at[:, mine], dst_ref=outs[a].at[:, theirs],
                    send_sem=scr[0].at[nr * a + n], recv_sem=scr[1].at[nr * a + n],
                    device_id=peer, device_id_type=MESH)
                out.append((send, recv))
        return out

    def local(ins, outs, scr):
        mine = block(*_place())
        return [pltpu.make_async_copy(ins[a].at[:, mine], outs[a].at[:, mine], scr[2].at[a]) for a in range(na)]

    def start(ins, outs, scr):
        for cp in local(ins, outs, scr):
            cp.start()
        for send, _ in copies(ins, outs, scr):
            send().start()

    def wait(ins, outs, scr):
        for _, recv in copies(ins, outs, scr):
            recv().wait_recv()
        for send, _ in copies(ins, outs, scr):
            send().wait_send()
        for cp in local(ins, outs, scr):
            cp.wait()

    return _Comm(sums, [HBM_SPEC] * na, shapes, [HBM_SPEC] * na,
                 [pltpu.SemaphoreType.DMA((nr * na,)), pltpu.SemaphoreType.DMA((nr * na,)),
                  pltpu.SemaphoreType.DMA((na,))], start, wait)


def _small_sum_comm(arrays):
    na = len(arrays)

    def copies(ins, scr):
        bufs, send_sems, recv_sems = scr[:na], scr[na], scr[na + 1]
        _, me = _peer(0)
        out = []
        for a in range(na):
            for k in EVERYONE:
                peer, theirs = _peer(k)
                sems = dict(send_sem=send_sems.at[7 * a + k - 1], recv_sem=recv_sems.at[7 * a + k - 1])
                send = functools.partial(
                    pltpu.make_async_remote_copy,
                    src_ref=ins[a], dst_ref=bufs[a].at[me], device_id=peer, device_id_type=MESH, **sems)
                recv = functools.partial(
                    pltpu.make_async_remote_copy,
                    src_ref=ins[a], dst_ref=bufs[a].at[theirs], device_id=peer, device_id_type=MESH, **sems)
                out.append((send, recv))
        return out

    def start(ins, outs, scr):
        _, me = _peer(0)
        for a in range(na):
            scr[a][me] = ins[a][...]
        for send, _ in copies(ins, scr):
            send().start()

    def wait(ins, outs, scr):
        for _, recv in copies(ins, scr):
            recv().wait_recv()
        for send, _ in copies(ins, scr):
            send().wait_send()
        for a in range(na):
            acc = scr[a][0]
            for j in range(1, N_DEV):
                acc = acc + scr[a][j]
            outs[a][...] = acc

    return _Comm(arrays, [VMEM_SPEC] * na, [jax.ShapeDtypeStruct(s.shape, F32) for s in arrays], [VMEM_SPEC] * na,
                 [pltpu.VMEM((N_DEV,) + s.shape, F32) for s in arrays]
                 + [pltpu.SemaphoreType.DMA((7 * na,)), pltpu.SemaphoreType.DMA((7 * na,))], start, wait)


def _prep_weights(ffn1, ffn2, w_in, w_out, name):
    rf, D = ffn1[2].shape
    ri, ro = w_in.shape[1], w_out.shape[0]

    def body(g1, u1, d1, g2, u2, d2, wi, wo, p1_ref, p2_ref, pi_ref, po_ref):
        for p_ref, shards in ((p1_ref, (g1, u1, d1)), (p2_ref, (g2, u2, d2))):
            for k, shard in enumerate(shards):
                p_ref[k] = shard[...].astype(BF16)
        pi_ref[...] = wi[...].T.astype(BF16)
        po_ref[...] = wo[...].astype(BF16)

    args = (*ffn1, *ffn2, w_in, w_out)
    whole = lambda shape: pl.BlockSpec(shape, lambda i: (0,) * len(shape))
    out_shapes = [(3, rf, D), (3, rf, D), (ri, D), (ro, D)]
    return _call(
        body, name=name, grid=(1,),
        in_specs=[whole(a.shape) for a in args], out_specs=[whole(s) for s in out_shapes],
        out_shape=[jax.ShapeDtypeStruct(s, BF16) for s in out_shapes],
        scratch_shapes=[], vmem_mib=48, args=args)


def _load_weights(w_hbm, w_vmem, sem):
    @pl.when(pl.program_id(0) == 0)
    def _():
        copies = [pltpu.make_async_copy(w_hbm.at[k], w_vmem.at[k], sem.at[k]) for k in range(3)]
        for cp in copies:
            cp.start()
        for cp in copies:
            cp.wait()


def _ffn_fwd(x, g, w3, tm, cf, name, rider=None, head=None):
    T, D = x.shape
    F = w3.shape[1]
    n_head = 0 if head is None else 2

    def body(x_ref, g_ref, w_hbm, *refs):
        head_refs, refs = refs[:n_head], refs[n_head:]
        if head is None:
            (xo_ref, h_ref, dau_ref, dag_ref, act_ref, wv, sem) = refs
        else:
            (dx_ref, dob_ref, dgf_ref, loss_ref, h_ref, dau_ref, dag_ref, act_ref, wv, sem) = refs
        _load_weights(w_hbm, wv, sem)
        xhat, _ = _rms_stats(x_ref[...])
        hb = (xhat * g_ref[...]).astype(BF16)
        h_ref[...] = hb
        for lo in range(0, F, cf):
            gate = _dot(hb, wv[0, lo:lo + cf, :], NT)
            up = _dot(hb, wv[1, lo:lo + cf, :], NT)
            sig = _sigmoid(gate)
            silu = gate * sig
            dau_ref[:, lo:lo + cf] = silu.astype(BF16)
            dag_ref[:, lo:lo + cf] = (up * (sig * (1.0 + gate * (1.0 - sig)))).astype(BF16)
            act_ref[:, lo:lo + cf] = (silu * up).astype(BF16)
        x_out = x_ref[...] + FFN_RES * _dot(act_ref[...], wv[2], NN)
        if head is None:
            xo_ref[...] = x_out
            return

        @pl.when(pl.program_id(0) == 0)
        def _():
            dgf_ref[...] = jnp.zeros_like(dgf_ref)
            loss_ref[...] = jnp.zeros_like(loss_ref)

        gf_ref, tgt_ref = head_refs
        yhat, rstd = _rms_stats(x_out)
        gf = gf_ref[...]
        err = yhat * gf - tgt_ref[...]
        loss_ref[...] += (0.5 / D) * jnp.sum(err * err)
        dx, dgf = _rms_bwd(yhat, rstd, gf, err * (1.0 / D))
        dx_ref[...] = dx
        dob_ref[...] = (FFN_RES * dx).astype(BF16)
        dgf_ref[...] += dgf

    row = pl.BlockSpec((tm, D), lambda i: (i, 0))
    hid = pl.BlockSpec((tm, F), lambda i: (i, 0))
    vec = pl.BlockSpec((1, D), lambda i: (0, 0))
    row_f32, row_bf16 = jax.ShapeDtypeStruct((T, D), F32), jax.ShapeDtypeStruct((T, D), BF16)
    if head is None:
        first_specs, first_shapes = [row], [row_f32]
    else:
        first_specs = [row, row, vec, pl.BlockSpec((1, 128), lambda i: (0, 0))]
        first_shapes = [row_f32, row_bf16, jax.ShapeDtypeStruct((1, D), F32), jax.ShapeDtypeStruct((1, 128), F32)]
    return _call(
        body, name=name, grid=(T // tm,),
        in_specs=[row, vec, HBM_SPEC] + ([] if head is None else [vec, row]),
        out_specs=first_specs + [row, hid, hid, hid],
        out_shape=first_shapes + [row_bf16] + [jax.ShapeDtypeStruct((T, F), BF16)] * 3,
        scratch_shapes=[pltpu.VMEM((3, F, D), BF16), pltpu.SemaphoreType.DMA((3,))],
        vmem_mib=60, args=(x, g, w3) + (() if head is None else tuple(head)), rider=rider)


def _ffn_dgrad(dout, x, g, dau, dag, w3, tm, cf, name, rider=None):
    T, D = x.shape
    F = w3.shape[1]

    def body(do_ref, x_ref, g_ref, dau_ref, dag_ref, w_hbm, dx_ref, dgate_ref, dup_ref, dg_ref, wv, sem):
        _load_weights(w_hbm, wv, sem)

        @pl.when(pl.program_id(0) == 0)
        def _():
            dg_ref[...] = jnp.zeros_like(dg_ref)

        dob = (FFN_RES * do_ref[...]).astype(BF16)
        for lo in range(0, F, cf):
            dact = _dot(dob, wv[2, lo:lo + cf, :], NT)
            dup_ref[:, lo:lo + cf] = (dact * dau_ref[:, lo:lo + cf].astype(F32)).astype(BF16)
            dgate_ref[:, lo:lo + cf] = (dact * dag_ref[:, lo:lo + cf].astype(F32)).astype(BF16)
        dh = _dot(dgate_ref[...], wv[0], NN) + _dot(dup_ref[...], wv[1], NN)
        xhat, rstd = _rms_stats(x_ref[...])
        dx, dg = _rms_bwd(xhat, rstd, g_ref[...], dh)
        dx_ref[...] = do_ref[...] + dx
        dg_ref[...] += dg

    row = pl.BlockSpec((tm, D), lambda i: (i, 0))
    hid = pl.BlockSpec((tm, F), lambda i: (i, 0))
    vec = pl.BlockSpec((1, D), lambda i: (0, 0))
    return _call(
        body, name=name, grid=(T // tm,),
        in_specs=[row, row, vec, hid, hid, HBM_SPEC],
        out_specs=[row, hid, hid, vec],
        out_shape=[jax.ShapeDtypeStruct((T, D), F32), jax.ShapeDtypeStruct((T, F), BF16),
                   jax.ShapeDtypeStruct((T, F), BF16), jax.ShapeDtypeStruct((1, D), F32)],
        scratch_shapes=[pltpu.VMEM((3, F, D), BF16), pltpu.SemaphoreType.DMA((3,))],
        vmem_mib=52, args=(dout, x, g, dau, dag, w3), rider=rider)


def _wgrad(lhs, rhs, tm, tf, name, rider=None):
    T, F = lhs[0].shape
    D = rhs.shape[1]
    K = len(lhs)

    def body(*refs):
        lhs_refs, rhs_ref, dw_ref, accs = refs[:K], refs[K], refs[K + 1], refs[K + 2:]
        i = pl.program_id(1)

        @pl.when(i == 0)
        def _():
            for acc in accs:
                acc[...] = jnp.zeros_like(acc)

        rv = rhs_ref[...]
        for acc, lhs_ref in zip(accs, lhs_refs):
            acc[...] += _dot(lhs_ref[...], rv, TN)

        @pl.when(i == pl.num_programs(1) - 1)
        def _():
            for k, acc in enumerate(accs):
                dw_ref[k] = acc[...].astype(BF16)

    hid = pl.BlockSpec((tm, tf), lambda f, i: (i, f))
    return _call(
        body, name=name, grid=(F // tf, T // tm),
        in_specs=[hid] * K + [pl.BlockSpec((tm, D), lambda f, i: (i, 0))],
        out_specs=[pl.BlockSpec((K, tf, D), lambda f, i: (0, f, 0))],
        out_shape=[jax.ShapeDtypeStruct((K, F, D), BF16)],
        scratch_shapes=[pltpu.VMEM((tf, D), F32)] * K,
        vmem_mib=56, args=(*lhs, rhs), rider=rider)


def _lru_gates(xr, bda_ref, bdx_ref, vec_ref):
    xrb = xr.astype(BF16)
    r = _sigmoid(_dot(xrb, bda_ref[...], NN) + vec_ref[V_BA:V_BA + 1, :])
    ig = _sigmoid(_dot(xrb, bdx_ref[...], NN) + vec_ref[V_BX:V_BX + 1, :])
    sp = _softplus_neg(vec_ref[V_LAM:V_LAM + 1, :])
    log_a = (-LRU_C * sp) * r
    a = jnp.exp(log_a)
    mult = jnp.sqrt(_neg_expm1(2.0 * log_a))
    return xrb, r, ig, sp, a, mult


def _layernorm_stats(u1):
    xc = u1 - jnp.mean(u1, axis=-1, keepdims=True)
    rs = lax.rsqrt(jnp.mean(xc * xc, axis=-1, keepdims=True) + LN_EPS)
    return xc * rs, rs


def _mix_core_fwd(x1, g, w_in_t, w_out, bda, bdx, cw, lw, vec, tm, name, rider=None):
    T, D = x1.shape
    W = cw.shape[1]
    assert tm >= CONV_HALO and w_in_t.shape[0] == 4 * W

    def body(x1_ref, g_ref, wi_ref, wo_ref, bda_ref, bdx_ref, cw_ref, lw_ref, vec_ref,
             x2_ref, z_ref, mix_ref, u1_ref, xr_ref, hst_ref, ubuf, rbuf, hc):
        @pl.when(pl.program_id(0) == 0)
        def _():
            ubuf[0:CONV_HALO, :] = jnp.zeros((CONV_HALO, W), F32)
            rbuf[0:LRU_HALO, :] = jnp.zeros((LRU_HALO, W), F32)
            hc[...] = jnp.zeros_like(hc)

        xhat, _ = _rms_stats(x1_ref[...])
        z_ref[...] = _dot((xhat * g_ref[...]).astype(BF16), wi_ref[...], NT)

        ubuf[CONV_HALO:CONV_HALO + tm, :] = z_ref[:, 0:W] * _sigmoid(z_ref[:, W:2 * W])
        u1 = jnp.zeros((tm, W), F32) + vec_ref[V_CB:V_CB + 1, :]
        base = CONV_HALO - (CONV_K - 1)
        for off, win in _row_windows(ubuf, tm, range(base, base + CONV_K)):
            u1 = u1 + cw_ref[off - base:off - base + 1, :] * win
        ubuf[0:CONV_HALO, :] = ubuf[tm:tm + CONV_HALO, :]
        u1_ref[...] = u1
        xh, _ = _layernorm_stats(u1)
        u2 = xh * vec_ref[V_LNG:V_LNG + 1, :] + vec_ref[V_LNB:V_LNB + 1, :]
        ub = (u2 * _sigmoid(u2)).astype(BF16)
        mix_ref[:, 0:W] = ub

        rbuf[LRU_HALO:LRU_HALO + tm, :] = z_ref[:, 2 * W:3 * W]
        xr = jnp.zeros((tm, W), F32) + vec_ref[V_LCB:V_LCB + 1, :]
        for k in range(LRU_K):
            off = LRU_HALO - (LRU_K - 1) + k
            xr = xr + lw_ref[k:k + 1, :] * rbuf[off:off + tm, :]
        rbuf[0:LRU_HALO, :] = rbuf[tm:tm + LRU_HALO, :]
        xr_ref[...] = xr
        _, _, ig, _, a, mult = _lru_gates(xr, bda_ref, bdx_ref, vec_ref)
        hc[0:1, :] = _scan_rows(a, mult * (ig * xr), hc[0:1, :], hst_ref)
        gl, _ = _gelu_parts(z_ref[:, 3 * W:4 * W])
        yb = (hst_ref[...] * gl).astype(BF16)
        mix_ref[:, W:2 * W] = yb

        x2_ref[...] = x1_ref[...] + _dot(ub, wo_ref[0:W, :], NN) + _dot(yb, wo_ref[W:2 * W, :], NN)

    full = lambda a: pl.BlockSpec(a.shape, lambda i: (0,) * a.ndim)
    tile = lambda n: pl.BlockSpec((tm, n), lambda i: (i, 0))
    return _call(
        body, name=name, grid=(T // tm,),
        in_specs=[tile(D), full(g), full(w_in_t), full(w_out), full(bda), full(bdx), full(cw), full(lw), full(vec)],
        out_specs=[tile(D), tile(4 * W), tile(2 * W), tile(W), tile(W), tile(W)],
        out_shape=[jax.ShapeDtypeStruct((T, D), F32), jax.ShapeDtypeStruct((T, 4 * W), F32),
                   jax.ShapeDtypeStruct((T, 2 * W), BF16), jax.ShapeDtypeStruct((T, W), F32),
                   jax.ShapeDtypeStruct((T, W), F32), jax.ShapeDtypeStruct((T, W), F32)],
        scratch_shapes=[pltpu.VMEM((tm + CONV_HALO, W), F32), pltpu.VMEM((tm + LRU_HALO, W), F32),
                        pltpu.VMEM((8, W), F32)],
        vmem_mib=56, args=(x1, g, w_in_t, w_out, bda, bdx, cw, lw, vec), rider=rider)


def _mix_bwd(dx2, z, u1, xr, hst, x1, mix, g, w_in_t, w_out, bda, bdx, cw, lw, vec, tm, name, rider=None):
    T, D = dx2.shape
    W = cw.shape[1]
    nt = T // tm
    assert tm >= CONV_HALO and tm % CONV_HALO == 0

    def body(dx_ref, z_ref, zh_ref, u1_ref, xr_ref, h_ref, hh_ref, wo_ref, bda_ref, bdx_ref, cw_ref, lw_ref, vec_ref,
             x1_ref, mix_ref, g_ref, wi_ref,
             dx1_ref, sg_ref, dbda_ref, dbdx_ref, dob_ref, dg_ref, dwi_ref, dwo_ref,
             u0buf, du1buf, rxbuf, dxrbuf, gbuf, gc, spacc, dz_ref, ai_ref, ao_ref):
        i = pl.program_id(0)
        first = i == nt - 1
        row = lax.broadcasted_iota(jnp.int32, (tm, W), 0)

        @pl.when(i == 0)
        def _():
            sg_ref[...] = jnp.zeros_like(sg_ref)
            dbda_ref[...] = jnp.zeros_like(dbda_ref)
            dbdx_ref[...] = jnp.zeros_like(dbdx_ref)
            du1buf[tm:tm + CONV_HALO, :] = jnp.zeros((CONV_HALO, W), F32)
            dxrbuf[tm:tm + LRU_HALO, :] = jnp.zeros((LRU_HALO, W), F32)
            gc[...] = jnp.zeros_like(gc)
            spacc[...] = jnp.zeros_like(spacc)
            dg_ref[...] = jnp.zeros_like(dg_ref)
            ai_ref[...] = jnp.zeros_like(ai_ref)
            ao_ref[...] = jnp.zeros_like(ao_ref)

        def accum(r, val):
            sg_ref[r:r + 1, :] += jnp.sum(val, axis=0, keepdims=True)

        x1hat, x1rstd = _rms_stats(x1_ref[...])
        gain = g_ref[...]
        hb = (x1hat * gain).astype(BF16)

        def in_proj_bwd(lo, hi):
            dzb = dz_ref[:, lo:hi]
            ai_ref[lo:hi, :] += _dot(dzb, hb, TN)
            return _dot(dzb, wi_ref[lo:hi, :], NN)

        dxb = dx_ref[...].astype(BF16)
        ao_ref[...] += _dot(mix_ref[...], dxb, TN)
        dmix = _dot(dxb, wo_ref[...], NT)
        d_u = dmix[:, 0:W]
        d_yr = dmix[:, W:2 * W]

        xh, rs = _layernorm_stats(u1_ref[...])
        ln_g = vec_ref[V_LNG:V_LNG + 1, :]
        u2 = xh * ln_g + vec_ref[V_LNB:V_LNB + 1, :]
        s2 = _sigmoid(u2)
        d_u2 = d_u * (s2 * (1.0 + u2 * (1.0 - s2)))
        accum(G_LNG, d_u2 * xh)
        accum(G_LNB, d_u2)
        d_xh = d_u2 * ln_g
        d_u1 = rs * (d_xh - jnp.mean(d_xh, axis=-1, keepdims=True)
                     - xh * jnp.mean(d_xh * xh, axis=-1, keepdims=True))
        accum(G_CB, d_u1)
        halo_on = jnp.where(first, 0.0, 1.0)
        u0buf[0:CONV_HALO, :] = halo_on * (zh_ref[:, 0:W] * _sigmoid(zh_ref[:, W:2 * W]))
        cv = z_ref[:, 0:W]
        sgc = _sigmoid(z_ref[:, W:2 * W])
        u0buf[CONV_HALO:CONV_HALO + tm, :] = cv * sgc
        du1buf[0:tm, :] = d_u1
        base = CONV_HALO - (CONV_K - 1)
        for off, win in _row_windows(u0buf, tm, range(base, base + CONV_K)):
            accum(G_CW + off - base, d_u1 * win)
        d_u0 = jnp.zeros((tm, W), F32)
        for off, win in _row_windows(du1buf, tm, range(0, CONV_K)):
            d_u0 = d_u0 + cw_ref[CONV_K - 1 - off:CONV_K - off, :] * win
        du1buf[tm:tm + CONV_HALO, :] = du1buf[0:CONV_HALO, :]
        dz_ref[:, 0:W] = (d_u0 * sgc).astype(BF16)
        dz_ref[:, W:2 * W] = (d_u0 * cv * (sgc * (1.0 - sgc))).astype(BF16)
        dh = in_proj_bwd(0, 2 * W)

        xrv = xr_ref[...]
        xrb, r, ig, sp, a, mult = _lru_gates(xrv, bda_ref, bdx_ref, vec_ref)
        h = h_ref[...]
        gl, dgl = _gelu_parts(z_ref[:, 3 * W:4 * W])
        dz_ref[:, 3 * W:4 * W] = (d_yr * h * dgl).astype(BF16)
        dh = dh + in_proj_bwd(3 * W, 4 * W)
        a_next = jnp.where(row == tm - 1, 1.0, pltpu.roll(a, tm - 1, 0))
        g_first = _scan_rows(a_next, d_yr * gl, gc[0:1, :], gbuf, reverse=True)
        g = gbuf[...]
        gc[0:1, :] = a[0:1, :] * g_first
        hprev = jnp.where(row == 0, halo_on * hh_ref[LRU_HALO - 1:LRU_HALO, :], pltpu.roll(h, 1, 0))
        d_log_a = (g * hprev) * a - (g * ig * xrv) * (a * a) / mult
        d_ig = g * mult * xrv
        d_xr = g * mult * ig
        spacc[0:1, :] += jnp.sum(d_log_a * r, axis=0, keepdims=True)
        d_pa32 = (d_log_a * (-LRU_C * sp)) * (r * (1.0 - r))
        d_px32 = d_ig * (ig * (1.0 - ig))
        accum(G_BA, d_pa32)
        accum(G_BX, d_px32)
        d_pa = d_pa32.astype(BF16)
        d_px = d_px32.astype(BF16)
        d_xr = d_xr + _dot(d_pa, bda_ref[...], NT) + _dot(d_px, bdx_ref[...], NT)
        dbda_ref[...] += _dot(xrb, d_pa, TN)
        dbdx_ref[...] += _dot(xrb, d_px, TN)
        accum(G_LCB, d_xr)
        rxbuf[0:LRU_HALO, :] = halo_on * zh_ref[CONV_HALO - LRU_HALO:CONV_HALO, 2 * W:3 * W]
        rxbuf[LRU_HALO:LRU_HALO + tm, :] = z_ref[:, 2 * W:3 * W]
        dxrbuf[0:tm, :] = d_xr
        d_rx = jnp.zeros((tm, W), F32)
        for k in range(LRU_K):
            off = LRU_HALO - (LRU_K - 1) + k
            accum(G_LW + k, d_xr * rxbuf[off:off + tm, :])
            d_rx = d_rx + lw_ref[k:k + 1, :] * dxrbuf[LRU_K - 1 - k:LRU_K - 1 - k + tm, :]
        dxrbuf[tm:tm + LRU_HALO, :] = dxrbuf[0:LRU_HALO, :]
        dz_ref[:, 2 * W:3 * W] = d_rx.astype(BF16)
        dh = dh + in_proj_bwd(2 * W, 3 * W)

        dx, dg = _rms_bwd(x1hat, x1rstd, gain, dh)
        dx1 = dx_ref[...] + dx
        dx1_ref[...] = dx1
        dob_ref[...] = (FFN_RES * dx1).astype(BF16)
        dg_ref[...] += dg

        @pl.when(first)
        def _():
            lam = vec_ref[V_LAM:V_LAM + 1, :]
            sg_ref[G_LAM:G_LAM + 1, :] = LRU_C * _sigmoid(-lam) * spacc[0:1, :]
            dwi_ref[...] = ai_ref[...].astype(BF16)
            dwo_ref[...] = ao_ref[...].astype(BF16)

    full = lambda a: pl.BlockSpec(a.shape, lambda i: (0,) * a.ndim)
    tile = lambda n: pl.BlockSpec((tm, n), lambda i: (nt - 1 - i, 0))
    halo = lambda rows, n: pl.BlockSpec(
        (rows, n), lambda i: (jnp.maximum((nt - 1 - i) * (tm // rows) - 1, 0), 0))
    const = lambda r, c: pl.BlockSpec((r, c), lambda i: (0, 0))
    return _call(
        body, name=name, grid=(nt,),
        in_specs=[tile(D), tile(4 * W), halo(CONV_HALO, 4 * W), tile(W), tile(W), tile(W), halo(LRU_HALO, W),
                  full(w_out), full(bda), full(bdx), full(cw), full(lw), full(vec),
                  tile(D), tile(2 * W), full(g), full(w_in_t)],
        out_specs=[tile(D), const(G_ROWS, W), const(W, W), const(W, W),
                   tile(D), const(1, D), const(4 * W, D), const(2 * W, D)],
        out_shape=[jax.ShapeDtypeStruct((T, D), F32), jax.ShapeDtypeStruct((G_ROWS, W), F32),
                   jax.ShapeDtypeStruct((W, W), F32), jax.ShapeDtypeStruct((W, W), F32),
                   jax.ShapeDtypeStruct((T, D), BF16), jax.ShapeDtypeStruct((1, D), F32),
                   jax.ShapeDtypeStruct((4 * W, D), BF16), jax.ShapeDtypeStruct((2 * W, D), BF16)],
        scratch_shapes=[pltpu.VMEM((tm + CONV_HALO, W), F32), pltpu.VMEM((tm + CONV_HALO, W), F32),
                        pltpu.VMEM((tm + LRU_HALO, W), F32), pltpu.VMEM((tm + LRU_HALO, W), F32),
                        pltpu.VMEM((tm, W), F32), pltpu.VMEM((8, W), F32), pltpu.VMEM((8, W), F32),
                        pltpu.VMEM((tm, 4 * W), BF16), pltpu.VMEM((4 * W, D), F32), pltpu.VMEM((2 * W, D), F32)],
        vmem_mib=60, args=(dx2, z, z, u1, xr, hst, hst, w_out, bda, bdx, cw, lw, vec, x1, mix, g, w_in_t),
        rider=rider)


def _pair_add(full, recv, name):
    K, _, _, rows, D = full.shape

    def body(c_ref, a_ref, b_ref, o_ref):
        o_ref[...] = (a_ref[...].astype(F32) + b_ref[...].astype(F32)).astype(BF16)

    c = lax.axis_index("c").astype(jnp.int32).reshape((1,))
    return _call(
        body, name=name, grid=(K, N_CHIP), num_scalar_prefetch=1,
        in_specs=[pl.BlockSpec((None, None, None, rows, D), lambda k, q, c_ref: (k, q, c_ref[0], 0, 0)),
                  pl.BlockSpec((None, None, rows, D), lambda k, q, c_ref: (k, q, 0, 0))],
        out_specs=pl.BlockSpec((None, None, rows, D), lambda k, q, c_ref: (k, q, 0, 0)),
        out_shape=jax.ShapeDtypeStruct(recv.shape, BF16),
        scratch_shapes=[], vmem_mib=16, args=(c, full, recv))


def _adamw_update(wv, gv, mv, vv):
    m2 = ADAM_B1 * mv + (1.0 - ADAM_B1) * gv
    v2 = ADAM_B2 * vv + (1.0 - ADAM_B2) * (gv * gv)
    m_hat = m2 / (1.0 - ADAM_B1 ** ADAM_STEP)
    v_hat = v2 / (1.0 - ADAM_B2 ** ADAM_STEP)
    return -ADAM_LR * (m_hat / (jnp.sqrt(v_hat) + ADAM_EPS) + ADAM_WD * wv), m2, v2


def _finish(parts, k, w, m, v, transpose, name):
    _, n_parts, rows, D = parts.shape

    def body(p_ref, w_ref, m_ref, v_ref, g_ref, d_ref, mo_ref, vo_ref):
        acc = p_ref[0].astype(F32)
        for q in range(1, n_parts):
            acc = acc + p_ref[q].astype(F32)
        gv = acc.T if transpose else acc
        g_ref[...] = gv
        d_ref[...], mo_ref[...], vo_ref[...] = _adamw_update(w_ref[...], gv, m_ref[...], v_ref[...])

    whole = pl.BlockSpec(w.shape, lambda i: (0, 0))
    return _call(
        body, name=name, grid=(1,),
        in_specs=[pl.BlockSpec((None, n_parts, rows, D), lambda i: (k, 0, 0, 0)), whole, whole, whole],
        out_specs=[whole] * 4, out_shape=[pltpu.HBM(w.shape, F32)] * 4,
        scratch_shapes=[], vmem_mib=40, args=(parts, w, m, v))


def _adamw_each(ws, gs, ms, vs, name):
    n = len(ws)

    def body(*refs):
        w_refs, g_refs, m_refs, v_refs, outs = refs[:n], refs[n:2 * n], refs[2 * n:3 * n], refs[3 * n:4 * n], refs[4 * n:]
        for k in range(n):
            outs[k][...], outs[n + k][...], outs[2 * n + k][...] = _adamw_update(
                w_refs[k][...], g_refs[k][...], m_refs[k][...], v_refs[k][...])

    shapes = [jax.ShapeDtypeStruct(w.shape, F32) for w in ws]
    return pl.pallas_call(
        body, name=name,
        in_specs=[VMEM_SPEC] * (4 * n), out_specs=[VMEM_SPEC] * (3 * n), out_shape=shapes * 3,
        compiler_params=pltpu.CompilerParams(vmem_limit_bytes=32 * MIB),
    )(*ws, *gs, *ms, *vs)


def _block_diag(w):
    h, d, _ = w.shape
    onto = jnp.eye(h, dtype=w.dtype)
    return (w[:, :, None, :] * onto[:, None, :, None]).reshape(h * d, h * d)


def _diag_blocks(m, h):
    d = m.shape[0] // h
    onto = jnp.eye(h, dtype=m.dtype)
    return (m.reshape(h, d, h, d) * onto[:, None, :, None]).sum(axis=2)


def _reduce_level1(full, tag):
    got = _run_comm(_sibling_comm(full), "rs_sibling_" + tag)
    return [_pair_add(a, b, "rs_pair_add_%s%d" % (tag, n)) for n, (a, b) in enumerate(zip(full, got))]


def kernel(x, ffn1_norm, ffn1_w_gate, ffn1_w_up, ffn1_w_down, mix_norm, w_in, conv_dw, conv_dw_bias, conv_ln_g, conv_ln_b, lru_conv_w, lru_conv_b, lru_w_a, lru_b_a, lru_w_x, lru_b_x, lru_lambda, w_out, ffn2_norm, ffn2_w_gate, ffn2_w_up, ffn2_w_down, final_norm, loss_target, m_ffn1_norm, m_ffn1_w_gate, m_ffn1_w_up, m_ffn1_w_down, m_mix_norm, m_w_in, m_conv_dw, m_conv_dw_bias, m_conv_ln_g, m_conv_ln_b, m_lru_conv_w, m_lru_conv_b, m_lru_w_a, m_lru_b_a, m_lru_w_x, m_lru_b_x, m_lru_lambda, m_w_out, m_ffn2_norm, m_ffn2_w_gate, m_ffn2_w_up, m_ffn2_w_down, m_final_norm, v_ffn1_norm, v_ffn1_w_gate, v_ffn1_w_up, v_ffn1_w_down, v_mix_norm, v_w_in, v_conv_dw, v_conv_dw_bias, v_conv_ln_g, v_conv_ln_b, v_lru_conv_w, v_lru_conv_b, v_lru_w_a, v_lru_b_a, v_lru_w_x, v_lru_b_x, v_lru_lambda, v_w_out, v_ffn2_norm, v_ffn2_w_gate, v_ffn2_w_up, v_ffn2_w_down, v_final_norm):
    T, D = x.shape[1], x.shape[2]
    F = ffn1_w_down.shape[0] * N_DEV
    rf = ffn1_w_down.shape[0]
    ri = w_in.shape[1]
    ro = w_out.shape[0]
    W = conv_dw_bias.shape[0]
    wc = conv_dw.shape[1]
    H = lru_w_a.shape[0]
    xs = x.reshape(T, D)
    tgt = loss_target.reshape(T, D)
    tm_ffn = min(256, T)
    tm_fwd = min(512, T)
    cf = 256
    tm_w = min(1024, T)
    tm_w1 = min(2048, T)
    tm_mix = min(256, T)
    tf_w = F // 2
    row = lambda v: v.reshape(1, -1)
    by_owner = lambda a, rows: a.reshape(a.shape[0], N_CHIP, 2, rows, D)

    p3a, p3b, p_in, p_out = _prep_weights(
        (ffn1_w_gate.T, ffn1_w_up.T, ffn1_w_down), (ffn2_w_gate.T, ffn2_w_up.T, ffn2_w_down), w_in, w_out,
        "prep_weights")
    tile_rows = lambda a: jnp.pad(a, ((0, -a.shape[0] % SUBLANES), (0, 0)))
    p_cw = jnp.concatenate([tile_rows(conv_dw), tile_rows(lru_conv_w)], axis=0)
    lw_row = p_cw.shape[0] - SUBLANES
    stacked = lambda r, j: r.at[:, j]
    plain = lambda r, j: r.at[j]
    g3_shape = jax.ShapeDtypeStruct((3, N_DEV, rf, D), BF16)
    (g3a,) = _all_gather([p3a], [stacked], [g3_shape], "ag_ffn1")
    w3a = g3a.reshape(3, F, D)
    bda = _block_diag(lru_w_a).astype(BF16)
    bdx = _block_diag(lru_w_x).astype(BF16)
    vec = jnp.concatenate([tile_rows(v[None]) for v in
                           (conv_dw_bias, conv_ln_g, conv_ln_b, lru_conv_b, lru_b_a, lru_b_x, lru_lambda)], axis=0)

    gather_rest = _gather_comm(
        [p3b, p_in, p_out, p_cw], [stacked, plain, plain, plain],
        [g3_shape, jax.ShapeDtypeStruct((N_DEV, ri, D), BF16), jax.ShapeDtypeStruct((N_DEV, ro, D), BF16),
         jax.ShapeDtypeStruct((N_DEV,) + p_cw.shape, F32)],
        [(SIBLING,) + SAME_CORE, EVERYONE, EVERYONE, EVERYONE])
    (x1, h1, dau1, dag1, act1), (g3b_half, g_in, g_out, g_cw) = _ffn_fwd(
        xs, row(ffn1_norm), w3a, tm_fwd, cf, "ffn1_fwd", rider=gather_rest)
    w_in_t = g_in.reshape(N_DEV * ri, D)
    w_out_f = g_out.reshape(N_DEV * ro, D)
    cw_all = jnp.transpose(g_cw, (1, 0, 2)).reshape(p_cw.shape[0], N_DEV * wc)
    cw = cw_all[0:CONV_K]
    lw = cw_all[lw_row:lw_row + LRU_K]
    (x2, z, mix, u1, xr, hst), (g3b,) = _mix_core_fwd(
        x1, row(mix_norm), w_in_t, w_out_f, bda, bdx, cw, lw, vec, tm_mix, "mix_core_fwd",
        rider=_forward_comm([g3b_half], [stacked]))
    w3b = g3b.reshape(3, F, D)
    dx3, dob2, d_final_norm, loss_part, h3, dau2, dag2, act2 = _ffn_fwd(
        x2, row(ffn2_norm), w3b, tm_fwd, cf, "ffn2_fwd_loss", head=(row(final_norm), tgt))

    dx2, dgate2, dup2, d_ffn2_norm = _ffn_dgrad(dx3, x2, row(ffn2_norm), dau2, dag2, w3b, tm_ffn, cf, "ffn2_dgrad")
    (dw_gu2,) = _wgrad([dgate2, dup2], h3, tm_w, tf_w, "ffn2_wgrad_gu")
    (dw_d2,) = _wgrad([act2], dob2, tm_w1, tf_w, "ffn2_wgrad_d")
    by_device = lambda a: a.reshape(a.shape[0], N_DEV, rf, D)
    (dx1, sg, dbda, dbdx, dob1, d_mix_norm, dw_in_t, dw_out), parts_f2 = _mix_bwd(
        dx2, z, u1, xr, hst, x1, mix, row(mix_norm), w_in_t, w_out_f, bda, bdx, cw, lw, vec, tm_mix, "mix_bwd",
        rider=_chips_comm([by_device(dw_gu2), by_device(dw_d2)], every_device=True))
    small = [d_mix_norm, d_ffn2_norm, d_final_norm, sg, _diag_blocks(dbda, H).reshape(-1, D),
             _diag_blocks(dbdx, H).reshape(-1, D), loss_part]
    io = [dw_in_t.reshape(1, N_DEV, ri, D), dw_out.reshape(1, N_DEV, ro, D)]
    (dw_d1,), summed_and_parts = _wgrad(
        [act1], dob1, tm_w1, tf_w, "ffn1_wgrad_d",
        rider=_both(_small_sum_comm(small), _chips_comm(io, every_device=True)))
    summed, parts_io = summed_and_parts[:len(small)], summed_and_parts[len(small):]
    sums_d1 = _reduce_level1([by_owner(dw_d1, rf)], "d1")
    dx0, dgate1, dup1, d_ffn1_norm = _ffn_dgrad(dx1, xs, row(ffn1_norm), dau1, dag1, w3a, tm_ffn, cf, "ffn1_dgrad")
    (dw_g1,), (g_norm1, *parts_d1) = _wgrad(
        [dgate1], h1, tm_w1, tf_w, "ffn1_wgrad_g", rider=_both(_small_sum_comm([d_ffn1_norm]), _chips_comm(sums_d1)))
    sums_g1 = _reduce_level1([by_owner(dw_g1, rf)], "g1")
    (dw_u1,), parts_g1 = _wgrad([dup1], h1, tm_w1, tf_w, "ffn1_wgrad_u", rider=_chips_comm(sums_g1))
    sums_u1 = _reduce_level1([by_owner(dw_u1, rf)], "u1")
    parts_u1 = _run_comm(_chips_comm(sums_u1), "rs_chips_u1")

    g_norm_mix, g_norm2, g_norm_final, g_sg, g_w_a, g_w_x, g_loss = summed
    loss = g_loss[0, 0]
    me = 4 * lax.axis_index("x") + 2 * lax.axis_index("y") + lax.axis_index("c")
    chan = lambda full_g: lax.dynamic_slice_in_dim(full_g, me * wc, wc, axis=1)
    grads = {
        "ffn1_norm": g_norm1.reshape(D), "mix_norm": g_norm_mix.reshape(D), "ffn2_norm": g_norm2.reshape(D),
        "final_norm": g_norm_final.reshape(D),
        "conv_dw_bias": g_sg[G_CB], "conv_ln_g": g_sg[G_LNG], "conv_ln_b": g_sg[G_LNB],
        "lru_conv_b": g_sg[G_LCB], "lru_b_a": g_sg[G_BA], "lru_b_x": g_sg[G_BX], "lru_lambda": g_sg[G_LAM],
        "lru_w_a": g_w_a.reshape(lru_w_a.shape), "lru_w_x": g_w_x.reshape(lru_w_x.shape),
        "conv_dw": chan(g_sg[G_CW:G_CW + CONV_K]), "lru_conv_w": chan(g_sg[G_LW:G_LW + LRU_K]),
    }

    weights = dict(ffn1_norm=ffn1_norm, ffn1_w_gate=ffn1_w_gate, ffn1_w_up=ffn1_w_up, ffn1_w_down=ffn1_w_down, mix_norm=mix_norm, w_in=w_in, conv_dw=conv_dw, conv_dw_bias=conv_dw_bias, conv_ln_g=conv_ln_g, conv_ln_b=conv_ln_b, lru_conv_w=lru_conv_w, lru_conv_b=lru_conv_b, lru_w_a=lru_w_a, lru_b_a=lru_b_a, lru_w_x=lru_w_x, lru_b_x=lru_b_x, lru_lambda=lru_lambda, w_out=w_out, ffn2_norm=ffn2_norm, ffn2_w_gate=ffn2_w_gate, ffn2_w_up=ffn2_w_up, ffn2_w_down=ffn2_w_down, final_norm=final_norm)
    moment1 = dict(ffn1_norm=m_ffn1_norm, ffn1_w_gate=m_ffn1_w_gate, ffn1_w_up=m_ffn1_w_up, ffn1_w_down=m_ffn1_w_down, mix_norm=m_mix_norm, w_in=m_w_in, conv_dw=m_conv_dw, conv_dw_bias=m_conv_dw_bias, conv_ln_g=m_conv_ln_g, conv_ln_b=m_conv_ln_b, lru_conv_w=m_lru_conv_w, lru_conv_b=m_lru_conv_b, lru_w_a=m_lru_w_a, lru_b_a=m_lru_b_a, lru_w_x=m_lru_w_x, lru_b_x=m_lru_b_x, lru_lambda=m_lru_lambda, w_out=m_w_out, ffn2_norm=m_ffn2_norm, ffn2_w_gate=m_ffn2_w_gate, ffn2_w_up=m_ffn2_w_up, ffn2_w_down=m_ffn2_w_down, final_norm=m_final_norm)
    moment2 = dict(ffn1_norm=v_ffn1_norm, ffn1_w_gate=v_ffn1_w_gate, ffn1_w_up=v_ffn1_w_up, ffn1_w_down=v_ffn1_w_down, mix_norm=v_mix_norm, w_in=v_w_in, conv_dw=v_conv_dw, conv_dw_bias=v_conv_dw_bias, conv_ln_g=v_conv_ln_g, conv_ln_b=v_conv_ln_b, lru_conv_w=v_lru_conv_w, lru_conv_b=v_lru_conv_b, lru_w_a=v_lru_w_a, lru_b_a=v_lru_b_a, lru_w_x=v_lru_w_x, lru_b_x=v_lru_b_x, lru_lambda=v_lru_lambda, w_out=v_w_out, ffn2_norm=v_ffn2_norm, ffn2_w_gate=v_ffn2_w_gate, ffn2_w_up=v_ffn2_w_up, ffn2_w_down=v_ffn2_w_down, final_norm=v_final_norm)
    order = list(weights)
    big = {"ffn1_w_gate": (parts_g1[0], 0, True), "ffn1_w_up": (parts_u1[0], 0, True),
           "ffn1_w_down": (parts_d1[0], 0, False), "w_in": (parts_io[0], 0, True), "w_out": (parts_io[1], 0, False),
           "ffn2_w_gate": (parts_f2[0], 0, True), "ffn2_w_up": (parts_f2[0], 1, True),
           "ffn2_w_down": (parts_f2[1], 0, False)}
    delta, new_m, new_v = {}, {}, {}
    for n, (parts, k, d_major) in big.items():
        operands = weights[n], moment1[n], moment2[n]
        if d_major and n != "w_in":
            results = _finish(parts, k, *[a.T for a in operands], False, "finish_" + n)
            grads[n], delta[n], new_m[n], new_v[n] = [r.T for r in results]
        else:
            grads[n], delta[n], new_m[n], new_v[n] = _finish(parts, k, *operands, d_major, "finish_" + n)
    rest = [n for n in order if n not in big]
    updates = _adamw_each([weights[n] for n in rest], [grads[n] for n in rest], [moment1[n] for n in rest],
                          [moment2[n] for n in rest], "adamw_small")
    for k, n in enumerate(rest):
        delta[n], new_m[n], new_v[n] = updates[k], updates[len(rest) + k], updates[2 * len(rest) + k]

    return (loss, dx0.reshape(x.shape), *[grads[n] for n in order], *[delta[n] for n in order],
            *[new_m[n] for n in order], *[new_v[n] for n in order])
```

```python
import functools
import math

import jax
import jax.numpy as jnp
from jax import lax
from jax.experimental import pallas as pl
from jax.experimental.pallas import tpu as pltpu

F32 = jnp.float32
BF16 = jnp.bfloat16
MESH = pl.DeviceIdType.MESH

N_DEV = 8
N_CHIP = 4
SUBLANES = 8
RMS_EPS = 1e-6
LN_EPS = 1e-5
LRU_C = 8.0
CONV_K = 31
LRU_K = 4
CONV_HALO = 32
LRU_HALO = 8
FFN_RES = 0.5
ADAM_LR, ADAM_B1, ADAM_B2, ADAM_EPS, ADAM_WD, ADAM_STEP = 0.001, 0.9, 0.999, 1e-08, 0.01, 10
GELU_K = math.sqrt(2.0 / math.pi)
GELU_C = 0.044715

MIB = 1024 * 1024
NT = (((1,), (1,)), ((), ()))
NN = (((1,), (0,)), ((), ()))
TN = (((0,), (0,)), ((), ()))

V_CB, V_LNG, V_LNB, V_LCB, V_BA, V_BX, V_LAM = range(0, 7 * SUBLANES, SUBLANES)
G_CW = 0
G_CB, G_LNG, G_LNB = 31, 32, 33
G_LW = 34
G_LCB, G_BA, G_BX, G_LAM = 38, 39, 40, 41
G_ROWS = 48

HBM_SPEC = pl.BlockSpec(memory_space=pltpu.HBM)
VMEM_SPEC = pl.BlockSpec(memory_space=pltpu.VMEM)


def _dot(a, b, dims):
    return lax.dot_general(a, b, dims, preferred_element_type=F32)


def _sigmoid(x):
    return 1.0 / (1.0 + jnp.exp(-x))


def _gelu_parts(x):
    x2 = x * x
    th = jnp.tanh(GELU_K * x * (1.0 + GELU_C * x2))
    gl = 0.5 * x * (1.0 + th)
    dgl = 0.5 * (1.0 + th) + 0.5 * x * (1.0 - th * th) * GELU_K * (1.0 + 3.0 * GELU_C * x2)
    return gl, dgl


def _neg_expm1(y):
    series = -y * (1.0 + y * (1.0 / 2) * (1.0 + y * (1.0 / 3) * (1.0 + y * (1.0 / 4) * (1.0 + y * (1.0 / 5) * (1.0 + y * (1.0 / 6))))))
    return jnp.where(y > -0.25, series, 1.0 - jnp.exp(y))


def _softplus_neg(lam):
    t = -lam
    e = jnp.exp(-jnp.abs(t))
    s = 1.0 + e
    log1p_e = jnp.log(s) - ((s - 1.0) - e) / s
    return jnp.maximum(t, 0.0) + log1p_e


def _rms_stats(xv):
    rstd = lax.rsqrt(jnp.mean(xv * xv, axis=-1, keepdims=True) + RMS_EPS)
    return xv * rstd, rstd


def _rms_bwd(xhat, rstd, g, dh):
    dxhat = dh * g
    dx = rstd * (dxhat - xhat * jnp.mean(dxhat * xhat, axis=-1, keepdims=True))
    return dx, jnp.sum(dh * xhat, axis=0, keepdims=True)


def _row_windows(buf_ref, n_rows, offsets):
    total = buf_ref.shape[0]
    full = buf_ref[...]
    for b in range(SUBLANES):
        offs = [o for o in offsets if o % SUBLANES == b]
        if not offs:
            continue
        assert max(offs) + n_rows <= total
        moved = full if b == 0 else pltpu.roll(full, total - b, 0)
        for o in offs:
            yield o, moved[o - b:o - b + n_rows, :]


def _scan_rows(av, bv, edge, out_ref, reverse=False):
    tm, W = av.shape
    sub = lax.broadcasted_iota(jnp.int32, (tm, W), 0) % SUBLANES
    s = 1
    while s < SUBLANES:
        keep = (sub < SUBLANES - s) if reverse else (sub >= s)
        shift = tm - s if reverse else s
        bv = jnp.where(keep, av * pltpu.roll(bv, shift, 0) + bv, bv)
        av = jnp.where(keep, av * pltpu.roll(av, shift, 0), av)
        s *= 2
    starts = range(0, tm, SUBLANES)
    for r0 in (reversed(starts) if reverse else starts):
        group = av[r0:r0 + SUBLANES, :] * edge + bv[r0:r0 + SUBLANES, :]
        out_ref[r0:r0 + SUBLANES, :] = group
        edge = group[0:1, :] if reverse else group[SUBLANES - 1:SUBLANES, :]
    return edge


class _Comm:
    def __init__(self, arrays, in_specs, out_shapes, out_specs, scratch, start, wait, aliases=None):
        self.arrays, self.in_specs = list(arrays), list(in_specs)
        self.out_shapes, self.out_specs = list(out_shapes), list(out_specs)
        self.scratch, self.start, self.wait = list(scratch), start, wait
        self.aliases = dict(aliases or {})


def _in_hbm(a):
    return pltpu.with_memory_space_constraint(a, pltpu.HBM)


def _operands(comm):
    return [a if spec is VMEM_SPEC else _in_hbm(a) for a, spec in zip(comm.arrays, comm.in_specs)]


def _call(body, *, name, grid, in_specs, out_specs, out_shape, scratch_shapes, vmem_mib, args, rider=None,
          num_scalar_prefetch=0):
    params = pltpu.CompilerParams(dimension_semantics=("arbitrary",) * len(grid), vmem_limit_bytes=vmem_mib * MIB)
    args = [a if k < num_scalar_prefetch else _in_hbm(a) for k, a in enumerate(args)]
    if rider is None:
        return pl.pallas_call(
            body, name=name,
            grid_spec=pltpu.PrefetchScalarGridSpec(
                num_scalar_prefetch=num_scalar_prefetch, grid=grid, in_specs=in_specs, out_specs=out_specs,
                scratch_shapes=scratch_shapes),
            out_shape=out_shape, compiler_params=params)(*args)
    assert num_scalar_prefetch == 0
    n_in, n_out, n_scr = len(in_specs), len(out_specs), len(scratch_shapes)
    r_in, r_out = len(rider.arrays), len(rider.out_shapes)
    n_axes = len(grid)

    def carried(*refs):
        pos = [0]

        def take(n):
            pos[0] += n
            return refs[pos[0] - n:pos[0]]

        ins, r_ins, outs, r_outs, scr, r_scr = take(n_in), take(r_in), take(n_out), take(r_out), take(n_scr), take(len(rider.scratch))
        first = pl.program_id(0) == 0
        last = pl.program_id(0) == grid[0] - 1
        for ax in range(1, n_axes):
            first = first & (pl.program_id(ax) == 0)
            last = last & (pl.program_id(ax) == grid[ax] - 1)

        @pl.when(first)
        def _():
            rider.start(r_ins, r_outs, r_scr)

        body(*ins, *outs, *scr)

        @pl.when(last)
        def _():
            rider.wait(r_ins, r_outs, r_scr)

    res = pl.pallas_call(
        carried, name=name,
        grid=grid,
        in_specs=list(in_specs) + rider.in_specs,
        out_specs=list(out_specs) + rider.out_specs,
        out_shape=list(out_shape) + rider.out_shapes,
        scratch_shapes=list(scratch_shapes) + rider.scratch,
        input_output_aliases={n_in + i: n_out + o for i, o in rider.aliases.items()},
        compiler_params=params)(*args, *_operands(rider))
    return res[:n_out], res[n_out:]


def _run_comm(comm, name):
    n_in, n_out = len(comm.arrays), len(comm.out_shapes)

    def body(*refs):
        ins, outs, scr = refs[:n_in], refs[n_in:n_in + n_out], refs[n_in + n_out:]
        comm.start(ins, outs, scr)
        comm.wait(ins, outs, scr)

    return pl.pallas_call(
        body, name=name,
        in_specs=comm.in_specs, out_specs=comm.out_specs, out_shape=comm.out_shapes,
        scratch_shapes=comm.scratch, input_output_aliases=comm.aliases,
        compiler_params=pltpu.CompilerParams(vmem_limit_bytes=24 * MIB))(*_operands(comm))


def _both(a, b):
    ni, no, ns = len(a.arrays), len(a.out_shapes), len(a.scratch)

    def start(ins, outs, scr):
        a.start(ins[:ni], outs[:no], scr[:ns])
        b.start(ins[ni:], outs[no:], scr[ns:])

    def wait(ins, outs, scr):
        a.wait(ins[:ni], outs[:no], scr[:ns])
        b.wait(ins[ni:], outs[no:], scr[ns:])

    aliases = dict(a.aliases)
    aliases.update({ni + i: no + o for i, o in b.aliases.items()})
    return _Comm(a.arrays + b.arrays, a.in_specs + b.in_specs, a.out_shapes + b.out_shapes,
                 a.out_specs + b.out_specs, a.scratch + b.scratch, start, wait, aliases)


def _place():
    return lax.axis_index("x"), lax.axis_index("y"), lax.axis_index("c")


def _peer(k):
    x, y, c = _place()
    px, py, pc = x ^ ((k >> 2) & 1), y ^ ((k >> 1) & 1), c ^ (k & 1)
    return (px, py, pc), 4 * px + 2 * py + pc


SIBLING = 1
SAME_CORE = (2, 4, 6)
EVERYONE = tuple(range(1, N_DEV))


def _gather_comm(shards, views, out_shapes, relations):
    na = len(shards)

    def copies(ins, outs, scr):
        send_sems, recv_sems, _ = scr
        _, me = _peer(0)
        out = []
        for a in range(na):
            for k in relations[a]:
                peer, theirs = _peer(k)
                send = functools.partial(
                    pltpu.make_async_remote_copy,
                    src_ref=ins[a], dst_ref=views[a](outs[a], me),
                    send_sem=send_sems.at[7 * a + k - 1], recv_sem=recv_sems.at[7 * a + k - 1],
                    device_id=peer, device_id_type=MESH)
                recv = functools.partial(
                    pltpu.make_async_remote_copy,
                    src_ref=ins[a], dst_ref=views[a](outs[a], theirs),
                    send_sem=send_sems.at[7 * a + k - 1], recv_sem=recv_sems.at[7 * a + k - 1],
                    device_id=peer, device_id_type=MESH)
                out.append((send, recv))
        return out

    def local(ins, outs, scr):
        _, me = _peer(0)
        return [pltpu.make_async_copy(ins[a], views[a](outs[a], me), scr[2].at[a]) for a in range(na)]

    def start(ins, outs, scr):
        for cp in local(ins, outs, scr):
            cp.start()
        for send, _ in copies(ins, outs, scr):
            send().start()

    def wait(ins, outs, scr):
        for _, recv in copies(ins, outs, scr):
            recv().wait_recv()
        for send, _ in copies(ins, outs, scr):
            send().wait_send()
        for cp in local(ins, outs, scr):
            cp.wait()

    return _Comm(shards, [HBM_SPEC] * na, out_shapes, [HBM_SPEC] * na,
                 [pltpu.SemaphoreType.DMA((7 * na,)), pltpu.SemaphoreType.DMA((7 * na,)),
                  pltpu.SemaphoreType.DMA((na,))], start, wait)


def _forward_comm(gathered, views):
    na = len(gathered)
    shapes = [jax.ShapeDtypeStruct(g.shape, g.dtype) for g in gathered]

    def copies(outs, scr):
        send_sems, recv_sems = scr
        sibling, _ = _peer(SIBLING)
        out = []
        for a in range(na):
            for n, k in enumerate(SAME_CORE):
                _, mine = _peer(k)
                _, theirs = _peer(k ^ SIBLING)
                send = functools.partial(
                    pltpu.make_async_remote_copy,
                    src_ref=views[a](outs[a], mine), dst_ref=views[a](outs[a], mine),
                    send_sem=send_sems.at[3 * a + n], recv_sem=recv_sems.at[3 * a + n],
                    device_id=sibling, device_id_type=MESH)
                recv = functools.partial(
                    pltpu.make_async_remote_copy,
                    src_ref=views[a](outs[a], mine), dst_ref=views[a](outs[a], theirs),
                    send_sem=send_sems.at[3 * a + n], recv_sem=recv_sems.at[3 * a + n],
                    device_id=sibling, device_id_type=MESH)
                out.append((send, recv))
        return out

    def start(ins, outs, scr):
        for send, _ in copies(outs, scr):
            send().start()

    def wait(ins, outs, scr):
        for _, recv in copies(outs, scr):
            recv().wait_recv()
        for send, _ in copies(outs, scr):
            send().wait_send()

    return _Comm(gathered, [HBM_SPEC] * na, shapes, [HBM_SPEC] * na,
                 [pltpu.SemaphoreType.DMA((3 * na,)), pltpu.SemaphoreType.DMA((3 * na,))], start, wait,
                 aliases={a: a for a in range(na)})


Y_NEIGHBOUR, X_NEIGHBOUR, DIAGONAL = SAME_CORE


def _all_gather(shards, views, out_shapes, name):
    na = len(shards)
    near = (SIBLING, Y_NEIGHBOUR, X_NEIGHBOUR)
    level1 = _gather_comm(shards, views, out_shapes, [near] * na)

    def body(*refs):
        ins, outs = refs[:na], refs[na:2 * na]
        send_sems, recv_sems, local_sems, fwd_send, fwd_recv, relay_send, relay_recv = refs[2 * na:]
        sibling, _ = _peer(SIBLING)
        c = lax.axis_index("c")
        level1.start(ins, outs, (send_sems, recv_sems, local_sems))

        def block_copy(a, block, to, send_sem, recv_sem):
            return pltpu.make_async_remote_copy(
                src_ref=views[a](outs[a], block), dst_ref=views[a](outs[a], block),
                send_sem=send_sem, recv_sem=recv_sem, device_id=to, device_id_type=MESH)

        def to_sibling(a, n, k):
            _, mine = _peer(k)
            _, theirs = _peer(k ^ SIBLING)
            fwd = block_copy(a, mine, sibling, fwd_send.at[3 * a + n], fwd_recv.at[3 * a + n])
            fwd.start()
            return fwd, block_copy(a, theirs, sibling, fwd_send.at[3 * a + n], fwd_recv.at[3 * a + n])

        passed, landing = [], []
        for a in range(na):
            for n, k in enumerate((Y_NEIGHBOUR, X_NEIGHBOUR)):
                peer, origin = _peer(k)
                pltpu.make_async_remote_copy(
                    src_ref=ins[a], dst_ref=views[a](outs[a], origin),
                    send_sem=send_sems.at[7 * a + k - 1], recv_sem=recv_sems.at[7 * a + k - 1],
                    device_id=peer, device_id_type=MESH).wait_recv()

                @pl.when(c == (0 if k == X_NEIGHBOUR else 1))
                def _():
                    other, _ = _peer(DIAGONAL ^ k)
                    block_copy(a, origin, other, relay_send.at[a], relay_recv.at[a]).start()

                fwd, lands = to_sibling(a, n, k)
                passed.append(fwd)
                landing.append(lands)
            _, far = _peer(DIAGONAL)
            block_copy(a, far, sibling, relay_send.at[a], relay_recv.at[a]).wait_recv()
            fwd, lands = to_sibling(a, 2, DIAGONAL)
            passed.append(fwd)
            landing.append(lands)
        for a in range(na):
            _, theirs = _peer(SIBLING)
            pltpu.make_async_remote_copy(
                src_ref=ins[a], dst_ref=views[a](outs[a], theirs),
                send_sem=send_sems.at[7 * a + SIBLING - 1], recv_sem=recv_sems.at[7 * a + SIBLING - 1],
                device_id=sibling, device_id_type=MESH).wait_recv()
        for cp in landing:
            cp.wait_recv()
        for cp in passed:
            cp.wait_send()
        _, me = _peer(0)
        for a in range(na):
            block_copy(a, me, sibling, relay_send.at[a], relay_recv.at[a]).wait_send()
            for k in near:
                peer, _ = _peer(k)
                pltpu.make_async_remote_copy(
                    src_ref=ins[a], dst_ref=views[a](outs[a], me),
                    send_sem=send_sems.at[7 * a + k - 1], recv_sem=recv_sems.at[7 * a + k - 1],
                    device_id=peer, device_id_type=MESH).wait_send()
            pltpu.make_async_copy(ins[a], views[a](outs[a], me), local_sems.at[a]).wait()

    return pl.pallas_call(
        body, name=name,
        in_specs=[HBM_SPEC] * na, out_specs=[HBM_SPEC] * na, out_shape=out_shapes,
        scratch_shapes=level1.scratch + [pltpu.SemaphoreType.DMA((3 * na,)), pltpu.SemaphoreType.DMA((3 * na,)),
                                         pltpu.SemaphoreType.DMA((na,)), pltpu.SemaphoreType.DMA((na,))],
    )(*[_in_hbm(s) for s in shards])


def _sibling_comm(grads):
    na = len(grads)
    shapes = [jax.ShapeDtypeStruct(g.shape[:2] + g.shape[3:], g.dtype) for g in grads]

    def copies(ins, outs, scr):
        x, y, c = _place()
        return [pltpu.make_async_remote_copy(
            src_ref=ins[a].at[:, :, 1 - c], dst_ref=outs[a],
            send_sem=scr[0].at[a], recv_sem=scr[1].at[a],
            device_id=(x, y, 1 - c), device_id_type=MESH) for a in range(na)]

    def start(ins, outs, scr):
        for cp in copies(ins, outs, scr):
            cp.start()

    def wait(ins, outs, scr):
        for cp in copies(ins, outs, scr):
            cp.wait()

    return _Comm(grads, [HBM_SPEC] * na, shapes, [HBM_SPEC] * na,
                 [pltpu.SemaphoreType.DMA((na,)), pltpu.SemaphoreType.DMA((na,))], start, wait)


def _chips_comm(sums, every_device=False):
    na = len(sums)
    shapes = [jax.ShapeDtypeStruct(s.shape, s.dtype) for s in sums]
    relations = EVERYONE if every_device else SAME_CORE
    nr = len(relations)

    def block(px, py, pc):
        return 4 * px + 2 * py + pc if every_device else 2 * px + py

    def copies(ins, outs, scr):
        mine = block(*_place())
        out = []
        for a in range(na):
            for n, k in enumerate(relations):
                peer, _ = _peer(k)
                theirs = block(*peer)
                send = functools.partial(
                    pltpu.make_async_remote_copy,
                    src_ref=ins[a].at[:, theirs], dst_ref=outs[a].at[:, mine],
                    send_sem=scr[0].at[nr * a + n], recv_sem=scr[1].at[nr * a + n],
                    device_id=peer, device_id_type=MESH)
                recv = functools.partial(
                    pltpu.make_async_remote_copy,
                    src_ref=ins[a].at[:, mine], dst_ref=outs[a].at[:, theirs],
                    send_sem=scr[0].at[nr * a + n], recv_sem=scr[1].at[nr * a + n],
                    device_id=peer, device_id_type=MESH)
                out.append((send, recv))
        return out

    def local(ins, outs, scr):
        mine = block(*_place())
        return [pltpu.make_async_copy(ins[a].at[:, mine], outs[a].at[:, mine], scr[2].at[a]) for a in range(na)]

    def start(ins, outs, scr):
        for cp in local(ins, outs, scr):
            cp.start()
        for send, _ in copies(ins, outs, scr):
            send().start()

    def wait(ins, outs, scr):
        for _, recv in copies(ins, outs, scr):
            recv().wait_recv()
        for send, _ in copies(ins, outs, scr):
            send().wait_send()
        for cp in local(ins, outs, scr):
            cp.wait()

    return _Comm(sums, [HBM_SPEC] * na, shapes, [HBM_SPEC] * na,
                 [pltpu.SemaphoreType.DMA((nr * na,)), pltpu.SemaphoreType.DMA((nr * na,)),
                  pltpu.SemaphoreType.DMA((na,))], start, wait)


def _small_sum_comm(arrays):
    na = len(arrays)

    def copies(ins, scr):
        bufs, send_sems, recv_sems = scr[:na], scr[na], scr[na + 1]
        _, me = _peer(0)
        out = []
        for a in range(na):
            for k in EVERYONE:
                peer, theirs = _peer(k)
                sems = dict(send_sem=send_sems.at[7 * a + k - 1], recv_sem=recv_sems.at[7 * a + k - 1])
                send = functools.partial(
                    pltpu.make_async_remote_copy,
                    src_ref=ins[a], dst_ref=bufs[a].at[me], device_id=peer, device_id_type=MESH, **sems)
                recv = functools.partial(
                    pltpu.make_async_remote_copy,
                    src_ref=ins[a], dst_ref=bufs[a].at[theirs], device_id=peer, device_id_type=MESH, **sems)
                out.append((send, recv))
        return out

    def start(ins, outs, scr):
        _, me = _peer(0)
        for a in range(na):
            scr[a][me] = ins[a][...]
        for send, _ in copies(ins, scr):
            send().start()

    def wait(ins, outs, scr):
        for _, recv in copies(ins, scr):
            recv().wait_recv()
        for send, _ in copies(ins, scr):
            send().wait_send()
        for a in range(na):
            acc = scr[a][0]
            for j in range(1, N_DEV):
                acc = acc + scr[a][j]
            outs[a][...] = acc

    return _Comm(arrays, [VMEM_SPEC] * na, [jax.ShapeDtypeStruct(s.shape, F32) for s in arrays], [VMEM_SPEC] * na,
                 [pltpu.VMEM((N_DEV,) + s.shape, F32) for s in arrays]
                 + [pltpu.SemaphoreType.DMA((7 * na,)), pltpu.SemaphoreType.DMA((7 * na,))], start, wait)


def _prep_weights(ffn1, ffn2, w_in, w_out, name):
    rf, D = ffn1[2].shape
    ri, ro = w_in.shape[1], w_out.shape[0]

    def body(g1, u1, d1, g2, u2, d2, wi, wo, p1_ref, p2_ref, pi_ref, po_ref):
        for p_ref, shards in ((p1_ref, (g1, u1, d1)), (p2_ref, (g2, u2, d2))):
            for k, shard in enumerate(shards):
                p_ref[k] = shard[...].astype(BF16)
        pi_ref[...] = wi[...].T.astype(BF16)
        po_ref[...] = wo[...].astype(BF16)

    args = (*ffn1, *ffn2, w_in, w_out)
    whole = lambda shape: pl.BlockSpec(shape, lambda i: (0,) * len(shape))
    out_shapes = [(3, rf, D), (3, rf, D), (ri, D), (ro, D)]
    return _call(
        body, name=name, grid=(1,),
        in_specs=[whole(a.shape) for a in args], out_specs=[whole(s) for s in out_shapes],
        out_shape=[jax.ShapeDtypeStruct(s, BF16) for s in out_shapes],
        scratch_shapes=[], vmem_mib=48, args=args)


def _load_weights(w_hbm, w_vmem, sem):
    @pl.when(pl.program_id(0) == 0)
    def _():
        copies = [pltpu.make_async_copy(w_hbm.at[k], w_vmem.at[k], sem.at[k]) for k in range(3)]
        for cp in copies:
            cp.start()
        for cp in copies:
            cp.wait()


def _ffn_fwd(x, g, w3, tm, cf, name, rider=None, head=None):
    T, D = x.shape
    F = w3.shape[1]
    n_head = 0 if head is None else 2

    def body(x_ref, g_ref, w_hbm, *refs):
        head_refs, refs = refs[:n_head], refs[n_head:]
        if head is None:
            (xo_ref, h_ref, dau_ref, dag_ref, act_ref, wv, sem) = refs
        else:
            (dx_ref, dob_ref, dgf_ref, loss_ref, h_ref, dau_ref, dag_ref, act_ref, wv, sem) = refs
        _load_weights(w_hbm, wv, sem)
        xhat, _ = _rms_stats(x_ref[...])
        hb = (xhat * g_ref[...]).astype(BF16)
        h_ref[...] = hb
        for lo in range(0, F, cf):
            gate = _dot(hb, wv[0, lo:lo + cf, :], NT)
            up = _dot(hb, wv[1, lo:lo + cf, :], NT)
            sig = _sigmoid(gate)
            silu = gate * sig
            dau_ref[:, lo:lo + cf] = silu.astype(BF16)
            dag_ref[:, lo:lo + cf] = (up * (sig * (1.0 + gate * (1.0 - sig)))).astype(BF16)
            act_ref[:, lo:lo + cf] = (silu * up).astype(BF16)
        x_out = x_ref[...] + FFN_RES * _dot(act_ref[...], wv[2], NN)
        if head is None:
            xo_ref[...] = x_out
            return

        @pl.when(pl.program_id(0) == 0)
        def _():
            dgf_ref[...] = jnp.zeros_like(dgf_ref)
            loss_ref[...] = jnp.zeros_like(loss_ref)

        gf_ref, tgt_ref = head_refs
        yhat, rstd = _rms_stats(x_out)
        gf = gf_ref[...]
        err = yhat * gf - tgt_ref[...]
        loss_ref[...] += (0.5 / D) * jnp.sum(err * err)
        dx, dgf = _rms_bwd(yhat, rstd, gf, err * (1.0 / D))
        dx_ref[...] = dx
        dob_ref[...] = (FFN_RES * dx).astype(BF16)
        dgf_ref[...] += dgf

    row = pl.BlockSpec((tm, D), lambda i: (i, 0))
    hid = pl.BlockSpec((tm, F), lambda i: (i, 0))
    vec = pl.BlockSpec((1, D), lambda i: (0, 0))
    row_f32, row_bf16 = jax.ShapeDtypeStruct((T, D), F32), jax.ShapeDtypeStruct((T, D), BF16)
    if head is None:
        first_specs, first_shapes = [row], [row_f32]
    else:
        first_specs = [row, row, vec, pl.BlockSpec((1, 128), lambda i: (0, 0))]
        first_shapes = [row_f32, row_bf16, jax.ShapeDtypeStruct((1, D), F32), jax.ShapeDtypeStruct((1, 128), F32)]
    return _call(
        body, name=name, grid=(T // tm,),
        in_specs=[row, vec, HBM_SPEC] + ([] if head is None else [vec, row]),
        out_specs=first_specs + [row, hid, hid, hid],
        out_shape=first_shapes + [row_bf16] + [jax.ShapeDtypeStruct((T, F), BF16)] * 3,
        scratch_shapes=[pltpu.VMEM((3, F, D), BF16), pltpu.SemaphoreType.DMA((3,))],
        vmem_mib=60, args=(x, g, w3) + (() if head is None else tuple(head)), rider=rider)


def _ffn_dgrad(dout, x, g, dau, dag, w3, tm, cf, name, rider=None):
    T, D = x.shape
    F = w3.shape[1]

    def body(do_ref, x_ref, g_ref, dau_ref, dag_ref, w_hbm, dx_ref, dgate_ref, dup_ref, dg_ref, wv, sem):
        _load_weights(w_hbm, wv, sem)

        @pl.when(pl.program_id(0) == 0)
        def _():
            dg_ref[...] = jnp.zeros_like(dg_ref)

        dob = (FFN_RES * do_ref[...]).astype(BF16)
        for lo in range(0, F, cf):
            dact = _dot(dob, wv[2, lo:lo + cf, :], NT)
            dup_ref[:, lo:lo + cf] = (dact * dau_ref[:, lo:lo + cf].astype(F32)).astype(BF16)
            dgate_ref[:, lo:lo + cf] = (dact * dag_ref[:, lo:lo + cf].astype(F32)).astype(BF16)
        dh = _dot(dgate_ref[...], wv[0], NN) + _dot(dup_ref[...], wv[1], NN)
        xhat, rstd = _rms_stats(x_ref[...])
        dx, dg = _rms_bwd(xhat, rstd, g_ref[...], dh)
        dx_ref[...] = do_ref[...] + dx
        dg_ref[...] += dg

    row = pl.BlockSpec((tm, D), lambda i: (i, 0))
    hid = pl.BlockSpec((tm, F), lambda i: (i, 0))
    vec = pl.BlockSpec((1, D), lambda i: (0, 0))
    return _call(
        body, name=name, grid=(T // tm,),
        in_specs=[row, row, vec, hid, hid, HBM_SPEC],
        out_specs=[row, hid, hid, vec],
        out_shape=[jax.ShapeDtypeStruct((T, D), F32), jax.ShapeDtypeStruct((T, F), BF16),
                   jax.ShapeDtypeStruct((T, F), BF16), jax.ShapeDtypeStruct((1, D), F32)],
        scratch_shapes=[pltpu.VMEM((3, F, D), BF16), pltpu.SemaphoreType.DMA((3,))],
        vmem_mib=52, args=(dout, x, g, dau, dag, w3), rider=rider)


def _wgrad(lhs, rhs, tm, tf, name, rider=None):
    T, F = lhs[0].shape
    D = rhs.shape[1]
    K = len(lhs)

    def body(*refs):
        lhs_refs, rhs_ref, dw_ref, accs = refs[:K], refs[K], refs[K + 1], refs[K + 2:]
        i = pl.program_id(1)

        @pl.when(i == 0)
        def _():
            for acc in accs:
                acc[...] = jnp.zeros_like(acc)

        rv = rhs_ref[...]
        for acc, lhs_ref in zip(accs, lhs_refs):
            acc[...] += _dot(lhs_ref[...], rv, TN)

        @pl.when(i == pl.num_programs(1) - 1)
        def _():
            for k, acc in enumerate(accs):
                dw_ref[k] = acc[...].astype(BF16)

    hid = pl.BlockSpec((tm, tf), lambda f, i: (i, f))
    return _call(
        body, name=name, grid=(F // tf, T // tm),
        in_specs=[hid] * K + [pl.BlockSpec((tm, D), lambda f, i: (i, 0))],
        out_specs=[pl.BlockSpec((K, tf, D), lambda f, i: (0, f, 0))],
        out_shape=[jax.ShapeDtypeStruct((K, F, D), BF16)],
        scratch_shapes=[pltpu.VMEM((tf, D), F32)] * K,
        vmem_mib=56, args=(*lhs, rhs), rider=rider)


def _lru_gates(xr, bda_ref, bdx_ref, vec_ref):
    xrb = xr.astype(BF16)
    r = _sigmoid(_dot(xrb, bda_ref[...], NN) + vec_ref[V_BA:V_BA + 1, :])
    ig = _sigmoid(_dot(xrb, bdx_ref[...], NN) + vec_ref[V_BX:V_BX + 1, :])
    sp = _softplus_neg(vec_ref[V_LAM:V_LAM + 1, :])
    log_a = (-LRU_C * sp) * r
    a = jnp.exp(log_a)
    mult = jnp.sqrt(_neg_expm1(2.0 * log_a))
    return xrb, r, ig, sp, a, mult


def _layernorm_stats(u1):
    xc = u1 - jnp.mean(u1, axis=-1, keepdims=True)
    rs = lax.rsqrt(jnp.mean(xc * xc, axis=-1, keepdims=True) + LN_EPS)
    return xc * rs, rs


def _mix_core_fwd(x1, g, w_in_t, w_out, bda, bdx, cw, lw, vec, tm, name, rider=None):
    T, D = x1.shape
    W = cw.shape[1]
    assert tm >= CONV_HALO and w_in_t.shape[0] == 4 * W

    def body(x1_ref, g_ref, wi_ref, wo_ref, bda_ref, bdx_ref, cw_ref, lw_ref, vec_ref,
             x2_ref, z_ref, mix_ref, u1_ref, xr_ref, hst_ref, ubuf, rbuf, hc):
        @pl.when(pl.program_id(0) == 0)
        def _():
            ubuf[0:CONV_HALO, :] = jnp.zeros((CONV_HALO, W), F32)
            rbuf[0:LRU_HALO, :] = jnp.zeros((LRU_HALO, W), F32)
            hc[...] = jnp.zeros_like(hc)

        xhat, _ = _rms_stats(x1_ref[...])
        z_ref[...] = _dot((xhat * g_ref[...]).astype(BF16), wi_ref[...], NT)

        ubuf[CONV_HALO:CONV_HALO + tm, :] = z_ref[:, 0:W] * _sigmoid(z_ref[:, W:2 * W])
        u1 = jnp.zeros((tm, W), F32) + vec_ref[V_CB:V_CB + 1, :]
        base = CONV_HALO - (CONV_K - 1)
        for off, win in _row_windows(ubuf, tm, range(base, base + CONV_K)):
            u1 = u1 + cw_ref[off - base:off - base + 1, :] * win
        ubuf[0:CONV_HALO, :] = ubuf[tm:tm + CONV_HALO, :]
        u1_ref[...] = u1
        xh, _ = _layernorm_stats(u1)
        u2 = xh * vec_ref[V_LNG:V_LNG + 1, :] + vec_ref[V_LNB:V_LNB + 1, :]
        ub = (u2 * _sigmoid(u2)).astype(BF16)
        mix_ref[:, 0:W] = ub

        rbuf[LRU_HALO:LRU_HALO + tm, :] = z_ref[:, 2 * W:3 * W]
        xr = jnp.zeros((tm, W), F32) + vec_ref[V_LCB:V_LCB + 1, :]
        for k in range(LRU_K):
            off = LRU_HALO - (LRU_K - 1) + k
            xr = xr + lw_ref[k:k + 1, :] * rbuf[off:off + tm, :]
        rbuf[0:LRU_HALO, :] = rbuf[tm:tm + LRU_HALO, :]
        xr_ref[...] = xr
        _, _, ig, _, a, mult = _lru_gates(xr, bda_ref, bdx_ref, vec_ref)
        hc[0:1, :] = _scan_rows(a, mult * (ig * xr), hc[0:1, :], hst_ref)
        gl, _ = _gelu_parts(z_ref[:, 3 * W:4 * W])
        yb = (hst_ref[...] * gl).astype(BF16)
        mix_ref[:, W:2 * W] = yb

        x2_ref[...] = x1_ref[...] + _dot(ub, wo_ref[0:W, :], NN) + _dot(yb, wo_ref[W:2 * W, :], NN)

    full = lambda a: pl.BlockSpec(a.shape, lambda i: (0,) * a.ndim)
    tile = lambda n: pl.BlockSpec((tm, n), lambda i: (i, 0))
    return _call(
        body, name=name, grid=(T // tm,),
        in_specs=[tile(D), full(g), full(w_in_t), full(w_out), full(bda), full(bdx), full(cw), full(lw), full(vec)],
        out_specs=[tile(D), tile(4 * W), tile(2 * W), tile(W), tile(W), tile(W)],
        out_shape=[jax.ShapeDtypeStruct((T, D), F32), jax.ShapeDtypeStruct((T, 4 * W), F32),
                   jax.ShapeDtypeStruct((T, 2 * W), BF16), jax.ShapeDtypeStruct((T, W), F32),
                   jax.ShapeDtypeStruct((T, W), F32), jax.ShapeDtypeStruct((T, W), F32)],
        scratch_shapes=[pltpu.VMEM((tm + CONV_HALO, W), F32), pltpu.VMEM((tm + LRU_HALO, W), F32),
                        pltpu.VMEM((8, W), F32)],
        vmem_mib=56, args=(x1, g, w_in_t, w_out, bda, bdx, cw, lw, vec), rider=rider)


def _mix_bwd(dx2, z, u1, xr, hst, x1, mix, g, w_in_t, w_out, bda, bdx, cw, lw, vec, tm, name, rider=None):
    T, D = dx2.shape
    W = cw.shape[1]
    nt = T // tm
    assert tm >= CONV_HALO and tm % CONV_HALO == 0

    def body(dx_ref, z_ref, zh_ref, u1_ref, xr_ref, h_ref, hh_ref, wo_ref, bda_ref, bdx_ref, cw_ref, lw_ref, vec_ref,
             x1_ref, mix_ref, g_ref, wi_ref,
             dx1_ref, sg_ref, dbda_ref, dbdx_ref, dob_ref, dg_ref, dwi_ref, dwo_ref,
             u0buf, du1buf, rxbuf, dxrbuf, gbuf, gc, spacc, dz_ref, ai_ref, ao_ref):
        i = pl.program_id(0)
        first = i == nt - 1
        row = lax.broadcasted_iota(jnp.int32, (tm, W), 0)

        @pl.when(i == 0)
        def _():
            sg_ref[...] = jnp.zeros_like(sg_ref)
            dbda_ref[...] = jnp.zeros_like(dbda_ref)
            dbdx_ref[...] = jnp.zeros_like(dbdx_ref)
            du1buf[tm:tm + CONV_HALO, :] = jnp.zeros((CONV_HALO, W), F32)
            dxrbuf[tm:tm + LRU_HALO, :] = jnp.zeros((LRU_HALO, W), F32)
            gc[...] = jnp.zeros_like(gc)
            spacc[...] = jnp.zeros_like(spacc)
            dg_ref[...] = jnp.zeros_like(dg_ref)
            ai_ref[...] = jnp.zeros_like(ai_ref)
            ao_ref[...] = jnp.zeros_like(ao_ref)

        def accum(r, val):
            sg_ref[r:r + 1, :] += jnp.sum(val, axis=0, keepdims=True)

        x1hat, x1rstd = _rms_stats(x1_ref[...])
        gain = g_ref[...]
        hb = (x1hat * gain).astype(BF16)

        def in_proj_bwd(lo, hi):
            dzb = dz_ref[:, lo:hi]
            ai_ref[lo:hi, :] += _dot(dzb, hb, TN)
            return _dot(dzb, wi_ref[lo:hi, :], NN)

        dxb = dx_ref[...].astype(BF16)
        ao_ref[...] += _dot(mix_ref[...], dxb, TN)
        dmix = _dot(dxb, wo_ref[...], NT)
        d_u = dmix[:, 0:W]
        d_yr = dmix[:, W:2 * W]

        xh, rs = _layernorm_stats(u1_ref[...])
        ln_g = vec_ref[V_LNG:V_LNG + 1, :]
        u2 = xh * ln_g + vec_ref[V_LNB:V_LNB + 1, :]
        s2 = _sigmoid(u2)
        d_u2 = d_u * (s2 * (1.0 + u2 * (1.0 - s2)))
        accum(G_LNG, d_u2 * xh)
        accum(G_LNB, d_u2)
        d_xh = d_u2 * ln_g
        d_u1 = rs * (d_xh - jnp.mean(d_xh, axis=-1, keepdims=True)
                     - xh * jnp.mean(d_xh * xh, axis=-1, keepdims=True))
        accum(G_CB, d_u1)
        halo_on = jnp.where(first, 0.0, 1.0)
        u0buf[0:CONV_HALO, :] = halo_on * (zh_ref[:, 0:W] * _sigmoid(zh_ref[:, W:2 * W]))
        cv = z_ref[:, 0:W]
        sgc = _sigmoid(z_ref[:, W:2 * W])
        u0buf[CONV_HALO:CONV_HALO + tm, :] = cv * sgc
        du1buf[0:tm, :] = d_u1
        base = CONV_HALO - (CONV_K - 1)
        for off, win in _row_windows(u0buf, tm, range(base, base + CONV_K)):
            accum(G_CW + off - base, d_u1 * win)
        d_u0 = jnp.zeros((tm, W), F32)
        for off, win in _row_windows(du1buf, tm, range(0, CONV_K)):
            d_u0 = d_u0 + cw_ref[CONV_K - 1 - off:CONV_K - off, :] * win
        du1buf[tm:tm + CONV_HALO, :] = du1buf[0:CONV_HALO, :]
        dz_ref[:, 0:W] = (d_u0 * sgc).astype(BF16)
        dz_ref[:, W:2 * W] = (d_u0 * cv * (sgc * (1.0 - sgc))).astype(BF16)
        dh = in_proj_bwd(0, 2 * W)

        xrv = xr_ref[...]
        xrb, r, ig, sp, a, mult = _lru_gates(xrv, bda_ref, bdx_ref, vec_ref)
        h = h_ref[...]
        gl, dgl = _gelu_parts(z_ref[:, 3 * W:4 * W])
        dz_ref[:, 3 * W:4 * W] = (d_yr * h * dgl).astype(BF16)
        dh = dh + in_proj_bwd(3 * W, 4 * W)
        a_next = jnp.where(row == tm - 1, 1.0, pltpu.roll(a, tm - 1, 0))
        g_first = _scan_rows(a_next, d_yr * gl, gc[0:1, :], gbuf, reverse=True)
        g = gbuf[...]
        gc[0:1, :] = a[0:1, :] * g_first
        hprev = jnp.where(row == 0, halo_on * hh_ref[LRU_HALO - 1:LRU_HALO, :], pltpu.roll(h, 1, 0))
        d_log_a = (g * hprev) * a - (g * ig * xrv) * (a * a) / mult
        d_ig = g * mult * xrv
        d_xr = g * mult * ig
        spacc[0:1, :] += jnp.sum(d_log_a * r, axis=0, keepdims=True)
        d_pa32 = (d_log_a * (-LRU_C * sp)) * (r * (1.0 - r))
        d_px32 = d_ig * (ig * (1.0 - ig))
        accum(G_BA, d_pa32)
        accum(G_BX, d_px32)
        d_pa = d_pa32.astype(BF16)
        d_px = d_px32.astype(BF16)
        d_xr = d_xr + _dot(d_pa, bda_ref[...], NT) + _dot(d_px, bdx_ref[...], NT)
        dbda_ref[...] += _dot(xrb, d_pa, TN)
        dbdx_ref[...] += _dot(xrb, d_px, TN)
        accum(G_LCB, d_xr)
        rxbuf[0:LRU_HALO, :] = halo_on * zh_ref[CONV_HALO - LRU_HALO:CONV_HALO, 2 * W:3 * W]
        rxbuf[LRU_HALO:LRU_HALO + tm, :] = z_ref[:, 2 * W:3 * W]
        dxrbuf[0:tm, :] = d_xr
        d_rx = jnp.zeros((tm, W), F32)
        for k in range(LRU_K):
            off = LRU_HALO - (LRU_K - 1) + k
            accum(G_LW + k, d_xr * rxbuf[off:off + tm, :])
            d_rx = d_rx + lw_ref[k:k + 1, :] * dxrbuf[LRU_K - 1 - k:LRU_K - 1 - k + tm, :]
        dxrbuf[tm:tm + LRU_HALO, :] = dxrbuf[0:LRU_HALO, :]
        dz_ref[:, 2 * W:3 * W] = d_rx.astype(BF16)
        dh = dh + in_proj_bwd(2 * W, 3 * W)

        dx, dg = _rms_bwd(x1hat, x1rstd, gain, dh)
        dx1 = dx_ref[...] + dx
        dx1_ref[...] = dx1
        dob_ref[...] = (FFN_RES * dx1).astype(BF16)
        dg_ref[...] += dg

        @pl.when(first)
        def _():
            lam = vec_ref[V_LAM:V_LAM + 1, :]
            sg_ref[G_LAM:G_LAM + 1, :] = LRU_C * _sigmoid(-lam) * spacc[0:1, :]
            dwi_ref[...] = ai_ref[...].astype(BF16)
            dwo_ref[...] = ao_ref[...].astype(BF16)

    full = lambda a: pl.BlockSpec(a.shape, lambda i: (0,) * a.ndim)
    tile = lambda n: pl.BlockSpec((tm, n), lambda i: (nt - 1 - i, 0))
    halo = lambda rows, n: pl.BlockSpec(
        (rows, n), lambda i: (jnp.maximum((nt - 1 - i) * (tm // rows) - 1, 0), 0))
    const = lambda r, c: pl.BlockSpec((r, c), lambda i: (0, 0))
    return _call(
        body, name=name, grid=(nt,),
        in_specs=[tile(D), tile(4 * W), halo(CONV_HALO, 4 * W), tile(W), tile(W), tile(W), halo(LRU_HALO, W),
                  full(w_out), full(bda), full(bdx), full(cw), full(lw), full(vec),
                  tile(D), tile(2 * W), full(g), full(w_in_t)],
        out_specs=[tile(D), const(G_ROWS, W), const(W, W), const(W, W),
                   tile(D), const(1, D), const(4 * W, D), const(2 * W, D)],
        out_shape=[jax.ShapeDtypeStruct((T, D), F32), jax.ShapeDtypeStruct((G_ROWS, W), F32),
                   jax.ShapeDtypeStruct((W, W), F32), jax.ShapeDtypeStruct((W, W), F32),
                   jax.ShapeDtypeStruct((T, D), BF16), jax.ShapeDtypeStruct((1, D), F32),
                   jax.ShapeDtypeStruct((4 * W, D), BF16), jax.ShapeDtypeStruct((2 * W, D), BF16)],
        scratch_shapes=[pltpu.VMEM((tm + CONV_HALO, W), F32), pltpu.VMEM((tm + CONV_HALO, W), F32),
                        pltpu.VMEM((tm + LRU_HALO, W), F32), pltpu.VMEM((tm + LRU_HALO, W), F32),
                        pltpu.VMEM((tm, W), F32), pltpu.VMEM((8, W), F32), pltpu.VMEM((8, W), F32),
                        pltpu.VMEM((tm, 4 * W), BF16), pltpu.VMEM((4 * W, D), F32), pltpu.VMEM((2 * W, D), F32)],
        vmem_mib=60, args=(dx2, z, z, u1, xr, hst, hst, w_out, bda, bdx, cw, lw, vec, x1, mix, g, w_in_t),
        rider=rider)


def _pair_add(full, recv, name):
    K, _, _, rows, D = full.shape

    def body(c_ref, a_ref, b_ref, o_ref):
        o_ref[...] = (a_ref[...].astype(F32) + b_ref[...].astype(F32)).astype(BF16)

    c = lax.axis_index("c").astype(jnp.int32).reshape((1,))
    return _call(
        body, name=name, grid=(K, N_CHIP), num_scalar_prefetch=1,
        in_specs=[pl.BlockSpec((None, None, None, rows, D), lambda k, q, c_ref: (k, q, c_ref[0], 0, 0)),
                  pl.BlockSpec((None, None, rows, D), lambda k, q, c_ref: (k, q, 0, 0))],
        out_specs=pl.BlockSpec((None, None, rows, D), lambda k, q, c_ref: (k, q, 0, 0)),
        out_shape=jax.ShapeDtypeStruct(recv.shape, BF16),
        scratch_shapes=[], vmem_mib=16, args=(c, full, recv))


def _adamw_update(wv, gv, mv, vv):
    m2 = ADAM_B1 * mv + (1.0 - ADAM_B1) * gv
    v2 = ADAM_B2 * vv + (1.0 - ADAM_B2) * (gv * gv)
    m_hat = m2 / (1.0 - ADAM_B1 ** ADAM_STEP)
    v_hat = v2 / (1.0 - ADAM_B2 ** ADAM_STEP)
    return -ADAM_LR * (m_hat / (jnp.sqrt(v_hat) + ADAM_EPS) + ADAM_WD * wv), m2, v2


def _finish(parts, k, w, m, v, transpose, name):
    _, n_parts, rows, D = parts.shape

    def body(p_ref, w_ref, m_ref, v_ref, g_ref, d_ref, mo_ref, vo_ref):
        acc = p_ref[0].astype(F32)
        for q in range(1, n_parts):
            acc = acc + p_ref[q].astype(F32)
        gv = acc.T if transpose else acc
        g_ref[...] = gv
        d_ref[...], mo_ref[...], vo_ref[...] = _adamw_update(w_ref[...], gv, m_ref[...], v_ref[...])

    whole = pl.BlockSpec(w.shape, lambda i: (0, 0))
    return _call(
        body, name=name, grid=(1,),
        in_specs=[pl.BlockSpec((None, n_parts, rows, D), lambda i: (k, 0, 0, 0)), whole, whole, whole],
        out_specs=[whole] * 4, out_shape=[pltpu.HBM(w.shape, F32)] * 4,
        scratch_shapes=[], vmem_mib=40, args=(parts, w, m, v))


def _adamw_each(ws, gs, ms, vs, name):
    n = len(ws)

    def body(*refs):
        w_refs, g_refs, m_refs, v_refs, outs = refs[:n], refs[n:2 * n], refs[2 * n:3 * n], refs[3 * n:4 * n], refs[4 * n:]
        for k in range(n):
            outs[k][...], outs[n + k][...], outs[2 * n + k][...] = _adamw_update(
                w_refs[k][...], g_refs[k][...], m_refs[k][...], v_refs[k][...])

    shapes = [jax.ShapeDtypeStruct(w.shape, F32) for w in ws]
    return pl.pallas_call(
        body, name=name,
        in_specs=[VMEM_SPEC] * (4 * n), out_specs=[VMEM_SPEC] * (3 * n), out_shape=shapes * 3,
        compiler_params=pltpu.CompilerParams(vmem_limit_bytes=32 * MIB),
    )(*ws, *gs, *ms, *vs)


def _block_diag(w):
    h, d, _ = w.shape
    onto = jnp.eye(h, dtype=w.dtype)
    return (w[:, :, None, :] * onto[:, None, :, None]).reshape(h * d, h * d)


def _diag_blocks(m, h):
    d = m.shape[0] // h
    onto = jnp.eye(h, dtype=m.dtype)
    return (m.reshape(h, d, h, d) * onto[:, None, :, None]).sum(axis=2)


def _reduce_level1(full, tag):
    got = _run_comm(_sibling_comm(full), "rs_sibling_" + tag)
    return [_pair_add(a, b, "rs_pair_add_%s%d" % (tag, n)) for n, (a, b) in enumerate(zip(full, got))]


def kernel(x, ffn1_norm, ffn1_w_gate, ffn1_w_up, ffn1_w_down, mix_norm, w_in, conv_dw, conv_dw_bias, conv_ln_g, conv_ln_b, lru_conv_w, lru_conv_b, lru_w_a, lru_b_a, lru_w_x, lru_b_x, lru_lambda, w_out, ffn2_norm, ffn2_w_gate, ffn2_w_up, ffn2_w_down, final_norm, loss_target, m_ffn1_norm, m_ffn1_w_gate, m_ffn1_w_up, m_ffn1_w_down, m_mix_norm, m_w_in, m_conv_dw, m_conv_dw_bias, m_conv_ln_g, m_conv_ln_b, m_lru_conv_w, m_lru_conv_b, m_lru_w_a, m_lru_b_a, m_lru_w_x, m_lru_b_x, m_lru_lambda, m_w_out, m_ffn2_norm, m_ffn2_w_gate, m_ffn2_w_up, m_ffn2_w_down, m_final_norm, v_ffn1_norm, v_ffn1_w_gate, v_ffn1_w_up, v_ffn1_w_down, v_mix_norm, v_w_in, v_conv_dw, v_conv_dw_bias, v_conv_ln_g, v_conv_ln_b, v_lru_conv_w, v_lru_conv_b, v_lru_w_a, v_lru_b_a, v_lru_w_x, v_lru_b_x, v_lru_lambda, v_w_out, v_ffn2_norm, v_ffn2_w_gate, v_ffn2_w_up, v_ffn2_w_down, v_final_norm):
    T, D = x.shape[1], x.shape[2]
    F = ffn1_w_down.shape[0] * N_DEV
    rf = ffn1_w_down.shape[0]
    ri = w_in.shape[1]
    ro = w_out.shape[0]
    W = conv_dw_bias.shape[0]
    wc = conv_dw.shape[1]
    H = lru_w_a.shape[0]
    xs = x.reshape(T, D)
    tgt = loss_target.reshape(T, D)
    tm_ffn = min(256, T)
    tm_fwd = min(512, T)
    cf = 256
    tm_w = min(1024, T)
    tm_w1 = min(2048, T)
    tm_mix = min(256, T)
    tf_w = F // 2
    row = lambda v: v.reshape(1, -1)
    by_owner = lambda a, rows: a.reshape(a.shape[0], N_CHIP, 2, rows, D)

    p3a, p3b, p_in, p_out = _prep_weights(
        (ffn1_w_gate.T, ffn1_w_up.T, ffn1_w_down), (ffn2_w_gate.T, ffn2_w_up.T, ffn2_w_down), w_in, w_out,
        "prep_weights")
    tile_rows = lambda a: jnp.pad(a, ((0, -a.shape[0] % SUBLANES), (0, 0)))
    p_cw = jnp.concatenate([tile_rows(conv_dw), tile_rows(lru_conv_w)], axis=0)
    lw_row = p_cw.shape[0] - SUBLANES
    stacked = lambda r, j: r.at[:, j]
    plain = lambda r, j: r.at[j]
    g3_shape = jax.ShapeDtypeStruct((3, N_DEV, rf, D), BF16)
    (g3a,) = _all_gather([p3a], [stacked], [g3_shape], "ag_ffn1")
    w3a = g3a.reshape(3, F, D)
    bda = _block_diag(lru_w_a).astype(BF16)
    bdx = _block_diag(lru_w_x).astype(BF16)
    vec = jnp.concatenate([tile_rows(v[None]) for v in
                           (conv_dw_bias, conv_ln_g, conv_ln_b, lru_conv_b, lru_b_a, lru_b_x, lru_lambda)], axis=0)

    gather_rest = _gather_comm(
        [p3b, p_in, p_out, p_cw], [stacked, plain, plain, plain],
        [g3_shape, jax.ShapeDtypeStruct((N_DEV, ri, D), BF16), jax.ShapeDtypeStruct((N_DEV, ro, D), BF16),
         jax.ShapeDtypeStruct((N_DEV,) + p_cw.shape, F32)],
        [(SIBLING,) + SAME_CORE, EVERYONE, EVERYONE, EVERYONE])
    (x1, h1, dau1, dag1, act1), (g3b_half, g_in, g_out, g_cw) = _ffn_fwd(
        xs, row(ffn1_norm), w3a, tm_fwd, cf, "ffn1_fwd", rider=gather_rest)
    w_in_t = g_in.reshape(N_DEV * ri, D)
    w_out_f = g_out.reshape(N_DEV * ro, D)
    cw_all = jnp.transpose(g_cw, (1, 0, 2)).reshape(p_cw.shape[0], N_DEV * wc)
    cw = cw_all[0:CONV_K]
    lw = cw_all[lw_row:lw_row + LRU_K]
    (x2, z, mix, u1, xr, hst), (g3b,) = _mix_core_fwd(
        x1, row(mix_norm), w_in_t, w_out_f, bda, bdx, cw, lw, vec, tm_mix, "mix_core_fwd",
        rider=_forward_comm([g3b_half], [stacked]))
    w3b = g3b.reshape(3, F, D)
    dx3, dob2, d_final_norm, loss_part, h3, dau2, dag2, act2 = _ffn_fwd(
        x2, row(ffn2_norm), w3b, tm_fwd, cf, "ffn2_fwd_loss", head=(row(final_norm), tgt))

    dx2, dgate2, dup2, d_ffn2_norm = _ffn_dgrad(dx3, x2, row(ffn2_norm), dau2, dag2, w3b, tm_ffn, cf, "ffn2_dgrad")
    (dw_gu2,) = _wgrad([dgate2, dup2], h3, tm_w, tf_w, "ffn2_wgrad_gu")
    (dw_d2,) = _wgrad([act2], dob2, tm_w1, tf_w, "ffn2_wgrad_d")
    by_device = lambda a: a.reshape(a.shape[0], N_DEV, rf, D)
    (dx1, sg, dbda, dbdx, dob1, d_mix_norm, dw_in_t, dw_out), parts_f2 = _mix_bwd(
        dx2, z, u1, xr, hst, x1, mix, row(mix_norm), w_in_t, w_out_f, bda, bdx, cw, lw, vec, tm_mix, "mix_bwd",
        rider=_chips_comm([by_device(dw_gu2), by_device(dw_d2)], every_device=True))
    io = [dw_in_t.reshape(1, N_DEV, ri, D), dw_out.reshape(1, N_DEV, ro, D)]
    (dw_d1,), parts_io = _wgrad(
        [act1], dob1, tm_w1, tf_w, "ffn1_wgrad_d", rider=_chips_comm(io, every_device=True))
    sums_d1 = _reduce_level1([by_owner(dw_d1, rf)], "d1")
    dx0, dgate1, dup1, d_ffn1_norm = _ffn_dgrad(dx1, xs, row(ffn1_norm), dau1, dag1, w3a, tm_ffn, cf, "ffn1_dgrad")
    (dw_g1,), parts_d1 = _wgrad([dgate1], h1, tm_w1, tf_w, "ffn1_wgrad_g", rider=_chips_comm(sums_d1))
    sums_g1 = _reduce_level1([by_owner(dw_g1, rf)], "g1")
    small = [d_ffn1_norm, d_mix_norm, d_ffn2_norm, d_final_norm, sg, _diag_blocks(dbda, H).reshape(-1, D),
             _diag_blocks(dbdx, H).reshape(-1, D), loss_part]
    (dw_u1,), summed_and_parts = _wgrad(
        [dup1], h1, tm_w1, tf_w, "ffn1_wgrad_u", rider=_both(_small_sum_comm(small), _chips_comm(sums_g1)))
    summed, parts_g1 = summed_and_parts[:len(small)], summed_and_parts[len(small):]
    sums_u1 = _reduce_level1([by_owner(dw_u1, rf)], "u1")
    parts_u1 = _run_comm(_chips_comm(sums_u1), "rs_chips_u1")

    g_norm1, g_norm_mix, g_norm2, g_norm_final, g_sg, g_w_a, g_w_x, g_loss = summed
    loss = g_loss[0, 0]
    me = 4 * lax.axis_index("x") + 2 * lax.axis_index("y") + lax.axis_index("c")
    chan = lambda full_g: lax.dynamic_slice_in_dim(full_g, me * wc, wc, axis=1)
    grads = {
        "ffn1_norm": g_norm1.reshape(D), "mix_norm": g_norm_mix.reshape(D), "ffn2_norm": g_norm2.reshape(D),
        "final_norm": g_norm_final.reshape(D),
        "conv_dw_bias": g_sg[G_CB], "conv_ln_g": g_sg[G_LNG], "conv_ln_b": g_sg[G_LNB],
        "lru_conv_b": g_sg[G_LCB], "lru_b_a": g_sg[G_BA], "lru_b_x": g_sg[G_BX], "lru_lambda": g_sg[G_LAM],
        "lru_w_a": g_w_a.reshape(lru_w_a.shape), "lru_w_x": g_w_x.reshape(lru_w_x.shape),
        "conv_dw": chan(g_sg[G_CW:G_CW + CONV_K]), "lru_conv_w": chan(g_sg[G_LW:G_LW + LRU_K]),
    }

    weights = dict(ffn1_norm=ffn1_norm, ffn1_w_gate=ffn1_w_gate, ffn1_w_up=ffn1_w_up, ffn1_w_down=ffn1_w_down, mix_norm=mix_norm, w_in=w_in, conv_dw=conv_dw, conv_dw_bias=conv_dw_bias, conv_ln_g=conv_ln_g, conv_ln_b=conv_ln_b, lru_conv_w=lru_conv_w, lru_conv_b=lru_conv_b, lru_w_a=lru_w_a, lru_b_a=lru_b_a, lru_w_x=lru_w_x, lru_b_x=lru_b_x, lru_lambda=lru_lambda, w_out=w_out, ffn2_norm=ffn2_norm, ffn2_w_gate=ffn2_w_gate, ffn2_w_up=ffn2_w_up, ffn2_w_down=ffn2_w_down, final_norm=final_norm)
    moment1 = dict(ffn1_norm=m_ffn1_norm, ffn1_w_gate=m_ffn1_w_gate, ffn1_w_up=m_ffn1_w_up, ffn1_w_down=m_ffn1_w_down, mix_norm=m_mix_norm, w_in=m_w_in, conv_dw=m_conv_dw, conv_dw_bias=m_conv_dw_bias, conv_ln_g=m_conv_ln_g, conv_ln_b=m_conv_ln_b, lru_conv_w=m_lru_conv_w, lru_conv_b=m_lru_conv_b, lru_w_a=m_lru_w_a, lru_b_a=m_lru_b_a, lru_w_x=m_lru_w_x, lru_b_x=m_lru_b_x, lru_lambda=m_lru_lambda, w_out=m_w_out, ffn2_norm=m_ffn2_norm, ffn2_w_gate=m_ffn2_w_gate, ffn2_w_up=m_ffn2_w_up, ffn2_w_down=m_ffn2_w_down, final_norm=m_final_norm)
    moment2 = dict(ffn1_norm=v_ffn1_norm, ffn1_w_gate=v_ffn1_w_gate, ffn1_w_up=v_ffn1_w_up, ffn1_w_down=v_ffn1_w_down, mix_norm=v_mix_norm, w_in=v_w_in, conv_dw=v_conv_dw, conv_dw_bias=v_conv_dw_bias, conv_ln_g=v_conv_ln_g, conv_ln_b=v_conv_ln_b, lru_conv_w=v_lru_conv_w, lru_conv_b=v_lru_conv_b, lru_w_a=v_lru_w_a, lru_b_a=v_lru_b_a, lru_w_x=v_lru_w_x, lru_b_x=v_lru_b_x, lru_lambda=v_lru_lambda, w_out=v_w_out, ffn2_norm=v_ffn2_norm, ffn2_w_gate=v_ffn2_w_gate, ffn2_w_up=v_ffn2_w_up, ffn2_w_down=v_ffn2_w_down, final_norm=v_final_norm)
    order = list(weights)
    big = {"ffn1_w_gate": (parts_g1[0], 0, True), "ffn1_w_up": (parts_u1[0], 0, True),
           "ffn1_w_down": (parts_d1[0], 0, False), "w_in": (parts_io[0], 0, True), "w_out": (parts_io[1], 0, False),
           "ffn2_w_gate": (parts_f2[0], 0, True), "ffn2_w_up": (parts_f2[0], 1, True),
           "ffn2_w_down": (parts_f2[1], 0, False)}
    delta, new_m, new_v = {}, {}, {}
    for n, (parts, k, d_major) in big.items():
        operands = weights[n], moment1[n], moment2[n]
        if d_major and n != "w_in":
            results = _finish(parts, k, *[a.T for a in operands], False, "finish_" + n)
            grads[n], delta[n], new_m[n], new_v[n] = [r.T for r in results]
        else:
            grads[n], delta[n], new_m[n], new_v[n] = _finish(parts, k, *operands, d_major, "finish_" + n)
    rest = [n for n in order if n not in big]
    updates = _adamw_each([weights[n] for n in rest], [grads[n] for n in rest], [moment1[n] for n in rest],
                          [moment2[n] for n in rest], "adamw_small")
    for k, n in enumerate(rest):
        delta[n], new_m[n], new_v[n] = updates[k], updates[len(rest) + k], updates[2 * len(rest) + k]

    return (loss, dx0.reshape(x.shape), *[grads[n] for n in order], *[delta[n] for n in order],
            *[new_m[n] for n in order], *[new_v[n] for n in order])
```

```python
import functools
import math

import jax
import jax.numpy as jnp
from jax import lax
from jax.experimental import pallas as pl
from jax.experimental.pallas import tpu as pltpu

F32 = jnp.float32
BF16 = jnp.bfloat16
MESH = pl.DeviceIdType.MESH

N_DEV = 8
N_CHIP = 4
SUBLANES = 8
RMS_EPS = 1e-6
LN_EPS = 1e-5
LRU_C = 8.0
CONV_K = 31
LRU_K = 4
CONV_HALO = 32
LRU_HALO = 8
FFN_RES = 0.5
ADAM_LR, ADAM_B1, ADAM_B2, ADAM_EPS, ADAM_WD, ADAM_STEP = 0.001, 0.9, 0.999, 1e-08, 0.01, 10
GELU_K = math.sqrt(2.0 / math.pi)
GELU_C = 0.044715

MIB = 1024 * 1024
NT = (((1,), (1,)), ((), ()))
NN = (((1,), (0,)), ((), ()))
TN = (((0,), (0,)), ((), ()))

V_CB, V_LNG, V_LNB, V_LCB, V_BA, V_BX, V_LAM = range(0, 7 * SUBLANES, SUBLANES)
G_CW = 0
G_CB, G_LNG, G_LNB = 31, 32, 33
G_LW = 34
G_LCB, G_BA, G_BX, G_LAM = 38, 39, 40, 41
G_ROWS = 48

HBM_SPEC = pl.BlockSpec(memory_space=pltpu.HBM)
VMEM_SPEC = pl.BlockSpec(memory_space=pltpu.VMEM)


def _dot(a, b, dims):
    return lax.dot_general(a, b, dims, preferred_element_type=F32)


def _sigmoid(x):
    return 1.0 / (1.0 + jnp.exp(-x))


def _gelu_parts(x):
    x2 = x * x
    th = jnp.tanh(GELU_K * x * (1.0 + GELU_C * x2))
    gl = 0.5 * x * (1.0 + th)
    dgl = 0.5 * (1.0 + th) + 0.5 * x * (1.0 - th * th) * GELU_K * (1.0 + 3.0 * GELU_C * x2)
    return gl, dgl


def _neg_expm1(y):
    series = -y * (1.0 + y * (1.0 / 2) * (1.0 + y * (1.0 / 3) * (1.0 + y * (1.0 / 4) * (1.0 + y * (1.0 / 5) * (1.0 + y * (1.0 / 6))))))
    return jnp.where(y > -0.25, series, 1.0 - jnp.exp(y))


def _softplus_neg(lam):
    t = -lam
    e = jnp.exp(-jnp.abs(t))
    s = 1.0 + e
    log1p_e = jnp.log(s) - ((s - 1.0) - e) / s
    return jnp.maximum(t, 0.0) + log1p_e


def _rms_stats(xv):
    rstd = lax.rsqrt(jnp.mean(xv * xv, axis=-1, keepdims=True) + RMS_EPS)
    return xv * rstd, rstd


def _rms_bwd(xhat, rstd, g, dh):
    dxhat = dh * g
    dx = rstd * (dxhat - xhat * jnp.mean(dxhat * xhat, axis=-1, keepdims=True))
    return dx, jnp.sum(dh * xhat, axis=0, keepdims=True)


def _row_windows(buf_ref, n_rows, offsets):
    total = buf_ref.shape[0]
    full = buf_ref[...]
    for b in range(SUBLANES):
        offs = [o for o in offsets if o % SUBLANES == b]
        if not offs:
            continue
        assert max(offs) + n_rows <= total
        moved = full if b == 0 else pltpu.roll(full, total - b, 0)
        for o in offs:
            yield o, moved[o - b:o - b + n_rows, :]


def _scan_rows(av, bv, edge, out_ref, reverse=False):
    tm, W = av.shape
    sub = lax.broadcasted_iota(jnp.int32, (tm, W), 0) % SUBLANES
    s = 1
    while s < SUBLANES:
        keep = (sub < SUBLANES - s) if reverse else (sub >= s)
        shift = tm - s if reverse else s
        bv = jnp.where(keep, av * pltpu.roll(bv, shift, 0) + bv, bv)
        av = jnp.where(keep, av * pltpu.roll(av, shift, 0), av)
        s *= 2
    starts = range(0, tm, SUBLANES)
    for r0 in (reversed(starts) if reverse else starts):
        group = av[r0:r0 + SUBLANES, :] * edge + bv[r0:r0 + SUBLANES, :]
        out_ref[r0:r0 + SUBLANES, :] = group
        edge = group[0:1, :] if reverse else group[SUBLANES - 1:SUBLANES, :]
    return edge


class _Comm:
    def __init__(self, arrays, in_specs, out_shapes, out_specs, scratch, start, wait, aliases=None):
        self.arrays, self.in_specs = list(arrays), list(in_specs)
        self.out_shapes, self.out_specs = list(out_shapes), list(out_specs)
        self.scratch, self.start, self.wait = list(scratch), start, wait
        self.aliases = dict(aliases or {})


def _in_hbm(a):
    return pltpu.with_memory_space_constraint(a, pltpu.HBM)


def _operands(comm):
    return [a if spec is VMEM_SPEC else _in_hbm(a) for a, spec in zip(comm.arrays, comm.in_specs)]


def _call(body, *, name, grid, in_specs, out_specs, out_shape, scratch_shapes, vmem_mib, args, rider=None,
          num_scalar_prefetch=0):
    params = pltpu.CompilerParams(dimension_semantics=("arbitrary",) * len(grid), vmem_limit_bytes=vmem_mib * MIB)
    args = [a if k < num_scalar_prefetch else _in_hbm(a) for k, a in enumerate(args)]
    if rider is None:
        return pl.pallas_call(
            body, name=name,
            grid_spec=pltpu.PrefetchScalarGridSpec(
                num_scalar_prefetch=num_scalar_prefetch, grid=grid, in_specs=in_specs, out_specs=out_specs,
                scratch_shapes=scratch_shapes),
            out_shape=out_shape, compiler_params=params)(*args)
    assert num_scalar_prefetch == 0
    n_in, n_out, n_scr = len(in_specs), len(out_specs), len(scratch_shapes)
    r_in, r_out = len(rider.arrays), len(rider.out_shapes)
    n_axes = len(grid)

    def carried(*refs):
        pos = [0]

        def take(n):
            pos[0] += n
            return refs[pos[0] - n:pos[0]]

        ins, r_ins, outs, r_outs, scr, r_scr = take(n_in), take(r_in), take(n_out), take(r_out), take(n_scr), take(len(rider.scratch))
        first = pl.program_id(0) == 0
        last = pl.program_id(0) == grid[0] - 1
        for ax in range(1, n_axes):
            first = first & (pl.program_id(ax) == 0)
            last = last & (pl.program_id(ax) == grid[ax] - 1)

        @pl.when(first)
        def _():
            rider.start(r_ins, r_outs, r_scr)

        body(*ins, *outs, *scr)

        @pl.when(last)
        def _():
            rider.wait(r_ins, r_outs, r_scr)

    res = pl.pallas_call(
        carried, name=name,
        grid=grid,
        in_specs=list(in_specs) + rider.in_specs,
        out_specs=list(out_specs) + rider.out_specs,
        out_shape=list(out_shape) + rider.out_shapes,
        scratch_shapes=list(scratch_shapes) + rider.scratch,
        input_output_aliases={n_in + i: n_out + o for i, o in rider.aliases.items()},
        compiler_params=params)(*args, *_operands(rider))
    return res[:n_out], res[n_out:]


def _run_comm(comm, name):
    n_in, n_out = len(comm.arrays), len(comm.out_shapes)

    def body(*refs):
        ins, outs, scr = refs[:n_in], refs[n_in:n_in + n_out], refs[n_in + n_out:]
        comm.start(ins, outs, scr)
        comm.wait(ins, outs, scr)

    return pl.pallas_call(
        body, name=name,
        in_specs=comm.in_specs, out_specs=comm.out_specs, out_shape=comm.out_shapes,
        scratch_shapes=comm.scratch, input_output_aliases=comm.aliases,
        compiler_params=pltpu.CompilerParams(vmem_limit_bytes=24 * MIB))(*_operands(comm))


def _both(a, b):
    ni, no, ns = len(a.arrays), len(a.out_shapes), len(a.scratch)

    def start(ins, outs, scr):
        a.start(ins[:ni], outs[:no], scr[:ns])
        b.start(ins[ni:], outs[no:], scr[ns:])

    def wait(ins, outs, scr):
        a.wait(ins[:ni], outs[:no], scr[:ns])
        b.wait(ins[ni:], outs[no:], scr[ns:])

    aliases = dict(a.aliases)
    aliases.update({ni + i: no + o for i, o in b.aliases.items()})
    return _Comm(a.arrays + b.arrays, a.in_specs + b.in_specs, a.out_shapes + b.out_shapes,
                 a.out_specs + b.out_specs, a.scratch + b.scratch, start, wait, aliases)


def _place():
    return lax.axis_index("x"), lax.axis_index("y"), lax.axis_index("c")


def _peer(k):
    x, y, c = _place()
    px, py, pc = x ^ ((k >> 2) & 1), y ^ ((k >> 1) & 1), c ^ (k & 1)
    return (px, py, pc), 4 * px + 2 * py + pc


SIBLING = 1
SAME_CORE = (2, 4, 6)
EVERYONE = tuple(range(1, N_DEV))


def _gather_comm(shards, views, out_shapes, relations):
    na = len(shards)

    def copies(ins, outs, scr):
        send_sems, recv_sems, _ = scr
        _, me = _peer(0)
        out = []
        for a in range(na):
            for k in relations[a]:
                peer, theirs = _peer(k)
                send = functools.partial(
                    pltpu.make_async_remote_copy,
                    src_ref=ins[a], dst_ref=views[a](outs[a], me),
                    send_sem=send_sems.at[7 * a + k - 1], recv_sem=recv_sems.at[7 * a + k - 1],
                    device_id=peer, device_id_type=MESH)
                recv = functools.partial(
                    pltpu.make_async_remote_copy,
                    src_ref=ins[a], dst_ref=views[a](outs[a], theirs),
                    send_sem=send_sems.at[7 * a + k - 1], recv_sem=recv_sems.at[7 * a + k - 1],
                    device_id=peer, device_id_type=MESH)
                out.append((send, recv))
        return out

    def local(ins, outs, scr):
        _, me = _peer(0)
        return [pltpu.make_async_copy(ins[a], views[a](outs[a], me), scr[2].at[a]) for a in range(na)]

    def start(ins, outs, scr):
        for cp in local(ins, outs, scr):
            cp.start()
        for send, _ in copies(ins, outs, scr):
            send().start()

    def wait(ins, outs, scr):
        for _, recv in copies(ins, outs, scr):
            recv().wait_recv()
        for send, _ in copies(ins, outs, scr):
            send().wait_send()
        for cp in local(ins, outs, scr):
            cp.wait()

    return _Comm(shards, [HBM_SPEC] * na, out_shapes, [HBM_SPEC] * na,
                 [pltpu.SemaphoreType.DMA((7 * na,)), pltpu.SemaphoreType.DMA((7 * na,)),
                  pltpu.SemaphoreType.DMA((na,))], start, wait)


def _forward_comm(gathered, views):
    na = len(gathered)
    shapes = [jax.ShapeDtypeStruct(g.shape, g.dtype) for g in gathered]

    def copies(outs, scr):
        send_sems, recv_sems = scr
        sibling, _ = _peer(SIBLING)
        out = []
        for a in range(na):
            for n, k in enumerate(SAME_CORE):
                _, mine = _peer(k)
                _, theirs = _peer(k ^ SIBLING)
                send = functools.partial(
                    pltpu.make_async_remote_copy,
                    src_ref=views[a](outs[a], mine), dst_ref=views[a](outs[a], mine),
                    send_sem=send_sems.at[3 * a + n], recv_sem=recv_sems.at[3 * a + n],
                    device_id=sibling, device_id_type=MESH)
                recv = functools.partial(
                    pltpu.make_async_remote_copy,
                    src_ref=views[a](outs[a], mine), dst_ref=views[a](outs[a], theirs),
                    send_sem=send_sems.at[3 * a + n], recv_sem=recv_sems.at[3 * a + n],
                    device_id=sibling, device_id_type=MESH)
                out.append((send, recv))
        return out

    def start(ins, outs, scr):
        for send, _ in copies(outs, scr):
            send().start()

    def wait(ins, outs, scr):
        for _, recv in copies(outs, scr):
            recv().wait_recv()
        for send, _ in copies(outs, scr):
            send().wait_send()

    return _Comm(gathered, [HBM_SPEC] * na, shapes, [HBM_SPEC] * na,
                 [pltpu.SemaphoreType.DMA((3 * na,)), pltpu.SemaphoreType.DMA((3 * na,))], start, wait,
                 aliases={a: a for a in range(na)})


Y_NEIGHBOUR, X_NEIGHBOUR, DIAGONAL = SAME_CORE


def _all_gather(shards, views, out_shapes, name):
    na = len(shards)
    near = (SIBLING, Y_NEIGHBOUR, X_NEIGHBOUR)
    level1 = _gather_comm(shards, views, out_shapes, [near] * na)

    def body(*refs):
        ins, outs = refs[:na], refs[na:2 * na]
        send_sems, recv_sems, local_sems, fwd_send, fwd_recv, relay_send, relay_recv = refs[2 * na:]
        sibling, _ = _peer(SIBLING)
        c = lax.axis_index("c")
        level1.start(ins, outs, (send_sems, recv_sems, local_sems))

        def block_copy(a, block, to, send_sem, recv_sem):
            return pltpu.make_async_remote_copy(
                src_ref=views[a](outs[a], block), dst_ref=views[a](outs[a], block),
                send_sem=send_sem, recv_sem=recv_sem, device_id=to, device_id_type=MESH)

        def to_sibling(a, n, k):
            _, mine = _peer(k)
            _, theirs = _peer(k ^ SIBLING)
            fwd = block_copy(a, mine, sibling, fwd_send.at[3 * a + n], fwd_recv.at[3 * a + n])
            fwd.start()
            return fwd, block_copy(a, theirs, sibling, fwd_send.at[3 * a + n], fwd_recv.at[3 * a + n])

        passed, landing = [], []
        for a in range(na):
            for n, k in enumerate((Y_NEIGHBOUR, X_NEIGHBOUR)):
                peer, origin = _peer(k)
                pltpu.make_async_remote_copy(
                    src_ref=ins[a], dst_ref=views[a](outs[a], origin),
                    send_sem=send_sems.at[7 * a + k - 1], recv_sem=recv_sems.at[7 * a + k - 1],
                    device_id=peer, device_id_type=MESH).wait_recv()

                @pl.when(c == (0 if k == X_NEIGHBOUR else 1))
                def _():
                    other, _ = _peer(DIAGONAL ^ k)
                    block_copy(a, origin, other, relay_send.at[a], relay_recv.at[a]).start()

                fwd, lands = to_sibling(a, n, k)
                passed.append(fwd)
                landing.append(lands)
            _, far = _peer(DIAGONAL)
            block_copy(a, far, sibling, relay_send.at[a], relay_recv.at[a]).wait_recv()
            fwd, lands = to_sibling(a, 2, DIAGONAL)
            passed.append(fwd)
            landing.append(lands)
        for a in range(na):
            _, theirs = _peer(SIBLING)
            pltpu.make_async_remote_copy(
                src_ref=ins[a], dst_ref=views[a](outs[a], theirs),
                send_sem=send_sems.at[7 * a + SIBLING - 1], recv_sem=recv_sems.at[7 * a + SIBLING - 1],
                device_id=sibling, device_id_type=MESH).wait_recv()
        for cp in landing:
            cp.wait_recv()
        for cp in passed:
            cp.wait_send()
        _, me = _peer(0)
        for a in range(na):
            block_copy(a, me, sibling, relay_send.at[a], relay_recv.at[a]).wait_send()
            for k in near:
                peer, _ = _peer(k)
                pltpu.make_async_remote_copy(
                    src_ref=ins[a], dst_ref=views[a](outs[a], me),
                    send_sem=send_sems.at[7 * a + k - 1], recv_sem=recv_sems.at[7 * a + k - 1],
                    device_id=peer, device_id_type=MESH).wait_send()
            pltpu.make_async_copy(ins[a], views[a](outs[a], me), local_sems.at[a]).wait()

    return pl.pallas_call(
        body, name=name,
        in_specs=[HBM_SPEC] * na, out_specs=[HBM_SPEC] * na, out_shape=out_shapes,
        scratch_shapes=level1.scratch + [pltpu.SemaphoreType.DMA((3 * na,)), pltpu.SemaphoreType.DMA((3 * na,)),
                                         pltpu.SemaphoreType.DMA((na,)), pltpu.SemaphoreType.DMA((na,))],
    )(*[_in_hbm(s) for s in shards])


def _sibling_comm(grads):
    na = len(grads)
    shapes = [jax.ShapeDtypeStruct(g.shape[:2] + g.shape[3:], g.dtype) for g in grads]

    def copies(ins, outs, scr):
        x, y, c = _place()
        return [pltpu.make_async_remote_copy(
            src_ref=ins[a].at[:, :, 1 - c], dst_ref=outs[a],
            send_sem=scr[0].at[a], recv_sem=scr[1].at[a],
            device_id=(x, y, 1 - c), device_id_type=MESH) for a in range(na)]

    def start(ins, outs, scr):
        for cp in copies(ins, outs, scr):
            cp.start()

    def wait(ins, outs, scr):
        for cp in copies(ins, outs, scr):
            cp.wait()

    return _Comm(grads, [HBM_SPEC] * na, shapes, [HBM_SPEC] * na,
                 [pltpu.SemaphoreType.DMA((na,)), pltpu.SemaphoreType.DMA((na,))], start, wait)


def _chips_comm(sums, every_device=False):
    na = len(sums)
    shapes = [jax.ShapeDtypeStruct(s.shape, s.dtype) for s in sums]
    relations = EVERYONE if every_device else SAME_CORE
    nr = len(relations)

    def block(px, py, pc):
        return 4 * px + 2 * py + pc if every_device else 2 * px + py

    def copies(ins, outs, scr):
        mine = block(*_place())
        out = []
        for a in range(na):
            for n, k in enumerate(relations):
                peer, _ = _peer(k)
                theirs = block(*peer)
                send = functools.partial(
                    pltpu.make_async_remote_copy,
                    src_ref=ins[a].at[:, theirs], dst_ref=outs[a].at[:, mine],
                    send_sem=scr[0].at[nr * a + n], recv_sem=scr[1].at[nr * a + n],
                    device_id=peer, device_id_type=MESH)
                recv = functools.partial(
                    pltpu.make_async_remote_copy,
                    src_ref=ins[a].at[:, mine], dst_ref=outs[a].at[:, theirs],
                    send_sem=scr[0].at[nr * a + n], recv_sem=scr[1].at[nr * a + n],
                    device_id=peer, device_id_type=MESH)
                out.append((send, recv))
        return out

    def local(ins, outs, scr):
        mine = block(*_place())
        return [pltpu.make_async_copy(ins[a].at[:, mine], outs[a].at[:, mine], scr[2].at[a]) for a in range(na)]

    def start(ins, outs, scr):
        for cp in local(ins, outs, scr):
            cp.start()
        for send, _ in copies(ins, outs, scr):
            send().start()

    def wait(ins, outs, scr):
        for _, recv in copies(ins, outs, scr):
            recv().wait_recv()
        for send, _ in copies(ins, outs, scr):
            send().wait_send()
        for cp in local(ins, outs, scr):
            cp.wait()

    return _Comm(sums, [HBM_SPEC] * na, shapes, [HBM_SPEC] * na,
                 [pltpu.SemaphoreType.DMA((nr * na,)), pltpu.SemaphoreType.DMA((nr * na,)),
                  pltpu.SemaphoreType.DMA((na,))], start, wait)


def _small_sum_comm(arrays):
    na = len(arrays)

    def copies(ins, scr):
        bufs, send_sems, recv_sems = scr[:na], scr[na], scr[na + 1]
        _, me = _peer(0)
        out = []
        for a in range(na):
            for k in EVERYONE:
                peer, theirs = _peer(k)
                sems = dict(send_sem=send_sems.at[7 * a + k - 1], recv_sem=recv_sems.at[7 * a + k - 1])
                send = functools.partial(
                    pltpu.make_async_remote_copy,
                    src_ref=ins[a], dst_ref=bufs[a].at[me], device_id=peer, device_id_type=MESH, **sems)
                recv = functools.partial(
                    pltpu.make_async_remote_copy,
                    src_ref=ins[a], dst_ref=bufs[a].at[theirs], device_id=peer, device_id_type=MESH, **sems)
                out.append((send, recv))
        return out

    def start(ins, outs, scr):
        _, me = _peer(0)
        for a in range(na):
            scr[a][me] = ins[a][...]
        for send, _ in copies(ins, scr):
            send().start()

    def wait(ins, outs, scr):
        for _, recv in copies(ins, scr):
            recv().wait_recv()
        for send, _ in copies(ins, scr):
            send().wait_send()
        for a in range(na):
            acc = scr[a][0]
            for j in range(1, N_DEV):
                acc = acc + scr[a][j]
            outs[a][...] = acc

    return _Comm(arrays, [VMEM_SPEC] * na, [jax.ShapeDtypeStruct(s.shape, F32) for s in arrays], [VMEM_SPEC] * na,
                 [pltpu.VMEM((N_DEV,) + s.shape, F32) for s in arrays]
                 + [pltpu.SemaphoreType.DMA((7 * na,)), pltpu.SemaphoreType.DMA((7 * na,))], start, wait)


def _prep_weights(ffn1, ffn2, w_in, w_out, name):
    rf, D = ffn1[2].shape
    ri, ro = w_in.shape[1], w_out.shape[0]

    def body(g1, u1, d1, g2, u2, d2, wi, wo, p1_ref, p2_ref, pi_ref, po_ref):
        for p_ref, shards in ((p1_ref, (g1, u1, d1)), (p2_ref, (g2, u2, d2))):
            for k, shard in enumerate(shards):
                p_ref[k] = shard[...].astype(BF16)
        pi_ref[...] = wi[...].T.astype(BF16)
        po_ref[...] = wo[...].astype(BF16)

    args = (*ffn1, *ffn2, w_in, w_out)
    whole = lambda shape: pl.BlockSpec(shape, lambda i: (0,) * len(shape))
    out_shapes = [(3, rf, D), (3, rf, D), (ri, D), (ro, D)]
    return _call(
        body, name=name, grid=(1,),
        in_specs=[whole(a.shape) for a in args], out_specs=[whole(s) for s in out_shapes],
        out_shape=[jax.ShapeDtypeStruct(s, BF16) for s in out_shapes],
        scratch_shapes=[], vmem_mib=48, args=args)


def _load_weights(w_hbm, w_vmem, sem):
    @pl.when(pl.program_id(0) == 0)
    def _():
        copies = [pltpu.make_async_copy(w_hbm.at[k], w_vmem.at[k], sem.at[k]) for k in range(3)]
        for cp in copies:
            cp.start()
        for cp in copies:
            cp.wait()


def _ffn_fwd(x, g, w3, w_gu, tm, cf, name, rider=None, head=None):
    T, D = x.shape
    F = w3.shape[1]
    n_head = 0 if head is None else 2

    def body(x_ref, g_ref, w_hbm, wgu_hbm, *refs):
        head_refs, refs = refs[:n_head], refs[n_head:]
        if head is None:
            (xo_ref, h_ref, dau_ref, dag_ref, act_ref, wgu, wd, sem) = refs
        else:
            (dx_ref, dob_ref, dgf_ref, loss_ref, h_ref, dau_ref, dag_ref, act_ref, wgu, wd, sem) = refs

        @pl.when(pl.program_id(0) == 0)
        def _():
            copies = [pltpu.make_async_copy(wgu_hbm.at[0], wgu.at[0], sem.at[0]),
                      pltpu.make_async_copy(wgu_hbm.at[1], wgu.at[1], sem.at[1]),
                      pltpu.make_async_copy(w_hbm.at[2], wd, sem.at[2])]
            for cp in copies:
                cp.start()
            for cp in copies:
                cp.wait()

        xhat, _ = _rms_stats(x_ref[...])
        hb = (xhat * g_ref[...]).astype(BF16)
        h_ref[...] = hb
        for lo in range(0, F, cf):
            gate = _dot(hb, wgu[0, :, lo:lo + cf], NN)
            up = _dot(hb, wgu[1, :, lo:lo + cf], NN)
            sig = _sigmoid(gate)
            silu = gate * sig
            dau_ref[:, lo:lo + cf] = silu.astype(BF16)
            dag_ref[:, lo:lo + cf] = (up * (sig * (1.0 + gate * (1.0 - sig)))).astype(BF16)
            act_ref[:, lo:lo + cf] = (silu * up).astype(BF16)
        x_out = x_ref[...] + FFN_RES * _dot(act_ref[...], wd[...], NN)
        if head is None:
            xo_ref[...] = x_out
            return

        @pl.when(pl.program_id(0) == 0)
        def _():
            dgf_ref[...] = jnp.zeros_like(dgf_ref)
            loss_ref[...] = jnp.zeros_like(loss_ref)

        gf_ref, tgt_ref = head_refs
        yhat, rstd = _rms_stats(x_out)
        gf = gf_ref[...]
        err = yhat * gf - tgt_ref[...]
        loss_ref[...] += (0.5 / D) * jnp.sum(err * err)
        dx, dgf = _rms_bwd(yhat, rstd, gf, err * (1.0 / D))
        dx_ref[...] = dx
        dob_ref[...] = (FFN_RES * dx).astype(BF16)
        dgf_ref[...] += dgf

    row = pl.BlockSpec((tm, D), lambda i: (i, 0))
    hid = pl.BlockSpec((tm, F), lambda i: (i, 0))
    vec = pl.BlockSpec((1, D), lambda i: (0, 0))
    row_f32, row_bf16 = jax.ShapeDtypeStruct((T, D), F32), jax.ShapeDtypeStruct((T, D), BF16)
    if head is None:
        first_specs, first_shapes = [row], [row_f32]
    else:
        first_specs = [row, row, vec, pl.BlockSpec((1, 128), lambda i: (0, 0))]
        first_shapes = [row_f32, row_bf16, jax.ShapeDtypeStruct((1, D), F32), jax.ShapeDtypeStruct((1, 128), F32)]
    return _call(
        body, name=name, grid=(T // tm,),
        in_specs=[row, vec, HBM_SPEC, HBM_SPEC] + ([] if head is None else [vec, row]),
        out_specs=first_specs + [row, hid, hid, hid],
        out_shape=first_shapes + [row_bf16] + [jax.ShapeDtypeStruct((T, F), BF16)] * 3,
        scratch_shapes=[pltpu.VMEM((2, D, F), BF16), pltpu.VMEM((F, D), BF16), pltpu.SemaphoreType.DMA((3,))],
        vmem_mib=60, args=(x, g, w3, w_gu) + (() if head is None else tuple(head)), rider=rider)


def _ffn_dgrad(dout, x, g, dau, dag, w3, tm, cf, name, rider=None):
    T, D = x.shape
    F = w3.shape[1]

    def body(do_ref, x_ref, g_ref, dau_ref, dag_ref, w_hbm, dx_ref, dgate_ref, dup_ref, dg_ref, wv, sem):
        _load_weights(w_hbm, wv, sem)

        @pl.when(pl.program_id(0) == 0)
        def _():
            dg_ref[...] = jnp.zeros_like(dg_ref)

        dob = (FFN_RES * do_ref[...]).astype(BF16)
        for lo in range(0, F, cf):
            dact = _dot(dob, wv[2, lo:lo + cf, :], NT)
            dup_ref[:, lo:lo + cf] = (dact * dau_ref[:, lo:lo + cf].astype(F32)).astype(BF16)
            dgate_ref[:, lo:lo + cf] = (dact * dag_ref[:, lo:lo + cf].astype(F32)).astype(BF16)
        dh = _dot(dgate_ref[...], wv[0], NN) + _dot(dup_ref[...], wv[1], NN)
        xhat, rstd = _rms_stats(x_ref[...])
        dx, dg = _rms_bwd(xhat, rstd, g_ref[...], dh)
        dx_ref[...] = do_ref[...] + dx
        dg_ref[...] += dg

    row = pl.BlockSpec((tm, D), lambda i: (i, 0))
    hid = pl.BlockSpec((tm, F), lambda i: (i, 0))
    vec = pl.BlockSpec((1, D), lambda i: (0, 0))
    return _call(
        body, name=name, grid=(T // tm,),
        in_specs=[row, row, vec, hid, hid, HBM_SPEC],
        out_specs=[row, hid, hid, vec],
        out_shape=[jax.ShapeDtypeStruct((T, D), F32), jax.ShapeDtypeStruct((T, F), BF16),
                   jax.ShapeDtypeStruct((T, F), BF16), jax.ShapeDtypeStruct((1, D), F32)],
        scratch_shapes=[pltpu.VMEM((3, F, D), BF16), pltpu.SemaphoreType.DMA((3,))],
        vmem_mib=52, args=(dout, x, g, dau, dag, w3), rider=rider)


def _wgrad(lhs, rhs, tm, tf, name, rider=None):
    T, F = lhs[0].shape
    D = rhs.shape[1]
    K = len(lhs)

    def body(*refs):
        lhs_refs, rhs_ref, dw_ref, accs = refs[:K], refs[K], refs[K + 1], refs[K + 2:]
        i = pl.program_id(1)

        @pl.when(i == 0)
        def _():
            for acc in accs:
                acc[...] = jnp.zeros_like(acc)

        rv = rhs_ref[...]
        for acc, lhs_ref in zip(accs, lhs_refs):
            acc[...] += _dot(lhs_ref[...], rv, TN)

        @pl.when(i == pl.num_programs(1) - 1)
        def _():
            for k, acc in enumerate(accs):
                dw_ref[k] = acc[...].astype(BF16)

    hid = pl.BlockSpec((tm, tf), lambda f, i: (i, f))
    return _call(
        body, name=name, grid=(F // tf, T // tm),
        in_specs=[hid] * K + [pl.BlockSpec((tm, D), lambda f, i: (i, 0))],
        out_specs=[pl.BlockSpec((K, tf, D), lambda f, i: (0, f, 0))],
        out_shape=[jax.ShapeDtypeStruct((K, F, D), BF16)],
        scratch_shapes=[pltpu.VMEM((tf, D), F32)] * K,
        vmem_mib=56, args=(*lhs, rhs), rider=rider)


def _lru_gates(xr, bda_ref, bdx_ref, vec_ref):
    xrb = xr.astype(BF16)
    r = _sigmoid(_dot(xrb, bda_ref[...], NN) + vec_ref[V_BA:V_BA + 1, :])
    ig = _sigmoid(_dot(xrb, bdx_ref[...], NN) + vec_ref[V_BX:V_BX + 1, :])
    sp = _softplus_neg(vec_ref[V_LAM:V_LAM + 1, :])
    log_a = (-LRU_C * sp) * r
    a = jnp.exp(log_a)
    mult = jnp.sqrt(_neg_expm1(2.0 * log_a))
    return xrb, r, ig, sp, a, mult


def _layernorm_stats(u1):
    xc = u1 - jnp.mean(u1, axis=-1, keepdims=True)
    rs = lax.rsqrt(jnp.mean(xc * xc, axis=-1, keepdims=True) + LN_EPS)
    return xc * rs, rs


def _mix_core_fwd(x1, g, w_in_t, w_out, bda, bdx, cw, lw, vec, tm, name, rider=None):
    T, D = x1.shape
    W = cw.shape[1]
    assert tm >= CONV_HALO and w_in_t.shape[0] == 4 * W

    def body(x1_ref, g_ref, wi_ref, wo_ref, bda_ref, bdx_ref, cw_ref, lw_ref, vec_ref,
             x2_ref, z_ref, mix_ref, u1_ref, xr_ref, hst_ref, ubuf, rbuf, hc):
        @pl.when(pl.program_id(0) == 0)
        def _():
            ubuf[0:CONV_HALO, :] = jnp.zeros((CONV_HALO, W), F32)
            rbuf[0:LRU_HALO, :] = jnp.zeros((LRU_HALO, W), F32)
            hc[...] = jnp.zeros_like(hc)

        xhat, _ = _rms_stats(x1_ref[...])
        z_ref[...] = _dot((xhat * g_ref[...]).astype(BF16), wi_ref[...], NT)

        ubuf[CONV_HALO:CONV_HALO + tm, :] = z_ref[:, 0:W] * _sigmoid(z_ref[:, W:2 * W])
        u1 = jnp.zeros((tm, W), F32) + vec_ref[V_CB:V_CB + 1, :]
        base = CONV_HALO - (CONV_K - 1)
        for off, win in _row_windows(ubuf, tm, range(base, base + CONV_K)):
            u1 = u1 + cw_ref[off - base:off - base + 1, :] * win
        ubuf[0:CONV_HALO, :] = ubuf[tm:tm + CONV_HALO, :]
        u1_ref[...] = u1
        xh, _ = _layernorm_stats(u1)
        u2 = xh * vec_ref[V_LNG:V_LNG + 1, :] + vec_ref[V_LNB:V_LNB + 1, :]
        ub = (u2 * _sigmoid(u2)).astype(BF16)
        mix_ref[:, 0:W] = ub

        rbuf[LRU_HALO:LRU_HALO + tm, :] = z_ref[:, 2 * W:3 * W]
        xr = jnp.zeros((tm, W), F32) + vec_ref[V_LCB:V_LCB + 1, :]
        for k in range(LRU_K):
            off = LRU_HALO - (LRU_K - 1) + k
            xr = xr + lw_ref[k:k + 1, :] * rbuf[off:off + tm, :]
        rbuf[0:LRU_HALO, :] = rbuf[tm:tm + LRU_HALO, :]
        xr_ref[...] = xr
        _, _, ig, _, a, mult = _lru_gates(xr, bda_ref, bdx_ref, vec_ref)
        hc[0:1, :] = _scan_rows(a, mult * (ig * xr), hc[0:1, :], hst_ref)
        gl, _ = _gelu_parts(z_ref[:, 3 * W:4 * W])
        yb = (hst_ref[...] * gl).astype(BF16)
        mix_ref[:, W:2 * W] = yb

        x2_ref[...] = x1_ref[...] + _dot(ub, wo_ref[0:W, :], NN) + _dot(yb, wo_ref[W:2 * W, :], NN)

    full = lambda a: pl.BlockSpec(a.shape, lambda i: (0,) * a.ndim)
    tile = lambda n: pl.BlockSpec((tm, n), lambda i: (i, 0))
    return _call(
        body, name=name, grid=(T // tm,),
        in_specs=[tile(D), full(g), full(w_in_t), full(w_out), full(bda), full(bdx), full(cw), full(lw), full(vec)],
        out_specs=[tile(D), tile(4 * W), tile(2 * W), tile(W), tile(W), tile(W)],
        out_shape=[jax.ShapeDtypeStruct((T, D), F32), jax.ShapeDtypeStruct((T, 4 * W), F32),
                   jax.ShapeDtypeStruct((T, 2 * W), BF16), jax.ShapeDtypeStruct((T, W), F32),
                   jax.ShapeDtypeStruct((T, W), F32), jax.ShapeDtypeStruct((T, W), F32)],
        scratch_shapes=[pltpu.VMEM((tm + CONV_HALO, W), F32), pltpu.VMEM((tm + LRU_HALO, W), F32),
                        pltpu.VMEM((8, W), F32)],
        vmem_mib=56, args=(x1, g, w_in_t, w_out, bda, bdx, cw, lw, vec), rider=rider)


def _mix_bwd(dx2, z, u1, xr, hst, x1, mix, g, w_in_t, w_out, bda, bdx, cw, lw, vec, tm, name, rider=None):
    T, D = dx2.shape
    W = cw.shape[1]
    nt = T // tm
    assert tm >= CONV_HALO and tm % CONV_HALO == 0

    def body(dx_ref, z_ref, zh_ref, u1_ref, xr_ref, h_ref, hh_ref, wo_ref, bda_ref, bdx_ref, cw_ref, lw_ref, vec_ref,
             x1_ref, mix_ref, g_ref, wi_ref,
             dx1_ref, sg_ref, dbda_ref, dbdx_ref, dob_ref, dg_ref, dwi_ref, dwo_ref,
             u0buf, du1buf, rxbuf, dxrbuf, gbuf, gc, spacc, dz_ref, ai_ref, ao_ref):
        i = pl.program_id(0)
        first = i == nt - 1
        row = lax.broadcasted_iota(jnp.int32, (tm, W), 0)

        @pl.when(i == 0)
        def _():
            sg_ref[...] = jnp.zeros_like(sg_ref)
            dbda_ref[...] = jnp.zeros_like(dbda_ref)
            dbdx_ref[...] = jnp.zeros_like(dbdx_ref)
            du1buf[tm:tm + CONV_HALO, :] = jnp.zeros((CONV_HALO, W), F32)
            dxrbuf[tm:tm + LRU_HALO, :] = jnp.zeros((LRU_HALO, W), F32)
            gc[...] = jnp.zeros_like(gc)
            spacc[...] = jnp.zeros_like(spacc)
            dg_ref[...] = jnp.zeros_like(dg_ref)
            ai_ref[...] = jnp.zeros_like(ai_ref)
            ao_ref[...] = jnp.zeros_like(ao_ref)

        def accum(r, val):
            sg_ref[r:r + 1, :] += jnp.sum(val, axis=0, keepdims=True)

        x1hat, x1rstd = _rms_stats(x1_ref[...])
        gain = g_ref[...]
        hb = (x1hat * gain).astype(BF16)

        def in_proj_bwd(lo, hi):
            dzb = dz_ref[:, lo:hi]
            ai_ref[lo:hi, :] += _dot(dzb, hb, TN)
            return _dot(dzb, wi_ref[lo:hi, :], NN)

        dxb = dx_ref[...].astype(BF16)
        ao_ref[...] += _dot(mix_ref[...], dxb, TN)
        dmix = _dot(dxb, wo_ref[...], NT)
        d_u = dmix[:, 0:W]
        d_yr = dmix[:, W:2 * W]

        xh, rs = _layernorm_stats(u1_ref[...])
        ln_g = vec_ref[V_LNG:V_LNG + 1, :]
        u2 = xh * ln_g + vec_ref[V_LNB:V_LNB + 1, :]
        s2 = _sigmoid(u2)
        d_u2 = d_u * (s2 * (1.0 + u2 * (1.0 - s2)))
        accum(G_LNG, d_u2 * xh)
        accum(G_LNB, d_u2)
        d_xh = d_u2 * ln_g
        d_u1 = rs * (d_xh - jnp.mean(d_xh, axis=-1, keepdims=True)
                     - xh * jnp.mean(d_xh * xh, axis=-1, keepdims=True))
        accum(G_CB, d_u1)
        halo_on = jnp.where(first, 0.0, 1.0)
        u0buf[0:CONV_HALO, :] = halo_on * (zh_ref[:, 0:W] * _sigmoid(zh_ref[:, W:2 * W]))
        cv = z_ref[:, 0:W]
        sgc = _sigmoid(z_ref[:, W:2 * W])
        u0buf[CONV_HALO:CONV_HALO + tm, :] = cv * sgc
        du1buf[0:tm, :] = d_u1
        base = CONV_HALO - (CONV_K - 1)
        for off, win in _row_windows(u0buf, tm, range(base, base + CONV_K)):
            accum(G_CW + off - base, d_u1 * win)
        d_u0 = jnp.zeros((tm, W), F32)
        for off, win in _row_windows(du1buf, tm, range(0, CONV_K)):
            d_u0 = d_u0 + cw_ref[CONV_K - 1 - off:CONV_K - off, :] * win
        du1buf[tm:tm + CONV_HALO, :] = du1buf[0:CONV_HALO, :]
        dz_ref[:, 0:W] = (d_u0 * sgc).astype(BF16)
        dz_ref[:, W:2 * W] = (d_u0 * cv * (sgc * (1.0 - sgc))).astype(BF16)
        dh = in_proj_bwd(0, 2 * W)

        xrv = xr_ref[...]
        xrb, r, ig, sp, a, mult = _lru_gates(xrv, bda_ref, bdx_ref, vec_ref)
        h = h_ref[...]
        gl, dgl = _gelu_parts(z_ref[:, 3 * W:4 * W])
        dz_ref[:, 3 * W:4 * W] = (d_yr * h * dgl).astype(BF16)
        dh = dh + in_proj_bwd(3 * W, 4 * W)
        a_next = jnp.where(row == tm - 1, 1.0, pltpu.roll(a, tm - 1, 0))
        g_first = _scan_rows(a_next, d_yr * gl, gc[0:1, :], gbuf, reverse=True)
        g = gbuf[...]
        gc[0:1, :] = a[0:1, :] * g_first
        hprev = jnp.where(row == 0, halo_on * hh_ref[LRU_HALO - 1:LRU_HALO, :], pltpu.roll(h, 1, 0))
        d_log_a = (g * hprev) * a - (g * ig * xrv) * (a * a) / mult
        d_ig = g * mult * xrv
        d_xr = g * mult * ig
        spacc[0:1, :] += jnp.sum(d_log_a * r, axis=0, keepdims=True)
        d_pa32 = (d_log_a * (-LRU_C * sp)) * (r * (1.0 - r))
        d_px32 = d_ig * (ig * (1.0 - ig))
        accum(G_BA, d_pa32)
        accum(G_BX, d_px32)
        d_pa = d_pa32.astype(BF16)
        d_px = d_px32.astype(BF16)
        d_xr = d_xr + _dot(d_pa, bda_ref[...], NT) + _dot(d_px, bdx_ref[...], NT)
        dbda_ref[...] += _dot(xrb, d_pa, TN)
        dbdx_ref[...] += _dot(xrb, d_px, TN)
        accum(G_LCB, d_xr)
        rxbuf[0:LRU_HALO, :] = halo_on * zh_ref[CONV_HALO - LRU_HALO:CONV_HALO, 2 * W:3 * W]
        rxbuf[LRU_HALO:LRU_HALO + tm, :] = z_ref[:, 2 * W:3 * W]
        dxrbuf[0:tm, :] = d_xr
        d_rx = jnp.zeros((tm, W), F32)
        for k in range(LRU_K):
            off = LRU_HALO - (LRU_K - 1) + k
            accum(G_LW + k, d_xr * rxbuf[off:off + tm, :])
            d_rx = d_rx + lw_ref[k:k + 1, :] * dxrbuf[LRU_K - 1 - k:LRU_K - 1 - k + tm, :]
        dxrbuf[tm:tm + LRU_HALO, :] = dxrbuf[0:LRU_HALO, :]
        dz_ref[:, 2 * W:3 * W] = d_rx.astype(BF16)
        dh = dh + in_proj_bwd(2 * W, 3 * W)

        dx, dg = _rms_bwd(x1hat, x1rstd, gain, dh)
        dx1 = dx_ref[...] + dx
        dx1_ref[...] = dx1
        dob_ref[...] = (FFN_RES * dx1).astype(BF16)
        dg_ref[...] += dg

        @pl.when(first)
        def _():
            lam = vec_ref[V_LAM:V_LAM + 1, :]
            sg_ref[G_LAM:G_LAM + 1, :] = LRU_C * _sigmoid(-lam) * spacc[0:1, :]
            dwi_ref[...] = ai_ref[...].astype(BF16)
            dwo_ref[...] = ao_ref[...].astype(BF16)

    full = lambda a: pl.BlockSpec(a.shape, lambda i: (0,) * a.ndim)
    tile = lambda n: pl.BlockSpec((tm, n), lambda i: (nt - 1 - i, 0))
    halo = lambda rows, n: pl.BlockSpec(
        (rows, n), lambda i: (jnp.maximum((nt - 1 - i) * (tm // rows) - 1, 0), 0))
    const = lambda r, c: pl.BlockSpec((r, c), lambda i: (0, 0))
    return _call(
        body, name=name, grid=(nt,),
        in_specs=[tile(D), tile(4 * W), halo(CONV_HALO, 4 * W), tile(W), tile(W), tile(W), halo(LRU_HALO, W),
                  full(w_out), full(bda), full(bdx), full(cw), full(lw), full(vec),
                  tile(D), tile(2 * W), full(g), full(w_in_t)],
        out_specs=[tile(D), const(G_ROWS, W), const(W, W), const(W, W),
                   tile(D), const(1, D), const(4 * W, D), const(2 * W, D)],
        out_shape=[jax.ShapeDtypeStruct((T, D), F32), jax.ShapeDtypeStruct((G_ROWS, W), F32),
                   jax.ShapeDtypeStruct((W, W), F32), jax.ShapeDtypeStruct((W, W), F32),
                   jax.ShapeDtypeStruct((T, D), BF16), jax.ShapeDtypeStruct((1, D), F32),
                   jax.ShapeDtypeStruct((4 * W, D), BF16), jax.ShapeDtypeStruct((2 * W, D), BF16)],
        scratch_shapes=[pltpu.VMEM((tm + CONV_HALO, W), F32), pltpu.VMEM((tm + CONV_HALO, W), F32),
                        pltpu.VMEM((tm + LRU_HALO, W), F32), pltpu.VMEM((tm + LRU_HALO, W), F32),
                        pltpu.VMEM((tm, W), F32), pltpu.VMEM((8, W), F32), pltpu.VMEM((8, W), F32),
                        pltpu.VMEM((tm, 4 * W), BF16), pltpu.VMEM((4 * W, D), F32), pltpu.VMEM((2 * W, D), F32)],
        vmem_mib=60, args=(dx2, z, z, u1, xr, hst, hst, w_out, bda, bdx, cw, lw, vec, x1, mix, g, w_in_t),
        rider=rider)


def _pair_add(full, recv, name):
    K, _, _, rows, D = full.shape

    def body(c_ref, a_ref, b_ref, o_ref):
        o_ref[...] = (a_ref[...].astype(F32) + b_ref[...].astype(F32)).astype(BF16)

    c = lax.axis_index("c").astype(jnp.int32).reshape((1,))
    return _call(
        body, name=name, grid=(K, N_CHIP), num_scalar_prefetch=1,
        in_specs=[pl.BlockSpec((None, None, None, rows, D), lambda k, q, c_ref: (k, q, c_ref[0], 0, 0)),
                  pl.BlockSpec((None, None, rows, D), lambda k, q, c_ref: (k, q, 0, 0))],
        out_specs=pl.BlockSpec((None, None, rows, D), lambda k, q, c_ref: (k, q, 0, 0)),
        out_shape=jax.ShapeDtypeStruct(recv.shape, BF16),
        scratch_shapes=[], vmem_mib=16, args=(c, full, recv))


def _adamw_update(wv, gv, mv, vv):
    m2 = ADAM_B1 * mv + (1.0 - ADAM_B1) * gv
    v2 = ADAM_B2 * vv + (1.0 - ADAM_B2) * (gv * gv)
    m_hat = m2 / (1.0 - ADAM_B1 ** ADAM_STEP)
    v_hat = v2 / (1.0 - ADAM_B2 ** ADAM_STEP)
    return -ADAM_LR * (m_hat / (jnp.sqrt(v_hat) + ADAM_EPS) + ADAM_WD * wv), m2, v2


def _finish(parts, k, w, m, v, transpose, name):
    _, n_parts, rows, D = parts.shape

    def body(p_ref, w_ref, m_ref, v_ref, g_ref, d_ref, mo_ref, vo_ref):
        acc = p_ref[0].astype(F32)
        for q in range(1, n_parts):
            acc = acc + p_ref[q].astype(F32)
        gv = acc.T if transpose else acc
        g_ref[...] = gv
        d_ref[...], mo_ref[...], vo_ref[...] = _adamw_update(w_ref[...], gv, m_ref[...], v_ref[...])

    whole = pl.BlockSpec(w.shape, lambda i: (0, 0))
    return _call(
        body, name=name, grid=(1,),
        in_specs=[pl.BlockSpec((None, n_parts, rows, D), lambda i: (k, 0, 0, 0)), whole, whole, whole],
        out_specs=[whole] * 4, out_shape=[pltpu.HBM(w.shape, F32)] * 4,
        scratch_shapes=[], vmem_mib=40, args=(parts, w, m, v))


def _adamw_each(ws, gs, ms, vs, name):
    n = len(ws)

    def body(*refs):
        w_refs, g_refs, m_refs, v_refs, outs = refs[:n], refs[n:2 * n], refs[2 * n:3 * n], refs[3 * n:4 * n], refs[4 * n:]
        for k in range(n):
            outs[k][...], outs[n + k][...], outs[2 * n + k][...] = _adamw_update(
                w_refs[k][...], g_refs[k][...], m_refs[k][...], v_refs[k][...])

    shapes = [jax.ShapeDtypeStruct(w.shape, F32) for w in ws]
    return pl.pallas_call(
        body, name=name,
        in_specs=[VMEM_SPEC] * (4 * n), out_specs=[VMEM_SPEC] * (3 * n), out_shape=shapes * 3,
        compiler_params=pltpu.CompilerParams(vmem_limit_bytes=32 * MIB),
    )(*ws, *gs, *ms, *vs)


def _block_diag(w):
    h, d, _ = w.shape
    onto = jnp.eye(h, dtype=w.dtype)
    return (w[:, :, None, :] * onto[:, None, :, None]).reshape(h * d, h * d)


def _diag_blocks(m, h):
    d = m.shape[0] // h
    onto = jnp.eye(h, dtype=m.dtype)
    return (m.reshape(h, d, h, d) * onto[:, None, :, None]).sum(axis=2)


def _reduce_level1(full, tag):
    got = _run_comm(_sibling_comm(full), "rs_sibling_" + tag)
    return [_pair_add(a, b, "rs_pair_add_%s%d" % (tag, n)) for n, (a, b) in enumerate(zip(full, got))]


def kernel(x, ffn1_norm, ffn1_w_gate, ffn1_w_up, ffn1_w_down, mix_norm, w_in, conv_dw, conv_dw_bias, conv_ln_g, conv_ln_b, lru_conv_w, lru_conv_b, lru_w_a, lru_b_a, lru_w_x, lru_b_x, lru_lambda, w_out, ffn2_norm, ffn2_w_gate, ffn2_w_up, ffn2_w_down, final_norm, loss_target, m_ffn1_norm, m_ffn1_w_gate, m_ffn1_w_up, m_ffn1_w_down, m_mix_norm, m_w_in, m_conv_dw, m_conv_dw_bias, m_conv_ln_g, m_conv_ln_b, m_lru_conv_w, m_lru_conv_b, m_lru_w_a, m_lru_b_a, m_lru_w_x, m_lru_b_x, m_lru_lambda, m_w_out, m_ffn2_norm, m_ffn2_w_gate, m_ffn2_w_up, m_ffn2_w_down, m_final_norm, v_ffn1_norm, v_ffn1_w_gate, v_ffn1_w_up, v_ffn1_w_down, v_mix_norm, v_w_in, v_conv_dw, v_conv_dw_bias, v_conv_ln_g, v_conv_ln_b, v_lru_conv_w, v_lru_conv_b, v_lru_w_a, v_lru_b_a, v_lru_w_x, v_lru_b_x, v_lru_lambda, v_w_out, v_ffn2_norm, v_ffn2_w_gate, v_ffn2_w_up, v_ffn2_w_down, v_final_norm):
    T, D = x.shape[1], x.shape[2]
    F = ffn1_w_down.shape[0] * N_DEV
    rf = ffn1_w_down.shape[0]
    ri = w_in.shape[1]
    ro = w_out.shape[0]
    W = conv_dw_bias.shape[0]
    wc = conv_dw.shape[1]
    H = lru_w_a.shape[0]
    xs = x.reshape(T, D)
    tgt = loss_target.reshape(T, D)
    tm_ffn = min(256, T)
    tm_fwd = min(512, T)
    cf = 256
    tm_w = min(1024, T)
    tm_w1 = min(2048, T)
    tm_mix = min(256, T)
    tf_w = F // 2
    row = lambda v: v.reshape(1, -1)
    by_owner = lambda a, rows: a.reshape(a.shape[0], N_CHIP, 2, rows, D)

    p3a, p3b, p_in, p_out = _prep_weights(
        (ffn1_w_gate.T, ffn1_w_up.T, ffn1_w_down), (ffn2_w_gate.T, ffn2_w_up.T, ffn2_w_down), w_in, w_out,
        "prep_weights")
    tile_rows = lambda a: jnp.pad(a, ((0, -a.shape[0] % SUBLANES), (0, 0)))
    p_cw = jnp.concatenate([tile_rows(conv_dw), tile_rows(lru_conv_w)], axis=0)
    lw_row = p_cw.shape[0] - SUBLANES
    stacked = lambda r, j: r.at[:, j]
    plain = lambda r, j: r.at[j]
    g3_shape = jax.ShapeDtypeStruct((3, N_DEV, rf, D), BF16)
    (g3a,) = _all_gather([p3a], [stacked], [g3_shape], "ag_ffn1")
    w3a = g3a.reshape(3, F, D)
    bda = _block_diag(lru_w_a).astype(BF16)
    bdx = _block_diag(lru_w_x).astype(BF16)
    vec = jnp.concatenate([tile_rows(v[None]) for v in
                           (conv_dw_bias, conv_ln_g, conv_ln_b, lru_conv_b, lru_b_a, lru_b_x, lru_lambda)], axis=0)

    gather_rest = _gather_comm(
        [p3b, p_in, p_out, p_cw], [stacked, plain, plain, plain],
        [g3_shape, jax.ShapeDtypeStruct((N_DEV, ri, D), BF16), jax.ShapeDtypeStruct((N_DEV, ro, D), BF16),
         jax.ShapeDtypeStruct((N_DEV,) + p_cw.shape, F32)],
        [(SIBLING,) + SAME_CORE, EVERYONE, EVERYONE, EVERYONE])
    hidden_last = lambda w3: jnp.swapaxes(w3[0:2], 1, 2)
    (x1, h1, dau1, dag1, act1), (g3b_half, g_in, g_out, g_cw) = _ffn_fwd(
        xs, row(ffn1_norm), w3a, hidden_last(w3a), tm_fwd, cf, "ffn1_fwd", rider=gather_rest)
    w_in_t = g_in.reshape(N_DEV * ri, D)
    w_out_f = g_out.reshape(N_DEV * ro, D)
    cw_all = jnp.transpose(g_cw, (1, 0, 2)).reshape(p_cw.shape[0], N_DEV * wc)
    cw = cw_all[0:CONV_K]
    lw = cw_all[lw_row:lw_row + LRU_K]
    (x2, z, mix, u1, xr, hst), (g3b,) = _mix_core_fwd(
        x1, row(mix_norm), w_in_t, w_out_f, bda, bdx, cw, lw, vec, tm_mix, "mix_core_fwd",
        rider=_forward_comm([g3b_half], [stacked]))
    w3b = g3b.reshape(3, F, D)
    dx3, dob2, d_final_norm, loss_part, h3, dau2, dag2, act2 = _ffn_fwd(
        x2, row(ffn2_norm), w3b, hidden_last(w3b), tm_fwd, cf, "ffn2_fwd_loss", head=(row(final_norm), tgt))

    dx2, dgate2, dup2, d_ffn2_norm = _ffn_dgrad(dx3, x2, row(ffn2_norm), dau2, dag2, w3b, tm_ffn, cf, "ffn2_dgrad")
    (dw_gu2,) = _wgrad([dgate2, dup2], h3, tm_w, tf_w, "ffn2_wgrad_gu")
    (dw_d2,) = _wgrad([act2], dob2, tm_w1, tf_w, "ffn2_wgrad_d")
    by_device = lambda a: a.reshape(a.shape[0], N_DEV, rf, D)
    (dx1, sg, dbda, dbdx, dob1, d_mix_norm, dw_in_t, dw_out), parts_f2 = _mix_bwd(
        dx2, z, u1, xr, hst, x1, mix, row(mix_norm), w_in_t, w_out_f, bda, bdx, cw, lw, vec, tm_mix, "mix_bwd",
        rider=_chips_comm([by_device(dw_gu2), by_device(dw_d2)], every_device=True))
    io = [dw_in_t.reshape(1, N_DEV, ri, D), dw_out.reshape(1, N_DEV, ro, D)]
    (dw_d1,), parts_io = _wgrad(
        [act1], dob1, tm_w1, tf_w, "ffn1_wgrad_d", rider=_chips_comm(io, every_device=True))
    sums_d1 = _reduce_level1([by_owner(dw_d1, rf)], "d1")
    dx0, dgate1, dup1, d_ffn1_norm = _ffn_dgrad(dx1, xs, row(ffn1_norm), dau1, dag1, w3a, tm_ffn, cf, "ffn1_dgrad")
    (dw_g1,), parts_d1 = _wgrad([dgate1], h1, tm_w1, tf_w, "ffn1_wgrad_g", rider=_chips_comm(sums_d1))
    sums_g1 = _reduce_level1([by_owner(dw_g1, rf)], "g1")
    small = [d_ffn1_norm, d_mix_norm, d_ffn2_norm, d_final_norm, sg, _diag_blocks(dbda, H).reshape(-1, D),
             _diag_blocks(dbdx, H).reshape(-1, D), loss_part]
    (dw_u1,), summed_and_parts = _wgrad(
        [dup1], h1, tm_w1, tf_w, "ffn1_wgrad_u", rider=_both(_small_sum_comm(small), _chips_comm(sums_g1)))
    summed, parts_g1 = summed_and_parts[:len(small)], summed_and_parts[len(small):]
    sums_u1 = _reduce_level1([by_owner(dw_u1, rf)], "u1")
    parts_u1 = _run_comm(_chips_comm(sums_u1), "rs_chips_u1")

    g_norm1, g_norm_mix, g_norm2, g_norm_final, g_sg, g_w_a, g_w_x, g_loss = summed
    loss = g_loss[0, 0]
    me = 4 * lax.axis_index("x") + 2 * lax.axis_index("y") + lax.axis_index("c")
    chan = lambda full_g: lax.dynamic_slice_in_dim(full_g, me * wc, wc, axis=1)
    grads = {
        "ffn1_norm": g_norm1.reshape(D), "mix_norm": g_norm_mix.reshape(D), "ffn2_norm": g_norm2.reshape(D),
        "final_norm": g_norm_final.reshape(D),
        "conv_dw_bias": g_sg[G_CB], "conv_ln_g": g_sg[G_LNG], "conv_ln_b": g_sg[G_LNB],
        "lru_conv_b": g_sg[G_LCB], "lru_b_a": g_sg[G_BA], "lru_b_x": g_sg[G_BX], "lru_lambda": g_sg[G_LAM],
        "lru_w_a": g_w_a.reshape(lru_w_a.shape), "lru_w_x": g_w_x.reshape(lru_w_x.shape),
        "conv_dw": chan(g_sg[G_CW:G_CW + CONV_K]), "lru_conv_w": chan(g_sg[G_LW:G_LW + LRU_K]),
    }

    weights = dict(ffn1_norm=ffn1_norm, ffn1_w_gate=ffn1_w_gate, ffn1_w_up=ffn1_w_up, ffn1_w_down=ffn1_w_down, mix_norm=mix_norm, w_in=w_in, conv_dw=conv_dw, conv_dw_bias=conv_dw_bias, conv_ln_g=conv_ln_g, conv_ln_b=conv_ln_b, lru_conv_w=lru_conv_w, lru_conv_b=lru_conv_b, lru_w_a=lru_w_a, lru_b_a=lru_b_a, lru_w_x=lru_w_x, lru_b_x=lru_b_x, lru_lambda=lru_lambda, w_out=w_out, ffn2_norm=ffn2_norm, ffn2_w_gate=ffn2_w_gate, ffn2_w_up=ffn2_w_up, ffn2_w_down=ffn2_w_down, final_norm=final_norm)
    moment1 = dict(ffn1_norm=m_ffn1_norm, ffn1_w_gate=m_ffn1_w_gate, ffn1_w_up=m_ffn1_w_up, ffn1_w_down=m_ffn1_w_down, mix_norm=m_mix_norm, w_in=m_w_in, conv_dw=m_conv_dw, conv_dw_bias=m_conv_dw_bias, conv_ln_g=m_conv_ln_g, conv_ln_b=m_conv_ln_b, lru_conv_w=m_lru_conv_w, lru_conv_b=m_lru_conv_b, lru_w_a=m_lru_w_a, lru_b_a=m_lru_b_a, lru_w_x=m_lru_w_x, lru_b_x=m_lru_b_x, lru_lambda=m_lru_lambda, w_out=m_w_out, ffn2_norm=m_ffn2_norm, ffn2_w_gate=m_ffn2_w_gate, ffn2_w_up=m_ffn2_w_up, ffn2_w_down=m_ffn2_w_down, final_norm=m_final_norm)
    moment2 = dict(ffn1_norm=v_ffn1_norm, ffn1_w_gate=v_ffn1_w_gate, ffn1_w_up=v_ffn1_w_up, ffn1_w_down=v_ffn1_w_down, mix_norm=v_mix_norm, w_in=v_w_in, conv_dw=v_conv_dw, conv_dw_bias=v_conv_dw_bias, conv_ln_g=v_conv_ln_g, conv_ln_b=v_conv_ln_b, lru_conv_w=v_lru_conv_w, lru_conv_b=v_lru_conv_b, lru_w_a=v_lru_w_a, lru_b_a=v_lru_b_a, lru_w_x=v_lru_w_x, lru_b_x=v_lru_b_x, lru_lambda=v_lru_lambda, w_out=v_w_out, ffn2_norm=v_ffn2_norm, ffn2_w_gate=v_ffn2_w_gate, ffn2_w_up=v_ffn2_w_up, ffn2_w_down=v_ffn2_w_down, final_norm=v_final_norm)
    order = list(weights)
    big = {"ffn1_w_gate": (parts_g1[0], 0, True), "ffn1_w_up": (parts_u1[0], 0, True),
           "ffn1_w_down": (parts_d1[0], 0, False), "w_in": (parts_io[0], 0, True), "w_out": (parts_io[1], 0, False),
           "ffn2_w_gate": (parts_f2[0], 0, True), "ffn2_w_up": (parts_f2[0], 1, True),
           "ffn2_w_down": (parts_f2[1], 0, False)}
    delta, new_m, new_v = {}, {}, {}
    for n, (parts, k, d_major) in big.items():
        operands = weights[n], moment1[n], moment2[n]
        if d_major and n != "w_in":
            results = _finish(parts, k, *[a.T for a in operands], False, "finish_" + n)
            grads[n], delta[n], new_m[n], new_v[n] = [r.T for r in results]
        else:
            grads[n], delta[n], new_m[n], new_v[n] = _finish(parts, k, *operands, d_major, "finish_" + n)
    rest = [n for n in order if n not in big]
    updates = _adamw_each([weights[n] for n in rest], [grads[n] for n in rest], [moment1[n] for n in rest],
                          [moment2[n] for n in rest], "adamw_small")
    for k, n in enumerate(rest):
        delta[n], new_m[n], new_v[n] = updates[k], updates[len(rest) + k], updates[2 * len(rest) + k]

    return (loss, dx0.reshape(x.shape), *[grads[n] for n in order], *[delta[n] for n in order],
            *[new_m[n] for n in order], *[new_v[n] for n in order])
```

```python
import functools
import math

import jax
import jax.numpy as jnp
from jax import lax
from jax.experimental import pallas as pl
from jax.experimental.pallas import tpu as pltpu

F32 = jnp.float32
BF16 = jnp.bfloat16
MESH = pl.DeviceIdType.MESH

N_DEV = 8
N_CHIP = 4
SUBLANES = 8
RMS_EPS = 1e-6
LN_EPS = 1e-5
LRU_C = 8.0
CONV_K = 31
LRU_K = 4
CONV_HALO = 32
LRU_HALO = 8
FFN_RES = 0.5
ADAM_LR, ADAM_B1, ADAM_B2, ADAM_EPS, ADAM_WD, ADAM_STEP = 0.001, 0.9, 0.999, 1e-08, 0.01, 10
GELU_K = math.sqrt(2.0 / math.pi)
GELU_C = 0.044715

MIB = 1024 * 1024
NT = (((1,), (1,)), ((), ()))
NN = (((1,), (0,)), ((), ()))
TN = (((0,), (0,)), ((), ()))

V_CB, V_LNG, V_LNB, V_LCB, V_BA, V_BX, V_LAM = range(0, 7 * SUBLANES, SUBLANES)
G_CW = 0
G_CB, G_LNG, G_LNB = 31, 32, 33
G_LW = 34
G_LCB, G_BA, G_BX, G_LAM = 38, 39, 40, 41
G_ROWS = 48

HBM_SPEC = pl.BlockSpec(memory_space=pltpu.HBM)
VMEM_SPEC = pl.BlockSpec(memory_space=pltpu.VMEM)


def _dot(a, b, dims):
    return lax.dot_general(a, b, dims, preferred_element_type=F32)


def _sigmoid(x):
    return 1.0 / (1.0 + jnp.exp(-x))


def _gelu_parts(x):
    x2 = x * x
    th = jnp.tanh(GELU_K * x * (1.0 + GELU_C * x2))
    gl = 0.5 * x * (1.0 + th)
    dgl = 0.5 * (1.0 + th) + 0.5 * x * (1.0 - th * th) * GELU_K * (1.0 + 3.0 * GELU_C * x2)
    return gl, dgl


def _neg_expm1(y):
    series = -y * (1.0 + y * (1.0 / 2) * (1.0 + y * (1.0 / 3) * (1.0 + y * (1.0 / 4) * (1.0 + y * (1.0 / 5) * (1.0 + y * (1.0 / 6))))))
    return jnp.where(y > -0.25, series, 1.0 - jnp.exp(y))


def _softplus_neg(lam):
    t = -lam
    e = jnp.exp(-jnp.abs(t))
    s = 1.0 + e
    log1p_e = jnp.log(s) - ((s - 1.0) - e) / s
    return jnp.maximum(t, 0.0) + log1p_e


def _rms_stats(xv):
    rstd = lax.rsqrt(jnp.mean(xv * xv, axis=-1, keepdims=True) + RMS_EPS)
    return xv * rstd, rstd


def _rms_bwd(xhat, rstd, g, dh):
    dxhat = dh * g
    dx = rstd * (dxhat - xhat * jnp.mean(dxhat * xhat, axis=-1, keepdims=True))
    return dx, jnp.sum(dh * xhat, axis=0, keepdims=True)


def _row_windows(buf_ref, n_rows, offsets):
    total = buf_ref.shape[0]
    full = buf_ref[...]
    for b in range(SUBLANES):
        offs = [o for o in offsets if o % SUBLANES == b]
        if not offs:
            continue
        assert max(offs) + n_rows <= total
        moved = full if b == 0 else pltpu.roll(full, total - b, 0)
        for o in offs:
            yield o, moved[o - b:o - b + n_rows, :]


def _scan_rows(av, bv, edge, out_ref, reverse=False):
    tm, W = av.shape
    sub = lax.broadcasted_iota(jnp.int32, (tm, W), 0) % SUBLANES
    s = 1
    while s < SUBLANES:
        keep = (sub < SUBLANES - s) if reverse else (sub >= s)
        shift = tm - s if reverse else s
        bv = jnp.where(keep, av * pltpu.roll(bv, shift, 0) + bv, bv)
        av = jnp.where(keep, av * pltpu.roll(av, shift, 0), av)
        s *= 2
    starts = range(0, tm, SUBLANES)
    for r0 in (reversed(starts) if reverse else starts):
        group = av[r0:r0 + SUBLANES, :] * edge + bv[r0:r0 + SUBLANES, :]
        out_ref[r0:r0 + SUBLANES, :] = group
        edge = group[0:1, :] if reverse else group[SUBLANES - 1:SUBLANES, :]
    return edge


class _Comm:
    def __init__(self, arrays, in_specs, out_shapes, out_specs, scratch, start, wait, aliases=None):
        self.arrays, self.in_specs = list(arrays), list(in_specs)
        self.out_shapes, self.out_specs = list(out_shapes), list(out_specs)
        self.scratch, self.start, self.wait = list(scratch), start, wait
        self.aliases = dict(aliases or {})


def _in_hbm(a):
    return pltpu.with_memory_space_constraint(a, pltpu.HBM)


def _operands(comm):
    return [a if spec is VMEM_SPEC else _in_hbm(a) for a, spec in zip(comm.arrays, comm.in_specs)]


def _call(body, *, name, grid, in_specs, out_specs, out_shape, scratch_shapes, vmem_mib, args, rider=None,
          num_scalar_prefetch=0):
    params = pltpu.CompilerParams(dimension_semantics=("arbitrary",) * len(grid), vmem_limit_bytes=vmem_mib * MIB)
    args = [a if k < num_scalar_prefetch else _in_hbm(a) for k, a in enumerate(args)]
    if rider is None:
        return pl.pallas_call(
            body, name=name,
            grid_spec=pltpu.PrefetchScalarGridSpec(
                num_scalar_prefetch=num_scalar_prefetch, grid=grid, in_specs=in_specs, out_specs=out_specs,
                scratch_shapes=scratch_shapes),
            out_shape=out_shape, compiler_params=params)(*args)
    assert num_scalar_prefetch == 0
    n_in, n_out, n_scr = len(in_specs), len(out_specs), len(scratch_shapes)
    r_in, r_out = len(rider.arrays), len(rider.out_shapes)
    n_axes = len(grid)

    def carried(*refs):
        pos = [0]

        def take(n):
            pos[0] += n
            return refs[pos[0] - n:pos[0]]

        ins, r_ins, outs, r_outs, scr, r_scr = take(n_in), take(r_in), take(n_out), take(r_out), take(n_scr), take(len(rider.scratch))
        first = pl.program_id(0) == 0
        last = pl.program_id(0) == grid[0] - 1
        for ax in range(1, n_axes):
            first = first & (pl.program_id(ax) == 0)
            last = last & (pl.program_id(ax) == grid[ax] - 1)

        @pl.when(first)
        def _():
            rider.start(r_ins, r_outs, r_scr)

        body(*ins, *outs, *scr)

        @pl.when(last)
        def _():
            rider.wait(r_ins, r_outs, r_scr)

    res = pl.pallas_call(
        carried, name=name,
        grid=grid,
        in_specs=list(in_specs) + rider.in_specs,
        out_specs=list(out_specs) + rider.out_specs,
        out_shape=list(out_shape) + rider.out_shapes,
        scratch_shapes=list(scratch_shapes) + rider.scratch,
        input_output_aliases={n_in + i: n_out + o for i, o in rider.aliases.items()},
        compiler_params=params)(*args, *_operands(rider))
    return res[:n_out], res[n_out:]


def _run_comm(comm, name):
    n_in, n_out = len(comm.arrays), len(comm.out_shapes)

    def body(*refs):
        ins, outs, scr = refs[:n_in], refs[n_in:n_in + n_out], refs[n_in + n_out:]
        comm.start(ins, outs, scr)
        comm.wait(ins, outs, scr)

    return pl.pallas_call(
        body, name=name,
        in_specs=comm.in_specs, out_specs=comm.out_specs, out_shape=comm.out_shapes,
        scratch_shapes=comm.scratch, input_output_aliases=comm.aliases,
        compiler_params=pltpu.CompilerParams(vmem_limit_bytes=24 * MIB))(*_operands(comm))


def _both(a, b):
    ni, no, ns = len(a.arrays), len(a.out_shapes), len(a.scratch)

    def start(ins, outs, scr):
        a.start(ins[:ni], outs[:no], scr[:ns])
        b.start(ins[ni:], outs[no:], scr[ns:])

    def wait(ins, outs, scr):
        a.wait(ins[:ni], outs[:no], scr[:ns])
        b.wait(ins[ni:], outs[no:], scr[ns:])

    aliases = dict(a.aliases)
    aliases.update({ni + i: no + o for i, o in b.aliases.items()})
    return _Comm(a.arrays + b.arrays, a.in_specs + b.in_specs, a.out_shapes + b.out_shapes,
                 a.out_specs + b.out_specs, a.scratch + b.scratch, start, wait, aliases)


def _place():
    return lax.axis_index("x"), lax.axis_index("y"), lax.axis_index("c")


def _peer(k):
    x, y, c = _place()
    px, py, pc = x ^ ((k >> 2) & 1), y ^ ((k >> 1) & 1), c ^ (k & 1)
    return (px, py, pc), 4 * px + 2 * py + pc


SIBLING = 1
SAME_CORE = (2, 4, 6)
EVERYONE = tuple(range(1, N_DEV))


def _gather_comm(shards, views, out_shapes, relations):
    na = len(shards)

    def copies(ins, outs, scr):
        send_sems, recv_sems, _ = scr
        _, me = _peer(0)
        out = []
        for a in range(na):
            for k in relations[a]:
                peer, theirs = _peer(k)
                send = functools.partial(
                    pltpu.make_async_remote_copy,
                    src_ref=ins[a], dst_ref=views[a](outs[a], me),
                    send_sem=send_sems.at[7 * a + k - 1], recv_sem=recv_sems.at[7 * a + k - 1],
                    device_id=peer, device_id_type=MESH)
                recv = functools.partial(
                    pltpu.make_async_remote_copy,
                    src_ref=ins[a], dst_ref=views[a](outs[a], theirs),
                    send_sem=send_sems.at[7 * a + k - 1], recv_sem=recv_sems.at[7 * a + k - 1],
                    device_id=peer, device_id_type=MESH)
                out.append((send, recv))
        return out

    def local(ins, outs, scr):
        _, me = _peer(0)
        return [pltpu.make_async_copy(ins[a], views[a](outs[a], me), scr[2].at[a]) for a in range(na)]

    def start(ins, outs, scr):
        for cp in local(ins, outs, scr):
            cp.start()
        for send, _ in copies(ins, outs, scr):
            send().start()

    def wait(ins, outs, scr):
        for _, recv in copies(ins, outs, scr):
            recv().wait_recv()
        for send, _ in copies(ins, outs, scr):
            send().wait_send()
        for cp in local(ins, outs, scr):
            cp.wait()

    return _Comm(shards, [HBM_SPEC] * na, out_shapes, [HBM_SPEC] * na,
                 [pltpu.SemaphoreType.DMA((7 * na,)), pltpu.SemaphoreType.DMA((7 * na,)),
                  pltpu.SemaphoreType.DMA((na,))], start, wait)


def _forward_comm(gathered, views):
    na = len(gathered)
    shapes = [jax.ShapeDtypeStruct(g.shape, g.dtype) for g in gathered]

    def copies(outs, scr):
        send_sems, recv_sems = scr
        sibling, _ = _peer(SIBLING)
        out = []
        for a in range(na):
            for n, k in enumerate(SAME_CORE):
                _, mine = _peer(k)
                _, theirs = _peer(k ^ SIBLING)
                send = functools.partial(
                    pltpu.make_async_remote_copy,
                    src_ref=views[a](outs[a], mine), dst_ref=views[a](outs[a], mine),
                    send_sem=send_sems.at[3 * a + n], recv_sem=recv_sems.at[3 * a + n],
                    device_id=sibling, device_id_type=MESH)
                recv = functools.partial(
                    pltpu.make_async_remote_copy,
                    src_ref=views[a](outs[a], mine), dst_ref=views[a](outs[a], theirs),
                    send_sem=send_sems.at[3 * a + n], recv_sem=recv_sems.at[3 * a + n],
                    device_id=sibling, device_id_type=MESH)
                out.append((send, recv))
        return out

    def start(ins, outs, scr):
        for send, _ in copies(outs, scr):
            send().start()

    def wait(ins, outs, scr):
        for _, recv in copies(outs, scr):
            recv().wait_recv()
        for send, _ in copies(outs, scr):
            send().wait_send()

    return _Comm(gathered, [HBM_SPEC] * na, shapes, [HBM_SPEC] * na,
                 [pltpu.SemaphoreType.DMA((3 * na,)), pltpu.SemaphoreType.DMA((3 * na,))], start, wait,
                 aliases={a: a for a in range(na)})


Y_NEIGHBOUR, X_NEIGHBOUR, DIAGONAL = SAME_CORE


def _all_gather(shards, views, out_shapes, name):
    na = len(shards)
    near = (SIBLING, Y_NEIGHBOUR, X_NEIGHBOUR)
    level1 = _gather_comm(shards, views, out_shapes, [near] * na)

    def body(*refs):
        ins, outs = refs[:na], refs[na:2 * na]
        send_sems, recv_sems, local_sems, fwd_send, fwd_recv, relay_send, relay_recv = refs[2 * na:]
        sibling, _ = _peer(SIBLING)
        c = lax.axis_index("c")
        level1.start(ins, outs, (send_sems, recv_sems, local_sems))

        def block_copy(a, block, to, send_sem, recv_sem):
            return pltpu.make_async_remote_copy(
                src_ref=views[a](outs[a], block), dst_ref=views[a](outs[a], block),
                send_sem=send_sem, recv_sem=recv_sem, device_id=to, device_id_type=MESH)

        def to_sibling(a, n, k):
            _, mine = _peer(k)
            _, theirs = _peer(k ^ SIBLING)
            fwd = block_copy(a, mine, sibling, fwd_send.at[3 * a + n], fwd_recv.at[3 * a + n])
            fwd.start()
            return fwd, block_copy(a, theirs, sibling, fwd_send.at[3 * a + n], fwd_recv.at[3 * a + n])

        passed, landing = [], []
        for a in range(na):
            for n, k in enumerate((Y_NEIGHBOUR, X_NEIGHBOUR)):
                peer, origin = _peer(k)
                pltpu.make_async_remote_copy(
                    src_ref=ins[a], dst_ref=views[a](outs[a], origin),
                    send_sem=send_sems.at[7 * a + k - 1], recv_sem=recv_sems.at[7 * a + k - 1],
                    device_id=peer, device_id_type=MESH).wait_recv()

                @pl.when(c == (0 if k == X_NEIGHBOUR else 1))
                def _():
                    other, _ = _peer(DIAGONAL ^ k)
                    block_copy(a, origin, other, relay_send.at[a], relay_recv.at[a]).start()

                fwd, lands = to_sibling(a, n, k)
                passed.append(fwd)
                landing.append(lands)
            _, far = _peer(DIAGONAL)
            block_copy(a, far, sibling, relay_send.at[a], relay_recv.at[a]).wait_recv()
            fwd, lands = to_sibling(a, 2, DIAGONAL)
            passed.append(fwd)
            landing.append(lands)
        for a in range(na):
            _, theirs = _peer(SIBLING)
            pltpu.make_async_remote_copy(
                src_ref=ins[a], dst_ref=views[a](outs[a], theirs),
                send_sem=send_sems.at[7 * a + SIBLING - 1], recv_sem=recv_sems.at[7 * a + SIBLING - 1],
                device_id=sibling, device_id_type=MESH).wait_recv()
        for cp in landing:
            cp.wait_recv()
        for cp in passed:
            cp.wait_send()
        _, me = _peer(0)
        for a in range(na):
            block_copy(a, me, sibling, relay_send.at[a], relay_recv.at[a]).wait_send()
            for k in near:
                peer, _ = _peer(k)
                pltpu.make_async_remote_copy(
                    src_ref=ins[a], dst_ref=views[a](outs[a], me),
                    send_sem=send_sems.at[7 * a + k - 1], recv_sem=recv_sems.at[7 * a + k - 1],
                    device_id=peer, device_id_type=MESH).wait_send()
            pltpu.make_async_copy(ins[a], views[a](outs[a], me), local_sems.at[a]).wait()

    return pl.pallas_call(
        body, name=name,
        in_specs=[HBM_SPEC] * na, out_specs=[HBM_SPEC] * na, out_shape=out_shapes,
        scratch_shapes=level1.scratch + [pltpu.SemaphoreType.DMA((3 * na,)), pltpu.SemaphoreType.DMA((3 * na,)),
                                         pltpu.SemaphoreType.DMA((na,)), pltpu.SemaphoreType.DMA((na,))],
    )(*[_in_hbm(s) for s in shards])


def _sibling_comm(grads):
    na = len(grads)
    shapes = [jax.ShapeDtypeStruct(g.shape[:2] + g.shape[3:], g.dtype) for g in grads]

    def copies(ins, outs, scr):
        x, y, c = _place()
        return [pltpu.make_async_remote_copy(
            src_ref=ins[a].at[:, :, 1 - c], dst_ref=outs[a],
            send_sem=scr[0].at[a], recv_sem=scr[1].at[a],
            device_id=(x, y, 1 - c), device_id_type=MESH) for a in range(na)]

    def start(ins, outs, scr):
        for cp in copies(ins, outs, scr):
            cp.start()

    def wait(ins, outs, scr):
        for cp in copies(ins, outs, scr):
            cp.wait()

    return _Comm(grads, [HBM_SPEC] * na, shapes, [HBM_SPEC] * na,
                 [pltpu.SemaphoreType.DMA((na,)), pltpu.SemaphoreType.DMA((na,))], start, wait)


def _chips_comm(sums, every_device=False):
    na = len(sums)
    shapes = [jax.ShapeDtypeStruct(s.shape, s.dtype) for s in sums]
    relations = EVERYONE if every_device else SAME_CORE
    nr = len(relations)

    def block(px, py, pc):
        return 4 * px + 2 * py + pc if every_device else 2 * px + py

    def copies(ins, outs, scr):
        mine = block(*_place())
        out = []
        for a in range(na):
            for n, k in enumerate(relations):
                peer, _ = _peer(k)
                theirs = block(*peer)
                send = functools.partial(
                    pltpu.make_async_remote_copy,
                    src_ref=ins[a].at[:, theirs], dst_ref=outs[a].at[:, mine],
                    send_sem=scr[0].at[nr * a + n], recv_sem=scr[1].at[nr * a + n],
                    device_id=peer, device_id_type=MESH)
                recv = functools.partial(
                    pltpu.make_async_remote_copy,
                    src_ref=ins[a].at[:, mine], dst_ref=outs[a].at[:, theirs],
                    send_sem=scr[0].at[nr * a + n], recv_sem=scr[1].at[nr * a + n],
                    device_id=peer, device_id_type=MESH)
                out.append((send, recv))
        return out

    def local(ins, outs, scr):
        mine = block(*_place())
        return [pltpu.make_async_copy(ins[a].at[:, mine], outs[a].at[:, mine], scr[2].at[a]) for a in range(na)]

    def start(ins, outs, scr):
        for cp in local(ins, outs, scr):
            cp.start()
        for send, _ in copies(ins, outs, scr):
            send().start()

    def wait(ins, outs, scr):
        for _, recv in copies(ins, outs, scr):
            recv().wait_recv()
        for send, _ in copies(ins, outs, scr):
            send().wait_send()
        for cp in local(ins, outs, scr):
            cp.wait()

    return _Comm(sums, [HBM_SPEC] * na, shapes, [HBM_SPEC] * na,
                 [pltpu.SemaphoreType.DMA((nr * na,)), pltpu.SemaphoreType.DMA((nr * na,)),
                  pltpu.SemaphoreType.DMA((na,))], start, wait)


def _small_sum_comm(arrays):
    na = len(arrays)

    def copies(ins, scr):
        bufs, send_sems, recv_sems = scr[:na], scr[na], scr[na + 1]
        _, me = _peer(0)
        out = []
        for a in range(na):
            for k in EVERYONE:
                peer, theirs = _peer(k)
                sems = dict(send_sem=send_sems.at[7 * a + k - 1], recv_sem=recv_sems.at[7 * a + k - 1])
                send = functools.partial(
                    pltpu.make_async_remote_copy,
                    src_ref=ins[a], dst_ref=bufs[a].at[me], device_id=peer, device_id_type=MESH, **sems)
                recv = functools.partial(
                    pltpu.make_async_remote_copy,
                    src_ref=ins[a], dst_ref=bufs[a].at[theirs], device_id=peer, device_id_type=MESH, **sems)
                out.append((send, recv))
        return out

    def start(ins, outs, scr):
        _, me = _peer(0)
        for a in range(na):
            scr[a][me] = ins[a][...]
        for send, _ in copies(ins, scr):
            send().start()

    def wait(ins, outs, scr):
        for _, recv in copies(ins, scr):
            recv().wait_recv()
        for send, _ in copies(ins, scr):
            send().wait_send()
        for a in range(na):
            acc = scr[a][0]
            for j in range(1, N_DEV):
                acc = acc + scr[a][j]
            outs[a][...] = acc

    return _Comm(arrays, [VMEM_SPEC] * na, [jax.ShapeDtypeStruct(s.shape, F32) for s in arrays], [VMEM_SPEC] * na,
                 [pltpu.VMEM((N_DEV,) + s.shape, F32) for s in arrays]
                 + [pltpu.SemaphoreType.DMA((7 * na,)), pltpu.SemaphoreType.DMA((7 * na,))], start, wait)


def _prep_weights(ffn1, ffn2, w_in, w_out, name):
    rf, D = ffn1[2].shape
    ri, ro = w_in.shape[1], w_out.shape[0]

    def body(g1, u1, d1, g2, u2, d2, wi, wo, p1_ref, p2_ref, pi_ref, po_ref):
        for p_ref, shards in ((p1_ref, (g1, u1, d1)), (p2_ref, (g2, u2, d2))):
            for k, shard in enumerate(shards):
                p_ref[k] = shard[...].astype(BF16)
        pi_ref[...] = wi[...].T.astype(BF16)
        po_ref[...] = wo[...].astype(BF16)

    args = (*ffn1, *ffn2, w_in, w_out)
    whole = lambda shape: pl.BlockSpec(shape, lambda i: (0,) * len(shape))
    out_shapes = [(3, rf, D), (3, rf, D), (ri, D), (ro, D)]
    return _call(
        body, name=name, grid=(1,),
        in_specs=[whole(a.shape) for a in args], out_specs=[whole(s) for s in out_shapes],
        out_shape=[jax.ShapeDtypeStruct(s, BF16) for s in out_shapes],
        scratch_shapes=[], vmem_mib=48, args=args)


def _load_weights(w_hbm, w_vmem, sem):
    @pl.when(pl.program_id(0) == 0)
    def _():
        copies = [pltpu.make_async_copy(w_hbm.at[k], w_vmem.at[k], sem.at[k]) for k in range(3)]
        for cp in copies:
            cp.start()
        for cp in copies:
            cp.wait()


def _ffn_fwd(x, g, w3, tm, cf, name, rider=None, head=None):
    T, D = x.shape
    F = w3.shape[1]
    n_head = 0 if head is None else 2

    def body(x_ref, g_ref, w_hbm, *refs):
        head_refs, refs = refs[:n_head], refs[n_head:]
        if head is None:
            (xo_ref, h_ref, dau_ref, dag_ref, act_ref, wv, sem) = refs
        else:
            (dx_ref, dob_ref, dgf_ref, loss_ref, h_ref, dau_ref, dag_ref, act_ref, wv, sem) = refs
        _load_weights(w_hbm, wv, sem)
        xhat, _ = _rms_stats(x_ref[...])
        hb = (xhat * g_ref[...]).astype(BF16)
        h_ref[...] = hb
        for lo in range(0, F, cf):
            gate = _dot(hb, wv[0, lo:lo + cf, :], NT)
            up = _dot(hb, wv[1, lo:lo + cf, :], NT)
            sig = _sigmoid(gate)
            silu = gate * sig
            dau_ref[:, lo:lo + cf] = silu.astype(BF16)
            dag_ref[:, lo:lo + cf] = (up * (sig * (1.0 + gate * (1.0 - sig)))).astype(BF16)
            act_ref[:, lo:lo + cf] = (silu * up).astype(BF16)
        x_out = x_ref[...] + FFN_RES * _dot(act_ref[...], wv[2], NN)
        if head is None:
            xo_ref[...] = x_out
            return

        @pl.when(pl.program_id(0) == 0)
        def _():
            dgf_ref[...] = jnp.zeros_like(dgf_ref)
            loss_ref[...] = jnp.zeros_like(loss_ref)

        gf_ref, tgt_ref = head_refs
        yhat, rstd = _rms_stats(x_out)
        gf = gf_ref[...]
        err = yhat * gf - tgt_ref[...]
        loss_ref[...] += (0.5 / D) * jnp.sum(err * err)
        dx, dgf = _rms_bwd(yhat, rstd, gf, err * (1.0 / D))
        dx_ref[...] = dx
        dob_ref[...] = (FFN_RES * dx).astype(BF16)
        dgf_ref[...] += dgf

    row = pl.BlockSpec((tm, D), lambda i: (i, 0))
    hid = pl.BlockSpec((tm, F), lambda i: (i, 0))
    vec = pl.BlockSpec((1, D), lambda i: (0, 0))
    row_f32, row_bf16 = jax.ShapeDtypeStruct((T, D), F32), jax.ShapeDtypeStruct((T, D), BF16)
    if head is None:
        first_specs, first_shapes = [row], [row_f32]
    else:
        first_specs = [row, row, vec, pl.BlockSpec((1, 128), lambda i: (0, 0))]
        first_shapes = [row_f32, row_bf16, jax.ShapeDtypeStruct((1, D), F32), jax.ShapeDtypeStruct((1, 128), F32)]
    return _call(
        body, name=name, grid=(T // tm,),
        in_specs=[row, vec, HBM_SPEC] + ([] if head is None else [vec, row]),
        out_specs=first_specs + [row, hid, hid, hid],
        out_shape=first_shapes + [row_bf16] + [jax.ShapeDtypeStruct((T, F), BF16)] * 3,
        scratch_shapes=[pltpu.VMEM((3, F, D), BF16), pltpu.SemaphoreType.DMA((3,))],
        vmem_mib=60, args=(x, g, w3) + (() if head is None else tuple(head)), rider=rider)


def _ffn_dgrad(dout, x, g, dau, dag, w3, tm, cf, name, rider=None):
    T, D = x.shape
    F = w3.shape[1]

    def body(do_ref, x_ref, g_ref, dau_ref, dag_ref, w_hbm, dx_ref, dgate_ref, dup_ref, dg_ref, wv, sem):
        _load_weights(w_hbm, wv, sem)

        @pl.when(pl.program_id(0) == 0)
        def _():
            dg_ref[...] = jnp.zeros_like(dg_ref)

        dob = (FFN_RES * do_ref[...]).astype(BF16)
        for lo in range(0, F, cf):
            dact = _dot(dob, wv[2, lo:lo + cf, :], NT)
            dup_ref[:, lo:lo + cf] = (dact * dau_ref[:, lo:lo + cf].astype(F32)).astype(BF16)
            dgate_ref[:, lo:lo + cf] = (dact * dag_ref[:, lo:lo + cf].astype(F32)).astype(BF16)
        dh = _dot(dgate_ref[...], wv[0], NN) + _dot(dup_ref[...], wv[1], NN)
        xhat, rstd = _rms_stats(x_ref[...])
        dx, dg = _rms_bwd(xhat, rstd, g_ref[...], dh)
        dx_ref[...] = do_ref[...] + dx
        dg_ref[...] += dg

    row = pl.BlockSpec((tm, D), lambda i: (i, 0))
    hid = pl.BlockSpec((tm, F), lambda i: (i, 0))
    vec = pl.BlockSpec((1, D), lambda i: (0, 0))
    return _call(
        body, name=name, grid=(T // tm,),
        in_specs=[row, row, vec, hid, hid, HBM_SPEC],
        out_specs=[row, hid, hid, vec],
        out_shape=[jax.ShapeDtypeStruct((T, D), F32), jax.ShapeDtypeStruct((T, F), BF16),
                   jax.ShapeDtypeStruct((T, F), BF16), jax.ShapeDtypeStruct((1, D), F32)],
        scratch_shapes=[pltpu.VMEM((3, F, D), BF16), pltpu.SemaphoreType.DMA((3,))],
        vmem_mib=52, args=(dout, x, g, dau, dag, w3), rider=rider)


def _wgrad(lhs, rhs, tm, tf, name, rider=None):
    T, F = lhs[0].shape
    D = rhs.shape[1]
    K = len(lhs)

    def body(*refs):
        lhs_refs, rhs_ref, dw_ref, accs = refs[:K], refs[K], refs[K + 1], refs[K + 2:]
        i = pl.program_id(1)

        @pl.when(i == 0)
        def _():
            for acc in accs:
                acc[...] = jnp.zeros_like(acc)

        rv = rhs_ref[...]
        for acc, lhs_ref in zip(accs, lhs_refs):
            acc[...] += _dot(lhs_ref[...], rv, TN)

        @pl.when(i == pl.num_programs(1) - 1)
        def _():
            for k, acc in enumerate(accs):
                dw_ref[k] = acc[...].astype(BF16)

    hid = pl.BlockSpec((tm, tf), lambda f, i: (i, f))
    return _call(
        body, name=name, grid=(F // tf, T // tm),
        in_specs=[hid] * K + [pl.BlockSpec((tm, D), lambda f, i: (i, 0))],
        out_specs=[pl.BlockSpec((K, tf, D), lambda f, i: (0, f, 0))],
        out_shape=[jax.ShapeDtypeStruct((K, F, D), BF16)],
        scratch_shapes=[pltpu.VMEM((tf, D), F32)] * K,
        vmem_mib=56, args=(*lhs, rhs), rider=rider)


def _lru_gates(xr, bda_ref, bdx_ref, vec_ref):
    xrb = xr.astype(BF16)
    r = _sigmoid(_dot(xrb, bda_ref[...], NN) + vec_ref[V_BA:V_BA + 1, :])
    ig = _sigmoid(_dot(xrb, bdx_ref[...], NN) + vec_ref[V_BX:V_BX + 1, :])
    sp = _softplus_neg(vec_ref[V_LAM:V_LAM + 1, :])
    log_a = (-LRU_C * sp) * r
    a = jnp.exp(log_a)
    mult = jnp.sqrt(_neg_expm1(2.0 * log_a))
    return xrb, r, ig, sp, a, mult


def _layernorm_stats(u1):
    xc = u1 - jnp.mean(u1, axis=-1, keepdims=True)
    rs = lax.rsqrt(jnp.mean(xc * xc, axis=-1, keepdims=True) + LN_EPS)
    return xc * rs, rs


def _mix_core_fwd(x1, g, w_in_t, w_out, bda, bdx, cw, lw, vec, tm, name, rider=None):
    T, D = x1.shape
    W = cw.shape[1]
    assert tm >= CONV_HALO and w_in_t.shape[0] == 4 * W

    def body(x1_ref, g_ref, wi_ref, wo_ref, bda_ref, bdx_ref, cw_ref, lw_ref, vec_ref,
             x2_ref, z_ref, mix_ref, u1_ref, xr_ref, hst_ref, ubuf, rbuf, hc):
        @pl.when(pl.program_id(0) == 0)
        def _():
            ubuf[0:CONV_HALO, :] = jnp.zeros((CONV_HALO, W), F32)
            rbuf[0:LRU_HALO, :] = jnp.zeros((LRU_HALO, W), F32)
            hc[...] = jnp.zeros_like(hc)

        xhat, _ = _rms_stats(x1_ref[...])
        z_ref[...] = _dot((xhat * g_ref[...]).astype(BF16), wi_ref[...], NT)

        ubuf[CONV_HALO:CONV_HALO + tm, :] = z_ref[:, 0:W] * _sigmoid(z_ref[:, W:2 * W])
        u1 = jnp.zeros((tm, W), F32) + vec_ref[V_CB:V_CB + 1, :]
        base = CONV_HALO - (CONV_K - 1)
        for off, win in _row_windows(ubuf, tm, range(base, base + CONV_K)):
            u1 = u1 + cw_ref[off - base:off - base + 1, :] * win
        ubuf[0:CONV_HALO, :] = ubuf[tm:tm + CONV_HALO, :]
        u1_ref[...] = u1
        xh, _ = _layernorm_stats(u1)
        u2 = xh * vec_ref[V_LNG:V_LNG + 1, :] + vec_ref[V_LNB:V_LNB + 1, :]
        ub = (u2 * _sigmoid(u2)).astype(BF16)
        mix_ref[:, 0:W] = ub

        rbuf[LRU_HALO:LRU_HALO + tm, :] = z_ref[:, 2 * W:3 * W]
        xr = jnp.zeros((tm, W), F32) + vec_ref[V_LCB:V_LCB + 1, :]
        for k in range(LRU_K):
            off = LRU_HALO - (LRU_K - 1) + k
            xr = xr + lw_ref[k:k + 1, :] * rbuf[off:off + tm, :]
        rbuf[0:LRU_HALO, :] = rbuf[tm:tm + LRU_HALO, :]
        xr_ref[...] = xr
        _, _, ig, _, a, mult = _lru_gates(xr, bda_ref, bdx_ref, vec_ref)
        hc[0:1, :] = _scan_rows(a, mult * (ig * xr), hc[0:1, :], hst_ref)
        gl, _ = _gelu_parts(z_ref[:, 3 * W:4 * W])
        yb = (hst_ref[...] * gl).astype(BF16)
        mix_ref[:, W:2 * W] = yb

        x2_ref[...] = x1_ref[...] + _dot(ub, wo_ref[0:W, :], NN) + _dot(yb, wo_ref[W:2 * W, :], NN)

    full = lambda a: pl.BlockSpec(a.shape, lambda i: (0,) * a.ndim)
    tile = lambda n: pl.BlockSpec((tm, n), lambda i: (i, 0))
    return _call(
        body, name=name, grid=(T // tm,),
        in_specs=[tile(D), full(g), full(w_in_t), full(w_out), full(bda), full(bdx), full(cw), full(lw), full(vec)],
        out_specs=[tile(D), tile(4 * W), tile(2 * W), tile(W), tile(W), tile(W)],
        out_shape=[jax.ShapeDtypeStruct((T, D), F32), jax.ShapeDtypeStruct((T, 4 * W), F32),
                   jax.ShapeDtypeStruct((T, 2 * W), BF16), jax.ShapeDtypeStruct((T, W), F32),
                   jax.ShapeDtypeStruct((T, W), F32), jax.ShapeDtypeStruct((T, W), F32)],
        scratch_shapes=[pltpu.VMEM((tm + CONV_HALO, W), F32), pltpu.VMEM((tm + LRU_HALO, W), F32),
                        pltpu.VMEM((8, W), F32)],
        vmem_mib=56, args=(x1, g, w_in_t, w_out, bda, bdx, cw, lw, vec), rider=rider)


def _mix_bwd(dx2, z, u1, xr, hst, x1, mix, g, w_in_t, w_out, bda, bdx, cw, lw, vec, tm, name, rider=None):
    T, D = dx2.shape
    W = cw.shape[1]
    nt = T // tm
    assert tm >= CONV_HALO and tm % CONV_HALO == 0

    def body(dx_ref, z_ref, zh_ref, u1_ref, xr_ref, h_ref, hh_ref, wo_ref, bda_ref, bdx_ref, cw_ref, lw_ref, vec_ref,
             x1_ref, mix_ref, g_ref, wi_ref,
             dx1_ref, sg_ref, dbda_ref, dbdx_ref, dob_ref, dg_ref, dwi_ref, dwo_ref,
             u0buf, du1buf, rxbuf, dxrbuf, gbuf, gc, spacc, dz_ref, ai_ref, ao_ref):
        i = pl.program_id(0)
        first = i == nt - 1
        row = lax.broadcasted_iota(jnp.int32, (tm, W), 0)

        @pl.when(i == 0)
        def _():
            sg_ref[...] = jnp.zeros_like(sg_ref)
            dbda_ref[...] = jnp.zeros_like(dbda_ref)
            dbdx_ref[...] = jnp.zeros_like(dbdx_ref)
            du1buf[tm:tm + CONV_HALO, :] = jnp.zeros((CONV_HALO, W), F32)
            dxrbuf[tm:tm + LRU_HALO, :] = jnp.zeros((LRU_HALO, W), F32)
            gc[...] = jnp.zeros_like(gc)
            spacc[...] = jnp.zeros_like(spacc)
            dg_ref[...] = jnp.zeros_like(dg_ref)
            ai_ref[...] = jnp.zeros_like(ai_ref)
            ao_ref[...] = jnp.zeros_like(ao_ref)

        def accum(r, val):
            sg_ref[r:r + 1, :] += jnp.sum(val, axis=0, keepdims=True)

        x1hat, x1rstd = _rms_stats(x1_ref[...])
        gain = g_ref[...]
        hb = (x1hat * gain).astype(BF16)

        def in_proj_bwd(lo, hi):
            dzb = dz_ref[:, lo:hi]
            ai_ref[lo:hi, :] += _dot(dzb, hb, TN)
            return _dot(dzb, wi_ref[lo:hi, :], NN)

        dxb = dx_ref[...].astype(BF16)
        ao_ref[...] += _dot(mix_ref[...], dxb, TN)
        dmix = _dot(dxb, wo_ref[...], NT)
        d_u = dmix[:, 0:W]
        d_yr = dmix[:, W:2 * W]

        xh, rs = _layernorm_stats(u1_ref[...])
        ln_g = vec_ref[V_LNG:V_LNG + 1, :]
        u2 = xh * ln_g + vec_ref[V_LNB:V_LNB + 1, :]
        s2 = _sigmoid(u2)
        d_u2 = d_u * (s2 * (1.0 + u2 * (1.0 - s2)))
        accum(G_LNG, d_u2 * xh)
        accum(G_LNB, d_u2)
        d_xh = d_u2 * ln_g
        d_u1 = rs * (d_xh - jnp.mean(d_xh, axis=-1, keepdims=True)
                     - xh * jnp.mean(d_xh * xh, axis=-1, keepdims=True))
        accum(G_CB, d_u1)
        halo_on = jnp.where(first, 0.0, 1.0)
        u0buf[0:CONV_HALO, :] = halo_on * (zh_ref[:, 0:W] * _sigmoid(zh_ref[:, W:2 * W]))
        cv = z_ref[:, 0:W]
        sgc = _sigmoid(z_ref[:, W:2 * W])
        u0buf[CONV_HALO:CONV_HALO + tm, :] = cv * sgc
        du1buf[0:tm, :] = d_u1
        base = CONV_HALO - (CONV_K - 1)
        for off, win in _row_windows(u0buf, tm, range(base, base + CONV_K)):
            accum(G_CW + off - base, d_u1 * win)
        d_u0 = jnp.zeros((tm, W), F32)
        for off, win in _row_windows(du1buf, tm, range(0, CONV_K)):
            d_u0 = d_u0 + cw_ref[CONV_K - 1 - off:CONV_K - off, :] * win
        du1buf[tm:tm + CONV_HALO, :] = du1buf[0:CONV_HALO, :]
        dz_ref[:, 0:W] = (d_u0 * sgc).astype(BF16)
        dz_ref[:, W:2 * W] = (d_u0 * cv * (sgc * (1.0 - sgc))).astype(BF16)
        dh = in_proj_bwd(0, 2 * W)

        xrv = xr_ref[...]
        xrb, r, ig, sp, a, mult = _lru_gates(xrv, bda_ref, bdx_ref, vec_ref)
        h = h_ref[...]
        gl, dgl = _gelu_parts(z_ref[:, 3 * W:4 * W])
        dz_ref[:, 3 * W:4 * W] = (d_yr * h * dgl).astype(BF16)
        dh = dh + in_proj_bwd(3 * W, 4 * W)
        a_next = jnp.where(row == tm - 1, 1.0, pltpu.roll(a, tm - 1, 0))
        g_first = _scan_rows(a_next, d_yr * gl, gc[0:1, :], gbuf, reverse=True)
        g = gbuf[...]
        gc[0:1, :] = a[0:1, :] * g_first
        hprev = jnp.where(row == 0, halo_on * hh_ref[LRU_HALO - 1:LRU_HALO, :], pltpu.roll(h, 1, 0))
        d_log_a = (g * hprev) * a - (g * ig * xrv) * (a * a) / mult
        d_ig = g * mult * xrv
        d_xr = g * mult * ig
        spacc[0:1, :] += jnp.sum(d_log_a * r, axis=0, keepdims=True)
        d_pa32 = (d_log_a * (-LRU_C * sp)) * (r * (1.0 - r))
        d_px32 = d_ig * (ig * (1.0 - ig))
        accum(G_BA, d_pa32)
        accum(G_BX, d_px32)
        d_pa = d_pa32.astype(BF16)
        d_px = d_px32.astype(BF16)
        d_xr = d_xr + _dot(d_pa, bda_ref[...], NT) + _dot(d_px, bdx_ref[...], NT)
        dbda_ref[...] += _dot(xrb, d_pa, TN)
        dbdx_ref[...] += _dot(xrb, d_px, TN)
        accum(G_LCB, d_xr)
        rxbuf[0:LRU_HALO, :] = halo_on * zh_ref[CONV_HALO - LRU_HALO:CONV_HALO, 2 * W:3 * W]
        rxbuf[LRU_HALO:LRU_HALO + tm, :] = z_ref[:, 2 * W:3 * W]
        dxrbuf[0:tm, :] = d_xr
        d_rx = jnp.zeros((tm, W), F32)
        for k in range(LRU_K):
            off = LRU_HALO - (LRU_K - 1) + k
            accum(G_LW + k, d_xr * rxbuf[off:off + tm, :])
            d_rx = d_rx + lw_ref[k:k + 1, :] * dxrbuf[LRU_K - 1 - k:LRU_K - 1 - k + tm, :]
        dxrbuf[tm:tm + LRU_HALO, :] = dxrbuf[0:LRU_HALO, :]
        dz_ref[:, 2 * W:3 * W] = d_rx.astype(BF16)
        dh = dh + in_proj_bwd(2 * W, 3 * W)

        dx, dg = _rms_bwd(x1hat, x1rstd, gain, dh)
        dx1 = dx_ref[...] + dx
        dx1_ref[...] = dx1
        dob_ref[...] = (FFN_RES * dx1).astype(BF16)
        dg_ref[...] += dg

        @pl.when(first)
        def _():
            lam = vec_ref[V_LAM:V_LAM + 1, :]
            sg_ref[G_LAM:G_LAM + 1, :] = LRU_C * _sigmoid(-lam) * spacc[0:1, :]
            dwi_ref[...] = ai_ref[...].astype(BF16)
            dwo_ref[...] = ao_ref[...].astype(BF16)

    full = lambda a: pl.BlockSpec(a.shape, lambda i: (0,) * a.ndim)
    tile = lambda n: pl.BlockSpec((tm, n), lambda i: (nt - 1 - i, 0))
    halo = lambda rows, n: pl.BlockSpec(
        (rows, n), lambda i: (jnp.maximum((nt - 1 - i) * (tm // rows) - 1, 0), 0))
    const = lambda r, c: pl.BlockSpec((r, c), lambda i: (0, 0))
    return _call(
        body, name=name, grid=(nt,),
        in_specs=[tile(D), tile(4 * W), halo(CONV_HALO, 4 * W), tile(W), tile(W), tile(W), halo(LRU_HALO, W),
                  full(w_out), full(bda), full(bdx), full(cw), full(lw), full(vec),
                  tile(D), tile(2 * W), full(g), full(w_in_t)],
        out_specs=[tile(D), const(G_ROWS, W), const(W, W), const(W, W),
                   tile(D), const(1, D), const(4 * W, D), const(2 * W, D)],
        out_shape=[jax.ShapeDtypeStruct((T, D), F32), jax.ShapeDtypeStruct((G_ROWS, W), F32),
                   jax.ShapeDtypeStruct((W, W), F32), jax.ShapeDtypeStruct((W, W), F32),
                   jax.ShapeDtypeStruct((T, D), BF16), jax.ShapeDtypeStruct((1, D), F32),
                   jax.ShapeDtypeStruct((4 * W, D), BF16), jax.ShapeDtypeStruct((2 * W, D), BF16)],
        scratch_shapes=[pltpu.VMEM((tm + CONV_HALO, W), F32), pltpu.VMEM((tm + CONV_HALO, W), F32),
                        pltpu.VMEM((tm + LRU_HALO, W), F32), pltpu.VMEM((tm + LRU_HALO, W), F32),
                        pltpu.VMEM((tm, W), F32), pltpu.VMEM((8, W), F32), pltpu.VMEM((8, W), F32),
                        pltpu.VMEM((tm, 4 * W), BF16), pltpu.VMEM((4 * W, D), F32), pltpu.VMEM((2 * W, D), F32)],
        vmem_mib=60, args=(dx2, z, z, u1, xr, hst, hst, w_out, bda, bdx, cw, lw, vec, x1, mix, g, w_in_t),
        rider=rider)


def _pair_add(full, recv, name):
    K, _, _, rows, D = full.shape

    def body(c_ref, a_ref, b_ref, o_ref):
        o_ref[...] = (a_ref[...].astype(F32) + b_ref[...].astype(F32)).astype(BF16)

    c = lax.axis_index("c").astype(jnp.int32).reshape((1,))
    return _call(
        body, name=name, grid=(K, N_CHIP), num_scalar_prefetch=1,
        in_specs=[pl.BlockSpec((None, None, None, rows, D), lambda k, q, c_ref: (k, q, c_ref[0], 0, 0)),
                  pl.BlockSpec((None, None, rows, D), lambda k, q, c_ref: (k, q, 0, 0))],
        out_specs=pl.BlockSpec((None, None, rows, D), lambda k, q, c_ref: (k, q, 0, 0)),
        out_shape=jax.ShapeDtypeStruct(recv.shape, BF16),
        scratch_shapes=[], vmem_mib=16, args=(c, full, recv))


def _adamw_update(wv, gv, mv, vv):
    m2 = ADAM_B1 * mv + (1.0 - ADAM_B1) * gv
    v2 = ADAM_B2 * vv + (1.0 - ADAM_B2) * (gv * gv)
    m_hat = m2 / (1.0 - ADAM_B1 ** ADAM_STEP)
    v_hat = v2 / (1.0 - ADAM_B2 ** ADAM_STEP)
    return -ADAM_LR * (m_hat / (jnp.sqrt(v_hat) + ADAM_EPS) + ADAM_WD * wv), m2, v2


def _finish(parts, k, w, m, v, transpose, name):
    _, n_parts, rows, D = parts.shape

    def body(p_ref, w_ref, m_ref, v_ref, g_ref, d_ref, mo_ref, vo_ref):
        acc = p_ref[0].astype(F32)
        for q in range(1, n_parts):
            acc = acc + p_ref[q].astype(F32)
        gv = acc.T if transpose else acc
        g_ref[...] = gv
        d_ref[...], mo_ref[...], vo_ref[...] = _adamw_update(w_ref[...], gv, m_ref[...], v_ref[...])

    whole = pl.BlockSpec(w.shape, lambda i: (0, 0))
    return _call(
        body, name=name, grid=(1,),
        in_specs=[pl.BlockSpec((None, n_parts, rows, D), lambda i: (k, 0, 0, 0)), whole, whole, whole],
        out_specs=[whole] * 4, out_shape=[pltpu.HBM(w.shape, F32)] * 4,
        scratch_shapes=[], vmem_mib=40, args=(parts, w, m, v))


def _finish_many(items, name, rider=None):
    n_items = len(items)
    D = items[0][2].shape[1]
    max_rows = max(w.shape[0] for _, _, w, _, _ in items)
    max_parts = max(parts.shape[1] for parts, _, _, _, _ in items)
    n_loads, n_stores = 4, 4

    def body(*refs):
        ins, outs = refs[:4 * n_items], refs[4 * n_items:8 * n_items]
        pbuf, wbuf, obuf, sem = refs[8 * n_items:]
        step = pl.program_id(0)
        for j, (parts, k, w, _, _) in enumerate(items):
            rows, n_parts = w.shape[0], parts.shape[1]

            @pl.when(step == j)
            def _():
                p_ref, w_ref, m_ref, v_ref = ins[4 * j:4 * j + 4]
                loads = [pltpu.make_async_copy(p_ref.at[k], pbuf.at[0:n_parts, 0:rows], sem.at[0])]
                loads += [pltpu.make_async_copy(src, wbuf.at[r, 0:rows], sem.at[1 + r])
                          for r, src in enumerate((w_ref, m_ref, v_ref))]
                for cp in loads:
                    cp.start()
                for cp in loads:
                    cp.wait()
                acc = pbuf[0, 0:rows, :].astype(F32)
                for q in range(1, n_parts):
                    acc = acc + pbuf[q, 0:rows, :].astype(F32)
                obuf[0, 0:rows, :] = acc
                obuf[1, 0:rows, :], obuf[2, 0:rows, :], obuf[3, 0:rows, :] = _adamw_update(
                    wbuf[0, 0:rows, :], acc, wbuf[1, 0:rows, :], wbuf[2, 0:rows, :])
                stores = [pltpu.make_async_copy(obuf.at[r, 0:rows], outs[4 * j + r], sem.at[n_loads + r])
                          for r in range(n_stores)]
                for cp in stores:
                    cp.start()
                for cp in stores:
                    cp.wait()

    args = [a for parts, _, w, m, v in items for a in (parts, w, m, v)]
    shapes = [jax.ShapeDtypeStruct(w.shape, F32) for _, _, w, _, _ in items for _ in range(4)]
    return _call(
        body, name=name, grid=(n_items,),
        in_specs=[HBM_SPEC] * len(args), out_specs=[HBM_SPEC] * len(shapes), out_shape=shapes,
        scratch_shapes=[pltpu.VMEM((max_parts, max_rows, D), BF16), pltpu.VMEM((3, max_rows, D), F32),
                        pltpu.VMEM((4, max_rows, D), F32), pltpu.SemaphoreType.DMA((n_loads + n_stores,))],
        vmem_mib=40, args=args, rider=rider)


def _adamw_each(ws, gs, ms, vs, name):
    n = len(ws)

    def body(*refs):
        w_refs, g_refs, m_refs, v_refs, outs = refs[:n], refs[n:2 * n], refs[2 * n:3 * n], refs[3 * n:4 * n], refs[4 * n:]
        for k in range(n):
            outs[k][...], outs[n + k][...], outs[2 * n + k][...] = _adamw_update(
                w_refs[k][...], g_refs[k][...], m_refs[k][...], v_refs[k][...])

    shapes = [jax.ShapeDtypeStruct(w.shape, F32) for w in ws]
    return pl.pallas_call(
        body, name=name,
        in_specs=[VMEM_SPEC] * (4 * n), out_specs=[VMEM_SPEC] * (3 * n), out_shape=shapes * 3,
        compiler_params=pltpu.CompilerParams(vmem_limit_bytes=32 * MIB),
    )(*ws, *gs, *ms, *vs)


def _block_diag(w):
    h, d, _ = w.shape
    onto = jnp.eye(h, dtype=w.dtype)
    return (w[:, :, None, :] * onto[:, None, :, None]).reshape(h * d, h * d)


def _diag_blocks(m, h):
    d = m.shape[0] // h
    onto = jnp.eye(h, dtype=m.dtype)
    return (m.reshape(h, d, h, d) * onto[:, None, :, None]).sum(axis=2)


def _reduce_level1(full, tag):
    got = _run_comm(_sibling_comm(full), "rs_sibling_" + tag)
    return [_pair_add(a, b, "rs_pair_add_%s%d" % (tag, n)) for n, (a, b) in enumerate(zip(full, got))]


def kernel(x, ffn1_norm, ffn1_w_gate, ffn1_w_up, ffn1_w_down, mix_norm, w_in, conv_dw, conv_dw_bias, conv_ln_g, conv_ln_b, lru_conv_w, lru_conv_b, lru_w_a, lru_b_a, lru_w_x, lru_b_x, lru_lambda, w_out, ffn2_norm, ffn2_w_gate, ffn2_w_up, ffn2_w_down, final_norm, loss_target, m_ffn1_norm, m_ffn1_w_gate, m_ffn1_w_up, m_ffn1_w_down, m_mix_norm, m_w_in, m_conv_dw, m_conv_dw_bias, m_conv_ln_g, m_conv_ln_b, m_lru_conv_w, m_lru_conv_b, m_lru_w_a, m_lru_b_a, m_lru_w_x, m_lru_b_x, m_lru_lambda, m_w_out, m_ffn2_norm, m_ffn2_w_gate, m_ffn2_w_up, m_ffn2_w_down, m_final_norm, v_ffn1_norm, v_ffn1_w_gate, v_ffn1_w_up, v_ffn1_w_down, v_mix_norm, v_w_in, v_conv_dw, v_conv_dw_bias, v_conv_ln_g, v_conv_ln_b, v_lru_conv_w, v_lru_conv_b, v_lru_w_a, v_lru_b_a, v_lru_w_x, v_lru_b_x, v_lru_lambda, v_w_out, v_ffn2_norm, v_ffn2_w_gate, v_ffn2_w_up, v_ffn2_w_down, v_final_norm):
    T, D = x.shape[1], x.shape[2]
    F = ffn1_w_down.shape[0] * N_DEV
    rf = ffn1_w_down.shape[0]
    ri = w_in.shape[1]
    ro = w_out.shape[0]
    W = conv_dw_bias.shape[0]
    wc = conv_dw.shape[1]
    H = lru_w_a.shape[0]
    xs = x.reshape(T, D)
    tgt = loss_target.reshape(T, D)
    tm_ffn = min(256, T)
    tm_fwd = min(512, T)
    cf = 256
    tm_w = min(1024, T)
    tm_w1 = min(2048, T)
    tm_mix = min(256, T)
    tf_w = F // 2
    row = lambda v: v.reshape(1, -1)
    by_owner = lambda a, rows: a.reshape(a.shape[0], N_CHIP, 2, rows, D)

    p3a, p3b, p_in, p_out = _prep_weights(
        (ffn1_w_gate.T, ffn1_w_up.T, ffn1_w_down), (ffn2_w_gate.T, ffn2_w_up.T, ffn2_w_down), w_in, w_out,
        "prep_weights")
    tile_rows = lambda a: jnp.pad(a, ((0, -a.shape[0] % SUBLANES), (0, 0)))
    p_cw = jnp.concatenate([tile_rows(conv_dw), tile_rows(lru_conv_w)], axis=0)
    lw_row = p_cw.shape[0] - SUBLANES
    stacked = lambda r, j: r.at[:, j]
    plain = lambda r, j: r.at[j]
    g3_shape = jax.ShapeDtypeStruct((3, N_DEV, rf, D), BF16)
    (g3a,) = _all_gather([p3a], [stacked], [g3_shape], "ag_ffn1")
    w3a = g3a.reshape(3, F, D)
    bda = _block_diag(lru_w_a).astype(BF16)
    bdx = _block_diag(lru_w_x).astype(BF16)
    vec = jnp.concatenate([tile_rows(v[None]) for v in
                           (conv_dw_bias, conv_ln_g, conv_ln_b, lru_conv_b, lru_b_a, lru_b_x, lru_lambda)], axis=0)

    gather_rest = _gather_comm(
        [p3b, p_in, p_out, p_cw], [stacked, plain, plain, plain],
        [g3_shape, jax.ShapeDtypeStruct((N_DEV, ri, D), BF16), jax.ShapeDtypeStruct((N_DEV, ro, D), BF16),
         jax.ShapeDtypeStruct((N_DEV,) + p_cw.shape, F32)],
        [(SIBLING,) + SAME_CORE, EVERYONE, EVERYONE, EVERYONE])
    (x1, h1, dau1, dag1, act1), (g3b_half, g_in, g_out, g_cw) = _ffn_fwd(
        xs, row(ffn1_norm), w3a, tm_fwd, cf, "ffn1_fwd", rider=gather_rest)
    w_in_t = g_in.reshape(N_DEV * ri, D)
    w_out_f = g_out.reshape(N_DEV * ro, D)
    cw_all = jnp.transpose(g_cw, (1, 0, 2)).reshape(p_cw.shape[0], N_DEV * wc)
    cw = cw_all[0:CONV_K]
    lw = cw_all[lw_row:lw_row + LRU_K]
    (x2, z, mix, u1, xr, hst), (g3b,) = _mix_core_fwd(
        x1, row(mix_norm), w_in_t, w_out_f, bda, bdx, cw, lw, vec, tm_mix, "mix_core_fwd",
        rider=_forward_comm([g3b_half], [stacked]))
    w3b = g3b.reshape(3, F, D)
    dx3, dob2, d_final_norm, loss_part, h3, dau2, dag2, act2 = _ffn_fwd(
        x2, row(ffn2_norm), w3b, tm_fwd, cf, "ffn2_fwd_loss", head=(row(final_norm), tgt))

    dx2, dgate2, dup2, d_ffn2_norm = _ffn_dgrad(dx3, x2, row(ffn2_norm), dau2, dag2, w3b, tm_ffn, cf, "ffn2_dgrad")
    (dw_gu2,) = _wgrad([dgate2, dup2], h3, tm_w, tf_w, "ffn2_wgrad_gu")
    (dw_d2,) = _wgrad([act2], dob2, tm_w1, tf_w, "ffn2_wgrad_d")
    by_device = lambda a: a.reshape(a.shape[0], N_DEV, rf, D)
    (dx1, sg, dbda, dbdx, dob1, d_mix_norm, dw_in_t, dw_out), parts_f2 = _mix_bwd(
        dx2, z, u1, xr, hst, x1, mix, row(mix_norm), w_in_t, w_out_f, bda, bdx, cw, lw, vec, tm_mix, "mix_bwd",
        rider=_chips_comm([by_device(dw_gu2), by_device(dw_d2)], every_device=True))
    io = [dw_in_t.reshape(1, N_DEV, ri, D), dw_out.reshape(1, N_DEV, ro, D)]
    (dw_d1,), parts_io = _wgrad(
        [act1], dob1, tm_w1, tf_w, "ffn1_wgrad_d", rider=_chips_comm(io, every_device=True))
    sums_d1 = _reduce_level1([by_owner(dw_d1, rf)], "d1")
    dx0, dgate1, dup1, d_ffn1_norm = _ffn_dgrad(dx1, xs, row(ffn1_norm), dau1, dag1, w3a, tm_ffn, cf, "ffn1_dgrad")
    (dw_g1,), parts_d1 = _wgrad([dgate1], h1, tm_w1, tf_w, "ffn1_wgrad_g", rider=_chips_comm(sums_d1))
    sums_g1 = _reduce_level1([by_owner(dw_g1, rf)], "g1")
    small = [d_ffn1_norm, d_mix_norm, d_ffn2_norm, d_final_norm, sg, _diag_blocks(dbda, H).reshape(-1, D),
             _diag_blocks(dbdx, H).reshape(-1, D), loss_part]
    (dw_u1,), summed_and_parts = _wgrad(
        [dup1], h1, tm_w1, tf_w, "ffn1_wgrad_u", rider=_both(_small_sum_comm(small), _chips_comm(sums_g1)))
    summed, parts_g1 = summed_and_parts[:len(small)], summed_and_parts[len(small):]
    sums_u1 = _reduce_level1([by_owner(dw_u1, rf)], "u1")

    g_norm1, g_norm_mix, g_norm2, g_norm_final, g_sg, g_w_a, g_w_x, g_loss = summed
    loss = g_loss[0, 0]
    me = 4 * lax.axis_index("x") + 2 * lax.axis_index("y") + lax.axis_index("c")
    chan = lambda full_g: lax.dynamic_slice_in_dim(full_g, me * wc, wc, axis=1)
    grads = {
        "ffn1_norm": g_norm1.reshape(D), "mix_norm": g_norm_mix.reshape(D), "ffn2_norm": g_norm2.reshape(D),
        "final_norm": g_norm_final.reshape(D),
        "conv_dw_bias": g_sg[G_CB], "conv_ln_g": g_sg[G_LNG], "conv_ln_b": g_sg[G_LNB],
        "lru_conv_b": g_sg[G_LCB], "lru_b_a": g_sg[G_BA], "lru_b_x": g_sg[G_BX], "lru_lambda": g_sg[G_LAM],
        "lru_w_a": g_w_a.reshape(lru_w_a.shape), "lru_w_x": g_w_x.reshape(lru_w_x.shape),
        "conv_dw": chan(g_sg[G_CW:G_CW + CONV_K]), "lru_conv_w": chan(g_sg[G_LW:G_LW + LRU_K]),
    }

    weights = dict(ffn1_norm=ffn1_norm, ffn1_w_gate=ffn1_w_gate, ffn1_w_up=ffn1_w_up, ffn1_w_down=ffn1_w_down, mix_norm=mix_norm, w_in=w_in, conv_dw=conv_dw, conv_dw_bias=conv_dw_bias, conv_ln_g=conv_ln_g, conv_ln_b=conv_ln_b, lru_conv_w=lru_conv_w, lru_conv_b=lru_conv_b, lru_w_a=lru_w_a, lru_b_a=lru_b_a, lru_w_x=lru_w_x, lru_b_x=lru_b_x, lru_lambda=lru_lambda, w_out=w_out, ffn2_norm=ffn2_norm, ffn2_w_gate=ffn2_w_gate, ffn2_w_up=ffn2_w_up, ffn2_w_down=ffn2_w_down, final_norm=final_norm)
    moment1 = dict(ffn1_norm=m_ffn1_norm, ffn1_w_gate=m_ffn1_w_gate, ffn1_w_up=m_ffn1_w_up, ffn1_w_down=m_ffn1_w_down, mix_norm=m_mix_norm, w_in=m_w_in, conv_dw=m_conv_dw, conv_dw_bias=m_conv_dw_bias, conv_ln_g=m_conv_ln_g, conv_ln_b=m_conv_ln_b, lru_conv_w=m_lru_conv_w, lru_conv_b=m_lru_conv_b, lru_w_a=m_lru_w_a, lru_b_a=m_lru_b_a, lru_w_x=m_lru_w_x, lru_b_x=m_lru_b_x, lru_lambda=m_lru_lambda, w_out=m_w_out, ffn2_norm=m_ffn2_norm, ffn2_w_gate=m_ffn2_w_gate, ffn2_w_up=m_ffn2_w_up, ffn2_w_down=m_ffn2_w_down, final_norm=m_final_norm)
    moment2 = dict(ffn1_norm=v_ffn1_norm, ffn1_w_gate=v_ffn1_w_gate, ffn1_w_up=v_ffn1_w_up, ffn1_w_down=v_ffn1_w_down, mix_norm=v_mix_norm, w_in=v_w_in, conv_dw=v_conv_dw, conv_dw_bias=v_conv_dw_bias, conv_ln_g=v_conv_ln_g, conv_ln_b=v_conv_ln_b, lru_conv_w=v_lru_conv_w, lru_conv_b=v_lru_conv_b, lru_w_a=v_lru_w_a, lru_b_a=v_lru_b_a, lru_w_x=v_lru_w_x, lru_b_x=v_lru_b_x, lru_lambda=v_lru_lambda, w_out=v_w_out, ffn2_norm=v_ffn2_norm, ffn2_w_gate=v_ffn2_w_gate, ffn2_w_up=v_ffn2_w_up, ffn2_w_down=v_ffn2_w_down, final_norm=v_final_norm)
    order = list(weights)
    gate_up = {"ffn1_w_gate", "ffn1_w_up", "ffn2_w_gate", "ffn2_w_up"}
    view = lambda n, a: a.T if n in gate_up else a
    operands = lambda n: [view(n, d[n]) for d in (weights, moment1, moment2)]
    delta, new_m, new_v = {}, {}, {}
    six = {"ffn1_w_gate": (parts_g1[0], 0), "ffn1_w_down": (parts_d1[0], 0), "w_out": (parts_io[1], 0),
           "ffn2_w_gate": (parts_f2[0], 0), "ffn2_w_up": (parts_f2[0], 1), "ffn2_w_down": (parts_f2[1], 0)}
    results, parts_u1 = _finish_many([(parts, k, *operands(n)) for n, (parts, k) in six.items()], "finish_six",
                                     rider=_chips_comm(sums_u1))
    for j, n in enumerate(six):
        grads[n], delta[n], new_m[n], new_v[n] = [view(n, r) for r in results[4 * j:4 * j + 4]]
    for n, parts, d_major in (("ffn1_w_up", parts_u1[0], False), ("w_in", parts_io[0], True)):
        grads[n], delta[n], new_m[n], new_v[n] = [
            view(n, r) for r in _finish(parts, 0, *operands(n), d_major, "finish_" + n)]
    big = set(six) | {"ffn1_w_up", "w_in"}
    rest = [n for n in order if n not in big]
    updates = _adamw_each([weights[n] for n in rest], [grads[n] for n in rest], [moment1[n] for n in rest],
                          [moment2[n] for n in rest], "adamw_small")
    for k, n in enumerate(rest):
        delta[n], new_m[n], new_v[n] = updates[k], updates[len(rest) + k], updates[2 * len(rest) + k]

    return (loss, dx0.reshape(x.shape), *[grads[n] for n in order], *[delta[n] for n in order],
            *[new_m[n] for n in order], *[new_v[n] for n in order])
```

```python
import functools
import math

import jax
import jax.numpy as jnp
from jax import lax
from jax.experimental import pallas as pl
from jax.experimental.pallas import tpu as pltpu

F32 = jnp.float32
BF16 = jnp.bfloat16
MESH = pl.DeviceIdType.MESH

N_DEV = 8
N_CHIP = 4
SUBLANES = 8
RMS_EPS = 1e-6
LN_EPS = 1e-5
LRU_C = 8.0
CONV_K = 31
LRU_K = 4
CONV_HALO = 32
LRU_HALO = 8
FFN_RES = 0.5
ADAM_LR, ADAM_B1, ADAM_B2, ADAM_EPS, ADAM_WD, ADAM_STEP = 0.001, 0.9, 0.999, 1e-08, 0.01, 10
GELU_K = math.sqrt(2.0 / math.pi)
GELU_C = 0.044715

MIB = 1024 * 1024
NT = (((1,), (1,)), ((), ()))
NN = (((1,), (0,)), ((), ()))
TN = (((0,), (0,)), ((), ()))

V_CB, V_LNG, V_LNB, V_LCB, V_BA, V_BX, V_LAM = range(0, 7 * SUBLANES, SUBLANES)
G_CW = 0
G_CB, G_LNG, G_LNB = 31, 32, 33
G_LW = 34
G_LCB, G_BA, G_BX, G_LAM = 38, 39, 40, 41
G_ROWS = 48

HBM_SPEC = pl.BlockSpec(memory_space=pltpu.HBM)
VMEM_SPEC = pl.BlockSpec(memory_space=pltpu.VMEM)


def _dot(a, b, dims):
    return lax.dot_general(a, b, dims, preferred_element_type=F32)


def _sigmoid(x):
    return 1.0 / (1.0 + jnp.exp(-x))


def _gelu_parts(x):
    x2 = x * x
    th = jnp.tanh(GELU_K * x * (1.0 + GELU_C * x2))
    gl = 0.5 * x * (1.0 + th)
    dgl = 0.5 * (1.0 + th) + 0.5 * x * (1.0 - th * th) * GELU_K * (1.0 + 3.0 * GELU_C * x2)
    return gl, dgl


def _neg_expm1(y):
    series = -y * (1.0 + y * (1.0 / 2) * (1.0 + y * (1.0 / 3) * (1.0 + y * (1.0 / 4) * (1.0 + y * (1.0 / 5) * (1.0 + y * (1.0 / 6))))))
    return jnp.where(y > -0.25, series, 1.0 - jnp.exp(y))


def _softplus_neg(lam):
    t = -lam
    e = jnp.exp(-jnp.abs(t))
    s = 1.0 + e
    log1p_e = jnp.log(s) - ((s - 1.0) - e) / s
    return jnp.maximum(t, 0.0) + log1p_e


def _rms_stats(xv):
    rstd = lax.rsqrt(jnp.mean(xv * xv, axis=-1, keepdims=True) + RMS_EPS)
    return xv * rstd, rstd


def _rms_bwd(xhat, rstd, g, dh):
    dxhat = dh * g
    dx = rstd * (dxhat - xhat * jnp.mean(dxhat * xhat, axis=-1, keepdims=True))
    return dx, jnp.sum(dh * xhat, axis=0, keepdims=True)


def _row_windows(buf_ref, n_rows, offsets):
    total = buf_ref.shape[0]
    full = buf_ref[...]
    for b in range(SUBLANES):
        offs = [o for o in offsets if o % SUBLANES == b]
        if not offs:
            continue
        assert max(offs) + n_rows <= total
        moved = full if b == 0 else pltpu.roll(full, total - b, 0)
        for o in offs:
            yield o, moved[o - b:o - b + n_rows, :]


def _scan_rows(av, bv, edge, out_ref, reverse=False):
    tm, W = av.shape
    sub = lax.broadcasted_iota(jnp.int32, (tm, W), 0) % SUBLANES
    s = 1
    while s < SUBLANES:
        keep = (sub < SUBLANES - s) if reverse else (sub >= s)
        shift = tm - s if reverse else s
        bv = jnp.where(keep, av * pltpu.roll(bv, shift, 0) + bv, bv)
        av = jnp.where(keep, av * pltpu.roll(av, shift, 0), av)
        s *= 2
    starts = range(0, tm, SUBLANES)
    for r0 in (reversed(starts) if reverse else starts):
        group = av[r0:r0 + SUBLANES, :] * edge + bv[r0:r0 + SUBLANES, :]
        out_ref[r0:r0 + SUBLANES, :] = group
        edge = group[0:1, :] if reverse else group[SUBLANES - 1:SUBLANES, :]
    return edge


class _Comm:
    def __init__(self, arrays, in_specs, out_shapes, out_specs, scratch, start, wait, aliases=None):
        self.arrays, self.in_specs = list(arrays), list(in_specs)
        self.out_shapes, self.out_specs = list(out_shapes), list(out_specs)
        self.scratch, self.start, self.wait = list(scratch), start, wait
        self.aliases = dict(aliases or {})


def _in_hbm(a):
    return pltpu.with_memory_space_constraint(a, pltpu.HBM)


def _operands(comm):
    return [a if spec is VMEM_SPEC else _in_hbm(a) for a, spec in zip(comm.arrays, comm.in_specs)]


def _call(body, *, name, grid, in_specs, out_specs, out_shape, scratch_shapes, vmem_mib, args, rider=None,
          num_scalar_prefetch=0):
    params = pltpu.CompilerParams(dimension_semantics=("arbitrary",) * len(grid), vmem_limit_bytes=vmem_mib * MIB)
    args = [a if k < num_scalar_prefetch else _in_hbm(a) for k, a in enumerate(args)]
    if rider is None:
        return pl.pallas_call(
            body, name=name,
            grid_spec=pltpu.PrefetchScalarGridSpec(
                num_scalar_prefetch=num_scalar_prefetch, grid=grid, in_specs=in_specs, out_specs=out_specs,
                scratch_shapes=scratch_shapes),
            out_shape=out_shape, compiler_params=params)(*args)
    assert num_scalar_prefetch == 0
    n_in, n_out, n_scr = len(in_specs), len(out_specs), len(scratch_shapes)
    r_in, r_out = len(rider.arrays), len(rider.out_shapes)
    n_axes = len(grid)

    def carried(*refs):
        pos = [0]

        def take(n):
            pos[0] += n
            return refs[pos[0] - n:pos[0]]

        ins, r_ins, outs, r_outs, scr, r_scr = take(n_in), take(r_in), take(n_out), take(r_out), take(n_scr), take(len(rider.scratch))
        first = pl.program_id(0) == 0
        last = pl.program_id(0) == grid[0] - 1
        for ax in range(1, n_axes):
            first = first & (pl.program_id(ax) == 0)
            last = last & (pl.program_id(ax) == grid[ax] - 1)

        @pl.when(first)
        def _():
            rider.start(r_ins, r_outs, r_scr)

        body(*ins, *outs, *scr)

        @pl.when(last)
        def _():
            rider.wait(r_ins, r_outs, r_scr)

    res = pl.pallas_call(
        carried, name=name,
        grid=grid,
        in_specs=list(in_specs) + rider.in_specs,
        out_specs=list(out_specs) + rider.out_specs,
        out_shape=list(out_shape) + rider.out_shapes,
        scratch_shapes=list(scratch_shapes) + rider.scratch,
        input_output_aliases={n_in + i: n_out + o for i, o in rider.aliases.items()},
        compiler_params=params)(*args, *_operands(rider))
    return res[:n_out], res[n_out:]


def _run_comm(comm, name):
    n_in, n_out = len(comm.arrays), len(comm.out_shapes)

    def body(*refs):
        ins, outs, scr = refs[:n_in], refs[n_in:n_in + n_out], refs[n_in + n_out:]
        comm.start(ins, outs, scr)
        comm.wait(ins, outs, scr)

    return pl.pallas_call(
        body, name=name,
        in_specs=comm.in_specs, out_specs=comm.out_specs, out_shape=comm.out_shapes,
        scratch_shapes=comm.scratch, input_output_aliases=comm.aliases,
        compiler_params=pltpu.CompilerParams(vmem_limit_bytes=24 * MIB))(*_operands(comm))


def _both(a, b):
    ni, no, ns = len(a.arrays), len(a.out_shapes), len(a.scratch)

    def start(ins, outs, scr):
        a.start(ins[:ni], outs[:no], scr[:ns])
        b.start(ins[ni:], outs[no:], scr[ns:])

    def wait(ins, outs, scr):
        a.wait(ins[:ni], outs[:no], scr[:ns])
        b.wait(ins[ni:], outs[no:], scr[ns:])

    aliases = dict(a.aliases)
    aliases.update({ni + i: no + o for i, o in b.aliases.items()})
    return _Comm(a.arrays + b.arrays, a.in_specs + b.in_specs, a.out_shapes + b.out_shapes,
                 a.out_specs + b.out_specs, a.scratch + b.scratch, start, wait, aliases)


def _place():
    return lax.axis_index("x"), lax.axis_index("y"), lax.axis_index("c")


def _peer(k):
    x, y, c = _place()
    px, py, pc = x ^ ((k >> 2) & 1), y ^ ((k >> 1) & 1), c ^ (k & 1)
    return (px, py, pc), 4 * px + 2 * py + pc


SIBLING = 1
SAME_CORE = (2, 4, 6)
EVERYONE = tuple(range(1, N_DEV))


def _gather_comm(shards, views, out_shapes, relations):
    na = len(shards)

    def copies(ins, outs, scr):
        send_sems, recv_sems, _ = scr
        _, me = _peer(0)
        out = []
        for a in range(na):
            for k in relations[a]:
                peer, theirs = _peer(k)
                send = functools.partial(
                    pltpu.make_async_remote_copy,
                    src_ref=ins[a], dst_ref=views[a](outs[a], me),
                    send_sem=send_sems.at[7 * a + k - 1], recv_sem=recv_sems.at[7 * a + k - 1],
                    device_id=peer, device_id_type=MESH)
                recv = functools.partial(
                    pltpu.make_async_remote_copy,
                    src_ref=ins[a], dst_ref=views[a](outs[a], theirs),
                    send_sem=send_sems.at[7 * a + k - 1], recv_sem=recv_sems.at[7 * a + k - 1],
                    device_id=peer, device_id_type=MESH)
                out.append((send, recv))
        return out

    def local(ins, outs, scr):
        _, me = _peer(0)
        return [pltpu.make_async_copy(ins[a], views[a](outs[a], me), scr[2].at[a]) for a in range(na)]

    def start(ins, outs, scr):
        for cp in local(ins, outs, scr):
            cp.start()
        for send, _ in copies(ins, outs, scr):
            send().start()

    def wait(ins, outs, scr):
        for _, recv in copies(ins, outs, scr):
            recv().wait_recv()
        for send, _ in copies(ins, outs, scr):
            send().wait_send()
        for cp in local(ins, outs, scr):
            cp.wait()

    return _Comm(shards, [HBM_SPEC] * na, out_shapes, [HBM_SPEC] * na,
                 [pltpu.SemaphoreType.DMA((7 * na,)), pltpu.SemaphoreType.DMA((7 * na,)),
                  pltpu.SemaphoreType.DMA((na,))], start, wait)


def _forward_comm(gathered, views):
    na = len(gathered)
    shapes = [jax.ShapeDtypeStruct(g.shape, g.dtype) for g in gathered]

    def copies(outs, scr):
        send_sems, recv_sems = scr
        sibling, _ = _peer(SIBLING)
        out = []
        for a in range(na):
            for n, k in enumerate(SAME_CORE):
                _, mine = _peer(k)
                _, theirs = _peer(k ^ SIBLING)
                send = functools.partial(
                    pltpu.make_async_remote_copy,
                    src_ref=views[a](outs[a], mine), dst_ref=views[a](outs[a], mine),
                    send_sem=send_sems.at[3 * a + n], recv_sem=recv_sems.at[3 * a + n],
                    device_id=sibling, device_id_type=MESH)
                recv = functools.partial(
                    pltpu.make_async_remote_copy,
                    src_ref=views[a](outs[a], mine), dst_ref=views[a](outs[a], theirs),
                    send_sem=send_sems.at[3 * a + n], recv_sem=recv_sems.at[3 * a + n],
                    device_id=sibling, device_id_type=MESH)
                out.append((send, recv))
        return out

    def start(ins, outs, scr):
        for send, _ in copies(outs, scr):
            send().start()

    def wait(ins, outs, scr):
        for _, recv in copies(outs, scr):
            recv().wait_recv()
        for send, _ in copies(outs, scr):
            send().wait_send()

    return _Comm(gathered, [HBM_SPEC] * na, shapes, [HBM_SPEC] * na,
                 [pltpu.SemaphoreType.DMA((3 * na,)), pltpu.SemaphoreType.DMA((3 * na,))], start, wait,
                 aliases={a: a for a in range(na)})


Y_NEIGHBOUR, X_NEIGHBOUR, DIAGONAL = SAME_CORE


def _all_gather(shards, views, out_shapes, name):
    na = len(shards)
    near = (SIBLING, Y_NEIGHBOUR, X_NEIGHBOUR)
    level1 = _gather_comm(shards, views, out_shapes, [near] * na)

    def body(*refs):
        ins, outs = refs[:na], refs[na:2 * na]
        send_sems, recv_sems, local_sems, fwd_send, fwd_recv, relay_send, relay_recv = refs[2 * na:]
        sibling, _ = _peer(SIBLING)
        c = lax.axis_index("c")
        level1.start(ins, outs, (send_sems, recv_sems, local_sems))

        def block_copy(a, block, to, send_sem, recv_sem):
            return pltpu.make_async_remote_copy(
                src_ref=views[a](outs[a], block), dst_ref=views[a](outs[a], block),
                send_sem=send_sem, recv_sem=recv_sem, device_id=to, device_id_type=MESH)

        def to_sibling(a, n, k):
            _, mine = _peer(k)
            _, theirs = _peer(k ^ SIBLING)
            fwd = block_copy(a, mine, sibling, fwd_send.at[3 * a + n], fwd_recv.at[3 * a + n])
            fwd.start()
            return fwd, block_copy(a, theirs, sibling, fwd_send.at[3 * a + n], fwd_recv.at[3 * a + n])

        passed, landing = [], []
        for a in range(na):
            for n, k in enumerate((Y_NEIGHBOUR, X_NEIGHBOUR)):
                peer, origin = _peer(k)
                pltpu.make_async_remote_copy(
                    src_ref=ins[a], dst_ref=views[a](outs[a], origin),
                    send_sem=send_sems.at[7 * a + k - 1], recv_sem=recv_sems.at[7 * a + k - 1],
                    device_id=peer, device_id_type=MESH).wait_recv()

                @pl.when(c == (0 if k == X_NEIGHBOUR else 1))
                def _():
                    other, _ = _peer(DIAGONAL ^ k)
                    block_copy(a, origin, other, relay_send.at[a], relay_recv.at[a]).start()

                fwd, lands = to_sibling(a, n, k)
                passed.append(fwd)
                landing.append(lands)
            _, far = _peer(DIAGONAL)
            block_copy(a, far, sibling, relay_send.at[a], relay_recv.at[a]).wait_recv()
            fwd, lands = to_sibling(a, 2, DIAGONAL)
            passed.append(fwd)
            landing.append(lands)
        for a in range(na):
            _, theirs = _peer(SIBLING)
            pltpu.make_async_remote_copy(
                src_ref=ins[a], dst_ref=views[a](outs[a], theirs),
                send_sem=send_sems.at[7 * a + SIBLING - 1], recv_sem=recv_sems.at[7 * a + SIBLING - 1],
                device_id=sibling, device_id_type=MESH).wait_recv()
        for cp in landing:
            cp.wait_recv()
        for cp in passed:
            cp.wait_send()
        _, me = _peer(0)
        for a in range(na):
            block_copy(a, me, sibling, relay_send.at[a], relay_recv.at[a]).wait_send()
            for k in near:
                peer, _ = _peer(k)
                pltpu.make_async_remote_copy(
                    src_ref=ins[a], dst_ref=views[a](outs[a], me),
                    send_sem=send_sems.at[7 * a + k - 1], recv_sem=recv_sems.at[7 * a + k - 1],
                    device_id=peer, device_id_type=MESH).wait_send()
            pltpu.make_async_copy(ins[a], views[a](outs[a], me), local_sems.at[a]).wait()

    return pl.pallas_call(
        body, name=name,
        in_specs=[HBM_SPEC] * na, out_specs=[HBM_SPEC] * na, out_shape=out_shapes,
        scratch_shapes=level1.scratch + [pltpu.SemaphoreType.DMA((3 * na,)), pltpu.SemaphoreType.DMA((3 * na,)),
                                         pltpu.SemaphoreType.DMA((na,)), pltpu.SemaphoreType.DMA((na,))],
    )(*[_in_hbm(s) for s in shards])


def _sibling_comm(grads):
    na = len(grads)
    shapes = [jax.ShapeDtypeStruct(g.shape[:2] + g.shape[3:], g.dtype) for g in grads]

    def copies(ins, outs, scr):
        x, y, c = _place()
        return [pltpu.make_async_remote_copy(
            src_ref=ins[a].at[:, :, 1 - c], dst_ref=outs[a],
            send_sem=scr[0].at[a], recv_sem=scr[1].at[a],
            device_id=(x, y, 1 - c), device_id_type=MESH) for a in range(na)]

    def start(ins, outs, scr):
        for cp in copies(ins, outs, scr):
            cp.start()

    def wait(ins, outs, scr):
        for cp in copies(ins, outs, scr):
            cp.wait()

    return _Comm(grads, [HBM_SPEC] * na, shapes, [HBM_SPEC] * na,
                 [pltpu.SemaphoreType.DMA((na,)), pltpu.SemaphoreType.DMA((na,))], start, wait)


def _chips_comm(sums, every_device=False):
    na = len(sums)
    shapes = [jax.ShapeDtypeStruct(s.shape, s.dtype) for s in sums]
    relations = EVERYONE if every_device else SAME_CORE
    nr = len(relations)

    def block(px, py, pc):
        return 4 * px + 2 * py + pc if every_device else 2 * px + py

    def copies(ins, outs, scr):
        mine = block(*_place())
        out = []
        for a in range(na):
            for n, k in enumerate(relations):
                peer, _ = _peer(k)
                theirs = block(*peer)
                send = functools.partial(
                    pltpu.make_async_remote_copy,
                    src_ref=ins[a].at[:, theirs], dst_ref=outs[a].at[:, mine],
                    send_sem=scr[0].at[nr * a + n], recv_sem=scr[1].at[nr * a + n],
                    device_id=peer, device_id_type=MESH)
                recv = functools.partial(
                    pltpu.make_async_remote_copy,
                    src_ref=ins[a].at[:, mine], dst_ref=outs[a].at[:, theirs],
                    send_sem=scr[0].at[nr * a + n], recv_sem=scr[1].at[nr * a + n],
                    device_id=peer, device_id_type=MESH)
                out.append((send, recv))
        return out

    def local(ins, outs, scr):
        mine = block(*_place())
        return [pltpu.make_async_copy(ins[a].at[:, mine], outs[a].at[:, mine], scr[2].at[a]) for a in range(na)]

    def start(ins, outs, scr):
        for cp in local(ins, outs, scr):
            cp.start()
        for send, _ in copies(ins, outs, scr):
            send().start()

    def wait(ins, outs, scr):
        for _, recv in copies(ins, outs, scr):
            recv().wait_recv()
        for send, _ in copies(ins, outs, scr):
            send().wait_send()
        for cp in local(ins, outs, scr):
            cp.wait()

    return _Comm(sums, [HBM_SPEC] * na, shapes, [HBM_SPEC] * na,
                 [pltpu.SemaphoreType.DMA((nr * na,)), pltpu.SemaphoreType.DMA((nr * na,)),
                  pltpu.SemaphoreType.DMA((na,))], start, wait)


def _small_sum_comm(arrays):
    na = len(arrays)

    def copies(ins, scr):
        bufs, send_sems, recv_sems = scr[:na], scr[na], scr[na + 1]
        _, me = _peer(0)
        out = []
        for a in range(na):
            for k in EVERYONE:
                peer, theirs = _peer(k)
                sems = dict(send_sem=send_sems.at[7 * a + k - 1], recv_sem=recv_sems.at[7 * a + k - 1])
                send = functools.partial(
                    pltpu.make_async_remote_copy,
                    src_ref=ins[a], dst_ref=bufs[a].at[me], device_id=peer, device_id_type=MESH, **sems)
                recv = functools.partial(
                    pltpu.make_async_remote_copy,
                    src_ref=ins[a], dst_ref=bufs[a].at[theirs], device_id=peer, device_id_type=MESH, **sems)
                out.append((send, recv))
        return out

    def start(ins, outs, scr):
        _, me = _peer(0)
        for a in range(na):
            scr[a][me] = ins[a][...]
        for send, _ in copies(ins, scr):
            send().start()

    def wait(ins, outs, scr):
        for _, recv in copies(ins, scr):
            recv().wait_recv()
        for send, _ in copies(ins, scr):
            send().wait_send()
        for a in range(na):
            acc = scr[a][0]
            for j in range(1, N_DEV):
                acc = acc + scr[a][j]
            outs[a][...] = acc

    return _Comm(arrays, [VMEM_SPEC] * na, [jax.ShapeDtypeStruct(s.shape, F32) for s in arrays], [VMEM_SPEC] * na,
                 [pltpu.VMEM((N_DEV,) + s.shape, F32) for s in arrays]
                 + [pltpu.SemaphoreType.DMA((7 * na,)), pltpu.SemaphoreType.DMA((7 * na,))], start, wait)


def _prep_weights(ffn1, ffn2, w_in, w_out, name):
    rf, D = ffn1[2].shape
    ri, ro = w_in.shape[1], w_out.shape[0]

    def body(g1, u1, d1, g2, u2, d2, wi, wo, p1_ref, p2_ref, pi_ref, po_ref):
        for p_ref, shards in ((p1_ref, (g1, u1, d1)), (p2_ref, (g2, u2, d2))):
            for k, shard in enumerate(shards):
                p_ref[k] = shard[...].astype(BF16)
        pi_ref[...] = wi[...].T.astype(BF16)
        po_ref[...] = wo[...].astype(BF16)

    args = (*ffn1, *ffn2, w_in, w_out)
    whole = lambda shape: pl.BlockSpec(shape, lambda i: (0,) * len(shape))
    out_shapes = [(3, rf, D), (3, rf, D), (ri, D), (ro, D)]
    return _call(
        body, name=name, grid=(1,),
        in_specs=[whole(a.shape) for a in args], out_specs=[whole(s) for s in out_shapes],
        out_shape=[jax.ShapeDtypeStruct(s, BF16) for s in out_shapes],
        scratch_shapes=[], vmem_mib=48, args=args)


def _load_weights(w_hbm, w_vmem, sem):
    @pl.when(pl.program_id(0) == 0)
    def _():
        copies = [pltpu.make_async_copy(w_hbm.at[k], w_vmem.at[k], sem.at[k]) for k in range(3)]
        for cp in copies:
            cp.start()
        for cp in copies:
            cp.wait()


def _ffn_fwd(x, g, w3, tm, cf, name, rider=None, head=None):
    T, D = x.shape
    F = w3.shape[1]
    n_head = 0 if head is None else 2

    def body(x_ref, g_ref, w_hbm, *refs):
        head_refs, refs = refs[:n_head], refs[n_head:]
        if head is None:
            (xo_ref, h_ref, dau_ref, dag_ref, act_ref, wv, sem) = refs
        else:
            (dx_ref, dob_ref, dgf_ref, loss_ref, h_ref, dau_ref, dag_ref, act_ref, wv, sem) = refs
        _load_weights(w_hbm, wv, sem)
        xhat, _ = _rms_stats(x_ref[...])
        hb = (xhat * g_ref[...]).astype(BF16)
        h_ref[...] = hb
        for lo in range(0, F, cf):
            gate = _dot(hb, wv[0, lo:lo + cf, :], NT)
            up = _dot(hb, wv[1, lo:lo + cf, :], NT)
            sig = _sigmoid(gate)
            silu = gate * sig
            dau_ref[:, lo:lo + cf] = silu.astype(BF16)
            dag_ref[:, lo:lo + cf] = (up * (sig * (1.0 + gate * (1.0 - sig)))).astype(BF16)
            act_ref[:, lo:lo + cf] = (silu * up).astype(BF16)
        x_out = x_ref[...] + FFN_RES * _dot(act_ref[...], wv[2], NN)
        if head is None:
            xo_ref[...] = x_out
            return

        @pl.when(pl.program_id(0) == 0)
        def _():
            dgf_ref[...] = jnp.zeros_like(dgf_ref)
            loss_ref[...] = jnp.zeros_like(loss_ref)

        gf_ref, tgt_ref = head_refs
        yhat, rstd = _rms_stats(x_out)
        gf = gf_ref[...]
        err = yhat * gf - tgt_ref[...]
        loss_ref[...] += (0.5 / D) * jnp.sum(err * err)
        dx, dgf = _rms_bwd(yhat, rstd, gf, err * (1.0 / D))
        dx_ref[...] = dx
        dob_ref[...] = (FFN_RES * dx).astype(BF16)
        dgf_ref[...] += dgf

    row = pl.BlockSpec((tm, D), lambda i: (i, 0))
    hid = pl.BlockSpec((tm, F), lambda i: (i, 0))
    vec = pl.BlockSpec((1, D), lambda i: (0, 0))
    row_f32, row_bf16 = jax.ShapeDtypeStruct((T, D), F32), jax.ShapeDtypeStruct((T, D), BF16)
    if head is None:
        first_specs, first_shapes = [row], [row_f32]
    else:
        first_specs = [row, row, vec, pl.BlockSpec((1, 128), lambda i: (0, 0))]
        first_shapes = [row_f32, row_bf16, jax.ShapeDtypeStruct((1, D), F32), jax.ShapeDtypeStruct((1, 128), F32)]
    return _call(
        body, name=name, grid=(T // tm,),
        in_specs=[row, vec, HBM_SPEC] + ([] if head is None else [vec, row]),
        out_specs=first_specs + [row, hid, hid, hid],
        out_shape=first_shapes + [row_bf16] + [jax.ShapeDtypeStruct((T, F), BF16)] * 3,
        scratch_shapes=[pltpu.VMEM((3, F, D), BF16), pltpu.SemaphoreType.DMA((3,))],
        vmem_mib=60, args=(x, g, w3) + (() if head is None else tuple(head)), rider=rider)


def _ffn_dgrad(dout, x, g, dau, dag, w3, tm, cf, name, rider=None):
    T, D = x.shape
    F = w3.shape[1]

    def body(do_ref, x_ref, g_ref, dau_ref, dag_ref, w_hbm, dx_ref, dgate_ref, dup_ref, dg_ref, wv, sem):
        _load_weights(w_hbm, wv, sem)

        @pl.when(pl.program_id(0) == 0)
        def _():
            dg_ref[...] = jnp.zeros_like(dg_ref)

        dob = (FFN_RES * do_ref[...]).astype(BF16)
        for lo in range(0, F, cf):
            dact = _dot(dob, wv[2, lo:lo + cf, :], NT)
            dup_ref[:, lo:lo + cf] = (dact * dau_ref[:, lo:lo + cf].astype(F32)).astype(BF16)
            dgate_ref[:, lo:lo + cf] = (dact * dag_ref[:, lo:lo + cf].astype(F32)).astype(BF16)
        dh = _dot(dgate_ref[...], wv[0], NN) + _dot(dup_ref[...], wv[1], NN)
        xhat, rstd = _rms_stats(x_ref[...])
        dx, dg = _rms_bwd(xhat, rstd, g_ref[...], dh)
        dx_ref[...] = do_ref[...] + dx
        dg_ref[...] += dg

    row = pl.BlockSpec((tm, D), lambda i: (i, 0))
    hid = pl.BlockSpec((tm, F), lambda i: (i, 0))
    vec = pl.BlockSpec((1, D), lambda i: (0, 0))
    return _call(
        body, name=name, grid=(T // tm,),
        in_specs=[row, row, vec, hid, hid, HBM_SPEC],
        out_specs=[row, hid, hid, vec],
        out_shape=[jax.ShapeDtypeStruct((T, D), F32), jax.ShapeDtypeStruct((T, F), BF16),
                   jax.ShapeDtypeStruct((T, F), BF16), jax.ShapeDtypeStruct((1, D), F32)],
        scratch_shapes=[pltpu.VMEM((3, F, D), BF16), pltpu.SemaphoreType.DMA((3,))],
        vmem_mib=52, args=(dout, x, g, dau, dag, w3), rider=rider)


def _wgrad(lhs, rhs, tm, tf, name, rider=None):
    T, F = lhs[0].shape
    D = rhs.shape[1]
    K = len(lhs)

    def body(*refs):
        lhs_refs, rhs_ref, dw_ref, accs = refs[:K], refs[K], refs[K + 1], refs[K + 2:]
        i = pl.program_id(1)

        @pl.when(i == 0)
        def _():
            for acc in accs:
                acc[...] = jnp.zeros_like(acc)

        rv = rhs_ref[...]
        for acc, lhs_ref in zip(accs, lhs_refs):
            acc[...] += _dot(lhs_ref[...], rv, TN)

        @pl.when(i == pl.num_programs(1) - 1)
        def _():
            for k, acc in enumerate(accs):
                dw_ref[k] = acc[...].astype(BF16)

    hid = pl.BlockSpec((tm, tf), lambda f, i: (i, f))
    return _call(
        body, name=name, grid=(F // tf, T // tm),
        in_specs=[hid] * K + [pl.BlockSpec((tm, D), lambda f, i: (i, 0))],
        out_specs=[pl.BlockSpec((K, tf, D), lambda f, i: (0, f, 0))],
        out_shape=[jax.ShapeDtypeStruct((K, F, D), BF16)],
        scratch_shapes=[pltpu.VMEM((tf, D), F32)] * K,
        vmem_mib=56, args=(*lhs, rhs), rider=rider)


def _lru_gates(xr, bda_ref, bdx_ref, vec_ref):
    xrb = xr.astype(BF16)
    r = _sigmoid(_dot(xrb, bda_ref[...], NN) + vec_ref[V_BA:V_BA + 1, :])
    ig = _sigmoid(_dot(xrb, bdx_ref[...], NN) + vec_ref[V_BX:V_BX + 1, :])
    sp = _softplus_neg(vec_ref[V_LAM:V_LAM + 1, :])
    log_a = (-LRU_C * sp) * r
    a = jnp.exp(log_a)
    mult = jnp.sqrt(_neg_expm1(2.0 * log_a))
    return xrb, r, ig, sp, a, mult


def _layernorm_stats(u1):
    xc = u1 - jnp.mean(u1, axis=-1, keepdims=True)
    rs = lax.rsqrt(jnp.mean(xc * xc, axis=-1, keepdims=True) + LN_EPS)
    return xc * rs, rs


def _mix_core_fwd(x1, g, w_in_t, w_out, bda, bdx, cw, lw, vec, tm, name, rider=None):
    T, D = x1.shape
    W = cw.shape[1]
    assert tm >= CONV_HALO and w_in_t.shape[0] == 4 * W

    def body(x1_ref, g_ref, wi_ref, wo_ref, bda_ref, bdx_ref, cw_ref, lw_ref, vec_ref,
             x2_ref, z_ref, mix_ref, u1_ref, xr_ref, hst_ref, ubuf, rbuf, hc):
        @pl.when(pl.program_id(0) == 0)
        def _():
            ubuf[0:CONV_HALO, :] = jnp.zeros((CONV_HALO, W), F32)
            rbuf[0:LRU_HALO, :] = jnp.zeros((LRU_HALO, W), F32)
            hc[...] = jnp.zeros_like(hc)

        xhat, _ = _rms_stats(x1_ref[...])
        z_ref[...] = _dot((xhat * g_ref[...]).astype(BF16), wi_ref[...], NT)

        ubuf[CONV_HALO:CONV_HALO + tm, :] = z_ref[:, 0:W] * _sigmoid(z_ref[:, W:2 * W])
        u1 = jnp.zeros((tm, W), F32) + vec_ref[V_CB:V_CB + 1, :]
        base = CONV_HALO - (CONV_K - 1)
        for off, win in _row_windows(ubuf, tm, range(base, base + CONV_K)):
            u1 = u1 + cw_ref[off - base:off - base + 1, :] * win
        ubuf[0:CONV_HALO, :] = ubuf[tm:tm + CONV_HALO, :]
        u1_ref[...] = u1
        xh, _ = _layernorm_stats(u1)
        u2 = xh * vec_ref[V_LNG:V_LNG + 1, :] + vec_ref[V_LNB:V_LNB + 1, :]
        ub = (u2 * _sigmoid(u2)).astype(BF16)
        mix_ref[:, 0:W] = ub

        rbuf[LRU_HALO:LRU_HALO + tm, :] = z_ref[:, 2 * W:3 * W]
        xr = jnp.zeros((tm, W), F32) + vec_ref[V_LCB:V_LCB + 1, :]
        for k in range(LRU_K):
            off = LRU_HALO - (LRU_K - 1) + k
            xr = xr + lw_ref[k:k + 1, :] * rbuf[off:off + tm, :]
        rbuf[0:LRU_HALO, :] = rbuf[tm:tm + LRU_HALO, :]
        xr_ref[...] = xr
        _, _, ig, _, a, mult = _lru_gates(xr, bda_ref, bdx_ref, vec_ref)
        hc[0:1, :] = _scan_rows(a, mult * (ig * xr), hc[0:1, :], hst_ref)
        gl, _ = _gelu_parts(z_ref[:, 3 * W:4 * W])
        yb = (hst_ref[...] * gl).astype(BF16)
        mix_ref[:, W:2 * W] = yb

        x2_ref[...] = x1_ref[...] + _dot(ub, wo_ref[0:W, :], NN) + _dot(yb, wo_ref[W:2 * W, :], NN)

    full = lambda a: pl.BlockSpec(a.shape, lambda i: (0,) * a.ndim)
    tile = lambda n: pl.BlockSpec((tm, n), lambda i: (i, 0))
    return _call(
        body, name=name, grid=(T // tm,),
        in_specs=[tile(D), full(g), full(w_in_t), full(w_out), full(bda), full(bdx), full(cw), full(lw), full(vec)],
        out_specs=[tile(D), tile(4 * W), tile(2 * W), tile(W), tile(W), tile(W)],
        out_shape=[jax.ShapeDtypeStruct((T, D), F32), jax.ShapeDtypeStruct((T, 4 * W), F32),
                   jax.ShapeDtypeStruct((T, 2 * W), BF16), jax.ShapeDtypeStruct((T, W), F32),
                   jax.ShapeDtypeStruct((T, W), F32), jax.ShapeDtypeStruct((T, W), F32)],
        scratch_shapes=[pltpu.VMEM((tm + CONV_HALO, W), F32), pltpu.VMEM((tm + LRU_HALO, W), F32),
                        pltpu.VMEM((8, W), F32)],
        vmem_mib=56, args=(x1, g, w_in_t, w_out, bda, bdx, cw, lw, vec), rider=rider)


def _mix_bwd(dx2, z, u1, xr, hst, x1, mix, g, w_in_t, w_out, bda, bdx, cw, lw, vec, tm, name, rider=None):
    T, D = dx2.shape
    W = cw.shape[1]
    nt = T // tm
    assert tm >= CONV_HALO and tm % CONV_HALO == 0

    def body(dx_ref, z_ref, zh_ref, u1_ref, xr_ref, h_ref, hh_ref, wo_ref, bda_ref, bdx_ref, cw_ref, lw_ref, vec_ref,
             x1_ref, mix_ref, g_ref, wi_ref,
             dx1_ref, sg_ref, dbda_ref, dbdx_ref, dob_ref, dg_ref, dwi_ref, dwo_ref,
             u0buf, du1buf, rxbuf, dxrbuf, gbuf, gc, spacc, dz_ref, ai_ref, ao_ref):
        i = pl.program_id(0)
        first = i == nt - 1
        row = lax.broadcasted_iota(jnp.int32, (tm, W), 0)

        @pl.when(i == 0)
        def _():
            sg_ref[...] = jnp.zeros_like(sg_ref)
            dbda_ref[...] = jnp.zeros_like(dbda_ref)
            dbdx_ref[...] = jnp.zeros_like(dbdx_ref)
            du1buf[tm:tm + CONV_HALO, :] = jnp.zeros((CONV_HALO, W), F32)
            dxrbuf[tm:tm + LRU_HALO, :] = jnp.zeros((LRU_HALO, W), F32)
            gc[...] = jnp.zeros_like(gc)
            spacc[...] = jnp.zeros_like(spacc)
            dg_ref[...] = jnp.zeros_like(dg_ref)
            ai_ref[...] = jnp.zeros_like(ai_ref)
            ao_ref[...] = jnp.zeros_like(ao_ref)

        def accum(r, val):
            sg_ref[r:r + 1, :] += jnp.sum(val, axis=0, keepdims=True)

        x1hat, x1rstd = _rms_stats(x1_ref[...])
        gain = g_ref[...]
        hb = (x1hat * gain).astype(BF16)

        def in_proj_bwd(lo, hi):
            dzb = dz_ref[:, lo:hi]
            ai_ref[lo:hi, :] += _dot(dzb, hb, TN)
            return _dot(dzb, wi_ref[lo:hi, :], NN)

        dxb = dx_ref[...].astype(BF16)
        ao_ref[...] += _dot(mix_ref[...], dxb, TN)
        dmix = _dot(dxb, wo_ref[...], NT)
        d_u = dmix[:, 0:W]
        d_yr = dmix[:, W:2 * W]

        xh, rs = _layernorm_stats(u1_ref[...])
        ln_g = vec_ref[V_LNG:V_LNG + 1, :]
        u2 = xh * ln_g + vec_ref[V_LNB:V_LNB + 1, :]
        s2 = _sigmoid(u2)
        d_u2 = d_u * (s2 * (1.0 + u2 * (1.0 - s2)))
        accum(G_LNG, d_u2 * xh)
        accum(G_LNB, d_u2)
        d_xh = d_u2 * ln_g
        d_u1 = rs * (d_xh - jnp.mean(d_xh, axis=-1, keepdims=True)
                     - xh * jnp.mean(d_xh * xh, axis=-1, keepdims=True))
        accum(G_CB, d_u1)
        halo_on = jnp.where(first, 0.0, 1.0)
        u0buf[0:CONV_HALO, :] = halo_on * (zh_ref[:, 0:W] * _sigmoid(zh_ref[:, W:2 * W]))
        cv = z_ref[:, 0:W]
        sgc = _sigmoid(z_ref[:, W:2 * W])
        u0buf[CONV_HALO:CONV_HALO + tm, :] = cv * sgc
        du1buf[0:tm, :] = d_u1
        base = CONV_HALO - (CONV_K - 1)
        for off, win in _row_windows(u0buf, tm, range(base, base + CONV_K)):
            accum(G_CW + off - base, d_u1 * win)
        d_u0 = jnp.zeros((tm, W), F32)
        for off, win in _row_windows(du1buf, tm, range(0, CONV_K)):
            d_u0 = d_u0 + cw_ref[CONV_K - 1 - off:CONV_K - off, :] * win
        du1buf[tm:tm + CONV_HALO, :] = du1buf[0:CONV_HALO, :]
        dz_ref[:, 0:W] = (d_u0 * sgc).astype(BF16)
        dz_ref[:, W:2 * W] = (d_u0 * cv * (sgc * (1.0 - sgc))).astype(BF16)
        dh = in_proj_bwd(0, 2 * W)

        xrv = xr_ref[...]
        xrb, r, ig, sp, a, mult = _lru_gates(xrv, bda_ref, bdx_ref, vec_ref)
        h = h_ref[...]
        gl, dgl = _gelu_parts(z_ref[:, 3 * W:4 * W])
        dz_ref[:, 3 * W:4 * W] = (d_yr * h * dgl).astype(BF16)
        dh = dh + in_proj_bwd(3 * W, 4 * W)
        a_next = jnp.where(row == tm - 1, 1.0, pltpu.roll(a, tm - 1, 0))
        g_first = _scan_rows(a_next, d_yr * gl, gc[0:1, :], gbuf, reverse=True)
        g = gbuf[...]
        gc[0:1, :] = a[0:1, :] * g_first
        hprev = jnp.where(row == 0, halo_on * hh_ref[LRU_HALO - 1:LRU_HALO, :], pltpu.roll(h, 1, 0))
        d_log_a = (g * hprev) * a - (g * ig * xrv) * (a * a) / mult
        d_ig = g * mult * xrv
        d_xr = g * mult * ig
        spacc[0:1, :] += jnp.sum(d_log_a * r, axis=0, keepdims=True)
        d_pa32 = (d_log_a * (-LRU_C * sp)) * (r * (1.0 - r))
        d_px32 = d_ig * (ig * (1.0 - ig))
        accum(G_BA, d_pa32)
        accum(G_BX, d_px32)
        d_pa = d_pa32.astype(BF16)
        d_px = d_px32.astype(BF16)
        d_xr = d_xr + _dot(d_pa, bda_ref[...], NT) + _dot(d_px, bdx_ref[...], NT)
        dbda_ref[...] += _dot(xrb, d_pa, TN)
        dbdx_ref[...] += _dot(xrb, d_px, TN)
        accum(G_LCB, d_xr)
        rxbuf[0:LRU_HALO, :] = halo_on * zh_ref[CONV_HALO - LRU_HALO:CONV_HALO, 2 * W:3 * W]
        rxbuf[LRU_HALO:LRU_HALO + tm, :] = z_ref[:, 2 * W:3 * W]
        dxrbuf[0:tm, :] = d_xr
        d_rx = jnp.zeros((tm, W), F32)
        for k in range(LRU_K):
            off = LRU_HALO - (LRU_K - 1) + k
            accum(G_LW + k, d_xr * rxbuf[off:off + tm, :])
            d_rx = d_rx + lw_ref[k:k + 1, :] * dxrbuf[LRU_K - 1 - k:LRU_K - 1 - k + tm, :]
        dxrbuf[tm:tm + LRU_HALO, :] = dxrbuf[0:LRU_HALO, :]
        dz_ref[:, 2 * W:3 * W] = d_rx.astype(BF16)
        dh = dh + in_proj_bwd(2 * W, 3 * W)

        dx, dg = _rms_bwd(x1hat, x1rstd, gain, dh)
        dx1 = dx_ref[...] + dx
        dx1_ref[...] = dx1
        dob_ref[...] = (FFN_RES * dx1).astype(BF16)
        dg_ref[...] += dg

        @pl.when(first)
        def _():
            lam = vec_ref[V_LAM:V_LAM + 1, :]
            sg_ref[G_LAM:G_LAM + 1, :] = LRU_C * _sigmoid(-lam) * spacc[0:1, :]
            dwi_ref[...] = ai_ref[...].astype(BF16)
            dwo_ref[...] = ao_ref[...].astype(BF16)

    full = lambda a: pl.BlockSpec(a.shape, lambda i: (0,) * a.ndim)
    tile = lambda n: pl.BlockSpec((tm, n), lambda i: (nt - 1 - i, 0))
    halo = lambda rows, n: pl.BlockSpec(
        (rows, n), lambda i: (jnp.maximum((nt - 1 - i) * (tm // rows) - 1, 0), 0))
    const = lambda r, c: pl.BlockSpec((r, c), lambda i: (0, 0))
    return _call(
        body, name=name, grid=(nt,),
        in_specs=[tile(D), tile(4 * W), halo(CONV_HALO, 4 * W), tile(W), tile(W), tile(W), halo(LRU_HALO, W),
                  full(w_out), full(bda), full(bdx), full(cw), full(lw), full(vec),
                  tile(D), tile(2 * W), full(g), full(w_in_t)],
        out_specs=[tile(D), const(G_ROWS, W), const(W, W), const(W, W),
                   tile(D), const(1, D), const(4 * W, D), const(2 * W, D)],
        out_shape=[jax.ShapeDtypeStruct((T, D), F32), jax.ShapeDtypeStruct((G_ROWS, W), F32),
                   jax.ShapeDtypeStruct((W, W), F32), jax.ShapeDtypeStruct((W, W), F32),
                   jax.ShapeDtypeStruct((T, D), BF16), jax.ShapeDtypeStruct((1, D), F32),
                   jax.ShapeDtypeStruct((4 * W, D), BF16), jax.ShapeDtypeStruct((2 * W, D), BF16)],
        scratch_shapes=[pltpu.VMEM((tm + CONV_HALO, W), F32), pltpu.VMEM((tm + CONV_HALO, W), F32),
                        pltpu.VMEM((tm + LRU_HALO, W), F32), pltpu.VMEM((tm + LRU_HALO, W), F32),
                        pltpu.VMEM((tm, W), F32), pltpu.VMEM((8, W), F32), pltpu.VMEM((8, W), F32),
                        pltpu.VMEM((tm, 4 * W), BF16), pltpu.VMEM((4 * W, D), F32), pltpu.VMEM((2 * W, D), F32)],
        vmem_mib=60, args=(dx2, z, z, u1, xr, hst, hst, w_out, bda, bdx, cw, lw, vec, x1, mix, g, w_in_t),
        rider=rider)


def _pair_add(full, recv, name):
    K, _, _, rows, D = full.shape

    def body(c_ref, a_ref, b_ref, o_ref):
        o_ref[...] = (a_ref[...].astype(F32) + b_ref[...].astype(F32)).astype(BF16)

    c = lax.axis_index("c").astype(jnp.int32).reshape((1,))
    return _call(
        body, name=name, grid=(K, N_CHIP), num_scalar_prefetch=1,
        in_specs=[pl.BlockSpec((None, None, None, rows, D), lambda k, q, c_ref: (k, q, c_ref[0], 0, 0)),
                  pl.BlockSpec((None, None, rows, D), lambda k, q, c_ref: (k, q, 0, 0))],
        out_specs=pl.BlockSpec((None, None, rows, D), lambda k, q, c_ref: (k, q, 0, 0)),
        out_shape=jax.ShapeDtypeStruct(recv.shape, BF16),
        scratch_shapes=[], vmem_mib=16, args=(c, full, recv))


def _adamw_update(wv, gv, mv, vv):
    m2 = ADAM_B1 * mv + (1.0 - ADAM_B1) * gv
    v2 = ADAM_B2 * vv + (1.0 - ADAM_B2) * (gv * gv)
    m_hat = m2 / (1.0 - ADAM_B1 ** ADAM_STEP)
    v_hat = v2 / (1.0 - ADAM_B2 ** ADAM_STEP)
    return -ADAM_LR * (m_hat / (jnp.sqrt(v_hat) + ADAM_EPS) + ADAM_WD * wv), m2, v2


def _finish(parts, k, w, m, v, transpose, name):
    _, n_parts, rows, D = parts.shape

    def body(p_ref, w_ref, m_ref, v_ref, g_ref, d_ref, mo_ref, vo_ref):
        acc = p_ref[0].astype(F32)
        for q in range(1, n_parts):
            acc = acc + p_ref[q].astype(F32)
        gv = acc.T if transpose else acc
        g_ref[...] = gv
        d_ref[...], mo_ref[...], vo_ref[...] = _adamw_update(w_ref[...], gv, m_ref[...], v_ref[...])

    whole = pl.BlockSpec(w.shape, lambda i: (0, 0))
    return _call(
        body, name=name, grid=(1,),
        in_specs=[pl.BlockSpec((None, n_parts, rows, D), lambda i: (k, 0, 0, 0)), whole, whole, whole],
        out_specs=[whole] * 4, out_shape=[pltpu.HBM(w.shape, F32)] * 4,
        scratch_shapes=[], vmem_mib=40, args=(parts, w, m, v))


def _finish_many(items, name, rider=None):
    n_items = len(items)
    D = items[0][2].shape[1]
    max_rows = max(w.shape[0] for _, _, w, _, _ in items)
    max_parts = max(parts.shape[1] for parts, _, _, _, _ in items)
    n_loads, n_stores = 4, 4

    def body(*refs):
        ins, outs = refs[:4 * n_items], refs[4 * n_items:8 * n_items]
        pbuf, wbuf, obuf, sem = refs[8 * n_items:]
        step = pl.program_id(0)
        for j, (parts, k, w, _, _) in enumerate(items):
            rows, n_parts = w.shape[0], parts.shape[1]

            @pl.when(step == j)
            def _():
                p_ref, w_ref, m_ref, v_ref = ins[4 * j:4 * j + 4]
                loads = [pltpu.make_async_copy(p_ref.at[k], pbuf.at[0:n_parts, 0:rows], sem.at[0])]
                loads += [pltpu.make_async_copy(src, wbuf.at[r, 0:rows], sem.at[1 + r])
                          for r, src in enumerate((w_ref, m_ref, v_ref))]
                for cp in loads:
                    cp.start()
                for cp in loads:
                    cp.wait()
                acc = pbuf[0, 0:rows, :].astype(F32)
                for q in range(1, n_parts):
                    acc = acc + pbuf[q, 0:rows, :].astype(F32)
                obuf[0, 0:rows, :] = acc
                obuf[1, 0:rows, :], obuf[2, 0:rows, :], obuf[3, 0:rows, :] = _adamw_update(
                    wbuf[0, 0:rows, :], acc, wbuf[1, 0:rows, :], wbuf[2, 0:rows, :])
                stores = [pltpu.make_async_copy(obuf.at[r, 0:rows], outs[4 * j + r], sem.at[n_loads + r])
                          for r in range(n_stores)]
                for cp in stores:
                    cp.start()
                for cp in stores:
                    cp.wait()

    args = [a for parts, _, w, m, v in items for a in (parts, w, m, v)]
    shapes = [jax.ShapeDtypeStruct(w.shape, F32) for _, _, w, _, _ in items for _ in range(4)]
    return _call(
        body, name=name, grid=(n_items,),
        in_specs=[HBM_SPEC] * len(args), out_specs=[HBM_SPEC] * len(shapes), out_shape=shapes,
        scratch_shapes=[pltpu.VMEM((max_parts, max_rows, D), BF16), pltpu.VMEM((3, max_rows, D), F32),
                        pltpu.VMEM((4, max_rows, D), F32), pltpu.SemaphoreType.DMA((n_loads + n_stores,))],
        vmem_mib=40, args=args, rider=rider)


def _adamw_each(ws, gs, ms, vs, name):
    n = len(ws)

    def body(*refs):
        w_refs, g_refs, m_refs, v_refs, outs = refs[:n], refs[n:2 * n], refs[2 * n:3 * n], refs[3 * n:4 * n], refs[4 * n:]
        for k in range(n):
            outs[k][...], outs[n + k][...], outs[2 * n + k][...] = _adamw_update(
                w_refs[k][...], g_refs[k][...], m_refs[k][...], v_refs[k][...])

    shapes = [jax.ShapeDtypeStruct(w.shape, F32) for w in ws]
    return pl.pallas_call(
        body, name=name,
        in_specs=[VMEM_SPEC] * (4 * n), out_specs=[VMEM_SPEC] * (3 * n), out_shape=shapes * 3,
        compiler_params=pltpu.CompilerParams(vmem_limit_bytes=32 * MIB),
    )(*ws, *gs, *ms, *vs)


def _block_diag(w):
    h, d, _ = w.shape
    onto = jnp.eye(h, dtype=w.dtype)
    return (w[:, :, None, :] * onto[:, None, :, None]).reshape(h * d, h * d)


def _diag_blocks(m, h):
    d = m.shape[0] // h
    onto = jnp.eye(h, dtype=m.dtype)
    return (m.reshape(h, d, h, d) * onto[:, None, :, None]).sum(axis=2)


def _reduce_level1(full, tag):
    got = _run_comm(_sibling_comm(full), "rs_sibling_" + tag)
    return [_pair_add(a, b, "rs_pair_add_%s%d" % (tag, n)) for n, (a, b) in enumerate(zip(full, got))]


def kernel(x, ffn1_norm, ffn1_w_gate, ffn1_w_up, ffn1_w_down, mix_norm, w_in, conv_dw, conv_dw_bias, conv_ln_g, conv_ln_b, lru_conv_w, lru_conv_b, lru_w_a, lru_b_a, lru_w_x, lru_b_x, lru_lambda, w_out, ffn2_norm, ffn2_w_gate, ffn2_w_up, ffn2_w_down, final_norm, loss_target, m_ffn1_norm, m_ffn1_w_gate, m_ffn1_w_up, m_ffn1_w_down, m_mix_norm, m_w_in, m_conv_dw, m_conv_dw_bias, m_conv_ln_g, m_conv_ln_b, m_lru_conv_w, m_lru_conv_b, m_lru_w_a, m_lru_b_a, m_lru_w_x, m_lru_b_x, m_lru_lambda, m_w_out, m_ffn2_norm, m_ffn2_w_gate, m_ffn2_w_up, m_ffn2_w_down, m_final_norm, v_ffn1_norm, v_ffn1_w_gate, v_ffn1_w_up, v_ffn1_w_down, v_mix_norm, v_w_in, v_conv_dw, v_conv_dw_bias, v_conv_ln_g, v_conv_ln_b, v_lru_conv_w, v_lru_conv_b, v_lru_w_a, v_lru_b_a, v_lru_w_x, v_lru_b_x, v_lru_lambda, v_w_out, v_ffn2_norm, v_ffn2_w_gate, v_ffn2_w_up, v_ffn2_w_down, v_final_norm):
    T, D = x.shape[1], x.shape[2]
    F = ffn1_w_down.shape[0] * N_DEV
    rf = ffn1_w_down.shape[0]
    ri = w_in.shape[1]
    ro = w_out.shape[0]
    W = conv_dw_bias.shape[0]
    wc = conv_dw.shape[1]
    H = lru_w_a.shape[0]
    xs = x.reshape(T, D)
    tgt = loss_target.reshape(T, D)
    tm_ffn = min(256, T)
    tm_fwd = min(512, T)
    cf = 256
    tm_w = min(1024, T)
    tm_w1 = min(2048, T)
    tm_mix = min(256, T)
    tf_w = F // 2
    row = lambda v: v.reshape(1, -1)
    by_owner = lambda a, rows: a.reshape(a.shape[0], N_CHIP, 2, rows, D)

    p3a, p3b, p_in, p_out = _prep_weights(
        (ffn1_w_gate.T, ffn1_w_up.T, ffn1_w_down), (ffn2_w_gate.T, ffn2_w_up.T, ffn2_w_down), w_in, w_out,
        "prep_weights")
    tile_rows = lambda a: jnp.pad(a, ((0, -a.shape[0] % SUBLANES), (0, 0)))
    p_cw = jnp.concatenate([tile_rows(conv_dw), tile_rows(lru_conv_w)], axis=0)
    lw_row = p_cw.shape[0] - SUBLANES
    stacked = lambda r, j: r.at[:, j]
    plain = lambda r, j: r.at[j]
    g3_shape = jax.ShapeDtypeStruct((3, N_DEV, rf, D), BF16)
    (g3a,) = _all_gather([p3a], [stacked], [g3_shape], "ag_ffn1")
    w3a = g3a.reshape(3, F, D)
    bda = _block_diag(lru_w_a).astype(BF16)
    bdx = _block_diag(lru_w_x).astype(BF16)
    vec = jnp.concatenate([tile_rows(v[None]) for v in
                           (conv_dw_bias, conv_ln_g, conv_ln_b, lru_conv_b, lru_b_a, lru_b_x, lru_lambda)], axis=0)

    gather_rest = _gather_comm(
        [p3b, p_in, p_out, p_cw], [stacked, plain, plain, plain],
        [g3_shape, jax.ShapeDtypeStruct((N_DEV, ri, D), BF16), jax.ShapeDtypeStruct((N_DEV, ro, D), BF16),
         jax.ShapeDtypeStruct((N_DEV,) + p_cw.shape, F32)],
        [(SIBLING,) + SAME_CORE, EVERYONE, EVERYONE, EVERYONE])
    (x1, h1, dau1, dag1, act1), (g3b_half, g_in, g_out, g_cw) = _ffn_fwd(
        xs, row(ffn1_norm), w3a, tm_fwd, cf, "ffn1_fwd", rider=gather_rest)
    w_in_t = g_in.reshape(N_DEV * ri, D)
    w_out_f = g_out.reshape(N_DEV * ro, D)
    cw_all = jnp.transpose(g_cw, (1, 0, 2)).reshape(p_cw.shape[0], N_DEV * wc)
    cw = cw_all[0:CONV_K]
    lw = cw_all[lw_row:lw_row + LRU_K]
    (x2, z, mix, u1, xr, hst), (g3b,) = _mix_core_fwd(
        x1, row(mix_norm), w_in_t, w_out_f, bda, bdx, cw, lw, vec, min(512, T), "mix_core_fwd",
        rider=_forward_comm([g3b_half], [stacked]))
    w3b = g3b.reshape(3, F, D)
    dx3, dob2, d_final_norm, loss_part, h3, dau2, dag2, act2 = _ffn_fwd(
        x2, row(ffn2_norm), w3b, tm_fwd, cf, "ffn2_fwd_loss", head=(row(final_norm), tgt))

    dx2, dgate2, dup2, d_ffn2_norm = _ffn_dgrad(dx3, x2, row(ffn2_norm), dau2, dag2, w3b, tm_ffn, cf, "ffn2_dgrad")
    (dw_gu2,) = _wgrad([dgate2, dup2], h3, tm_w, tf_w, "ffn2_wgrad_gu")
    (dw_d2,) = _wgrad([act2], dob2, tm_w1, tf_w, "ffn2_wgrad_d")
    by_device = lambda a: a.reshape(a.shape[0], N_DEV, rf, D)
    (dx1, sg, dbda, dbdx, dob1, d_mix_norm, dw_in_t, dw_out), parts_f2 = _mix_bwd(
        dx2, z, u1, xr, hst, x1, mix, row(mix_norm), w_in_t, w_out_f, bda, bdx, cw, lw, vec, tm_mix, "mix_bwd",
        rider=_chips_comm([by_device(dw_gu2), by_device(dw_d2)], every_device=True))
    io = [dw_in_t.reshape(1, N_DEV, ri, D), dw_out.reshape(1, N_DEV, ro, D)]
    (dw_d1,), parts_io = _wgrad(
        [act1], dob1, tm_w1, tf_w, "ffn1_wgrad_d", rider=_chips_comm(io, every_device=True))
    sums_d1 = _reduce_level1([by_owner(dw_d1, rf)], "d1")
    dx0, dgate1, dup1, d_ffn1_norm = _ffn_dgrad(dx1, xs, row(ffn1_norm), dau1, dag1, w3a, tm_ffn, cf, "ffn1_dgrad")
    (dw_g1,), parts_d1 = _wgrad([dgate1], h1, tm_w1, tf_w, "ffn1_wgrad_g", rider=_chips_comm(sums_d1))
    sums_g1 = _reduce_level1([by_owner(dw_g1, rf)], "g1")
    small = [d_ffn1_norm, d_mix_norm, d_ffn2_norm, d_final_norm, sg, _diag_blocks(dbda, H).reshape(-1, D),
             _diag_blocks(dbdx, H).reshape(-1, D), loss_part]
    (dw_u1,), summed_and_parts = _wgrad(
        [dup1], h1, tm_w1, tf_w, "ffn1_wgrad_u", rider=_both(_small_sum_comm(small), _chips_comm(sums_g1)))
    summed, parts_g1 = summed_and_parts[:len(small)], summed_and_parts[len(small):]
    sums_u1 = _reduce_level1([by_owner(dw_u1, rf)], "u1")

    g_norm1, g_norm_mix, g_norm2, g_norm_final, g_sg, g_w_a, g_w_x, g_loss = summed
    loss = g_loss[0, 0]
    me = 4 * lax.axis_index("x") + 2 * lax.axis_index("y") + lax.axis_index("c")
    chan = lambda full_g: lax.dynamic_slice_in_dim(full_g, me * wc, wc, axis=1)
    grads = {
        "ffn1_norm": g_norm1.reshape(D), "mix_norm": g_norm_mix.reshape(D), "ffn2_norm": g_norm2.reshape(D),
        "final_norm": g_norm_final.reshape(D),
        "conv_dw_bias": g_sg[G_CB], "conv_ln_g": g_sg[G_LNG], "conv_ln_b": g_sg[G_LNB],
        "lru_conv_b": g_sg[G_LCB], "lru_b_a": g_sg[G_BA], "lru_b_x": g_sg[G_BX], "lru_lambda": g_sg[G_LAM],
        "lru_w_a": g_w_a.reshape(lru_w_a.shape), "lru_w_x": g_w_x.reshape(lru_w_x.shape),
        "conv_dw": chan(g_sg[G_CW:G_CW + CONV_K]), "lru_conv_w": chan(g_sg[G_LW:G_LW + LRU_K]),
    }

    weights = dict(ffn1_norm=ffn1_norm, ffn1_w_gate=ffn1_w_gate, ffn1_w_up=ffn1_w_up, ffn1_w_down=ffn1_w_down, mix_norm=mix_norm, w_in=w_in, conv_dw=conv_dw, conv_dw_bias=conv_dw_bias, conv_ln_g=conv_ln_g, conv_ln_b=conv_ln_b, lru_conv_w=lru_conv_w, lru_conv_b=lru_conv_b, lru_w_a=lru_w_a, lru_b_a=lru_b_a, lru_w_x=lru_w_x, lru_b_x=lru_b_x, lru_lambda=lru_lambda, w_out=w_out, ffn2_norm=ffn2_norm, ffn2_w_gate=ffn2_w_gate, ffn2_w_up=ffn2_w_up, ffn2_w_down=ffn2_w_down, final_norm=final_norm)
    moment1 = dict(ffn1_norm=m_ffn1_norm, ffn1_w_gate=m_ffn1_w_gate, ffn1_w_up=m_ffn1_w_up, ffn1_w_down=m_ffn1_w_down, mix_norm=m_mix_norm, w_in=m_w_in, conv_dw=m_conv_dw, conv_dw_bias=m_conv_dw_bias, conv_ln_g=m_conv_ln_g, conv_ln_b=m_conv_ln_b, lru_conv_w=m_lru_conv_w, lru_conv_b=m_lru_conv_b, lru_w_a=m_lru_w_a, lru_b_a=m_lru_b_a, lru_w_x=m_lru_w_x, lru_b_x=m_lru_b_x, lru_lambda=m_lru_lambda, w_out=m_w_out, ffn2_norm=m_ffn2_norm, ffn2_w_gate=m_ffn2_w_gate, ffn2_w_up=m_ffn2_w_up, ffn2_w_down=m_ffn2_w_down, final_norm=m_final_norm)
    moment2 = dict(ffn1_norm=v_ffn1_norm, ffn1_w_gate=v_ffn1_w_gate, ffn1_w_up=v_ffn1_w_up, ffn1_w_down=v_ffn1_w_down, mix_norm=v_mix_norm, w_in=v_w_in, conv_dw=v_conv_dw, conv_dw_bias=v_conv_dw_bias, conv_ln_g=v_conv_ln_g, conv_ln_b=v_conv_ln_b, lru_conv_w=v_lru_conv_w, lru_conv_b=v_lru_conv_b, lru_w_a=v_lru_w_a, lru_b_a=v_lru_b_a, lru_w_x=v_lru_w_x, lru_b_x=v_lru_b_x, lru_lambda=v_lru_lambda, w_out=v_w_out, ffn2_norm=v_ffn2_norm, ffn2_w_gate=v_ffn2_w_gate, ffn2_w_up=v_ffn2_w_up, ffn2_w_down=v_ffn2_w_down, final_norm=v_final_norm)
    order = list(weights)
    gate_up = {"ffn1_w_gate", "ffn1_w_up", "ffn2_w_gate", "ffn2_w_up"}
    view = lambda n, a: a.T if n in gate_up else a
    operands = lambda n: [view(n, d[n]) for d in (weights, moment1, moment2)]
    delta, new_m, new_v = {}, {}, {}
    six = {"ffn1_w_gate": (parts_g1[0], 0), "ffn1_w_down": (parts_d1[0], 0), "w_out": (parts_io[1], 0),
           "ffn2_w_gate": (parts_f2[0], 0), "ffn2_w_up": (parts_f2[0], 1), "ffn2_w_down": (parts_f2[1], 0)}
    results, parts_u1 = _finish_many([(parts, k, *operands(n)) for n, (parts, k) in six.items()], "finish_six",
                                     rider=_chips_comm(sums_u1))
    for j, n in enumerate(six):
        grads[n], delta[n], new_m[n], new_v[n] = [view(n, r) for r in results[4 * j:4 * j + 4]]
    for n, parts, d_major in (("ffn1_w_up", parts_u1[0], False), ("w_in", parts_io[0], True)):
        grads[n], delta[n], new_m[n], new_v[n] = [
            view(n, r) for r in _finish(parts, 0, *operands(n), d_major, "finish_" + n)]
    big = set(six) | {"ffn1_w_up", "w_in"}
    rest = [n for n in order if n not in big]
    updates = _adamw_each([weights[n] for n in rest], [grads[n] for n in rest], [moment1[n] for n in rest],
                          [moment2[n] for n in rest], "adamw_small")
    for k, n in enumerate(rest):
        delta[n], new_m[n], new_v[n] = updates[k], updates[len(rest) + k], updates[2 * len(rest) + k]

    return (loss, dx0.reshape(x.shape), *[grads[n] for n in order], *[delta[n] for n in order],
            *[new_m[n] for n in order], *[new_v[n] for n in order])
```

```python
import functools
import math

import jax
import jax.numpy as jnp
from jax import lax
from jax.experimental import pallas as pl
from jax.experimental.pallas import tpu as pltpu

F32 = jnp.float32
BF16 = jnp.bfloat16
MESH = pl.DeviceIdType.MESH

N_DEV = 8
N_CHIP = 4
SUBLANES = 8
RMS_EPS = 1e-6
LN_EPS = 1e-5
LRU_C = 8.0
CONV_K = 31
LRU_K = 4
CONV_HALO = 32
LRU_HALO = 8
FFN_RES = 0.5
ADAM_LR, ADAM_B1, ADAM_B2, ADAM_EPS, ADAM_WD, ADAM_STEP = 0.001, 0.9, 0.999, 1e-08, 0.01, 10
GELU_K = math.sqrt(2.0 / math.pi)
GELU_C = 0.044715

MIB = 1024 * 1024
NT = (((1,), (1,)), ((), ()))
NN = (((1,), (0,)), ((), ()))
TN = (((0,), (0,)), ((), ()))

V_CB, V_LNG, V_LNB, V_LCB, V_BA, V_BX, V_LAM = range(0, 7 * SUBLANES, SUBLANES)
G_CW = 0
G_CB, G_LNG, G_LNB = 31, 32, 33
G_LW = 34
G_LCB, G_BA, G_BX, G_LAM = 38, 39, 40, 41
G_ROWS = 48

HBM_SPEC = pl.BlockSpec(memory_space=pltpu.HBM)
VMEM_SPEC = pl.BlockSpec(memory_space=pltpu.VMEM)


def _dot(a, b, dims):
    return lax.dot_general(a, b, dims, preferred_element_type=F32)


def _sigmoid(x):
    return 1.0 / (1.0 + jnp.exp(-x))


def _gelu_parts(x):
    x2 = x * x
    th = jnp.tanh(GELU_K * x * (1.0 + GELU_C * x2))
    gl = 0.5 * x * (1.0 + th)
    dgl = 0.5 * (1.0 + th) + 0.5 * x * (1.0 - th * th) * GELU_K * (1.0 + 3.0 * GELU_C * x2)
    return gl, dgl


def _neg_expm1(y):
    series = -y * (1.0 + y * (1.0 / 2) * (1.0 + y * (1.0 / 3) * (1.0 + y * (1.0 / 4) * (1.0 + y * (1.0 / 5) * (1.0 + y * (1.0 / 6))))))
    return jnp.where(y > -0.25, series, 1.0 - jnp.exp(y))


def _softplus_neg(lam):
    t = -lam
    e = jnp.exp(-jnp.abs(t))
    s = 1.0 + e
    log1p_e = jnp.log(s) - ((s - 1.0) - e) / s
    return jnp.maximum(t, 0.0) + log1p_e


def _rms_stats(xv):
    rstd = lax.rsqrt(jnp.mean(xv * xv, axis=-1, keepdims=True) + RMS_EPS)
    return xv * rstd, rstd


def _rms_bwd(xhat, rstd, g, dh):
    dxhat = dh * g
    dx = rstd * (dxhat - xhat * jnp.mean(dxhat * xhat, axis=-1, keepdims=True))
    return dx, jnp.sum(dh * xhat, axis=0, keepdims=True)


def _row_windows(buf_ref, n_rows, offsets):
    total = buf_ref.shape[0]
    full = buf_ref[...]
    for b in range(SUBLANES):
        offs = [o for o in offsets if o % SUBLANES == b]
        if not offs:
            continue
        assert max(offs) + n_rows <= total
        moved = full if b == 0 else pltpu.roll(full, total - b, 0)
        for o in offs:
            yield o, moved[o - b:o - b + n_rows, :]


def _scan_rows(av, bv, edge, out_ref, reverse=False):
    tm, W = av.shape
    sub = lax.broadcasted_iota(jnp.int32, (tm, W), 0) % SUBLANES
    s = 1
    while s < SUBLANES:
        keep = (sub < SUBLANES - s) if reverse else (sub >= s)
        shift = tm - s if reverse else s
        bv = jnp.where(keep, av * pltpu.roll(bv, shift, 0) + bv, bv)
        av = jnp.where(keep, av * pltpu.roll(av, shift, 0), av)
        s *= 2
    starts = range(0, tm, SUBLANES)
    for r0 in (reversed(starts) if reverse else starts):
        group = av[r0:r0 + SUBLANES, :] * edge + bv[r0:r0 + SUBLANES, :]
        out_ref[r0:r0 + SUBLANES, :] = group
        edge = group[0:1, :] if reverse else group[SUBLANES - 1:SUBLANES, :]
    return edge


class _Comm:
    def __init__(self, arrays, in_specs, out_shapes, out_specs, scratch, start, wait, aliases=None):
        self.arrays, self.in_specs = list(arrays), list(in_specs)
        self.out_shapes, self.out_specs = list(out_shapes), list(out_specs)
        self.scratch, self.start, self.wait = list(scratch), start, wait
        self.aliases = dict(aliases or {})


def _in_hbm(a):
    return pltpu.with_memory_space_constraint(a, pltpu.HBM)


def _operands(comm):
    return [a if spec is VMEM_SPEC else _in_hbm(a) for a, spec in zip(comm.arrays, comm.in_specs)]


def _call(body, *, name, grid, in_specs, out_specs, out_shape, scratch_shapes, vmem_mib, args, rider=None,
          num_scalar_prefetch=0):
    params = pltpu.CompilerParams(dimension_semantics=("arbitrary",) * len(grid), vmem_limit_bytes=vmem_mib * MIB)
    args = [a if k < num_scalar_prefetch else _in_hbm(a) for k, a in enumerate(args)]
    if rider is None:
        return pl.pallas_call(
            body, name=name,
            grid_spec=pltpu.PrefetchScalarGridSpec(
                num_scalar_prefetch=num_scalar_prefetch, grid=grid, in_specs=in_specs, out_specs=out_specs,
                scratch_shapes=scratch_shapes),
            out_shape=out_shape, compiler_params=params)(*args)
    assert num_scalar_prefetch == 0
    n_in, n_out, n_scr = len(in_specs), len(out_specs), len(scratch_shapes)
    r_in, r_out = len(rider.arrays), len(rider.out_shapes)
    n_axes = len(grid)

    def carried(*refs):
        pos = [0]

        def take(n):
            pos[0] += n
            return refs[pos[0] - n:pos[0]]

        ins, r_ins, outs, r_outs, scr, r_scr = take(n_in), take(r_in), take(n_out), take(r_out), take(n_scr), take(len(rider.scratch))
        first = pl.program_id(0) == 0
        last = pl.program_id(0) == grid[0] - 1
        for ax in range(1, n_axes):
            first = first & (pl.program_id(ax) == 0)
            last = last & (pl.program_id(ax) == grid[ax] - 1)

        @pl.when(first)
        def _():
            rider.start(r_ins, r_outs, r_scr)

        body(*ins, *outs, *scr)

        @pl.when(last)
        def _():
            rider.wait(r_ins, r_outs, r_scr)

    res = pl.pallas_call(
        carried, name=name,
        grid=grid,
        in_specs=list(in_specs) + rider.in_specs,
        out_specs=list(out_specs) + rider.out_specs,
        out_shape=list(out_shape) + rider.out_shapes,
        scratch_shapes=list(scratch_shapes) + rider.scratch,
        input_output_aliases={n_in + i: n_out + o for i, o in rider.aliases.items()},
        compiler_params=params)(*args, *_operands(rider))
    return res[:n_out], res[n_out:]


def _run_comm(comm, name):
    n_in, n_out = len(comm.arrays), len(comm.out_shapes)

    def body(*refs):
        ins, outs, scr = refs[:n_in], refs[n_in:n_in + n_out], refs[n_in + n_out:]
        comm.start(ins, outs, scr)
        comm.wait(ins, outs, scr)

    return pl.pallas_call(
        body, name=name,
        in_specs=comm.in_specs, out_specs=comm.out_specs, out_shape=comm.out_shapes,
        scratch_shapes=comm.scratch, input_output_aliases=comm.aliases,
        compiler_params=pltpu.CompilerParams(vmem_limit_bytes=24 * MIB))(*_operands(comm))


def _both(a, b):
    ni, no, ns = len(a.arrays), len(a.out_shapes), len(a.scratch)

    def start(ins, outs, scr):
        a.start(ins[:ni], outs[:no], scr[:ns])
        b.start(ins[ni:], outs[no:], scr[ns:])

    def wait(ins, outs, scr):
        a.wait(ins[:ni], outs[:no], scr[:ns])
        b.wait(ins[ni:], outs[no:], scr[ns:])

    aliases = dict(a.aliases)
    aliases.update({ni + i: no + o for i, o in b.aliases.items()})
    return _Comm(a.arrays + b.arrays, a.in_specs + b.in_specs, a.out_shapes + b.out_shapes,
                 a.out_specs + b.out_specs, a.scratch + b.scratch, start, wait, aliases)


def _place():
    return lax.axis_index("x"), lax.axis_index("y"), lax.axis_index("c")


def _peer(k):
    x, y, c = _place()
    px, py, pc = x ^ ((k >> 2) & 1), y ^ ((k >> 1) & 1), c ^ (k & 1)
    return (px, py, pc), 4 * px + 2 * py + pc


SIBLING = 1
SAME_CORE = (2, 4, 6)
EVERYONE = tuple(range(1, N_DEV))


def _gather_comm(shards, views, out_shapes, relations):
    na = len(shards)

    def copies(ins, outs, scr):
        send_sems, recv_sems, _ = scr
        _, me = _peer(0)
        out = []
        for a in range(na):
            for k in relations[a]:
                peer, theirs = _peer(k)
                send = functools.partial(
                    pltpu.make_async_remote_copy,
                    src_ref=ins[a], dst_ref=views[a](outs[a], me),
                    send_sem=send_sems.at[7 * a + k - 1], recv_sem=recv_sems.at[7 * a + k - 1],
                    device_id=peer, device_id_type=MESH)
                recv = functools.partial(
                    pltpu.make_async_remote_copy,
                    src_ref=ins[a], dst_ref=views[a](outs[a], theirs),
                    send_sem=send_sems.at[7 * a + k - 1], recv_sem=recv_sems.at[7 * a + k - 1],
                    device_id=peer, device_id_type=MESH)
                out.append((send, recv))
        return out

    def local(ins, outs, scr):
        _, me = _peer(0)
        return [pltpu.make_async_copy(ins[a], views[a](outs[a], me), scr[2].at[a]) for a in range(na)]

    def start(ins, outs, scr):
        for cp in local(ins, outs, scr):
            cp.start()
        for send, _ in copies(ins, outs, scr):
            send().start()

    def wait(ins, outs, scr):
        for _, recv in copies(ins, outs, scr):
            recv().wait_recv()
        for send, _ in copies(ins, outs, scr):
            send().wait_send()
        for cp in local(ins, outs, scr):
            cp.wait()

    return _Comm(shards, [HBM_SPEC] * na, out_shapes, [HBM_SPEC] * na,
                 [pltpu.SemaphoreType.DMA((7 * na,)), pltpu.SemaphoreType.DMA((7 * na,)),
                  pltpu.SemaphoreType.DMA((na,))], start, wait)


def _forward_comm(gathered, views):
    na = len(gathered)
    shapes = [jax.ShapeDtypeStruct(g.shape, g.dtype) for g in gathered]

    def copies(outs, scr):
        send_sems, recv_sems = scr
        sibling, _ = _peer(SIBLING)
        out = []
        for a in range(na):
            for n, k in enumerate(SAME_CORE):
                _, mine = _peer(k)
                _, theirs = _peer(k ^ SIBLING)
                send = functools.partial(
                    pltpu.make_async_remote_copy,
                    src_ref=views[a](outs[a], mine), dst_ref=views[a](outs[a], mine),
                    send_sem=send_sems.at[3 * a + n], recv_sem=recv_sems.at[3 * a + n],
                    device_id=sibling, device_id_type=MESH)
                recv = functools.partial(
                    pltpu.make_async_remote_copy,
                    src_ref=views[a](outs[a], mine), dst_ref=views[a](outs[a], theirs),
                    send_sem=send_sems.at[3 * a + n], recv_sem=recv_sems.at[3 * a + n],
                    device_id=sibling, device_id_type=MESH)
                out.append((send, recv))
        return out

    def start(ins, outs, scr):
        for send, _ in copies(outs, scr):
            send().start()

    def wait(ins, outs, scr):
        for _, recv in copies(outs, scr):
            recv().wait_recv()
        for send, _ in copies(outs, scr):
            send().wait_send()

    return _Comm(gathered, [HBM_SPEC] * na, shapes, [HBM_SPEC] * na,
                 [pltpu.SemaphoreType.DMA((3 * na,)), pltpu.SemaphoreType.DMA((3 * na,))], start, wait,
                 aliases={a: a for a in range(na)})


Y_NEIGHBOUR, X_NEIGHBOUR, DIAGONAL = SAME_CORE


def _all_gather(shards, views, out_shapes, name):
    na = len(shards)
    near = (SIBLING, Y_NEIGHBOUR, X_NEIGHBOUR)
    level1 = _gather_comm(shards, views, out_shapes, [near] * na)

    def body(*refs):
        ins, outs = refs[:na], refs[na:2 * na]
        send_sems, recv_sems, local_sems, fwd_send, fwd_recv, relay_send, relay_recv = refs[2 * na:]
        sibling, _ = _peer(SIBLING)
        c = lax.axis_index("c")
        level1.start(ins, outs, (send_sems, recv_sems, local_sems))

        def block_copy(a, block, to, send_sem, recv_sem):
            return pltpu.make_async_remote_copy(
                src_ref=views[a](outs[a], block), dst_ref=views[a](outs[a], block),
                send_sem=send_sem, recv_sem=recv_sem, device_id=to, device_id_type=MESH)

        def to_sibling(a, n, k):
            _, mine = _peer(k)
            _, theirs = _peer(k ^ SIBLING)
            fwd = block_copy(a, mine, sibling, fwd_send.at[3 * a + n], fwd_recv.at[3 * a + n])
            fwd.start()
            return fwd, block_copy(a, theirs, sibling, fwd_send.at[3 * a + n], fwd_recv.at[3 * a + n])

        passed, landing = [], []
        for a in range(na):
            for n, k in enumerate((Y_NEIGHBOUR, X_NEIGHBOUR)):
                peer, origin = _peer(k)
                pltpu.make_async_remote_copy(
                    src_ref=ins[a], dst_ref=views[a](outs[a], origin),
                    send_sem=send_sems.at[7 * a + k - 1], recv_sem=recv_sems.at[7 * a + k - 1],
                    device_id=peer, device_id_type=MESH).wait_recv()

                @pl.when(c == (0 if k == X_NEIGHBOUR else 1))
                def _():
                    other, _ = _peer(DIAGONAL ^ k)
                    block_copy(a, origin, other, relay_send.at[a], relay_recv.at[a]).start()

                fwd, lands = to_sibling(a, n, k)
                passed.append(fwd)
                landing.append(lands)
            _, far = _peer(DIAGONAL)
            block_copy(a, far, sibling, relay_send.at[a], relay_recv.at[a]).wait_recv()
            fwd, lands = to_sibling(a, 2, DIAGONAL)
            passed.append(fwd)
            landing.append(lands)
        for a in range(na):
            _, theirs = _peer(SIBLING)
            pltpu.make_async_remote_copy(
                src_ref=ins[a], dst_ref=views[a](outs[a], theirs),
                send_sem=send_sems.at[7 * a + SIBLING - 1], recv_sem=recv_sems.at[7 * a + SIBLING - 1],
                device_id=sibling, device_id_type=MESH).wait_recv()
        for cp in landing:
            cp.wait_recv()
        for cp in passed:
            cp.wait_send()
        _, me = _peer(0)
        for a in range(na):
            block_copy(a, me, sibling, relay_send.at[a], relay_recv.at[a]).wait_send()
            for k in near:
                peer, _ = _peer(k)
                pltpu.make_async_remote_copy(
                    src_ref=ins[a], dst_ref=views[a](outs[a], me),
                    send_sem=send_sems.at[7 * a + k - 1], recv_sem=recv_sems.at[7 * a + k - 1],
                    device_id=peer, device_id_type=MESH).wait_send()
            pltpu.make_async_copy(ins[a], views[a](outs[a], me), local_sems.at[a]).wait()

    return pl.pallas_call(
        body, name=name,
        in_specs=[HBM_SPEC] * na, out_specs=[HBM_SPEC] * na, out_shape=out_shapes,
        scratch_shapes=level1.scratch + [pltpu.SemaphoreType.DMA((3 * na,)), pltpu.SemaphoreType.DMA((3 * na,)),
                                         pltpu.SemaphoreType.DMA((na,)), pltpu.SemaphoreType.DMA((na,))],
    )(*[_in_hbm(s) for s in shards])


def _sibling_comm(grads):
    na = len(grads)
    shapes = [jax.ShapeDtypeStruct(g.shape[:2] + g.shape[3:], g.dtype) for g in grads]

    def copies(ins, outs, scr):
        x, y, c = _place()
        return [pltpu.make_async_remote_copy(
            src_ref=ins[a].at[:, :, 1 - c], dst_ref=outs[a],
            send_sem=scr[0].at[a], recv_sem=scr[1].at[a],
            device_id=(x, y, 1 - c), device_id_type=MESH) for a in range(na)]

    def start(ins, outs, scr):
        for cp in copies(ins, outs, scr):
            cp.start()

    def wait(ins, outs, scr):
        for cp in copies(ins, outs, scr):
            cp.wait()

    return _Comm(grads, [HBM_SPEC] * na, shapes, [HBM_SPEC] * na,
                 [pltpu.SemaphoreType.DMA((na,)), pltpu.SemaphoreType.DMA((na,))], start, wait)


def _chips_comm(sums, every_device=False):
    na = len(sums)
    shapes = [jax.ShapeDtypeStruct(s.shape, s.dtype) for s in sums]
    relations = EVERYONE if every_device else SAME_CORE
    nr = len(relations)

    def block(px, py, pc):
        return 4 * px + 2 * py + pc if every_device else 2 * px + py

    def copies(ins, outs, scr):
        mine = block(*_place())
        out = []
        for a in range(na):
            for n, k in enumerate(relations):
                peer, _ = _peer(k)
                theirs = block(*peer)
                send = functools.partial(
                    pltpu.make_async_remote_copy,
                    src_ref=ins[a].at[:, theirs], dst_ref=outs[a].at[:, mine],
                    send_sem=scr[0].at[nr * a + n], recv_sem=scr[1].at[nr * a + n],
                    device_id=peer, device_id_type=MESH)
                recv = functools.partial(
                    pltpu.make_async_remote_copy,
                    src_ref=ins[a].at[:, mine], dst_ref=outs[a].at[:, theirs],
                    send_sem=scr[0].at[nr * a + n], recv_sem=scr[1].at[nr * a + n],
                    device_id=peer, device_id_type=MESH)
                out.append((send, recv))
        return out

    def local(ins, outs, scr):
        mine = block(*_place())
        return [pltpu.make_async_copy(ins[a].at[:, mine], outs[a].at[:, mine], scr[2].at[a]) for a in range(na)]

    def start(ins, outs, scr):
        for cp in local(ins, outs, scr):
            cp.start()
        for send, _ in copies(ins, outs, scr):
            send().start()

    def wait(ins, outs, scr):
        for _, recv in copies(ins, outs, scr):
            recv().wait_recv()
        for send, _ in copies(ins, outs, scr):
            send().wait_send()
        for cp in local(ins, outs, scr):
            cp.wait()

    return _Comm(sums, [HBM_SPEC] * na, shapes, [HBM_SPEC] * na,
                 [pltpu.SemaphoreType.DMA((nr * na,)), pltpu.SemaphoreType.DMA((nr * na,)),
                  pltpu.SemaphoreType.DMA((na,))], start, wait)


def _small_sum_comm(arrays):
    na = len(arrays)

    def copies(ins, scr):
        bufs, send_sems, recv_sems = scr[:na], scr[na], scr[na + 1]
        _, me = _peer(0)
        out = []
        for a in range(na):
            for k in EVERYONE:
                peer, theirs = _peer(k)
                sems = dict(send_sem=send_sems.at[7 * a + k - 1], recv_sem=recv_sems.at[7 * a + k - 1])
                send = functools.partial(
                    pltpu.make_async_remote_copy,
                    src_ref=ins[a], dst_ref=bufs[a].at[me], device_id=peer, device_id_type=MESH, **sems)
                recv = functools.partial(
                    pltpu.make_async_remote_copy,
                    src_ref=ins[a], dst_ref=bufs[a].at[theirs], device_id=peer, device_id_type=MESH, **sems)
                out.append((send, recv))
        return out

    def start(ins, outs, scr):
        _, me = _peer(0)
        for a in range(na):
            scr[a][me] = ins[a][...]
        for send, _ in copies(ins, scr):
            send().start()

    def wait(ins, outs, scr):
        for _, recv in copies(ins, scr):
            recv().wait_recv()
        for send, _ in copies(ins, scr):
            send().wait_send()
        for a in range(na):
            acc = scr[a][0]
            for j in range(1, N_DEV):
                acc = acc + scr[a][j]
            outs[a][...] = acc

    return _Comm(arrays, [VMEM_SPEC] * na, [jax.ShapeDtypeStruct(s.shape, F32) for s in arrays], [VMEM_SPEC] * na,
                 [pltpu.VMEM((N_DEV,) + s.shape, F32) for s in arrays]
                 + [pltpu.SemaphoreType.DMA((7 * na,)), pltpu.SemaphoreType.DMA((7 * na,))], start, wait)


def _prep_weights(ffn1, ffn2, w_in, w_out, name):
    rf, D = ffn1[2].shape
    ri, ro = w_in.shape[1], w_out.shape[0]

    def body(g1, u1, d1, g2, u2, d2, wi, wo, p1_ref, p2_ref, pi_ref, po_ref):
        for p_ref, shards in ((p1_ref, (g1, u1, d1)), (p2_ref, (g2, u2, d2))):
            for k, shard in enumerate(shards):
                p_ref[k] = shard[...].astype(BF16)
        pi_ref[...] = wi[...].T.astype(BF16)
        po_ref[...] = wo[...].astype(BF16)

    args = (*ffn1, *ffn2, w_in, w_out)
    whole = lambda shape: pl.BlockSpec(shape, lambda i: (0,) * len(shape))
    out_shapes = [(3, rf, D), (3, rf, D), (ri, D), (ro, D)]
    return _call(
        body, name=name, grid=(1,),
        in_specs=[whole(a.shape) for a in args], out_specs=[whole(s) for s in out_shapes],
        out_shape=[jax.ShapeDtypeStruct(s, BF16) for s in out_shapes],
        scratch_shapes=[], vmem_mib=48, args=args)


def _load_weights(w_hbm, w_vmem, sem):
    @pl.when(pl.program_id(0) == 0)
    def _():
        copies = [pltpu.make_async_copy(w_hbm.at[k], w_vmem.at[k], sem.at[k]) for k in range(3)]
        for cp in copies:
            cp.start()
        for cp in copies:
            cp.wait()


def _ffn_fwd(x, g, w3, tm, cf, name, rider=None, head=None):
    T, D = x.shape
    F = w3.shape[1]
    n_head = 0 if head is None else 2

    def body(x_ref, g_ref, w_hbm, *refs):
        head_refs, refs = refs[:n_head], refs[n_head:]
        if head is None:
            (xo_ref, h_ref, dau_ref, dag_ref, act_ref, wv, sem) = refs
        else:
            (dx_ref, dob_ref, dgf_ref, loss_ref, h_ref, dau_ref, dag_ref, act_ref, wv, sem) = refs
        _load_weights(w_hbm, wv, sem)
        xhat, _ = _rms_stats(x_ref[...])
        hb = (xhat * g_ref[...]).astype(BF16)
        h_ref[...] = hb
        for lo in range(0, F, cf):
            gate = _dot(hb, wv[0, lo:lo + cf, :], NT)
            up = _dot(hb, wv[1, lo:lo + cf, :], NT)
            sig = _sigmoid(gate)
            silu = gate * sig
            dau_ref[:, lo:lo + cf] = silu.astype(BF16)
            dag_ref[:, lo:lo + cf] = (up * (sig * (1.0 + gate * (1.0 - sig)))).astype(BF16)
            act_ref[:, lo:lo + cf] = (silu * up).astype(BF16)
        x_out = x_ref[...] + FFN_RES * _dot(act_ref[...], wv[2], NN)
        if head is None:
            xo_ref[...] = x_out
            return

        @pl.when(pl.program_id(0) == 0)
        def _():
            dgf_ref[...] = jnp.zeros_like(dgf_ref)
            loss_ref[...] = jnp.zeros_like(loss_ref)

        gf_ref, tgt_ref = head_refs
        yhat, rstd = _rms_stats(x_out)
        gf = gf_ref[...]
        err = yhat * gf - tgt_ref[...]
        loss_ref[...] += (0.5 / D) * jnp.sum(err * err)
        dx, dgf = _rms_bwd(yhat, rstd, gf, err * (1.0 / D))
        dx_ref[...] = dx
        dob_ref[...] = (FFN_RES * dx).astype(BF16)
        dgf_ref[...] += dgf

    row = pl.BlockSpec((tm, D), lambda i: (i, 0))
    hid = pl.BlockSpec((tm, F), lambda i: (i, 0))
    vec = pl.BlockSpec((1, D), lambda i: (0, 0))
    row_f32, row_bf16 = jax.ShapeDtypeStruct((T, D), F32), jax.ShapeDtypeStruct((T, D), BF16)
    if head is None:
        first_specs, first_shapes = [row], [row_f32]
    else:
        first_specs = [row, row, vec, pl.BlockSpec((1, 128), lambda i: (0, 0))]
        first_shapes = [row_f32, row_bf16, jax.ShapeDtypeStruct((1, D), F32), jax.ShapeDtypeStruct((1, 128), F32)]
    return _call(
        body, name=name, grid=(T // tm,),
        in_specs=[row, vec, HBM_SPEC] + ([] if head is None else [vec, row]),
        out_specs=first_specs + [row, hid, hid, hid],
        out_shape=first_shapes + [row_bf16] + [jax.ShapeDtypeStruct((T, F), BF16)] * 3,
        scratch_shapes=[pltpu.VMEM((3, F, D), BF16), pltpu.SemaphoreType.DMA((3,))],
        vmem_mib=60, args=(x, g, w3) + (() if head is None else tuple(head)), rider=rider)


def _ffn_dgrad(dout, x, g, dau, dag, w3, tm, cf, name, rider=None):
    T, D = x.shape
    F = w3.shape[1]

    def body(do_ref, x_ref, g_ref, dau_ref, dag_ref, w_hbm, dx_ref, dgate_ref, dup_ref, dg_ref, wv, sem):
        _load_weights(w_hbm, wv, sem)

        @pl.when(pl.program_id(0) == 0)
        def _():
            dg_ref[...] = jnp.zeros_like(dg_ref)

        dob = (FFN_RES * do_ref[...]).astype(BF16)
        for lo in range(0, F, cf):
            dact = _dot(dob, wv[2, lo:lo + cf, :], NT)
            dup_ref[:, lo:lo + cf] = (dact * dau_ref[:, lo:lo + cf].astype(F32)).astype(BF16)
            dgate_ref[:, lo:lo + cf] = (dact * dag_ref[:, lo:lo + cf].astype(F32)).astype(BF16)
        dh = _dot(dgate_ref[...], wv[0], NN) + _dot(dup_ref[...], wv[1], NN)
        xhat, rstd = _rms_stats(x_ref[...])
        dx, dg = _rms_bwd(xhat, rstd, g_ref[...], dh)
        dx_ref[...] = do_ref[...] + dx
        dg_ref[...] += dg

    row = pl.BlockSpec((tm, D), lambda i: (i, 0))
    hid = pl.BlockSpec((tm, F), lambda i: (i, 0))
    vec = pl.BlockSpec((1, D), lambda i: (0, 0))
    return _call(
        body, name=name, grid=(T // tm,),
        in_specs=[row, row, vec, hid, hid, HBM_SPEC],
        out_specs=[row, hid, hid, vec],
        out_shape=[jax.ShapeDtypeStruct((T, D), F32), jax.ShapeDtypeStruct((T, F), BF16),
                   jax.ShapeDtypeStruct((T, F), BF16), jax.ShapeDtypeStruct((1, D), F32)],
        scratch_shapes=[pltpu.VMEM((3, F, D), BF16), pltpu.SemaphoreType.DMA((3,))],
        vmem_mib=60, args=(dout, x, g, dau, dag, w3), rider=rider)


def _wgrad(lhs, rhs, tm, tf, name, rider=None):
    T, F = lhs[0].shape
    D = rhs.shape[1]
    K = len(lhs)

    def body(*refs):
        lhs_refs, rhs_ref, dw_ref, accs = refs[:K], refs[K], refs[K + 1], refs[K + 2:]
        i = pl.program_id(1)

        @pl.when(i == 0)
        def _():
            for acc in accs:
                acc[...] = jnp.zeros_like(acc)

        rv = rhs_ref[...]
        for acc, lhs_ref in zip(accs, lhs_refs):
            acc[...] += _dot(lhs_ref[...], rv, TN)

        @pl.when(i == pl.num_programs(1) - 1)
        def _():
            for k, acc in enumerate(accs):
                dw_ref[k] = acc[...].astype(BF16)

    hid = pl.BlockSpec((tm, tf), lambda f, i: (i, f))
    return _call(
        body, name=name, grid=(F // tf, T // tm),
        in_specs=[hid] * K + [pl.BlockSpec((tm, D), lambda f, i: (i, 0))],
        out_specs=[pl.BlockSpec((K, tf, D), lambda f, i: (0, f, 0))],
        out_shape=[jax.ShapeDtypeStruct((K, F, D), BF16)],
        scratch_shapes=[pltpu.VMEM((tf, D), F32)] * K,
        vmem_mib=56, args=(*lhs, rhs), rider=rider)


def _lru_gates(xr, bda_ref, bdx_ref, vec_ref):
    xrb = xr.astype(BF16)
    r = _sigmoid(_dot(xrb, bda_ref[...], NN) + vec_ref[V_BA:V_BA + 1, :])
    ig = _sigmoid(_dot(xrb, bdx_ref[...], NN) + vec_ref[V_BX:V_BX + 1, :])
    sp = _softplus_neg(vec_ref[V_LAM:V_LAM + 1, :])
    log_a = (-LRU_C * sp) * r
    a = jnp.exp(log_a)
    mult = jnp.sqrt(_neg_expm1(2.0 * log_a))
    return xrb, r, ig, sp, a, mult


def _layernorm_stats(u1):
    xc = u1 - jnp.mean(u1, axis=-1, keepdims=True)
    rs = lax.rsqrt(jnp.mean(xc * xc, axis=-1, keepdims=True) + LN_EPS)
    return xc * rs, rs


def _mix_core_fwd(x1, g, w_in_t, w_out, bda, bdx, cw, lw, vec, tm, name, rider=None):
    T, D = x1.shape
    W = cw.shape[1]
    assert tm >= CONV_HALO and w_in_t.shape[0] == 4 * W

    def body(x1_ref, g_ref, wi_ref, wo_ref, bda_ref, bdx_ref, cw_ref, lw_ref, vec_ref,
             x2_ref, z_ref, mix_ref, u1_ref, xr_ref, hst_ref, ubuf, rbuf, hc):
        @pl.when(pl.program_id(0) == 0)
        def _():
            ubuf[0:CONV_HALO, :] = jnp.zeros((CONV_HALO, W), F32)
            rbuf[0:LRU_HALO, :] = jnp.zeros((LRU_HALO, W), F32)
            hc[...] = jnp.zeros_like(hc)

        xhat, _ = _rms_stats(x1_ref[...])
        z_ref[...] = _dot((xhat * g_ref[...]).astype(BF16), wi_ref[...], NT)

        ubuf[CONV_HALO:CONV_HALO + tm, :] = z_ref[:, 0:W] * _sigmoid(z_ref[:, W:2 * W])
        u1 = jnp.zeros((tm, W), F32) + vec_ref[V_CB:V_CB + 1, :]
        base = CONV_HALO - (CONV_K - 1)
        for off, win in _row_windows(ubuf, tm, range(base, base + CONV_K)):
            u1 = u1 + cw_ref[off - base:off - base + 1, :] * win
        ubuf[0:CONV_HALO, :] = ubuf[tm:tm + CONV_HALO, :]
        u1_ref[...] = u1
        xh, _ = _layernorm_stats(u1)
        u2 = xh * vec_ref[V_LNG:V_LNG + 1, :] + vec_ref[V_LNB:V_LNB + 1, :]
        ub = (u2 * _sigmoid(u2)).astype(BF16)
        mix_ref[:, 0:W] = ub

        rbuf[LRU_HALO:LRU_HALO + tm, :] = z_ref[:, 2 * W:3 * W]
        xr = jnp.zeros((tm, W), F32) + vec_ref[V_LCB:V_LCB + 1, :]
        for k in range(LRU_K):
            off = LRU_HALO - (LRU_K - 1) + k
            xr = xr + lw_ref[k:k + 1, :] * rbuf[off:off + tm, :]
        rbuf[0:LRU_HALO, :] = rbuf[tm:tm + LRU_HALO, :]
        xr_ref[...] = xr
        _, _, ig, _, a, mult = _lru_gates(xr, bda_ref, bdx_ref, vec_ref)
        hc[0:1, :] = _scan_rows(a, mult * (ig * xr), hc[0:1, :], hst_ref)
        gl, _ = _gelu_parts(z_ref[:, 3 * W:4 * W])
        yb = (hst_ref[...] * gl).astype(BF16)
        mix_ref[:, W:2 * W] = yb

        x2_ref[...] = x1_ref[...] + _dot(ub, wo_ref[0:W, :], NN) + _dot(yb, wo_ref[W:2 * W, :], NN)

    full = lambda a: pl.BlockSpec(a.shape, lambda i: (0,) * a.ndim)
    tile = lambda n: pl.BlockSpec((tm, n), lambda i: (i, 0))
    return _call(
        body, name=name, grid=(T // tm,),
        in_specs=[tile(D), full(g), full(w_in_t), full(w_out), full(bda), full(bdx), full(cw), full(lw), full(vec)],
        out_specs=[tile(D), tile(4 * W), tile(2 * W), tile(W), tile(W), tile(W)],
        out_shape=[jax.ShapeDtypeStruct((T, D), F32), jax.ShapeDtypeStruct((T, 4 * W), F32),
                   jax.ShapeDtypeStruct((T, 2 * W), BF16), jax.ShapeDtypeStruct((T, W), F32),
                   jax.ShapeDtypeStruct((T, W), F32), jax.ShapeDtypeStruct((T, W), F32)],
        scratch_shapes=[pltpu.VMEM((tm + CONV_HALO, W), F32), pltpu.VMEM((tm + LRU_HALO, W), F32),
                        pltpu.VMEM((8, W), F32)],
        vmem_mib=56, args=(x1, g, w_in_t, w_out, bda, bdx, cw, lw, vec), rider=rider)


def _mix_bwd(dx2, z, u1, xr, hst, x1, mix, g, w_in_t, w_out, bda, bdx, cw, lw, vec, tm, name, rider=None):
    T, D = dx2.shape
    W = cw.shape[1]
    nt = T // tm
    assert tm >= CONV_HALO and tm % CONV_HALO == 0

    def body(dx_ref, z_ref, zh_ref, u1_ref, xr_ref, h_ref, hh_ref, wo_ref, bda_ref, bdx_ref, cw_ref, lw_ref, vec_ref,
             x1_ref, mix_ref, g_ref, wi_ref,
             dx1_ref, sg_ref, dbda_ref, dbdx_ref, dob_ref, dg_ref, dwi_ref, dwo_ref,
             u0buf, du1buf, rxbuf, dxrbuf, gbuf, gc, spacc, dz_ref, ai_ref, ao_ref):
        i = pl.program_id(0)
        first = i == nt - 1
        row = lax.broadcasted_iota(jnp.int32, (tm, W), 0)

        @pl.when(i == 0)
        def _():
            sg_ref[...] = jnp.zeros_like(sg_ref)
            dbda_ref[...] = jnp.zeros_like(dbda_ref)
            dbdx_ref[...] = jnp.zeros_like(dbdx_ref)
            du1buf[tm:tm + CONV_HALO, :] = jnp.zeros((CONV_HALO, W), F32)
            dxrbuf[tm:tm + LRU_HALO, :] = jnp.zeros((LRU_HALO, W), F32)
            gc[...] = jnp.zeros_like(gc)
            spacc[...] = jnp.zeros_like(spacc)
            dg_ref[...] = jnp.zeros_like(dg_ref)
            ai_ref[...] = jnp.zeros_like(ai_ref)
            ao_ref[...] = jnp.zeros_like(ao_ref)

        def accum(r, val):
            sg_ref[r:r + 1, :] += jnp.sum(val, axis=0, keepdims=True)

        x1hat, x1rstd = _rms_stats(x1_ref[...])
        gain = g_ref[...]
        hb = (x1hat * gain).astype(BF16)

        def in_proj_bwd(lo, hi):
            dzb = dz_ref[:, lo:hi]
            ai_ref[lo:hi, :] += _dot(dzb, hb, TN)
            return _dot(dzb, wi_ref[lo:hi, :], NN)

        dxb = dx_ref[...].astype(BF16)
        ao_ref[...] += _dot(mix_ref[...], dxb, TN)
        dmix = _dot(dxb, wo_ref[...], NT)
        d_u = dmix[:, 0:W]
        d_yr = dmix[:, W:2 * W]

        xh, rs = _layernorm_stats(u1_ref[...])
        ln_g = vec_ref[V_LNG:V_LNG + 1, :]
        u2 = xh * ln_g + vec_ref[V_LNB:V_LNB + 1, :]
        s2 = _sigmoid(u2)
        d_u2 = d_u * (s2 * (1.0 + u2 * (1.0 - s2)))
        accum(G_LNG, d_u2 * xh)
        accum(G_LNB, d_u2)
        d_xh = d_u2 * ln_g
        d_u1 = rs * (d_xh - jnp.mean(d_xh, axis=-1, keepdims=True)
                     - xh * jnp.mean(d_xh * xh, axis=-1, keepdims=True))
        accum(G_CB, d_u1)
        halo_on = jnp.where(first, 0.0, 1.0)
        u0buf[0:CONV_HALO, :] = halo_on * (zh_ref[:, 0:W] * _sigmoid(zh_ref[:, W:2 * W]))
        cv = z_ref[:, 0:W]
        sgc = _sigmoid(z_ref[:, W:2 * W])
        u0buf[CONV_HALO:CONV_HALO + tm, :] = cv * sgc
        du1buf[0:tm, :] = d_u1
        base = CONV_HALO - (CONV_K - 1)
        for off, win in _row_windows(u0buf, tm, range(base, base + CONV_K)):
            accum(G_CW + off - base, d_u1 * win)
        d_u0 = jnp.zeros((tm, W), F32)
        for off, win in _row_windows(du1buf, tm, range(0, CONV_K)):
            d_u0 = d_u0 + cw_ref[CONV_K - 1 - off:CONV_K - off, :] * win
        du1buf[tm:tm + CONV_HALO, :] = du1buf[0:CONV_HALO, :]
        dz_ref[:, 0:W] = (d_u0 * sgc).astype(BF16)
        dz_ref[:, W:2 * W] = (d_u0 * cv * (sgc * (1.0 - sgc))).astype(BF16)
        dh = in_proj_bwd(0, 2 * W)

        xrv = xr_ref[...]
        xrb, r, ig, sp, a, mult = _lru_gates(xrv, bda_ref, bdx_ref, vec_ref)
        h = h_ref[...]
        gl, dgl = _gelu_parts(z_ref[:, 3 * W:4 * W])
        dz_ref[:, 3 * W:4 * W] = (d_yr * h * dgl).astype(BF16)
        dh = dh + in_proj_bwd(3 * W, 4 * W)
        a_next = jnp.where(row == tm - 1, 1.0, pltpu.roll(a, tm - 1, 0))
        g_first = _scan_rows(a_next, d_yr * gl, gc[0:1, :], gbuf, reverse=True)
        g = gbuf[...]
        gc[0:1, :] = a[0:1, :] * g_first
        hprev = jnp.where(row == 0, halo_on * hh_ref[LRU_HALO - 1:LRU_HALO, :], pltpu.roll(h, 1, 0))
        d_log_a = (g * hprev) * a - (g * ig * xrv) * (a * a) / mult
        d_ig = g * mult * xrv
        d_xr = g * mult * ig
        spacc[0:1, :] += jnp.sum(d_log_a * r, axis=0, keepdims=True)
        d_pa32 = (d_log_a * (-LRU_C * sp)) * (r * (1.0 - r))
        d_px32 = d_ig * (ig * (1.0 - ig))
        accum(G_BA, d_pa32)
        accum(G_BX, d_px32)
        d_pa = d_pa32.astype(BF16)
        d_px = d_px32.astype(BF16)
        d_xr = d_xr + _dot(d_pa, bda_ref[...], NT) + _dot(d_px, bdx_ref[...], NT)
        dbda_ref[...] += _dot(xrb, d_pa, TN)
        dbdx_ref[...] += _dot(xrb, d_px, TN)
        accum(G_LCB, d_xr)
        rxbuf[0:LRU_HALO, :] = halo_on * zh_ref[CONV_HALO - LRU_HALO:CONV_HALO, 2 * W:3 * W]
        rxbuf[LRU_HALO:LRU_HALO + tm, :] = z_ref[:, 2 * W:3 * W]
        dxrbuf[0:tm, :] = d_xr
        d_rx = jnp.zeros((tm, W), F32)
        for k in range(LRU_K):
            off = LRU_HALO - (LRU_K - 1) + k
            accum(G_LW + k, d_xr * rxbuf[off:off + tm, :])
            d_rx = d_rx + lw_ref[k:k + 1, :] * dxrbuf[LRU_K - 1 - k:LRU_K - 1 - k + tm, :]
        dxrbuf[tm:tm + LRU_HALO, :] = dxrbuf[0:LRU_HALO, :]
        dz_ref[:, 2 * W:3 * W] = d_rx.astype(BF16)
        dh = dh + in_proj_bwd(2 * W, 3 * W)

        dx, dg = _rms_bwd(x1hat, x1rstd, gain, dh)
        dx1 = dx_ref[...] + dx
        dx1_ref[...] = dx1
        dob_ref[...] = (FFN_RES * dx1).astype(BF16)
        dg_ref[...] += dg

        @pl.when(first)
        def _():
            lam = vec_ref[V_LAM:V_LAM + 1, :]
            sg_ref[G_LAM:G_LAM + 1, :] = LRU_C * _sigmoid(-lam) * spacc[0:1, :]
            dwi_ref[...] = ai_ref[...].astype(BF16)
            dwo_ref[...] = ao_ref[...].astype(BF16)

    full = lambda a: pl.BlockSpec(a.shape, lambda i: (0,) * a.ndim)
    tile = lambda n: pl.BlockSpec((tm, n), lambda i: (nt - 1 - i, 0))
    halo = lambda rows, n: pl.BlockSpec(
        (rows, n), lambda i: (jnp.maximum((nt - 1 - i) * (tm // rows) - 1, 0), 0))
    const = lambda r, c: pl.BlockSpec((r, c), lambda i: (0, 0))
    return _call(
        body, name=name, grid=(nt,),
        in_specs=[tile(D), tile(4 * W), halo(CONV_HALO, 4 * W), tile(W), tile(W), tile(W), halo(LRU_HALO, W),
                  full(w_out), full(bda), full(bdx), full(cw), full(lw), full(vec),
                  tile(D), tile(2 * W), full(g), full(w_in_t)],
        out_specs=[tile(D), const(G_ROWS, W), const(W, W), const(W, W),
                   tile(D), const(1, D), const(4 * W, D), const(2 * W, D)],
        out_shape=[jax.ShapeDtypeStruct((T, D), F32), jax.ShapeDtypeStruct((G_ROWS, W), F32),
                   jax.ShapeDtypeStruct((W, W), F32), jax.ShapeDtypeStruct((W, W), F32),
                   jax.ShapeDtypeStruct((T, D), BF16), jax.ShapeDtypeStruct((1, D), F32),
                   jax.ShapeDtypeStruct((4 * W, D), BF16), jax.ShapeDtypeStruct((2 * W, D), BF16)],
        scratch_shapes=[pltpu.VMEM((tm + CONV_HALO, W), F32), pltpu.VMEM((tm + CONV_HALO, W), F32),
                        pltpu.VMEM((tm + LRU_HALO, W), F32), pltpu.VMEM((tm + LRU_HALO, W), F32),
                        pltpu.VMEM((tm, W), F32), pltpu.VMEM((8, W), F32), pltpu.VMEM((8, W), F32),
                        pltpu.VMEM((tm, 4 * W), BF16), pltpu.VMEM((4 * W, D), F32), pltpu.VMEM((2 * W, D), F32)],
        vmem_mib=60, args=(dx2, z, z, u1, xr, hst, hst, w_out, bda, bdx, cw, lw, vec, x1, mix, g, w_in_t),
        rider=rider)


def _pair_add(full, recv, name):
    K, _, _, rows, D = full.shape

    def body(c_ref, a_ref, b_ref, o_ref):
        o_ref[...] = (a_ref[...].astype(F32) + b_ref[...].astype(F32)).astype(BF16)

    c = lax.axis_index("c").astype(jnp.int32).reshape((1,))
    return _call(
        body, name=name, grid=(K, N_CHIP), num_scalar_prefetch=1,
        in_specs=[pl.BlockSpec((None, None, None, rows, D), lambda k, q, c_ref: (k, q, c_ref[0], 0, 0)),
                  pl.BlockSpec((None, None, rows, D), lambda k, q, c_ref: (k, q, 0, 0))],
        out_specs=pl.BlockSpec((None, None, rows, D), lambda k, q, c_ref: (k, q, 0, 0)),
        out_shape=jax.ShapeDtypeStruct(recv.shape, BF16),
        scratch_shapes=[], vmem_mib=16, args=(c, full, recv))


def _adamw_update(wv, gv, mv, vv):
    m2 = ADAM_B1 * mv + (1.0 - ADAM_B1) * gv
    v2 = ADAM_B2 * vv + (1.0 - ADAM_B2) * (gv * gv)
    m_hat = m2 / (1.0 - ADAM_B1 ** ADAM_STEP)
    v_hat = v2 / (1.0 - ADAM_B2 ** ADAM_STEP)
    return -ADAM_LR * (m_hat / (jnp.sqrt(v_hat) + ADAM_EPS) + ADAM_WD * wv), m2, v2


def _finish(parts, k, w, m, v, transpose, name):
    _, n_parts, rows, D = parts.shape

    def body(p_ref, w_ref, m_ref, v_ref, g_ref, d_ref, mo_ref, vo_ref):
        acc = p_ref[0].astype(F32)
        for q in range(1, n_parts):
            acc = acc + p_ref[q].astype(F32)
        gv = acc.T if transpose else acc
        g_ref[...] = gv
        d_ref[...], mo_ref[...], vo_ref[...] = _adamw_update(w_ref[...], gv, m_ref[...], v_ref[...])

    whole = pl.BlockSpec(w.shape, lambda i: (0, 0))
    return _call(
        body, name=name, grid=(1,),
        in_specs=[pl.BlockSpec((None, n_parts, rows, D), lambda i: (k, 0, 0, 0)), whole, whole, whole],
        out_specs=[whole] * 4, out_shape=[pltpu.HBM(w.shape, F32)] * 4,
        scratch_shapes=[], vmem_mib=40, args=(parts, w, m, v))


def _finish_many(items, name, rider=None):
    n_items = len(items)
    D = items[0][2].shape[1]
    max_rows = max(w.shape[0] for _, _, w, _, _ in items)
    max_parts = max(parts.shape[1] for parts, _, _, _, _ in items)
    n_loads, n_stores = 4, 4

    def body(*refs):
        ins, outs = refs[:4 * n_items], refs[4 * n_items:8 * n_items]
        pbuf, wbuf, obuf, sem = refs[8 * n_items:]
        step = pl.program_id(0)
        for j, (parts, k, w, _, _) in enumerate(items):
            rows, n_parts = w.shape[0], parts.shape[1]

            @pl.when(step == j)
            def _():
                p_ref, w_ref, m_ref, v_ref = ins[4 * j:4 * j + 4]
                loads = [pltpu.make_async_copy(p_ref.at[k], pbuf.at[0:n_parts, 0:rows], sem.at[0])]
                loads += [pltpu.make_async_copy(src, wbuf.at[r, 0:rows], sem.at[1 + r])
                          for r, src in enumerate((w_ref, m_ref, v_ref))]
                for cp in loads:
                    cp.start()
                for cp in loads:
                    cp.wait()
                acc = pbuf[0, 0:rows, :].astype(F32)
                for q in range(1, n_parts):
                    acc = acc + pbuf[q, 0:rows, :].astype(F32)
                obuf[0, 0:rows, :] = acc
                obuf[1, 0:rows, :], obuf[2, 0:rows, :], obuf[3, 0:rows, :] = _adamw_update(
                    wbuf[0, 0:rows, :], acc, wbuf[1, 0:rows, :], wbuf[2, 0:rows, :])
                stores = [pltpu.make_async_copy(obuf.at[r, 0:rows], outs[4 * j + r], sem.at[n_loads + r])
                          for r in range(n_stores)]
                for cp in stores:
                    cp.start()
                for cp in stores:
                    cp.wait()

    args = [a for parts, _, w, m, v in items for a in (parts, w, m, v)]
    shapes = [jax.ShapeDtypeStruct(w.shape, F32) for _, _, w, _, _ in items for _ in range(4)]
    return _call(
        body, name=name, grid=(n_items,),
        in_specs=[HBM_SPEC] * len(args), out_specs=[HBM_SPEC] * len(shapes), out_shape=shapes,
        scratch_shapes=[pltpu.VMEM((max_parts, max_rows, D), BF16), pltpu.VMEM((3, max_rows, D), F32),
                        pltpu.VMEM((4, max_rows, D), F32), pltpu.SemaphoreType.DMA((n_loads + n_stores,))],
        vmem_mib=40, args=args, rider=rider)


def _adamw_each(ws, gs, ms, vs, name):
    n = len(ws)

    def body(*refs):
        w_refs, g_refs, m_refs, v_refs, outs = refs[:n], refs[n:2 * n], refs[2 * n:3 * n], refs[3 * n:4 * n], refs[4 * n:]
        for k in range(n):
            outs[k][...], outs[n + k][...], outs[2 * n + k][...] = _adamw_update(
                w_refs[k][...], g_refs[k][...], m_refs[k][...], v_refs[k][...])

    shapes = [jax.ShapeDtypeStruct(w.shape, F32) for w in ws]
    return pl.pallas_call(
        body, name=name,
        in_specs=[VMEM_SPEC] * (4 * n), out_specs=[VMEM_SPEC] * (3 * n), out_shape=shapes * 3,
        compiler_params=pltpu.CompilerParams(vmem_limit_bytes=32 * MIB),
    )(*ws, *gs, *ms, *vs)


def _block_diag(w):
    h, d, _ = w.shape
    onto = jnp.eye(h, dtype=w.dtype)
    return (w[:, :, None, :] * onto[:, None, :, None]).reshape(h * d, h * d)


def _diag_blocks(m, h):
    d = m.shape[0] // h
    onto = jnp.eye(h, dtype=m.dtype)
    return (m.reshape(h, d, h, d) * onto[:, None, :, None]).sum(axis=2)


def _reduce_level1(full, tag):
    got = _run_comm(_sibling_comm(full), "rs_sibling_" + tag)
    return [_pair_add(a, b, "rs_pair_add_%s%d" % (tag, n)) for n, (a, b) in enumerate(zip(full, got))]


def kernel(x, ffn1_norm, ffn1_w_gate, ffn1_w_up, ffn1_w_down, mix_norm, w_in, conv_dw, conv_dw_bias, conv_ln_g, conv_ln_b, lru_conv_w, lru_conv_b, lru_w_a, lru_b_a, lru_w_x, lru_b_x, lru_lambda, w_out, ffn2_norm, ffn2_w_gate, ffn2_w_up, ffn2_w_down, final_norm, loss_target, m_ffn1_norm, m_ffn1_w_gate, m_ffn1_w_up, m_ffn1_w_down, m_mix_norm, m_w_in, m_conv_dw, m_conv_dw_bias, m_conv_ln_g, m_conv_ln_b, m_lru_conv_w, m_lru_conv_b, m_lru_w_a, m_lru_b_a, m_lru_w_x, m_lru_b_x, m_lru_lambda, m_w_out, m_ffn2_norm, m_ffn2_w_gate, m_ffn2_w_up, m_ffn2_w_down, m_final_norm, v_ffn1_norm, v_ffn1_w_gate, v_ffn1_w_up, v_ffn1_w_down, v_mix_norm, v_w_in, v_conv_dw, v_conv_dw_bias, v_conv_ln_g, v_conv_ln_b, v_lru_conv_w, v_lru_conv_b, v_lru_w_a, v_lru_b_a, v_lru_w_x, v_lru_b_x, v_lru_lambda, v_w_out, v_ffn2_norm, v_ffn2_w_gate, v_ffn2_w_up, v_ffn2_w_down, v_final_norm):
    T, D = x.shape[1], x.shape[2]
    F = ffn1_w_down.shape[0] * N_DEV
    rf = ffn1_w_down.shape[0]
    ri = w_in.shape[1]
    ro = w_out.shape[0]
    W = conv_dw_bias.shape[0]
    wc = conv_dw.shape[1]
    H = lru_w_a.shape[0]
    xs = x.reshape(T, D)
    tgt = loss_target.reshape(T, D)
    tm_ffn = min(512, T)
    tm_fwd = min(512, T)
    cf = 256
    tm_w = min(1024, T)
    tm_w1 = min(2048, T)
    tm_mix = min(256, T)
    tf_w = F // 2
    row = lambda v: v.reshape(1, -1)
    by_owner = lambda a, rows: a.reshape(a.shape[0], N_CHIP, 2, rows, D)

    p3a, p3b, p_in, p_out = _prep_weights(
        (ffn1_w_gate.T, ffn1_w_up.T, ffn1_w_down), (ffn2_w_gate.T, ffn2_w_up.T, ffn2_w_down), w_in, w_out,
        "prep_weights")
    tile_rows = lambda a: jnp.pad(a, ((0, -a.shape[0] % SUBLANES), (0, 0)))
    p_cw = jnp.concatenate([tile_rows(conv_dw), tile_rows(lru_conv_w)], axis=0)
    lw_row = p_cw.shape[0] - SUBLANES
    stacked = lambda r, j: r.at[:, j]
    plain = lambda r, j: r.at[j]
    g3_shape = jax.ShapeDtypeStruct((3, N_DEV, rf, D), BF16)
    (g3a,) = _all_gather([p3a], [stacked], [g3_shape], "ag_ffn1")
    w3a = g3a.reshape(3, F, D)
    bda = _block_diag(lru_w_a).astype(BF16)
    bdx = _block_diag(lru_w_x).astype(BF16)
    vec = jnp.concatenate([tile_rows(v[None]) for v in
                           (conv_dw_bias, conv_ln_g, conv_ln_b, lru_conv_b, lru_b_a, lru_b_x, lru_lambda)], axis=0)

    gather_rest = _gather_comm(
        [p3b, p_in, p_out, p_cw], [stacked, plain, plain, plain],
        [g3_shape, jax.ShapeDtypeStruct((N_DEV, ri, D), BF16), jax.ShapeDtypeStruct((N_DEV, ro, D), BF16),
         jax.ShapeDtypeStruct((N_DEV,) + p_cw.shape, F32)],
        [(SIBLING,) + SAME_CORE, EVERYONE, EVERYONE, EVERYONE])
    (x1, h1, dau1, dag1, act1), (g3b_half, g_in, g_out, g_cw) = _ffn_fwd(
        xs, row(ffn1_norm), w3a, tm_fwd, cf, "ffn1_fwd", rider=gather_rest)
    w_in_t = g_in.reshape(N_DEV * ri, D)
    w_out_f = g_out.reshape(N_DEV * ro, D)
    cw_all = jnp.transpose(g_cw, (1, 0, 2)).reshape(p_cw.shape[0], N_DEV * wc)
    cw = cw_all[0:CONV_K]
    lw = cw_all[lw_row:lw_row + LRU_K]
    (x2, z, mix, u1, xr, hst), (g3b,) = _mix_core_fwd(
        x1, row(mix_norm), w_in_t, w_out_f, bda, bdx, cw, lw, vec, min(512, T), "mix_core_fwd",
        rider=_forward_comm([g3b_half], [stacked]))
    w3b = g3b.reshape(3, F, D)
    dx3, dob2, d_final_norm, loss_part, h3, dau2, dag2, act2 = _ffn_fwd(
        x2, row(ffn2_norm), w3b, tm_fwd, cf, "ffn2_fwd_loss", head=(row(final_norm), tgt))

    dx2, dgate2, dup2, d_ffn2_norm = _ffn_dgrad(dx3, x2, row(ffn2_norm), dau2, dag2, w3b, tm_ffn, cf, "ffn2_dgrad")
    (dw_gu2,) = _wgrad([dgate2, dup2], h3, tm_w, tf_w, "ffn2_wgrad_gu")
    (dw_d2,) = _wgrad([act2], dob2, tm_w1, tf_w, "ffn2_wgrad_d")
    by_device = lambda a: a.reshape(a.shape[0], N_DEV, rf, D)
    (dx1, sg, dbda, dbdx, dob1, d_mix_norm, dw_in_t, dw_out), parts_f2 = _mix_bwd(
        dx2, z, u1, xr, hst, x1, mix, row(mix_norm), w_in_t, w_out_f, bda, bdx, cw, lw, vec, tm_mix, "mix_bwd",
        rider=_chips_comm([by_device(dw_gu2), by_device(dw_d2)], every_device=True))
    io = [dw_in_t.reshape(1, N_DEV, ri, D), dw_out.reshape(1, N_DEV, ro, D)]
    (dw_d1,), parts_io = _wgrad(
        [act1], dob1, tm_w1, tf_w, "ffn1_wgrad_d", rider=_chips_comm(io, every_device=True))
    sums_d1 = _reduce_level1([by_owner(dw_d1, rf)], "d1")
    dx0, dgate1, dup1, d_ffn1_norm = _ffn_dgrad(dx1, xs, row(ffn1_norm), dau1, dag1, w3a, tm_ffn, cf, "ffn1_dgrad")
    (dw_g1,), parts_d1 = _wgrad([dgate1], h1, tm_w1, tf_w, "ffn1_wgrad_g", rider=_chips_comm(sums_d1))
    sums_g1 = _reduce_level1([by_owner(dw_g1, rf)], "g1")
    small = [d_ffn1_norm, d_mix_norm, d_ffn2_norm, d_final_norm, sg, _diag_blocks(dbda, H).reshape(-1, D),
             _diag_blocks(dbdx, H).reshape(-1, D), loss_part]
    (dw_u1,), summed_and_parts = _wgrad(
        [dup1], h1, tm_w1, tf_w, "ffn1_wgrad_u", rider=_both(_small_sum_comm(small), _chips_comm(sums_g1)))
    summed, parts_g1 = summed_and_parts[:len(small)], summed_and_parts[len(small):]
    sums_u1 = _reduce_level1([by_owner(dw_u1, rf)], "u1")

    g_norm1, g_norm_mix, g_norm2, g_norm_final, g_sg, g_w_a, g_w_x, g_loss = summed
    loss = g_loss[0, 0]
    me = 4 * lax.axis_index("x") + 2 * lax.axis_index("y") + lax.axis_index("c")
    chan = lambda full_g: lax.dynamic_slice_in_dim(full_g, me * wc, wc, axis=1)
    grads = {
        "ffn1_norm": g_norm1.reshape(D), "mix_norm": g_norm_mix.reshape(D), "ffn2_norm": g_norm2.reshape(D),
        "final_norm": g_norm_final.reshape(D),
        "conv_dw_bias": g_sg[G_CB], "conv_ln_g": g_sg[G_LNG], "conv_ln_b": g_sg[G_LNB],
        "lru_conv_b": g_sg[G_LCB], "lru_b_a": g_sg[G_BA], "lru_b_x": g_sg[G_BX], "lru_lambda": g_sg[G_LAM],
        "lru_w_a": g_w_a.reshape(lru_w_a.shape), "lru_w_x": g_w_x.reshape(lru_w_x.shape),
        "conv_dw": chan(g_sg[G_CW:G_CW + CONV_K]), "lru_conv_w": chan(g_sg[G_LW:G_LW + LRU_K]),
    }

    weights = dict(ffn1_norm=ffn1_norm, ffn1_w_gate=ffn1_w_gate, ffn1_w_up=ffn1_w_up, ffn1_w_down=ffn1_w_down, mix_norm=mix_norm, w_in=w_in, conv_dw=conv_dw, conv_dw_bias=conv_dw_bias, conv_ln_g=conv_ln_g, conv_ln_b=conv_ln_b, lru_conv_w=lru_conv_w, lru_conv_b=lru_conv_b, lru_w_a=lru_w_a, lru_b_a=lru_b_a, lru_w_x=lru_w_x, lru_b_x=lru_b_x, lru_lambda=lru_lambda, w_out=w_out, ffn2_norm=ffn2_norm, ffn2_w_gate=ffn2_w_gate, ffn2_w_up=ffn2_w_up, ffn2_w_down=ffn2_w_down, final_norm=final_norm)
    moment1 = dict(ffn1_norm=m_ffn1_norm, ffn1_w_gate=m_ffn1_w_gate, ffn1_w_up=m_ffn1_w_up, ffn1_w_down=m_ffn1_w_down, mix_norm=m_mix_norm, w_in=m_w_in, conv_dw=m_conv_dw, conv_dw_bias=m_conv_dw_bias, conv_ln_g=m_conv_ln_g, conv_ln_b=m_conv_ln_b, lru_conv_w=m_lru_conv_w, lru_conv_b=m_lru_conv_b, lru_w_a=m_lru_w_a, lru_b_a=m_lru_b_a, lru_w_x=m_lru_w_x, lru_b_x=m_lru_b_x, lru_lambda=m_lru_lambda, w_out=m_w_out, ffn2_norm=m_ffn2_norm, ffn2_w_gate=m_ffn2_w_gate, ffn2_w_up=m_ffn2_w_up, ffn2_w_down=m_ffn2_w_down, final_norm=m_final_norm)
    moment2 = dict(ffn1_norm=v_ffn1_norm, ffn1_w_gate=v_ffn1_w_gate, ffn1_w_up=v_ffn1_w_up, ffn1_w_down=v_ffn1_w_down, mix_norm=v_mix_norm, w_in=v_w_in, conv_dw=v_conv_dw, conv_dw_bias=v_conv_dw_bias, conv_ln_g=v_conv_ln_g, conv_ln_b=v_conv_ln_b, lru_conv_w=v_lru_conv_w, lru_conv_b=v_lru_conv_b, lru_w_a=v_lru_w_a, lru_b_a=v_lru_b_a, lru_w_x=v_lru_w_x, lru_b_x=v_lru_b_x, lru_lambda=v_lru_lambda, w_out=v_w_out, ffn2_norm=v_ffn2_norm, ffn2_w_gate=v_ffn2_w_gate, ffn2_w_up=v_ffn2_w_up, ffn2_w_down=v_ffn2_w_down, final_norm=v_final_norm)
    order = list(weights)
    gate_up = {"ffn1_w_gate", "ffn1_w_up", "ffn2_w_gate", "ffn2_w_up"}
    view = lambda n, a: a.T if n in gate_up else a
    operands = lambda n: [view(n, d[n]) for d in (weights, moment1, moment2)]
    delta, new_m, new_v = {}, {}, {}
    six = {"ffn1_w_gate": (parts_g1[0], 0), "ffn1_w_down": (parts_d1[0], 0), "w_out": (parts_io[1], 0),
           "ffn2_w_gate": (parts_f2[0], 0), "ffn2_w_up": (parts_f2[0], 1), "ffn2_w_down": (parts_f2[1], 0)}
    results, parts_u1 = _finish_many([(parts, k, *operands(n)) for n, (parts, k) in six.items()], "finish_six",
                                     rider=_chips_comm(sums_u1))
    for j, n in enumerate(six):
        grads[n], delta[n], new_m[n], new_v[n] = [view(n, r) for r in results[4 * j:4 * j + 4]]
    for n, parts, d_major in (("ffn1_w_up", parts_u1[0], False), ("w_in", parts_io[0], True)):
        grads[n], delta[n], new_m[n], new_v[n] = [
            view(n, r) for r in _finish(parts, 0, *operands(n), d_major, "finish_" + n)]
    big = set(six) | {"ffn1_w_up", "w_in"}
    rest = [n for n in order if n not in big]
    updates = _adamw_each([weights[n] for n in rest], [grads[n] for n in rest], [moment1[n] for n in rest],
                          [moment2[n] for n in rest], "adamw_small")
    for k, n in enumerate(rest):
        delta[n], new_m[n], new_v[n] = updates[k], updates[len(rest) + k], updates[2 * len(rest) + k]

    return (loss, dx0.reshape(x.shape), *[grads[n] for n in order], *[delta[n] for n in order],
            *[new_m[n] for n in order], *[new_v[n] for n in order])
```

```python
import functools
import math

import jax
import jax.numpy as jnp
from jax import lax
from jax.experimental import pallas as pl
from jax.experimental.pallas import tpu as pltpu

F32 = jnp.float32
BF16 = jnp.bfloat16
MESH = pl.DeviceIdType.MESH

N_DEV = 8
N_CHIP = 4
SUBLANES = 8
RMS_EPS = 1e-6
LN_EPS = 1e-5
LRU_C = 8.0
CONV_K = 31
LRU_K = 4
CONV_HALO = 32
LRU_HALO = 8
FFN_RES = 0.5
ADAM_LR, ADAM_B1, ADAM_B2, ADAM_EPS, ADAM_WD, ADAM_STEP = 0.001, 0.9, 0.999, 1e-08, 0.01, 10
GELU_K = math.sqrt(2.0 / math.pi)
GELU_C = 0.044715

MIB = 1024 * 1024
NT = (((1,), (1,)), ((), ()))
NN = (((1,), (0,)), ((), ()))
TN = (((0,), (0,)), ((), ()))

V_CB, V_LNG, V_LNB, V_LCB, V_BA, V_BX, V_LAM = range(0, 7 * SUBLANES, SUBLANES)
G_CW = 0
G_CB, G_LNG, G_LNB = 31, 32, 33
G_LW = 34
G_LCB, G_BA, G_BX, G_LAM = 38, 39, 40, 41
G_ROWS = 48

HBM_SPEC = pl.BlockSpec(memory_space=pltpu.HBM)
VMEM_SPEC = pl.BlockSpec(memory_space=pltpu.VMEM)


def _dot(a, b, dims):
    return lax.dot_general(a, b, dims, preferred_element_type=F32)


def _sigmoid(x):
    return 1.0 / (1.0 + jnp.exp(-x))


def _gelu_parts(x):
    x2 = x * x
    th = jnp.tanh(GELU_K * x * (1.0 + GELU_C * x2))
    gl = 0.5 * x * (1.0 + th)
    dgl = 0.5 * (1.0 + th) + 0.5 * x * (1.0 - th * th) * GELU_K * (1.0 + 3.0 * GELU_C * x2)
    return gl, dgl


def _neg_expm1(y):
    series = -y * (1.0 + y * (1.0 / 2) * (1.0 + y * (1.0 / 3) * (1.0 + y * (1.0 / 4) * (1.0 + y * (1.0 / 5) * (1.0 + y * (1.0 / 6))))))
    return jnp.where(y > -0.25, series, 1.0 - jnp.exp(y))


def _softplus_neg(lam):
    t = -lam
    e = jnp.exp(-jnp.abs(t))
    s = 1.0 + e
    log1p_e = jnp.log(s) - ((s - 1.0) - e) / s
    return jnp.maximum(t, 0.0) + log1p_e


def _rms_stats(xv):
    rstd = lax.rsqrt(jnp.mean(xv * xv, axis=-1, keepdims=True) + RMS_EPS)
    return xv * rstd, rstd


def _rms_bwd(xhat, rstd, g, dh):
    dxhat = dh * g
    dx = rstd * (dxhat - xhat * jnp.mean(dxhat * xhat, axis=-1, keepdims=True))
    return dx, jnp.sum(dh * xhat, axis=0, keepdims=True)


def _row_windows(buf_ref, n_rows, offsets):
    total = buf_ref.shape[0]
    full = buf_ref[...]
    for b in range(SUBLANES):
        offs = [o for o in offsets if o % SUBLANES == b]
        if not offs:
            continue
        assert max(offs) + n_rows <= total
        moved = full if b == 0 else pltpu.roll(full, total - b, 0)
        for o in offs:
            yield o, moved[o - b:o - b + n_rows, :]


def _scan_rows(av, bv, edge, out_ref, reverse=False):
    tm, W = av.shape
    sub = lax.broadcasted_iota(jnp.int32, (tm, W), 0) % SUBLANES
    s = 1
    while s < SUBLANES:
        keep = (sub < SUBLANES - s) if reverse else (sub >= s)
        shift = tm - s if reverse else s
        bv = jnp.where(keep, av * pltpu.roll(bv, shift, 0) + bv, bv)
        av = jnp.where(keep, av * pltpu.roll(av, shift, 0), av)
        s *= 2
    starts = range(0, tm, SUBLANES)
    for r0 in (reversed(starts) if reverse else starts):
        group = av[r0:r0 + SUBLANES, :] * edge + bv[r0:r0 + SUBLANES, :]
        out_ref[r0:r0 + SUBLANES, :] = group
        edge = group[0:1, :] if reverse else group[SUBLANES - 1:SUBLANES, :]
    return edge


class _Comm:
    def __init__(self, arrays, in_specs, out_shapes, out_specs, scratch, start, wait, aliases=None):
        self.arrays, self.in_specs = list(arrays), list(in_specs)
        self.out_shapes, self.out_specs = list(out_shapes), list(out_specs)
        self.scratch, self.start, self.wait = list(scratch), start, wait
        self.aliases = dict(aliases or {})


def _in_hbm(a):
    return pltpu.with_memory_space_constraint(a, pltpu.HBM)


def _operands(comm):
    return [a if spec is VMEM_SPEC else _in_hbm(a) for a, spec in zip(comm.arrays, comm.in_specs)]


def _call(body, *, name, grid, in_specs, out_specs, out_shape, scratch_shapes, vmem_mib, args, rider=None,
          num_scalar_prefetch=0):
    params = pltpu.CompilerParams(dimension_semantics=("arbitrary",) * len(grid), vmem_limit_bytes=vmem_mib * MIB)
    args = [a if k < num_scalar_prefetch else _in_hbm(a) for k, a in enumerate(args)]
    if rider is None:
        return pl.pallas_call(
            body, name=name,
            grid_spec=pltpu.PrefetchScalarGridSpec(
                num_scalar_prefetch=num_scalar_prefetch, grid=grid, in_specs=in_specs, out_specs=out_specs,
                scratch_shapes=scratch_shapes),
            out_shape=out_shape, compiler_params=params)(*args)
    assert num_scalar_prefetch == 0
    n_in, n_out, n_scr = len(in_specs), len(out_specs), len(scratch_shapes)
    r_in, r_out = len(rider.arrays), len(rider.out_shapes)
    n_axes = len(grid)

    def carried(*refs):
        pos = [0]

        def take(n):
            pos[0] += n
            return refs[pos[0] - n:pos[0]]

        ins, r_ins, outs, r_outs, scr, r_scr = take(n_in), take(r_in), take(n_out), take(r_out), take(n_scr), take(len(rider.scratch))
        first = pl.program_id(0) == 0
        last = pl.program_id(0) == grid[0] - 1
        for ax in range(1, n_axes):
            first = first & (pl.program_id(ax) == 0)
            last = last & (pl.program_id(ax) == grid[ax] - 1)

        @pl.when(first)
        def _():
            rider.start(r_ins, r_outs, r_scr)

        body(*ins, *outs, *scr)

        @pl.when(last)
        def _():
            rider.wait(r_ins, r_outs, r_scr)

    res = pl.pallas_call(
        carried, name=name,
        grid=grid,
        in_specs=list(in_specs) + rider.in_specs,
        out_specs=list(out_specs) + rider.out_specs,
        out_shape=list(out_shape) + rider.out_shapes,
        scratch_shapes=list(scratch_shapes) + rider.scratch,
        input_output_aliases={n_in + i: n_out + o for i, o in rider.aliases.items()},
        compiler_params=params)(*args, *_operands(rider))
    return res[:n_out], res[n_out:]


def _run_comm(comm, name):
    n_in, n_out = len(comm.arrays), len(comm.out_shapes)

    def body(*refs):
        ins, outs, scr = refs[:n_in], refs[n_in:n_in + n_out], refs[n_in + n_out:]
        comm.start(ins, outs, scr)
        comm.wait(ins, outs, scr)

    return pl.pallas_call(
        body, name=name,
        in_specs=comm.in_specs, out_specs=comm.out_specs, out_shape=comm.out_shapes,
        scratch_shapes=comm.scratch, input_output_aliases=comm.aliases,
        compiler_params=pltpu.CompilerParams(vmem_limit_bytes=24 * MIB))(*_operands(comm))


def _both(a, b):
    ni, no, ns = len(a.arrays), len(a.out_shapes), len(a.scratch)

    def start(ins, outs, scr):
        a.start(ins[:ni], outs[:no], scr[:ns])
        b.start(ins[ni:], outs[no:], scr[ns:])

    def wait(ins, outs, scr):
        a.wait(ins[:ni], outs[:no], scr[:ns])
        b.wait(ins[ni:], outs[no:], scr[ns:])

    aliases = dict(a.aliases)
    aliases.update({ni + i: no + o for i, o in b.aliases.items()})
    return _Comm(a.arrays + b.arrays, a.in_specs + b.in_specs, a.out_shapes + b.out_shapes,
                 a.out_specs + b.out_specs, a.scratch + b.scratch, start, wait, aliases)


def _place():
    return lax.axis_index("x"), lax.axis_index("y"), lax.axis_index("c")


def _peer(k):
    x, y, c = _place()
    px, py, pc = x ^ ((k >> 2) & 1), y ^ ((k >> 1) & 1), c ^ (k & 1)
    return (px, py, pc), 4 * px + 2 * py + pc


SIBLING = 1
SAME_CORE = (2, 4, 6)
EVERYONE = tuple(range(1, N_DEV))


def _gather_comm(shards, views, out_shapes, relations):
    na = len(shards)

    def copies(ins, outs, scr):
        send_sems, recv_sems, _ = scr
        _, me = _peer(0)
        out = []
        for a in range(na):
            for k in relations[a]:
                peer, theirs = _peer(k)
                send = functools.partial(
                    pltpu.make_async_remote_copy,
                    src_ref=ins[a], dst_ref=views[a](outs[a], me),
                    send_sem=send_sems.at[7 * a + k - 1], recv_sem=recv_sems.at[7 * a + k - 1],
                    device_id=peer, device_id_type=MESH)
                recv = functools.partial(
                    pltpu.make_async_remote_copy,
                    src_ref=ins[a], dst_ref=views[a](outs[a], theirs),
                    send_sem=send_sems.at[7 * a + k - 1], recv_sem=recv_sems.at[7 * a + k - 1],
                    device_id=peer, device_id_type=MESH)
                out.append((send, recv))
        return out

    def local(ins, outs, scr):
        _, me = _peer(0)
        return [pltpu.make_async_copy(ins[a], views[a](outs[a], me), scr[2].at[a]) for a in range(na)]

    def start(ins, outs, scr):
        for cp in local(ins, outs, scr):
            cp.start()
        for send, _ in copies(ins, outs, scr):
            send().start()

    def wait(ins, outs, scr):
        for _, recv in copies(ins, outs, scr):
            recv().wait_recv()
        for send, _ in copies(ins, outs, scr):
            send().wait_send()
        for cp in local(ins, outs, scr):
            cp.wait()

    return _Comm(shards, [HBM_SPEC] * na, out_shapes, [HBM_SPEC] * na,
                 [pltpu.SemaphoreType.DMA((7 * na,)), pltpu.SemaphoreType.DMA((7 * na,)),
                  pltpu.SemaphoreType.DMA((na,))], start, wait)


def _forward_comm(gathered, views):
    na = len(gathered)
    shapes = [jax.ShapeDtypeStruct(g.shape, g.dtype) for g in gathered]

    def copies(outs, scr):
        send_sems, recv_sems = scr
        sibling, _ = _peer(SIBLING)
        out = []
        for a in range(na):
            for n, k in enumerate(SAME_CORE):
                _, mine = _peer(k)
                _, theirs = _peer(k ^ SIBLING)
                send = functools.partial(
                    pltpu.make_async_remote_copy,
                    src_ref=views[a](outs[a], mine), dst_ref=views[a](outs[a], mine),
                    send_sem=send_sems.at[3 * a + n], recv_sem=recv_sems.at[3 * a + n],
                    device_id=sibling, device_id_type=MESH)
                recv = functools.partial(
                    pltpu.make_async_remote_copy,
                    src_ref=views[a](outs[a], mine), dst_ref=views[a](outs[a], theirs),
                    send_sem=send_sems.at[3 * a + n], recv_sem=recv_sems.at[3 * a + n],
                    device_id=sibling, device_id_type=MESH)
                out.append((send, recv))
        return out

    def start(ins, outs, scr):
        for send, _ in copies(outs, scr):
            send().start()

    def wait(ins, outs, scr):
        for _, recv in copies(outs, scr):
            recv().wait_recv()
        for send, _ in copies(outs, scr):
            send().wait_send()

    return _Comm(gathered, [HBM_SPEC] * na, shapes, [HBM_SPEC] * na,
                 [pltpu.SemaphoreType.DMA((3 * na,)), pltpu.SemaphoreType.DMA((3 * na,))], start, wait,
                 aliases={a: a for a in range(na)})


Y_NEIGHBOUR, X_NEIGHBOUR, DIAGONAL = SAME_CORE


def _all_gather(stack, name):
    na = stack.shape[0]
    views = [lambda r, j, k=k: r.at[k, j] for k in range(na)]
    out_shape = jax.ShapeDtypeStruct((na, N_DEV) + stack.shape[1:], stack.dtype)
    near = (SIBLING, Y_NEIGHBOUR, X_NEIGHBOUR)
    level1 = _gather_comm([stack] * na, views, [out_shape] * na, [near] * na)

    def body(stack_ref, out_ref, *scratch):
        ins, outs = [stack_ref.at[k] for k in range(na)], [out_ref] * na
        send_sems, recv_sems, local_sems, fwd_send, fwd_recv, relay_send, relay_recv = scratch
        sibling, _ = _peer(SIBLING)
        c = lax.axis_index("c")
        level1.start(ins, outs, (send_sems, recv_sems, local_sems))

        def block_copy(a, block, to, send_sem, recv_sem):
            return pltpu.make_async_remote_copy(
                src_ref=views[a](outs[a], block), dst_ref=views[a](outs[a], block),
                send_sem=send_sem, recv_sem=recv_sem, device_id=to, device_id_type=MESH)

        def to_sibling(a, n, k):
            _, mine = _peer(k)
            _, theirs = _peer(k ^ SIBLING)
            fwd = block_copy(a, mine, sibling, fwd_send.at[3 * a + n], fwd_recv.at[3 * a + n])
            fwd.start()
            return fwd, block_copy(a, theirs, sibling, fwd_send.at[3 * a + n], fwd_recv.at[3 * a + n])

        passed, landing = [], []
        for a in range(na):
            for n, k in enumerate((Y_NEIGHBOUR, X_NEIGHBOUR)):
                peer, origin = _peer(k)
                pltpu.make_async_remote_copy(
                    src_ref=ins[a], dst_ref=views[a](outs[a], origin),
                    send_sem=send_sems.at[7 * a + k - 1], recv_sem=recv_sems.at[7 * a + k - 1],
                    device_id=peer, device_id_type=MESH).wait_recv()

                @pl.when(c == (0 if k == X_NEIGHBOUR else 1))
                def _():
                    other, _ = _peer(DIAGONAL ^ k)
                    block_copy(a, origin, other, relay_send.at[a], relay_recv.at[a]).start()

                fwd, lands = to_sibling(a, n, k)
                passed.append(fwd)
                landing.append(lands)
        for a in range(na):
            _, far = _peer(DIAGONAL)
            block_copy(a, far, sibling, relay_send.at[a], relay_recv.at[a]).wait_recv()
            fwd, lands = to_sibling(a, 2, DIAGONAL)
            passed.append(fwd)
            landing.append(lands)
        for a in range(na):
            _, theirs = _peer(SIBLING)
            pltpu.make_async_remote_copy(
                src_ref=ins[a], dst_ref=views[a](outs[a], theirs),
                send_sem=send_sems.at[7 * a + SIBLING - 1], recv_sem=recv_sems.at[7 * a + SIBLING - 1],
                device_id=sibling, device_id_type=MESH).wait_recv()
        for cp in landing:
            cp.wait_recv()
        for cp in passed:
            cp.wait_send()
        _, me = _peer(0)
        for a in range(na):
            block_copy(a, me, sibling, relay_send.at[a], relay_recv.at[a]).wait_send()
            for k in near:
                peer, _ = _peer(k)
                pltpu.make_async_remote_copy(
                    src_ref=ins[a], dst_ref=views[a](outs[a], me),
                    send_sem=send_sems.at[7 * a + k - 1], recv_sem=recv_sems.at[7 * a + k - 1],
                    device_id=peer, device_id_type=MESH).wait_send()
            pltpu.make_async_copy(ins[a], views[a](outs[a], me), local_sems.at[a]).wait()

    return pl.pallas_call(
        body, name=name,
        in_specs=[HBM_SPEC], out_specs=HBM_SPEC, out_shape=out_shape,
        scratch_shapes=level1.scratch + [pltpu.SemaphoreType.DMA((3 * na,)), pltpu.SemaphoreType.DMA((3 * na,)),
                                         pltpu.SemaphoreType.DMA((na,)), pltpu.SemaphoreType.DMA((na,))],
    )(_in_hbm(stack))


def _sibling_comm(grads):
    na = len(grads)
    shapes = [jax.ShapeDtypeStruct(g.shape[:2] + g.shape[3:], g.dtype) for g in grads]

    def copies(ins, outs, scr):
        x, y, c = _place()
        return [pltpu.make_async_remote_copy(
            src_ref=ins[a].at[:, :, 1 - c], dst_ref=outs[a],
            send_sem=scr[0].at[a], recv_sem=scr[1].at[a],
            device_id=(x, y, 1 - c), device_id_type=MESH) for a in range(na)]

    def start(ins, outs, scr):
        for cp in copies(ins, outs, scr):
            cp.start()

    def wait(ins, outs, scr):
        for cp in copies(ins, outs, scr):
            cp.wait()

    return _Comm(grads, [HBM_SPEC] * na, shapes, [HBM_SPEC] * na,
                 [pltpu.SemaphoreType.DMA((na,)), pltpu.SemaphoreType.DMA((na,))], start, wait)


def _chips_comm(sums, every_device=False):
    na = len(sums)
    shapes = [jax.ShapeDtypeStruct(s.shape, s.dtype) for s in sums]
    relations = EVERYONE if every_device else SAME_CORE
    nr = len(relations)

    def block(px, py, pc):
        return 4 * px + 2 * py + pc if every_device else 2 * px + py

    def copies(ins, outs, scr):
        mine = block(*_place())
        out = []
        for a in range(na):
            for n, k in enumerate(relations):
                peer, _ = _peer(k)
                theirs = block(*peer)
                send = functools.partial(
                    pltpu.make_async_remote_copy,
                    src_ref=ins[a].at[:, theirs], dst_ref=outs[a].at[:, mine],
                    send_sem=scr[0].at[nr * a + n], recv_sem=scr[1].at[nr * a + n],
                    device_id=peer, device_id_type=MESH)
                recv = functools.partial(
                    pltpu.make_async_remote_copy,
                    src_ref=ins[a].at[:, mine], dst_ref=outs[a].at[:, theirs],
                    send_sem=scr[0].at[nr * a + n], recv_sem=scr[1].at[nr * a + n],
                    device_id=peer, device_id_type=MESH)
                out.append((send, recv))
        return out

    def local(ins, outs, scr):
        mine = block(*_place())
        return [pltpu.make_async_copy(ins[a].at[:, mine], outs[a].at[:, mine], scr[2].at[a]) for a in range(na)]

    def start(ins, outs, scr):
        for cp in local(ins, outs, scr):
            cp.start()
        for send, _ in copies(ins, outs, scr):
            send().start()

    def wait(ins, outs, scr):
        for _, recv in copies(ins, outs, scr):
            recv().wait_recv()
        for send, _ in copies(ins, outs, scr):
            send().wait_send()
        for cp in local(ins, outs, scr):
            cp.wait()

    return _Comm(sums, [HBM_SPEC] * na, shapes, [HBM_SPEC] * na,
                 [pltpu.SemaphoreType.DMA((nr * na,)), pltpu.SemaphoreType.DMA((nr * na,)),
                  pltpu.SemaphoreType.DMA((na,))], start, wait)


def _small_sum_comm(arrays):
    na = len(arrays)

    def copies(ins, scr):
        bufs, send_sems, recv_sems = scr[:na], scr[na], scr[na + 1]
        _, me = _peer(0)
        out = []
        for a in range(na):
            for k in EVERYONE:
                peer, theirs = _peer(k)
                sems = dict(send_sem=send_sems.at[7 * a + k - 1], recv_sem=recv_sems.at[7 * a + k - 1])
                send = functools.partial(
                    pltpu.make_async_remote_copy,
                    src_ref=ins[a], dst_ref=bufs[a].at[me], device_id=peer, device_id_type=MESH, **sems)
                recv = functools.partial(
                    pltpu.make_async_remote_copy,
                    src_ref=ins[a], dst_ref=bufs[a].at[theirs], device_id=peer, device_id_type=MESH, **sems)
                out.append((send, recv))
        return out

    def start(ins, outs, scr):
        _, me = _peer(0)
        for a in range(na):
            scr[a][me] = ins[a][...]
        for send, _ in copies(ins, scr):
            send().start()

    def wait(ins, outs, scr):
        for _, recv in copies(ins, scr):
            recv().wait_recv()
        for send, _ in copies(ins, scr):
            send().wait_send()
        for a in range(na):
            acc = scr[a][0]
            for j in range(1, N_DEV):
                acc = acc + scr[a][j]
            outs[a][...] = acc

    return _Comm(arrays, [VMEM_SPEC] * na, [jax.ShapeDtypeStruct(s.shape, F32) for s in arrays], [VMEM_SPEC] * na,
                 [pltpu.VMEM((N_DEV,) + s.shape, F32) for s in arrays]
                 + [pltpu.SemaphoreType.DMA((7 * na,)), pltpu.SemaphoreType.DMA((7 * na,))], start, wait)


def _prep_weights(ffn1, ffn2, w_in, w_out, name):
    rf, D = ffn1[2].shape
    ri, ro = w_in.shape[1], w_out.shape[0]

    def body(g1, u1, d1, g2, u2, d2, wi, wo, p1_ref, p2_ref, pi_ref, po_ref):
        for p_ref, shards in ((p1_ref, (g1, u1, d1)), (p2_ref, (g2, u2, d2))):
            for k, shard in enumerate(shards):
                p_ref[k] = shard[...].astype(BF16)
        pi_ref[...] = wi[...].T.astype(BF16)
        po_ref[...] = wo[...].astype(BF16)

    args = (*ffn1, *ffn2, w_in, w_out)
    whole = lambda shape: pl.BlockSpec(shape, lambda i: (0,) * len(shape))
    out_shapes = [(3, rf, D), (3, rf, D), (ri, D), (ro, D)]
    return _call(
        body, name=name, grid=(1,),
        in_specs=[whole(a.shape) for a in args], out_specs=[whole(s) for s in out_shapes],
        out_shape=[jax.ShapeDtypeStruct(s, BF16) for s in out_shapes],
        scratch_shapes=[], vmem_mib=48, args=args)


def _load_weights(w_hbm, w_vmem, sem):
    @pl.when(pl.program_id(0) == 0)
    def _():
        copies = [pltpu.make_async_copy(w_hbm.at[k], w_vmem.at[k], sem.at[k]) for k in range(3)]
        for cp in copies:
            cp.start()
        for cp in copies:
            cp.wait()


def _ffn_fwd(x, g, w3, tm, cf, name, rider=None, head=None):
    T, D = x.shape
    F = w3.shape[1]
    n_head = 0 if head is None else 2

    def body(x_ref, g_ref, w_hbm, *refs):
        head_refs, refs = refs[:n_head], refs[n_head:]
        if head is None:
            (xo_ref, h_ref, dau_ref, dag_ref, act_ref, wv, sem) = refs
        else:
            (dx_ref, dob_ref, dgf_ref, loss_ref, h_ref, dau_ref, dag_ref, act_ref, wv, sem) = refs
        _load_weights(w_hbm, wv, sem)
        xhat, _ = _rms_stats(x_ref[...])
        hb = (xhat * g_ref[...]).astype(BF16)
        h_ref[...] = hb
        for lo in range(0, F, cf):
            gate = _dot(hb, wv[0, lo:lo + cf, :], NT)
            up = _dot(hb, wv[1, lo:lo + cf, :], NT)
            sig = _sigmoid(gate)
            silu = gate * sig
            dau_ref[:, lo:lo + cf] = silu.astype(BF16)
            dag_ref[:, lo:lo + cf] = (up * (sig * (1.0 + gate * (1.0 - sig)))).astype(BF16)
            act_ref[:, lo:lo + cf] = (silu * up).astype(BF16)
        x_out = x_ref[...] + FFN_RES * _dot(act_ref[...], wv[2], NN)
        if head is None:
            xo_ref[...] = x_out
            return

        @pl.when(pl.program_id(0) == 0)
        def _():
            dgf_ref[...] = jnp.zeros_like(dgf_ref)
            loss_ref[...] = jnp.zeros_like(loss_ref)

        gf_ref, tgt_ref = head_refs
        yhat, rstd = _rms_stats(x_out)
        gf = gf_ref[...]
        err = yhat * gf - tgt_ref[...]
        loss_ref[...] += (0.5 / D) * jnp.sum(err * err)
        dx, dgf = _rms_bwd(yhat, rstd, gf, err * (1.0 / D))
        dx_ref[...] = dx
        dob_ref[...] = (FFN_RES * dx).astype(BF16)
        dgf_ref[...] += dgf

    row = pl.BlockSpec((tm, D), lambda i: (i, 0))
    hid = pl.BlockSpec((tm, F), lambda i: (i, 0))
    vec = pl.BlockSpec((1, D), lambda i: (0, 0))
    row_f32, row_bf16 = jax.ShapeDtypeStruct((T, D), F32), jax.ShapeDtypeStruct((T, D), BF16)
    if head is None:
        first_specs, first_shapes = [row], [row_f32]
    else:
        first_specs = [row, row, vec, pl.BlockSpec((1, 128), lambda i: (0, 0))]
        first_shapes = [row_f32, row_bf16, jax.ShapeDtypeStruct((1, D), F32), jax.ShapeDtypeStruct((1, 128), F32)]
    return _call(
        body, name=name, grid=(T // tm,),
        in_specs=[row, vec, HBM_SPEC] + ([] if head is None else [vec, row]),
        out_specs=first_specs + [row, hid, hid, hid],
        out_shape=first_shapes + [row_bf16] + [jax.ShapeDtypeStruct((T, F), BF16)] * 3,
        scratch_shapes=[pltpu.VMEM((3, F, D), BF16), pltpu.SemaphoreType.DMA((3,))],
        vmem_mib=60, args=(x, g, w3) + (() if head is None else tuple(head)), rider=rider)


def _ffn_dgrad(dout, x, g, dau, dag, w3, tm, cf, name, rider=None):
    T, D = x.shape
    F = w3.shape[1]

    def body(do_ref, x_ref, g_ref, dau_ref, dag_ref, w_hbm, dx_ref, dgate_ref, dup_ref, dg_ref, wv, sem):
        _load_weights(w_hbm, wv, sem)

        @pl.when(pl.program_id(0) == 0)
        def _():
            dg_ref[...] = jnp.zeros_like(dg_ref)

        dob = (FFN_RES * do_ref[...]).astype(BF16)
        for lo in range(0, F, cf):
            dact = _dot(dob, wv[2, lo:lo + cf, :], NT)
            dup_ref[:, lo:lo + cf] = (dact * dau_ref[:, lo:lo + cf].astype(F32)).astype(BF16)
            dgate_ref[:, lo:lo + cf] = (dact * dag_ref[:, lo:lo + cf].astype(F32)).astype(BF16)
        dh = _dot(dgate_ref[...], wv[0], NN) + _dot(dup_ref[...], wv[1], NN)
        xhat, rstd = _rms_stats(x_ref[...])
        dx, dg = _rms_bwd(xhat, rstd, g_ref[...], dh)
        dx_ref[...] = do_ref[...] + dx
        dg_ref[...] += dg

    row = pl.BlockSpec((tm, D), lambda i: (i, 0))
    hid = pl.BlockSpec((tm, F), lambda i: (i, 0))
    vec = pl.BlockSpec((1, D), lambda i: (0, 0))
    return _call(
        body, name=name, grid=(T // tm,),
        in_specs=[row, row, vec, hid, hid, HBM_SPEC],
        out_specs=[row, hid, hid, vec],
        out_shape=[jax.ShapeDtypeStruct((T, D), F32), jax.ShapeDtypeStruct((T, F), BF16),
                   jax.ShapeDtypeStruct((T, F), BF16), jax.ShapeDtypeStruct((1, D), F32)],
        scratch_shapes=[pltpu.VMEM((3, F, D), BF16), pltpu.SemaphoreType.DMA((3,))],
        vmem_mib=60, args=(dout, x, g, dau, dag, w3), rider=rider)


def _wgrad(lhs, rhs, tm, tf, name, rider=None):
    T, F = lhs[0].shape
    D = rhs.shape[1]
    K = len(lhs)

    def body(*refs):
        lhs_refs, rhs_ref, dw_ref, accs = refs[:K], refs[K], refs[K + 1], refs[K + 2:]
        i = pl.program_id(1)

        @pl.when(i == 0)
        def _():
            for acc in accs:
                acc[...] = jnp.zeros_like(acc)

        rv = rhs_ref[...]
        for acc, lhs_ref in zip(accs, lhs_refs):
            acc[...] += _dot(lhs_ref[...], rv, TN)

        @pl.when(i == pl.num_programs(1) - 1)
        def _():
            for k, acc in enumerate(accs):
                dw_ref[k] = acc[...].astype(BF16)

    hid = pl.BlockSpec((tm, tf), lambda f, i: (i, f))
    return _call(
        body, name=name, grid=(F // tf, T // tm),
        in_specs=[hid] * K + [pl.BlockSpec((tm, D), lambda f, i: (i, 0))],
        out_specs=[pl.BlockSpec((K, tf, D), lambda f, i: (0, f, 0))],
        out_shape=[jax.ShapeDtypeStruct((K, F, D), BF16)],
        scratch_shapes=[pltpu.VMEM((tf, D), F32)] * K,
        vmem_mib=56, args=(*lhs, rhs), rider=rider)


def _lru_gates(xr, bda_ref, bdx_ref, vec_ref):
    xrb = xr.astype(BF16)
    r = _sigmoid(_dot(xrb, bda_ref[...], NN) + vec_ref[V_BA:V_BA + 1, :])
    ig = _sigmoid(_dot(xrb, bdx_ref[...], NN) + vec_ref[V_BX:V_BX + 1, :])
    sp = _softplus_neg(vec_ref[V_LAM:V_LAM + 1, :])
    log_a = (-LRU_C * sp) * r
    a = jnp.exp(log_a)
    mult = jnp.sqrt(_neg_expm1(2.0 * log_a))
    return xrb, r, ig, sp, a, mult


def _layernorm_stats(u1):
    xc = u1 - jnp.mean(u1, axis=-1, keepdims=True)
    rs = lax.rsqrt(jnp.mean(xc * xc, axis=-1, keepdims=True) + LN_EPS)
    return xc * rs, rs


def _mix_core_fwd(x1, g, w_in_t, w_out, bda, bdx, cw, lw, vec, tm, name, rider=None):
    T, D = x1.shape
    W = cw.shape[1]
    assert tm >= CONV_HALO and w_in_t.shape[0] == 4 * W

    def body(x1_ref, g_ref, wi_ref, wo_ref, bda_ref, bdx_ref, cw_ref, lw_ref, vec_ref,
             x2_ref, z_ref, mix_ref, u1_ref, xr_ref, hst_ref, ubuf, rbuf, hc):
        @pl.when(pl.program_id(0) == 0)
        def _():
            ubuf[0:CONV_HALO, :] = jnp.zeros((CONV_HALO, W), F32)
            rbuf[0:LRU_HALO, :] = jnp.zeros((LRU_HALO, W), F32)
            hc[...] = jnp.zeros_like(hc)

        xhat, _ = _rms_stats(x1_ref[...])
        z_ref[...] = _dot((xhat * g_ref[...]).astype(BF16), wi_ref[...], NT)

        ubuf[CONV_HALO:CONV_HALO + tm, :] = z_ref[:, 0:W] * _sigmoid(z_ref[:, W:2 * W])
        u1 = jnp.zeros((tm, W), F32) + vec_ref[V_CB:V_CB + 1, :]
        base = CONV_HALO - (CONV_K - 1)
        for off, win in _row_windows(ubuf, tm, range(base, base + CONV_K)):
            u1 = u1 + cw_ref[off - base:off - base + 1, :] * win
        ubuf[0:CONV_HALO, :] = ubuf[tm:tm + CONV_HALO, :]
        u1_ref[...] = u1
        xh, _ = _layernorm_stats(u1)
        u2 = xh * vec_ref[V_LNG:V_LNG + 1, :] + vec_ref[V_LNB:V_LNB + 1, :]
        ub = (u2 * _sigmoid(u2)).astype(BF16)
        mix_ref[:, 0:W] = ub

        rbuf[LRU_HALO:LRU_HALO + tm, :] = z_ref[:, 2 * W:3 * W]
        xr = jnp.zeros((tm, W), F32) + vec_ref[V_LCB:V_LCB + 1, :]
        for k in range(LRU_K):
            off = LRU_HALO - (LRU_K - 1) + k
            xr = xr + lw_ref[k:k + 1, :] * rbuf[off:off + tm, :]
        rbuf[0:LRU_HALO, :] = rbuf[tm:tm + LRU_HALO, :]
        xr_ref[...] = xr
        _, _, ig, _, a, mult = _lru_gates(xr, bda_ref, bdx_ref, vec_ref)
        hc[0:1, :] = _scan_rows(a, mult * (ig * xr), hc[0:1, :], hst_ref)
        gl, _ = _gelu_parts(z_ref[:, 3 * W:4 * W])
        yb = (hst_ref[...] * gl).astype(BF16)
        mix_ref[:, W:2 * W] = yb

        x2_ref[...] = x1_ref[...] + _dot(ub, wo_ref[0:W, :], NN) + _dot(yb, wo_ref[W:2 * W, :], NN)

    full = lambda a: pl.BlockSpec(a.shape, lambda i: (0,) * a.ndim)
    tile = lambda n: pl.BlockSpec((tm, n), lambda i: (i, 0))
    return _call(
        body, name=name, grid=(T // tm,),
        in_specs=[tile(D), full(g), full(w_in_t), full(w_out), full(bda), full(bdx), full(cw), full(lw), full(vec)],
        out_specs=[tile(D), tile(4 * W), tile(2 * W), tile(W), tile(W), tile(W)],
        out_shape=[jax.ShapeDtypeStruct((T, D), F32), jax.ShapeDtypeStruct((T, 4 * W), F32),
                   jax.ShapeDtypeStruct((T, 2 * W), BF16), jax.ShapeDtypeStruct((T, W), F32),
                   jax.ShapeDtypeStruct((T, W), F32), jax.ShapeDtypeStruct((T, W), F32)],
        scratch_shapes=[pltpu.VMEM((tm + CONV_HALO, W), F32), pltpu.VMEM((tm + LRU_HALO, W), F32),
                        pltpu.VMEM((8, W), F32)],
        vmem_mib=56, args=(x1, g, w_in_t, w_out, bda, bdx, cw, lw, vec), rider=rider)


def _mix_bwd(dx2, z, u1, xr, hst, x1, mix, g, w_in_t, w_out, bda, bdx, cw, lw, vec, tm, name, rider=None):
    T, D = dx2.shape
    W = cw.shape[1]
    nt = T // tm
    assert tm >= CONV_HALO and tm % CONV_HALO == 0

    def body(dx_ref, z_ref, zh_ref, u1_ref, xr_ref, h_ref, hh_ref, wo_ref, bda_ref, bdx_ref, cw_ref, lw_ref, vec_ref,
             x1_ref, mix_ref, g_ref, wi_ref,
             dx1_ref, sg_ref, dbda_ref, dbdx_ref, dob_ref, dg_ref, dwi_ref, dwo_ref,
             u0buf, du1buf, rxbuf, dxrbuf, gbuf, gc, spacc, dz_ref, ai_ref, ao_ref):
        i = pl.program_id(0)
        first = i == nt - 1
        row = lax.broadcasted_iota(jnp.int32, (tm, W), 0)

        @pl.when(i == 0)
        def _():
            sg_ref[...] = jnp.zeros_like(sg_ref)
            dbda_ref[...] = jnp.zeros_like(dbda_ref)
            dbdx_ref[...] = jnp.zeros_like(dbdx_ref)
            du1buf[tm:tm + CONV_HALO, :] = jnp.zeros((CONV_HALO, W), F32)
            dxrbuf[tm:tm + LRU_HALO, :] = jnp.zeros((LRU_HALO, W), F32)
            gc[...] = jnp.zeros_like(gc)
            spacc[...] = jnp.zeros_like(spacc)
            dg_ref[...] = jnp.zeros_like(dg_ref)
            ai_ref[...] = jnp.zeros_like(ai_ref)
            ao_ref[...] = jnp.zeros_like(ao_ref)

        def accum(r, val):
            sg_ref[r:r + 1, :] += jnp.sum(val, axis=0, keepdims=True)

        x1hat, x1rstd = _rms_stats(x1_ref[...])
        gain = g_ref[...]
        hb = (x1hat * gain).astype(BF16)

        def in_proj_bwd(lo, hi):
            dzb = dz_ref[:, lo:hi]
            ai_ref[lo:hi, :] += _dot(dzb, hb, TN)
            return _dot(dzb, wi_ref[lo:hi, :], NN)

        dxb = dx_ref[...].astype(BF16)
        ao_ref[...] += _dot(mix_ref[...], dxb, TN)
        dmix = _dot(dxb, wo_ref[...], NT)
        d_u = dmix[:, 0:W]
        d_yr = dmix[:, W:2 * W]

        xh, rs = _layernorm_stats(u1_ref[...])
        ln_g = vec_ref[V_LNG:V_LNG + 1, :]
        u2 = xh * ln_g + vec_ref[V_LNB:V_LNB + 1, :]
        s2 = _sigmoid(u2)
        d_u2 = d_u * (s2 * (1.0 + u2 * (1.0 - s2)))
        accum(G_LNG, d_u2 * xh)
        accum(G_LNB, d_u2)
        d_xh = d_u2 * ln_g
        d_u1 = rs * (d_xh - jnp.mean(d_xh, axis=-1, keepdims=True)
                     - xh * jnp.mean(d_xh * xh, axis=-1, keepdims=True))
        accum(G_CB, d_u1)
        halo_on = jnp.where(first, 0.0, 1.0)
        u0buf[0:CONV_HALO, :] = halo_on * (zh_ref[:, 0:W] * _sigmoid(zh_ref[:, W:2 * W]))
        cv = z_ref[:, 0:W]
        sgc = _sigmoid(z_ref[:, W:2 * W])
        u0buf[CONV_HALO:CONV_HALO + tm, :] = cv * sgc
        du1buf[0:tm, :] = d_u1
        base = CONV_HALO - (CONV_K - 1)
        for off, win in _row_windows(u0buf, tm, range(base, base + CONV_K)):
            accum(G_CW + off - base, d_u1 * win)
        d_u0 = jnp.zeros((tm, W), F32)
        for off, win in _row_windows(du1buf, tm, range(0, CONV_K)):
            d_u0 = d_u0 + cw_ref[CONV_K - 1 - off:CONV_K - off, :] * win
        du1buf[tm:tm + CONV_HALO, :] = du1buf[0:CONV_HALO, :]
        dz_ref[:, 0:W] = (d_u0 * sgc).astype(BF16)
        dz_ref[:, W:2 * W] = (d_u0 * cv * (sgc * (1.0 - sgc))).astype(BF16)
        dh = in_proj_bwd(0, 2 * W)

        xrv = xr_ref[...]
        xrb, r, ig, sp, a, mult = _lru_gates(xrv, bda_ref, bdx_ref, vec_ref)
        h = h_ref[...]
        gl, dgl = _gelu_parts(z_ref[:, 3 * W:4 * W])
        dz_ref[:, 3 * W:4 * W] = (d_yr * h * dgl).astype(BF16)
        dh = dh + in_proj_bwd(3 * W, 4 * W)
        a_next = jnp.where(row == tm - 1, 1.0, pltpu.roll(a, tm - 1, 0))
        g_first = _scan_rows(a_next, d_yr * gl, gc[0:1, :], gbuf, reverse=True)
        g = gbuf[...]
        gc[0:1, :] = a[0:1, :] * g_first
        hprev = jnp.where(row == 0, halo_on * hh_ref[LRU_HALO - 1:LRU_HALO, :], pltpu.roll(h, 1, 0))
        d_log_a = (g * hprev) * a - (g * ig * xrv) * (a * a) / mult
        d_ig = g * mult * xrv
        d_xr = g * mult * ig
        spacc[0:1, :] += jnp.sum(d_log_a * r, axis=0, keepdims=True)
        d_pa32 = (d_log_a * (-LRU_C * sp)) * (r * (1.0 - r))
        d_px32 = d_ig * (ig * (1.0 - ig))
        accum(G_BA, d_pa32)
        accum(G_BX, d_px32)
        d_pa = d_pa32.astype(BF16)
        d_px = d_px32.astype(BF16)
        d_xr = d_xr + _dot(d_pa, bda_ref[...], NT) + _dot(d_px, bdx_ref[...], NT)
        dbda_ref[...] += _dot(xrb, d_pa, TN)
        dbdx_ref[...] += _dot(xrb, d_px, TN)
        accum(G_LCB, d_xr)
        rxbuf[0:LRU_HALO, :] = halo_on * zh_ref[CONV_HALO - LRU_HALO:CONV_HALO, 2 * W:3 * W]
        rxbuf[LRU_HALO:LRU_HALO + tm, :] = z_ref[:, 2 * W:3 * W]
        dxrbuf[0:tm, :] = d_xr
        d_rx = jnp.zeros((tm, W), F32)
        for k in range(LRU_K):
            off = LRU_HALO - (LRU_K - 1) + k
            accum(G_LW + k, d_xr * rxbuf[off:off + tm, :])
            d_rx = d_rx + lw_ref[k:k + 1, :] * dxrbuf[LRU_K - 1 - k:LRU_K - 1 - k + tm, :]
        dxrbuf[tm:tm + LRU_HALO, :] = dxrbuf[0:LRU_HALO, :]
        dz_ref[:, 2 * W:3 * W] = d_rx.astype(BF16)
        dh = dh + in_proj_bwd(2 * W, 3 * W)

        dx, dg = _rms_bwd(x1hat, x1rstd, gain, dh)
        dx1 = dx_ref[...] + dx
        dx1_ref[...] = dx1
        dob_ref[...] = (FFN_RES * dx1).astype(BF16)
        dg_ref[...] += dg

        @pl.when(first)
        def _():
            lam = vec_ref[V_LAM:V_LAM + 1, :]
            sg_ref[G_LAM:G_LAM + 1, :] = LRU_C * _sigmoid(-lam) * spacc[0:1, :]
            dwi_ref[...] = ai_ref[...].astype(BF16)
            dwo_ref[...] = ao_ref[...].astype(BF16)

    full = lambda a: pl.BlockSpec(a.shape, lambda i: (0,) * a.ndim)
    tile = lambda n: pl.BlockSpec((tm, n), lambda i: (nt - 1 - i, 0))
    halo = lambda rows, n: pl.BlockSpec(
        (rows, n), lambda i: (jnp.maximum((nt - 1 - i) * (tm // rows) - 1, 0), 0))
    const = lambda r, c: pl.BlockSpec((r, c), lambda i: (0, 0))
    return _call(
        body, name=name, grid=(nt,),
        in_specs=[tile(D), tile(4 * W), halo(CONV_HALO, 4 * W), tile(W), tile(W), tile(W), halo(LRU_HALO, W),
                  full(w_out), full(bda), full(bdx), full(cw), full(lw), full(vec),
                  tile(D), tile(2 * W), full(g), full(w_in_t)],
        out_specs=[tile(D), const(G_ROWS, W), const(W, W), const(W, W),
                   tile(D), const(1, D), const(4 * W, D), const(2 * W, D)],
        out_shape=[jax.ShapeDtypeStruct((T, D), F32), jax.ShapeDtypeStruct((G_ROWS, W), F32),
                   jax.ShapeDtypeStruct((W, W), F32), jax.ShapeDtypeStruct((W, W), F32),
                   jax.ShapeDtypeStruct((T, D), BF16), jax.ShapeDtypeStruct((1, D), F32),
                   jax.ShapeDtypeStruct((4 * W, D), BF16), jax.ShapeDtypeStruct((2 * W, D), BF16)],
        scratch_shapes=[pltpu.VMEM((tm + CONV_HALO, W), F32), pltpu.VMEM((tm + CONV_HALO, W), F32),
                        pltpu.VMEM((tm + LRU_HALO, W), F32), pltpu.VMEM((tm + LRU_HALO, W), F32),
                        pltpu.VMEM((tm, W), F32), pltpu.VMEM((8, W), F32), pltpu.VMEM((8, W), F32),
                        pltpu.VMEM((tm, 4 * W), BF16), pltpu.VMEM((4 * W, D), F32), pltpu.VMEM((2 * W, D), F32)],
        vmem_mib=60, args=(dx2, z, z, u1, xr, hst, hst, w_out, bda, bdx, cw, lw, vec, x1, mix, g, w_in_t),
        rider=rider)


def _pair_add(full, recv, name):
    K, _, _, rows, D = full.shape

    def body(c_ref, a_ref, b_ref, o_ref):
        o_ref[...] = (a_ref[...].astype(F32) + b_ref[...].astype(F32)).astype(BF16)

    c = lax.axis_index("c").astype(jnp.int32).reshape((1,))
    return _call(
        body, name=name, grid=(K, N_CHIP), num_scalar_prefetch=1,
        in_specs=[pl.BlockSpec((None, None, None, rows, D), lambda k, q, c_ref: (k, q, c_ref[0], 0, 0)),
                  pl.BlockSpec((None, None, rows, D), lambda k, q, c_ref: (k, q, 0, 0))],
        out_specs=pl.BlockSpec((None, None, rows, D), lambda k, q, c_ref: (k, q, 0, 0)),
        out_shape=jax.ShapeDtypeStruct(recv.shape, BF16),
        scratch_shapes=[], vmem_mib=16, args=(c, full, recv))


def _adamw_update(wv, gv, mv, vv):
    m2 = ADAM_B1 * mv + (1.0 - ADAM_B1) * gv
    v2 = ADAM_B2 * vv + (1.0 - ADAM_B2) * (gv * gv)
    m_hat = m2 / (1.0 - ADAM_B1 ** ADAM_STEP)
    v_hat = v2 / (1.0 - ADAM_B2 ** ADAM_STEP)
    return -ADAM_LR * (m_hat / (jnp.sqrt(v_hat) + ADAM_EPS) + ADAM_WD * wv), m2, v2


def _finish(parts, k, w, m, v, transpose, name):
    _, n_parts, rows, D = parts.shape

    def body(p_ref, w_ref, m_ref, v_ref, g_ref, d_ref, mo_ref, vo_ref):
        acc = p_ref[0].astype(F32)
        for q in range(1, n_parts):
            acc = acc + p_ref[q].astype(F32)
        gv = acc.T if transpose else acc
        g_ref[...] = gv
        d_ref[...], mo_ref[...], vo_ref[...] = _adamw_update(w_ref[...], gv, m_ref[...], v_ref[...])

    whole = pl.BlockSpec(w.shape, lambda i: (0, 0))
    return _call(
        body, name=name, grid=(1,),
        in_specs=[pl.BlockSpec((None, n_parts, rows, D), lambda i: (k, 0, 0, 0)), whole, whole, whole],
        out_specs=[whole] * 4, out_shape=[pltpu.HBM(w.shape, F32)] * 4,
        scratch_shapes=[], vmem_mib=40, args=(parts, w, m, v))


def _finish_many(items, name, rider=None):
    n_items = len(items)
    D = items[0][2].shape[1]
    max_rows = max(w.shape[0] for _, _, w, _, _ in items)
    max_parts = max(parts.shape[1] for parts, _, _, _, _ in items)
    n_loads, n_stores = 4, 4

    def body(*refs):
        ins, outs = refs[:4 * n_items], refs[4 * n_items:8 * n_items]
        pbuf, wbuf, obuf, sem = refs[8 * n_items:]
        step = pl.program_id(0)
        for j, (parts, k, w, _, _) in enumerate(items):
            rows, n_parts = w.shape[0], parts.shape[1]

            @pl.when(step == j)
            def _():
                p_ref, w_ref, m_ref, v_ref = ins[4 * j:4 * j + 4]
                loads = [pltpu.make_async_copy(p_ref.at[k], pbuf.at[0:n_parts, 0:rows], sem.at[0])]
                loads += [pltpu.make_async_copy(src, wbuf.at[r, 0:rows], sem.at[1 + r])
                          for r, src in enumerate((w_ref, m_ref, v_ref))]
                for cp in loads:
                    cp.start()
                for cp in loads:
                    cp.wait()
                acc = pbuf[0, 0:rows, :].astype(F32)
                for q in range(1, n_parts):
                    acc = acc + pbuf[q, 0:rows, :].astype(F32)
                obuf[0, 0:rows, :] = acc
                obuf[1, 0:rows, :], obuf[2, 0:rows, :], obuf[3, 0:rows, :] = _adamw_update(
                    wbuf[0, 0:rows, :], acc, wbuf[1, 0:rows, :], wbuf[2, 0:rows, :])
                stores = [pltpu.make_async_copy(obuf.at[r, 0:rows], outs[4 * j + r], sem.at[n_loads + r])
                          for r in range(n_stores)]
                for cp in stores:
                    cp.start()
                for cp in stores:
                    cp.wait()

    args = [a for parts, _, w, m, v in items for a in (parts, w, m, v)]
    shapes = [jax.ShapeDtypeStruct(w.shape, F32) for _, _, w, _, _ in items for _ in range(4)]
    return _call(
        body, name=name, grid=(n_items,),
        in_specs=[HBM_SPEC] * len(args), out_specs=[HBM_SPEC] * len(shapes), out_shape=shapes,
        scratch_shapes=[pltpu.VMEM((max_parts, max_rows, D), BF16), pltpu.VMEM((3, max_rows, D), F32),
                        pltpu.VMEM((4, max_rows, D), F32), pltpu.SemaphoreType.DMA((n_loads + n_stores,))],
        vmem_mib=40, args=args, rider=rider)


def _adamw_each(ws, gs, ms, vs, name):
    n = len(ws)

    def body(*refs):
        w_refs, g_refs, m_refs, v_refs, outs = refs[:n], refs[n:2 * n], refs[2 * n:3 * n], refs[3 * n:4 * n], refs[4 * n:]
        for k in range(n):
            outs[k][...], outs[n + k][...], outs[2 * n + k][...] = _adamw_update(
                w_refs[k][...], g_refs[k][...], m_refs[k][...], v_refs[k][...])

    shapes = [jax.ShapeDtypeStruct(w.shape, F32) for w in ws]
    return pl.pallas_call(
        body, name=name,
        in_specs=[VMEM_SPEC] * (4 * n), out_specs=[VMEM_SPEC] * (3 * n), out_shape=shapes * 3,
        compiler_params=pltpu.CompilerParams(vmem_limit_bytes=32 * MIB),
    )(*ws, *gs, *ms, *vs)


def _block_diag(w):
    h, d, _ = w.shape
    onto = jnp.eye(h, dtype=w.dtype)
    return (w[:, :, None, :] * onto[:, None, :, None]).reshape(h * d, h * d)


def _diag_blocks(m, h):
    d = m.shape[0] // h
    onto = jnp.eye(h, dtype=m.dtype)
    return (m.reshape(h, d, h, d) * onto[:, None, :, None]).sum(axis=2)


def _reduce_level1(full, tag):
    got = _run_comm(_sibling_comm(full), "rs_sibling_" + tag)
    return [_pair_add(a, b, "rs_pair_add_%s%d" % (tag, n)) for n, (a, b) in enumerate(zip(full, got))]


def kernel(x, ffn1_norm, ffn1_w_gate, ffn1_w_up, ffn1_w_down, mix_norm, w_in, conv_dw, conv_dw_bias, conv_ln_g, conv_ln_b, lru_conv_w, lru_conv_b, lru_w_a, lru_b_a, lru_w_x, lru_b_x, lru_lambda, w_out, ffn2_norm, ffn2_w_gate, ffn2_w_up, ffn2_w_down, final_norm, loss_target, m_ffn1_norm, m_ffn1_w_gate, m_ffn1_w_up, m_ffn1_w_down, m_mix_norm, m_w_in, m_conv_dw, m_conv_dw_bias, m_conv_ln_g, m_conv_ln_b, m_lru_conv_w, m_lru_conv_b, m_lru_w_a, m_lru_b_a, m_lru_w_x, m_lru_b_x, m_lru_lambda, m_w_out, m_ffn2_norm, m_ffn2_w_gate, m_ffn2_w_up, m_ffn2_w_down, m_final_norm, v_ffn1_norm, v_ffn1_w_gate, v_ffn1_w_up, v_ffn1_w_down, v_mix_norm, v_w_in, v_conv_dw, v_conv_dw_bias, v_conv_ln_g, v_conv_ln_b, v_lru_conv_w, v_lru_conv_b, v_lru_w_a, v_lru_b_a, v_lru_w_x, v_lru_b_x, v_lru_lambda, v_w_out, v_ffn2_norm, v_ffn2_w_gate, v_ffn2_w_up, v_ffn2_w_down, v_final_norm):
    T, D = x.shape[1], x.shape[2]
    F = ffn1_w_down.shape[0] * N_DEV
    rf = ffn1_w_down.shape[0]
    ri = w_in.shape[1]
    ro = w_out.shape[0]
    W = conv_dw_bias.shape[0]
    wc = conv_dw.shape[1]
    H = lru_w_a.shape[0]
    xs = x.reshape(T, D)
    tgt = loss_target.reshape(T, D)
    tm_ffn = min(512, T)
    tm_fwd = min(512, T)
    cf = 256
    tm_w = min(1024, T)
    tm_w1 = min(2048, T)
    tm_mix = min(256, T)
    tf_w = F // 2
    row = lambda v: v.reshape(1, -1)
    by_owner = lambda a, rows: a.reshape(a.shape[0], N_CHIP, 2, rows, D)

    p3a, p3b, p_in, p_out = _prep_weights(
        (ffn1_w_gate.T, ffn1_w_up.T, ffn1_w_down), (ffn2_w_gate.T, ffn2_w_up.T, ffn2_w_down), w_in, w_out,
        "prep_weights")
    tile_rows = lambda a: jnp.pad(a, ((0, -a.shape[0] % SUBLANES), (0, 0)))
    p_cw = jnp.concatenate([tile_rows(conv_dw), tile_rows(lru_conv_w)], axis=0)
    lw_row = p_cw.shape[0] - SUBLANES
    stacked = lambda r, j: r.at[:, j]
    plain = lambda r, j: r.at[j]
    g3_shape = jax.ShapeDtypeStruct((3, N_DEV, rf, D), BF16)
    g3a = _all_gather(p3a, "ag_ffn1")
    w3a = g3a.reshape(3, F, D)
    bda = _block_diag(lru_w_a).astype(BF16)
    bdx = _block_diag(lru_w_x).astype(BF16)
    vec = jnp.concatenate([tile_rows(v[None]) for v in
                           (conv_dw_bias, conv_ln_g, conv_ln_b, lru_conv_b, lru_b_a, lru_b_x, lru_lambda)], axis=0)

    gather_rest = _gather_comm(
        [p3b, p_in, p_out, p_cw], [stacked, plain, plain, plain],
        [g3_shape, jax.ShapeDtypeStruct((N_DEV, ri, D), BF16), jax.ShapeDtypeStruct((N_DEV, ro, D), BF16),
         jax.ShapeDtypeStruct((N_DEV,) + p_cw.shape, F32)],
        [(SIBLING,) + SAME_CORE, EVERYONE, EVERYONE, EVERYONE])
    (x1, h1, dau1, dag1, act1), (g3b_half, g_in, g_out, g_cw) = _ffn_fwd(
        xs, row(ffn1_norm), w3a, tm_fwd, cf, "ffn1_fwd", rider=gather_rest)
    w_in_t = g_in.reshape(N_DEV * ri, D)
    w_out_f = g_out.reshape(N_DEV * ro, D)
    cw_all = jnp.transpose(g_cw, (1, 0, 2)).reshape(p_cw.shape[0], N_DEV * wc)
    cw = cw_all[0:CONV_K]
    lw = cw_all[lw_row:lw_row + LRU_K]
    (x2, z, mix, u1, xr, hst), (g3b,) = _mix_core_fwd(
        x1, row(mix_norm), w_in_t, w_out_f, bda, bdx, cw, lw, vec, min(512, T), "mix_core_fwd",
        rider=_forward_comm([g3b_half], [stacked]))
    w3b = g3b.reshape(3, F, D)
    dx3, dob2, d_final_norm, loss_part, h3, dau2, dag2, act2 = _ffn_fwd(
        x2, row(ffn2_norm), w3b, tm_fwd, cf, "ffn2_fwd_loss", head=(row(final_norm), tgt))

    dx2, dgate2, dup2, d_ffn2_norm = _ffn_dgrad(dx3, x2, row(ffn2_norm), dau2, dag2, w3b, tm_ffn, cf, "ffn2_dgrad")
    (dw_gu2,) = _wgrad([dgate2, dup2], h3, tm_w, tf_w, "ffn2_wgrad_gu")
    (dw_d2,) = _wgrad([act2], dob2, tm_w1, tf_w, "ffn2_wgrad_d")
    by_device = lambda a: a.reshape(a.shape[0], N_DEV, rf, D)
    (dx1, sg, dbda, dbdx, dob1, d_mix_norm, dw_in_t, dw_out), parts_f2 = _mix_bwd(
        dx2, z, u1, xr, hst, x1, mix, row(mix_norm), w_in_t, w_out_f, bda, bdx, cw, lw, vec, tm_mix, "mix_bwd",
        rider=_chips_comm([by_device(dw_gu2), by_device(dw_d2)], every_device=True))
    io = [dw_in_t.reshape(1, N_DEV, ri, D), dw_out.reshape(1, N_DEV, ro, D)]
    (dw_d1,), parts_io = _wgrad(
        [act1], dob1, tm_w1, tf_w, "ffn1_wgrad_d", rider=_chips_comm(io, every_device=True))
    sums_d1 = _reduce_level1([by_owner(dw_d1, rf)], "d1")
    dx0, dgate1, dup1, d_ffn1_norm = _ffn_dgrad(dx1, xs, row(ffn1_norm), dau1, dag1, w3a, tm_ffn, cf, "ffn1_dgrad")
    (dw_g1,), parts_d1 = _wgrad([dgate1], h1, tm_w1, tf_w, "ffn1_wgrad_g", rider=_chips_comm(sums_d1))
    sums_g1 = _reduce_level1([by_owner(dw_g1, rf)], "g1")
    small = [d_ffn1_norm, d_mix_norm, d_ffn2_norm, d_final_norm, sg, _diag_blocks(dbda, H).reshape(-1, D),
             _diag_blocks(dbdx, H).reshape(-1, D), loss_part]
    (dw_u1,), summed_and_parts = _wgrad(
        [dup1], h1, tm_w1, tf_w, "ffn1_wgrad_u", rider=_both(_small_sum_comm(small), _chips_comm(sums_g1)))
    summed, parts_g1 = summed_and_parts[:len(small)], summed_and_parts[len(small):]
    sums_u1 = _reduce_level1([by_owner(dw_u1, rf)], "u1")

    g_norm1, g_norm_mix, g_norm2, g_norm_final, g_sg, g_w_a, g_w_x, g_loss = summed
    loss = g_loss[0, 0]
    me = 4 * lax.axis_index("x") + 2 * lax.axis_index("y") + lax.axis_index("c")
    chan = lambda full_g: lax.dynamic_slice_in_dim(full_g, me * wc, wc, axis=1)
    grads = {
        "ffn1_norm": g_norm1.reshape(D), "mix_norm": g_norm_mix.reshape(D), "ffn2_norm": g_norm2.reshape(D),
        "final_norm": g_norm_final.reshape(D),
        "conv_dw_bias": g_sg[G_CB], "conv_ln_g": g_sg[G_LNG], "conv_ln_b": g_sg[G_LNB],
        "lru_conv_b": g_sg[G_LCB], "lru_b_a": g_sg[G_BA], "lru_b_x": g_sg[G_BX], "lru_lambda": g_sg[G_LAM],
        "lru_w_a": g_w_a.reshape(lru_w_a.shape), "lru_w_x": g_w_x.reshape(lru_w_x.shape),
        "conv_dw": chan(g_sg[G_CW:G_CW + CONV_K]), "lru_conv_w": chan(g_sg[G_LW:G_LW + LRU_K]),
    }

    weights = dict(ffn1_norm=ffn1_norm, ffn1_w_gate=ffn1_w_gate, ffn1_w_up=ffn1_w_up, ffn1_w_down=ffn1_w_down, mix_norm=mix_norm, w_in=w_in, conv_dw=conv_dw, conv_dw_bias=conv_dw_bias, conv_ln_g=conv_ln_g, conv_ln_b=conv_ln_b, lru_conv_w=lru_conv_w, lru_conv_b=lru_conv_b, lru_w_a=lru_w_a, lru_b_a=lru_b_a, lru_w_x=lru_w_x, lru_b_x=lru_b_x, lru_lambda=lru_lambda, w_out=w_out, ffn2_norm=ffn2_norm, ffn2_w_gate=ffn2_w_gate, ffn2_w_up=ffn2_w_up, ffn2_w_down=ffn2_w_down, final_norm=final_norm)
    moment1 = dict(ffn1_norm=m_ffn1_norm, ffn1_w_gate=m_ffn1_w_gate, ffn1_w_up=m_ffn1_w_up, ffn1_w_down=m_ffn1_w_down, mix_norm=m_mix_norm, w_in=m_w_in, conv_dw=m_conv_dw, conv_dw_bias=m_conv_dw_bias, conv_ln_g=m_conv_ln_g, conv_ln_b=m_conv_ln_b, lru_conv_w=m_lru_conv_w, lru_conv_b=m_lru_conv_b, lru_w_a=m_lru_w_a, lru_b_a=m_lru_b_a, lru_w_x=m_lru_w_x, lru_b_x=m_lru_b_x, lru_lambda=m_lru_lambda, w_out=m_w_out, ffn2_norm=m_ffn2_norm, ffn2_w_gate=m_ffn2_w_gate, ffn2_w_up=m_ffn2_w_up, ffn2_w_down=m_ffn2_w_down, final_norm=m_final_norm)
    moment2 = dict(ffn1_norm=v_ffn1_norm, ffn1_w_gate=v_ffn1_w_gate, ffn1_w_up=v_ffn1_w_up, ffn1_w_down=v_ffn1_w_down, mix_norm=v_mix_norm, w_in=v_w_in, conv_dw=v_conv_dw, conv_dw_bias=v_conv_dw_bias, conv_ln_g=v_conv_ln_g, conv_ln_b=v_conv_ln_b, lru_conv_w=v_lru_conv_w, lru_conv_b=v_lru_conv_b, lru_w_a=v_lru_w_a, lru_b_a=v_lru_b_a, lru_w_x=v_lru_w_x, lru_b_x=v_lru_b_x, lru_lambda=v_lru_lambda, w_out=v_w_out, ffn2_norm=v_ffn2_norm, ffn2_w_gate=v_ffn2_w_gate, ffn2_w_up=v_ffn2_w_up, ffn2_w_down=v_ffn2_w_down, final_norm=v_final_norm)
    order = list(weights)
    gate_up = {"ffn1_w_gate", "ffn1_w_up", "ffn2_w_gate", "ffn2_w_up"}
    view = lambda n, a: a.T if n in gate_up else a
    operands = lambda n: [view(n, d[n]) for d in (weights, moment1, moment2)]
    delta, new_m, new_v = {}, {}, {}
    six = {"ffn1_w_gate": (parts_g1[0], 0), "ffn1_w_down": (parts_d1[0], 0), "w_out": (parts_io[1], 0),
           "ffn2_w_gate": (parts_f2[0], 0), "ffn2_w_up": (parts_f2[0], 1), "ffn2_w_down": (parts_f2[1], 0)}
    results, parts_u1 = _finish_many([(parts, k, *operands(n)) for n, (parts, k) in six.items()], "finish_six",
                                     rider=_chips_comm(sums_u1))
    for j, n in enumerate(six):
        grads[n], delta[n], new_m[n], new_v[n] = [view(n, r) for r in results[4 * j:4 * j + 4]]
    for n, parts, d_major in (("ffn1_w_up", parts_u1[0], False), ("w_in", parts_io[0], True)):
        grads[n], delta[n], new_m[n], new_v[n] = [
            view(n, r) for r in _finish(parts, 0, *operands(n), d_major, "finish_" + n)]
    big = set(six) | {"ffn1_w_up", "w_in"}
    rest = [n for n in order if n not in big]
    updates = _adamw_each([weights[n] for n in rest], [grads[n] for n in rest], [moment1[n] for n in rest],
                          [moment2[n] for n in rest], "adamw_small")
    for k, n in enumerate(rest):
        delta[n], new_m[n], new_v[n] = updates[k], updates[len(rest) + k], updates[2 * len(rest) + k]

    return (loss, dx0.reshape(x.shape), *[grads[n] for n in order], *[delta[n] for n in order],
            *[new_m[n] for n in order], *[new_v[n] for n in order])
```

```python
import functools
import math

import jax
import jax.numpy as jnp
from jax import lax
from jax.experimental import pallas as pl
from jax.experimental.pallas import tpu as pltpu

F32 = jnp.float32
BF16 = jnp.bfloat16
MESH = pl.DeviceIdType.MESH

N_DEV = 8
N_CHIP = 4
SUBLANES = 8
RMS_EPS = 1e-6
LN_EPS = 1e-5
LRU_C = 8.0
CONV_K = 31
LRU_K = 4
CONV_HALO = 32
LRU_HALO = 8
FFN_RES = 0.5
ADAM_LR, ADAM_B1, ADAM_B2, ADAM_EPS, ADAM_WD, ADAM_STEP = 0.001, 0.9, 0.999, 1e-08, 0.01, 10
GELU_K = math.sqrt(2.0 / math.pi)
GELU_C = 0.044715

MIB = 1024 * 1024
NT = (((1,), (1,)), ((), ()))
NN = (((1,), (0,)), ((), ()))
TN = (((0,), (0,)), ((), ()))

V_CB, V_LNG, V_LNB, V_LCB, V_BA, V_BX, V_LAM = range(0, 7 * SUBLANES, SUBLANES)
G_CW = 0
G_CB, G_LNG, G_LNB = 31, 32, 33
G_LW = 34
G_LCB, G_BA, G_BX, G_LAM = 38, 39, 40, 41
G_ROWS = 48

HBM_SPEC = pl.BlockSpec(memory_space=pltpu.HBM)
VMEM_SPEC = pl.BlockSpec(memory_space=pltpu.VMEM)


def _dot(a, b, dims):
    return lax.dot_general(a, b, dims, preferred_element_type=F32)


def _sigmoid(x):
    return 1.0 / (1.0 + jnp.exp(-x))


def _gelu_parts(x):
    x2 = x * x
    th = jnp.tanh(GELU_K * x * (1.0 + GELU_C * x2))
    gl = 0.5 * x * (1.0 + th)
    dgl = 0.5 * (1.0 + th) + 0.5 * x * (1.0 - th * th) * GELU_K * (1.0 + 3.0 * GELU_C * x2)
    return gl, dgl


def _neg_expm1(y):
    series = -y * (1.0 + y * (1.0 / 2) * (1.0 + y * (1.0 / 3) * (1.0 + y * (1.0 / 4) * (1.0 + y * (1.0 / 5) * (1.0 + y * (1.0 / 6))))))
    return jnp.where(y > -0.25, series, 1.0 - jnp.exp(y))


def _softplus_neg(lam):
    t = -lam
    e = jnp.exp(-jnp.abs(t))
    s = 1.0 + e
    log1p_e = jnp.log(s) - ((s - 1.0) - e) / s
    return jnp.maximum(t, 0.0) + log1p_e


def _rms_stats(xv):
    rstd = lax.rsqrt(jnp.mean(xv * xv, axis=-1, keepdims=True) + RMS_EPS)
    return xv * rstd, rstd


def _rms_bwd(xhat, rstd, g, dh):
    dxhat = dh * g
    dx = rstd * (dxhat - xhat * jnp.mean(dxhat * xhat, axis=-1, keepdims=True))
    return dx, jnp.sum(dh * xhat, axis=0, keepdims=True)


def _row_windows(buf_ref, n_rows, offsets):
    total = buf_ref.shape[0]
    full = buf_ref[...]
    for b in range(SUBLANES):
        offs = [o for o in offsets if o % SUBLANES == b]
        if not offs:
            continue
        assert max(offs) + n_rows <= total
        moved = full if b == 0 else pltpu.roll(full, total - b, 0)
        for o in offs:
            yield o, moved[o - b:o - b + n_rows, :]


def _scan_rows(av, bv, edge, out_ref, reverse=False):
    tm, W = av.shape
    sub = lax.broadcasted_iota(jnp.int32, (tm, W), 0) % SUBLANES
    s = 1
    while s < SUBLANES:
        keep = (sub < SUBLANES - s) if reverse else (sub >= s)
        shift = tm - s if reverse else s
        bv = jnp.where(keep, av * pltpu.roll(bv, shift, 0) + bv, bv)
        av = jnp.where(keep, av * pltpu.roll(av, shift, 0), av)
        s *= 2
    starts = range(0, tm, SUBLANES)
    for r0 in (reversed(starts) if reverse else starts):
        group = av[r0:r0 + SUBLANES, :] * edge + bv[r0:r0 + SUBLANES, :]
        out_ref[r0:r0 + SUBLANES, :] = group
        edge = group[0:1, :] if reverse else group[SUBLANES - 1:SUBLANES, :]
    return edge


class _Comm:
    def __init__(self, arrays, in_specs, out_shapes, out_specs, scratch, start, wait, aliases=None):
        self.arrays, self.in_specs = list(arrays), list(in_specs)
        self.out_shapes, self.out_specs = list(out_shapes), list(out_specs)
        self.scratch, self.start, self.wait = list(scratch), start, wait
        self.aliases = dict(aliases or {})


def _in_hbm(a):
    return pltpu.with_memory_space_constraint(a, pltpu.HBM)


def _operands(comm):
    return [a if spec is VMEM_SPEC else _in_hbm(a) for a, spec in zip(comm.arrays, comm.in_specs)]


def _call(body, *, name, grid, in_specs, out_specs, out_shape, scratch_shapes, vmem_mib, args, rider=None,
          num_scalar_prefetch=0):
    params = pltpu.CompilerParams(dimension_semantics=("arbitrary",) * len(grid), vmem_limit_bytes=vmem_mib * MIB)
    args = [a if k < num_scalar_prefetch else _in_hbm(a) for k, a in enumerate(args)]
    if rider is None:
        return pl.pallas_call(
            body, name=name,
            grid_spec=pltpu.PrefetchScalarGridSpec(
                num_scalar_prefetch=num_scalar_prefetch, grid=grid, in_specs=in_specs, out_specs=out_specs,
                scratch_shapes=scratch_shapes),
            out_shape=out_shape, compiler_params=params)(*args)
    assert num_scalar_prefetch == 0
    n_in, n_out, n_scr = len(in_specs), len(out_specs), len(scratch_shapes)
    r_in, r_out = len(rider.arrays), len(rider.out_shapes)
    n_axes = len(grid)

    def carried(*refs):
        pos = [0]

        def take(n):
            pos[0] += n
            return refs[pos[0] - n:pos[0]]

        ins, r_ins, outs, r_outs, scr, r_scr = take(n_in), take(r_in), take(n_out), take(r_out), take(n_scr), take(len(rider.scratch))
        first = pl.program_id(0) == 0
        last = pl.program_id(0) == grid[0] - 1
        for ax in range(1, n_axes):
            first = first & (pl.program_id(ax) == 0)
            last = last & (pl.program_id(ax) == grid[ax] - 1)

        @pl.when(first)
        def _():
            rider.start(r_ins, r_outs, r_scr)

        body(*ins, *outs, *scr)

        @pl.when(last)
        def _():
            rider.wait(r_ins, r_outs, r_scr)

    res = pl.pallas_call(
        carried, name=name,
        grid=grid,
        in_specs=list(in_specs) + rider.in_specs,
        out_specs=list(out_specs) + rider.out_specs,
        out_shape=list(out_shape) + rider.out_shapes,
        scratch_shapes=list(scratch_shapes) + rider.scratch,
        input_output_aliases={n_in + i: n_out + o for i, o in rider.aliases.items()},
        compiler_params=params)(*args, *_operands(rider))
    return res[:n_out], res[n_out:]


def _run_comm(comm, name):
    n_in, n_out = len(comm.arrays), len(comm.out_shapes)

    def body(*refs):
        ins, outs, scr = refs[:n_in], refs[n_in:n_in + n_out], refs[n_in + n_out:]
        comm.start(ins, outs, scr)
        comm.wait(ins, outs, scr)

    return pl.pallas_call(
        body, name=name,
        in_specs=comm.in_specs, out_specs=comm.out_specs, out_shape=comm.out_shapes,
        scratch_shapes=comm.scratch, input_output_aliases=comm.aliases,
        compiler_params=pltpu.CompilerParams(vmem_limit_bytes=24 * MIB))(*_operands(comm))


def _both(a, b):
    ni, no, ns = len(a.arrays), len(a.out_shapes), len(a.scratch)

    def start(ins, outs, scr):
        a.start(ins[:ni], outs[:no], scr[:ns])
        b.start(ins[ni:], outs[no:], scr[ns:])

    def wait(ins, outs, scr):
        a.wait(ins[:ni], outs[:no], scr[:ns])
        b.wait(ins[ni:], outs[no:], scr[ns:])

    aliases = dict(a.aliases)
    aliases.update({ni + i: no + o for i, o in b.aliases.items()})
    return _Comm(a.arrays + b.arrays, a.in_specs + b.in_specs, a.out_shapes + b.out_shapes,
                 a.out_specs + b.out_specs, a.scratch + b.scratch, start, wait, aliases)


def _place():
    return lax.axis_index("x"), lax.axis_index("y"), lax.axis_index("c")


def _peer(k):
    x, y, c = _place()
    px, py, pc = x ^ ((k >> 2) & 1), y ^ ((k >> 1) & 1), c ^ (k & 1)
    return (px, py, pc), 4 * px + 2 * py + pc


SIBLING = 1
SAME_CORE = (2, 4, 6)
EVERYONE = tuple(range(1, N_DEV))


def _gather_comm(shards, views, out_shapes, relations):
    na = len(shards)

    def copies(ins, outs, scr):
        send_sems, recv_sems, _ = scr
        _, me = _peer(0)
        out = []
        for a in range(na):
            for k in relations[a]:
                peer, theirs = _peer(k)
                send = functools.partial(
                    pltpu.make_async_remote_copy,
                    src_ref=ins[a], dst_ref=views[a](outs[a], me),
                    send_sem=send_sems.at[7 * a + k - 1], recv_sem=recv_sems.at[7 * a + k - 1],
                    device_id=peer, device_id_type=MESH)
                recv = functools.partial(
                    pltpu.make_async_remote_copy,
                    src_ref=ins[a], dst_ref=views[a](outs[a], theirs),
                    send_sem=send_sems.at[7 * a + k - 1], recv_sem=recv_sems.at[7 * a + k - 1],
                    device_id=peer, device_id_type=MESH)
                out.append((send, recv))
        return out

    def local(ins, outs, scr):
        _, me = _peer(0)
        return [pltpu.make_async_copy(ins[a], views[a](outs[a], me), scr[2].at[a]) for a in range(na)]

    def start(ins, outs, scr):
        for cp in local(ins, outs, scr):
            cp.start()
        for send, _ in copies(ins, outs, scr):
            send().start()

    def wait(ins, outs, scr):
        for _, recv in copies(ins, outs, scr):
            recv().wait_recv()
        for send, _ in copies(ins, outs, scr):
            send().wait_send()
        for cp in local(ins, outs, scr):
            cp.wait()

    return _Comm(shards, [HBM_SPEC] * na, out_shapes, [HBM_SPEC] * na,
                 [pltpu.SemaphoreType.DMA((7 * na,)), pltpu.SemaphoreType.DMA((7 * na,)),
                  pltpu.SemaphoreType.DMA((na,))], start, wait)


def _forward_comm(gathered, views):
    na = len(gathered)
    shapes = [jax.ShapeDtypeStruct(g.shape, g.dtype) for g in gathered]

    def copies(outs, scr):
        send_sems, recv_sems = scr
        sibling, _ = _peer(SIBLING)
        out = []
        for a in range(na):
            for n, k in enumerate(SAME_CORE):
                _, mine = _peer(k)
                _, theirs = _peer(k ^ SIBLING)
                send = functools.partial(
                    pltpu.make_async_remote_copy,
                    src_ref=views[a](outs[a], mine), dst_ref=views[a](outs[a], mine),
                    send_sem=send_sems.at[3 * a + n], recv_sem=recv_sems.at[3 * a + n],
                    device_id=sibling, device_id_type=MESH)
                recv = functools.partial(
                    pltpu.make_async_remote_copy,
                    src_ref=views[a](outs[a], mine), dst_ref=views[a](outs[a], theirs),
                    send_sem=send_sems.at[3 * a + n], recv_sem=recv_sems.at[3 * a + n],
                    device_id=sibling, device_id_type=MESH)
                out.append((send, recv))
        return out

    def start(ins, outs, scr):
        for send, _ in copies(outs, scr):
            send().start()

    def wait(ins, outs, scr):
        for _, recv in copies(outs, scr):
            recv().wait_recv()
        for send, _ in copies(outs, scr):
            send().wait_send()

    return _Comm(gathered, [HBM_SPEC] * na, shapes, [HBM_SPEC] * na,
                 [pltpu.SemaphoreType.DMA((3 * na,)), pltpu.SemaphoreType.DMA((3 * na,))], start, wait,
                 aliases={a: a for a in range(na)})


Y_NEIGHBOUR, X_NEIGHBOUR, DIAGONAL = SAME_CORE


def _all_gather(stack, name):
    K, rows = stack.shape[0], stack.shape[1]
    half = rows // 2
    assert half % 16 == 0
    pieces = [(k, pl.ds(h * half, half)) for k in range(K) for h in range(2)]
    na = len(pieces)
    views = [lambda r, j, k=k, part=part: r.at[k, j, part] for k, part in pieces]
    out_shape = jax.ShapeDtypeStruct((K, N_DEV) + stack.shape[1:], stack.dtype)
    near = (SIBLING, Y_NEIGHBOUR, X_NEIGHBOUR)
    level1 = _gather_comm([stack] * na, views, [out_shape] * na, [near] * na)

    def body(stack_ref, out_ref, *scratch):
        ins, outs = [stack_ref.at[k, part] for k, part in pieces], [out_ref] * na
        send_sems, recv_sems, local_sems, fwd_send, fwd_recv, relay_send, relay_recv = scratch
        sibling, _ = _peer(SIBLING)
        c = lax.axis_index("c")
        level1.start(ins, outs, (send_sems, recv_sems, local_sems))

        def block_copy(a, block, to, send_sem, recv_sem):
            return pltpu.make_async_remote_copy(
                src_ref=views[a](outs[a], block), dst_ref=views[a](outs[a], block),
                send_sem=send_sem, recv_sem=recv_sem, device_id=to, device_id_type=MESH)

        def to_sibling(a, n, k):
            _, mine = _peer(k)
            _, theirs = _peer(k ^ SIBLING)
            fwd = block_copy(a, mine, sibling, fwd_send.at[3 * a + n], fwd_recv.at[3 * a + n])
            fwd.start()
            return fwd, block_copy(a, theirs, sibling, fwd_send.at[3 * a + n], fwd_recv.at[3 * a + n])

        passed, landing = [], []
        for a in range(na):
            for n, k in enumerate((Y_NEIGHBOUR, X_NEIGHBOUR)):
                peer, origin = _peer(k)
                pltpu.make_async_remote_copy(
                    src_ref=ins[a], dst_ref=views[a](outs[a], origin),
                    send_sem=send_sems.at[7 * a + k - 1], recv_sem=recv_sems.at[7 * a + k - 1],
                    device_id=peer, device_id_type=MESH).wait_recv()

                @pl.when(c == (0 if k == X_NEIGHBOUR else 1))
                def _():
                    other, _ = _peer(DIAGONAL ^ k)
                    block_copy(a, origin, other, relay_send.at[a], relay_recv.at[a]).start()

                fwd, lands = to_sibling(a, n, k)
                passed.append(fwd)
                landing.append(lands)
        for a in range(na):
            _, far = _peer(DIAGONAL)
            block_copy(a, far, sibling, relay_send.at[a], relay_recv.at[a]).wait_recv()
            fwd, lands = to_sibling(a, 2, DIAGONAL)
            passed.append(fwd)
            landing.append(lands)
        for a in range(na):
            _, theirs = _peer(SIBLING)
            pltpu.make_async_remote_copy(
                src_ref=ins[a], dst_ref=views[a](outs[a], theirs),
                send_sem=send_sems.at[7 * a + SIBLING - 1], recv_sem=recv_sems.at[7 * a + SIBLING - 1],
                device_id=sibling, device_id_type=MESH).wait_recv()
        for cp in landing:
            cp.wait_recv()
        for cp in passed:
            cp.wait_send()
        _, me = _peer(0)
        for a in range(na):
            block_copy(a, me, sibling, relay_send.at[a], relay_recv.at[a]).wait_send()
            for k in near:
                peer, _ = _peer(k)
                pltpu.make_async_remote_copy(
                    src_ref=ins[a], dst_ref=views[a](outs[a], me),
                    send_sem=send_sems.at[7 * a + k - 1], recv_sem=recv_sems.at[7 * a + k - 1],
                    device_id=peer, device_id_type=MESH).wait_send()
            pltpu.make_async_copy(ins[a], views[a](outs[a], me), local_sems.at[a]).wait()

    return pl.pallas_call(
        body, name=name,
        in_specs=[HBM_SPEC], out_specs=HBM_SPEC, out_shape=out_shape,
        scratch_shapes=level1.scratch + [pltpu.SemaphoreType.DMA((3 * na,)), pltpu.SemaphoreType.DMA((3 * na,)),
                                         pltpu.SemaphoreType.DMA((na,)), pltpu.SemaphoreType.DMA((na,))],
    )(_in_hbm(stack))


def _sibling_comm(grads):
    na = len(grads)
    shapes = [jax.ShapeDtypeStruct(g.shape[:2] + g.shape[3:], g.dtype) for g in grads]

    def copies(ins, outs, scr):
        x, y, c = _place()
        return [pltpu.make_async_remote_copy(
            src_ref=ins[a].at[:, :, 1 - c], dst_ref=outs[a],
            send_sem=scr[0].at[a], recv_sem=scr[1].at[a],
            device_id=(x, y, 1 - c), device_id_type=MESH) for a in range(na)]

    def start(ins, outs, scr):
        for cp in copies(ins, outs, scr):
            cp.start()

    def wait(ins, outs, scr):
        for cp in copies(ins, outs, scr):
            cp.wait()

    return _Comm(grads, [HBM_SPEC] * na, shapes, [HBM_SPEC] * na,
                 [pltpu.SemaphoreType.DMA((na,)), pltpu.SemaphoreType.DMA((na,))], start, wait)


def _chips_comm(sums, every_device=False):
    na = len(sums)
    shapes = [jax.ShapeDtypeStruct(s.shape, s.dtype) for s in sums]
    relations = EVERYONE if every_device else SAME_CORE
    nr = len(relations)

    def block(px, py, pc):
        return 4 * px + 2 * py + pc if every_device else 2 * px + py

    def copies(ins, outs, scr):
        mine = block(*_place())
        out = []
        for a in range(na):
            for n, k in enumerate(relations):
                peer, _ = _peer(k)
                theirs = block(*peer)
                send = functools.partial(
                    pltpu.make_async_remote_copy,
                    src_ref=ins[a].at[:, theirs], dst_ref=outs[a].at[:, mine],
                    send_sem=scr[0].at[nr * a + n], recv_sem=scr[1].at[nr * a + n],
                    device_id=peer, device_id_type=MESH)
                recv = functools.partial(
                    pltpu.make_async_remote_copy,
                    src_ref=ins[a].at[:, mine], dst_ref=outs[a].at[:, theirs],
                    send_sem=scr[0].at[nr * a + n], recv_sem=scr[1].at[nr * a + n],
                    device_id=peer, device_id_type=MESH)
                out.append((send, recv))
        return out

    def local(ins, outs, scr):
        mine = block(*_place())
        return [pltpu.make_async_copy(ins[a].at[:, mine], outs[a].at[:, mine], scr[2].at[a]) for a in range(na)]

    def start(ins, outs, scr):
        for cp in local(ins, outs, scr):
            cp.start()
        for send, _ in copies(ins, outs, scr):
            send().start()

    def wait(ins, outs, scr):
        for _, recv in copies(ins, outs, scr):
            recv().wait_recv()
        for send, _ in copies(ins, outs, scr):
            send().wait_send()
        for cp in local(ins, outs, scr):
            cp.wait()

    return _Comm(sums, [HBM_SPEC] * na, shapes, [HBM_SPEC] * na,
                 [pltpu.SemaphoreType.DMA((nr * na,)), pltpu.SemaphoreType.DMA((nr * na,)),
                  pltpu.SemaphoreType.DMA((na,))], start, wait)


def _small_sum_comm(arrays):
    na = len(arrays)

    def copies(ins, scr):
        bufs, send_sems, recv_sems = scr[:na], scr[na], scr[na + 1]
        _, me = _peer(0)
        out = []
        for a in range(na):
            for k in EVERYONE:
                peer, theirs = _peer(k)
                sems = dict(send_sem=send_sems.at[7 * a + k - 1], recv_sem=recv_sems.at[7 * a + k - 1])
                send = functools.partial(
                    pltpu.make_async_remote_copy,
                    src_ref=ins[a], dst_ref=bufs[a].at[me], device_id=peer, device_id_type=MESH, **sems)
                recv = functools.partial(
                    pltpu.make_async_remote_copy,
                    src_ref=ins[a], dst_ref=bufs[a].at[theirs], device_id=peer, device_id_type=MESH, **sems)
                out.append((send, recv))
        return out

    def start(ins, outs, scr):
        _, me = _peer(0)
        for a in range(na):
            scr[a][me] = ins[a][...]
        for send, _ in copies(ins, scr):
            send().start()

    def wait(ins, outs, scr):
        for _, recv in copies(ins, scr):
            recv().wait_recv()
        for send, _ in copies(ins, scr):
            send().wait_send()
        for a in range(na):
            acc = scr[a][0]
            for j in range(1, N_DEV):
                acc = acc + scr[a][j]
            outs[a][...] = acc

    return _Comm(arrays, [VMEM_SPEC] * na, [jax.ShapeDtypeStruct(s.shape, F32) for s in arrays], [VMEM_SPEC] * na,
                 [pltpu.VMEM((N_DEV,) + s.shape, F32) for s in arrays]
                 + [pltpu.SemaphoreType.DMA((7 * na,)), pltpu.SemaphoreType.DMA((7 * na,))], start, wait)


def _prep_weights(ffn1, ffn2, w_in, w_out, name):
    rf, D = ffn1[2].shape
    ri, ro = w_in.shape[1], w_out.shape[0]

    def body(g1, u1, d1, g2, u2, d2, wi, wo, p1_ref, p2_ref, pi_ref, po_ref):
        for p_ref, shards in ((p1_ref, (g1, u1, d1)), (p2_ref, (g2, u2, d2))):
            for k, shard in enumerate(shards):
                p_ref[k] = shard[...].astype(BF16)
        pi_ref[...] = wi[...].T.astype(BF16)
        po_ref[...] = wo[...].astype(BF16)

    args = (*ffn1, *ffn2, w_in, w_out)
    whole = lambda shape: pl.BlockSpec(shape, lambda i: (0,) * len(shape))
    out_shapes = [(3, rf, D), (3, rf, D), (ri, D), (ro, D)]
    return _call(
        body, name=name, grid=(1,),
        in_specs=[whole(a.shape) for a in args], out_specs=[whole(s) for s in out_shapes],
        out_shape=[jax.ShapeDtypeStruct(s, BF16) for s in out_shapes],
        scratch_shapes=[], vmem_mib=48, args=args)


def _load_weights(w_hbm, w_vmem, sem):
    @pl.when(pl.program_id(0) == 0)
    def _():
        copies = [pltpu.make_async_copy(w_hbm.at[k], w_vmem.at[k], sem.at[k]) for k in range(3)]
        for cp in copies:
            cp.start()
        for cp in copies:
            cp.wait()


def _ffn_fwd(x, g, w3, tm, cf, name, rider=None, head=None):
    T, D = x.shape
    F = w3.shape[1]
    n_head = 0 if head is None else 2

    def body(x_ref, g_ref, w_hbm, *refs):
        head_refs, refs = refs[:n_head], refs[n_head:]
        if head is None:
            (xo_ref, h_ref, dau_ref, dag_ref, act_ref, wv, sem) = refs
        else:
            (dx_ref, dob_ref, dgf_ref, loss_ref, h_ref, dau_ref, dag_ref, act_ref, wv, sem) = refs
        _load_weights(w_hbm, wv, sem)
        xhat, _ = _rms_stats(x_ref[...])
        hb = (xhat * g_ref[...]).astype(BF16)
        h_ref[...] = hb
        for lo in range(0, F, cf):
            gate = _dot(hb, wv[0, lo:lo + cf, :], NT)
            up = _dot(hb, wv[1, lo:lo + cf, :], NT)
            sig = _sigmoid(gate)
            silu = gate * sig
            dau_ref[:, lo:lo + cf] = silu.astype(BF16)
            dag_ref[:, lo:lo + cf] = (up * (sig * (1.0 + gate * (1.0 - sig)))).astype(BF16)
            act_ref[:, lo:lo + cf] = (silu * up).astype(BF16)
        x_out = x_ref[...] + FFN_RES * _dot(act_ref[...], wv[2], NN)
        if head is None:
            xo_ref[...] = x_out
            return

        @pl.when(pl.program_id(0) == 0)
        def _():
            dgf_ref[...] = jnp.zeros_like(dgf_ref)
            loss_ref[...] = jnp.zeros_like(loss_ref)

        gf_ref, tgt_ref = head_refs
        yhat, rstd = _rms_stats(x_out)
        gf = gf_ref[...]
        err = yhat * gf - tgt_ref[...]
        loss_ref[...] += (0.5 / D) * jnp.sum(err * err)
        dx, dgf = _rms_bwd(yhat, rstd, gf, err * (1.0 / D))
        dx_ref[...] = dx
        dob_ref[...] = (FFN_RES * dx).astype(BF16)
        dgf_ref[...] += dgf

    row = pl.BlockSpec((tm, D), lambda i: (i, 0))
    hid = pl.BlockSpec((tm, F), lambda i: (i, 0))
    vec = pl.BlockSpec((1, D), lambda i: (0, 0))
    row_f32, row_bf16 = jax.ShapeDtypeStruct((T, D), F32), jax.ShapeDtypeStruct((T, D), BF16)
    if head is None:
        first_specs, first_shapes = [row], [row_f32]
    else:
        first_specs = [row, row, vec, pl.BlockSpec((1, 128), lambda i: (0, 0))]
        first_shapes = [row_f32, row_bf16, jax.ShapeDtypeStruct((1, D), F32), jax.ShapeDtypeStruct((1, 128), F32)]
    return _call(
        body, name=name, grid=(T // tm,),
        in_specs=[row, vec, HBM_SPEC] + ([] if head is None else [vec, row]),
        out_specs=first_specs + [row, hid, hid, hid],
        out_shape=first_shapes + [row_bf16] + [jax.ShapeDtypeStruct((T, F), BF16)] * 3,
        scratch_shapes=[pltpu.VMEM((3, F, D), BF16), pltpu.SemaphoreType.DMA((3,))],
        vmem_mib=60, args=(x, g, w3) + (() if head is None else tuple(head)), rider=rider)


def _ffn_dgrad(dout, x, g, dau, dag, w3, tm, cf, name, rider=None):
    T, D = x.shape
    F = w3.shape[1]

    def body(do_ref, x_ref, g_ref, dau_ref, dag_ref, w_hbm, dx_ref, dgate_ref, dup_ref, dg_ref, wv, sem):
        _load_weights(w_hbm, wv, sem)

        @pl.when(pl.program_id(0) == 0)
        def _():
            dg_ref[...] = jnp.zeros_like(dg_ref)

        dob = (FFN_RES * do_ref[...]).astype(BF16)
        for lo in range(0, F, cf):
            dact = _dot(dob, wv[2, lo:lo + cf, :], NT)
            dup_ref[:, lo:lo + cf] = (dact * dau_ref[:, lo:lo + cf].astype(F32)).astype(BF16)
            dgate_ref[:, lo:lo + cf] = (dact * dag_ref[:, lo:lo + cf].astype(F32)).astype(BF16)
        dh = _dot(dgate_ref[...], wv[0], NN) + _dot(dup_ref[...], wv[1], NN)
        xhat, rstd = _rms_stats(x_ref[...])
        dx, dg = _rms_bwd(xhat, rstd, g_ref[...], dh)
        dx_ref[...] = do_ref[...] + dx
        dg_ref[...] += dg

    row = pl.BlockSpec((tm, D), lambda i: (i, 0))
    hid = pl.BlockSpec((tm, F), lambda i: (i, 0))
    vec = pl.BlockSpec((1, D), lambda i: (0, 0))
    return _call(
        body, name=name, grid=(T // tm,),
        in_specs=[row, row, vec, hid, hid, HBM_SPEC],
        out_specs=[row, hid, hid, vec],
        out_shape=[jax.ShapeDtypeStruct((T, D), F32), jax.ShapeDtypeStruct((T, F), BF16),
                   jax.ShapeDtypeStruct((T, F), BF16), jax.ShapeDtypeStruct((1, D), F32)],
        scratch_shapes=[pltpu.VMEM((3, F, D), BF16), pltpu.SemaphoreType.DMA((3,))],
        vmem_mib=60, args=(dout, x, g, dau, dag, w3), rider=rider)


def _wgrad(lhs, rhs, tm, tf, name, rider=None):
    T, F = lhs[0].shape
    D = rhs.shape[1]
    K = len(lhs)

    def body(*refs):
        lhs_refs, rhs_ref, dw_ref, accs = refs[:K], refs[K], refs[K + 1], refs[K + 2:]
        i = pl.program_id(1)

        @pl.when(i == 0)
        def _():
            for acc in accs:
                acc[...] = jnp.zeros_like(acc)

        rv = rhs_ref[...]
        for acc, lhs_ref in zip(accs, lhs_refs):
            acc[...] += _dot(lhs_ref[...], rv, TN)

        @pl.when(i == pl.num_programs(1) - 1)
        def _():
            for k, acc in enumerate(accs):
                dw_ref[k] = acc[...].astype(BF16)

    hid = pl.BlockSpec((tm, tf), lambda f, i: (i, f))
    return _call(
        body, name=name, grid=(F // tf, T // tm),
        in_specs=[hid] * K + [pl.BlockSpec((tm, D), lambda f, i: (i, 0))],
        out_specs=[pl.BlockSpec((K, tf, D), lambda f, i: (0, f, 0))],
        out_shape=[jax.ShapeDtypeStruct((K, F, D), BF16)],
        scratch_shapes=[pltpu.VMEM((tf, D), F32)] * K,
        vmem_mib=56, args=(*lhs, rhs), rider=rider)


def _lru_gates(xr, bda_ref, bdx_ref, vec_ref):
    xrb = xr.astype(BF16)
    r = _sigmoid(_dot(xrb, bda_ref[...], NN) + vec_ref[V_BA:V_BA + 1, :])
    ig = _sigmoid(_dot(xrb, bdx_ref[...], NN) + vec_ref[V_BX:V_BX + 1, :])
    sp = _softplus_neg(vec_ref[V_LAM:V_LAM + 1, :])
    log_a = (-LRU_C * sp) * r
    a = jnp.exp(log_a)
    mult = jnp.sqrt(_neg_expm1(2.0 * log_a))
    return xrb, r, ig, sp, a, mult


def _layernorm_stats(u1):
    xc = u1 - jnp.mean(u1, axis=-1, keepdims=True)
    rs = lax.rsqrt(jnp.mean(xc * xc, axis=-1, keepdims=True) + LN_EPS)
    return xc * rs, rs


def _mix_core_fwd(x1, g, w_in_t, w_out, bda, bdx, cw, lw, vec, tm, name, rider=None):
    T, D = x1.shape
    W = cw.shape[1]
    assert tm >= CONV_HALO and w_in_t.shape[0] == 4 * W

    def body(x1_ref, g_ref, wi_ref, wo_ref, bda_ref, bdx_ref, cw_ref, lw_ref, vec_ref,
             x2_ref, z_ref, mix_ref, u1_ref, xr_ref, hst_ref, ubuf, rbuf, hc):
        @pl.when(pl.program_id(0) == 0)
        def _():
            ubuf[0:CONV_HALO, :] = jnp.zeros((CONV_HALO, W), F32)
            rbuf[0:LRU_HALO, :] = jnp.zeros((LRU_HALO, W), F32)
            hc[...] = jnp.zeros_like(hc)

        xhat, _ = _rms_stats(x1_ref[...])
        z_ref[...] = _dot((xhat * g_ref[...]).astype(BF16), wi_ref[...], NT)

        ubuf[CONV_HALO:CONV_HALO + tm, :] = z_ref[:, 0:W] * _sigmoid(z_ref[:, W:2 * W])
        u1 = jnp.zeros((tm, W), F32) + vec_ref[V_CB:V_CB + 1, :]
        base = CONV_HALO - (CONV_K - 1)
        for off, win in _row_windows(ubuf, tm, range(base, base + CONV_K)):
            u1 = u1 + cw_ref[off - base:off - base + 1, :] * win
        ubuf[0:CONV_HALO, :] = ubuf[tm:tm + CONV_HALO, :]
        u1_ref[...] = u1
        xh, _ = _layernorm_stats(u1)
        u2 = xh * vec_ref[V_LNG:V_LNG + 1, :] + vec_ref[V_LNB:V_LNB + 1, :]
        ub = (u2 * _sigmoid(u2)).astype(BF16)
        mix_ref[:, 0:W] = ub

        rbuf[LRU_HALO:LRU_HALO + tm, :] = z_ref[:, 2 * W:3 * W]
        xr = jnp.zeros((tm, W), F32) + vec_ref[V_LCB:V_LCB + 1, :]
        for k in range(LRU_K):
            off = LRU_HALO - (LRU_K - 1) + k
            xr = xr + lw_ref[k:k + 1, :] * rbuf[off:off + tm, :]
        rbuf[0:LRU_HALO, :] = rbuf[tm:tm + LRU_HALO, :]
        xr_ref[...] = xr
        _, _, ig, _, a, mult = _lru_gates(xr, bda_ref, bdx_ref, vec_ref)
        hc[0:1, :] = _scan_rows(a, mult * (ig * xr), hc[0:1, :], hst_ref)
        gl, _ = _gelu_parts(z_ref[:, 3 * W:4 * W])
        yb = (hst_ref[...] * gl).astype(BF16)
        mix_ref[:, W:2 * W] = yb

        x2_ref[...] = x1_ref[...] + _dot(ub, wo_ref[0:W, :], NN) + _dot(yb, wo_ref[W:2 * W, :], NN)

    full = lambda a: pl.BlockSpec(a.shape, lambda i: (0,) * a.ndim)
    tile = lambda n: pl.BlockSpec((tm, n), lambda i: (i, 0))
    return _call(
        body, name=name, grid=(T // tm,),
        in_specs=[tile(D), full(g), full(w_in_t), full(w_out), full(bda), full(bdx), full(cw), full(lw), full(vec)],
        out_specs=[tile(D), tile(4 * W), tile(2 * W), tile(W), tile(W), tile(W)],
        out_shape=[jax.ShapeDtypeStruct((T, D), F32), jax.ShapeDtypeStruct((T, 4 * W), F32),
                   jax.ShapeDtypeStruct((T, 2 * W), BF16), jax.ShapeDtypeStruct((T, W), F32),
                   jax.ShapeDtypeStruct((T, W), F32), jax.ShapeDtypeStruct((T, W), F32)],
        scratch_shapes=[pltpu.VMEM((tm + CONV_HALO, W), F32), pltpu.VMEM((tm + LRU_HALO, W), F32),
                        pltpu.VMEM((8, W), F32)],
        vmem_mib=56, args=(x1, g, w_in_t, w_out, bda, bdx, cw, lw, vec), rider=rider)


def _mix_bwd(dx2, z, u1, xr, hst, x1, mix, g, w_in_t, w_out, bda, bdx, cw, lw, vec, tm, name, rider=None):
    T, D = dx2.shape
    W = cw.shape[1]
    nt = T // tm
    assert tm >= CONV_HALO and tm % CONV_HALO == 0

    def body(dx_ref, z_ref, zh_ref, u1_ref, xr_ref, h_ref, hh_ref, wo_ref, bda_ref, bdx_ref, cw_ref, lw_ref, vec_ref,
             x1_ref, mix_ref, g_ref, wi_ref,
             dx1_ref, sg_ref, dbda_ref, dbdx_ref, dob_ref, dg_ref, dwi_ref, dwo_ref,
             u0buf, du1buf, rxbuf, dxrbuf, gbuf, gc, spacc, dz_ref, ai_ref, ao_ref):
        i = pl.program_id(0)
        first = i == nt - 1
        row = lax.broadcasted_iota(jnp.int32, (tm, W), 0)

        @pl.when(i == 0)
        def _():
            sg_ref[...] = jnp.zeros_like(sg_ref)
            dbda_ref[...] = jnp.zeros_like(dbda_ref)
            dbdx_ref[...] = jnp.zeros_like(dbdx_ref)
            du1buf[tm:tm + CONV_HALO, :] = jnp.zeros((CONV_HALO, W), F32)
            dxrbuf[tm:tm + LRU_HALO, :] = jnp.zeros((LRU_HALO, W), F32)
            gc[...] = jnp.zeros_like(gc)
            spacc[...] = jnp.zeros_like(spacc)
            dg_ref[...] = jnp.zeros_like(dg_ref)
            ai_ref[...] = jnp.zeros_like(ai_ref)
            ao_ref[...] = jnp.zeros_like(ao_ref)

        def accum(r, val):
            sg_ref[r:r + 1, :] += jnp.sum(val, axis=0, keepdims=True)

        x1hat, x1rstd = _rms_stats(x1_ref[...])
        gain = g_ref[...]
        hb = (x1hat * gain).astype(BF16)

        def in_proj_bwd(lo, hi):
            dzb = dz_ref[:, lo:hi]
            ai_ref[lo:hi, :] += _dot(dzb, hb, TN)
            return _dot(dzb, wi_ref[lo:hi, :], NN)

        dxb = dx_ref[...].astype(BF16)
        ao_ref[...] += _dot(mix_ref[...], dxb, TN)
        dmix = _dot(dxb, wo_ref[...], NT)
        d_u = dmix[:, 0:W]
        d_yr = dmix[:, W:2 * W]

        xh, rs = _layernorm_stats(u1_ref[...])
        ln_g = vec_ref[V_LNG:V_LNG + 1, :]
        u2 = xh * ln_g + vec_ref[V_LNB:V_LNB + 1, :]
        s2 = _sigmoid(u2)
        d_u2 = d_u * (s2 * (1.0 + u2 * (1.0 - s2)))
        accum(G_LNG, d_u2 * xh)
        accum(G_LNB, d_u2)
        d_xh = d_u2 * ln_g
        d_u1 = rs * (d_xh - jnp.mean(d_xh, axis=-1, keepdims=True)
                     - xh * jnp.mean(d_xh * xh, axis=-1, keepdims=True))
        accum(G_CB, d_u1)
        halo_on = jnp.where(first, 0.0, 1.0)
        u0buf[0:CONV_HALO, :] = halo_on * (zh_ref[:, 0:W] * _sigmoid(zh_ref[:, W:2 * W]))
        cv = z_ref[:, 0:W]
        sgc = _sigmoid(z_ref[:, W:2 * W])
        u0buf[CONV_HALO:CONV_HALO + tm, :] = cv * sgc
        du1buf[0:tm, :] = d_u1
        base = CONV_HALO - (CONV_K - 1)
        for off, win in _row_windows(u0buf, tm, range(base, base + CONV_K)):
            accum(G_CW + off - base, d_u1 * win)
        d_u0 = jnp.zeros((tm, W), F32)
        for off, win in _row_windows(du1buf, tm, range(0, CONV_K)):
            d_u0 = d_u0 + cw_ref[CONV_K - 1 - off:CONV_K - off, :] * win
        du1buf[tm:tm + CONV_HALO, :] = du1buf[0:CONV_HALO, :]
        dz_ref[:, 0:W] = (d_u0 * sgc).astype(BF16)
        dz_ref[:, W:2 * W] = (d_u0 * cv * (sgc * (1.0 - sgc))).astype(BF16)
        dh = in_proj_bwd(0, 2 * W)

        xrv = xr_ref[...]
        xrb, r, ig, sp, a, mult = _lru_gates(xrv, bda_ref, bdx_ref, vec_ref)
        h = h_ref[...]
        gl, dgl = _gelu_parts(z_ref[:, 3 * W:4 * W])
        dz_ref[:, 3 * W:4 * W] = (d_yr * h * dgl).astype(BF16)
        dh = dh + in_proj_bwd(3 * W, 4 * W)
        a_next = jnp.where(row == tm - 1, 1.0, pltpu.roll(a, tm - 1, 0))
        g_first = _scan_rows(a_next, d_yr * gl, gc[0:1, :], gbuf, reverse=True)
        g = gbuf[...]
        gc[0:1, :] = a[0:1, :] * g_first
        hprev = jnp.where(row == 0, halo_on * hh_ref[LRU_HALO - 1:LRU_HALO, :], pltpu.roll(h, 1, 0))
        d_log_a = (g * hprev) * a - (g * ig * xrv) * (a * a) / mult
        d_ig = g * mult * xrv
        d_xr = g * mult * ig
        spacc[0:1, :] += jnp.sum(d_log_a * r, axis=0, keepdims=True)
        d_pa32 = (d_log_a * (-LRU_C * sp)) * (r * (1.0 - r))
        d_px32 = d_ig * (ig * (1.0 - ig))
        accum(G_BA, d_pa32)
        accum(G_BX, d_px32)
        d_pa = d_pa32.astype(BF16)
        d_px = d_px32.astype(BF16)
        d_xr = d_xr + _dot(d_pa, bda_ref[...], NT) + _dot(d_px, bdx_ref[...], NT)
        dbda_ref[...] += _dot(xrb, d_pa, TN)
        dbdx_ref[...] += _dot(xrb, d_px, TN)
        accum(G_LCB, d_xr)
        rxbuf[0:LRU_HALO, :] = halo_on * zh_ref[CONV_HALO - LRU_HALO:CONV_HALO, 2 * W:3 * W]
        rxbuf[LRU_HALO:LRU_HALO + tm, :] = z_ref[:, 2 * W:3 * W]
        dxrbuf[0:tm, :] = d_xr
        d_rx = jnp.zeros((tm, W), F32)
        for k in range(LRU_K):
            off = LRU_HALO - (LRU_K - 1) + k
            accum(G_LW + k, d_xr * rxbuf[off:off + tm, :])
            d_rx = d_rx + lw_ref[k:k + 1, :] * dxrbuf[LRU_K - 1 - k:LRU_K - 1 - k + tm, :]
        dxrbuf[tm:tm + LRU_HALO, :] = dxrbuf[0:LRU_HALO, :]
        dz_ref[:, 2 * W:3 * W] = d_rx.astype(BF16)
        dh = dh + in_proj_bwd(2 * W, 3 * W)

        dx, dg = _rms_bwd(x1hat, x1rstd, gain, dh)
        dx1 = dx_ref[...] + dx
        dx1_ref[...] = dx1
        dob_ref[...] = (FFN_RES * dx1).astype(BF16)
        dg_ref[...] += dg

        @pl.when(first)
        def _():
            lam = vec_ref[V_LAM:V_LAM + 1, :]
            sg_ref[G_LAM:G_LAM + 1, :] = LRU_C * _sigmoid(-lam) * spacc[0:1, :]
            dwi_ref[...] = ai_ref[...].astype(BF16)
            dwo_ref[...] = ao_ref[...].astype(BF16)

    full = lambda a: pl.BlockSpec(a.shape, lambda i: (0,) * a.ndim)
    tile = lambda n: pl.BlockSpec((tm, n), lambda i: (nt - 1 - i, 0))
    halo = lambda rows, n: pl.BlockSpec(
        (rows, n), lambda i: (jnp.maximum((nt - 1 - i) * (tm // rows) - 1, 0), 0))
    const = lambda r, c: pl.BlockSpec((r, c), lambda i: (0, 0))
    return _call(
        body, name=name, grid=(nt,),
        in_specs=[tile(D), tile(4 * W), halo(CONV_HALO, 4 * W), tile(W), tile(W), tile(W), halo(LRU_HALO, W),
                  full(w_out), full(bda), full(bdx), full(cw), full(lw), full(vec),
                  tile(D), tile(2 * W), full(g), full(w_in_t)],
        out_specs=[tile(D), const(G_ROWS, W), const(W, W), const(W, W),
                   tile(D), const(1, D), const(4 * W, D), const(2 * W, D)],
        out_shape=[jax.ShapeDtypeStruct((T, D), F32), jax.ShapeDtypeStruct((G_ROWS, W), F32),
                   jax.ShapeDtypeStruct((W, W), F32), jax.ShapeDtypeStruct((W, W), F32),
                   jax.ShapeDtypeStruct((T, D), BF16), jax.ShapeDtypeStruct((1, D), F32),
                   jax.ShapeDtypeStruct((4 * W, D), BF16), jax.ShapeDtypeStruct((2 * W, D), BF16)],
        scratch_shapes=[pltpu.VMEM((tm + CONV_HALO, W), F32), pltpu.VMEM((tm + CONV_HALO, W), F32),
                        pltpu.VMEM((tm + LRU_HALO, W), F32), pltpu.VMEM((tm + LRU_HALO, W), F32),
                        pltpu.VMEM((tm, W), F32), pltpu.VMEM((8, W), F32), pltpu.VMEM((8, W), F32),
                        pltpu.VMEM((tm, 4 * W), BF16), pltpu.VMEM((4 * W, D), F32), pltpu.VMEM((2 * W, D), F32)],
        vmem_mib=60, args=(dx2, z, z, u1, xr, hst, hst, w_out, bda, bdx, cw, lw, vec, x1, mix, g, w_in_t),
        rider=rider)


def _pair_add(full, recv, name):
    K, _, _, rows, D = full.shape

    def body(c_ref, a_ref, b_ref, o_ref):
        o_ref[...] = (a_ref[...].astype(F32) + b_ref[...].astype(F32)).astype(BF16)

    c = lax.axis_index("c").astype(jnp.int32).reshape((1,))
    return _call(
        body, name=name, grid=(K, N_CHIP), num_scalar_prefetch=1,
        in_specs=[pl.BlockSpec((None, None, None, rows, D), lambda k, q, c_ref: (k, q, c_ref[0], 0, 0)),
                  pl.BlockSpec((None, None, rows, D), lambda k, q, c_ref: (k, q, 0, 0))],
        out_specs=pl.BlockSpec((None, None, rows, D), lambda k, q, c_ref: (k, q, 0, 0)),
        out_shape=jax.ShapeDtypeStruct(recv.shape, BF16),
        scratch_shapes=[], vmem_mib=16, args=(c, full, recv))


def _adamw_update(wv, gv, mv, vv):
    m2 = ADAM_B1 * mv + (1.0 - ADAM_B1) * gv
    v2 = ADAM_B2 * vv + (1.0 - ADAM_B2) * (gv * gv)
    m_hat = m2 / (1.0 - ADAM_B1 ** ADAM_STEP)
    v_hat = v2 / (1.0 - ADAM_B2 ** ADAM_STEP)
    return -ADAM_LR * (m_hat / (jnp.sqrt(v_hat) + ADAM_EPS) + ADAM_WD * wv), m2, v2


def _finish(parts, k, w, m, v, transpose, name):
    _, n_parts, rows, D = parts.shape

    def body(p_ref, w_ref, m_ref, v_ref, g_ref, d_ref, mo_ref, vo_ref):
        acc = p_ref[0].astype(F32)
        for q in range(1, n_parts):
            acc = acc + p_ref[q].astype(F32)
        gv = acc.T if transpose else acc
        g_ref[...] = gv
        d_ref[...], mo_ref[...], vo_ref[...] = _adamw_update(w_ref[...], gv, m_ref[...], v_ref[...])

    whole = pl.BlockSpec(w.shape, lambda i: (0, 0))
    return _call(
        body, name=name, grid=(1,),
        in_specs=[pl.BlockSpec((None, n_parts, rows, D), lambda i: (k, 0, 0, 0)), whole, whole, whole],
        out_specs=[whole] * 4, out_shape=[pltpu.HBM(w.shape, F32)] * 4,
        scratch_shapes=[], vmem_mib=40, args=(parts, w, m, v))


def _finish_many(items, name, rider=None):
    n_items = len(items)
    D = items[0][2].shape[1]
    max_rows = max(w.shape[0] for _, _, w, _, _ in items)
    max_parts = max(parts.shape[1] for parts, _, _, _, _ in items)
    n_loads, n_stores = 4, 4

    def body(*refs):
        ins, outs = refs[:4 * n_items], refs[4 * n_items:8 * n_items]
        pbuf, wbuf, obuf, sem = refs[8 * n_items:]
        step = pl.program_id(0)
        for j, (parts, k, w, _, _) in enumerate(items):
            rows, n_parts = w.shape[0], parts.shape[1]

            @pl.when(step == j)
            def _():
                p_ref, w_ref, m_ref, v_ref = ins[4 * j:4 * j + 4]
                loads = [pltpu.make_async_copy(p_ref.at[k], pbuf.at[0:n_parts, 0:rows], sem.at[0])]
                loads += [pltpu.make_async_copy(src, wbuf.at[r, 0:rows], sem.at[1 + r])
                          for r, src in enumerate((w_ref, m_ref, v_ref))]
                for cp in loads:
                    cp.start()
                for cp in loads:
                    cp.wait()
                acc = pbuf[0, 0:rows, :].astype(F32)
                for q in range(1, n_parts):
                    acc = acc + pbuf[q, 0:rows, :].astype(F32)
                obuf[0, 0:rows, :] = acc
                obuf[1, 0:rows, :], obuf[2, 0:rows, :], obuf[3, 0:rows, :] = _adamw_update(
                    wbuf[0, 0:rows, :], acc, wbuf[1, 0:rows, :], wbuf[2, 0:rows, :])
                stores = [pltpu.make_async_copy(obuf.at[r, 0:rows], outs[4 * j + r], sem.at[n_loads + r])
                          for r in range(n_stores)]
                for cp in stores:
                    cp.start()
                for cp in stores:
                    cp.wait()

    args = [a for parts, _, w, m, v in items for a in (parts, w, m, v)]
    shapes = [jax.ShapeDtypeStruct(w.shape, F32) for _, _, w, _, _ in items for _ in range(4)]
    return _call(
        body, name=name, grid=(n_items,),
        in_specs=[HBM_SPEC] * len(args), out_specs=[HBM_SPEC] * len(shapes), out_shape=shapes,
        scratch_shapes=[pltpu.VMEM((max_parts, max_rows, D), BF16), pltpu.VMEM((3, max_rows, D), F32),
                        pltpu.VMEM((4, max_rows, D), F32), pltpu.SemaphoreType.DMA((n_loads + n_stores,))],
        vmem_mib=40, args=args, rider=rider)


def _adamw_each(ws, gs, ms, vs, name):
    n = len(ws)

    def body(*refs):
        w_refs, g_refs, m_refs, v_refs, outs = refs[:n], refs[n:2 * n], refs[2 * n:3 * n], refs[3 * n:4 * n], refs[4 * n:]
        for k in range(n):
            outs[k][...], outs[n + k][...], outs[2 * n + k][...] = _adamw_update(
                w_refs[k][...], g_refs[k][...], m_refs[k][...], v_refs[k][...])

    shapes = [jax.ShapeDtypeStruct(w.shape, F32) for w in ws]
    return pl.pallas_call(
        body, name=name,
        in_specs=[VMEM_SPEC] * (4 * n), out_specs=[VMEM_SPEC] * (3 * n), out_shape=shapes * 3,
        compiler_params=pltpu.CompilerParams(vmem_limit_bytes=32 * MIB),
    )(*ws, *gs, *ms, *vs)


def _block_diag(w):
    h, d, _ = w.shape
    onto = jnp.eye(h, dtype=w.dtype)
    return (w[:, :, None, :] * onto[:, None, :, None]).reshape(h * d, h * d)


def _diag_blocks(m, h):
    d = m.shape[0] // h
    onto = jnp.eye(h, dtype=m.dtype)
    return (m.reshape(h, d, h, d) * onto[:, None, :, None]).sum(axis=2)


def _reduce_level1(full, tag):
    got = _run_comm(_sibling_comm(full), "rs_sibling_" + tag)
    return [_pair_add(a, b, "rs_pair_add_%s%d" % (tag, n)) for n, (a, b) in enumerate(zip(full, got))]


def kernel(x, ffn1_norm, ffn1_w_gate, ffn1_w_up, ffn1_w_down, mix_norm, w_in, conv_dw, conv_dw_bias, conv_ln_g, conv_ln_b, lru_conv_w, lru_conv_b, lru_w_a, lru_b_a, lru_w_x, lru_b_x, lru_lambda, w_out, ffn2_norm, ffn2_w_gate, ffn2_w_up, ffn2_w_down, final_norm, loss_target, m_ffn1_norm, m_ffn1_w_gate, m_ffn1_w_up, m_ffn1_w_down, m_mix_norm, m_w_in, m_conv_dw, m_conv_dw_bias, m_conv_ln_g, m_conv_ln_b, m_lru_conv_w, m_lru_conv_b, m_lru_w_a, m_lru_b_a, m_lru_w_x, m_lru_b_x, m_lru_lambda, m_w_out, m_ffn2_norm, m_ffn2_w_gate, m_ffn2_w_up, m_ffn2_w_down, m_final_norm, v_ffn1_norm, v_ffn1_w_gate, v_ffn1_w_up, v_ffn1_w_down, v_mix_norm, v_w_in, v_conv_dw, v_conv_dw_bias, v_conv_ln_g, v_conv_ln_b, v_lru_conv_w, v_lru_conv_b, v_lru_w_a, v_lru_b_a, v_lru_w_x, v_lru_b_x, v_lru_lambda, v_w_out, v_ffn2_norm, v_ffn2_w_gate, v_ffn2_w_up, v_ffn2_w_down, v_final_norm):
    T, D = x.shape[1], x.shape[2]
    F = ffn1_w_down.shape[0] * N_DEV
    rf = ffn1_w_down.shape[0]
    ri = w_in.shape[1]
    ro = w_out.shape[0]
    W = conv_dw_bias.shape[0]
    wc = conv_dw.shape[1]
    H = lru_w_a.shape[0]
    xs = x.reshape(T, D)
    tgt = loss_target.reshape(T, D)
    tm_ffn = min(512, T)
    tm_fwd = min(512, T)
    cf = 256
    tm_w = min(1024, T)
    tm_w1 = min(2048, T)
    tm_mix = min(256, T)
    tf_w = F // 2
    row = lambda v: v.reshape(1, -1)
    by_owner = lambda a, rows: a.reshape(a.shape[0], N_CHIP, 2, rows, D)

    p3a, p3b, p_in, p_out = _prep_weights(
        (ffn1_w_gate.T, ffn1_w_up.T, ffn1_w_down), (ffn2_w_gate.T, ffn2_w_up.T, ffn2_w_down), w_in, w_out,
        "prep_weights")
    tile_rows = lambda a: jnp.pad(a, ((0, -a.shape[0] % SUBLANES), (0, 0)))
    p_cw = jnp.concatenate([tile_rows(conv_dw), tile_rows(lru_conv_w)], axis=0)
    lw_row = p_cw.shape[0] - SUBLANES
    stacked = lambda r, j: r.at[:, j]
    plain = lambda r, j: r.at[j]
    g3_shape = jax.ShapeDtypeStruct((3, N_DEV, rf, D), BF16)
    g3a = _all_gather(p3a, "ag_ffn1")
    w3a = g3a.reshape(3, F, D)
    bda = _block_diag(lru_w_a).astype(BF16)
    bdx = _block_diag(lru_w_x).astype(BF16)
    vec = jnp.concatenate([tile_rows(v[None]) for v in
                           (conv_dw_bias, conv_ln_g, conv_ln_b, lru_conv_b, lru_b_a, lru_b_x, lru_lambda)], axis=0)

    gather_rest = _gather_comm(
        [p3b, p_in, p_out, p_cw], [stacked, plain, plain, plain],
        [g3_shape, jax.ShapeDtypeStruct((N_DEV, ri, D), BF16), jax.ShapeDtypeStruct((N_DEV, ro, D), BF16),
         jax.ShapeDtypeStruct((N_DEV,) + p_cw.shape, F32)],
        [(SIBLING,) + SAME_CORE, EVERYONE, EVERYONE, EVERYONE])
    (x1, h1, dau1, dag1, act1), (g3b_half, g_in, g_out, g_cw) = _ffn_fwd(
        xs, row(ffn1_norm), w3a, tm_fwd, cf, "ffn1_fwd", rider=gather_rest)
    w_in_t = g_in.reshape(N_DEV * ri, D)
    w_out_f = g_out.reshape(N_DEV * ro, D)
    cw_all = jnp.transpose(g_cw, (1, 0, 2)).reshape(p_cw.shape[0], N_DEV * wc)
    cw = cw_all[0:CONV_K]
    lw = cw_all[lw_row:lw_row + LRU_K]
    (x2, z, mix, u1, xr, hst), (g3b,) = _mix_core_fwd(
        x1, row(mix_norm), w_in_t, w_out_f, bda, bdx, cw, lw, vec, min(512, T), "mix_core_fwd",
        rider=_forward_comm([g3b_half], [stacked]))
    w3b = g3b.reshape(3, F, D)
    dx3, dob2, d_final_norm, loss_part, h3, dau2, dag2, act2 = _ffn_fwd(
        x2, row(ffn2_norm), w3b, tm_fwd, cf, "ffn2_fwd_loss", head=(row(final_norm), tgt))

    dx2, dgate2, dup2, d_ffn2_norm = _ffn_dgrad(dx3, x2, row(ffn2_norm), dau2, dag2, w3b, tm_ffn, cf, "ffn2_dgrad")
    (dw_gu2,) = _wgrad([dgate2, dup2], h3, tm_w, tf_w, "ffn2_wgrad_gu")
    (dw_d2,) = _wgrad([act2], dob2, tm_w1, tf_w, "ffn2_wgrad_d")
    by_device = lambda a: a.reshape(a.shape[0], N_DEV, rf, D)
    (dx1, sg, dbda, dbdx, dob1, d_mix_norm, dw_in_t, dw_out), parts_f2 = _mix_bwd(
        dx2, z, u1, xr, hst, x1, mix, row(mix_norm), w_in_t, w_out_f, bda, bdx, cw, lw, vec, tm_mix, "mix_bwd",
        rider=_chips_comm([by_device(dw_gu2), by_device(dw_d2)], every_device=True))
    io = [dw_in_t.reshape(1, N_DEV, ri, D), dw_out.reshape(1, N_DEV, ro, D)]
    (dw_d1,), parts_io = _wgrad(
        [act1], dob1, tm_w1, tf_w, "ffn1_wgrad_d", rider=_chips_comm(io, every_device=True))
    sums_d1 = _reduce_level1([by_owner(dw_d1, rf)], "d1")
    dx0, dgate1, dup1, d_ffn1_norm = _ffn_dgrad(dx1, xs, row(ffn1_norm), dau1, dag1, w3a, tm_ffn, cf, "ffn1_dgrad")
    (dw_g1,), parts_d1 = _wgrad([dgate1], h1, tm_w1, tf_w, "ffn1_wgrad_g", rider=_chips_comm(sums_d1))
    sums_g1 = _reduce_level1([by_owner(dw_g1, rf)], "g1")
    small = [d_ffn1_norm, d_mix_norm, d_ffn2_norm, d_final_norm, sg, _diag_blocks(dbda, H).reshape(-1, D),
             _diag_blocks(dbdx, H).reshape(-1, D), loss_part]
    (dw_u1,), summed_and_parts = _wgrad(
        [dup1], h1, tm_w1, tf_w, "ffn1_wgrad_u", rider=_both(_small_sum_comm(small), _chips_comm(sums_g1)))
    summed, parts_g1 = summed_and_parts[:len(small)], summed_and_parts[len(small):]
    sums_u1 = _reduce_level1([by_owner(dw_u1, rf)], "u1")

    g_norm1, g_norm_mix, g_norm2, g_norm_final, g_sg, g_w_a, g_w_x, g_loss = summed
    loss = g_loss[0, 0]
    me = 4 * lax.axis_index("x") + 2 * lax.axis_index("y") + lax.axis_index("c")
    chan = lambda full_g: lax.dynamic_slice_in_dim(full_g, me * wc, wc, axis=1)
    grads = {
        "ffn1_norm": g_norm1.reshape(D), "mix_norm": g_norm_mix.reshape(D), "ffn2_norm": g_norm2.reshape(D),
        "final_norm": g_norm_final.reshape(D),
        "conv_dw_bias": g_sg[G_CB], "conv_ln_g": g_sg[G_LNG], "conv_ln_b": g_sg[G_LNB],
        "lru_conv_b": g_sg[G_LCB], "lru_b_a": g_sg[G_BA], "lru_b_x": g_sg[G_BX], "lru_lambda": g_sg[G_LAM],
        "lru_w_a": g_w_a.reshape(lru_w_a.shape), "lru_w_x": g_w_x.reshape(lru_w_x.shape),
        "conv_dw": chan(g_sg[G_CW:G_CW + CONV_K]), "lru_conv_w": chan(g_sg[G_LW:G_LW + LRU_K]),
    }

    weights = dict(ffn1_norm=ffn1_norm, ffn1_w_gate=ffn1_w_gate, ffn1_w_up=ffn1_w_up, ffn1_w_down=ffn1_w_down, mix_norm=mix_norm, w_in=w_in, conv_dw=conv_dw, conv_dw_bias=conv_dw_bias, conv_ln_g=conv_ln_g, conv_ln_b=conv_ln_b, lru_conv_w=lru_conv_w, lru_conv_b=lru_conv_b, lru_w_a=lru_w_a, lru_b_a=lru_b_a, lru_w_x=lru_w_x, lru_b_x=lru_b_x, lru_lambda=lru_lambda, w_out=w_out, ffn2_norm=ffn2_norm, ffn2_w_gate=ffn2_w_gate, ffn2_w_up=ffn2_w_up, ffn2_w_down=ffn2_w_down, final_norm=final_norm)
    moment1 = dict(ffn1_norm=m_ffn1_norm, ffn1_w_gate=m_ffn1_w_gate, ffn1_w_up=m_ffn1_w_up, ffn1_w_down=m_ffn1_w_down, mix_norm=m_mix_norm, w_in=m_w_in, conv_dw=m_conv_dw, conv_dw_bias=m_conv_dw_bias, conv_ln_g=m_conv_ln_g, conv_ln_b=m_conv_ln_b, lru_conv_w=m_lru_conv_w, lru_conv_b=m_lru_conv_b, lru_w_a=m_lru_w_a, lru_b_a=m_lru_b_a, lru_w_x=m_lru_w_x, lru_b_x=m_lru_b_x, lru_lambda=m_lru_lambda, w_out=m_w_out, ffn2_norm=m_ffn2_norm, ffn2_w_gate=m_ffn2_w_gate, ffn2_w_up=m_ffn2_w_up, ffn2_w_down=m_ffn2_w_down, final_norm=m_final_norm)
    moment2 = dict(ffn1_norm=v_ffn1_norm, ffn1_w_gate=v_ffn1_w_gate, ffn1_w_up=v_ffn1_w_up, ffn1_w_down=v_ffn1_w_down, mix_norm=v_mix_norm, w_in=v_w_in, conv_dw=v_conv_dw, conv_dw_bias=v_conv_dw_bias, conv_ln_g=v_conv_ln_g, conv_ln_b=v_conv_ln_b, lru_conv_w=v_lru_conv_w, lru_conv_b=v_lru_conv_b, lru_w_a=v_lru_w_a, lru_b_a=v_lru_b_a, lru_w_x=v_lru_w_x, lru_b_x=v_lru_b_x, lru_lambda=v_lru_lambda, w_out=v_w_out, ffn2_norm=v_ffn2_norm, ffn2_w_gate=v_ffn2_w_gate, ffn2_w_up=v_ffn2_w_up, ffn2_w_down=v_ffn2_w_down, final_norm=v_final_norm)
    order = list(weights)
    gate_up = {"ffn1_w_gate", "ffn1_w_up", "ffn2_w_gate", "ffn2_w_up"}
    view = lambda n, a: a.T if n in gate_up else a
    operands = lambda n: [view(n, d[n]) for d in (weights, moment1, moment2)]
    delta, new_m, new_v = {}, {}, {}
    six = {"ffn1_w_gate": (parts_g1[0], 0), "ffn1_w_down": (parts_d1[0], 0), "w_out": (parts_io[1], 0),
           "ffn2_w_gate": (parts_f2[0], 0), "ffn2_w_up": (parts_f2[0], 1), "ffn2_w_down": (parts_f2[1], 0)}
    results, parts_u1 = _finish_many([(parts, k, *operands(n)) for n, (parts, k) in six.items()], "finish_six",
                                     rider=_chips_comm(sums_u1))
    for j, n in enumerate(six):
        grads[n], delta[n], new_m[n], new_v[n] = [view(n, r) for r in results[4 * j:4 * j + 4]]
    for n, parts, d_major in (("ffn1_w_up", parts_u1[0], False), ("w_in", parts_io[0], True)):
        grads[n], delta[n], new_m[n], new_v[n] = [
            view(n, r) for r in _finish(parts, 0, *operands(n), d_major, "finish_" + n)]
    big = set(six) | {"ffn1_w_up", "w_in"}
    rest = [n for n in order if n not in big]
    updates = _adamw_each([weights[n] for n in rest], [grads[n] for n in rest], [moment1[n] for n in rest],
                          [moment2[n] for n in rest], "adamw_small")
    for k, n in enumerate(rest):
        delta[n], new_m[n], new_v[n] = updates[k], updates[len(rest) + k], updates[2 * len(rest) + k]

    return (loss, dx0.reshape(x.shape), *[grads[n] for n in order], *[delta[n] for n in order],
            *[new_m[n] for n in order], *[new_v[n] for n in order])
```

```python
import functools
import math

import jax
import jax.numpy as jnp
from jax import lax
from jax.experimental import pallas as pl
from jax.experimental.pallas import tpu as pltpu

F32 = jnp.float32
BF16 = jnp.bfloat16
MESH = pl.DeviceIdType.MESH

N_DEV = 8
N_CHIP = 4
SUBLANES = 8
RMS_EPS = 1e-6
LN_EPS = 1e-5
LRU_C = 8.0
CONV_K = 31
LRU_K = 4
CONV_HALO = 32
LRU_HALO = 8
FFN_RES = 0.5
ADAM_LR, ADAM_B1, ADAM_B2, ADAM_EPS, ADAM_WD, ADAM_STEP = 0.001, 0.9, 0.999, 1e-08, 0.01, 10
GELU_K = math.sqrt(2.0 / math.pi)
GELU_C = 0.044715

MIB = 1024 * 1024
NT = (((1,), (1,)), ((), ()))
NN = (((1,), (0,)), ((), ()))
TN = (((0,), (0,)), ((), ()))

V_CB, V_LNG, V_LNB, V_LCB, V_BA, V_BX, V_LAM = range(0, 7 * SUBLANES, SUBLANES)
G_CW = 0
G_CB, G_LNG, G_LNB = 31, 32, 33
G_LW = 34
G_LCB, G_BA, G_BX, G_LAM = 38, 39, 40, 41
G_ROWS = 48

HBM_SPEC = pl.BlockSpec(memory_space=pltpu.HBM)
VMEM_SPEC = pl.BlockSpec(memory_space=pltpu.VMEM)


def _dot(a, b, dims):
    return lax.dot_general(a, b, dims, preferred_element_type=F32)


def _sigmoid(x):
    return 1.0 / (1.0 + jnp.exp(-x))


def _gelu_parts(x):
    x2 = x * x
    th = jnp.tanh(GELU_K * x * (1.0 + GELU_C * x2))
    gl = 0.5 * x * (1.0 + th)
    dgl = 0.5 * (1.0 + th) + 0.5 * x * (1.0 - th * th) * GELU_K * (1.0 + 3.0 * GELU_C * x2)
    return gl, dgl


def _neg_expm1(y):
    series = -y * (1.0 + y * (1.0 / 2) * (1.0 + y * (1.0 / 3) * (1.0 + y * (1.0 / 4) * (1.0 + y * (1.0 / 5) * (1.0 + y * (1.0 / 6))))))
    return jnp.where(y > -0.25, series, 1.0 - jnp.exp(y))


def _softplus_neg(lam):
    t = -lam
    e = jnp.exp(-jnp.abs(t))
    s = 1.0 + e
    log1p_e = jnp.log(s) - ((s - 1.0) - e) / s
    return jnp.maximum(t, 0.0) + log1p_e


def _rms_stats(xv):
    rstd = lax.rsqrt(jnp.mean(xv * xv, axis=-1, keepdims=True) + RMS_EPS)
    return xv * rstd, rstd


def _rms_bwd(xhat, rstd, g, dh):
    dxhat = dh * g
    dx = rstd * (dxhat - xhat * jnp.mean(dxhat * xhat, axis=-1, keepdims=True))
    return dx, jnp.sum(dh * xhat, axis=0, keepdims=True)


def _row_windows(buf_ref, n_rows, offsets):
    total = buf_ref.shape[0]
    full = buf_ref[...]
    for b in range(SUBLANES):
        offs = [o for o in offsets if o % SUBLANES == b]
        if not offs:
            continue
        assert max(offs) + n_rows <= total
        moved = full if b == 0 else pltpu.roll(full, total - b, 0)
        for o in offs:
            yield o, moved[o - b:o - b + n_rows, :]


def _scan_rows(av, bv, edge, out_ref, reverse=False):
    tm, W = av.shape
    sub = lax.broadcasted_iota(jnp.int32, (tm, W), 0) % SUBLANES
    s = 1
    while s < SUBLANES:
        keep = (sub < SUBLANES - s) if reverse else (sub >= s)
        shift = tm - s if reverse else s
        bv = jnp.where(keep, av * pltpu.roll(bv, shift, 0) + bv, bv)
        av = jnp.where(keep, av * pltpu.roll(av, shift, 0), av)
        s *= 2
    starts = range(0, tm, SUBLANES)
    for r0 in (reversed(starts) if reverse else starts):
        group = av[r0:r0 + SUBLANES, :] * edge + bv[r0:r0 + SUBLANES, :]
        out_ref[r0:r0 + SUBLANES, :] = group
        edge = group[0:1, :] if reverse else group[SUBLANES - 1:SUBLANES, :]
    return edge


class _Comm:
    def __init__(self, arrays, in_specs, out_shapes, out_specs, scratch, start, wait, aliases=None):
        self.arrays, self.in_specs = list(arrays), list(in_specs)
        self.out_shapes, self.out_specs = list(out_shapes), list(out_specs)
        self.scratch, self.start, self.wait = list(scratch), start, wait
        self.aliases = dict(aliases or {})


def _in_hbm(a):
    return pltpu.with_memory_space_constraint(a, pltpu.HBM)


def _operands(comm):
    return [a if spec is VMEM_SPEC else _in_hbm(a) for a, spec in zip(comm.arrays, comm.in_specs)]


def _call(body, *, name, grid, in_specs, out_specs, out_shape, scratch_shapes, vmem_mib, args, rider=None,
          num_scalar_prefetch=0):
    params = pltpu.CompilerParams(dimension_semantics=("arbitrary",) * len(grid), vmem_limit_bytes=vmem_mib * MIB)
    args = [a if k < num_scalar_prefetch else _in_hbm(a) for k, a in enumerate(args)]
    if rider is None:
        return pl.pallas_call(
            body, name=name,
            grid_spec=pltpu.PrefetchScalarGridSpec(
                num_scalar_prefetch=num_scalar_prefetch, grid=grid, in_specs=in_specs, out_specs=out_specs,
                scratch_shapes=scratch_shapes),
            out_shape=out_shape, compiler_params=params)(*args)
    assert num_scalar_prefetch == 0
    n_in, n_out, n_scr = len(in_specs), len(out_specs), len(scratch_shapes)
    r_in, r_out = len(rider.arrays), len(rider.out_shapes)
    n_axes = len(grid)

    def carried(*refs):
        pos = [0]

        def take(n):
            pos[0] += n
            return refs[pos[0] - n:pos[0]]

        ins, r_ins, outs, r_outs, scr, r_scr = take(n_in), take(r_in), take(n_out), take(r_out), take(n_scr), take(len(rider.scratch))
        first = pl.program_id(0) == 0
        last = pl.program_id(0) == grid[0] - 1
        for ax in range(1, n_axes):
            first = first & (pl.program_id(ax) == 0)
            last = last & (pl.program_id(ax) == grid[ax] - 1)

        @pl.when(first)
        def _():
            rider.start(r_ins, r_outs, r_scr)

        body(*ins, *outs, *scr)

        @pl.when(last)
        def _():
            rider.wait(r_ins, r_outs, r_scr)

    res = pl.pallas_call(
        carried, name=name,
        grid=grid,
        in_specs=list(in_specs) + rider.in_specs,
        out_specs=list(out_specs) + rider.out_specs,
        out_shape=list(out_shape) + rider.out_shapes,
        scratch_shapes=list(scratch_shapes) + rider.scratch,
        input_output_aliases={n_in + i: n_out + o for i, o in rider.aliases.items()},
        compiler_params=params)(*args, *_operands(rider))
    return res[:n_out], res[n_out:]


def _run_comm(comm, name):
    n_in, n_out = len(comm.arrays), len(comm.out_shapes)

    def body(*refs):
        ins, outs, scr = refs[:n_in], refs[n_in:n_in + n_out], refs[n_in + n_out:]
        comm.start(ins, outs, scr)
        comm.wait(ins, outs, scr)

    return pl.pallas_call(
        body, name=name,
        in_specs=comm.in_specs, out_specs=comm.out_specs, out_shape=comm.out_shapes,
        scratch_shapes=comm.scratch, input_output_aliases=comm.aliases,
        compiler_params=pltpu.CompilerParams(vmem_limit_bytes=24 * MIB))(*_operands(comm))


def _both(a, b):
    ni, no, ns = len(a.arrays), len(a.out_shapes), len(a.scratch)

    def start(ins, outs, scr):
        a.start(ins[:ni], outs[:no], scr[:ns])
        b.start(ins[ni:], outs[no:], scr[ns:])

    def wait(ins, outs, scr):
        a.wait(ins[:ni], outs[:no], scr[:ns])
        b.wait(ins[ni:], outs[no:], scr[ns:])

    aliases = dict(a.aliases)
    aliases.update({ni + i: no + o for i, o in b.aliases.items()})
    return _Comm(a.arrays + b.arrays, a.in_specs + b.in_specs, a.out_shapes + b.out_shapes,
                 a.out_specs + b.out_specs, a.scratch + b.scratch, start, wait, aliases)


def _place():
    return lax.axis_index("x"), lax.axis_index("y"), lax.axis_index("c")


def _peer(k):
    x, y, c = _place()
    px, py, pc = x ^ ((k >> 2) & 1), y ^ ((k >> 1) & 1), c ^ (k & 1)
    return (px, py, pc), 4 * px + 2 * py + pc


SIBLING = 1
SAME_CORE = (2, 4, 6)
EVERYONE = tuple(range(1, N_DEV))


def _gather_comm(shards, views, out_shapes, relations):
    na = len(shards)

    def copies(ins, outs, scr):
        send_sems, recv_sems, _ = scr
        _, me = _peer(0)
        out = []
        for a in range(na):
            for k in relations[a]:
                peer, theirs = _peer(k)
                send = functools.partial(
                    pltpu.make_async_remote_copy,
                    src_ref=ins[a], dst_ref=views[a](outs[a], me),
                    send_sem=send_sems.at[7 * a + k - 1], recv_sem=recv_sems.at[7 * a + k - 1],
                    device_id=peer, device_id_type=MESH)
                recv = functools.partial(
                    pltpu.make_async_remote_copy,
                    src_ref=ins[a], dst_ref=views[a](outs[a], theirs),
                    send_sem=send_sems.at[7 * a + k - 1], recv_sem=recv_sems.at[7 * a + k - 1],
                    device_id=peer, device_id_type=MESH)
                out.append((send, recv))
        return out

    def local(ins, outs, scr):
        _, me = _peer(0)
        return [pltpu.make_async_copy(ins[a], views[a](outs[a], me), scr[2].at[a]) for a in range(na)]

    def start(ins, outs, scr):
        for cp in local(ins, outs, scr):
            cp.start()
        for send, _ in copies(ins, outs, scr):
            send().start()

    def wait(ins, outs, scr):
        for _, recv in copies(ins, outs, scr):
            recv().wait_recv()
        for send, _ in copies(ins, outs, scr):
            send().wait_send()
        for cp in local(ins, outs, scr):
            cp.wait()

    return _Comm(shards, [HBM_SPEC] * na, out_shapes, [HBM_SPEC] * na,
                 [pltpu.SemaphoreType.DMA((7 * na,)), pltpu.SemaphoreType.DMA((7 * na,)),
                  pltpu.SemaphoreType.DMA((na,))], start, wait)


def _forward_comm(gathered, views):
    na = len(gathered)
    shapes = [jax.ShapeDtypeStruct(g.shape, g.dtype) for g in gathered]

    def copies(outs, scr):
        send_sems, recv_sems = scr
        sibling, _ = _peer(SIBLING)
        out = []
        for a in range(na):
            for n, k in enumerate(SAME_CORE):
                _, mine = _peer(k)
                _, theirs = _peer(k ^ SIBLING)
                send = functools.partial(
                    pltpu.make_async_remote_copy,
                    src_ref=views[a](outs[a], mine), dst_ref=views[a](outs[a], mine),
                    send_sem=send_sems.at[3 * a + n], recv_sem=recv_sems.at[3 * a + n],
                    device_id=sibling, device_id_type=MESH)
                recv = functools.partial(
                    pltpu.make_async_remote_copy,
                    src_ref=views[a](outs[a], mine), dst_ref=views[a](outs[a], theirs),
                    send_sem=send_sems.at[3 * a + n], recv_sem=recv_sems.at[3 * a + n],
                    device_id=sibling, device_id_type=MESH)
                out.append((send, recv))
        return out

    def start(ins, outs, scr):
        for send, _ in copies(outs, scr):
            send().start()

    def wait(ins, outs, scr):
        for _, recv in copies(outs, scr):
            recv().wait_recv()
        for send, _ in copies(outs, scr):
            send().wait_send()

    return _Comm(gathered, [HBM_SPEC] * na, shapes, [HBM_SPEC] * na,
                 [pltpu.SemaphoreType.DMA((3 * na,)), pltpu.SemaphoreType.DMA((3 * na,))], start, wait,
                 aliases={a: a for a in range(na)})


Y_NEIGHBOUR, X_NEIGHBOUR, DIAGONAL = SAME_CORE


def _all_gather(stack, name):
    K, rows = stack.shape[0], stack.shape[1]
    half = rows // 2
    assert half % 16 == 0
    pieces = [(k, pl.ds(h * half, half)) for k in range(K) for h in range(2)]
    na = len(pieces)
    views = [lambda r, j, k=k, part=part: r.at[k, j, part] for k, part in pieces]
    out_shape = jax.ShapeDtypeStruct((K, N_DEV) + stack.shape[1:], stack.dtype)
    near = (SIBLING, Y_NEIGHBOUR, X_NEIGHBOUR)
    level1 = _gather_comm([stack] * na, views, [out_shape] * na, [near] * na)

    def body(stack_ref, out_ref, *scratch):
        ins, outs = [stack_ref.at[k, part] for k, part in pieces], [out_ref] * na
        send_sems, recv_sems, local_sems, fwd_send, fwd_recv, relay_send, relay_recv = scratch
        sibling, _ = _peer(SIBLING)
        c = lax.axis_index("c")
        level1.start(ins, outs, (send_sems, recv_sems, local_sems))

        def block_copy(a, block, to, send_sem, recv_sem):
            return pltpu.make_async_remote_copy(
                src_ref=views[a](outs[a], block), dst_ref=views[a](outs[a], block),
                send_sem=send_sem, recv_sem=recv_sem, device_id=to, device_id_type=MESH)

        def to_sibling(a, n, k):
            _, mine = _peer(k)
            _, theirs = _peer(k ^ SIBLING)
            fwd = block_copy(a, mine, sibling, fwd_send.at[3 * a + n], fwd_recv.at[3 * a + n])
            fwd.start()
            return fwd, block_copy(a, theirs, sibling, fwd_send.at[3 * a + n], fwd_recv.at[3 * a + n])

        passed, landing = [], []
        for a in range(na):
            for n, k in enumerate((Y_NEIGHBOUR, X_NEIGHBOUR)):
                peer, origin = _peer(k)
                pltpu.make_async_remote_copy(
                    src_ref=ins[a], dst_ref=views[a](outs[a], origin),
                    send_sem=send_sems.at[7 * a + k - 1], recv_sem=recv_sems.at[7 * a + k - 1],
                    device_id=peer, device_id_type=MESH).wait_recv()

                @pl.when(c == (0 if k == X_NEIGHBOUR else 1))
                def _():
                    other, _ = _peer(DIAGONAL ^ k)
                    block_copy(a, origin, other, relay_send.at[a], relay_recv.at[a]).start()

                fwd, lands = to_sibling(a, n, k)
                passed.append(fwd)
                landing.append(lands)
        for a in range(na):
            _, far = _peer(DIAGONAL)
            block_copy(a, far, sibling, relay_send.at[a], relay_recv.at[a]).wait_recv()
            fwd, lands = to_sibling(a, 2, DIAGONAL)
            passed.append(fwd)
            landing.append(lands)
        for a in range(na):
            _, theirs = _peer(SIBLING)
            pltpu.make_async_remote_copy(
                src_ref=ins[a], dst_ref=views[a](outs[a], theirs),
                send_sem=send_sems.at[7 * a + SIBLING - 1], recv_sem=recv_sems.at[7 * a + SIBLING - 1],
                device_id=sibling, device_id_type=MESH).wait_recv()
        for cp in landing:
            cp.wait_recv()
        for cp in passed:
            cp.wait_send()
        _, me = _peer(0)
        for a in range(na):
            block_copy(a, me, sibling, relay_send.at[a], relay_recv.at[a]).wait_send()
            for k in near:
                peer, _ = _peer(k)
                pltpu.make_async_remote_copy(
                    src_ref=ins[a], dst_ref=views[a](outs[a], me),
                    send_sem=send_sems.at[7 * a + k - 1], recv_sem=recv_sems.at[7 * a + k - 1],
                    device_id=peer, device_id_type=MESH).wait_send()
            pltpu.make_async_copy(ins[a], views[a](outs[a], me), local_sems.at[a]).wait()

    return pl.pallas_call(
        body, name=name,
        in_specs=[HBM_SPEC], out_specs=HBM_SPEC, out_shape=out_shape,
        scratch_shapes=level1.scratch + [pltpu.SemaphoreType.DMA((3 * na,)), pltpu.SemaphoreType.DMA((3 * na,)),
                                         pltpu.SemaphoreType.DMA((na,)), pltpu.SemaphoreType.DMA((na,))],
    )(_in_hbm(stack))


def _sibling_comm(grads):
    na = len(grads)
    shapes = [jax.ShapeDtypeStruct(g.shape[:2] + g.shape[3:], g.dtype) for g in grads]

    def copies(ins, outs, scr):
        x, y, c = _place()
        return [pltpu.make_async_remote_copy(
            src_ref=ins[a].at[:, :, 1 - c], dst_ref=outs[a],
            send_sem=scr[0].at[a], recv_sem=scr[1].at[a],
            device_id=(x, y, 1 - c), device_id_type=MESH) for a in range(na)]

    def start(ins, outs, scr):
        for cp in copies(ins, outs, scr):
            cp.start()

    def wait(ins, outs, scr):
        for cp in copies(ins, outs, scr):
            cp.wait()

    return _Comm(grads, [HBM_SPEC] * na, shapes, [HBM_SPEC] * na,
                 [pltpu.SemaphoreType.DMA((na,)), pltpu.SemaphoreType.DMA((na,))], start, wait)


def _chips_comm(sums, every_device=False):
    na = len(sums)
    shapes = [jax.ShapeDtypeStruct(s.shape, s.dtype) for s in sums]
    relations = EVERYONE if every_device else SAME_CORE
    nr = len(relations)

    def block(px, py, pc):
        return 4 * px + 2 * py + pc if every_device else 2 * px + py

    def copies(ins, outs, scr):
        mine = block(*_place())
        out = []
        for a in range(na):
            for n, k in enumerate(relations):
                peer, _ = _peer(k)
                theirs = block(*peer)
                send = functools.partial(
                    pltpu.make_async_remote_copy,
                    src_ref=ins[a].at[:, theirs], dst_ref=outs[a].at[:, mine],
                    send_sem=scr[0].at[nr * a + n], recv_sem=scr[1].at[nr * a + n],
                    device_id=peer, device_id_type=MESH)
                recv = functools.partial(
                    pltpu.make_async_remote_copy,
                    src_ref=ins[a].at[:, mine], dst_ref=outs[a].at[:, theirs],
                    send_sem=scr[0].at[nr * a + n], recv_sem=scr[1].at[nr * a + n],
                    device_id=peer, device_id_type=MESH)
                out.append((send, recv))
        return out

    def local(ins, outs, scr):
        mine = block(*_place())
        return [pltpu.make_async_copy(ins[a].at[:, mine], outs[a].at[:, mine], scr[2].at[a]) for a in range(na)]

    def start(ins, outs, scr):
        for cp in local(ins, outs, scr):
            cp.start()
        for send, _ in copies(ins, outs, scr):
            send().start()

    def wait(ins, outs, scr):
        for _, recv in copies(ins, outs, scr):
            recv().wait_recv()
        for send, _ in copies(ins, outs, scr):
            send().wait_send()
        for cp in local(ins, outs, scr):
            cp.wait()

    return _Comm(sums, [HBM_SPEC] * na, shapes, [HBM_SPEC] * na,
                 [pltpu.SemaphoreType.DMA((nr * na,)), pltpu.SemaphoreType.DMA((nr * na,)),
                  pltpu.SemaphoreType.DMA((na,))], start, wait)


def _small_sum_comm(arrays):
    na = len(arrays)

    def copies(ins, scr):
        bufs, send_sems, recv_sems = scr[:na], scr[na], scr[na + 1]
        _, me = _peer(0)
        out = []
        for a in range(na):
            for k in EVERYONE:
                peer, theirs = _peer(k)
                sems = dict(send_sem=send_sems.at[7 * a + k - 1], recv_sem=recv_sems.at[7 * a + k - 1])
                send = functools.partial(
                    pltpu.make_async_remote_copy,
                    src_ref=ins[a], dst_ref=bufs[a].at[me], device_id=peer, device_id_type=MESH, **sems)
                recv = functools.partial(
                    pltpu.make_async_remote_copy,
                    src_ref=ins[a], dst_ref=bufs[a].at[theirs], device_id=peer, device_id_type=MESH, **sems)
                out.append((send, recv))
        return out

    def start(ins, outs, scr):
        _, me = _peer(0)
        for a in range(na):
            scr[a][me] = ins[a][...]
        for send, _ in copies(ins, scr):
            send().start()

    def wait(ins, outs, scr):
        for _, recv in copies(ins, scr):
            recv().wait_recv()
        for send, _ in copies(ins, scr):
            send().wait_send()
        for a in range(na):
            acc = scr[a][0]
            for j in range(1, N_DEV):
                acc = acc + scr[a][j]
            outs[a][...] = acc

    return _Comm(arrays, [VMEM_SPEC] * na, [jax.ShapeDtypeStruct(s.shape, F32) for s in arrays], [VMEM_SPEC] * na,
                 [pltpu.VMEM((N_DEV,) + s.shape, F32) for s in arrays]
                 + [pltpu.SemaphoreType.DMA((7 * na,)), pltpu.SemaphoreType.DMA((7 * na,))], start, wait)


def _prep_weights(ffn1, ffn2, w_in, w_out, name):
    rf, D = ffn1[2].shape
    ri, ro = w_in.shape[1], w_out.shape[0]

    def body(g1, u1, d1, g2, u2, d2, wi, wo, p1_ref, p2_ref, pi_ref, po_ref):
        for p_ref, shards in ((p1_ref, (g1, u1, d1)), (p2_ref, (g2, u2, d2))):
            for k, shard in enumerate(shards):
                p_ref[k] = shard[...].astype(BF16)
        pi_ref[...] = wi[...].T.astype(BF16)
        po_ref[...] = wo[...].astype(BF16)

    args = (*ffn1, *ffn2, w_in, w_out)
    whole = lambda shape: pl.BlockSpec(shape, lambda i: (0,) * len(shape))
    out_shapes = [(3, rf, D), (3, rf, D), (ri, D), (ro, D)]
    return _call(
        body, name=name, grid=(1,),
        in_specs=[whole(a.shape) for a in args], out_specs=[whole(s) for s in out_shapes],
        out_shape=[jax.ShapeDtypeStruct(s, BF16) for s in out_shapes],
        scratch_shapes=[], vmem_mib=48, args=args)


def _load_weights(w_hbm, w_vmem, sem):
    @pl.when(pl.program_id(0) == 0)
    def _():
        copies = [pltpu.make_async_copy(w_hbm.at[k], w_vmem.at[k], sem.at[k]) for k in range(3)]
        for cp in copies:
            cp.start()
        for cp in copies:
            cp.wait()


def _ffn_fwd(x, g, w3, tm, cf, name, rider=None, head=None):
    T, D = x.shape
    F = w3.shape[1]
    n_head = 0 if head is None else 2

    def body(x_ref, g_ref, w_hbm, *refs):
        head_refs, refs = refs[:n_head], refs[n_head:]
        if head is None:
            (xo_ref, h_ref, dau_ref, dag_ref, act_ref, wv, sem) = refs
        else:
            (dx_ref, dob_ref, dgf_ref, loss_ref, h_ref, dau_ref, dag_ref, act_ref, wv, sem) = refs
        _load_weights(w_hbm, wv, sem)
        xhat, _ = _rms_stats(x_ref[...])
        hb = (xhat * g_ref[...]).astype(BF16)
        h_ref[...] = hb
        for lo in range(0, F, cf):
            gate = _dot(hb, wv[0, lo:lo + cf, :], NT)
            up = _dot(hb, wv[1, lo:lo + cf, :], NT)
            sig = _sigmoid(gate)
            silu = gate * sig
            dau_ref[:, lo:lo + cf] = silu.astype(BF16)
            dag_ref[:, lo:lo + cf] = (up * (sig * (1.0 + gate * (1.0 - sig)))).astype(BF16)
            act_ref[:, lo:lo + cf] = (silu * up).astype(BF16)
        x_out = x_ref[...] + FFN_RES * _dot(act_ref[...], wv[2], NN)
        if head is None:
            xo_ref[...] = x_out
            return

        @pl.when(pl.program_id(0) == 0)
        def _():
            dgf_ref[...] = jnp.zeros_like(dgf_ref)
            loss_ref[...] = jnp.zeros_like(loss_ref)

        gf_ref, tgt_ref = head_refs
        yhat, rstd = _rms_stats(x_out)
        gf = gf_ref[...]
        err = yhat * gf - tgt_ref[...]
        loss_ref[...] += (0.5 / D) * jnp.sum(err * err)
        dx, dgf = _rms_bwd(yhat, rstd, gf, err * (1.0 / D))
        dx_ref[...] = dx
        dob_ref[...] = (FFN_RES * dx).astype(BF16)
        dgf_ref[...] += dgf

    row = pl.BlockSpec((tm, D), lambda i: (i, 0))
    hid = pl.BlockSpec((tm, F), lambda i: (i, 0))
    vec = pl.BlockSpec((1, D), lambda i: (0, 0))
    row_f32, row_bf16 = jax.ShapeDtypeStruct((T, D), F32), jax.ShapeDtypeStruct((T, D), BF16)
    if head is None:
        first_specs, first_shapes = [row], [row_f32]
    else:
        first_specs = [row, row, vec, pl.BlockSpec((1, 128), lambda i: (0, 0))]
        first_shapes = [row_f32, row_bf16, jax.ShapeDtypeStruct((1, D), F32), jax.ShapeDtypeStruct((1, 128), F32)]
    return _call(
        body, name=name, grid=(T // tm,),
        in_specs=[row, vec, HBM_SPEC] + ([] if head is None else [vec, row]),
        out_specs=first_specs + [row, hid, hid, hid],
        out_shape=first_shapes + [row_bf16] + [jax.ShapeDtypeStruct((T, F), BF16)] * 3,
        scratch_shapes=[pltpu.VMEM((3, F, D), BF16), pltpu.SemaphoreType.DMA((3,))],
        vmem_mib=60, args=(x, g, w3) + (() if head is None else tuple(head)), rider=rider)


def _ffn_dgrad(dout, x, g, dau, dag, w3, tm, cf, name, rider=None):
    T, D = x.shape
    F = w3.shape[1]

    def body(do_ref, x_ref, g_ref, dau_ref, dag_ref, w_hbm, dx_ref, dgate_ref, dup_ref, dg_ref, wv, sem):
        _load_weights(w_hbm, wv, sem)

        @pl.when(pl.program_id(0) == 0)
        def _():
            dg_ref[...] = jnp.zeros_like(dg_ref)

        dob = (FFN_RES * do_ref[...]).astype(BF16)
        for lo in range(0, F, cf):
            dact = _dot(dob, wv[2, lo:lo + cf, :], NT)
            dup_ref[:, lo:lo + cf] = (dact * dau_ref[:, lo:lo + cf].astype(F32)).astype(BF16)
            dgate_ref[:, lo:lo + cf] = (dact * dag_ref[:, lo:lo + cf].astype(F32)).astype(BF16)
        dh = _dot(dgate_ref[...], wv[0], NN) + _dot(dup_ref[...], wv[1], NN)
        xhat, rstd = _rms_stats(x_ref[...])
        dx, dg = _rms_bwd(xhat, rstd, g_ref[...], dh)
        dx_ref[...] = do_ref[...] + dx
        dg_ref[...] += dg

    row = pl.BlockSpec((tm, D), lambda i: (i, 0))
    hid = pl.BlockSpec((tm, F), lambda i: (i, 0))
    vec = pl.BlockSpec((1, D), lambda i: (0, 0))
    return _call(
        body, name=name, grid=(T // tm,),
        in_specs=[row, row, vec, hid, hid, HBM_SPEC],
        out_specs=[row, hid, hid, vec],
        out_shape=[jax.ShapeDtypeStruct((T, D), F32), jax.ShapeDtypeStruct((T, F), BF16),
                   jax.ShapeDtypeStruct((T, F), BF16), jax.ShapeDtypeStruct((1, D), F32)],
        scratch_shapes=[pltpu.VMEM((3, F, D), BF16), pltpu.SemaphoreType.DMA((3,))],
        vmem_mib=60, args=(dout, x, g, dau, dag, w3), rider=rider)


def _wgrad(lhs, rhs, tm, tf, name, rider=None):
    T, F = lhs[0].shape
    D = rhs.shape[1]
    K = len(lhs)

    def body(*refs):
        lhs_refs, rhs_ref, dw_ref, accs = refs[:K], refs[K], refs[K + 1], refs[K + 2:]
        i = pl.program_id(1)

        @pl.when(i == 0)
        def _():
            for acc in accs:
                acc[...] = jnp.zeros_like(acc)

        rv = rhs_ref[...]
        for acc, lhs_ref in zip(accs, lhs_refs):
            acc[...] += _dot(lhs_ref[...], rv, TN)

        @pl.when(i == pl.num_programs(1) - 1)
        def _():
            for k, acc in enumerate(accs):
                dw_ref[k] = acc[...].astype(BF16)

    hid = pl.BlockSpec((tm, tf), lambda f, i: (i, f))
    return _call(
        body, name=name, grid=(F // tf, T // tm),
        in_specs=[hid] * K + [pl.BlockSpec((tm, D), lambda f, i: (i, 0))],
        out_specs=[pl.BlockSpec((K, tf, D), lambda f, i: (0, f, 0))],
        out_shape=[jax.ShapeDtypeStruct((K, F, D), BF16)],
        scratch_shapes=[pltpu.VMEM((tf, D), F32)] * K,
        vmem_mib=56, args=(*lhs, rhs), rider=rider)


def _lru_gates(xr, bda_ref, bdx_ref, vec_ref):
    xrb = xr.astype(BF16)
    r = _sigmoid(_dot(xrb, bda_ref[...], NN) + vec_ref[V_BA:V_BA + 1, :])
    ig = _sigmoid(_dot(xrb, bdx_ref[...], NN) + vec_ref[V_BX:V_BX + 1, :])
    sp = _softplus_neg(vec_ref[V_LAM:V_LAM + 1, :])
    log_a = (-LRU_C * sp) * r
    a = jnp.exp(log_a)
    mult = jnp.sqrt(_neg_expm1(2.0 * log_a))
    return xrb, r, ig, sp, a, mult


def _layernorm_stats(u1):
    xc = u1 - jnp.mean(u1, axis=-1, keepdims=True)
    rs = lax.rsqrt(jnp.mean(xc * xc, axis=-1, keepdims=True) + LN_EPS)
    return xc * rs, rs


def _mix_core_fwd(x1, g, w_in_t, w_out, bda, bdx, cw, lw, vec, tm, name, rider=None):
    T, D = x1.shape
    W = cw.shape[1]
    assert tm >= CONV_HALO and w_in_t.shape[0] == 4 * W

    def body(x1_ref, g_ref, wi_ref, wo_ref, bda_ref, bdx_ref, cw_ref, lw_ref, vec_ref,
             x2_ref, z_ref, mix_ref, u1_ref, xr_ref, hst_ref, ubuf, rbuf, hc):
        @pl.when(pl.program_id(0) == 0)
        def _():
            ubuf[0:CONV_HALO, :] = jnp.zeros((CONV_HALO, W), F32)
            rbuf[0:LRU_HALO, :] = jnp.zeros((LRU_HALO, W), F32)
            hc[...] = jnp.zeros_like(hc)

        xhat, _ = _rms_stats(x1_ref[...])
        z_ref[...] = _dot((xhat * g_ref[...]).astype(BF16), wi_ref[...], NT)

        ubuf[CONV_HALO:CONV_HALO + tm, :] = z_ref[:, 0:W] * _sigmoid(z_ref[:, W:2 * W])
        u1 = jnp.zeros((tm, W), F32) + vec_ref[V_CB:V_CB + 1, :]
        base = CONV_HALO - (CONV_K - 1)
        for off, win in _row_windows(ubuf, tm, range(base, base + CONV_K)):
            u1 = u1 + cw_ref[off - base:off - base + 1, :] * win
        ubuf[0:CONV_HALO, :] = ubuf[tm:tm + CONV_HALO, :]
        u1_ref[...] = u1
        xh, _ = _layernorm_stats(u1)
        u2 = xh * vec_ref[V_LNG:V_LNG + 1, :] + vec_ref[V_LNB:V_LNB + 1, :]
        ub = (u2 * _sigmoid(u2)).astype(BF16)
        mix_ref[:, 0:W] = ub

        rbuf[LRU_HALO:LRU_HALO + tm, :] = z_ref[:, 2 * W:3 * W]
        xr = jnp.zeros((tm, W), F32) + vec_ref[V_LCB:V_LCB + 1, :]
        for k in range(LRU_K):
            off = LRU_HALO - (LRU_K - 1) + k
            xr = xr + lw_ref[k:k + 1, :] * rbuf[off:off + tm, :]
        rbuf[0:LRU_HALO, :] = rbuf[tm:tm + LRU_HALO, :]
        xr_ref[...] = xr
        _, _, ig, _, a, mult = _lru_gates(xr, bda_ref, bdx_ref, vec_ref)
        hc[0:1, :] = _scan_rows(a, mult * (ig * xr), hc[0:1, :], hst_ref)
        gl, _ = _gelu_parts(z_ref[:, 3 * W:4 * W])
        yb = (hst_ref[...] * gl).astype(BF16)
        mix_ref[:, W:2 * W] = yb

        x2_ref[...] = x1_ref[...] + _dot(ub, wo_ref[0:W, :], NN) + _dot(yb, wo_ref[W:2 * W, :], NN)

    full = lambda a: pl.BlockSpec(a.shape, lambda i: (0,) * a.ndim)
    tile = lambda n: pl.BlockSpec((tm, n), lambda i: (i, 0))
    return _call(
        body, name=name, grid=(T // tm,),
        in_specs=[tile(D), full(g), full(w_in_t), full(w_out), full(bda), full(bdx), full(cw), full(lw), full(vec)],
        out_specs=[tile(D), tile(4 * W), tile(2 * W), tile(W), tile(W), tile(W)],
        out_shape=[jax.ShapeDtypeStruct((T, D), F32), jax.ShapeDtypeStruct((T, 4 * W), F32),
                   jax.ShapeDtypeStruct((T, 2 * W), BF16), jax.ShapeDtypeStruct((T, W), F32),
                   jax.ShapeDtypeStruct((T, W), F32), jax.ShapeDtypeStruct((T, W), F32)],
        scratch_shapes=[pltpu.VMEM((tm + CONV_HALO, W), F32), pltpu.VMEM((tm + LRU_HALO, W), F32),
                        pltpu.VMEM((8, W), F32)],
        vmem_mib=56, args=(x1, g, w_in_t, w_out, bda, bdx, cw, lw, vec), rider=rider)


def _mix_bwd(dx2, z, u1, xr, hst, x1, mix, g, w_in_t, w_out, bda, bdx, cw, lw, vec, tm, name, rider=None):
    T, D = dx2.shape
    W = cw.shape[1]
    nt = T // tm
    assert tm >= CONV_HALO and tm % CONV_HALO == 0

    def body(dx_ref, z_ref, zh_ref, u1_ref, xr_ref, h_ref, hh_ref, wo_ref, bda_ref, bdx_ref, cw_ref, lw_ref, vec_ref,
             x1_ref, mix_ref, g_ref, wi_ref,
             dx1_ref, sg_ref, dbda_ref, dbdx_ref, dob_ref, dg_ref, dwi_ref, dwo_ref,
             u0buf, du1buf, rxbuf, dxrbuf, gbuf, gc, spacc, dz_ref, ai_ref, ao_ref):
        i = pl.program_id(0)
        first = i == nt - 1
        row = lax.broadcasted_iota(jnp.int32, (tm, W), 0)

        @pl.when(i == 0)
        def _():
            sg_ref[...] = jnp.zeros_like(sg_ref)
            dbda_ref[...] = jnp.zeros_like(dbda_ref)
            dbdx_ref[...] = jnp.zeros_like(dbdx_ref)
            du1buf[tm:tm + CONV_HALO, :] = jnp.zeros((CONV_HALO, W), F32)
            dxrbuf[tm:tm + LRU_HALO, :] = jnp.zeros((LRU_HALO, W), F32)
            gc[...] = jnp.zeros_like(gc)
            spacc[...] = jnp.zeros_like(spacc)
            dg_ref[...] = jnp.zeros_like(dg_ref)
            ai_ref[...] = jnp.zeros_like(ai_ref)
            ao_ref[...] = jnp.zeros_like(ao_ref)

        def accum(r, val):
            sg_ref[r:r + 1, :] += jnp.sum(val, axis=0, keepdims=True)

        x1hat, x1rstd = _rms_stats(x1_ref[...])
        gain = g_ref[...]
        hb = (x1hat * gain).astype(BF16)

        def in_proj_bwd(lo, hi):
            dzb = dz_ref[:, lo:hi]
            ai_ref[lo:hi, :] += _dot(dzb, hb, TN)
            return _dot(dzb, wi_ref[lo:hi, :], NN)

        dxb = dx_ref[...].astype(BF16)
        ao_ref[...] += _dot(mix_ref[...], dxb, TN)
        dmix = _dot(dxb, wo_ref[...], NT)
        d_u = dmix[:, 0:W]
        d_yr = dmix[:, W:2 * W]

        xh, rs = _layernorm_stats(u1_ref[...])
        ln_g = vec_ref[V_LNG:V_LNG + 1, :]
        u2 = xh * ln_g + vec_ref[V_LNB:V_LNB + 1, :]
        s2 = _sigmoid(u2)
        d_u2 = d_u * (s2 * (1.0 + u2 * (1.0 - s2)))
        accum(G_LNG, d_u2 * xh)
        accum(G_LNB, d_u2)
        d_xh = d_u2 * ln_g
        d_u1 = rs * (d_xh - jnp.mean(d_xh, axis=-1, keepdims=True)
                     - xh * jnp.mean(d_xh * xh, axis=-1, keepdims=True))
        accum(G_CB, d_u1)
        halo_on = jnp.where(first, 0.0, 1.0)
        u0buf[0:CONV_HALO, :] = halo_on * (zh_ref[:, 0:W] * _sigmoid(zh_ref[:, W:2 * W]))
        cv = z_ref[:, 0:W]
        sgc = _sigmoid(z_ref[:, W:2 * W])
        u0buf[CONV_HALO:CONV_HALO + tm, :] = cv * sgc
        du1buf[0:tm, :] = d_u1
        base = CONV_HALO - (CONV_K - 1)
        for off, win in _row_windows(u0buf, tm, range(base, base + CONV_K)):
            accum(G_CW + off - base, d_u1 * win)
        d_u0 = jnp.zeros((tm, W), F32)
        for off, win in _row_windows(du1buf, tm, range(0, CONV_K)):
            d_u0 = d_u0 + cw_ref[CONV_K - 1 - off:CONV_K - off, :] * win
        du1buf[tm:tm + CONV_HALO, :] = du1buf[0:CONV_HALO, :]
        dz_ref[:, 0:W] = (d_u0 * sgc).astype(BF16)
        dz_ref[:, W:2 * W] = (d_u0 * cv * (sgc * (1.0 - sgc))).astype(BF16)
        dh = in_proj_bwd(0, 2 * W)

        xrv = xr_ref[...]
        xrb, r, ig, sp, a, mult = _lru_gates(xrv, bda_ref, bdx_ref, vec_ref)
        h = h_ref[...]
        gl, dgl = _gelu_parts(z_ref[:, 3 * W:4 * W])
        dz_ref[:, 3 * W:4 * W] = (d_yr * h * dgl).astype(BF16)
        dh = dh + in_proj_bwd(3 * W, 4 * W)
        a_next = jnp.where(row == tm - 1, 1.0, pltpu.roll(a, tm - 1, 0))
        g_first = _scan_rows(a_next, d_yr * gl, gc[0:1, :], gbuf, reverse=True)
        g = gbuf[...]
        gc[0:1, :] = a[0:1, :] * g_first
        hprev = jnp.where(row == 0, halo_on * hh_ref[LRU_HALO - 1:LRU_HALO, :], pltpu.roll(h, 1, 0))
        d_log_a = (g * hprev) * a - (g * ig * xrv) * (a * a) / mult
        d_ig = g * mult * xrv
        d_xr = g * mult * ig
        spacc[0:1, :] += jnp.sum(d_log_a * r, axis=0, keepdims=True)
        d_pa32 = (d_log_a * (-LRU_C * sp)) * (r * (1.0 - r))
        d_px32 = d_ig * (ig * (1.0 - ig))
        accum(G_BA, d_pa32)
        accum(G_BX, d_px32)
        d_pa = d_pa32.astype(BF16)
        d_px = d_px32.astype(BF16)
        d_xr = d_xr + _dot(d_pa, bda_ref[...], NT) + _dot(d_px, bdx_ref[...], NT)
        dbda_ref[...] += _dot(xrb, d_pa, TN)
        dbdx_ref[...] += _dot(xrb, d_px, TN)
        accum(G_LCB, d_xr)
        rxbuf[0:LRU_HALO, :] = halo_on * zh_ref[CONV_HALO - LRU_HALO:CONV_HALO, 2 * W:3 * W]
        rxbuf[LRU_HALO:LRU_HALO + tm, :] = z_ref[:, 2 * W:3 * W]
        dxrbuf[0:tm, :] = d_xr
        d_rx = jnp.zeros((tm, W), F32)
        for k in range(LRU_K):
            off = LRU_HALO - (LRU_K - 1) + k
            accum(G_LW + k, d_xr * rxbuf[off:off + tm, :])
            d_rx = d_rx + lw_ref[k:k + 1, :] * dxrbuf[LRU_K - 1 - k:LRU_K - 1 - k + tm, :]
        dxrbuf[tm:tm + LRU_HALO, :] = dxrbuf[0:LRU_HALO, :]
        dz_ref[:, 2 * W:3 * W] = d_rx.astype(BF16)
        dh = dh + in_proj_bwd(2 * W, 3 * W)

        dx, dg = _rms_bwd(x1hat, x1rstd, gain, dh)
        dx1 = dx_ref[...] + dx
        dx1_ref[...] = dx1
        dob_ref[...] = (FFN_RES * dx1).astype(BF16)
        dg_ref[...] += dg

        @pl.when(first)
        def _():
            lam = vec_ref[V_LAM:V_LAM + 1, :]
            sg_ref[G_LAM:G_LAM + 1, :] = LRU_C * _sigmoid(-lam) * spacc[0:1, :]
            dwi_ref[...] = ai_ref[...].astype(BF16)
            dwo_ref[...] = ao_ref[...].astype(BF16)

    full = lambda a: pl.BlockSpec(a.shape, lambda i: (0,) * a.ndim)
    tile = lambda n: pl.BlockSpec((tm, n), lambda i: (nt - 1 - i, 0))
    halo = lambda rows, n: pl.BlockSpec(
        (rows, n), lambda i: (jnp.maximum((nt - 1 - i) * (tm // rows) - 1, 0), 0))
    const = lambda r, c: pl.BlockSpec((r, c), lambda i: (0, 0))
    return _call(
        body, name=name, grid=(nt,),
        in_specs=[tile(D), tile(4 * W), halo(CONV_HALO, 4 * W), tile(W), tile(W), tile(W), halo(LRU_HALO, W),
                  full(w_out), full(bda), full(bdx), full(cw), full(lw), full(vec),
                  tile(D), tile(2 * W), full(g), full(w_in_t)],
        out_specs=[tile(D), const(G_ROWS, W), const(W, W), const(W, W),
                   tile(D), const(1, D), const(4 * W, D), const(2 * W, D)],
        out_shape=[jax.ShapeDtypeStruct((T, D), F32), jax.ShapeDtypeStruct((G_ROWS, W), F32),
                   jax.ShapeDtypeStruct((W, W), F32), jax.ShapeDtypeStruct((W, W), F32),
                   jax.ShapeDtypeStruct((T, D), BF16), jax.ShapeDtypeStruct((1, D), F32),
                   jax.ShapeDtypeStruct((4 * W, D), BF16), jax.ShapeDtypeStruct((2 * W, D), BF16)],
        scratch_shapes=[pltpu.VMEM((tm + CONV_HALO, W), F32), pltpu.VMEM((tm + CONV_HALO, W), F32),
                        pltpu.VMEM((tm + LRU_HALO, W), F32), pltpu.VMEM((tm + LRU_HALO, W), F32),
                        pltpu.VMEM((tm, W), F32), pltpu.VMEM((8, W), F32), pltpu.VMEM((8, W), F32),
                        pltpu.VMEM((tm, 4 * W), BF16), pltpu.VMEM((4 * W, D), F32), pltpu.VMEM((2 * W, D), F32)],
        vmem_mib=60, args=(dx2, z, z, u1, xr, hst, hst, w_out, bda, bdx, cw, lw, vec, x1, mix, g, w_in_t),
        rider=rider)


def _pair_add(full, recv, name):
    K, _, _, rows, D = full.shape

    def body(c_ref, a_ref, b_ref, o_ref):
        o_ref[...] = (a_ref[...].astype(F32) + b_ref[...].astype(F32)).astype(BF16)

    c = lax.axis_index("c").astype(jnp.int32).reshape((1,))
    return _call(
        body, name=name, grid=(K, N_CHIP), num_scalar_prefetch=1,
        in_specs=[pl.BlockSpec((None, None, None, rows, D), lambda k, q, c_ref: (k, q, c_ref[0], 0, 0)),
                  pl.BlockSpec((None, None, rows, D), lambda k, q, c_ref: (k, q, 0, 0))],
        out_specs=pl.BlockSpec((None, None, rows, D), lambda k, q, c_ref: (k, q, 0, 0)),
        out_shape=jax.ShapeDtypeStruct(recv.shape, BF16),
        scratch_shapes=[], vmem_mib=16, args=(c, full, recv))


def _adamw_update(wv, gv, mv, vv):
    m2 = ADAM_B1 * mv + (1.0 - ADAM_B1) * gv
    v2 = ADAM_B2 * vv + (1.0 - ADAM_B2) * (gv * gv)
    m_hat = m2 / (1.0 - ADAM_B1 ** ADAM_STEP)
    v_hat = v2 / (1.0 - ADAM_B2 ** ADAM_STEP)
    return -ADAM_LR * (m_hat / (jnp.sqrt(v_hat) + ADAM_EPS) + ADAM_WD * wv), m2, v2


def _finish(parts, k, w, m, v, transpose, name):
    _, n_parts, rows, D = parts.shape

    def body(p_ref, w_ref, m_ref, v_ref, g_ref, d_ref, mo_ref, vo_ref):
        acc = p_ref[0].astype(F32)
        for q in range(1, n_parts):
            acc = acc + p_ref[q].astype(F32)
        gv = acc.T if transpose else acc
        g_ref[...] = gv
        d_ref[...], mo_ref[...], vo_ref[...] = _adamw_update(w_ref[...], gv, m_ref[...], v_ref[...])

    whole = pl.BlockSpec(w.shape, lambda i: (0, 0))
    return _call(
        body, name=name, grid=(1,),
        in_specs=[pl.BlockSpec((None, n_parts, rows, D), lambda i: (k, 0, 0, 0)), whole, whole, whole],
        out_specs=[whole] * 4, out_shape=[pltpu.HBM(w.shape, F32)] * 4,
        scratch_shapes=[], vmem_mib=40, args=(parts, w, m, v))


def _finish_many(items, name, rider=None):
    n_items = len(items)
    D = items[0][2].shape[1]
    max_rows = max(w.shape[0] for _, _, w, _, _ in items)
    max_parts = max(parts.shape[1] for parts, _, _, _, _ in items)
    n_loads, n_stores = 4, 4

    def body(*refs):
        ins, outs = refs[:4 * n_items], refs[4 * n_items:8 * n_items]
        pbuf, wbuf, obuf, sem = refs[8 * n_items:]
        step = pl.program_id(0)
        for j, (parts, k, w, _, _) in enumerate(items):
            rows, n_parts = w.shape[0], parts.shape[1]

            @pl.when(step == j)
            def _():
                p_ref, w_ref, m_ref, v_ref = ins[4 * j:4 * j + 4]
                loads = [pltpu.make_async_copy(p_ref.at[k], pbuf.at[0:n_parts, 0:rows], sem.at[0])]
                loads += [pltpu.make_async_copy(src, wbuf.at[r, 0:rows], sem.at[1 + r])
                          for r, src in enumerate((w_ref, m_ref, v_ref))]
                for cp in loads:
                    cp.start()
                for cp in loads:
                    cp.wait()
                acc = pbuf[0, 0:rows, :].astype(F32)
                for q in range(1, n_parts):
                    acc = acc + pbuf[q, 0:rows, :].astype(F32)
                obuf[0, 0:rows, :] = acc
                obuf[1, 0:rows, :], obuf[2, 0:rows, :], obuf[3, 0:rows, :] = _adamw_update(
                    wbuf[0, 0:rows, :], acc, wbuf[1, 0:rows, :], wbuf[2, 0:rows, :])
                stores = [pltpu.make_async_copy(obuf.at[r, 0:rows], outs[4 * j + r], sem.at[n_loads + r])
                          for r in range(n_stores)]
                for cp in stores:
                    cp.start()
                for cp in stores:
                    cp.wait()

    args = [a for parts, _, w, m, v in items for a in (parts, w, m, v)]
    shapes = [jax.ShapeDtypeStruct(w.shape, F32) for _, _, w, _, _ in items for _ in range(4)]
    return _call(
        body, name=name, grid=(n_items,),
        in_specs=[HBM_SPEC] * len(args), out_specs=[HBM_SPEC] * len(shapes), out_shape=shapes,
        scratch_shapes=[pltpu.VMEM((max_parts, max_rows, D), BF16), pltpu.VMEM((3, max_rows, D), F32),
                        pltpu.VMEM((4, max_rows, D), F32), pltpu.SemaphoreType.DMA((n_loads + n_stores,))],
        vmem_mib=40, args=args, rider=rider)


def _adamw_each(ws, gs, ms, vs, name):
    n = len(ws)

    def body(*refs):
        w_refs, g_refs, m_refs, v_refs, outs = refs[:n], refs[n:2 * n], refs[2 * n:3 * n], refs[3 * n:4 * n], refs[4 * n:]
        for k in range(n):
            outs[k][...], outs[n + k][...], outs[2 * n + k][...] = _adamw_update(
                w_refs[k][...], g_refs[k][...], m_refs[k][...], v_refs[k][...])

    shapes = [jax.ShapeDtypeStruct(w.shape, F32) for w in ws]
    return pl.pallas_call(
        body, name=name,
        in_specs=[VMEM_SPEC] * (4 * n), out_specs=[VMEM_SPEC] * (3 * n), out_shape=shapes * 3,
        compiler_params=pltpu.CompilerParams(vmem_limit_bytes=32 * MIB),
    )(*ws, *gs, *ms, *vs)


def _block_diag(w):
    h, d, _ = w.shape
    onto = jnp.eye(h, dtype=w.dtype)
    return (w[:, :, None, :] * onto[:, None, :, None]).reshape(h * d, h * d)


def _diag_blocks(m, h):
    d = m.shape[0] // h
    onto = jnp.eye(h, dtype=m.dtype)
    return (m.reshape(h, d, h, d) * onto[:, None, :, None]).sum(axis=2)


def _reduce_level1(full, tag):
    got = _run_comm(_sibling_comm(full), "rs_sibling_" + tag)
    return [_pair_add(a, b, "rs_pair_add_%s%d" % (tag, n)) for n, (a, b) in enumerate(zip(full, got))]


def kernel(x, ffn1_norm, ffn1_w_gate, ffn1_w_up, ffn1_w_down, mix_norm, w_in, conv_dw, conv_dw_bias, conv_ln_g, conv_ln_b, lru_conv_w, lru_conv_b, lru_w_a, lru_b_a, lru_w_x, lru_b_x, lru_lambda, w_out, ffn2_norm, ffn2_w_gate, ffn2_w_up, ffn2_w_down, final_norm, loss_target, m_ffn1_norm, m_ffn1_w_gate, m_ffn1_w_up, m_ffn1_w_down, m_mix_norm, m_w_in, m_conv_dw, m_conv_dw_bias, m_conv_ln_g, m_conv_ln_b, m_lru_conv_w, m_lru_conv_b, m_lru_w_a, m_lru_b_a, m_lru_w_x, m_lru_b_x, m_lru_lambda, m_w_out, m_ffn2_norm, m_ffn2_w_gate, m_ffn2_w_up, m_ffn2_w_down, m_final_norm, v_ffn1_norm, v_ffn1_w_gate, v_ffn1_w_up, v_ffn1_w_down, v_mix_norm, v_w_in, v_conv_dw, v_conv_dw_bias, v_conv_ln_g, v_conv_ln_b, v_lru_conv_w, v_lru_conv_b, v_lru_w_a, v_lru_b_a, v_lru_w_x, v_lru_b_x, v_lru_lambda, v_w_out, v_ffn2_norm, v_ffn2_w_gate, v_ffn2_w_up, v_ffn2_w_down, v_final_norm):
    T, D = x.shape[1], x.shape[2]
    F = ffn1_w_down.shape[0] * N_DEV
    rf = ffn1_w_down.shape[0]
    ri = w_in.shape[1]
    ro = w_out.shape[0]
    W = conv_dw_bias.shape[0]
    wc = conv_dw.shape[1]
    H = lru_w_a.shape[0]
    xs = x.reshape(T, D)
    tgt = loss_target.reshape(T, D)
    tm_ffn = min(512, T)
    tm_fwd = min(512, T)
    cf = 256
    tm_w = min(1024, T)
    tm_w1 = min(2048, T)
    tm_mix = min(256, T)
    tf_w = F // 2
    row = lambda v: v.reshape(1, -1)
    by_owner = lambda a, rows: a.reshape(a.shape[0], N_CHIP, 2, rows, D)

    p3a, p3b, p_in, p_out = _prep_weights(
        (ffn1_w_gate.T, ffn1_w_up.T, ffn1_w_down), (ffn2_w_gate.T, ffn2_w_up.T, ffn2_w_down), w_in, w_out,
        "prep_weights")
    tile_rows = lambda a: jnp.pad(a, ((0, -a.shape[0] % SUBLANES), (0, 0)))
    p_cw = jnp.concatenate([tile_rows(conv_dw), tile_rows(lru_conv_w)], axis=0)
    lw_row = p_cw.shape[0] - SUBLANES
    stacked = lambda r, j: r.at[:, j]
    plain = lambda r, j: r.at[j]
    g3_shape = jax.ShapeDtypeStruct((3, N_DEV, rf, D), BF16)
    g3a = _all_gather(p3a, "ag_ffn1")
    w3a = g3a.reshape(3, F, D)
    bda = _block_diag(lru_w_a).astype(BF16)
    bdx = _block_diag(lru_w_x).astype(BF16)
    vec = jnp.concatenate([tile_rows(v[None]) for v in
                           (conv_dw_bias, conv_ln_g, conv_ln_b, lru_conv_b, lru_b_a, lru_b_x, lru_lambda)], axis=0)

    gather_rest = _gather_comm(
        [p3b, p_in, p_out, p_cw], [stacked, plain, plain, plain],
        [g3_shape, jax.ShapeDtypeStruct((N_DEV, ri, D), BF16), jax.ShapeDtypeStruct((N_DEV, ro, D), BF16),
         jax.ShapeDtypeStruct((N_DEV,) + p_cw.shape, F32)],
        [(SIBLING,) + SAME_CORE, EVERYONE, EVERYONE, EVERYONE])
    (x1, h1, dau1, dag1, act1), (g3b_half, g_in, g_out, g_cw) = _ffn_fwd(
        xs, row(ffn1_norm), w3a, tm_fwd, cf, "ffn1_fwd", rider=gather_rest)
    w_in_t = g_in.reshape(N_DEV * ri, D)
    w_out_f = g_out.reshape(N_DEV * ro, D)
    cw_all = jnp.transpose(g_cw, (1, 0, 2)).reshape(p_cw.shape[0], N_DEV * wc)
    cw = cw_all[0:CONV_K]
    lw = cw_all[lw_row:lw_row + LRU_K]
    (x2, z, mix, u1, xr, hst), (g3b,) = _mix_core_fwd(
        x1, row(mix_norm), w_in_t, w_out_f, bda, bdx, cw, lw, vec, min(512, T), "mix_core_fwd",
        rider=_forward_comm([g3b_half], [stacked]))
    w3b = g3b.reshape(3, F, D)
    dx3, dob2, d_final_norm, loss_part, h3, dau2, dag2, act2 = _ffn_fwd(
        x2, row(ffn2_norm), w3b, tm_fwd, cf, "ffn2_fwd_loss", head=(row(final_norm), tgt))

    dx2, dgate2, dup2, d_ffn2_norm = _ffn_dgrad(dx3, x2, row(ffn2_norm), dau2, dag2, w3b, tm_ffn, cf, "ffn2_dgrad")
    (dw_gu2,) = _wgrad([dgate2, dup2], h3, tm_w, tf_w, "ffn2_wgrad_gu")
    (dw_d2,) = _wgrad([act2], dob2, tm_w1, tf_w, "ffn2_wgrad_d")
    by_device = lambda a: a.reshape(a.shape[0], N_DEV, rf, D)
    (dx1, sg, dbda, dbdx, dob1, d_mix_norm, dw_in_t, dw_out), parts_f2 = _mix_bwd(
        dx2, z, u1, xr, hst, x1, mix, row(mix_norm), w_in_t, w_out_f, bda, bdx, cw, lw, vec, tm_mix, "mix_bwd",
        rider=_chips_comm([by_device(dw_gu2), by_device(dw_d2)], every_device=True))
    io = [dw_in_t.reshape(1, N_DEV, ri, D), dw_out.reshape(1, N_DEV, ro, D)]
    (dw_d1,), parts_io = _wgrad(
        [act1], dob1, tm_w1, tf_w, "ffn1_wgrad_d", rider=_chips_comm(io, every_device=True))
    sums_d1 = _reduce_level1([by_owner(dw_d1, rf)], "d1")
    dx0, dgate1, dup1, d_ffn1_norm = _ffn_dgrad(dx1, xs, row(ffn1_norm), dau1, dag1, w3a, tm_ffn, cf, "ffn1_dgrad")
    small = [d_ffn1_norm, d_mix_norm, d_ffn2_norm, d_final_norm, sg, loss_part]
    (dw_g1,), summed_and_parts = _wgrad(
        [dgate1], h1, tm_w1, tf_w, "ffn1_wgrad_g", rider=_both(_small_sum_comm(small), _chips_comm(sums_d1)))
    summed, parts_d1 = summed_and_parts[:len(small)], summed_and_parts[len(small):]
    sums_g1 = _reduce_level1([by_owner(dw_g1, rf)], "g1")
    gates = [_diag_blocks(dbda, H).reshape(-1, D), _diag_blocks(dbdx, H).reshape(-1, D)]
    (dw_u1,), (g_w_a, g_w_x, *parts_g1) = _wgrad(
        [dup1], h1, tm_w1, tf_w, "ffn1_wgrad_u", rider=_both(_small_sum_comm(gates), _chips_comm(sums_g1)))
    sums_u1 = _reduce_level1([by_owner(dw_u1, rf)], "u1")

    g_norm1, g_norm_mix, g_norm2, g_norm_final, g_sg, g_loss = summed
    loss = g_loss[0, 0]
    me = 4 * lax.axis_index("x") + 2 * lax.axis_index("y") + lax.axis_index("c")
    chan = lambda full_g: lax.dynamic_slice_in_dim(full_g, me * wc, wc, axis=1)
    grads = {
        "ffn1_norm": g_norm1.reshape(D), "mix_norm": g_norm_mix.reshape(D), "ffn2_norm": g_norm2.reshape(D),
        "final_norm": g_norm_final.reshape(D),
        "conv_dw_bias": g_sg[G_CB], "conv_ln_g": g_sg[G_LNG], "conv_ln_b": g_sg[G_LNB],
        "lru_conv_b": g_sg[G_LCB], "lru_b_a": g_sg[G_BA], "lru_b_x": g_sg[G_BX], "lru_lambda": g_sg[G_LAM],
        "lru_w_a": g_w_a.reshape(lru_w_a.shape), "lru_w_x": g_w_x.reshape(lru_w_x.shape),
        "conv_dw": chan(g_sg[G_CW:G_CW + CONV_K]), "lru_conv_w": chan(g_sg[G_LW:G_LW + LRU_K]),
    }

    weights = dict(ffn1_norm=ffn1_norm, ffn1_w_gate=ffn1_w_gate, ffn1_w_up=ffn1_w_up, ffn1_w_down=ffn1_w_down, mix_norm=mix_norm, w_in=w_in, conv_dw=conv_dw, conv_dw_bias=conv_dw_bias, conv_ln_g=conv_ln_g, conv_ln_b=conv_ln_b, lru_conv_w=lru_conv_w, lru_conv_b=lru_conv_b, lru_w_a=lru_w_a, lru_b_a=lru_b_a, lru_w_x=lru_w_x, lru_b_x=lru_b_x, lru_lambda=lru_lambda, w_out=w_out, ffn2_norm=ffn2_norm, ffn2_w_gate=ffn2_w_gate, ffn2_w_up=ffn2_w_up, ffn2_w_down=ffn2_w_down, final_norm=final_norm)
    moment1 = dict(ffn1_norm=m_ffn1_norm, ffn1_w_gate=m_ffn1_w_gate, ffn1_w_up=m_ffn1_w_up, ffn1_w_down=m_ffn1_w_down, mix_norm=m_mix_norm, w_in=m_w_in, conv_dw=m_conv_dw, conv_dw_bias=m_conv_dw_bias, conv_ln_g=m_conv_ln_g, conv_ln_b=m_conv_ln_b, lru_conv_w=m_lru_conv_w, lru_conv_b=m_lru_conv_b, lru_w_a=m_lru_w_a, lru_b_a=m_lru_b_a, lru_w_x=m_lru_w_x, lru_b_x=m_lru_b_x, lru_lambda=m_lru_lambda, w_out=m_w_out, ffn2_norm=m_ffn2_norm, ffn2_w_gate=m_ffn2_w_gate, ffn2_w_up=m_ffn2_w_up, ffn2_w_down=m_ffn2_w_down, final_norm=m_final_norm)
    moment2 = dict(ffn1_norm=v_ffn1_norm, ffn1_w_gate=v_ffn1_w_gate, ffn1_w_up=v_ffn1_w_up, ffn1_w_down=v_ffn1_w_down, mix_norm=v_mix_norm, w_in=v_w_in, conv_dw=v_conv_dw, conv_dw_bias=v_conv_dw_bias, conv_ln_g=v_conv_ln_g, conv_ln_b=v_conv_ln_b, lru_conv_w=v_lru_conv_w, lru_conv_b=v_lru_conv_b, lru_w_a=v_lru_w_a, lru_b_a=v_lru_b_a, lru_w_x=v_lru_w_x, lru_b_x=v_lru_b_x, lru_lambda=v_lru_lambda, w_out=v_w_out, ffn2_norm=v_ffn2_norm, ffn2_w_gate=v_ffn2_w_gate, ffn2_w_up=v_ffn2_w_up, ffn2_w_down=v_ffn2_w_down, final_norm=v_final_norm)
    order = list(weights)
    gate_up = {"ffn1_w_gate", "ffn1_w_up", "ffn2_w_gate", "ffn2_w_up"}
    view = lambda n, a: a.T if n in gate_up else a
    operands = lambda n: [view(n, d[n]) for d in (weights, moment1, moment2)]
    delta, new_m, new_v = {}, {}, {}
    six = {"ffn1_w_gate": (parts_g1[0], 0), "ffn1_w_down": (parts_d1[0], 0), "w_out": (parts_io[1], 0),
           "ffn2_w_gate": (parts_f2[0], 0), "ffn2_w_up": (parts_f2[0], 1), "ffn2_w_down": (parts_f2[1], 0)}
    results, parts_u1 = _finish_many([(parts, k, *operands(n)) for n, (parts, k) in six.items()], "finish_six",
                                     rider=_chips_comm(sums_u1))
    for j, n in enumerate(six):
        grads[n], delta[n], new_m[n], new_v[n] = [view(n, r) for r in results[4 * j:4 * j + 4]]
    for n, parts, d_major in (("ffn1_w_up", parts_u1[0], False), ("w_in", parts_io[0], True)):
        grads[n], delta[n], new_m[n], new_v[n] = [
            view(n, r) for r in _finish(parts, 0, *operands(n), d_major, "finish_" + n)]
    big = set(six) | {"ffn1_w_up", "w_in"}
    rest = [n for n in order if n not in big]
    updates = _adamw_each([weights[n] for n in rest], [grads[n] for n in rest], [moment1[n] for n in rest],
                          [moment2[n] for n in rest], "adamw_small")
    for k, n in enumerate(rest):
        delta[n], new_m[n], new_v[n] = updates[k], updates[len(rest) + k], updates[2 * len(rest) + k]

    return (loss, dx0.reshape(x.shape), *[grads[n] for n in order], *[delta[n] for n in order],
            *[new_m[n] for n in order], *[new_v[n] for n in order])
```

```python
import functools
import math

import jax
import jax.numpy as jnp
from jax import lax
from jax.experimental import pallas as pl
from jax.experimental.pallas import tpu as pltpu

F32 = jnp.float32
BF16 = jnp.bfloat16
MESH = pl.DeviceIdType.MESH

N_DEV = 8
N_CHIP = 4
SUBLANES = 8
RMS_EPS = 1e-6
LN_EPS = 1e-5
LRU_C = 8.0
CONV_K = 31
LRU_K = 4
CONV_HALO = 32
LRU_HALO = 8
FFN_RES = 0.5
ADAM_LR, ADAM_B1, ADAM_B2, ADAM_EPS, ADAM_WD, ADAM_STEP = 0.001, 0.9, 0.999, 1e-08, 0.01, 10
GELU_K = math.sqrt(2.0 / math.pi)
GELU_C = 0.044715

MIB = 1024 * 1024
NT = (((1,), (1,)), ((), ()))
NN = (((1,), (0,)), ((), ()))
TN = (((0,), (0,)), ((), ()))

V_CB, V_LNG, V_LNB, V_LCB, V_BA, V_BX, V_LAM = range(0, 7 * SUBLANES, SUBLANES)
G_CW = 0
G_CB, G_LNG, G_LNB = 31, 32, 33
G_LW = 34
G_LCB, G_BA, G_BX, G_LAM = 38, 39, 40, 41
G_ROWS = 48

HBM_SPEC = pl.BlockSpec(memory_space=pltpu.HBM)
VMEM_SPEC = pl.BlockSpec(memory_space=pltpu.VMEM)


def _dot(a, b, dims):
    return lax.dot_general(a, b, dims, preferred_element_type=F32)


def _sigmoid(x):
    return 1.0 / (1.0 + jnp.exp(-x))


def _gelu_parts(x):
    x2 = x * x
    th = jnp.tanh(GELU_K * x * (1.0 + GELU_C * x2))
    gl = 0.5 * x * (1.0 + th)
    dgl = 0.5 * (1.0 + th) + 0.5 * x * (1.0 - th * th) * GELU_K * (1.0 + 3.0 * GELU_C * x2)
    return gl, dgl


def _neg_expm1(y):
    series = -y * (1.0 + y * (1.0 / 2) * (1.0 + y * (1.0 / 3) * (1.0 + y * (1.0 / 4) * (1.0 + y * (1.0 / 5) * (1.0 + y * (1.0 / 6))))))
    return jnp.where(y > -0.25, series, 1.0 - jnp.exp(y))


def _softplus_neg(lam):
    t = -lam
    e = jnp.exp(-jnp.abs(t))
    s = 1.0 + e
    log1p_e = jnp.log(s) - ((s - 1.0) - e) / s
    return jnp.maximum(t, 0.0) + log1p_e


def _rms_stats(xv):
    rstd = lax.rsqrt(jnp.mean(xv * xv, axis=-1, keepdims=True) + RMS_EPS)
    return xv * rstd, rstd


def _rms_bwd(xhat, rstd, g, dh):
    dxhat = dh * g
    dx = rstd * (dxhat - xhat * jnp.mean(dxhat * xhat, axis=-1, keepdims=True))
    return dx, jnp.sum(dh * xhat, axis=0, keepdims=True)


def _row_windows(buf_ref, n_rows, offsets):
    total = buf_ref.shape[0]
    full = buf_ref[...]
    for b in range(SUBLANES):
        offs = [o for o in offsets if o % SUBLANES == b]
        if not offs:
            continue
        assert max(offs) + n_rows <= total
        moved = full if b == 0 else pltpu.roll(full, total - b, 0)
        for o in offs:
            yield o, moved[o - b:o - b + n_rows, :]


def _scan_rows(av, bv, edge, out_ref, reverse=False):
    tm, W = av.shape
    sub = lax.broadcasted_iota(jnp.int32, (tm, W), 0) % SUBLANES
    s = 1
    while s < SUBLANES:
        keep = (sub < SUBLANES - s) if reverse else (sub >= s)
        shift = tm - s if reverse else s
        bv = jnp.where(keep, av * pltpu.roll(bv, shift, 0) + bv, bv)
        av = jnp.where(keep, av * pltpu.roll(av, shift, 0), av)
        s *= 2
    starts = range(0, tm, SUBLANES)
    for r0 in (reversed(starts) if reverse else starts):
        group = av[r0:r0 + SUBLANES, :] * edge + bv[r0:r0 + SUBLANES, :]
        out_ref[r0:r0 + SUBLANES, :] = group
        edge = group[0:1, :] if reverse else group[SUBLANES - 1:SUBLANES, :]
    return edge


class _Comm:
    def __init__(self, arrays, in_specs, out_shapes, out_specs, scratch, start, wait, aliases=None):
        self.arrays, self.in_specs = list(arrays), list(in_specs)
        self.out_shapes, self.out_specs = list(out_shapes), list(out_specs)
        self.scratch, self.start, self.wait = list(scratch), start, wait
        self.aliases = dict(aliases or {})


def _in_hbm(a):
    return pltpu.with_memory_space_constraint(a, pltpu.HBM)


def _operands(comm):
    return [a if spec is VMEM_SPEC else _in_hbm(a) for a, spec in zip(comm.arrays, comm.in_specs)]


def _call(body, *, name, grid, in_specs, out_specs, out_shape, scratch_shapes, vmem_mib, args, rider=None,
          num_scalar_prefetch=0):
    params = pltpu.CompilerParams(dimension_semantics=("arbitrary",) * len(grid), vmem_limit_bytes=vmem_mib * MIB)
    args = [a if k < num_scalar_prefetch else _in_hbm(a) for k, a in enumerate(args)]
    if rider is None:
        return pl.pallas_call(
            body, name=name,
            grid_spec=pltpu.PrefetchScalarGridSpec(
                num_scalar_prefetch=num_scalar_prefetch, grid=grid, in_specs=in_specs, out_specs=out_specs,
                scratch_shapes=scratch_shapes),
            out_shape=out_shape, compiler_params=params)(*args)
    assert num_scalar_prefetch == 0
    n_in, n_out, n_scr = len(in_specs), len(out_specs), len(scratch_shapes)
    r_in, r_out = len(rider.arrays), len(rider.out_shapes)
    n_axes = len(grid)

    def carried(*refs):
        pos = [0]

        def take(n):
            pos[0] += n
            return refs[pos[0] - n:pos[0]]

        ins, r_ins, outs, r_outs, scr, r_scr = take(n_in), take(r_in), take(n_out), take(r_out), take(n_scr), take(len(rider.scratch))
        first = pl.program_id(0) == 0
        last = pl.program_id(0) == grid[0] - 1
        for ax in range(1, n_axes):
            first = first & (pl.program_id(ax) == 0)
            last = last & (pl.program_id(ax) == grid[ax] - 1)

        @pl.when(first)
        def _():
            rider.start(r_ins, r_outs, r_scr)

        body(*ins, *outs, *scr)

        @pl.when(last)
        def _():
            rider.wait(r_ins, r_outs, r_scr)

    res = pl.pallas_call(
        carried, name=name,
        grid=grid,
        in_specs=list(in_specs) + rider.in_specs,
        out_specs=list(out_specs) + rider.out_specs,
        out_shape=list(out_shape) + rider.out_shapes,
        scratch_shapes=list(scratch_shapes) + rider.scratch,
        input_output_aliases={n_in + i: n_out + o for i, o in rider.aliases.items()},
        compiler_params=params)(*args, *_operands(rider))
    return res[:n_out], res[n_out:]


def _run_comm(comm, name):
    n_in, n_out = len(comm.arrays), len(comm.out_shapes)

    def body(*refs):
        ins, outs, scr = refs[:n_in], refs[n_in:n_in + n_out], refs[n_in + n_out:]
        comm.start(ins, outs, scr)
        comm.wait(ins, outs, scr)

    return pl.pallas_call(
        body, name=name,
        in_specs=comm.in_specs, out_specs=comm.out_specs, out_shape=comm.out_shapes,
        scratch_shapes=comm.scratch, input_output_aliases=comm.aliases,
        compiler_params=pltpu.CompilerParams(vmem_limit_bytes=24 * MIB))(*_operands(comm))


def _both(a, b):
    ni, no, ns = len(a.arrays), len(a.out_shapes), len(a.scratch)

    def start(ins, outs, scr):
        a.start(ins[:ni], outs[:no], scr[:ns])
        b.start(ins[ni:], outs[no:], scr[ns:])

    def wait(ins, outs, scr):
        a.wait(ins[:ni], outs[:no], scr[:ns])
        b.wait(ins[ni:], outs[no:], scr[ns:])

    aliases = dict(a.aliases)
    aliases.update({ni + i: no + o for i, o in b.aliases.items()})
    return _Comm(a.arrays + b.arrays, a.in_specs + b.in_specs, a.out_shapes + b.out_shapes,
                 a.out_specs + b.out_specs, a.scratch + b.scratch, start, wait, aliases)


def _place():
    return lax.axis_index("x"), lax.axis_index("y"), lax.axis_index("c")


def _peer(k):
    x, y, c = _place()
    px, py, pc = x ^ ((k >> 2) & 1), y ^ ((k >> 1) & 1), c ^ (k & 1)
    return (px, py, pc), 4 * px + 2 * py + pc


SIBLING = 1
SAME_CORE = (2, 4, 6)
EVERYONE = tuple(range(1, N_DEV))


def _gather_comm(shards, views, out_shapes, relations):
    na = len(shards)

    def copies(ins, outs, scr):
        send_sems, recv_sems, _ = scr
        _, me = _peer(0)
        out = []
        for a in range(na):
            for k in relations[a]:
                peer, theirs = _peer(k)
                send = functools.partial(
                    pltpu.make_async_remote_copy,
                    src_ref=ins[a], dst_ref=views[a](outs[a], me),
                    send_sem=send_sems.at[7 * a + k - 1], recv_sem=recv_sems.at[7 * a + k - 1],
                    device_id=peer, device_id_type=MESH)
                recv = functools.partial(
                    pltpu.make_async_remote_copy,
                    src_ref=ins[a], dst_ref=views[a](outs[a], theirs),
                    send_sem=send_sems.at[7 * a + k - 1], recv_sem=recv_sems.at[7 * a + k - 1],
                    device_id=peer, device_id_type=MESH)
                out.append((send, recv))
        return out

    def local(ins, outs, scr):
        _, me = _peer(0)
        return [pltpu.make_async_copy(ins[a], views[a](outs[a], me), scr[2].at[a]) for a in range(na)]

    def start(ins, outs, scr):
        for cp in local(ins, outs, scr):
            cp.start()
        for send, _ in copies(ins, outs, scr):
            send().start()

    def wait(ins, outs, scr):
        for _, recv in copies(ins, outs, scr):
            recv().wait_recv()
        for send, _ in copies(ins, outs, scr):
            send().wait_send()
        for cp in local(ins, outs, scr):
            cp.wait()

    return _Comm(shards, [HBM_SPEC] * na, out_shapes, [HBM_SPEC] * na,
                 [pltpu.SemaphoreType.DMA((7 * na,)), pltpu.SemaphoreType.DMA((7 * na,)),
                  pltpu.SemaphoreType.DMA((na,))], start, wait)


def _forward_comm(gathered, views):
    na = len(gathered)
    shapes = [jax.ShapeDtypeStruct(g.shape, g.dtype) for g in gathered]

    def copies(outs, scr):
        send_sems, recv_sems = scr
        sibling, _ = _peer(SIBLING)
        out = []
        for a in range(na):
            for n, k in enumerate(SAME_CORE):
                _, mine = _peer(k)
                _, theirs = _peer(k ^ SIBLING)
                send = functools.partial(
                    pltpu.make_async_remote_copy,
                    src_ref=views[a](outs[a], mine), dst_ref=views[a](outs[a], mine),
                    send_sem=send_sems.at[3 * a + n], recv_sem=recv_sems.at[3 * a + n],
                    device_id=sibling, device_id_type=MESH)
                recv = functools.partial(
                    pltpu.make_async_remote_copy,
                    src_ref=views[a](outs[a], mine), dst_ref=views[a](outs[a], theirs),
                    send_sem=send_sems.at[3 * a + n], recv_sem=recv_sems.at[3 * a + n],
                    device_id=sibling, device_id_type=MESH)
                out.append((send, recv))
        return out

    def start(ins, outs, scr):
        for send, _ in copies(outs, scr):
            send().start()

    def wait(ins, outs, scr):
        for _, recv in copies(outs, scr):
            recv().wait_recv()
        for send, _ in copies(outs, scr):
            send().wait_send()

    return _Comm(gathered, [HBM_SPEC] * na, shapes, [HBM_SPEC] * na,
                 [pltpu.SemaphoreType.DMA((3 * na,)), pltpu.SemaphoreType.DMA((3 * na,))], start, wait,
                 aliases={a: a for a in range(na)})


Y_NEIGHBOUR, X_NEIGHBOUR, DIAGONAL = SAME_CORE


def _all_gather(stack, name):
    K, rows = stack.shape[0], stack.shape[1]
    half = rows // 2
    assert half % 16 == 0
    pieces = [(k, pl.ds(h * half, half)) for k in range(K) for h in range(2)]
    na = len(pieces)
    views = [lambda r, j, k=k, part=part: r.at[k, j, part] for k, part in pieces]
    out_shape = jax.ShapeDtypeStruct((K, N_DEV) + stack.shape[1:], stack.dtype)
    near = (SIBLING, Y_NEIGHBOUR, X_NEIGHBOUR)
    level1 = _gather_comm([stack] * na, views, [out_shape] * na, [near] * na)

    def body(stack_ref, out_ref, *scratch):
        ins, outs = [stack_ref.at[k, part] for k, part in pieces], [out_ref] * na
        send_sems, recv_sems, local_sems, fwd_send, fwd_recv, relay_send, relay_recv = scratch
        sibling, _ = _peer(SIBLING)
        c = lax.axis_index("c")
        level1.start(ins, outs, (send_sems, recv_sems, local_sems))

        def block_copy(a, block, to, send_sem, recv_sem):
            return pltpu.make_async_remote_copy(
                src_ref=views[a](outs[a], block), dst_ref=views[a](outs[a], block),
                send_sem=send_sem, recv_sem=recv_sem, device_id=to, device_id_type=MESH)

        def to_sibling(a, n, k):
            _, mine = _peer(k)
            _, theirs = _peer(k ^ SIBLING)
            fwd = block_copy(a, mine, sibling, fwd_send.at[3 * a + n], fwd_recv.at[3 * a + n])
            fwd.start()
            return fwd, block_copy(a, theirs, sibling, fwd_send.at[3 * a + n], fwd_recv.at[3 * a + n])

        passed, landing = [], []
        for a in range(na):
            for n, k in enumerate((Y_NEIGHBOUR, X_NEIGHBOUR)):
                peer, origin = _peer(k)
                pltpu.make_async_remote_copy(
                    src_ref=ins[a], dst_ref=views[a](outs[a], origin),
                    send_sem=send_sems.at[7 * a + k - 1], recv_sem=recv_sems.at[7 * a + k - 1],
                    device_id=peer, device_id_type=MESH).wait_recv()

                @pl.when(c == (0 if k == X_NEIGHBOUR else 1))
                def _():
                    other, _ = _peer(DIAGONAL ^ k)
                    block_copy(a, origin, other, relay_send.at[a], relay_recv.at[a]).start()

                fwd, lands = to_sibling(a, n, k)
                passed.append(fwd)
                landing.append(lands)
        for a in range(na):
            _, far = _peer(DIAGONAL)
            block_copy(a, far, sibling, relay_send.at[a], relay_recv.at[a]).wait_recv()
            fwd, lands = to_sibling(a, 2, DIAGONAL)
            passed.append(fwd)
            landing.append(lands)
        for a in range(na):
            _, theirs = _peer(SIBLING)
            pltpu.make_async_remote_copy(
                src_ref=ins[a], dst_ref=views[a](outs[a], theirs),
                send_sem=send_sems.at[7 * a + SIBLING - 1], recv_sem=recv_sems.at[7 * a + SIBLING - 1],
                device_id=sibling, device_id_type=MESH).wait_recv()
        for cp in landing:
            cp.wait_recv()
        for cp in passed:
            cp.wait_send()
        _, me = _peer(0)
        for a in range(na):
            block_copy(a, me, sibling, relay_send.at[a], relay_recv.at[a]).wait_send()
            for k in near:
                peer, _ = _peer(k)
                pltpu.make_async_remote_copy(
                    src_ref=ins[a], dst_ref=views[a](outs[a], me),
                    send_sem=send_sems.at[7 * a + k - 1], recv_sem=recv_sems.at[7 * a + k - 1],
                    device_id=peer, device_id_type=MESH).wait_send()
            pltpu.make_async_copy(ins[a], views[a](outs[a], me), local_sems.at[a]).wait()

    return pl.pallas_call(
        body, name=name,
        in_specs=[HBM_SPEC], out_specs=HBM_SPEC, out_shape=out_shape,
        scratch_shapes=level1.scratch + [pltpu.SemaphoreType.DMA((3 * na,)), pltpu.SemaphoreType.DMA((3 * na,)),
                                         pltpu.SemaphoreType.DMA((na,)), pltpu.SemaphoreType.DMA((na,))],
    )(_in_hbm(stack))


def _sibling_comm(grads):
    na = len(grads)
    shapes = [jax.ShapeDtypeStruct(g.shape[:2] + g.shape[3:], g.dtype) for g in grads]

    def copies(ins, outs, scr):
        x, y, c = _place()
        return [pltpu.make_async_remote_copy(
            src_ref=ins[a].at[:, :, 1 - c], dst_ref=outs[a],
            send_sem=scr[0].at[a], recv_sem=scr[1].at[a],
            device_id=(x, y, 1 - c), device_id_type=MESH) for a in range(na)]

    def start(ins, outs, scr):
        for cp in copies(ins, outs, scr):
            cp.start()

    def wait(ins, outs, scr):
        for cp in copies(ins, outs, scr):
            cp.wait()

    return _Comm(grads, [HBM_SPEC] * na, shapes, [HBM_SPEC] * na,
                 [pltpu.SemaphoreType.DMA((na,)), pltpu.SemaphoreType.DMA((na,))], start, wait)


def _chips_comm(sums, every_device=False):
    na = len(sums)
    shapes = [jax.ShapeDtypeStruct(s.shape, s.dtype) for s in sums]
    relations = EVERYONE if every_device else SAME_CORE
    nr = len(relations)

    def block(px, py, pc):
        return 4 * px + 2 * py + pc if every_device else 2 * px + py

    def copies(ins, outs, scr):
        mine = block(*_place())
        out = []
        for a in range(na):
            for n, k in enumerate(relations):
                peer, _ = _peer(k)
                theirs = block(*peer)
                send = functools.partial(
                    pltpu.make_async_remote_copy,
                    src_ref=ins[a].at[:, theirs], dst_ref=outs[a].at[:, mine],
                    send_sem=scr[0].at[nr * a + n], recv_sem=scr[1].at[nr * a + n],
                    device_id=peer, device_id_type=MESH)
                recv = functools.partial(
                    pltpu.make_async_remote_copy,
                    src_ref=ins[a].at[:, mine], dst_ref=outs[a].at[:, theirs],
                    send_sem=scr[0].at[nr * a + n], recv_sem=scr[1].at[nr * a + n],
                    device_id=peer, device_id_type=MESH)
                out.append((send, recv))
        return out

    def local(ins, outs, scr):
        mine = block(*_place())
        return [pltpu.make_async_copy(ins[a].at[:, mine], outs[a].at[:, mine], scr[2].at[a]) for a in range(na)]

    def start(ins, outs, scr):
        for cp in local(ins, outs, scr):
            cp.start()
        for send, _ in copies(ins, outs, scr):
            send().start()

    def wait(ins, outs, scr):
        for _, recv in copies(ins, outs, scr):
            recv().wait_recv()
        for send, _ in copies(ins, outs, scr):
            send().wait_send()
        for cp in local(ins, outs, scr):
            cp.wait()

    return _Comm(sums, [HBM_SPEC] * na, shapes, [HBM_SPEC] * na,
                 [pltpu.SemaphoreType.DMA((nr * na,)), pltpu.SemaphoreType.DMA((nr * na,)),
                  pltpu.SemaphoreType.DMA((na,))], start, wait)


def _small_sum_comm(arrays):
    na = len(arrays)

    def copies(ins, scr):
        bufs, send_sems, recv_sems = scr[:na], scr[na], scr[na + 1]
        _, me = _peer(0)
        out = []
        for a in range(na):
            for k in EVERYONE:
                peer, theirs = _peer(k)
                sems = dict(send_sem=send_sems.at[7 * a + k - 1], recv_sem=recv_sems.at[7 * a + k - 1])
                send = functools.partial(
                    pltpu.make_async_remote_copy,
                    src_ref=ins[a], dst_ref=bufs[a].at[me], device_id=peer, device_id_type=MESH, **sems)
                recv = functools.partial(
                    pltpu.make_async_remote_copy,
                    src_ref=ins[a], dst_ref=bufs[a].at[theirs], device_id=peer, device_id_type=MESH, **sems)
                out.append((send, recv))
        return out

    def start(ins, outs, scr):
        _, me = _peer(0)
        for a in range(na):
            scr[a][me] = ins[a][...]
        for send, _ in copies(ins, scr):
            send().start()

    def wait(ins, outs, scr):
        for _, recv in copies(ins, scr):
            recv().wait_recv()
        for send, _ in copies(ins, scr):
            send().wait_send()
        for a in range(na):
            acc = scr[a][0]
            for j in range(1, N_DEV):
                acc = acc + scr[a][j]
            outs[a][...] = acc

    return _Comm(arrays, [VMEM_SPEC] * na, [jax.ShapeDtypeStruct(s.shape, F32) for s in arrays], [VMEM_SPEC] * na,
                 [pltpu.VMEM((N_DEV,) + s.shape, F32) for s in arrays]
                 + [pltpu.SemaphoreType.DMA((7 * na,)), pltpu.SemaphoreType.DMA((7 * na,))], start, wait)


def _prep_weights(ffn1, ffn2, w_in, w_out, name):
    rf, D = ffn1[2].shape
    ri, ro = w_in.shape[1], w_out.shape[0]

    def body(g1, u1, d1, g2, u2, d2, wi, wo, p1_ref, p2_ref, pi_ref, po_ref):
        for p_ref, shards in ((p1_ref, (g1, u1, d1)), (p2_ref, (g2, u2, d2))):
            for k, shard in enumerate(shards):
                p_ref[k] = shard[...].astype(BF16)
        pi_ref[...] = wi[...].T.astype(BF16)
        po_ref[...] = wo[...].astype(BF16)

    args = (*ffn1, *ffn2, w_in, w_out)
    whole = lambda shape: pl.BlockSpec(shape, lambda i: (0,) * len(shape))
    out_shapes = [(3, rf, D), (3, rf, D), (ri, D), (ro, D)]
    return _call(
        body, name=name, grid=(1,),
        in_specs=[whole(a.shape) for a in args], out_specs=[whole(s) for s in out_shapes],
        out_shape=[jax.ShapeDtypeStruct(s, BF16) for s in out_shapes],
        scratch_shapes=[], vmem_mib=48, args=args)


def _load_weights(w_hbm, w_vmem, sem):
    @pl.when(pl.program_id(0) == 0)
    def _():
        copies = [pltpu.make_async_copy(w_hbm.at[k], w_vmem.at[k], sem.at[k]) for k in range(3)]
        for cp in copies:
            cp.start()
        for cp in copies:
            cp.wait()


def _ffn_fwd(x, g, w3, tm, cf, name, rider=None, head=None):
    T, D = x.shape
    F = w3.shape[1]
    n_head = 0 if head is None else 2

    def body(x_ref, g_ref, w_hbm, *refs):
        head_refs, refs = refs[:n_head], refs[n_head:]
        if head is None:
            (xo_ref, h_ref, dau_ref, dag_ref, act_ref, wv, sem) = refs
        else:
            (dx_ref, dob_ref, dgf_ref, loss_ref, h_ref, dau_ref, dag_ref, act_ref, wv, sem) = refs
        _load_weights(w_hbm, wv, sem)
        xhat, _ = _rms_stats(x_ref[...])
        hb = (xhat * g_ref[...]).astype(BF16)
        h_ref[...] = hb
        for lo in range(0, F, cf):
            gate = _dot(hb, wv[0, lo:lo + cf, :], NT)
            up = _dot(hb, wv[1, lo:lo + cf, :], NT)
            sig = _sigmoid(gate)
            silu = gate * sig
            dau_ref[:, lo:lo + cf] = silu.astype(BF16)
            dag_ref[:, lo:lo + cf] = (up * (sig * (1.0 + gate * (1.0 - sig)))).astype(BF16)
            act_ref[:, lo:lo + cf] = (silu * up).astype(BF16)
        x_out = x_ref[...] + FFN_RES * _dot(act_ref[...], wv[2], NN)
        if head is None:
            xo_ref[...] = x_out
            return

        @pl.when(pl.program_id(0) == 0)
        def _():
            dgf_ref[...] = jnp.zeros_like(dgf_ref)
            loss_ref[...] = jnp.zeros_like(loss_ref)

        gf_ref, tgt_ref = head_refs
        yhat, rstd = _rms_stats(x_out)
        gf = gf_ref[...]
        err = yhat * gf - tgt_ref[...]
        loss_ref[...] += (0.5 / D) * jnp.sum(err * err)
        dx, dgf = _rms_bwd(yhat, rstd, gf, err * (1.0 / D))
        dx_ref[...] = dx
        dob_ref[...] = (FFN_RES * dx).astype(BF16)
        dgf_ref[...] += dgf

    row = pl.BlockSpec((tm, D), lambda i: (i, 0))
    hid = pl.BlockSpec((tm, F), lambda i: (i, 0))
    vec = pl.BlockSpec((1, D), lambda i: (0, 0))
    row_f32, row_bf16 = jax.ShapeDtypeStruct((T, D), F32), jax.ShapeDtypeStruct((T, D), BF16)
    if head is None:
        first_specs, first_shapes = [row], [row_f32]
    else:
        first_specs = [row, row, vec, pl.BlockSpec((1, 128), lambda i: (0, 0))]
        first_shapes = [row_f32, row_bf16, jax.ShapeDtypeStruct((1, D), F32), jax.ShapeDtypeStruct((1, 128), F32)]
    return _call(
        body, name=name, grid=(T // tm,),
        in_specs=[row, vec, HBM_SPEC] + ([] if head is None else [vec, row]),
        out_specs=first_specs + [row, hid, hid, hid],
        out_shape=first_shapes + [row_bf16] + [jax.ShapeDtypeStruct((T, F), BF16)] * 3,
        scratch_shapes=[pltpu.VMEM((3, F, D), BF16), pltpu.SemaphoreType.DMA((3,))],
        vmem_mib=60, args=(x, g, w3) + (() if head is None else tuple(head)), rider=rider)


def _ffn_dgrad(dout, x, g, dau, dag, w3, tm, cf, name, rider=None):
    T, D = x.shape
    F = w3.shape[1]

    def body(do_ref, x_ref, g_ref, dau_ref, dag_ref, w_hbm, dx_ref, dgate_ref, dup_ref, dg_ref, wv, sem):
        _load_weights(w_hbm, wv, sem)

        @pl.when(pl.program_id(0) == 0)
        def _():
            dg_ref[...] = jnp.zeros_like(dg_ref)

        dob = (FFN_RES * do_ref[...]).astype(BF16)
        for lo in range(0, F, cf):
            dact = _dot(dob, wv[2, lo:lo + cf, :], NT)
            dup_ref[:, lo:lo + cf] = (dact * dau_ref[:, lo:lo + cf].astype(F32)).astype(BF16)
            dgate_ref[:, lo:lo + cf] = (dact * dag_ref[:, lo:lo + cf].astype(F32)).astype(BF16)
        dh = _dot(dgate_ref[...], wv[0], NN) + _dot(dup_ref[...], wv[1], NN)
        xhat, rstd = _rms_stats(x_ref[...])
        dx, dg = _rms_bwd(xhat, rstd, g_ref[...], dh)
        dx_ref[...] = do_ref[...] + dx
        dg_ref[...] += dg

    row = pl.BlockSpec((tm, D), lambda i: (i, 0))
    hid = pl.BlockSpec((tm, F), lambda i: (i, 0))
    vec = pl.BlockSpec((1, D), lambda i: (0, 0))
    return _call(
        body, name=name, grid=(T // tm,),
        in_specs=[row, row, vec, hid, hid, HBM_SPEC],
        out_specs=[row, hid, hid, vec],
        out_shape=[jax.ShapeDtypeStruct((T, D), F32), jax.ShapeDtypeStruct((T, F), BF16),
                   jax.ShapeDtypeStruct((T, F), BF16), jax.ShapeDtypeStruct((1, D), F32)],
        scratch_shapes=[pltpu.VMEM((3, F, D), BF16), pltpu.SemaphoreType.DMA((3,))],
        vmem_mib=60, args=(dout, x, g, dau, dag, w3), rider=rider)


def _wgrad(lhs, rhs, tm, tf, name, rider=None):
    T, F = lhs[0].shape
    D = rhs.shape[1]
    K = len(lhs)

    def body(*refs):
        lhs_refs, rhs_ref, dw_ref, accs = refs[:K], refs[K], refs[K + 1], refs[K + 2:]
        i = pl.program_id(1)

        @pl.when(i == 0)
        def _():
            for acc in accs:
                acc[...] = jnp.zeros_like(acc)

        rv = rhs_ref[...]
        for acc, lhs_ref in zip(accs, lhs_refs):
            acc[...] += _dot(lhs_ref[...], rv, TN)

        @pl.when(i == pl.num_programs(1) - 1)
        def _():
            for k, acc in enumerate(accs):
                dw_ref[k] = acc[...].astype(BF16)

    hid = pl.BlockSpec((tm, tf), lambda f, i: (i, f))
    return _call(
        body, name=name, grid=(F // tf, T // tm),
        in_specs=[hid] * K + [pl.BlockSpec((tm, D), lambda f, i: (i, 0))],
        out_specs=[pl.BlockSpec((K, tf, D), lambda f, i: (0, f, 0))],
        out_shape=[jax.ShapeDtypeStruct((K, F, D), BF16)],
        scratch_shapes=[pltpu.VMEM((tf, D), F32)] * K,
        vmem_mib=56, args=(*lhs, rhs), rider=rider)


def _lru_gates(xr, bda_ref, bdx_ref, vec_ref):
    xrb = xr.astype(BF16)
    r = _sigmoid(_dot(xrb, bda_ref[...], NN) + vec_ref[V_BA:V_BA + 1, :])
    ig = _sigmoid(_dot(xrb, bdx_ref[...], NN) + vec_ref[V_BX:V_BX + 1, :])
    sp = _softplus_neg(vec_ref[V_LAM:V_LAM + 1, :])
    log_a = (-LRU_C * sp) * r
    a = jnp.exp(log_a)
    mult = jnp.sqrt(_neg_expm1(2.0 * log_a))
    return xrb, r, ig, sp, a, mult


def _layernorm_stats(u1):
    xc = u1 - jnp.mean(u1, axis=-1, keepdims=True)
    rs = lax.rsqrt(jnp.mean(xc * xc, axis=-1, keepdims=True) + LN_EPS)
    return xc * rs, rs


def _mix_core_fwd(x1, g, w_in_t, w_out, bda, bdx, cw, lw, vec, tm, name, rider=None):
    T, D = x1.shape
    W = cw.shape[1]
    assert tm >= CONV_HALO and w_in_t.shape[0] == 4 * W

    def body(x1_ref, g_ref, wi_ref, wo_ref, bda_ref, bdx_ref, cw_ref, lw_ref, vec_ref,
             x2_ref, z_ref, mix_ref, u1_ref, xr_ref, hst_ref, ubuf, rbuf, hc):
        @pl.when(pl.program_id(0) == 0)
        def _():
            ubuf[0:CONV_HALO, :] = jnp.zeros((CONV_HALO, W), F32)
            rbuf[0:LRU_HALO, :] = jnp.zeros((LRU_HALO, W), F32)
            hc[...] = jnp.zeros_like(hc)

        xhat, _ = _rms_stats(x1_ref[...])
        z_ref[...] = _dot((xhat * g_ref[...]).astype(BF16), wi_ref[...], NT)

        ubuf[CONV_HALO:CONV_HALO + tm, :] = z_ref[:, 0:W] * _sigmoid(z_ref[:, W:2 * W])
        u1 = jnp.zeros((tm, W), F32) + vec_ref[V_CB:V_CB + 1, :]
        base = CONV_HALO - (CONV_K - 1)
        for off, win in _row_windows(ubuf, tm, range(base, base + CONV_K)):
            u1 = u1 + cw_ref[off - base:off - base + 1, :] * win
        ubuf[0:CONV_HALO, :] = ubuf[tm:tm + CONV_HALO, :]
        u1_ref[...] = u1
        xh, _ = _layernorm_stats(u1)
        u2 = xh * vec_ref[V_LNG:V_LNG + 1, :] + vec_ref[V_LNB:V_LNB + 1, :]
        ub = (u2 * _sigmoid(u2)).astype(BF16)
        mix_ref[:, 0:W] = ub

        rbuf[LRU_HALO:LRU_HALO + tm, :] = z_ref[:, 2 * W:3 * W]
        xr = jnp.zeros((tm, W), F32) + vec_ref[V_LCB:V_LCB + 1, :]
        for k in range(LRU_K):
            off = LRU_HALO - (LRU_K - 1) + k
            xr = xr + lw_ref[k:k + 1, :] * rbuf[off:off + tm, :]
        rbuf[0:LRU_HALO, :] = rbuf[tm:tm + LRU_HALO, :]
        xr_ref[...] = xr
        _, _, ig, _, a, mult = _lru_gates(xr, bda_ref, bdx_ref, vec_ref)
        hc[0:1, :] = _scan_rows(a, mult * (ig * xr), hc[0:1, :], hst_ref)
        gl, _ = _gelu_parts(z_ref[:, 3 * W:4 * W])
        yb = (hst_ref[...] * gl).astype(BF16)
        mix_ref[:, W:2 * W] = yb

        x2_ref[...] = x1_ref[...] + _dot(ub, wo_ref[0:W, :], NN) + _dot(yb, wo_ref[W:2 * W, :], NN)

    full = lambda a: pl.BlockSpec(a.shape, lambda i: (0,) * a.ndim)
    tile = lambda n: pl.BlockSpec((tm, n), lambda i: (i, 0))
    return _call(
        body, name=name, grid=(T // tm,),
        in_specs=[tile(D), full(g), full(w_in_t), full(w_out), full(bda), full(bdx), full(cw), full(lw), full(vec)],
        out_specs=[tile(D), tile(4 * W), tile(2 * W), tile(W), tile(W), tile(W)],
        out_shape=[jax.ShapeDtypeStruct((T, D), F32), jax.ShapeDtypeStruct((T, 4 * W), F32),
                   jax.ShapeDtypeStruct((T, 2 * W), BF16), jax.ShapeDtypeStruct((T, W), F32),
                   jax.ShapeDtypeStruct((T, W), F32), jax.ShapeDtypeStruct((T, W), F32)],
        scratch_shapes=[pltpu.VMEM((tm + CONV_HALO, W), F32), pltpu.VMEM((tm + LRU_HALO, W), F32),
                        pltpu.VMEM((8, W), F32)],
        vmem_mib=56, args=(x1, g, w_in_t, w_out, bda, bdx, cw, lw, vec), rider=rider)


def _mix_bwd(dx2, z, u1, xr, hst, x1, mix, g, w_in_t, w_out, bda, bdx, cw, lw, vec, tm, name, rider=None):
    T, D = dx2.shape
    W = cw.shape[1]
    nt = T // tm
    assert tm >= CONV_HALO and tm % CONV_HALO == 0

    def body(dx_ref, z_ref, zh_ref, u1_ref, xr_ref, h_ref, hh_ref, wo_ref, bda_ref, bdx_ref, cw_ref, lw_ref, vec_ref,
             x1_ref, mix_ref, g_ref, wi_ref,
             dx1_ref, sg_ref, dbda_ref, dbdx_ref, dob_ref, dg_ref, dwi_ref, dwo_ref,
             u0buf, du1buf, rxbuf, dxrbuf, gbuf, gc, spacc, dz_ref, ai_ref, ao_ref):
        i = pl.program_id(0)
        first = i == nt - 1
        row = lax.broadcasted_iota(jnp.int32, (tm, W), 0)

        @pl.when(i == 0)
        def _():
            sg_ref[...] = jnp.zeros_like(sg_ref)
            dbda_ref[...] = jnp.zeros_like(dbda_ref)
            dbdx_ref[...] = jnp.zeros_like(dbdx_ref)
            du1buf[tm:tm + CONV_HALO, :] = jnp.zeros((CONV_HALO, W), F32)
            dxrbuf[tm:tm + LRU_HALO, :] = jnp.zeros((LRU_HALO, W), F32)
            gc[...] = jnp.zeros_like(gc)
            spacc[...] = jnp.zeros_like(spacc)
            dg_ref[...] = jnp.zeros_like(dg_ref)
            ai_ref[...] = jnp.zeros_like(ai_ref)
            ao_ref[...] = jnp.zeros_like(ao_ref)

        def accum(r, val):
            sg_ref[r:r + 1, :] += jnp.sum(val, axis=0, keepdims=True)

        x1hat, x1rstd = _rms_stats(x1_ref[...])
        gain = g_ref[...]
        hb = (x1hat * gain).astype(BF16)

        def in_proj_bwd(lo, hi):
            dzb = dz_ref[:, lo:hi]
            ai_ref[lo:hi, :] += _dot(dzb, hb, TN)
            return _dot(dzb, wi_ref[lo:hi, :], NN)

        dxb = dx_ref[...].astype(BF16)
        ao_ref[...] += _dot(mix_ref[...], dxb, TN)
        dmix = _dot(dxb, wo_ref[...], NT)
        d_u = dmix[:, 0:W]
        d_yr = dmix[:, W:2 * W]

        xh, rs = _layernorm_stats(u1_ref[...])
        ln_g = vec_ref[V_LNG:V_LNG + 1, :]
        u2 = xh * ln_g + vec_ref[V_LNB:V_LNB + 1, :]
        s2 = _sigmoid(u2)
        d_u2 = d_u * (s2 * (1.0 + u2 * (1.0 - s2)))
        accum(G_LNG, d_u2 * xh)
        accum(G_LNB, d_u2)
        d_xh = d_u2 * ln_g
        d_u1 = rs * (d_xh - jnp.mean(d_xh, axis=-1, keepdims=True)
                     - xh * jnp.mean(d_xh * xh, axis=-1, keepdims=True))
        accum(G_CB, d_u1)
        halo_on = jnp.where(first, 0.0, 1.0)
        u0buf[0:CONV_HALO, :] = halo_on * (zh_ref[:, 0:W] * _sigmoid(zh_ref[:, W:2 * W]))
        cv = z_ref[:, 0:W]
        sgc = _sigmoid(z_ref[:, W:2 * W])
        u0buf[CONV_HALO:CONV_HALO + tm, :] = cv * sgc
        du1buf[0:tm, :] = d_u1
        base = CONV_HALO - (CONV_K - 1)
        for off, win in _row_windows(u0buf, tm, range(base, base + CONV_K)):
            accum(G_CW + off - base, d_u1 * win)
        d_u0 = jnp.zeros((tm, W), F32)
        for off, win in _row_windows(du1buf, tm, range(0, CONV_K)):
            d_u0 = d_u0 + cw_ref[CONV_K - 1 - off:CONV_K - off, :] * win
        du1buf[tm:tm + CONV_HALO, :] = du1buf[0:CONV_HALO, :]
        dz_ref[:, 0:W] = (d_u0 * sgc).astype(BF16)
        dz_ref[:, W:2 * W] = (d_u0 * cv * (sgc * (1.0 - sgc))).astype(BF16)
        dh = in_proj_bwd(0, 2 * W)

        xrv = xr_ref[...]
        xrb, r, ig, sp, a, mult = _lru_gates(xrv, bda_ref, bdx_ref, vec_ref)
        h = h_ref[...]
        gl, dgl = _gelu_parts(z_ref[:, 3 * W:4 * W])
        dz_ref[:, 3 * W:4 * W] = (d_yr * h * dgl).astype(BF16)
        dh = dh + in_proj_bwd(3 * W, 4 * W)
        a_next = jnp.where(row == tm - 1, 1.0, pltpu.roll(a, tm - 1, 0))
        g_first = _scan_rows(a_next, d_yr * gl, gc[0:1, :], gbuf, reverse=True)
        g = gbuf[...]
        gc[0:1, :] = a[0:1, :] * g_first
        hprev = jnp.where(row == 0, halo_on * hh_ref[LRU_HALO - 1:LRU_HALO, :], pltpu.roll(h, 1, 0))
        d_log_a = (g * hprev) * a - (g * ig * xrv) * (a * a) / mult
        d_ig = g * mult * xrv
        d_xr = g * mult * ig
        spacc[0:1, :] += jnp.sum(d_log_a * r, axis=0, keepdims=True)
        d_pa32 = (d_log_a * (-LRU_C * sp)) * (r * (1.0 - r))
        d_px32 = d_ig * (ig * (1.0 - ig))
        accum(G_BA, d_pa32)
        accum(G_BX, d_px32)
        d_pa = d_pa32.astype(BF16)
        d_px = d_px32.astype(BF16)
        d_xr = d_xr + _dot(d_pa, bda_ref[...], NT) + _dot(d_px, bdx_ref[...], NT)
        dbda_ref[...] += _dot(xrb, d_pa, TN)
        dbdx_ref[...] += _dot(xrb, d_px, TN)
        accum(G_LCB, d_xr)
        rxbuf[0:LRU_HALO, :] = halo_on * zh_ref[CONV_HALO - LRU_HALO:CONV_HALO, 2 * W:3 * W]
        rxbuf[LRU_HALO:LRU_HALO + tm, :] = z_ref[:, 2 * W:3 * W]
        dxrbuf[0:tm, :] = d_xr
        d_rx = jnp.zeros((tm, W), F32)
        for k in range(LRU_K):
            off = LRU_HALO - (LRU_K - 1) + k
            accum(G_LW + k, d_xr * rxbuf[off:off + tm, :])
            d_rx = d_rx + lw_ref[k:k + 1, :] * dxrbuf[LRU_K - 1 - k:LRU_K - 1 - k + tm, :]
        dxrbuf[tm:tm + LRU_HALO, :] = dxrbuf[0:LRU_HALO, :]
        dz_ref[:, 2 * W:3 * W] = d_rx.astype(BF16)
        dh = dh + in_proj_bwd(2 * W, 3 * W)

        dx, dg = _rms_bwd(x1hat, x1rstd, gain, dh)
        dx1 = dx_ref[...] + dx
        dx1_ref[...] = dx1
        dob_ref[...] = (FFN_RES * dx1).astype(BF16)
        dg_ref[...] += dg

        @pl.when(first)
        def _():
            lam = vec_ref[V_LAM:V_LAM + 1, :]
            sg_ref[G_LAM:G_LAM + 1, :] = LRU_C * _sigmoid(-lam) * spacc[0:1, :]
            dwi_ref[...] = ai_ref[...].astype(BF16)
            dwo_ref[...] = ao_ref[...].astype(BF16)

    full = lambda a: pl.BlockSpec(a.shape, lambda i: (0,) * a.ndim)
    tile = lambda n: pl.BlockSpec((tm, n), lambda i: (nt - 1 - i, 0))
    halo = lambda rows, n: pl.BlockSpec(
        (rows, n), lambda i: (jnp.maximum((nt - 1 - i) * (tm // rows) - 1, 0), 0))
    const = lambda r, c: pl.BlockSpec((r, c), lambda i: (0, 0))
    return _call(
        body, name=name, grid=(nt,),
        in_specs=[tile(D), tile(4 * W), halo(CONV_HALO, 4 * W), tile(W), tile(W), tile(W), halo(LRU_HALO, W),
                  full(w_out), full(bda), full(bdx), full(cw), full(lw), full(vec),
                  tile(D), tile(2 * W), full(g), full(w_in_t)],
        out_specs=[tile(D), const(G_ROWS, W), const(W, W), const(W, W),
                   tile(D), const(1, D), const(4 * W, D), const(2 * W, D)],
        out_shape=[jax.ShapeDtypeStruct((T, D), F32), jax.ShapeDtypeStruct((G_ROWS, W), F32),
                   jax.ShapeDtypeStruct((W, W), F32), jax.ShapeDtypeStruct((W, W), F32),
                   jax.ShapeDtypeStruct((T, D), BF16), jax.ShapeDtypeStruct((1, D), F32),
                   jax.ShapeDtypeStruct((4 * W, D), BF16), jax.ShapeDtypeStruct((2 * W, D), BF16)],
        scratch_shapes=[pltpu.VMEM((tm + CONV_HALO, W), F32), pltpu.VMEM((tm + CONV_HALO, W), F32),
                        pltpu.VMEM((tm + LRU_HALO, W), F32), pltpu.VMEM((tm + LRU_HALO, W), F32),
                        pltpu.VMEM((tm, W), F32), pltpu.VMEM((8, W), F32), pltpu.VMEM((8, W), F32),
                        pltpu.VMEM((tm, 4 * W), BF16), pltpu.VMEM((4 * W, D), F32), pltpu.VMEM((2 * W, D), F32)],
        vmem_mib=60, args=(dx2, z, z, u1, xr, hst, hst, w_out, bda, bdx, cw, lw, vec, x1, mix, g, w_in_t),
        rider=rider)


def _pair_add(full, recv, name):
    K, _, _, rows, D = full.shape

    def body(c_ref, a_ref, b_ref, o_ref):
        o_ref[...] = (a_ref[...].astype(F32) + b_ref[...].astype(F32)).astype(BF16)

    c = lax.axis_index("c").astype(jnp.int32).reshape((1,))
    return _call(
        body, name=name, grid=(K, N_CHIP), num_scalar_prefetch=1,
        in_specs=[pl.BlockSpec((None, None, None, rows, D), lambda k, q, c_ref: (k, q, c_ref[0], 0, 0)),
                  pl.BlockSpec((None, None, rows, D), lambda k, q, c_ref: (k, q, 0, 0))],
        out_specs=pl.BlockSpec((None, None, rows, D), lambda k, q, c_ref: (k, q, 0, 0)),
        out_shape=jax.ShapeDtypeStruct(recv.shape, BF16),
        scratch_shapes=[], vmem_mib=16, args=(c, full, recv))


def _adamw_update(wv, gv, mv, vv):
    m2 = ADAM_B1 * mv + (1.0 - ADAM_B1) * gv
    v2 = ADAM_B2 * vv + (1.0 - ADAM_B2) * (gv * gv)
    m_hat = m2 / (1.0 - ADAM_B1 ** ADAM_STEP)
    v_hat = v2 / (1.0 - ADAM_B2 ** ADAM_STEP)
    return -ADAM_LR * (m_hat / (jnp.sqrt(v_hat) + ADAM_EPS) + ADAM_WD * wv), m2, v2


def _finish(parts, k, w, m, v, transpose, name):
    _, n_parts, rows, D = parts.shape

    def body(p_ref, w_ref, m_ref, v_ref, g_ref, d_ref, mo_ref, vo_ref):
        acc = p_ref[0].astype(F32)
        for q in range(1, n_parts):
            acc = acc + p_ref[q].astype(F32)
        gv = acc.T if transpose else acc
        g_ref[...] = gv
        d_ref[...], mo_ref[...], vo_ref[...] = _adamw_update(w_ref[...], gv, m_ref[...], v_ref[...])

    whole = pl.BlockSpec(w.shape, lambda i: (0, 0))
    return _call(
        body, name=name, grid=(1,),
        in_specs=[pl.BlockSpec((None, n_parts, rows, D), lambda i: (k, 0, 0, 0)), whole, whole, whole],
        out_specs=[whole] * 4, out_shape=[pltpu.HBM(w.shape, F32)] * 4,
        scratch_shapes=[], vmem_mib=40, args=(parts, w, m, v))


def _finish_many(items, name, rider=None):
    n_items = len(items)
    D = items[0][2].shape[1]
    max_rows = max(w.shape[0] for _, _, w, _, _ in items)
    max_parts = max(parts.shape[1] for parts, _, _, _, _ in items)
    n_loads, n_stores = 4, 4

    def body(*refs):
        ins, outs = refs[:4 * n_items], refs[4 * n_items:8 * n_items]
        pbuf, wbuf, obuf, sem = refs[8 * n_items:]
        step = pl.program_id(0)

        def loads(j):
            parts, k, w, _, _ = items[j]
            rows, n_parts, s = w.shape[0], parts.shape[1], j % 2
            p_ref, w_ref, m_ref, v_ref = ins[4 * j:4 * j + 4]
            cps = [pltpu.make_async_copy(p_ref.at[k], pbuf.at[s, 0:n_parts, 0:rows], sem.at[s, 0])]
            return cps + [pltpu.make_async_copy(src, wbuf.at[s, r, 0:rows], sem.at[s, 1 + r])
                          for r, src in enumerate((w_ref, m_ref, v_ref))]

        def stores(j):
            rows, s = items[j][2].shape[0], j % 2
            return [pltpu.make_async_copy(obuf.at[s, r, 0:rows], outs[4 * j + r], sem.at[s, n_loads + r])
                    for r in range(n_stores)]

        for j, (parts, k, w, _, _) in enumerate(items):
            rows, n_parts, s = w.shape[0], parts.shape[1], j % 2

            @pl.when(step == j)
            def _():
                if j == 0:
                    for cp in loads(0):
                        cp.start()
                if j + 1 < n_items:
                    for cp in loads(j + 1):
                        cp.start()
                for cp in loads(j):
                    cp.wait()
                if j >= 2:
                    for cp in stores(j - 2):
                        cp.wait()
                acc = pbuf[s, 0, 0:rows, :].astype(F32)
                for q in range(1, n_parts):
                    acc = acc + pbuf[s, q, 0:rows, :].astype(F32)
                obuf[s, 0, 0:rows, :] = acc
                obuf[s, 1, 0:rows, :], obuf[s, 2, 0:rows, :], obuf[s, 3, 0:rows, :] = _adamw_update(
                    wbuf[s, 0, 0:rows, :], acc, wbuf[s, 1, 0:rows, :], wbuf[s, 2, 0:rows, :])
                for cp in stores(j):
                    cp.start()
                if j == n_items - 1:
                    for i in range(max(j - 1, 0), n_items):
                        for cp in stores(i):
                            cp.wait()

    args = [a for parts, _, w, m, v in items for a in (parts, w, m, v)]
    shapes = [jax.ShapeDtypeStruct(w.shape, F32) for _, _, w, _, _ in items for _ in range(4)]
    return _call(
        body, name=name, grid=(n_items,),
        in_specs=[HBM_SPEC] * len(args), out_specs=[HBM_SPEC] * len(shapes), out_shape=shapes,
        scratch_shapes=[pltpu.VMEM((2, max_parts, max_rows, D), BF16), pltpu.VMEM((2, 3, max_rows, D), F32),
                        pltpu.VMEM((2, 4, max_rows, D), F32), pltpu.SemaphoreType.DMA((2, n_loads + n_stores))],
        vmem_mib=56, args=args, rider=rider)


def _adamw_each(ws, gs, ms, vs, name):
    n = len(ws)

    def body(*refs):
        w_refs, g_refs, m_refs, v_refs, outs = refs[:n], refs[n:2 * n], refs[2 * n:3 * n], refs[3 * n:4 * n], refs[4 * n:]
        for k in range(n):
            outs[k][...], outs[n + k][...], outs[2 * n + k][...] = _adamw_update(
                w_refs[k][...], g_refs[k][...], m_refs[k][...], v_refs[k][...])

    shapes = [jax.ShapeDtypeStruct(w.shape, F32) for w in ws]
    return pl.pallas_call(
        body, name=name,
        in_specs=[VMEM_SPEC] * (4 * n), out_specs=[VMEM_SPEC] * (3 * n), out_shape=shapes * 3,
        compiler_params=pltpu.CompilerParams(vmem_limit_bytes=32 * MIB),
    )(*ws, *gs, *ms, *vs)


def _block_diag(w):
    h, d, _ = w.shape
    onto = jnp.eye(h, dtype=w.dtype)
    return (w[:, :, None, :] * onto[:, None, :, None]).reshape(h * d, h * d)


def _diag_blocks(m, h):
    d = m.shape[0] // h
    onto = jnp.eye(h, dtype=m.dtype)
    return (m.reshape(h, d, h, d) * onto[:, None, :, None]).sum(axis=2)


def _reduce_level1(full, tag):
    got = _run_comm(_sibling_comm(full), "rs_sibling_" + tag)
    return [_pair_add(a, b, "rs_pair_add_%s%d" % (tag, n)) for n, (a, b) in enumerate(zip(full, got))]


def kernel(x, ffn1_norm, ffn1_w_gate, ffn1_w_up, ffn1_w_down, mix_norm, w_in, conv_dw, conv_dw_bias, conv_ln_g, conv_ln_b, lru_conv_w, lru_conv_b, lru_w_a, lru_b_a, lru_w_x, lru_b_x, lru_lambda, w_out, ffn2_norm, ffn2_w_gate, ffn2_w_up, ffn2_w_down, final_norm, loss_target, m_ffn1_norm, m_ffn1_w_gate, m_ffn1_w_up, m_ffn1_w_down, m_mix_norm, m_w_in, m_conv_dw, m_conv_dw_bias, m_conv_ln_g, m_conv_ln_b, m_lru_conv_w, m_lru_conv_b, m_lru_w_a, m_lru_b_a, m_lru_w_x, m_lru_b_x, m_lru_lambda, m_w_out, m_ffn2_norm, m_ffn2_w_gate, m_ffn2_w_up, m_ffn2_w_down, m_final_norm, v_ffn1_norm, v_ffn1_w_gate, v_ffn1_w_up, v_ffn1_w_down, v_mix_norm, v_w_in, v_conv_dw, v_conv_dw_bias, v_conv_ln_g, v_conv_ln_b, v_lru_conv_w, v_lru_conv_b, v_lru_w_a, v_lru_b_a, v_lru_w_x, v_lru_b_x, v_lru_lambda, v_w_out, v_ffn2_norm, v_ffn2_w_gate, v_ffn2_w_up, v_ffn2_w_down, v_final_norm):
    T, D = x.shape[1], x.shape[2]
    F = ffn1_w_down.shape[0] * N_DEV
    rf = ffn1_w_down.shape[0]
    ri = w_in.shape[1]
    ro = w_out.shape[0]
    W = conv_dw_bias.shape[0]
    wc = conv_dw.shape[1]
    H = lru_w_a.shape[0]
    xs = x.reshape(T, D)
    tgt = loss_target.reshape(T, D)
    tm_ffn = min(512, T)
    tm_fwd = min(512, T)
    cf = 256
    tm_w = min(1024, T)
    tm_w1 = min(2048, T)
    tm_mix = min(256, T)
    tf_w = F // 2
    row = lambda v: v.reshape(1, -1)
    by_owner = lambda a, rows: a.reshape(a.shape[0], N_CHIP, 2, rows, D)

    p3a, p3b, p_in, p_out = _prep_weights(
        (ffn1_w_gate.T, ffn1_w_up.T, ffn1_w_down), (ffn2_w_gate.T, ffn2_w_up.T, ffn2_w_down), w_in, w_out,
        "prep_weights")
    tile_rows = lambda a: jnp.pad(a, ((0, -a.shape[0] % SUBLANES), (0, 0)))
    p_cw = jnp.concatenate([tile_rows(conv_dw), tile_rows(lru_conv_w)], axis=0)
    lw_row = p_cw.shape[0] - SUBLANES
    stacked = lambda r, j: r.at[:, j]
    plain = lambda r, j: r.at[j]
    g3_shape = jax.ShapeDtypeStruct((3, N_DEV, rf, D), BF16)
    g3a = _all_gather(p3a, "ag_ffn1")
    w3a = g3a.reshape(3, F, D)
    bda = _block_diag(lru_w_a).astype(BF16)
    bdx = _block_diag(lru_w_x).astype(BF16)
    vec = jnp.concatenate([tile_rows(v[None]) for v in
                           (conv_dw_bias, conv_ln_g, conv_ln_b, lru_conv_b, lru_b_a, lru_b_x, lru_lambda)], axis=0)

    gather_rest = _gather_comm(
        [p3b, p_in, p_out, p_cw], [stacked, plain, plain, plain],
        [g3_shape, jax.ShapeDtypeStruct((N_DEV, ri, D), BF16), jax.ShapeDtypeStruct((N_DEV, ro, D), BF16),
         jax.ShapeDtypeStruct((N_DEV,) + p_cw.shape, F32)],
        [(SIBLING,) + SAME_CORE, EVERYONE, EVERYONE, EVERYONE])
    (x1, h1, dau1, dag1, act1), (g3b_half, g_in, g_out, g_cw) = _ffn_fwd(
        xs, row(ffn1_norm), w3a, tm_fwd, cf, "ffn1_fwd", rider=gather_rest)
    w_in_t = g_in.reshape(N_DEV * ri, D)
    w_out_f = g_out.reshape(N_DEV * ro, D)
    cw_all = jnp.transpose(g_cw, (1, 0, 2)).reshape(p_cw.shape[0], N_DEV * wc)
    cw = cw_all[0:CONV_K]
    lw = cw_all[lw_row:lw_row + LRU_K]
    (x2, z, mix, u1, xr, hst), (g3b,) = _mix_core_fwd(
        x1, row(mix_norm), w_in_t, w_out_f, bda, bdx, cw, lw, vec, min(512, T), "mix_core_fwd",
        rider=_forward_comm([g3b_half], [stacked]))
    w3b = g3b.reshape(3, F, D)
    dx3, dob2, d_final_norm, loss_part, h3, dau2, dag2, act2 = _ffn_fwd(
        x2, row(ffn2_norm), w3b, tm_fwd, cf, "ffn2_fwd_loss", head=(row(final_norm), tgt))

    dx2, dgate2, dup2, d_ffn2_norm = _ffn_dgrad(dx3, x2, row(ffn2_norm), dau2, dag2, w3b, tm_ffn, cf, "ffn2_dgrad")
    (dw_gu2,) = _wgrad([dgate2, dup2], h3, tm_w, tf_w, "ffn2_wgrad_gu")
    (dw_d2,) = _wgrad([act2], dob2, tm_w1, tf_w, "ffn2_wgrad_d")
    by_device = lambda a: a.reshape(a.shape[0], N_DEV, rf, D)
    (dx1, sg, dbda, dbdx, dob1, d_mix_norm, dw_in_t, dw_out), parts_f2 = _mix_bwd(
        dx2, z, u1, xr, hst, x1, mix, row(mix_norm), w_in_t, w_out_f, bda, bdx, cw, lw, vec, tm_mix, "mix_bwd",
        rider=_chips_comm([by_device(dw_gu2), by_device(dw_d2)], every_device=True))
    io = [dw_in_t.reshape(1, N_DEV, ri, D), dw_out.reshape(1, N_DEV, ro, D)]
    (dw_d1,), parts_io = _wgrad(
        [act1], dob1, tm_w1, tf_w, "ffn1_wgrad_d", rider=_chips_comm(io, every_device=True))
    sums_d1 = _reduce_level1([by_owner(dw_d1, rf)], "d1")
    dx0, dgate1, dup1, d_ffn1_norm = _ffn_dgrad(dx1, xs, row(ffn1_norm), dau1, dag1, w3a, tm_ffn, cf, "ffn1_dgrad")
    small = [d_ffn1_norm, d_mix_norm, d_ffn2_norm, d_final_norm, sg, loss_part]
    (dw_g1,), summed_and_parts = _wgrad(
        [dgate1], h1, tm_w1, tf_w, "ffn1_wgrad_g", rider=_both(_small_sum_comm(small), _chips_comm(sums_d1)))
    summed, parts_d1 = summed_and_parts[:len(small)], summed_and_parts[len(small):]
    sums_g1 = _reduce_level1([by_owner(dw_g1, rf)], "g1")
    gates = [_diag_blocks(dbda, H).reshape(-1, D), _diag_blocks(dbdx, H).reshape(-1, D)]
    (dw_u1,), (g_w_a, g_w_x, *parts_g1) = _wgrad(
        [dup1], h1, tm_w1, tf_w, "ffn1_wgrad_u", rider=_both(_small_sum_comm(gates), _chips_comm(sums_g1)))
    sums_u1 = _reduce_level1([by_owner(dw_u1, rf)], "u1")

    g_norm1, g_norm_mix, g_norm2, g_norm_final, g_sg, g_loss = summed
    loss = g_loss[0, 0]
    me = 4 * lax.axis_index("x") + 2 * lax.axis_index("y") + lax.axis_index("c")
    chan = lambda full_g: lax.dynamic_slice_in_dim(full_g, me * wc, wc, axis=1)
    grads = {
        "ffn1_norm": g_norm1.reshape(D), "mix_norm": g_norm_mix.reshape(D), "ffn2_norm": g_norm2.reshape(D),
        "final_norm": g_norm_final.reshape(D),
        "conv_dw_bias": g_sg[G_CB], "conv_ln_g": g_sg[G_LNG], "conv_ln_b": g_sg[G_LNB],
        "lru_conv_b": g_sg[G_LCB], "lru_b_a": g_sg[G_BA], "lru_b_x": g_sg[G_BX], "lru_lambda": g_sg[G_LAM],
        "lru_w_a": g_w_a.reshape(lru_w_a.shape), "lru_w_x": g_w_x.reshape(lru_w_x.shape),
        "conv_dw": chan(g_sg[G_CW:G_CW + CONV_K]), "lru_conv_w": chan(g_sg[G_LW:G_LW + LRU_K]),
    }

    weights = dict(ffn1_norm=ffn1_norm, ffn1_w_gate=ffn1_w_gate, ffn1_w_up=ffn1_w_up, ffn1_w_down=ffn1_w_down, mix_norm=mix_norm, w_in=w_in, conv_dw=conv_dw, conv_dw_bias=conv_dw_bias, conv_ln_g=conv_ln_g, conv_ln_b=conv_ln_b, lru_conv_w=lru_conv_w, lru_conv_b=lru_conv_b, lru_w_a=lru_w_a, lru_b_a=lru_b_a, lru_w_x=lru_w_x, lru_b_x=lru_b_x, lru_lambda=lru_lambda, w_out=w_out, ffn2_norm=ffn2_norm, ffn2_w_gate=ffn2_w_gate, ffn2_w_up=ffn2_w_up, ffn2_w_down=ffn2_w_down, final_norm=final_norm)
    moment1 = dict(ffn1_norm=m_ffn1_norm, ffn1_w_gate=m_ffn1_w_gate, ffn1_w_up=m_ffn1_w_up, ffn1_w_down=m_ffn1_w_down, mix_norm=m_mix_norm, w_in=m_w_in, conv_dw=m_conv_dw, conv_dw_bias=m_conv_dw_bias, conv_ln_g=m_conv_ln_g, conv_ln_b=m_conv_ln_b, lru_conv_w=m_lru_conv_w, lru_conv_b=m_lru_conv_b, lru_w_a=m_lru_w_a, lru_b_a=m_lru_b_a, lru_w_x=m_lru_w_x, lru_b_x=m_lru_b_x, lru_lambda=m_lru_lambda, w_out=m_w_out, ffn2_norm=m_ffn2_norm, ffn2_w_gate=m_ffn2_w_gate, ffn2_w_up=m_ffn2_w_up, ffn2_w_down=m_ffn2_w_down, final_norm=m_final_norm)
    moment2 = dict(ffn1_norm=v_ffn1_norm, ffn1_w_gate=v_ffn1_w_gate, ffn1_w_up=v_ffn1_w_up, ffn1_w_down=v_ffn1_w_down, mix_norm=v_mix_norm, w_in=v_w_in, conv_dw=v_conv_dw, conv_dw_bias=v_conv_dw_bias, conv_ln_g=v_conv_ln_g, conv_ln_b=v_conv_ln_b, lru_conv_w=v_lru_conv_w, lru_conv_b=v_lru_conv_b, lru_w_a=v_lru_w_a, lru_b_a=v_lru_b_a, lru_w_x=v_lru_w_x, lru_b_x=v_lru_b_x, lru_lambda=v_lru_lambda, w_out=v_w_out, ffn2_norm=v_ffn2_norm, ffn2_w_gate=v_ffn2_w_gate, ffn2_w_up=v_ffn2_w_up, ffn2_w_down=v_ffn2_w_down, final_norm=v_final_norm)
    order = list(weights)
    gate_up = {"ffn1_w_gate", "ffn1_w_up", "ffn2_w_gate", "ffn2_w_up"}
    view = lambda n, a: a.T if n in gate_up else a
    operands = lambda n: [view(n, d[n]) for d in (weights, moment1, moment2)]
    delta, new_m, new_v = {}, {}, {}
    six = {"ffn1_w_gate": (parts_g1[0], 0), "ffn1_w_down": (parts_d1[0], 0), "w_out": (parts_io[1], 0),
           "ffn2_w_gate": (parts_f2[0], 0), "ffn2_w_up": (parts_f2[0], 1), "ffn2_w_down": (parts_f2[1], 0)}
    results, parts_u1 = _finish_many([(parts, k, *operands(n)) for n, (parts, k) in six.items()], "finish_six",
                                     rider=_chips_comm(sums_u1))
    for j, n in enumerate(six):
        grads[n], delta[n], new_m[n], new_v[n] = [view(n, r) for r in results[4 * j:4 * j + 4]]
    for n, parts, d_major in (("ffn1_w_up", parts_u1[0], False), ("w_in", parts_io[0], True)):
        grads[n], delta[n], new_m[n], new_v[n] = [
            view(n, r) for r in _finish(parts, 0, *operands(n), d_major, "finish_" + n)]
    big = set(six) | {"ffn1_w_up", "w_in"}
    rest = [n for n in order if n not in big]
    updates = _adamw_each([weights[n] for n in rest], [grads[n] for n in rest], [moment1[n] for n in rest],
                          [moment2[n] for n in rest], "adamw_small")
    for k, n in enumerate(rest):
        delta[n], new_m[n], new_v[n] = updates[k], updates[len(rest) + k], updates[2 * len(rest) + k]

    return (loss, dx0.reshape(x.shape), *[grads[n] for n in order], *[delta[n] for n in order],
            *[new_m[n] for n in order], *[new_v[n] for n in order])
```
